```python
import jax
import jax.numpy as jnp
from jax import lax
import numpy as np

D_MODEL = 1024
BATCH = 32
SEQ = 2048
DEPTH = 2

HEAD_DIM = 64
N_HEADS = 4
GROUP_W = N_HEADS * HEAD_DIM
N_GROUPS = 4
D_MIX = N_GROUPS * GROUP_W
ROPE_THETA = 10000.0
ATT_BLOCK = 128
NEG = -1e30
LN_EPS = 1e-5
A_KV_HEADS = 2
A_WINDOW = 128
B_CONV = 4
B_CHUNK = 128
B_FBIAS_LO = 3.0
B_FBIAS_HI = 6.0
C_CHUNK = 64
C_EPS = 1e-6
CMP_LEN = 32
CMP_STRIDE = 16
SLC_LEN = 64
N_SEL = 8
NSA_WINDOW = 512
SLC_QBLOCK = 32
FORCE_SCORE = 1e6
DN_ALPHA = (2.0 * DEPTH) ** 0.25
DN_BETA = (8.0 * DEPTH) ** -0.25

SPLITS = (
    ('a_q', GROUP_W), ('a_k', A_KV_HEADS * HEAD_DIM), ('a_v', A_KV_HEADS * HEAD_DIM), ('a_z', GROUP_W),
    ('b_qk', 2 * GROUP_W), ('b_v', GROUP_W), ('b_if', 2 * N_HEADS), ('b_o', GROUP_W), ('b_z', GROUP_W),
    ('c_q', GROUP_W), ('c_f', GROUP_W), ('c_i', GROUP_W), ('c_z', GROUP_W),
    ('d_q', GROUP_W), ('d_kv', 6 * HEAD_DIM), ('d_g', 3 * N_HEADS), ('d_z', GROUP_W),
)
N_COLS = sum(size for _, size in SPLITS)

kernel_name = 'hybrid_swa_mlstm_hgrn2_nsa_deepnorm'


def _col_offset(name):
    off = 0
    for n, size in SPLITS:
        if n == name:
            return off
        off += size
    raise KeyError(name)


def split_cols(h):
    cols = {}
    off = 0
    for name, size in SPLITS:
        cols[name] = h[..., off:off + size]
        off += size
    return cols


def layer_norm(x, g, b):
    xf = x.astype(jnp.float32)
    mu = jnp.mean(xf, axis=-1, keepdims=True)
    var = jnp.mean(jnp.square(xf - mu), axis=-1, keepdims=True)
    y = (xf - mu) * lax.rsqrt(var + LN_EPS) * g.astype(jnp.float32) + b.astype(jnp.float32)
    return y.astype(x.dtype)


def rope(x, pos):
    half = x.shape[-1] // 2
    inv = ROPE_THETA ** (-jnp.arange(half, dtype=jnp.float32) / half)
    ang = pos.astype(jnp.float32)[:, None] * inv[None, :]
    cos = jnp.cos(ang)[None, :, None, :]
    sin = jnp.sin(ang)[None, :, None, :]
    xf = x.astype(jnp.float32)
    x1, x2 = xf[..., :half], xf[..., half:]
    return jnp.concatenate([x1 * cos - x2 * sin, x1 * sin + x2 * cos], axis=-1).astype(x.dtype)


def causal_conv(x, w, b):
    K, C = w.shape
    y = lax.conv_general_dilated(x, w[:, None, :].astype(x.dtype), window_strides=(1,), padding=[(K - 1, 0)],
                                 dimension_numbers=('NWC', 'WIO', 'NWC'), feature_group_count=C)
    return y + b.astype(x.dtype)


def banded_attention(q, k, v, window, sinks=None):
    B, T, H, d = q.shape
    G = k.shape[2]
    R = H // G
    nb = T // ATT_BLOCK
    pad = window
    span = pad + ATT_BLOCK
    kp = jnp.pad(k, ((0, 0), (pad, 0), (0, 0), (0, 0)))
    vp = jnp.pad(v, ((0, 0), (pad, 0), (0, 0), (0, 0)))
    qb = q.reshape(B, nb, ATT_BLOCK, G, R, d).transpose(1, 0, 2, 3, 4, 5)
    qi = jnp.arange(ATT_BLOCK)[:, None]
    kj = jnp.arange(span)[None, :] - pad
    rel = qi - kj
    band = (rel >= 0) & (rel < window)
    scale = d ** -0.5

    def block(args):
        n, qblk = args
        start = n * ATT_BLOCK
        kb = lax.dynamic_slice_in_dim(kp, start, span, axis=1)
        vb = lax.dynamic_slice_in_dim(vp, start, span, axis=1)
        s = jnp.einsum('bqgrd,bkgd->bgrqk', qblk, kb).astype(jnp.float32) * scale
        s = jnp.where(band & (kj + start >= 0), s, NEG)
        if sinks is None:
            p = jax.nn.softmax(s, axis=-1)
        else:
            sink = jnp.broadcast_to(sinks.astype(jnp.float32).reshape(1, G, R, 1, 1), s.shape[:-1] + (1,))
            p = jax.nn.softmax(jnp.concatenate([s, sink], axis=-1), axis=-1)[..., :span]
        return jnp.einsum('bgrqk,bkgd->bqgrd', p.astype(v.dtype), vb)

    out = lax.map(block, (jnp.arange(nb), qb))
    return out.transpose(1, 0, 2, 3, 4, 5).reshape(B, T, H, d)


def mlstm_chunkwise(q, k, v, i_pre, f_pre):
    B, T, H, d = q.shape
    L = B_CHUNK
    nc = T // L
    f32 = jnp.float32
    to_chunks = lambda t: t.astype(f32).reshape(B, nc, L, H, -1).transpose(1, 0, 3, 2, 4)
    qc = to_chunks(q) * d ** -0.5
    kc = to_chunks(k)
    vc = to_chunks(v)
    ic = to_chunks(i_pre[..., None])[..., 0]
    fc = jax.nn.log_sigmoid(to_chunks(f_pre[..., None])[..., 0])
    causal = jnp.tril(jnp.ones((L, L), bool))

    def step(carry, inp):
        C, n, m = carry
        q_, k_, v_, i_, lf = inp
        b = jnp.cumsum(lf, axis=-1)
        D = jnp.where(causal, b[..., :, None] - b[..., None, :] + i_[..., None, :], -jnp.inf)
        inter = b + m[..., None]
        m_t = jnp.maximum(jnp.max(D, axis=-1), inter)
        w_inter = jnp.exp(inter - m_t)
        S = jnp.einsum('bhtd,bhsd->bhts', q_, k_) * jnp.exp(D - m_t[..., None])
        num = jnp.einsum('bhts,bhsv->bhtv', S, v_) + w_inter[..., None] * jnp.einsum('bhtk,bhvk->bhtv', q_, C)
        den = jnp.sum(S, axis=-1) + w_inter * jnp.einsum('bhtk,bhk->bht', q_, n)
        h = num / jnp.maximum(jnp.abs(den), jnp.exp(-m_t))[..., None]
        b_last = b[..., -1]
        d_s = b_last[..., None] - b + i_
        m_new = jnp.maximum(b_last + m, jnp.max(d_s, axis=-1))
        w_s = jnp.exp(d_s - m_new[..., None])
        decay = jnp.exp(b_last + m - m_new)
        C_new = decay[..., None, None] * C + jnp.einsum('bhs,bhsv,bhsk->bhvk', w_s, v_, k_)
        n_new = decay[..., None] * n + jnp.einsum('bhs,bhsk->bhk', w_s, k_)
        return (C_new, n_new, m_new), h

    init = (jnp.zeros((B, H, d, d), f32), jnp.zeros((B, H, d), f32), jnp.zeros((B, H), f32))
    _, hs = lax.scan(step, init, (qc, kc, vc, ic, fc))
    return hs.transpose(1, 0, 3, 2, 4).reshape(B, T, H, d).astype(q.dtype)


def hgrn2_chunkwise(q, log_f, k, v):
    B, T, H, dk = q.shape
    dv = v.shape[-1]
    L = C_CHUNK
    nc = T // L
    f32 = jnp.float32
    to_chunks = lambda t: t.astype(f32).reshape(B, nc, L, H, -1).transpose(1, 0, 3, 2, 4)
    causal = jnp.tril(jnp.ones((L, L), bool))[..., None]

    def step(S, inp):
        q_, g_, k_, v_ = inp
        a = jnp.cumsum(g_, axis=2)
        decay = jnp.exp(jnp.where(causal, a[:, :, :, None, :] - a[:, :, None, :, :], -jnp.inf))
        A = jnp.einsum('bhtd,bhtsd,bhsd->bhts', q_, decay, k_)
        o = jnp.einsum('bhts,bhsv->bhtv', A, v_) + jnp.einsum('bhtk,bhkv->bhtv', q_ * jnp.exp(a), S)
        a_last = a[:, :, -1:, :]
        S_new = jnp.exp(a_last[:, :, 0, :])[..., None] * S + jnp.einsum('bhsk,bhsv->bhkv', k_ * jnp.exp(a_last - a), v_)
        return S_new, o

    _, os_ = lax.scan(step, jnp.zeros((B, H, dk, dv), f32), (to_chunks(q), to_chunks(log_f), to_chunks(k), to_chunks(v)))
    return os_.transpose(1, 0, 3, 2, 4).reshape(B, T, H, dv)


def nsa_attention(q, k_cmp_in, v_cmp_in, k_slc, v_slc, k_win, v_win, gates, cmp_pe, cmp_w1, cmp_w2):
    B, T, H, d = q.shape
    pos = jnp.arange(T)
    scale = d ** -0.5
    n_cmp = (T - CMP_LEN) // CMP_STRIDE + 1
    starts = jnp.arange(n_cmp) * CMP_STRIDE
    ends = starts + CMP_LEN - 1
    idx = starts[:, None] + jnp.arange(CMP_LEN)[None, :]

    def compress(t, pe, w1, w2):
        blk = (t[:, idx] + pe).reshape(B, n_cmp, CMP_LEN * d)
        return jax.nn.silu(blk @ w1) @ w2

    k_cmp = rope(compress(k_cmp_in, cmp_pe[0], cmp_w1[0], cmp_w2[0])[:, :, None, :], ends)[:, :, 0, :]
    v_cmp = compress(v_cmp_in, cmp_pe[1], cmp_w1[1], cmp_w2[1])
    cmp_valid = ends[None, :] <= pos[:, None]
    s = jnp.einsum('bthd,bnd->bhtn', q, k_cmp).astype(jnp.float32) * scale
    p_cmp = jnp.where(cmp_valid, jax.nn.softmax(jnp.where(cmp_valid, s, NEG), axis=-1), 0.0)
    o_cmp = jnp.einsum('bhtn,bnd->bthd', p_cmp.astype(v_cmp.dtype), v_cmp)
    n_slc = T // SLC_LEN
    blk_id = jnp.arange(n_slc)
    overlap = ((starts[:, None] < (blk_id[None, :] + 1) * SLC_LEN)
               & (starts[:, None] + CMP_LEN > blk_id[None, :] * SLC_LEN)).astype(jnp.float32)
    imp = jnp.einsum('bhtn,nj->btj', p_cmp, overlap)
    cur = (pos // SLC_LEN)[:, None]
    forced = (blk_id[None, :] == 0) | (blk_id[None, :] == cur) | (blk_id[None, :] == cur - 1)
    imp = jnp.where(forced, FORCE_SCORE, imp)
    imp = jnp.where(blk_id[None, :] <= cur, imp, NEG)
    k_sel = min(N_SEL, n_slc)
    _, sel = lax.top_k(imp, k_sel)
    k_blocks = k_slc.reshape(B, n_slc, SLC_LEN, d)
    v_blocks = v_slc.reshape(B, n_slc, SLC_LEN, d)
    nq = T // SLC_QBLOCK
    q_ch = q.reshape(B, nq, SLC_QBLOCK, H, d).transpose(1, 0, 2, 3, 4)
    sel_ch = sel.reshape(B, nq, SLC_QBLOCK, k_sel).transpose(1, 0, 2, 3)
    offs = jnp.arange(SLC_LEN)

    def sel_block(args):
        t0, qc, ic = args
        kg = jax.vmap(lambda kb_, i_: kb_[i_])(k_blocks, ic).reshape(B, SLC_QBLOCK, k_sel * SLC_LEN, d)
        vg = jax.vmap(lambda vb_, i_: vb_[i_])(v_blocks, ic).reshape(B, SLC_QBLOCK, k_sel * SLC_LEN, d)
        tok = (ic[..., None] * SLC_LEN + offs).reshape(B, SLC_QBLOCK, k_sel * SLC_LEN)
        valid = tok <= (t0 + jnp.arange(SLC_QBLOCK))[None, :, None]
        s_ = jnp.einsum('bqhd,bqkd->bhqk', qc, kg).astype(jnp.float32) * scale
        p_ = jax.nn.softmax(jnp.where(valid[:, None], s_, NEG), axis=-1)
        return jnp.einsum('bhqk,bqkd->bqhd', p_.astype(vg.dtype), vg)

    o_slc = lax.map(sel_block, (jnp.arange(nq) * SLC_QBLOCK, q_ch, sel_ch))
    o_slc = o_slc.transpose(1, 0, 2, 3, 4).reshape(B, T, H, d)
    o_win = banded_attention(q, k_win[:, :, None, :], v_win[:, :, None, :], NSA_WINDOW)
    g = jax.nn.sigmoid(gates.astype(jnp.float32)).astype(q.dtype)
    return g[..., 0:1] * o_cmp + g[..., 1:2] * o_slc + g[..., 2:3] * o_win


def hybrid_mixer(x, w_in, b_in, a_sinks, b_conv_w, b_conv_b, lb, c_norm_g, d_cmp_pe, d_cmp_w1, d_cmp_w2, w_out):
    B, T, _ = x.shape
    pos = jnp.arange(T)
    f32 = jnp.float32
    silu = jax.nn.silu
    cols = split_cols(jnp.einsum('btd,dn->btn', x, w_in) + b_in)
    heads = lambda t: t.reshape(B, T, -1, HEAD_DIM)
    flat = lambda t: t.reshape(B, T, GROUP_W)
    q_a = rope(heads(cols['a_q']), pos)
    k_a = rope(heads(cols['a_k']), pos)
    y_a = flat(banded_attention(q_a, k_a, heads(cols['a_v']), A_WINDOW, a_sinks)) * silu(cols['a_z'])
    qk_b = silu(causal_conv(cols['b_qk'], b_conv_w, b_conv_b))
    h_b = mlstm_chunkwise(heads(qk_b[..., :GROUP_W]), heads(qk_b[..., GROUP_W:]), heads(cols['b_v']),
                          cols['b_if'][..., :N_HEADS], cols['b_if'][..., N_HEADS:])
    y_b = jax.nn.sigmoid(cols['b_o']) * flat(h_b) * silu(cols['b_z'])
    f_c = lb + (1.0 - lb) * jax.nn.sigmoid(cols['c_f'].astype(f32))
    o_c = hgrn2_chunkwise(heads(silu(cols['c_q'])), heads(jnp.log(f_c)), heads(1.0 - f_c), heads(cols['c_i']))
    o_c = o_c * lax.rsqrt(jnp.mean(jnp.square(o_c), axis=-1, keepdims=True) + C_EPS) * c_norm_g.astype(f32).reshape(N_HEADS, HEAD_DIM)
    y_c = flat(o_c).astype(x.dtype) * silu(cols['c_z'])
    q_d = rope(heads(cols['d_q']), pos)
    kv = cols['d_kv'].reshape(B, T, 6, HEAD_DIM)
    k_s = rope(kv[:, :, 2:3], pos)[:, :, 0]
    k_w = rope(kv[:, :, 4:5], pos)[:, :, 0]
    o_d = nsa_attention(q_d, kv[:, :, 0], kv[:, :, 1], k_s, kv[:, :, 3], k_w, kv[:, :, 5],
                        cols['d_g'].reshape(B, T, N_HEADS, 3), d_cmp_pe, d_cmp_w1, d_cmp_w2)
    y_d = flat(o_d) * silu(cols['d_z'])
    y = jnp.concatenate([y_a, y_b, y_c, y_d], axis=-1)
    return jnp.einsum('btm,md->btd', y, w_out)


def setup_inputs(seed: int = 0) -> dict:
    key = jax.random.key(seed)
    ks = jax.random.split(key, 17)
    f32 = jnp.float32
    nrm = lambda k, shape, s: s * jax.random.normal(k, shape, f32)
    x = jax.random.normal(ks[0], (BATCH, SEQ, D_MODEL), f32)
    ln0_g = 1.0 + nrm(ks[1], (D_MODEL,), 0.02)
    ln0_b = nrm(ks[2], (D_MODEL,), 0.02)
    w_in = nrm(ks[3], (DEPTH, D_MODEL, N_COLS), D_MODEL ** -0.5)
    off_f = _col_offset('b_if') + N_HEADS
    b_in = nrm(ks[4], (DEPTH, N_COLS), 0.02)
    b_in = b_in.at[:, off_f:off_f + N_HEADS].add(jnp.linspace(B_FBIAS_LO, B_FBIAS_HI, N_HEADS, dtype=f32))
    a_sinks = nrm(ks[5], (DEPTH, N_HEADS), 0.5)
    b_conv_w = nrm(ks[6], (DEPTH, B_CONV, 2 * GROUP_W), B_CONV ** -0.5)
    b_conv_b = nrm(ks[7], (DEPTH, 2 * GROUP_W), 0.02)
    c_lb = nrm(ks[8], (DEPTH, GROUP_W), 0.5)
    c_norm_g = 1.0 + nrm(ks[9], (DEPTH, GROUP_W), 0.02)
    d_cmp_pe = nrm(ks[10], (DEPTH, 2, CMP_LEN, HEAD_DIM), 0.02)
    d_cmp_w1 = nrm(ks[11], (DEPTH, 2, CMP_LEN * HEAD_DIM, HEAD_DIM), (CMP_LEN * HEAD_DIM) ** -0.5)
    d_cmp_w2 = nrm(ks[12], (DEPTH, 2, HEAD_DIM, HEAD_DIM), HEAD_DIM ** -0.5)
    w_out = nrm(ks[13], (DEPTH, D_MIX, D_MODEL), DN_BETA * D_MIX ** -0.5)
    ln_g = 1.0 + nrm(ks[14], (DEPTH, D_MODEL), 0.02)
    ln_b = nrm(ks[15], (DEPTH, D_MODEL), 0.02)
    return {'x': x, 'ln0_g': ln0_g, 'ln0_b': ln0_b, 'w_in': w_in, 'b_in': b_in, 'a_sinks': a_sinks,
            'b_conv_w': b_conv_w, 'b_conv_b': b_conv_b, 'c_lb': c_lb, 'c_norm_g': c_norm_g,
            'd_cmp_pe': d_cmp_pe, 'd_cmp_w1': d_cmp_w1, 'd_cmp_w2': d_cmp_w2, 'w_out': w_out,
            'ln_g': ln_g, 'ln_b': ln_b}


def reference(x, ln0_g, ln0_b, w_in, b_in, a_sinks, b_conv_w, b_conv_b, c_lb, c_norm_g, d_cmp_pe, d_cmp_w1, d_cmp_w2, w_out, ln_g, ln_b):
    lb = jnp.cumsum(jax.nn.softmax(c_lb.astype(jnp.float32), axis=0), axis=0)
    lb = lb - lb[0]
    h = layer_norm(x, ln0_g, ln0_b)
    for l in range(DEPTH):
        y = hybrid_mixer(h, w_in[l], b_in[l], a_sinks[l], b_conv_w[l], b_conv_b[l], lb[l], c_norm_g[l],
                         d_cmp_pe[l], d_cmp_w1[l], d_cmp_w2[l], w_out[l])
        h = layer_norm(DN_ALPHA * h + y, ln_g[l], ln_b[l])
    return h
```

```python
import functools

import numpy as np
import jax
import jax.numpy as jnp
from jax import lax
from jax.experimental import pallas as pl
from jax.experimental.pallas import tpu as pltpu

F32 = jnp.float32
BF16 = jnp.bfloat16

D_MODEL = 1024
DEPTH = 2
HEAD_DIM = 64
HALF = HEAD_DIM // 2
N_HEADS = 4
GROUP_W = N_HEADS * HEAD_DIM
ROPE_THETA = 10000.0
NEG = -1e30
LN_EPS = 1e-5
A_WINDOW = 128
B_CONV = 4
C_CHUNK = 64
C_SUB = 16
C_EPS = 1e-6
CMP_LEN = 32
CMP_STRIDE = 16
SLC_LEN = 64
N_SEL = 8
NSA_WINDOW = 512
FORCE_SCORE = 1e6
DN_ALPHA = (2.0 * DEPTH) ** 0.25
QBLK = 128
SCALE = HEAD_DIM ** -0.5

VMEM_LIMIT = 56 * 1024 * 1024

ORIG_SPLITS = (
    ('a_q', 256), ('a_k', 128), ('a_v', 128), ('a_z', 256),
    ('b_qk', 512), ('b_v', 256), ('b_if', 8), ('b_o', 256), ('b_z', 256),
    ('c_q', 256), ('c_f', 256), ('c_i', 256), ('c_z', 256),
    ('d_q', 256), ('d_kvc', 128), ('d_kvsw', 256), ('d_g', 12), ('d_z', 256),
)
NEW_LAYOUT = (
    ('b_qk', 512), ('a_q', 256), ('a_k', 128), ('a_v', 128), ('a_z', 256), ('b_v', 256), ('b_o', 256),
    ('b_z', 256), ('c_q', 256), ('c_f', 256), ('c_i', 256), ('c_z', 256), ('d_q', 256), ('d_z', 256),
    ('d_kvsw', 256), ('d_kvc', 128), ('d_g', 128), ('b_if', 128),
)
N_PROJ = sum(w for _, w in NEW_LAYOUT)


def _new_offset(name):
    off = 0
    for n, w in NEW_LAYOUT:
        if n == name:
            return off
        off += w
    raise KeyError(name)


def _permute_cols(w):
    src = {}
    off = 0
    for n, size in ORIG_SPLITS:
        src[n] = (off, size)
        off += size
    parts = []
    for n, width in NEW_LAYOUT:
        o, size = src[n]
        p = w[..., o:o + size]
        if size < width:
            p = jnp.concatenate([p, jnp.zeros(p.shape[:-1] + (width - size,), w.dtype)], axis=-1)
        parts.append(p)
    return jnp.concatenate(parts, axis=-1)


def _mm(a, b):
    return jnp.dot(a.astype(BF16), b.astype(BF16), preferred_element_type=F32)


def _mm_nt(a, b):
    return lax.dot_general(a.astype(BF16), b.astype(BF16), (((1,), (1,)), ((), ())),
                           preferred_element_type=F32)


def _mm_tn(a, b):
    return lax.dot_general(a.astype(BF16), b.astype(BF16), (((0,), (0,)), ((), ())),
                           preferred_element_type=F32)


def _mm_hi(a, b):
    return jnp.dot(a, b, preferred_element_type=F32, precision=lax.Precision.HIGHEST)


def _mm_hi_nt(a, b):
    return lax.dot_general(a, b, (((1,), (1,)), ((), ())), preferred_element_type=F32,
                           precision=lax.Precision.HIGHEST)


def _mm_split(a, b_exact):
    hi = a.astype(BF16)
    lo = (a - hi.astype(F32)).astype(BF16)
    return (jnp.dot(hi, b_exact, preferred_element_type=F32)
            + jnp.dot(lo, b_exact, preferred_element_type=F32))


def _sigmoid(x):
    return 1.0 / (1.0 + jnp.exp(-x))


def _silu(x):
    return x * _sigmoid(x)


def _log_sigmoid(x):
    return jnp.minimum(x, 0.0) - jnp.log1p(jnp.exp(-jnp.abs(x)))


def _rope(x, cos, sin_up, sin_dn):
    w = x.shape[-1]
    up = pltpu.roll(x, w - HALF, axis=1)
    dn = pltpu.roll(x, HALF, axis=1)
    return x * cos + up * sin_up + dn * sin_dn


def _rope_tables(pos, width, rot_lanes):
    inv = ROPE_THETA ** (-jnp.arange(HALF, dtype=F32) / HALF)
    ang = pos.astype(F32)[:, None] * inv[None, :]
    cos, sin = jnp.cos(ang), jnp.sin(ang)
    zero, one = jnp.zeros_like(sin), jnp.ones_like(cos)
    cos_h = jnp.concatenate([cos, cos], axis=1)
    up_h = jnp.concatenate([-sin, zero], axis=1)
    dn_h = jnp.concatenate([zero, sin], axis=1)
    id_c = jnp.concatenate([one, one], axis=1)
    id_s = jnp.concatenate([zero, zero], axis=1)
    cs, us, ds = [], [], []
    for l0 in range(0, width, HEAD_DIM):
        rot = (l0 % 128) < rot_lanes
        cs.append(cos_h if rot else id_c)
        us.append(up_h if rot else id_s)
        ds.append(dn_h if rot else id_s)
    return jnp.concatenate(cs, axis=1), jnp.concatenate(us, axis=1), jnp.concatenate(ds, axis=1)


def _iota(shape, dim):
    return lax.broadcasted_iota(jnp.int32, shape, dim)


def _cparams(sem):
    return pltpu.CompilerParams(dimension_semantics=sem, vmem_limit_bytes=VMEM_LIMIT)


def _ln_body(x_ref, g_ref, b_ref, o_ref):
    x = x_ref[...]
    mu = jnp.mean(x, axis=-1, keepdims=True)
    xc = x - mu
    var = jnp.mean(xc * xc, axis=-1, keepdims=True)
    o_ref[...] = xc * lax.rsqrt(var + LN_EPS) * g_ref[...] + b_ref[...]


def _layer_norm(x2, g, b, tm=512):
    m, d = x2.shape
    return pl.pallas_call(
        _ln_body,
        grid=(m // tm,),
        in_specs=[pl.BlockSpec((tm, d), lambda i: (i, 0)),
                  pl.BlockSpec((1, d), lambda i: (0, 0)),
                  pl.BlockSpec((1, d), lambda i: (0, 0))],
        out_specs=pl.BlockSpec((tm, d), lambda i: (i, 0)),
        out_shape=jax.ShapeDtypeStruct((m, d), F32),
        compiler_params=_cparams(("parallel",)),
        name="layer_norm",
    )(x2, g.reshape(1, d), b.reshape(1, d))


def _proj_body(x_ref, w_ref, b_ref, o_ref):
    o_ref[...] = jnp.dot(x_ref[...].astype(BF16), w_ref[...], preferred_element_type=F32) + b_ref[...]


def _in_proj(h2, w_bf, b, tm=256):
    m, d = h2.shape
    n = w_bf.shape[1]
    return pl.pallas_call(
        _proj_body,
        grid=(m // tm,),
        in_specs=[pl.BlockSpec((tm, d), lambda i: (i, 0)),
                  pl.BlockSpec((d, n), lambda i: (0, 0)),
                  pl.BlockSpec((1, n), lambda i: (0, 0))],
        out_specs=pl.BlockSpec((tm, n), lambda i: (i, 0)),
        out_shape=jax.ShapeDtypeStruct((m, n), F32),
        compiler_params=_cparams(("parallel",)),
        name="in_proj",
    )(h2, w_bf, b.reshape(1, n))


def _out_body(ya_ref, yb_ref, yc_ref, yd_ref, h_ref, w_ref, g_ref, b_ref, o_ref):
    y = _mm(ya_ref[...], w_ref[0:GROUP_W, :])
    y += _mm(yb_ref[...], w_ref[GROUP_W:2 * GROUP_W, :])
    y += _mm(yc_ref[...], w_ref[2 * GROUP_W:3 * GROUP_W, :])
    y += _mm(yd_ref[...], w_ref[3 * GROUP_W:4 * GROUP_W, :])
    r = DN_ALPHA * h_ref[...] + y
    mu = jnp.mean(r, axis=-1, keepdims=True)
    rc = r - mu
    var = jnp.mean(rc * rc, axis=-1, keepdims=True)
    o_ref[...] = rc * lax.rsqrt(var + LN_EPS) * g_ref[...] + b_ref[...]


def _out_proj(ys, h2, w_bf, g, b, tm=512):
    m, d = h2.shape
    yspec = pl.BlockSpec((tm, GROUP_W), lambda i: (i, 0))
    return pl.pallas_call(
        _out_body,
        grid=(m // tm,),
        in_specs=[yspec, yspec, yspec, yspec,
                  pl.BlockSpec((tm, d), lambda i: (i, 0)),
                  pl.BlockSpec((4 * GROUP_W, d), lambda i: (0, 0)),
                  pl.BlockSpec((1, d), lambda i: (0, 0)),
                  pl.BlockSpec((1, d), lambda i: (0, 0))],
        out_specs=pl.BlockSpec((tm, d), lambda i: (i, 0)),
        out_shape=jax.ShapeDtypeStruct((m, d), F32),
        compiler_params=_cparams(("parallel",)),
        name="out_proj",
    )(*[y.reshape(m, GROUP_W) for y in ys], h2, w_bf, g.reshape(1, d), b.reshape(1, d))


def _mixer_a_body(sink_ref, q_ref, kv_ref, z_ref, cos_ref, up_ref, dn_ref, o_ref, kr_ref):
    i = pl.program_id(1)
    seq = kv_ref.shape[1]

    @pl.when(i == 0)
    def _():
        def rope_rows(c, carry):
            rows = pl.ds(pl.multiple_of(c * 256, 256), 256)
            kr_ref[rows, :] = _rope(kv_ref[0, rows, 0:128], cos_ref[rows, 0:128], up_ref[rows, 0:128],
                                    dn_ref[rows, 0:128])
            return carry
        lax.fori_loop(0, seq // 256, rope_rows, 0)

    q0 = pl.multiple_of(i * QBLK, QBLK)
    qrows = pl.ds(q0, QBLK)
    qr = _rope(q_ref[0], cos_ref[qrows, :], up_ref[qrows, :], dn_ref[qrows, :]) * SCALE
    k0 = pl.multiple_of(jnp.maximum(i - 1, 0) * QBLK, QBLK)
    krows = pl.ds(k0, 2 * QBLK)
    kb = kr_ref[krows, :]
    vb = kv_ref[0, krows, 128:256]
    rel = (q0 + _iota((QBLK, 2 * QBLK), 0)) - (k0 + _iota((QBLK, 2 * QBLK), 1))
    band = (rel >= 0) & (rel < A_WINDOW)
    outs = []
    for h in range(N_HEADS):
        g = h // 2
        sink = sink_ref[h]
        s = _mm_nt(qr[:, h * HEAD_DIM:(h + 1) * HEAD_DIM], kb[:, g * HEAD_DIM:(g + 1) * HEAD_DIM])
        s = jnp.where(band, s, NEG)
        m = jnp.maximum(jnp.max(s, axis=-1, keepdims=True), sink)
        p = jnp.exp(s - m)
        l = jnp.sum(p, axis=-1, keepdims=True) + jnp.exp(sink - m)
        outs.append(_mm(p, vb[:, g * HEAD_DIM:(g + 1) * HEAD_DIM]) / l)
    o_ref[0] = jnp.concatenate(outs, axis=1) * _silu(z_ref[0])


def _mixer_a(proj, sinks, tabs):
    bsz, seq, _ = proj.shape
    cos, up, dn = tabs
    blk = lambda name: _new_offset(name) // GROUP_W
    tspec = pl.BlockSpec((seq, GROUP_W), lambda b, i: (0, 0))
    return pl.pallas_call(
        _mixer_a_body,
        grid=(bsz, seq // QBLK),
        in_specs=[pl.BlockSpec(memory_space=pltpu.SMEM),
                  pl.BlockSpec((1, QBLK, GROUP_W), lambda b, i: (b, i, blk('a_q'))),
                  pl.BlockSpec((1, seq, GROUP_W), lambda b, i: (b, 0, blk('a_k'))),
                  pl.BlockSpec((1, QBLK, GROUP_W), lambda b, i: (b, i, blk('a_z'))),
                  tspec, tspec, tspec],
        out_specs=pl.BlockSpec((1, QBLK, GROUP_W), lambda b, i: (b, i, 0)),
        out_shape=jax.ShapeDtypeStruct((bsz, seq, GROUP_W), F32),
        scratch_shapes=[pltpu.VMEM((seq, 128), F32)],
        compiler_params=_cparams(("arbitrary", "arbitrary")),
        name="mixer_a",
    )(sinks, proj, proj, proj, cos, up, dn)


B_CHUNK = 128


def _mixer_b_body(qk_ref, v_ref, og_ref, z_ref, if_ref, cw_ref, cb_ref, o_ref, xprev_ref, c_ref, m_ref):
    c = pl.program_id(1)
    L = B_CHUNK

    @pl.when(c == 0)
    def _():
        xprev_ref[...] = jnp.zeros_like(xprev_ref)
        c_ref[...] = jnp.zeros_like(c_ref)
        m_ref[...] = jnp.zeros_like(m_ref)

    x = qk_ref[0]
    xcat = jnp.concatenate([xprev_ref[...], x], axis=0)
    conv = cb_ref[...] + jnp.zeros_like(x)
    for j in range(B_CONV):
        s0 = 8 - (B_CONV - 1) + j
        conv = conv + cw_ref[j:j + 1, :] * xcat[s0:s0 + L, :]
    xprev_ref[...] = x[L - 8:L, :]
    qkc = _silu(conv)

    gates = if_ref[0]
    logf = _log_sigmoid(gates)
    tri = (_iota((L, L), 1) <= _iota((L, L), 0)).astype(F32)
    bcum = _mm_hi(tri, logf)
    gates_t = gates.T
    bcum_t = bcum.T
    causal = _iota((L, L), 1) <= _iota((L, L), 0)
    ones_col = (_iota((L, HEAD_DIM), 1) == 0).astype(F32)
    outs = []
    for h in range(N_HEADS):
        i_col = gates[:, h:h + 1]
        i_row = gates_t[h:h + 1, :]
        b_col = bcum[:, N_HEADS + h:N_HEADS + h + 1]
        b_row = bcum_t[N_HEADS + h:N_HEADS + h + 1, :]
        m_prev = m_ref[h:h + 1, 0:1]
        dmat = jnp.where(causal, b_col - b_row + i_row, NEG)
        inter = b_col + m_prev
        m_t = jnp.maximum(jnp.max(dmat, axis=-1, keepdims=True), inter)
        qh = qkc[:, h * HEAD_DIM:(h + 1) * HEAD_DIM] * SCALE
        kh = qkc[:, GROUP_W + h * HEAD_DIM:GROUP_W + (h + 1) * HEAD_DIM]
        vaug = jnp.concatenate([v_ref[0, :, h * HEAD_DIM:(h + 1) * HEAD_DIM], ones_col], axis=1)
        smat = _mm_nt(qh, kh) * jnp.exp(dmat - m_t)
        cst = c_ref[h]
        tot = _mm(smat, vaug) + jnp.exp(inter - m_t) * _mm(qh, cst)
        num = tot[:, 0:HEAD_DIM]
        den = tot[:, HEAD_DIM:HEAD_DIM + 1]
        outs.append(num / jnp.maximum(jnp.abs(den), jnp.exp(-m_t)))
        b_last = bcum[L - 1:L, N_HEADS + h:N_HEADS + h + 1]
        d_s = b_last - b_col + i_col
        m_new = jnp.maximum(b_last + m_prev, jnp.max(d_s, axis=0, keepdims=True))
        w_s = jnp.exp(d_s - m_new)
        decay = jnp.exp(b_last + m_prev - m_new)
        c_ref[h] = decay * cst + _mm_tn(kh * w_s, vaug)
        m_ref[h:h + 1, :] = jnp.broadcast_to(m_new, (1, 128))
    o_ref[0] = _sigmoid(og_ref[0]) * jnp.concatenate(outs, axis=1) * _silu(z_ref[0])


def _mixer_b(proj, conv_w, conv_b):
    bsz, seq, _ = proj.shape
    L = B_CHUNK
    blk = lambda name: _new_offset(name) // GROUP_W
    return pl.pallas_call(
        _mixer_b_body,
        grid=(bsz, seq // L),
        in_specs=[pl.BlockSpec((1, L, 2 * GROUP_W), lambda b, c: (b, c, 0)),
                  pl.BlockSpec((1, L, GROUP_W), lambda b, c: (b, c, blk('b_v'))),
                  pl.BlockSpec((1, L, GROUP_W), lambda b, c: (b, c, blk('b_o'))),
                  pl.BlockSpec((1, L, GROUP_W), lambda b, c: (b, c, blk('b_z'))),
                  pl.BlockSpec((1, L, 128), lambda b, c: (b, c, _new_offset('b_if') // 128)),
                  pl.BlockSpec((B_CONV, 2 * GROUP_W), lambda b, c: (0, 0)),
                  pl.BlockSpec((1, 2 * GROUP_W), lambda b, c: (0, 0))],
        out_specs=pl.BlockSpec((1, L, GROUP_W), lambda b, c: (b, c, 0)),
        out_shape=jax.ShapeDtypeStruct((bsz, seq, GROUP_W), F32),
        scratch_shapes=[pltpu.VMEM((8, 2 * GROUP_W), F32),
                        pltpu.VMEM((N_HEADS, HEAD_DIM, 128), F32),
                        pltpu.VMEM((8, 128), F32)],
        compiler_params=_cparams(("arbitrary", "arbitrary")),
        name="mixer_b",
    )(proj, proj, proj, proj, proj, conv_w, conv_b.reshape(1, -1))


def _mixer_c_body(layer, q_ref, f_ref, i_ref, z_ref, lb_ref, gn_ref, o_ref, st_ref):
    c = pl.program_id(1)
    L = C_CHUNK
    nsub = L // C_SUB

    @pl.when(c == 0)
    def _():
        st_ref[...] = jnp.zeros_like(st_ref)

    raw = lb_ref[...]
    ex = jnp.exp(raw - jnp.max(raw, axis=0, keepdims=True))
    if layer == 0:
        lb = jnp.zeros((1, GROUP_W), F32)
    else:
        lb = jnp.sum(ex[1:layer + 1], axis=0, keepdims=True) / jnp.sum(ex, axis=0, keepdims=True)
    f = lb + (1.0 - lb) * _sigmoid(f_ref[0])
    logf = jnp.log(f)
    k = 1.0 - f
    q = _silu(q_ref[0])
    v = i_ref[0]
    tri = (_iota((L, L), 1) <= _iota((L, L), 0)).astype(F32)
    a = _mm_hi(tri, logf)
    head_of_lane = _iota((1, GROUP_W), 1) // HEAD_DIM
    same_head = ((_iota((GROUP_W, GROUP_W), 0) // HEAD_DIM)
                 == (_iota((GROUP_W, GROUP_W), 1) // HEAD_DIM))
    ones_bd = same_head.astype(BF16)

    st = st_ref[...]
    o = _mm_nt(q * jnp.exp(a), st)

    pieces = [o[0:C_SUB]]
    for i in range(1, nsub):
        r0 = i * C_SUB
        piv = a[r0 - 1:r0, :]
        qt = q[r0:r0 + C_SUB] * jnp.exp(a[r0:r0 + C_SUB] - piv)
        kt = k[0:r0] * jnp.exp(piv - a[0:r0])
        kst = jnp.concatenate([jnp.where(head_of_lane == h, kt, 0.0) for h in range(N_HEADS)], axis=0)
        vst = jnp.concatenate([jnp.where(head_of_lane == h, v[0:r0], 0.0) for h in range(N_HEADS)], axis=0)
        pieces.append(o[r0:r0 + C_SUB] + _mm(_mm_nt(qt, kst), vst))
    o = jnp.concatenate(pieces, axis=0)

    a3 = a.reshape(nsub, C_SUB, GROUP_W)
    q3 = q.reshape(nsub, C_SUB, GROUP_W)
    k3 = k.reshape(nsub, C_SUB, GROUP_W)
    v3 = v.reshape(nsub, C_SUB, GROUP_W)
    t_in = _iota((nsub, C_SUB, GROUP_W), 1)
    for s in range(C_SUB):
        e = jnp.exp(jnp.minimum(a3 - a3[:, s:s + 1, :], 0.0))
        p = jnp.where(t_in >= s, q3 * k3[:, s:s + 1, :] * e, 0.0)
        w = _mm_split(p.reshape(L, GROUP_W), ones_bd)
        o = o + w * jnp.broadcast_to(v3[:, s:s + 1, :], (nsub, C_SUB, GROUP_W)).reshape(L, GROUP_W)

    a_last = a[L - 1:L, :]
    kdec = k * jnp.exp(a_last - a)
    st_ref[...] = st * jnp.exp(a_last) + jnp.where(same_head, _mm_tn(v, kdec), 0.0)

    ms = _mm_split(o * o, ones_bd) * (1.0 / HEAD_DIM)
    o = o * lax.rsqrt(ms + C_EPS) * gn_ref[...]
    o_ref[0] = o * _silu(z_ref[0])


def _mixer_c(proj, c_lb, norm_g, layer):
    bsz, seq, _ = proj.shape
    L = C_CHUNK
    blk = lambda name: _new_offset(name) // GROUP_W
    spec = lambda name: pl.BlockSpec((1, L, GROUP_W), lambda b, c: (b, c, blk(name)))
    vec = pl.BlockSpec((1, GROUP_W), lambda b, c: (0, 0))
    return pl.pallas_call(
        functools.partial(_mixer_c_body, layer),
        grid=(bsz, seq // L),
        in_specs=[spec('c_q'), spec('c_f'), spec('c_i'), spec('c_z'),
                  pl.BlockSpec((DEPTH, GROUP_W), lambda b, c: (0, 0)), vec],
        out_specs=pl.BlockSpec((1, L, GROUP_W), lambda b, c: (b, c, 0)),
        out_shape=jax.ShapeDtypeStruct((bsz, seq, GROUP_W), F32),
        scratch_shapes=[pltpu.VMEM((GROUP_W, GROUP_W), F32)],
        compiler_params=_cparams(("arbitrary", "arbitrary")),
        name="mixer_c",
    )(proj, proj, proj, proj, c_lb, norm_g.reshape(1, -1))


N_CMP_PAD = 128
N_SLC_PAD = 128
KT = 128


def _mixer_d_body(q_ref, z_ref, kv_ref, cin_ref, g_ref, cos_ref, up_ref, dn_ref, kcos_ref, kup_ref, kdn_ref,
                  ccos_ref, cup_ref, cdn_ref, w1_ref, pe_ref, w2_ref, ov_ref, o_ref, ksw_ref, kvc_ref):
    i = pl.program_id(1)
    seq = kv_ref.shape[1]
    half_cmp = CMP_LEN // 2

    @pl.when(i == 0)
    def _():
        def rope_rows(c, carry):
            rows = pl.ds(pl.multiple_of(c * 256, 256), 256)
            for t in range(2):
                ksw_ref[rows, t * 128:(t + 1) * 128] = _rope(
                    kv_ref[0, rows, t * 128:(t + 1) * 128], kcos_ref[rows, :], kup_ref[rows, :],
                    kdn_ref[rows, :])
            return carry
        lax.fori_loop(0, seq // 256, rope_rows, 0)
        u0 = jnp.zeros((N_CMP_PAD, 128), F32)
        u1 = jnp.zeros((N_CMP_PAD, 128), F32)
        for r in range(half_cmp):
            xr = cin_ref[0, pl.ds(r, N_CMP_PAD, stride=CMP_STRIDE), :]
            u0 = u0 + _mm_hi(xr + pe_ref[r:r + 1, :], w1_ref[r])
            u1 = u1 + _mm_hi(xr + pe_ref[half_cmp + r:half_cmp + r + 1, :], w1_ref[half_cmp + r])
        pre = u0 + pltpu.roll(u1, N_CMP_PAD - 1, axis=0)
        cv = _mm_hi(_silu(pre), w2_ref[...])
        kvc_ref[...] = _rope(cv, ccos_ref[...], cup_ref[...], cdn_ref[...])

    q0 = pl.multiple_of(i * QBLK, QBLK)
    qrows = pl.ds(q0, QBLK)
    qr = _rope(q_ref[0], cos_ref[qrows, :], up_ref[qrows, :], dn_ref[qrows, :]) * SCALE
    qs = jnp.concatenate([qr[:, h * HEAD_DIM:(h + 1) * HEAD_DIM] for h in range(N_HEADS)], axis=0)
    rows4 = N_HEADS * QBLK
    qpos1 = q0 + _iota((QBLK, 1), 0)

    def stack4(x):
        return jnp.concatenate([x] * N_HEADS, axis=0)

    kvc = kvc_ref[...]
    sc = _mm_hi_nt(qs, kvc[:, 0:HEAD_DIM])
    ends = _iota((1, N_CMP_PAD), 1) * CMP_STRIDE + (CMP_LEN - 1)
    cvalid = stack4(ends <= qpos1)
    scm = jnp.where(cvalid, sc, NEG)
    e = jnp.where(cvalid, jnp.exp(scm - jnp.max(scm, axis=-1, keepdims=True)), 0.0)
    l = jnp.sum(e, axis=-1, keepdims=True)
    pc = e * (1.0 / jnp.where(l > 0.0, l, 1.0))
    o_cmp = _mm(pc, kvc[:, HEAD_DIM:2 * HEAD_DIM])
    psum = pc[0:QBLK] + pc[QBLK:2 * QBLK] + pc[2 * QBLK:3 * QBLK] + pc[3 * QBLK:4 * QBLK]

    imp = _mm_hi(psum, ov_ref[...])
    jl = _iota((1, N_SLC_PAD), 1)
    cur = qpos1 // SLC_LEN
    forced = (jl == 0) | (jl == cur) | (jl == cur - 1)
    imp = jnp.where(forced, FORCE_SCORE, imp)
    imp = jnp.where(jl <= cur, imp, NEG)
    rank = jnp.zeros((QBLK, N_SLC_PAD), F32)
    for jp in range(seq // SLC_LEN):
        cj = imp[:, jp:jp + 1]
        ahead = (cj > imp) | ((cj == imp) & (jl > jp))
        rank = rank + jnp.where(ahead, 1.0, 0.0)
    sel = jnp.where(rank < N_SEL, 1.0, 0.0).astype(BF16)

    qpos4 = stack4(qpos1)

    def sel_tile(kt, carry):
        m_old, l_old, acc = carry
        kbase = pl.multiple_of(kt * KT, KT)
        krows = pl.ds(kbase, KT)
        blk_of_key = (kbase + _iota((N_SLC_PAD, KT), 1)) // SLC_LEN
        expand = jnp.where(_iota((N_SLC_PAD, KT), 0) == blk_of_key, 1.0, 0.0).astype(BF16)
        chosen = jnp.dot(sel, expand, preferred_element_type=F32) > 0.5
        ok = stack4(chosen) & ((kbase + _iota((1, KT), 1)) <= qpos4)
        s = jnp.where(ok, _mm_nt(qs, ksw_ref[krows, 0:HEAD_DIM]), NEG)
        m_new = jnp.maximum(m_old, jnp.max(s, axis=-1, keepdims=True))
        p = jnp.where(ok, jnp.exp(s - m_new), 0.0)
        alpha = jnp.exp(m_old - m_new)
        l_new = alpha * l_old + jnp.sum(p, axis=-1, keepdims=True)
        acc_new = alpha * acc + _mm(p, ksw_ref[krows, HEAD_DIM:2 * HEAD_DIM])
        return m_new, l_new, acc_new

    init = (jnp.full((rows4, 1), NEG, F32), jnp.zeros((rows4, 1), F32), jnp.zeros((rows4, HEAD_DIM), F32))
    _, l_s, acc_s = lax.fori_loop(0, i + 1, sel_tile, init)
    o_slc = acc_s / l_s

    span = NSA_WINDOW + QBLK
    k0 = pl.multiple_of(jnp.maximum(i - NSA_WINDOW // QBLK, 0) * QBLK, QBLK)
    wrows = pl.ds(k0, span)
    rel = qpos4 - (k0 + _iota((1, span), 1))
    band = (rel >= 0) & (rel < NSA_WINDOW)
    s = jnp.where(band, _mm_nt(qs, ksw_ref[wrows, 2 * HEAD_DIM:3 * HEAD_DIM]), NEG)
    p = jnp.exp(s - jnp.max(s, axis=-1, keepdims=True))
    o_win = _mm(p, ksw_ref[wrows, 3 * HEAD_DIM:4 * HEAD_DIM]) / jnp.sum(p, axis=-1, keepdims=True)

    gate = _sigmoid(g_ref[0])
    outs = []
    for h in range(N_HEADS):
        rs = slice(h * QBLK, (h + 1) * QBLK)
        outs.append(gate[:, 3 * h:3 * h + 1] * o_cmp[rs] + gate[:, 3 * h + 1:3 * h + 2] * o_slc[rs]
                    + gate[:, 3 * h + 2:3 * h + 3] * o_win[rs])
    o_ref[0] = jnp.concatenate(outs, axis=1) * _silu(z_ref[0])


def _mixer_d(proj, qtabs, ktabs, ctabs, w1bd, pe2, w2bd, overlap):
    bsz, seq, _ = proj.shape
    blk = lambda name: _new_offset(name) // GROUP_W
    full = lambda shape: pl.BlockSpec(shape, lambda b, i: (0,) * len(shape))
    return pl.pallas_call(
        _mixer_d_body,
        grid=(bsz, seq // QBLK),
        in_specs=[pl.BlockSpec((1, QBLK, GROUP_W), lambda b, i: (b, i, blk('d_q'))),
                  pl.BlockSpec((1, QBLK, GROUP_W), lambda b, i: (b, i, blk('d_z'))),
                  pl.BlockSpec((1, seq, GROUP_W), lambda b, i: (b, 0, blk('d_kvsw'))),
                  pl.BlockSpec((1, seq, 128), lambda b, i: (b, 0, _new_offset('d_kvc') // 128)),
                  pl.BlockSpec((1, QBLK, 128), lambda b, i: (b, i, _new_offset('d_g') // 128)),
                  full((seq, GROUP_W)), full((seq, GROUP_W)), full((seq, GROUP_W)),
                  full((seq, 128)), full((seq, 128)), full((seq, 128)),
                  full((N_CMP_PAD, 128)), full((N_CMP_PAD, 128)), full((N_CMP_PAD, 128)),
                  full((CMP_LEN, 128, 128)), full((CMP_LEN, 128)), full((128, 128)),
                  full((N_CMP_PAD, N_SLC_PAD))],
        out_specs=pl.BlockSpec((1, QBLK, GROUP_W), lambda b, i: (b, i, 0)),
        out_shape=jax.ShapeDtypeStruct((bsz, seq, GROUP_W), F32),
        scratch_shapes=[pltpu.VMEM((seq, 256), F32), pltpu.VMEM((N_CMP_PAD, 128), F32)],
        compiler_params=_cparams(("arbitrary", "arbitrary")),
        name="mixer_d",
    )(proj, proj, proj, proj, proj, *qtabs, *ktabs, *ctabs, w1bd, pe2, w2bd, overlap)


def _cmp_params(pe, w1, w2):
    w1 = w1.reshape(2, CMP_LEN, HEAD_DIM, HEAD_DIM)
    z = jnp.zeros((CMP_LEN, HEAD_DIM, HEAD_DIM), F32)
    w1bd = jnp.concatenate([jnp.concatenate([w1[0], z], axis=2), jnp.concatenate([z, w1[1]], axis=2)], axis=1)
    z2 = jnp.zeros((HEAD_DIM, HEAD_DIM), F32)
    w2bd = jnp.concatenate([jnp.concatenate([w2[0], z2], axis=1), jnp.concatenate([z2, w2[1]], axis=1)], axis=0)
    pe2 = jnp.concatenate([pe[0], pe[1]], axis=1)
    return w1bd, pe2, w2bd


def _overlap_matrix(seq):
    n_cmp = (seq - CMP_LEN) // CMP_STRIDE + 1
    starts = np.arange(N_CMP_PAD) * CMP_STRIDE
    blk = np.arange(N_SLC_PAD)
    ov = ((starts[:, None] < (blk[None, :] + 1) * SLC_LEN) & (starts[:, None] + CMP_LEN > blk[None, :] * SLC_LEN))
    ov = ov & (np.arange(N_CMP_PAD)[:, None] < n_cmp) & (blk[None, :] < seq // SLC_LEN)
    return jnp.asarray(ov.astype(np.float32))


def kernel(x, ln0_g, ln0_b, w_in, b_in, a_sinks, b_conv_w, b_conv_b, c_lb, c_norm_g, d_cmp_pe, d_cmp_w1, d_cmp_w2,
           w_out, ln_g, ln_b):
    bsz, seq, d = x.shape
    pos = jnp.arange(seq)
    qtabs = _rope_tables(pos, GROUP_W, 128)
    ktabs = _rope_tables(pos, 128, HEAD_DIM)
    ctabs = _rope_tables(jnp.arange(N_CMP_PAD) * CMP_STRIDE + (CMP_LEN - 1), 128, HEAD_DIM)
    overlap = _overlap_matrix(seq)

    h = _layer_norm(x.reshape(bsz * seq, d), ln0_g, ln0_b)
    for l in range(DEPTH):
        w_p = _permute_cols(w_in[l]).astype(BF16)
        b_p = _permute_cols(b_in[l])
        proj = _in_proj(h, w_p, b_p).reshape(bsz, seq, N_PROJ)
        y_a = _mixer_a(proj, a_sinks[l], qtabs)
        y_b = _mixer_b(proj, b_conv_w[l], b_conv_b[l])
        y_c = _mixer_c(proj, c_lb, c_norm_g[l], l)
        w1bd, pe2, w2bd = _cmp_params(d_cmp_pe[l], d_cmp_w1[l], d_cmp_w2[l])
        y_d = _mixer_d(proj, qtabs, ktabs, ctabs, w1bd, pe2, w2bd, overlap)
        h = _out_proj((y_a, y_b, y_c, y_d), h, w_out[l].astype(BF16), ln_g[l], ln_b[l])
    return h.reshape(bsz, seq, d)
```

```python
import functools

import numpy as np
import jax
import jax.numpy as jnp
from jax import lax
from jax.experimental import pallas as pl
from jax.experimental.pallas import tpu as pltpu

F32 = jnp.float32
BF16 = jnp.bfloat16

D_MODEL = 1024
DEPTH = 2
HEAD_DIM = 64
HALF = HEAD_DIM // 2
N_HEADS = 4
GROUP_W = N_HEADS * HEAD_DIM
ROPE_THETA = 10000.0
NEG = -1e30
LN_EPS = 1e-5
A_WINDOW = 128
B_CONV = 4
C_CHUNK = 64
C_SUB = 16
C_EPS = 1e-6
CMP_LEN = 32
CMP_STRIDE = 16
SLC_LEN = 64
N_SEL = 8
NSA_WINDOW = 512
FORCE_SCORE = 1e6
DN_ALPHA = (2.0 * DEPTH) ** 0.25
QBLK = 128
SCALE = HEAD_DIM ** -0.5

VMEM_LIMIT = 56 * 1024 * 1024

ORIG_SPLITS = (
    ('a_q', 256), ('a_k', 128), ('a_v', 128), ('a_z', 256),
    ('b_qk', 512), ('b_v', 256), ('b_if', 8), ('b_o', 256), ('b_z', 256),
    ('c_q', 256), ('c_f', 256), ('c_i', 256), ('c_z', 256),
    ('d_q', 256), ('d_kvc', 128), ('d_kvsw', 256), ('d_g', 12), ('d_z', 256),
)
NEW_LAYOUT = (
    ('b_qk', 512), ('a_q', 256), ('a_k', 128), ('a_v', 128), ('a_z', 256), ('b_v', 256), ('b_o', 256),
    ('b_z', 256), ('c_q', 256), ('c_f', 256), ('c_i', 256), ('c_z', 256), ('d_q', 256), ('d_z', 256),
    ('d_kvsw', 256), ('d_kvc', 128), ('d_g', 128), ('b_if', 128),
)
N_PROJ = sum(w for _, w in NEW_LAYOUT)


def _new_offset(name):
    off = 0
    for n, w in NEW_LAYOUT:
        if n == name:
            return off
        off += w
    raise KeyError(name)


def _permute_cols(w):
    src = {}
    off = 0
    for n, size in ORIG_SPLITS:
        src[n] = (off, size)
        off += size
    parts = []
    for n, width in NEW_LAYOUT:
        o, size = src[n]
        p = w[..., o:o + size]
        if size < width:
            p = jnp.concatenate([p, jnp.zeros(p.shape[:-1] + (width - size,), w.dtype)], axis=-1)
        parts.append(p)
    return jnp.concatenate(parts, axis=-1)


def _mm(a, b):
    return jnp.dot(a.astype(BF16), b.astype(BF16), preferred_element_type=F32)


def _mm_nt(a, b):
    return lax.dot_general(a.astype(BF16), b.astype(BF16), (((1,), (1,)), ((), ())),
                           preferred_element_type=F32)


def _mm_tn(a, b):
    return lax.dot_general(a.astype(BF16), b.astype(BF16), (((0,), (0,)), ((), ())),
                           preferred_element_type=F32)


def _mm_hi(a, b):
    return jnp.dot(a, b, preferred_element_type=F32, precision=lax.Precision.HIGHEST)


def _mm_hi_nt(a, b):
    return lax.dot_general(a, b, (((1,), (1,)), ((), ())), preferred_element_type=F32,
                           precision=lax.Precision.HIGHEST)


def _mm_split(a, b_exact):
    hi = a.astype(BF16)
    lo = (a - hi.astype(F32)).astype(BF16)
    return (jnp.dot(hi, b_exact, preferred_element_type=F32)
            + jnp.dot(lo, b_exact, preferred_element_type=F32))


def _sigmoid(x):
    return 1.0 / (1.0 + jnp.exp(-x))


def _silu(x):
    return x * _sigmoid(x)


def _log_sigmoid(x):
    return jnp.minimum(x, 0.0) - jnp.log1p(jnp.exp(-jnp.abs(x)))


def _rope(x, cos, sin_up, sin_dn):
    w = x.shape[-1]
    up = pltpu.roll(x, w - HALF, axis=1)
    dn = pltpu.roll(x, HALF, axis=1)
    return x * cos + up * sin_up + dn * sin_dn


def _rope_tables(pos, width, rot_lanes):
    inv = ROPE_THETA ** (-jnp.arange(HALF, dtype=F32) / HALF)
    ang = pos.astype(F32)[:, None] * inv[None, :]
    cos, sin = jnp.cos(ang), jnp.sin(ang)
    zero, one = jnp.zeros_like(sin), jnp.ones_like(cos)
    cos_h = jnp.concatenate([cos, cos], axis=1)
    up_h = jnp.concatenate([-sin, zero], axis=1)
    dn_h = jnp.concatenate([zero, sin], axis=1)
    id_c = jnp.concatenate([one, one], axis=1)
    id_s = jnp.concatenate([zero, zero], axis=1)
    cs, us, ds = [], [], []
    for l0 in range(0, width, HEAD_DIM):
        rot = (l0 % 128) < rot_lanes
        cs.append(cos_h if rot else id_c)
        us.append(up_h if rot else id_s)
        ds.append(dn_h if rot else id_s)
    return jnp.concatenate(cs, axis=1), jnp.concatenate(us, axis=1), jnp.concatenate(ds, axis=1)


def _iota(shape, dim):
    return lax.broadcasted_iota(jnp.int32, shape, dim)


def _cparams(sem):
    return pltpu.CompilerParams(dimension_semantics=sem, vmem_limit_bytes=VMEM_LIMIT)


def _ln_body(x_ref, g_ref, b_ref, o_ref):
    x = x_ref[...]
    mu = jnp.mean(x, axis=-1, keepdims=True)
    xc = x - mu
    var = jnp.mean(xc * xc, axis=-1, keepdims=True)
    o_ref[...] = xc * lax.rsqrt(var + LN_EPS) * g_ref[...] + b_ref[...]


def _layer_norm(x2, g, b, tm=512):
    m, d = x2.shape
    return pl.pallas_call(
        _ln_body,
        grid=(m // tm,),
        in_specs=[pl.BlockSpec((tm, d), lambda i: (i, 0)),
                  pl.BlockSpec((1, d), lambda i: (0, 0)),
                  pl.BlockSpec((1, d), lambda i: (0, 0))],
        out_specs=pl.BlockSpec((tm, d), lambda i: (i, 0)),
        out_shape=jax.ShapeDtypeStruct((m, d), F32),
        compiler_params=_cparams(("parallel",)),
        name="layer_norm",
    )(x2, g.reshape(1, d), b.reshape(1, d))


def _proj_body(x_ref, w_ref, b_ref, o_ref):
    o_ref[...] = jnp.dot(x_ref[...].astype(BF16), w_ref[...], preferred_element_type=F32) + b_ref[...]


def _in_proj(h2, w_bf, b, tm=256):
    m, d = h2.shape
    n = w_bf.shape[1]
    return pl.pallas_call(
        _proj_body,
        grid=(m // tm,),
        in_specs=[pl.BlockSpec((tm, d), lambda i: (i, 0)),
                  pl.BlockSpec((d, n), lambda i: (0, 0)),
                  pl.BlockSpec((1, n), lambda i: (0, 0))],
        out_specs=pl.BlockSpec((tm, n), lambda i: (i, 0)),
        out_shape=jax.ShapeDtypeStruct((m, n), F32),
        compiler_params=_cparams(("parallel",)),
        name="in_proj",
    )(h2, w_bf, b.reshape(1, n))


def _out_body(ya_ref, yb_ref, yc_ref, yd_ref, h_ref, w_ref, g_ref, b_ref, o_ref):
    y = _mm(ya_ref[...], w_ref[0:GROUP_W, :])
    y += _mm(yb_ref[...], w_ref[GROUP_W:2 * GROUP_W, :])
    y += _mm(yc_ref[...], w_ref[2 * GROUP_W:3 * GROUP_W, :])
    y += _mm(yd_ref[...], w_ref[3 * GROUP_W:4 * GROUP_W, :])
    r = DN_ALPHA * h_ref[...] + y
    mu = jnp.mean(r, axis=-1, keepdims=True)
    rc = r - mu
    var = jnp.mean(rc * rc, axis=-1, keepdims=True)
    o_ref[...] = rc * lax.rsqrt(var + LN_EPS) * g_ref[...] + b_ref[...]


def _out_proj(ys, h2, w_bf, g, b, tm=512):
    m, d = h2.shape
    yspec = pl.BlockSpec((tm, GROUP_W), lambda i: (i, 0))
    return pl.pallas_call(
        _out_body,
        grid=(m // tm,),
        in_specs=[yspec, yspec, yspec, yspec,
                  pl.BlockSpec((tm, d), lambda i: (i, 0)),
                  pl.BlockSpec((4 * GROUP_W, d), lambda i: (0, 0)),
                  pl.BlockSpec((1, d), lambda i: (0, 0)),
                  pl.BlockSpec((1, d), lambda i: (0, 0))],
        out_specs=pl.BlockSpec((tm, d), lambda i: (i, 0)),
        out_shape=jax.ShapeDtypeStruct((m, d), F32),
        compiler_params=_cparams(("parallel",)),
        name="out_proj",
    )(*[y.reshape(m, GROUP_W) for y in ys], h2, w_bf, g.reshape(1, d), b.reshape(1, d))


def _mixer_a_body(sink_ref, q_ref, kv_ref, z_ref, cos_ref, up_ref, dn_ref, o_ref, kr_ref):
    i = pl.program_id(1)
    seq = kv_ref.shape[1]

    @pl.when(i == 0)
    def _():
        def rope_rows(c, carry):
            rows = pl.ds(pl.multiple_of(c * 256, 256), 256)
            kr_ref[rows, :] = _rope(kv_ref[0, rows, 0:128], cos_ref[rows, 0:128], up_ref[rows, 0:128],
                                    dn_ref[rows, 0:128])
            return carry
        lax.fori_loop(0, seq // 256, rope_rows, 0)

    q0 = pl.multiple_of(i * QBLK, QBLK)
    qrows = pl.ds(q0, QBLK)
    qr = _rope(q_ref[0], cos_ref[qrows, :], up_ref[qrows, :], dn_ref[qrows, :]) * SCALE
    k0 = pl.multiple_of(jnp.maximum(i - 1, 0) * QBLK, QBLK)
    krows = pl.ds(k0, 2 * QBLK)
    kb = kr_ref[krows, :]
    vb = kv_ref[0, krows, 128:256]
    rel = (q0 + _iota((QBLK, 2 * QBLK), 0)) - (k0 + _iota((QBLK, 2 * QBLK), 1))
    band = (rel >= 0) & (rel < A_WINDOW)
    outs = []
    for h in range(N_HEADS):
        g = h // 2
        sink = sink_ref[h]
        s = _mm_nt(qr[:, h * HEAD_DIM:(h + 1) * HEAD_DIM], kb[:, g * HEAD_DIM:(g + 1) * HEAD_DIM])
        s = jnp.where(band, s, NEG)
        m = jnp.maximum(jnp.max(s, axis=-1, keepdims=True), sink)
        p = jnp.exp(s - m)
        l = jnp.sum(p, axis=-1, keepdims=True) + jnp.exp(sink - m)
        outs.append(_mm(p, vb[:, g * HEAD_DIM:(g + 1) * HEAD_DIM]) / l)
    o_ref[0] = jnp.concatenate(outs, axis=1) * _silu(z_ref[0])


def _mixer_a(proj, sinks, tabs):
    bsz, seq, _ = proj.shape
    cos, up, dn = tabs
    blk = lambda name: _new_offset(name) // GROUP_W
    tspec = pl.BlockSpec((seq, GROUP_W), lambda b, i: (0, 0))
    return pl.pallas_call(
        _mixer_a_body,
        grid=(bsz, seq // QBLK),
        in_specs=[pl.BlockSpec(memory_space=pltpu.SMEM),
                  pl.BlockSpec((1, QBLK, GROUP_W), lambda b, i: (b, i, blk('a_q'))),
                  pl.BlockSpec((1, seq, GROUP_W), lambda b, i: (b, 0, blk('a_k'))),
                  pl.BlockSpec((1, QBLK, GROUP_W), lambda b, i: (b, i, blk('a_z'))),
                  tspec, tspec, tspec],
        out_specs=pl.BlockSpec((1, QBLK, GROUP_W), lambda b, i: (b, i, 0)),
        out_shape=jax.ShapeDtypeStruct((bsz, seq, GROUP_W), F32),
        scratch_shapes=[pltpu.VMEM((seq, 128), F32)],
        compiler_params=_cparams(("arbitrary", "arbitrary")),
        name="mixer_a",
    )(sinks, proj, proj, proj, cos, up, dn)


B_CHUNK = 128


def _mixer_b_body(qk_ref, v_ref, og_ref, z_ref, if_ref, cw_ref, cb_ref, o_ref, xprev_ref, c_ref, m_ref):
    c = pl.program_id(1)
    L = B_CHUNK

    @pl.when(c == 0)
    def _():
        xprev_ref[...] = jnp.zeros_like(xprev_ref)
        c_ref[...] = jnp.zeros_like(c_ref)
        m_ref[...] = jnp.zeros_like(m_ref)

    x = qk_ref[0]
    xcat = jnp.concatenate([xprev_ref[...], x], axis=0)
    conv = cb_ref[...] + jnp.zeros_like(x)
    for j in range(B_CONV):
        s0 = 8 - (B_CONV - 1) + j
        conv = conv + cw_ref[j:j + 1, :] * xcat[s0:s0 + L, :]
    xprev_ref[...] = x[L - 8:L, :]
    qkc = _silu(conv)

    gates = if_ref[0]
    logf = _log_sigmoid(gates)
    tri = (_iota((L, L), 1) <= _iota((L, L), 0)).astype(F32)
    bcum = _mm_hi(tri, logf)
    gates_t = gates.T
    bcum_t = bcum.T
    causal = _iota((L, L), 1) <= _iota((L, L), 0)
    ones_col = (_iota((L, HEAD_DIM), 1) == 0).astype(F32)
    outs = []
    for h in range(N_HEADS):
        i_col = gates[:, h:h + 1]
        i_row = gates_t[h:h + 1, :]
        b_col = bcum[:, N_HEADS + h:N_HEADS + h + 1]
        b_row = bcum_t[N_HEADS + h:N_HEADS + h + 1, :]
        m_prev = m_ref[h:h + 1, 0:1]
        dmat = jnp.where(causal, b_col - b_row + i_row, NEG)
        inter = b_col + m_prev
        m_t = jnp.maximum(jnp.max(dmat, axis=-1, keepdims=True), inter)
        qh = qkc[:, h * HEAD_DIM:(h + 1) * HEAD_DIM] * SCALE
        kh = qkc[:, GROUP_W + h * HEAD_DIM:GROUP_W + (h + 1) * HEAD_DIM]
        vaug = jnp.concatenate([v_ref[0, :, h * HEAD_DIM:(h + 1) * HEAD_DIM], ones_col], axis=1)
        smat = _mm_nt(qh, kh) * jnp.exp(dmat - m_t)
        cst = c_ref[h]
        tot = _mm(smat, vaug) + jnp.exp(inter - m_t) * _mm(qh, cst)
        num = tot[:, 0:HEAD_DIM]
        den = tot[:, HEAD_DIM:HEAD_DIM + 1]
        outs.append(num / jnp.maximum(jnp.abs(den), jnp.exp(-m_t)))
        b_last = bcum[L - 1:L, N_HEADS + h:N_HEADS + h + 1]
        d_s = b_last - b_col + i_col
        m_new = jnp.maximum(b_last + m_prev, jnp.max(d_s, axis=0, keepdims=True))
        w_s = jnp.exp(d_s - m_new)
        decay = jnp.exp(b_last + m_prev - m_new)
        c_ref[h] = decay * cst + _mm_tn(kh * w_s, vaug)
        m_ref[h:h + 1, :] = jnp.broadcast_to(m_new, (1, 128))
    o_ref[0] = _sigmoid(og_ref[0]) * jnp.concatenate(outs, axis=1) * _silu(z_ref[0])


def _mixer_b(proj, conv_w, conv_b):
    bsz, seq, _ = proj.shape
    L = B_CHUNK
    blk = lambda name: _new_offset(name) // GROUP_W
    return pl.pallas_call(
        _mixer_b_body,
        grid=(bsz, seq // L),
        in_specs=[pl.BlockSpec((1, L, 2 * GROUP_W), lambda b, c: (b, c, 0)),
                  pl.BlockSpec((1, L, GROUP_W), lambda b, c: (b, c, blk('b_v'))),
                  pl.BlockSpec((1, L, GROUP_W), lambda b, c: (b, c, blk('b_o'))),
                  pl.BlockSpec((1, L, GROUP_W), lambda b, c: (b, c, blk('b_z'))),
                  pl.BlockSpec((1, L, 128), lambda b, c: (b, c, _new_offset('b_if') // 128)),
                  pl.BlockSpec((B_CONV, 2 * GROUP_W), lambda b, c: (0, 0)),
                  pl.BlockSpec((1, 2 * GROUP_W), lambda b, c: (0, 0))],
        out_specs=pl.BlockSpec((1, L, GROUP_W), lambda b, c: (b, c, 0)),
        out_shape=jax.ShapeDtypeStruct((bsz, seq, GROUP_W), F32),
        scratch_shapes=[pltpu.VMEM((8, 2 * GROUP_W), F32),
                        pltpu.VMEM((N_HEADS, HEAD_DIM, 128), F32),
                        pltpu.VMEM((8, 128), F32)],
        compiler_params=_cparams(("arbitrary", "arbitrary")),
        name="mixer_b",
    )(proj, proj, proj, proj, proj, conv_w, conv_b.reshape(1, -1))


def _mixer_c_body(layer, q_ref, f_ref, i_ref, z_ref, lb_ref, gn_ref, o_ref, st_ref):
    c = pl.program_id(1)
    L = C_CHUNK
    nsub = L // C_SUB

    @pl.when(c == 0)
    def _():
        st_ref[...] = jnp.zeros_like(st_ref)

    raw = lb_ref[...]
    ex = jnp.exp(raw - jnp.max(raw, axis=0, keepdims=True))
    if layer == 0:
        lb = jnp.zeros((1, GROUP_W), F32)
    else:
        lb = jnp.sum(ex[1:layer + 1], axis=0, keepdims=True) / jnp.sum(ex, axis=0, keepdims=True)
    f = lb + (1.0 - lb) * _sigmoid(f_ref[0])
    logf = jnp.log(f)
    k = 1.0 - f
    q = _silu(q_ref[0])
    v = i_ref[0]
    tri = (_iota((L, L), 1) <= _iota((L, L), 0)).astype(F32)
    a = _mm_hi(tri, logf)
    head_of_lane = _iota((1, GROUP_W), 1) // HEAD_DIM
    same_head = ((_iota((GROUP_W, GROUP_W), 0) // HEAD_DIM)
                 == (_iota((GROUP_W, GROUP_W), 1) // HEAD_DIM))
    ones_bd = same_head.astype(BF16)

    st = st_ref[...]
    o = _mm_nt(q * jnp.exp(a), st)

    pieces = [o[0:C_SUB]]
    for i in range(1, nsub):
        r0 = i * C_SUB
        piv = a[r0 - 1:r0, :]
        qt = q[r0:r0 + C_SUB] * jnp.exp(a[r0:r0 + C_SUB] - piv)
        kt = k[0:r0] * jnp.exp(piv - a[0:r0])
        kst = jnp.concatenate([jnp.where(head_of_lane == h, kt, 0.0) for h in range(N_HEADS)], axis=0)
        vst = jnp.concatenate([jnp.where(head_of_lane == h, v[0:r0], 0.0) for h in range(N_HEADS)], axis=0)
        pieces.append(o[r0:r0 + C_SUB] + _mm(_mm_nt(qt, kst), vst))
    o = jnp.concatenate(pieces, axis=0)

    a3 = a.reshape(nsub, C_SUB, GROUP_W)
    q3 = q.reshape(nsub, C_SUB, GROUP_W)
    k3 = k.reshape(nsub, C_SUB, GROUP_W)
    v3 = v.reshape(nsub, C_SUB, GROUP_W)
    t_in = _iota((nsub, C_SUB, GROUP_W), 1)
    for s in range(C_SUB):
        e = jnp.exp(jnp.minimum(a3 - a3[:, s:s + 1, :], 0.0))
        p = jnp.where(t_in >= s, q3 * k3[:, s:s + 1, :] * e, 0.0)
        w = _mm_split(p.reshape(L, GROUP_W), ones_bd)
        o = o + w * jnp.broadcast_to(v3[:, s:s + 1, :], (nsub, C_SUB, GROUP_W)).reshape(L, GROUP_W)

    a_last = a[L - 1:L, :]
    kdec = k * jnp.exp(a_last - a)
    st_ref[...] = st * jnp.exp(a_last) + jnp.where(same_head, _mm_tn(v, kdec), 0.0)

    ms = _mm_split(o * o, ones_bd) * (1.0 / HEAD_DIM)
    o = o * lax.rsqrt(ms + C_EPS) * gn_ref[...]
    o_ref[0] = o * _silu(z_ref[0])


def _mixer_c(proj, c_lb, norm_g, layer):
    bsz, seq, _ = proj.shape
    L = C_CHUNK
    blk = lambda name: _new_offset(name) // GROUP_W
    spec = lambda name: pl.BlockSpec((1, L, GROUP_W), lambda b, c: (b, c, blk(name)))
    vec = pl.BlockSpec((1, GROUP_W), lambda b, c: (0, 0))
    return pl.pallas_call(
        functools.partial(_mixer_c_body, layer),
        grid=(bsz, seq // L),
        in_specs=[spec('c_q'), spec('c_f'), spec('c_i'), spec('c_z'),
                  pl.BlockSpec((DEPTH, GROUP_W), lambda b, c: (0, 0)), vec],
        out_specs=pl.BlockSpec((1, L, GROUP_W), lambda b, c: (b, c, 0)),
        out_shape=jax.ShapeDtypeStruct((bsz, seq, GROUP_W), F32),
        scratch_shapes=[pltpu.VMEM((GROUP_W, GROUP_W), F32)],
        compiler_params=_cparams(("arbitrary", "arbitrary")),
        name="mixer_c",
    )(proj, proj, proj, proj, c_lb, norm_g.reshape(1, -1))


N_CMP_PAD = 128
KT = 512
PRO_ROWS = 512
LOG2E = 1.4426950408889634


def _mixer_d_body(q_ref, z_ref, kv_ref, cin_ref, g_ref, cos_ref, up_ref, dn_ref, kcos_ref, kup_ref, kdn_ref,
                  ccos_ref, cup_ref, cdn_ref, w1_ref, pe_ref, w2_ref, ov_ref, o_ref,
                  k_ref, vts_ref, vtw_ref, kc_ref, vct_ref, sel_ref):
    i = pl.program_id(1)
    seq = kv_ref.shape[1]
    half_cmp = CMP_LEN // 2

    @pl.when(i == 0)
    def _():
        def rope_rows(c, carry):
            for u in range(PRO_ROWS // KT):
                t0 = c * (PRO_ROWS // KT) + u
                rows = pl.ds(pl.multiple_of(t0 * KT, KT), KT)
                kvs = _rope(kv_ref[0, rows, 0:128], kcos_ref[rows, :], kup_ref[rows, :], kdn_ref[rows, :])
                kvw = _rope(kv_ref[0, rows, 128:256], kcos_ref[rows, :], kup_ref[rows, :], kdn_ref[rows, :])
                k_ref[0, rows, :] = kvs[:, 0:HEAD_DIM].astype(BF16)
                k_ref[1, rows, :] = kvw[:, 0:HEAD_DIM].astype(BF16)
                vts_ref[t0] = kvs.T[HEAD_DIM:2 * HEAD_DIM, :].astype(BF16)
                vwt = kvw.T[HEAD_DIM:2 * HEAD_DIM, :].astype(BF16)
                for w in range(KT // QBLK):
                    vtw_ref[t0 * (KT // QBLK) + w] = vwt[:, w * QBLK:(w + 1) * QBLK]
            return carry
        lax.fori_loop(0, seq // PRO_ROWS, rope_rows, 0)
        u0 = jnp.zeros((N_CMP_PAD, 128), F32)
        u1 = jnp.zeros((N_CMP_PAD, 128), F32)
        for r in range(half_cmp):
            xr = cin_ref[0, pl.ds(r, N_CMP_PAD, stride=CMP_STRIDE), :]
            u0 = u0 + _mm_hi(xr + pe_ref[r:r + 1, :], w1_ref[r])
            u1 = u1 + _mm_hi(xr + pe_ref[half_cmp + r:half_cmp + r + 1, :], w1_ref[half_cmp + r])
        pre = u0 + pltpu.roll(u1, N_CMP_PAD - 1, axis=0)
        cv = _rope(_mm_hi(_silu(pre), w2_ref[...]), ccos_ref[...], cup_ref[...], cdn_ref[...])
        kc_ref[...] = cv[:, 0:HEAD_DIM]
        vct_ref[...] = cv.T[HEAD_DIM:2 * HEAD_DIM, :].astype(BF16)

    q0 = pl.multiple_of(i * QBLK, QBLK)
    qrows = pl.ds(q0, QBLK)
    qr = _rope(q_ref[0], cos_ref[qrows, :], up_ref[qrows, :], dn_ref[qrows, :]) * (SCALE * LOG2E)
    qt = qr.T
    qst = jnp.concatenate([qt[h * HEAD_DIM:(h + 1) * HEAD_DIM] for h in range(N_HEADS)], axis=1)
    qst_bf = qst.astype(BF16)
    cols4 = N_HEADS * QBLK
    qpos = q0 + _iota((1, QBLK), 1)

    def lanes4(x):
        return jnp.concatenate([x] * N_HEADS, axis=1)

    n_win = NSA_WINDOW // QBLK + 1
    span = n_win * QBLK
    kt0 = jnp.maximum(i - NSA_WINDOW // QBLK, 0)
    k0 = pl.multiple_of(kt0 * QBLK, QBLK)
    rel = qpos - (k0 + _iota((span, 1), 0))
    in_band = lax.bitcast_convert_type(rel, jnp.uint32) < NSA_WINDOW
    s = jnp.dot(k_ref[1, pl.ds(k0, span), :], qst_bf, preferred_element_type=F32)
    s = s + lanes4(jnp.where(in_band, 0.0, NEG))
    p = jnp.exp2(s - jnp.max(s, axis=0, keepdims=True))
    l_w = jnp.sum(p, axis=0, keepdims=True)
    p = p.astype(BF16)
    o_win = jnp.zeros((HEAD_DIM, cols4), F32)
    for c in range(n_win):
        o_win = o_win + jnp.dot(vtw_ref[kt0 + c], p[c * QBLK:(c + 1) * QBLK], preferred_element_type=F32)
    o_win = o_win / l_w

    sc = _mm_hi(kc_ref[...], qst)
    ends = _iota((N_CMP_PAD, 1), 0) * CMP_STRIDE + (CMP_LEN - 1)
    scm = sc + lanes4(jnp.where(ends <= qpos, 0.0, NEG))
    e = jnp.exp2(scm - jnp.max(scm, axis=0, keepdims=True))
    l = jnp.sum(e, axis=0, keepdims=True)
    any_valid = lanes4(jnp.where(qpos >= CMP_LEN - 1, 1.0, 0.0))
    pc = e * (any_valid / l)
    o_cmp = jnp.dot(vct_ref[...], pc.astype(BF16), preferred_element_type=F32)
    psum = pc[:, 0:QBLK] + pc[:, QBLK:2 * QBLK] + pc[:, 2 * QBLK:3 * QBLK] + pc[:, 3 * QBLK:4 * QBLK]

    n_slc = seq // SLC_LEN
    imp = _mm_hi(ov_ref[...], psum)
    jl = _iota((n_slc, 1), 0)
    cur = jnp.right_shift(qpos, SLC_LEN.bit_length() - 1)
    imp = jnp.where(jl == 0, FORCE_SCORE, imp)
    imp = jnp.where(jl == cur, FORCE_SCORE, imp)
    imp = jnp.where(jl == cur - 1, FORCE_SCORE, imp)
    imp = jnp.where(jl <= cur, imp, NEG)
    rank = jnp.zeros((n_slc, QBLK), F32)
    for jp in range(n_slc):
        cj = imp[jp:jp + 1, :]
        tie = jnp.where(jl > jp, 1.0, 0.0)
        rank = rank + jnp.where(cj > imp, 1.0, jnp.where(cj == imp, tie, 0.0))
    sel_bias = jnp.where(rank < N_SEL, 0.0, NEG)
    blocks_per_tile = KT // SLC_LEN
    for t in range(n_slc // blocks_per_tile):
        sel_ref[t] = sel_bias[t * blocks_per_tile:(t + 1) * blocks_per_tile, :]

    n_pair = 1
    pw = cols4 // n_pair

    def sel_tile(kt, carry):
        kbase = pl.multiple_of(kt * KT, KT)
        picks = sel_ref[kt]
        chosen = jnp.concatenate([jnp.broadcast_to(picks[j:j + 1, :], (SLC_LEN, QBLK))
                                  for j in range(blocks_per_tile)], axis=0)
        bias = jnp.where((kbase + _iota((KT, 1), 0)) <= qpos, chosen, NEG)
        bias = jnp.concatenate([bias] * (pw // QBLK), axis=1)
        kb = k_ref[0, pl.ds(kbase, KT), :]
        vb = vts_ref[kt]
        out = []
        for c in range(n_pair):
            m_old, l_old, acc = carry[c]
            s = jnp.dot(kb, qst_bf[:, c * pw:(c + 1) * pw], preferred_element_type=F32) + bias
            m_new = jnp.maximum(m_old, jnp.max(s, axis=0, keepdims=True))
            p = jnp.exp2(s - m_new)
            alpha = jnp.exp2(m_old - m_new)
            l_new = alpha * l_old + jnp.sum(p, axis=0, keepdims=True)
            acc_new = alpha * acc + jnp.dot(vb, p.astype(BF16), preferred_element_type=F32)
            out.append((m_new, l_new, acc_new))
        return tuple(out)

    init = tuple((jnp.full((1, pw), NEG, F32), jnp.zeros((1, pw), F32), jnp.zeros((HEAD_DIM, pw), F32))
                 for _ in range(n_pair))
    n_tiles = (i + KT // QBLK) // (KT // QBLK)
    fin = lax.fori_loop(0, n_tiles, sel_tile, init)
    o_slc = jnp.concatenate([acc / l_c for _, l_c, acc in fin], axis=1)

    gate = _sigmoid(g_ref[0]).T
    outs = []
    for h in range(N_HEADS):
        cs = slice(h * QBLK, (h + 1) * QBLK)
        outs.append(gate[3 * h:3 * h + 1, :] * o_cmp[:, cs] + gate[3 * h + 1:3 * h + 2, :] * o_slc[:, cs]
                    + gate[3 * h + 2:3 * h + 3, :] * o_win[:, cs])
    o_ref[0] = jnp.concatenate(outs, axis=0).T * _silu(z_ref[0])


def _mixer_d(proj, qtabs, ktabs, ctabs, w1bd, pe2, w2bd, overlap):
    bsz, seq, _ = proj.shape
    blk = lambda name: _new_offset(name) // GROUP_W
    full = lambda shape: pl.BlockSpec(shape, lambda b, i: (0,) * len(shape))
    return pl.pallas_call(
        _mixer_d_body,
        grid=(bsz, seq // QBLK),
        in_specs=[pl.BlockSpec((1, QBLK, GROUP_W), lambda b, i: (b, i, blk('d_q'))),
                  pl.BlockSpec((1, QBLK, GROUP_W), lambda b, i: (b, i, blk('d_z'))),
                  pl.BlockSpec((1, seq, GROUP_W), lambda b, i: (b, 0, blk('d_kvsw'))),
                  pl.BlockSpec((1, seq, 128), lambda b, i: (b, 0, _new_offset('d_kvc') // 128)),
                  pl.BlockSpec((1, QBLK, 128), lambda b, i: (b, i, _new_offset('d_g') // 128)),
                  full((seq, GROUP_W)), full((seq, GROUP_W)), full((seq, GROUP_W)),
                  full((seq, 128)), full((seq, 128)), full((seq, 128)),
                  full((N_CMP_PAD, 128)), full((N_CMP_PAD, 128)), full((N_CMP_PAD, 128)),
                  full((CMP_LEN, 128, 128)), full((CMP_LEN, 128)), full((128, 128)),
                  full((seq // SLC_LEN, N_CMP_PAD))],
        out_specs=pl.BlockSpec((1, QBLK, GROUP_W), lambda b, i: (b, i, 0)),
        out_shape=jax.ShapeDtypeStruct((bsz, seq, GROUP_W), F32),
        scratch_shapes=[pltpu.VMEM((2, seq, HEAD_DIM), BF16),
                        pltpu.VMEM((seq // KT, HEAD_DIM, KT), BF16),
                        pltpu.VMEM((seq // QBLK, HEAD_DIM, QBLK), BF16),
                        pltpu.VMEM((N_CMP_PAD, HEAD_DIM), F32),
                        pltpu.VMEM((HEAD_DIM, N_CMP_PAD), BF16),
                        pltpu.VMEM((seq // KT, KT // SLC_LEN, QBLK), F32)],
        compiler_params=_cparams(("arbitrary", "arbitrary")),
        name="mixer_d",
    )(proj, proj, proj, proj, proj, *qtabs, *ktabs, *ctabs, w1bd, pe2, w2bd, overlap)


def _cmp_params(pe, w1, w2):
    w1 = w1.reshape(2, CMP_LEN, HEAD_DIM, HEAD_DIM)
    z = jnp.zeros((CMP_LEN, HEAD_DIM, HEAD_DIM), F32)
    w1bd = jnp.concatenate([jnp.concatenate([w1[0], z], axis=2), jnp.concatenate([z, w1[1]], axis=2)], axis=1)
    z2 = jnp.zeros((HEAD_DIM, HEAD_DIM), F32)
    w2bd = jnp.concatenate([jnp.concatenate([w2[0], z2], axis=1), jnp.concatenate([z2, w2[1]], axis=1)], axis=0)
    pe2 = jnp.concatenate([pe[0], pe[1]], axis=1)
    return w1bd, pe2, w2bd


def _overlap_matrix(seq):
    n_cmp = (seq - CMP_LEN) // CMP_STRIDE + 1
    starts = np.arange(N_CMP_PAD) * CMP_STRIDE
    blk = np.arange(seq // SLC_LEN)
    ov = ((starts[None, :] < (blk[:, None] + 1) * SLC_LEN) & (starts[None, :] + CMP_LEN > blk[:, None] * SLC_LEN))
    ov = ov & (np.arange(N_CMP_PAD)[None, :] < n_cmp)
    return jnp.asarray(ov.astype(np.float32))


def kernel(x, ln0_g, ln0_b, w_in, b_in, a_sinks, b_conv_w, b_conv_b, c_lb, c_norm_g, d_cmp_pe, d_cmp_w1, d_cmp_w2,
           w_out, ln_g, ln_b):
    bsz, seq, d = x.shape
    pos = jnp.arange(seq)
    qtabs = _rope_tables(pos, GROUP_W, 128)
    ktabs = _rope_tables(pos, 128, HEAD_DIM)
    ctabs = _rope_tables(jnp.arange(N_CMP_PAD) * CMP_STRIDE + (CMP_LEN - 1), 128, HEAD_DIM)
    overlap = _overlap_matrix(seq)

    h = _layer_norm(x.reshape(bsz * seq, d), ln0_g, ln0_b)
    for l in range(DEPTH):
        w_p = _permute_cols(w_in[l]).astype(BF16)
        b_p = _permute_cols(b_in[l])
        proj = _in_proj(h, w_p, b_p).reshape(bsz, seq, N_PROJ)
        y_a = _mixer_a(proj, a_sinks[l], qtabs)
        y_b = _mixer_b(proj, b_conv_w[l], b_conv_b[l])
        y_c = _mixer_c(proj, c_lb, c_norm_g[l], l)
        w1bd, pe2, w2bd = _cmp_params(d_cmp_pe[l], d_cmp_w1[l], d_cmp_w2[l])
        y_d = _mixer_d(proj, qtabs, ktabs, ctabs, w1bd, pe2, w2bd, overlap)
        h = _out_proj((y_a, y_b, y_c, y_d), h, w_out[l].astype(BF16), ln_g[l], ln_b[l])
    return h.reshape(bsz, seq, d)
```

```python
import functools

import numpy as np
import jax
import jax.numpy as jnp
from jax import lax
from jax.experimental import pallas as pl
from jax.experimental.pallas import tpu as pltpu

F32 = jnp.float32
BF16 = jnp.bfloat16

D_MODEL = 1024
DEPTH = 2
HEAD_DIM = 64
HALF = HEAD_DIM // 2
N_HEADS = 4
GROUP_W = N_HEADS * HEAD_DIM
ROPE_THETA = 10000.0
NEG = -1e30
LN_EPS = 1e-5
A_WINDOW = 128
B_CONV = 4
C_CHUNK = 64
C_SUB = 16
C_STEP = 128
C_EPS = 1e-6
CMP_LEN = 32
CMP_STRIDE = 16
SLC_LEN = 64
N_SEL = 8
NSA_WINDOW = 512
FORCE_SCORE = 1e6
DN_ALPHA = (2.0 * DEPTH) ** 0.25
QBLK = 128
SCALE = HEAD_DIM ** -0.5

VMEM_LIMIT = 56 * 1024 * 1024

LANES = 128

ORIG_SPLITS = (
    ('a_q', 256), ('a_k', 128), ('a_v', 128), ('a_z', 256),
    ('b_q', 256), ('b_k', 256), ('b_v', 256), ('b_if', 8), ('b_o', 256), ('b_z', 256),
    ('c_q', 256), ('c_f', 256), ('c_i', 256), ('c_z', 256),
    ('d_q', 256), ('d_kvc', 128), ('d_ksv', 128), ('d_kwv', 128), ('d_g', 12), ('d_z', 256),
)
N_COLS = sum(w for _, w in ORIG_SPLITS)
NEW_LAYOUT = (
    ('a_q', 256), ('a_k', 128), ('a_v', 128), ('a_z', 256), ('b_q', 256), ('b_k', 256), ('b_v', 256),
    ('b_o', 256), ('b_z', 256), ('c_q', 256), ('c_f', 256), ('c_i', 256), ('c_z', 256),
    ('d_q', 256), ('d_kvc', 128), ('d_ksv', 128), ('d_kwv', 128), ('b_if', 128), ('d_z', 256), ('d_g', 128),
)
N_PROJ = sum(w for _, w in NEW_LAYOUT)


def _offset(layout, name):
    off = 0
    for n, w in layout:
        if n == name:
            return off
        off += w
    raise KeyError(name)


def _new_offset(name):
    return _offset(NEW_LAYOUT, name)


def _permuted_cols(load):
    sizes = dict(ORIG_SPLITS)
    for name, width in NEW_LAYOUT:
        old, size, new = _offset(ORIG_SPLITS, name), sizes[name], _new_offset(name)
        lo = old // LANES * LANES
        if size == width:
            hi = min(-(-(old + size) // LANES) * LANES, N_COLS)
            yield new, load(lo, hi)[..., old - lo:old - lo + size]
        else:
            tile = load(lo, lo + LANES)
            if old != lo:
                tile = pltpu.roll(tile, LANES - (old - lo), axis=tile.ndim - 1)
            yield new, jnp.where(_iota(tile.shape, tile.ndim - 1) < size, tile, 0.0)


def _prep_body(w_ref, b_ref, wo_ref, bo_ref):
    for new, cols in _permuted_cols(lambda lo, hi: w_ref[0, :, lo:hi]):
        wo_ref[:, new:new + cols.shape[-1]] = cols.astype(BF16)
    for new, cols in _permuted_cols(lambda lo, hi: b_ref[0, :, lo:hi]):
        bo_ref[:, new:new + cols.shape[-1]] = cols


def _prep_in_weights(w_in, b_in, layer, rows=128):
    depth, d, n = w_in.shape
    return pl.pallas_call(
        _prep_body,
        grid=(d // rows,),
        in_specs=[pl.BlockSpec((1, rows, n), lambda r: (layer, r, 0)),
                  pl.BlockSpec((1, 1, n), lambda r: (layer, 0, 0))],
        out_specs=[pl.BlockSpec((rows, N_PROJ), lambda r: (r, 0)),
                   pl.BlockSpec((1, N_PROJ), lambda r: (0, 0))],
        out_shape=[jax.ShapeDtypeStruct((d, N_PROJ), BF16), jax.ShapeDtypeStruct((1, N_PROJ), F32)],
        compiler_params=_cparams(("arbitrary",)),
        name="prep_in_weights",
    )(w_in, b_in.reshape(depth, 1, n))


def _mm(a, b):
    return jnp.dot(a.astype(BF16), b.astype(BF16), preferred_element_type=F32)


def _mm_nt(a, b):
    return lax.dot_general(a.astype(BF16), b.astype(BF16), (((1,), (1,)), ((), ())),
                           preferred_element_type=F32)


def _mm_tn(a, b):
    return lax.dot_general(a.astype(BF16), b.astype(BF16), (((0,), (0,)), ((), ())),
                           preferred_element_type=F32)


def _mm_hi(a, b):
    return jnp.dot(a, b, preferred_element_type=F32, precision=lax.Precision.HIGHEST)


def _mm_hi_nt(a, b):
    return lax.dot_general(a, b, (((1,), (1,)), ((), ())), preferred_element_type=F32,
                           precision=lax.Precision.HIGHEST)


def _mm_split(a, b_exact):
    hi = a.astype(BF16)
    lo = (a - hi.astype(F32)).astype(BF16)
    return (jnp.dot(hi, b_exact, preferred_element_type=F32)
            + jnp.dot(lo, b_exact, preferred_element_type=F32))


def _sigmoid(x):
    return 1.0 / (1.0 + jnp.exp(-x))


def _silu(x):
    return x * _sigmoid(x)


def _log_sigmoid(x):
    return jnp.minimum(x, 0.0) - jnp.log1p(jnp.exp(-jnp.abs(x)))


def _rope(x, cos, sin_up, sin_dn):
    w = x.shape[-1]
    up = pltpu.roll(x, w - HALF, axis=1)
    dn = pltpu.roll(x, HALF, axis=1)
    return x * cos + up * sin_up + dn * sin_dn


def _rope_tables(pos, width, rot_lanes):
    inv = ROPE_THETA ** (-jnp.arange(HALF, dtype=F32) / HALF)
    ang = pos.astype(F32)[:, None] * inv[None, :]
    cos, sin = jnp.cos(ang), jnp.sin(ang)
    zero, one = jnp.zeros_like(sin), jnp.ones_like(cos)
    cos_h = jnp.concatenate([cos, cos], axis=1)
    up_h = jnp.concatenate([-sin, zero], axis=1)
    dn_h = jnp.concatenate([zero, sin], axis=1)
    id_c = jnp.concatenate([one, one], axis=1)
    id_s = jnp.concatenate([zero, zero], axis=1)
    cs, us, ds = [], [], []
    for l0 in range(0, width, HEAD_DIM):
        rot = (l0 % 128) < rot_lanes
        cs.append(cos_h if rot else id_c)
        us.append(up_h if rot else id_s)
        ds.append(dn_h if rot else id_s)
    return jnp.concatenate(cs, axis=1), jnp.concatenate(us, axis=1), jnp.concatenate(ds, axis=1)


def _iota(shape, dim):
    return lax.broadcasted_iota(jnp.int32, shape, dim)


def _cparams(sem):
    return pltpu.CompilerParams(dimension_semantics=sem, vmem_limit_bytes=VMEM_LIMIT)


def _ln_body(x_ref, g_ref, b_ref, o_ref):
    x = x_ref[...]
    mu = jnp.mean(x, axis=-1, keepdims=True)
    xc = x - mu
    var = jnp.mean(xc * xc, axis=-1, keepdims=True)
    o_ref[...] = xc * lax.rsqrt(var + LN_EPS) * g_ref[...] + b_ref[...]


def _layer_norm(x2, g, b, tm=512):
    m, d = x2.shape
    return pl.pallas_call(
        _ln_body,
        grid=(m // tm,),
        in_specs=[pl.BlockSpec((tm, d), lambda i: (i, 0)),
                  pl.BlockSpec((1, d), lambda i: (0, 0)),
                  pl.BlockSpec((1, d), lambda i: (0, 0))],
        out_specs=pl.BlockSpec((tm, d), lambda i: (i, 0)),
        out_shape=jax.ShapeDtypeStruct((m, d), F32),
        compiler_params=_cparams(("parallel",)),
        name="layer_norm",
    )(x2, g.reshape(1, d), b.reshape(1, d))


def _proj_body(x_ref, w_ref, b_ref, o_ref):
    o_ref[...] = jnp.dot(x_ref[...].astype(BF16), w_ref[...], preferred_element_type=F32) + b_ref[...]


def _in_proj(h2, w_bf, b, tm=256):
    m, d = h2.shape
    n = w_bf.shape[1]
    return pl.pallas_call(
        _proj_body,
        grid=(m // tm,),
        in_specs=[pl.BlockSpec((tm, d), lambda i: (i, 0)),
                  pl.BlockSpec((d, n), lambda i: (0, 0)),
                  pl.BlockSpec((1, n), lambda i: (0, 0))],
        out_specs=pl.BlockSpec((tm, n), lambda i: (i, 0)),
        out_shape=jax.ShapeDtypeStruct((m, n), F32),
        compiler_params=_cparams(("parallel",)),
        name="in_proj",
    )(h2, w_bf, b.reshape(1, n))


def _out_body(ya_ref, yb_ref, yc_ref, yd_ref, h_ref, w_ref, g_ref, b_ref, o_ref):
    y = _mm(ya_ref[...], w_ref[0:GROUP_W, :])
    y += _mm(yb_ref[...], w_ref[GROUP_W:2 * GROUP_W, :])
    y += _mm(yc_ref[...], w_ref[2 * GROUP_W:3 * GROUP_W, :])
    y += _mm(yd_ref[...], w_ref[3 * GROUP_W:4 * GROUP_W, :])
    r = DN_ALPHA * h_ref[...] + y
    mu = jnp.mean(r, axis=-1, keepdims=True)
    rc = r - mu
    var = jnp.mean(rc * rc, axis=-1, keepdims=True)
    o_ref[...] = rc * lax.rsqrt(var + LN_EPS) * g_ref[...] + b_ref[...]


def _out_proj(ys, h2, w_bf, g, b, tm=512):
    m, d = h2.shape
    yspec = pl.BlockSpec((tm, GROUP_W), lambda i: (i, 0))
    return pl.pallas_call(
        _out_body,
        grid=(m // tm,),
        in_specs=[yspec, yspec, yspec, yspec,
                  pl.BlockSpec((tm, d), lambda i: (i, 0)),
                  pl.BlockSpec((4 * GROUP_W, d), lambda i: (0, 0)),
                  pl.BlockSpec((1, d), lambda i: (0, 0)),
                  pl.BlockSpec((1, d), lambda i: (0, 0))],
        out_specs=pl.BlockSpec((tm, d), lambda i: (i, 0)),
        out_shape=jax.ShapeDtypeStruct((m, d), F32),
        compiler_params=_cparams(("parallel",)),
        name="out_proj",
    )(*[y.reshape(m, GROUP_W) for y in ys], h2, w_bf, g.reshape(1, d), b.reshape(1, d))


def _mixer_a_body(sink_ref, q_ref, kv_ref, z_ref, cos_ref, up_ref, dn_ref, o_ref, kr_ref):
    i = pl.program_id(1)
    seq = kv_ref.shape[1]

    @pl.when(i == 0)
    def _():
        def rope_rows(c, carry):
            rows = pl.ds(pl.multiple_of(c * 256, 256), 256)
            kr_ref[rows, :] = _rope(kv_ref[0, rows, 0:128], cos_ref[rows, 0:128], up_ref[rows, 0:128],
                                    dn_ref[rows, 0:128])
            return carry
        lax.fori_loop(0, seq // 256, rope_rows, 0)

    q0 = pl.multiple_of(i * QBLK, QBLK)
    qrows = pl.ds(q0, QBLK)
    qr = _rope(q_ref[0], cos_ref[qrows, :], up_ref[qrows, :], dn_ref[qrows, :]) * SCALE
    k0 = pl.multiple_of(jnp.maximum(i - 1, 0) * QBLK, QBLK)
    krows = pl.ds(k0, 2 * QBLK)
    kb = kr_ref[krows, :]
    vb = kv_ref[0, krows, 128:256]
    rel = (q0 + _iota((QBLK, 2 * QBLK), 0)) - (k0 + _iota((QBLK, 2 * QBLK), 1))
    band = (rel >= 0) & (rel < A_WINDOW)
    outs = []
    for h in range(N_HEADS):
        g = h // 2
        sink = sink_ref[h]
        s = _mm_nt(qr[:, h * HEAD_DIM:(h + 1) * HEAD_DIM], kb[:, g * HEAD_DIM:(g + 1) * HEAD_DIM])
        s = jnp.where(band, s, NEG)
        m = jnp.maximum(jnp.max(s, axis=-1, keepdims=True), sink)
        p = jnp.exp(s - m)
        l = jnp.sum(p, axis=-1, keepdims=True) + jnp.exp(sink - m)
        outs.append(_mm(p, vb[:, g * HEAD_DIM:(g + 1) * HEAD_DIM]) / l)
    o_ref[0] = jnp.concatenate(outs, axis=1) * _silu(z_ref[0])


def _mixer_a(proj, sinks, tabs):
    bsz, seq, _ = proj.shape
    cos, up, dn = tabs
    blk = lambda name: _new_offset(name) // GROUP_W
    tspec = pl.BlockSpec((seq, GROUP_W), lambda b, i: (0, 0))
    return pl.pallas_call(
        _mixer_a_body,
        grid=(bsz, seq // QBLK),
        in_specs=[pl.BlockSpec(memory_space=pltpu.SMEM),
                  pl.BlockSpec((1, QBLK, GROUP_W), lambda b, i: (b, i, blk('a_q'))),
                  pl.BlockSpec((1, seq, GROUP_W), lambda b, i: (b, 0, blk('a_k'))),
                  pl.BlockSpec((1, QBLK, GROUP_W), lambda b, i: (b, i, blk('a_z'))),
                  tspec, tspec, tspec],
        out_specs=pl.BlockSpec((1, QBLK, GROUP_W), lambda b, i: (b, i, 0)),
        out_shape=jax.ShapeDtypeStruct((bsz, seq, GROUP_W), F32),
        scratch_shapes=[pltpu.VMEM((seq, 128), F32)],
        compiler_params=_cparams(("arbitrary", "arbitrary")),
        name="mixer_a",
    )(sinks, proj, proj, proj, cos, up, dn)


B_CHUNK = 128
B_AUG = HEAD_DIM + 16


def _conv_silu(x, prev, w, b):
    row8 = _iota((8, x.shape[1]), 0)
    acc = b + w[B_CONV - 1:B_CONV, :] * x
    for s in range(1, B_CONV):
        xs = pltpu.roll(x, s, axis=0)
        ps = pltpu.roll(prev, s, axis=0)
        shifted = jnp.concatenate([jnp.where(row8 < s, ps, xs[0:8]), xs[8:]], axis=0)
        acc = acc + w[B_CONV - 1 - s:B_CONV - s, :] * shifted
    return _silu(acc)


def _mixer_b_body(xq_ref, xk_ref, v_ref, og_ref, z_ref, if_ref, cw_ref, cb_ref, o_ref, xprev_ref, c_ref, m_ref):
    c = pl.program_id(1)
    L = B_CHUNK

    @pl.when(c == 0)
    def _():
        xprev_ref[...] = jnp.zeros_like(xprev_ref)
        c_ref[...] = jnp.zeros_like(c_ref)
        m_ref[...] = jnp.zeros_like(m_ref)

    xq = xq_ref[0]
    xk = xk_ref[0]
    qc = _conv_silu(xq, xprev_ref[:, 0:GROUP_W], cw_ref[:, 0:GROUP_W], cb_ref[:, 0:GROUP_W]) * SCALE
    kc = _conv_silu(xk, xprev_ref[:, GROUP_W:], cw_ref[:, GROUP_W:], cb_ref[:, GROUP_W:])
    xprev_ref[:, 0:GROUP_W] = xq[L - 8:L, :]
    xprev_ref[:, GROUP_W:] = xk[L - 8:L, :]
    q_t = qc.T
    v_t = v_ref[0].T

    gates = if_ref[0]
    logf = _log_sigmoid(gates)
    tri = (_iota((L, L), 1) <= _iota((L, L), 0)).astype(F32)
    bcum = _mm_hi(tri, logf)
    gates_t = gates.T
    bcum_t = bcum.T
    key_first = _iota((L, L), 0) <= _iota((L, L), 1)
    half_of_lane = _iota((1, 128), 1) // HEAD_DIM
    ones_rows = jnp.ones((B_AUG - HEAD_DIM, L), F32)
    outs = []
    for h in range(N_HEADS):
        pair = slice((h // 2) * 128, (h // 2 + 1) * 128)
        own = half_of_lane == (h % 2)
        k_pair = kc[:, pair]
        q_pair_t = q_t[pair, :]
        i_row = gates_t[h:h + 1, :]
        b_row = bcum_t[N_HEADS + h:N_HEADS + h + 1, :]
        c_col = gates[:, h:h + 1] - bcum[:, N_HEADS + h:N_HEADS + h + 1]
        m_prev = m_ref[h:h + 1, 0:1]
        dmat = jnp.where(key_first, b_row + c_col, NEG)
        inter = b_row + m_prev
        m_t = jnp.maximum(jnp.max(dmat, axis=0, keepdims=True), inter)
        smat = _mm(jnp.where(own, k_pair, 0.0), q_pair_t) * jnp.exp(dmat - m_t)
        vaug_t = jnp.concatenate([v_t[h * HEAD_DIM:(h + 1) * HEAD_DIM, :], ones_rows], axis=0)
        cst = c_ref[h]
        tot = _mm(vaug_t, smat) + jnp.exp(inter - m_t) * _mm(cst, q_pair_t)
        num = tot[0:HEAD_DIM, :]
        den = tot[HEAD_DIM:HEAD_DIM + 1, :]
        outs.append(num / jnp.maximum(jnp.abs(den), jnp.exp(-m_t)))
        b_last = b_row[:, L - 1:L]
        d_row = b_last - b_row + i_row
        m_new = jnp.maximum(b_last + m_prev, jnp.max(d_row, axis=1, keepdims=True))
        w_row = jnp.exp(d_row - m_new)
        decay = jnp.exp(b_last + m_prev - m_new)
        c_ref[h] = decay * cst + jnp.where(own, _mm(vaug_t * w_row, k_pair), 0.0)
        m_ref[h:h + 1, :] = jnp.broadcast_to(m_new, (1, 128))
    o_ref[0] = _sigmoid(og_ref[0]) * jnp.concatenate(outs, axis=0).T * _silu(z_ref[0])


def _mixer_b(proj, conv_w, conv_b):
    bsz, seq, _ = proj.shape
    L = B_CHUNK
    blk = lambda name: _new_offset(name) // GROUP_W
    return pl.pallas_call(
        _mixer_b_body,
        grid=(bsz, seq // L),
        in_specs=[pl.BlockSpec((1, L, GROUP_W), lambda b, c: (b, c, blk('b_q'))),
                  pl.BlockSpec((1, L, GROUP_W), lambda b, c: (b, c, blk('b_k'))),
                  pl.BlockSpec((1, L, GROUP_W), lambda b, c: (b, c, blk('b_v'))),
                  pl.BlockSpec((1, L, GROUP_W), lambda b, c: (b, c, blk('b_o'))),
                  pl.BlockSpec((1, L, GROUP_W), lambda b, c: (b, c, blk('b_z'))),
                  pl.BlockSpec((1, L, 128), lambda b, c: (b, c, _new_offset('b_if') // 128)),
                  pl.BlockSpec((B_CONV, 2 * GROUP_W), lambda b, c: (0, 0)),
                  pl.BlockSpec((1, 2 * GROUP_W), lambda b, c: (0, 0))],
        out_specs=pl.BlockSpec((1, L, GROUP_W), lambda b, c: (b, c, 0)),
        out_shape=jax.ShapeDtypeStruct((bsz, seq, GROUP_W), F32),
        scratch_shapes=[pltpu.VMEM((8, 2 * GROUP_W), F32),
                        pltpu.VMEM((N_HEADS, B_AUG, 128), F32),
                        pltpu.VMEM((8, 128), F32)],
        compiler_params=_cparams(("arbitrary", "arbitrary")),
        name="mixer_b",
    )(proj, proj, proj, proj, proj, proj, conv_w, conv_b.reshape(1, -1))


def _mixer_c_body(layer, q_ref, f_ref, i_ref, z_ref, lb_ref, gn_ref, tri_ref, o_ref, st_ref):
    c = pl.program_id(1)
    L = C_CHUNK
    nsub = L // C_SUB

    @pl.when(c == 0)
    def _():
        st_ref[...] = jnp.zeros_like(st_ref)

    raw = lb_ref[...]
    ex = jnp.exp(raw - jnp.max(raw, axis=0, keepdims=True))
    if layer == 0:
        lb = jnp.zeros((1, GROUP_W), F32)
    else:
        lb = jnp.sum(ex[1:layer + 1], axis=0, keepdims=True) / jnp.sum(ex, axis=0, keepdims=True)
    tri = (_iota((L, L), 1) <= _iota((L, L), 0)).astype(F32)
    head_of_lane = _iota((1, GROUP_W), 1) // HEAD_DIM
    same_head = ((_iota((GROUP_W, GROUP_W), 0) // HEAD_DIM)
                 == (_iota((GROUP_W, GROUP_W), 1) // HEAD_DIM))
    ones_bd = same_head.astype(BF16)

    def chunk(rows, st):
        f = lb + (1.0 - lb) * _sigmoid(f_ref[0, rows, :])
        logf = jnp.log(f)
        k = 1.0 - f
        q = _silu(q_ref[0, rows, :])
        v = i_ref[0, rows, :]
        a = _mm_hi(tri, logf) * LOG2E
        o = _mm_nt(q * jnp.exp2(a), st)

        pieces = [o[0:C_SUB]]
        for i in range(1, nsub):
            r0 = i * C_SUB
            piv = a[r0 - 1:r0, :]
            qt = q[r0:r0 + C_SUB] * jnp.exp2(a[r0:r0 + C_SUB] - piv)
            kt = k[0:r0] * jnp.exp2(piv - a[0:r0])
            qst = jnp.concatenate([jnp.where(head_of_lane == h, qt, 0.0) for h in range(N_HEADS)], axis=0)
            r = _mm(_mm_nt(qst, kt), v[0:r0])
            acc = o[r0:r0 + C_SUB]
            for h in range(N_HEADS):
                acc = acc + jnp.where(head_of_lane == h, r[h * C_SUB:(h + 1) * C_SUB], 0.0)
            pieces.append(acc)
        o = jnp.concatenate(pieces, axis=0)

        a3 = a.reshape(nsub, C_SUB, GROUP_W)
        q3 = q.reshape(nsub, C_SUB, GROUP_W)
        k3 = k.reshape(nsub, C_SUB, GROUP_W)
        v3 = v.reshape(nsub, C_SUB, GROUP_W)
        ps = []
        for s in range(C_SUB):
            e = jnp.exp2(a3 - a3[:, s:s + 1, :] + tri_ref[s])
            ps.append((q3 * k3[:, s:s + 1, :] * e).reshape(L, GROUP_W).astype(BF16))
        w = jnp.dot(jnp.concatenate(ps, axis=0), ones_bd, preferred_element_type=F32)
        for s in range(C_SUB):
            vs = jnp.broadcast_to(v3[:, s:s + 1, :], (nsub, C_SUB, GROUP_W)).reshape(L, GROUP_W)
            o = o + w[s * L:(s + 1) * L] * vs

        a_last = a[L - 1:L, :]
        kdec = k * jnp.exp2(a_last - a)
        st_new = st * jnp.exp2(a_last) + jnp.where(same_head, _mm_tn(v, kdec), 0.0)

        ms = jnp.dot((o * o).astype(BF16), ones_bd, preferred_element_type=F32) * (1.0 / HEAD_DIM)
        o = o * lax.rsqrt(ms + C_EPS) * gn_ref[...]
        o_ref[0, rows, :] = o * _silu(z_ref[0, rows, :])
        return st_new

    st = st_ref[...]
    for u in range(q_ref.shape[1] // L):
        st = chunk(slice(u * L, (u + 1) * L), st)
    st_ref[...] = st


def _mixer_c(proj, c_lb, norm_g, layer):
    bsz, seq, _ = proj.shape
    L = C_STEP
    blk = lambda name: _new_offset(name) // GROUP_W
    spec = lambda name: pl.BlockSpec((1, L, GROUP_W), lambda b, c: (b, c, blk(name)))
    vec = pl.BlockSpec((1, GROUP_W), lambda b, c: (0, 0))
    t_ge_s = np.arange(C_SUB)[None, :, None] >= np.arange(C_SUB)[:, None, None]
    causal_bias = np.broadcast_to(np.where(t_ge_s, 0.0, NEG), (C_SUB, C_SUB, GROUP_W)).astype(np.float32)
    return pl.pallas_call(
        functools.partial(_mixer_c_body, layer),
        grid=(bsz, seq // L),
        in_specs=[spec('c_q'), spec('c_f'), spec('c_i'), spec('c_z'),
                  pl.BlockSpec((DEPTH, GROUP_W), lambda b, c: (0, 0)), vec,
                  pl.BlockSpec((C_SUB, C_SUB, GROUP_W), lambda b, c: (0, 0, 0))],
        out_specs=pl.BlockSpec((1, L, GROUP_W), lambda b, c: (b, c, 0)),
        out_shape=jax.ShapeDtypeStruct((bsz, seq, GROUP_W), F32),
        scratch_shapes=[pltpu.VMEM((GROUP_W, GROUP_W), F32)],
        compiler_params=_cparams(("arbitrary", "arbitrary")),
        name="mixer_c",
    )(proj, proj, proj, proj, c_lb, norm_g.reshape(1, -1), jnp.asarray(causal_bias))


N_CMP_PAD = 128
KT = 512
PRO_ROWS = 512
LOG2E = 1.4426950408889634


def _mixer_d_body(q_ref, z_ref, ksv_ref, kwv_ref, cin_ref, g_ref, cos_ref, up_ref, dn_ref, kcos_ref, kup_ref,
                  kdn_ref, ccos_ref, cup_ref, cdn_ref, w1_ref, pe_ref, w2_ref, ov_ref, o_ref,
                  k_ref, vts_ref, vtw_ref, kc_ref, vct_ref, sel_ref):
    i = pl.program_id(1)
    seq = ksv_ref.shape[1]
    half_cmp = CMP_LEN // 2

    @pl.when(i == 0)
    def _():
        def rope_rows(c, carry):
            for u in range(PRO_ROWS // KT):
                t0 = c * (PRO_ROWS // KT) + u
                rows = pl.ds(pl.multiple_of(t0 * KT, KT), KT)
                kvs = _rope(ksv_ref[0, rows, :], kcos_ref[rows, :], kup_ref[rows, :], kdn_ref[rows, :])
                kvw = _rope(kwv_ref[0, rows, :], kcos_ref[rows, :], kup_ref[rows, :], kdn_ref[rows, :])
                k_ref[0, rows, :] = kvs[:, 0:HEAD_DIM].astype(BF16)
                k_ref[1, rows, :] = kvw[:, 0:HEAD_DIM].astype(BF16)
                vts_ref[t0] = kvs.T[HEAD_DIM:2 * HEAD_DIM, :].astype(BF16)
                vwt = kvw.T[HEAD_DIM:2 * HEAD_DIM, :].astype(BF16)
                for w in range(KT // QBLK):
                    vtw_ref[t0 * (KT // QBLK) + w] = vwt[:, w * QBLK:(w + 1) * QBLK]
            return carry
        lax.fori_loop(0, seq // PRO_ROWS, rope_rows, 0)
        u0 = jnp.zeros((N_CMP_PAD, 128), F32)
        u1 = jnp.zeros((N_CMP_PAD, 128), F32)
        for r in range(half_cmp):
            xr = cin_ref[0, pl.ds(r, N_CMP_PAD, stride=CMP_STRIDE), :]
            u0 = u0 + _mm_hi(xr + pe_ref[r:r + 1, :], w1_ref[r])
            u1 = u1 + _mm_hi(xr + pe_ref[half_cmp + r:half_cmp + r + 1, :], w1_ref[half_cmp + r])
        pre = u0 + pltpu.roll(u1, N_CMP_PAD - 1, axis=0)
        cv = _rope(_mm_hi(_silu(pre), w2_ref[...]), ccos_ref[...], cup_ref[...], cdn_ref[...])
        kc_ref[...] = cv[:, 0:HEAD_DIM]
        vct_ref[...] = cv.T[HEAD_DIM:2 * HEAD_DIM, :].astype(BF16)

    q0 = pl.multiple_of(i * QBLK, QBLK)
    qrows = pl.ds(q0, QBLK)
    qr = _rope(q_ref[0], cos_ref[qrows, :], up_ref[qrows, :], dn_ref[qrows, :]) * (SCALE * LOG2E)
    qt = qr.T
    qst = jnp.concatenate([qt[h * HEAD_DIM:(h + 1) * HEAD_DIM] for h in range(N_HEADS)], axis=1)
    qst_bf = qst.astype(BF16)
    cols4 = N_HEADS * QBLK
    qpos = q0 + _iota((1, QBLK), 1)

    def lanes4(x):
        return jnp.concatenate([x] * N_HEADS, axis=1)

    n_win = NSA_WINDOW // QBLK + 1
    span = n_win * QBLK
    kt0 = jnp.maximum(i - NSA_WINDOW // QBLK, 0)
    k0 = pl.multiple_of(kt0 * QBLK, QBLK)
    rel = qpos - (k0 + _iota((span, 1), 0))
    in_band = lax.bitcast_convert_type(rel, jnp.uint32) < NSA_WINDOW
    s = jnp.dot(k_ref[1, pl.ds(k0, span), :], qst_bf, preferred_element_type=F32)
    s = s + lanes4(jnp.where(in_band, 0.0, NEG))
    p = jnp.exp2(s - jnp.max(s, axis=0, keepdims=True))
    l_w = jnp.sum(p, axis=0, keepdims=True)
    p = p.astype(BF16)
    o_win = jnp.zeros((HEAD_DIM, cols4), F32)
    for c in range(n_win):
        o_win = o_win + jnp.dot(vtw_ref[kt0 + c], p[c * QBLK:(c + 1) * QBLK], preferred_element_type=F32)
    o_win = o_win / l_w

    sc = _mm_hi(kc_ref[...], qst)
    ends = _iota((N_CMP_PAD, 1), 0) * CMP_STRIDE + (CMP_LEN - 1)
    scm = sc + lanes4(jnp.where(ends <= qpos, 0.0, NEG))
    e = jnp.exp2(scm - jnp.max(scm, axis=0, keepdims=True))
    l = jnp.sum(e, axis=0, keepdims=True)
    any_valid = lanes4(jnp.where(qpos >= CMP_LEN - 1, 1.0, 0.0))
    pc = e * (any_valid / l)
    o_cmp = jnp.dot(vct_ref[...], pc.astype(BF16), preferred_element_type=F32)
    psum = pc[:, 0:QBLK] + pc[:, QBLK:2 * QBLK] + pc[:, 2 * QBLK:3 * QBLK] + pc[:, 3 * QBLK:4 * QBLK]

    n_slc = seq // SLC_LEN
    imp = _mm_hi(ov_ref[...], psum)
    jl = _iota((n_slc, 1), 0)
    cur = jnp.right_shift(qpos, SLC_LEN.bit_length() - 1)
    imp = jnp.where(jl == 0, FORCE_SCORE, imp)
    imp = jnp.where(jl == cur, FORCE_SCORE, imp)
    imp = jnp.where(jl == cur - 1, FORCE_SCORE, imp)
    imp = jnp.where(jl <= cur, imp, NEG)
    rank = jnp.zeros((n_slc, QBLK), F32)
    for jp in range(n_slc):
        cj = imp[jp:jp + 1, :]
        tie = jnp.where(jl > jp, 1.0, 0.0)
        rank = rank + jnp.where(cj > imp, 1.0, jnp.where(cj == imp, tie, 0.0))
    sel_bias = jnp.where(rank < N_SEL, 0.0, NEG)
    blocks_per_tile = KT // SLC_LEN
    for t in range(n_slc // blocks_per_tile):
        sel_ref[t] = sel_bias[t * blocks_per_tile:(t + 1) * blocks_per_tile, :]

    n_pair = 1
    pw = cols4 // n_pair

    def sel_tile(kt, carry):
        kbase = pl.multiple_of(kt * KT, KT)
        picks = sel_ref[kt]
        chosen = jnp.concatenate([jnp.broadcast_to(picks[j:j + 1, :], (SLC_LEN, QBLK))
                                  for j in range(blocks_per_tile)], axis=0)
        bias = jnp.where((kbase + _iota((KT, 1), 0)) <= qpos, chosen, NEG)
        bias = jnp.concatenate([bias] * (pw // QBLK), axis=1)
        kb = k_ref[0, pl.ds(kbase, KT), :]
        vb = vts_ref[kt]
        out = []
        for c in range(n_pair):
            m_old, l_old, acc = carry[c]
            s = jnp.dot(kb, qst_bf[:, c * pw:(c + 1) * pw], preferred_element_type=F32) + bias
            m_new = jnp.maximum(m_old, jnp.max(s, axis=0, keepdims=True))
            p = jnp.exp2(s - m_new)
            alpha = jnp.exp2(m_old - m_new)
            l_new = alpha * l_old + jnp.sum(p, axis=0, keepdims=True)
            acc_new = alpha * acc + jnp.dot(vb, p.astype(BF16), preferred_element_type=F32)
            out.append((m_new, l_new, acc_new))
        return tuple(out)

    init = tuple((jnp.full((1, pw), NEG, F32), jnp.zeros((1, pw), F32), jnp.zeros((HEAD_DIM, pw), F32))
                 for _ in range(n_pair))
    n_tiles = (i + KT // QBLK) // (KT // QBLK)
    fin = lax.fori_loop(0, n_tiles, sel_tile, init)
    o_slc = jnp.concatenate([acc / l_c for _, l_c, acc in fin], axis=1)

    gate = _sigmoid(g_ref[0]).T
    outs = []
    for h in range(N_HEADS):
        cs = slice(h * QBLK, (h + 1) * QBLK)
        outs.append(gate[3 * h:3 * h + 1, :] * o_cmp[:, cs] + gate[3 * h + 1:3 * h + 2, :] * o_slc[:, cs]
                    + gate[3 * h + 2:3 * h + 3, :] * o_win[:, cs])
    o_ref[0] = jnp.concatenate(outs, axis=0).T * _silu(z_ref[0])


def _mixer_d(proj, qtabs, ktabs, ctabs, w1bd, pe2, w2bd, overlap):
    bsz, seq, _ = proj.shape
    blk = lambda name: _new_offset(name) // GROUP_W
    full = lambda shape: pl.BlockSpec(shape, lambda b, i: (0,) * len(shape))
    return pl.pallas_call(
        _mixer_d_body,
        grid=(bsz, seq // QBLK),
        in_specs=[pl.BlockSpec((1, QBLK, GROUP_W), lambda b, i: (b, i, blk('d_q'))),
                  pl.BlockSpec((1, QBLK, GROUP_W), lambda b, i: (b, i, blk('d_z'))),
                  pl.BlockSpec((1, seq, 128), lambda b, i: (b, 0, _new_offset('d_ksv') // 128)),
                  pl.BlockSpec((1, seq, 128), lambda b, i: (b, 0, _new_offset('d_kwv') // 128)),
                  pl.BlockSpec((1, seq, 128), lambda b, i: (b, 0, _new_offset('d_kvc') // 128)),
                  pl.BlockSpec((1, QBLK, 128), lambda b, i: (b, i, _new_offset('d_g') // 128)),
                  full((seq, GROUP_W)), full((seq, GROUP_W)), full((seq, GROUP_W)),
                  full((seq, 128)), full((seq, 128)), full((seq, 128)),
                  full((N_CMP_PAD, 128)), full((N_CMP_PAD, 128)), full((N_CMP_PAD, 128)),
                  full((CMP_LEN, 128, 128)), full((CMP_LEN, 128)), full((128, 128)),
                  full((seq // SLC_LEN, N_CMP_PAD))],
        out_specs=pl.BlockSpec((1, QBLK, GROUP_W), lambda b, i: (b, i, 0)),
        out_shape=jax.ShapeDtypeStruct((bsz, seq, GROUP_W), F32),
        scratch_shapes=[pltpu.VMEM((2, seq, HEAD_DIM), BF16),
                        pltpu.VMEM((seq // KT, HEAD_DIM, KT), BF16),
                        pltpu.VMEM((seq // QBLK, HEAD_DIM, QBLK), BF16),
                        pltpu.VMEM((N_CMP_PAD, HEAD_DIM), F32),
                        pltpu.VMEM((HEAD_DIM, N_CMP_PAD), BF16),
                        pltpu.VMEM((seq // KT, KT // SLC_LEN, QBLK), F32)],
        compiler_params=_cparams(("arbitrary", "arbitrary")),
        name="mixer_d",
    )(proj, proj, proj, proj, proj, proj, *qtabs, *ktabs, *ctabs, w1bd, pe2, w2bd, overlap)


def _cmp_params(pe, w1, w2):
    w1 = w1.reshape(2, CMP_LEN, HEAD_DIM, HEAD_DIM)
    z = jnp.zeros((CMP_LEN, HEAD_DIM, HEAD_DIM), F32)
    w1bd = jnp.concatenate([jnp.concatenate([w1[0], z], axis=2), jnp.concatenate([z, w1[1]], axis=2)], axis=1)
    z2 = jnp.zeros((HEAD_DIM, HEAD_DIM), F32)
    w2bd = jnp.concatenate([jnp.concatenate([w2[0], z2], axis=1), jnp.concatenate([z2, w2[1]], axis=1)], axis=0)
    pe2 = jnp.concatenate([pe[0], pe[1]], axis=1)
    return w1bd, pe2, w2bd


def _overlap_matrix(seq):
    n_cmp = (seq - CMP_LEN) // CMP_STRIDE + 1
    starts = np.arange(N_CMP_PAD) * CMP_STRIDE
    blk = np.arange(seq // SLC_LEN)
    ov = ((starts[None, :] < (blk[:, None] + 1) * SLC_LEN) & (starts[None, :] + CMP_LEN > blk[:, None] * SLC_LEN))
    ov = ov & (np.arange(N_CMP_PAD)[None, :] < n_cmp)
    return jnp.asarray(ov.astype(np.float32))


def kernel(x, ln0_g, ln0_b, w_in, b_in, a_sinks, b_conv_w, b_conv_b, c_lb, c_norm_g, d_cmp_pe, d_cmp_w1, d_cmp_w2,
           w_out, ln_g, ln_b):
    bsz, seq, d = x.shape
    pos = jnp.arange(seq)
    qtabs = _rope_tables(pos, GROUP_W, 128)
    ktabs = _rope_tables(pos, 128, HEAD_DIM)
    ctabs = _rope_tables(jnp.arange(N_CMP_PAD) * CMP_STRIDE + (CMP_LEN - 1), 128, HEAD_DIM)
    overlap = _overlap_matrix(seq)

    h = _layer_norm(x.reshape(bsz * seq, d), ln0_g, ln0_b)
    for l in range(DEPTH):
        w_p, b_p = _prep_in_weights(w_in, b_in, l)
        proj = _in_proj(h, w_p, b_p).reshape(bsz, seq, N_PROJ)
        y_a = _mixer_a(proj, a_sinks[l], qtabs)
        y_b = _mixer_b(proj, b_conv_w[l], b_conv_b[l])
        y_c = _mixer_c(proj, c_lb, c_norm_g[l], l)
        w1bd, pe2, w2bd = _cmp_params(d_cmp_pe[l], d_cmp_w1[l], d_cmp_w2[l])
        y_d = _mixer_d(proj, qtabs, ktabs, ctabs, w1bd, pe2, w2bd, overlap)
        h = _out_proj((y_a, y_b, y_c, y_d), h, w_out[l].astype(BF16), ln_g[l], ln_b[l])
    return h.reshape(bsz, seq, d)
```

```python
import functools

import numpy as np
import jax
import jax.numpy as jnp
from jax import lax
from jax.experimental import pallas as pl
from jax.experimental.pallas import tpu as pltpu

F32 = jnp.float32
BF16 = jnp.bfloat16

D_MODEL = 1024
DEPTH = 2
HEAD_DIM = 64
HALF = HEAD_DIM // 2
N_HEADS = 4
GROUP_W = N_HEADS * HEAD_DIM
ROPE_THETA = 10000.0
NEG = -1e30
LN_EPS = 1e-5
A_WINDOW = 128
A_GROUP = 2
LOG2E = 1.4426950408889634
PRO_ROWS = 512
B_CONV = 4
C_CHUNK = 64
C_SUB = 16
C_STEP = 128
C_EPS = 1e-6
CMP_LEN = 32
CMP_STRIDE = 16
SLC_LEN = 64
N_SEL = 8
NSA_WINDOW = 512
FORCE_SCORE = 1e6
DN_ALPHA = (2.0 * DEPTH) ** 0.25
QBLK = 128
SCALE = HEAD_DIM ** -0.5

VMEM_LIMIT = 56 * 1024 * 1024

LANES = 128

ORIG_SPLITS = (
    ('a_q', 256), ('a_k', 128), ('a_v', 128), ('a_z', 256),
    ('b_q', 256), ('b_k', 256), ('b_v', 256), ('b_if', 8), ('b_o', 256), ('b_z', 256),
    ('c_q', 256), ('c_f', 256), ('c_i', 256), ('c_z', 256),
    ('d_q', 256), ('d_kvc', 128), ('d_ksv', 128), ('d_kwv', 128), ('d_g', 12), ('d_z', 256),
)
N_COLS = sum(w for _, w in ORIG_SPLITS)
NEW_LAYOUT = (
    ('a_q', 256), ('a_k', 128), ('a_v', 128), ('a_z', 256), ('b_q', 256), ('b_k', 256), ('b_v', 256),
    ('b_o', 256), ('b_z', 256), ('c_q', 256), ('c_f', 256), ('c_i', 256), ('c_z', 256),
    ('d_q', 256), ('d_kvc', 128), ('d_ksv', 128), ('d_kwv', 128), ('b_if', 128), ('d_z', 256), ('d_g', 128),
)
N_PROJ = sum(w for _, w in NEW_LAYOUT)


def _offset(layout, name):
    off = 0
    for n, w in layout:
        if n == name:
            return off
        off += w
    raise KeyError(name)


def _new_offset(name):
    return _offset(NEW_LAYOUT, name)


def _permuted_cols(load):
    sizes = dict(ORIG_SPLITS)
    for name, width in NEW_LAYOUT:
        old, size, new = _offset(ORIG_SPLITS, name), sizes[name], _new_offset(name)
        lo = old // LANES * LANES
        if size == width:
            hi = min(-(-(old + size) // LANES) * LANES, N_COLS)
            yield new, load(lo, hi)[..., old - lo:old - lo + size]
        else:
            tile = load(lo, lo + LANES)
            if old != lo:
                tile = pltpu.roll(tile, LANES - (old - lo), axis=tile.ndim - 1)
            yield new, jnp.where(_iota(tile.shape, tile.ndim - 1) < size, tile, 0.0)


def _prep_body(w_ref, b_ref, wo_ref, bo_ref):
    for new, cols in _permuted_cols(lambda lo, hi: w_ref[0, :, lo:hi]):
        wo_ref[:, new:new + cols.shape[-1]] = cols.astype(BF16)
    for new, cols in _permuted_cols(lambda lo, hi: b_ref[0, :, lo:hi]):
        bo_ref[:, new:new + cols.shape[-1]] = cols


def _prep_in_weights(w_in, b_in, layer, rows=128):
    depth, d, n = w_in.shape
    return pl.pallas_call(
        _prep_body,
        grid=(d // rows,),
        in_specs=[pl.BlockSpec((1, rows, n), lambda r: (layer, r, 0)),
                  pl.BlockSpec((1, 1, n), lambda r: (layer, 0, 0))],
        out_specs=[pl.BlockSpec((rows, N_PROJ), lambda r: (r, 0)),
                   pl.BlockSpec((1, N_PROJ), lambda r: (0, 0))],
        out_shape=[jax.ShapeDtypeStruct((d, N_PROJ), BF16), jax.ShapeDtypeStruct((1, N_PROJ), F32)],
        compiler_params=_cparams(("arbitrary",)),
        name="prep_in_weights",
    )(w_in, b_in.reshape(depth, 1, n))


def _mm(a, b):
    return jnp.dot(a.astype(BF16), b.astype(BF16), preferred_element_type=F32)


def _mm_nt(a, b):
    return lax.dot_general(a.astype(BF16), b.astype(BF16), (((1,), (1,)), ((), ())),
                           preferred_element_type=F32)


def _mm_tn(a, b):
    return lax.dot_general(a.astype(BF16), b.astype(BF16), (((0,), (0,)), ((), ())),
                           preferred_element_type=F32)


def _dot_bf16(a, b):
    return jnp.dot(a, b, preferred_element_type=F32)


def _split_bf16(x, terms):
    out = []
    for _ in range(terms - 1):
        t = x.astype(BF16)
        out.append(t)
        x = x - t.astype(F32)
    out.append(x.astype(BF16))
    return out


def _mm_hi(a, b):
    ah, al = _split_bf16(a, 2)
    bh, bl = _split_bf16(b, 2)
    return _dot_bf16(ah, bh) + (_dot_bf16(ah, bl) + _dot_bf16(al, bh))


def _mm_sel(sel, x):
    sel = sel.astype(BF16)
    x1, x2, x3 = _split_bf16(x, 3)
    return _dot_bf16(sel, x1) + (_dot_bf16(sel, x2) + _dot_bf16(sel, x3))


def _sigmoid(x):
    return 1.0 / (1.0 + jnp.exp(-x))


def _silu(x):
    return x * _sigmoid(x)


def _log_sigmoid(x):
    return jnp.minimum(x, 0.0) - jnp.log1p(jnp.exp(-jnp.abs(x)))


def _rope(x, cos, sin_up, sin_dn):
    w = x.shape[-1]
    up = pltpu.roll(x, w - HALF, axis=1)
    dn = pltpu.roll(x, HALF, axis=1)
    return x * cos + up * sin_up + dn * sin_dn


def _rope_tables(pos, width, rot_lanes):
    inv = ROPE_THETA ** (-jnp.arange(HALF, dtype=F32) / HALF)
    ang = pos.astype(F32)[:, None] * inv[None, :]
    cos, sin = jnp.cos(ang), jnp.sin(ang)
    zero, one = jnp.zeros_like(sin), jnp.ones_like(cos)
    cos_h = jnp.concatenate([cos, cos], axis=1)
    up_h = jnp.concatenate([-sin, zero], axis=1)
    dn_h = jnp.concatenate([zero, sin], axis=1)
    id_c = jnp.concatenate([one, one], axis=1)
    id_s = jnp.concatenate([zero, zero], axis=1)
    cs, us, ds = [], [], []
    for l0 in range(0, width, HEAD_DIM):
        rot = (l0 % 128) < rot_lanes
        cs.append(cos_h if rot else id_c)
        us.append(up_h if rot else id_s)
        ds.append(dn_h if rot else id_s)
    return jnp.concatenate(cs, axis=1), jnp.concatenate(us, axis=1), jnp.concatenate(ds, axis=1)


def _iota(shape, dim):
    return lax.broadcasted_iota(jnp.int32, shape, dim)


def _cparams(sem):
    return pltpu.CompilerParams(dimension_semantics=sem, vmem_limit_bytes=VMEM_LIMIT)


def _ln_body(x_ref, g_ref, b_ref, o_ref):
    x = x_ref[...]
    mu = jnp.mean(x, axis=-1, keepdims=True)
    xc = x - mu
    var = jnp.mean(xc * xc, axis=-1, keepdims=True)
    o_ref[...] = xc * lax.rsqrt(var + LN_EPS) * g_ref[...] + b_ref[...]


def _layer_norm(x2, g, b, tm=512):
    m, d = x2.shape
    return pl.pallas_call(
        _ln_body,
        grid=(m // tm,),
        in_specs=[pl.BlockSpec((tm, d), lambda i: (i, 0)),
                  pl.BlockSpec((1, d), lambda i: (0, 0)),
                  pl.BlockSpec((1, d), lambda i: (0, 0))],
        out_specs=pl.BlockSpec((tm, d), lambda i: (i, 0)),
        out_shape=jax.ShapeDtypeStruct((m, d), F32),
        compiler_params=_cparams(("parallel",)),
        name="layer_norm",
    )(x2, g.reshape(1, d), b.reshape(1, d))


def _proj_body(x_ref, w_ref, b_ref, o_ref):
    o_ref[...] = jnp.dot(x_ref[...].astype(BF16), w_ref[...], preferred_element_type=F32) + b_ref[...]


def _in_proj(h2, w_bf, b, tm=256):
    m, d = h2.shape
    n = w_bf.shape[1]
    return pl.pallas_call(
        _proj_body,
        grid=(m // tm,),
        in_specs=[pl.BlockSpec((tm, d), lambda i: (i, 0)),
                  pl.BlockSpec((d, n), lambda i: (0, 0)),
                  pl.BlockSpec((1, n), lambda i: (0, 0))],
        out_specs=pl.BlockSpec((tm, n), lambda i: (i, 0)),
        out_shape=jax.ShapeDtypeStruct((m, n), F32),
        compiler_params=_cparams(("parallel",)),
        name="in_proj",
    )(h2, w_bf, b.reshape(1, n))


def _out_body(ya_ref, yb_ref, yc_ref, yd_ref, h_ref, w_ref, g_ref, b_ref, o_ref):
    y = _mm(ya_ref[...], w_ref[0:GROUP_W, :])
    y += _mm(yb_ref[...], w_ref[GROUP_W:2 * GROUP_W, :])
    y += _mm(yc_ref[...], w_ref[2 * GROUP_W:3 * GROUP_W, :])
    y += _mm(yd_ref[...], w_ref[3 * GROUP_W:4 * GROUP_W, :])
    r = DN_ALPHA * h_ref[...] + y
    mu = jnp.mean(r, axis=-1, keepdims=True)
    rc = r - mu
    var = jnp.mean(rc * rc, axis=-1, keepdims=True)
    o_ref[...] = rc * lax.rsqrt(var + LN_EPS) * g_ref[...] + b_ref[...]


def _out_proj(ys, h2, w_bf, g, b, tm=512):
    m, d = h2.shape
    yspec = pl.BlockSpec((tm, GROUP_W), lambda i: (i, 0))
    return pl.pallas_call(
        _out_body,
        grid=(m // tm,),
        in_specs=[yspec, yspec, yspec, yspec,
                  pl.BlockSpec((tm, d), lambda i: (i, 0)),
                  pl.BlockSpec((4 * GROUP_W, d), lambda i: (0, 0)),
                  pl.BlockSpec((1, d), lambda i: (0, 0)),
                  pl.BlockSpec((1, d), lambda i: (0, 0))],
        out_specs=pl.BlockSpec((tm, d), lambda i: (i, 0)),
        out_shape=jax.ShapeDtypeStruct((m, d), F32),
        compiler_params=_cparams(("parallel",)),
        name="out_proj",
    )(*[y.reshape(m, GROUP_W) for y in ys], h2, w_bf, g.reshape(1, d), b.reshape(1, d))


def _mixer_a_body(sink_ref, q_ref, kv_ref, z_ref, cos_ref, up_ref, dn_ref, o_ref, k_ref, vt_ref):
    i = pl.program_id(1)
    seq = kv_ref.shape[1]
    n_kv = N_HEADS // A_GROUP

    @pl.when(i == 0)
    def _():
        def rope_rows(c, carry):
            for u in range(PRO_ROWS // QBLK):
                t0 = c * (PRO_ROWS // QBLK) + u
                rows = pl.ds(pl.multiple_of(t0 * QBLK, QBLK), QBLK)
                kr = _rope(kv_ref[0, rows, 0:128], cos_ref[rows, 0:128], up_ref[rows, 0:128], dn_ref[rows, 0:128])
                v_t = kv_ref[0, rows, 128:256].T
                for g in range(n_kv):
                    k_ref[g, rows, :] = kr[:, g * HEAD_DIM:(g + 1) * HEAD_DIM].astype(BF16)
                    vt_ref[g, t0] = v_t[g * HEAD_DIM:(g + 1) * HEAD_DIM, :].astype(BF16)
            return carry
        lax.fori_loop(0, seq // PRO_ROWS, rope_rows, 0)

    q0 = pl.multiple_of(i * QBLK, QBLK)
    qrows = pl.ds(q0, QBLK)
    qr = _rope(q_ref[0], cos_ref[qrows, :], up_ref[qrows, :], dn_ref[qrows, :]) * (SCALE * LOG2E)
    qt = qr.T
    n_blk = A_WINDOW // QBLK + 1
    span = n_blk * QBLK
    kt0 = jnp.maximum(i - A_WINDOW // QBLK, 0)
    k0 = pl.multiple_of(kt0 * QBLK, QBLK)
    rel = (q0 + _iota((1, QBLK), 1)) - (k0 + _iota((span, 1), 0))
    in_band = lax.bitcast_convert_type(rel, jnp.uint32) < A_WINDOW
    bias = jnp.concatenate([jnp.where(in_band, 0.0, NEG)] * A_GROUP, axis=1)
    first_head = _iota((1, A_GROUP * QBLK), 1) < QBLK
    outs = []
    for g in range(n_kv):
        heads = range(g * A_GROUP, (g + 1) * A_GROUP)
        qg = jnp.concatenate([qt[h * HEAD_DIM:(h + 1) * HEAD_DIM] for h in heads], axis=1).astype(BF16)
        sink = jnp.where(first_head, sink_ref[g * A_GROUP], sink_ref[g * A_GROUP + 1]) * LOG2E
        s = jnp.dot(k_ref[g, pl.ds(k0, span), :], qg, preferred_element_type=F32) + bias
        m = jnp.maximum(jnp.max(s, axis=0, keepdims=True), sink)
        p = jnp.exp2(s - m)
        l = jnp.sum(p, axis=0, keepdims=True) + jnp.exp2(sink - m)
        p = p.astype(BF16)
        o = jnp.zeros((HEAD_DIM, A_GROUP * QBLK), F32)
        for c in range(n_blk):
            o = o + jnp.dot(vt_ref[g, kt0 + c], p[c * QBLK:(c + 1) * QBLK], preferred_element_type=F32)
        o = o / l
        outs.extend(o[:, r * QBLK:(r + 1) * QBLK] for r in range(A_GROUP))
    o_ref[0] = jnp.concatenate(outs, axis=0).T * _silu(z_ref[0])


def _mixer_a(proj, sinks, tabs):
    bsz, seq, _ = proj.shape
    cos, up, dn = tabs
    blk = lambda name: _new_offset(name) // GROUP_W
    tspec = pl.BlockSpec((seq, GROUP_W), lambda b, i: (0, 0))
    return pl.pallas_call(
        _mixer_a_body,
        grid=(bsz, seq // QBLK),
        in_specs=[pl.BlockSpec(memory_space=pltpu.SMEM),
                  pl.BlockSpec((1, QBLK, GROUP_W), lambda b, i: (b, i, blk('a_q'))),
                  pl.BlockSpec((1, seq, GROUP_W), lambda b, i: (b, 0, blk('a_k'))),
                  pl.BlockSpec((1, QBLK, GROUP_W), lambda b, i: (b, i, blk('a_z'))),
                  tspec, tspec, tspec],
        out_specs=pl.BlockSpec((1, QBLK, GROUP_W), lambda b, i: (b, i, 0)),
        out_shape=jax.ShapeDtypeStruct((bsz, seq, GROUP_W), F32),
        scratch_shapes=[pltpu.VMEM((N_HEADS // A_GROUP, seq, HEAD_DIM), BF16),
                        pltpu.VMEM((N_HEADS // A_GROUP, seq // QBLK, HEAD_DIM, QBLK), BF16)],
        compiler_params=_cparams(("arbitrary", "arbitrary")),
        name="mixer_a",
    )(sinks, proj, proj, proj, cos, up, dn)


B_CHUNK = 128
B_AUG = HEAD_DIM + 16


def _conv_silu(x, prev, w, b):
    row8 = _iota((8, x.shape[1]), 0)
    acc = b + w[B_CONV - 1:B_CONV, :] * x
    for s in range(1, B_CONV):
        xs = pltpu.roll(x, s, axis=0)
        ps = pltpu.roll(prev, s, axis=0)
        shifted = jnp.concatenate([jnp.where(row8 < s, ps, xs[0:8]), xs[8:]], axis=0)
        acc = acc + w[B_CONV - 1 - s:B_CONV - s, :] * shifted
    return _silu(acc)


def _mixer_b_body(xq_ref, xk_ref, v_ref, og_ref, z_ref, if_ref, cw_ref, cb_ref, o_ref, xprev_ref, c_ref, m_ref):
    c = pl.program_id(1)
    L = B_CHUNK

    @pl.when(c == 0)
    def _():
        xprev_ref[...] = jnp.zeros_like(xprev_ref)
        c_ref[...] = jnp.zeros_like(c_ref)
        m_ref[...] = jnp.zeros_like(m_ref)

    xq = xq_ref[0]
    xk = xk_ref[0]
    qc = _conv_silu(xq, xprev_ref[:, 0:GROUP_W], cw_ref[:, 0:GROUP_W], cb_ref[:, 0:GROUP_W]) * SCALE
    kc = _conv_silu(xk, xprev_ref[:, GROUP_W:], cw_ref[:, GROUP_W:], cb_ref[:, GROUP_W:])
    xprev_ref[:, 0:GROUP_W] = xq[L - 8:L, :]
    xprev_ref[:, GROUP_W:] = xk[L - 8:L, :]
    q_t = qc.T
    v_t = v_ref[0].T

    gates = if_ref[0]
    logf = _log_sigmoid(gates)
    tri = (_iota((L, L), 1) <= _iota((L, L), 0)).astype(F32)
    bcum = _mm_sel(tri, logf)
    gates_t = gates.T
    bcum_t = bcum.T
    key_first = _iota((L, L), 0) <= _iota((L, L), 1)
    half_of_lane = _iota((1, 128), 1) // HEAD_DIM
    ones_rows = jnp.ones((B_AUG - HEAD_DIM, L), F32)
    outs = []
    for h in range(N_HEADS):
        pair = slice((h // 2) * 128, (h // 2 + 1) * 128)
        own = half_of_lane == (h % 2)
        k_pair = kc[:, pair]
        q_pair_t = q_t[pair, :]
        i_row = gates_t[h:h + 1, :]
        b_row = bcum_t[N_HEADS + h:N_HEADS + h + 1, :]
        c_col = gates[:, h:h + 1] - bcum[:, N_HEADS + h:N_HEADS + h + 1]
        m_prev = m_ref[h:h + 1, 0:1]
        dmat = jnp.where(key_first, b_row + c_col, NEG)
        inter = b_row + m_prev
        m_t = jnp.maximum(jnp.max(dmat, axis=0, keepdims=True), inter)
        smat = _mm(jnp.where(own, k_pair, 0.0), q_pair_t) * jnp.exp(dmat - m_t)
        vaug_t = jnp.concatenate([v_t[h * HEAD_DIM:(h + 1) * HEAD_DIM, :], ones_rows], axis=0)
        cst = c_ref[h]
        tot = _mm(vaug_t, smat) + jnp.exp(inter - m_t) * _mm(cst, q_pair_t)
        num = tot[0:HEAD_DIM, :]
        den = tot[HEAD_DIM:HEAD_DIM + 1, :]
        outs.append(num / jnp.maximum(jnp.abs(den), jnp.exp(-m_t)))
        b_last = b_row[:, L - 1:L]
        d_row = b_last - b_row + i_row
        m_new = jnp.maximum(b_last + m_prev, jnp.max(d_row, axis=1, keepdims=True))
        w_row = jnp.exp(d_row - m_new)
        decay = jnp.exp(b_last + m_prev - m_new)
        c_ref[h] = decay * cst + jnp.where(own, _mm(vaug_t * w_row, k_pair), 0.0)
        m_ref[h:h + 1, :] = jnp.broadcast_to(m_new, (1, 128))
    o_ref[0] = _sigmoid(og_ref[0]) * jnp.concatenate(outs, axis=0).T * _silu(z_ref[0])


def _mixer_b(proj, conv_w, conv_b):
    bsz, seq, _ = proj.shape
    L = B_CHUNK
    blk = lambda name: _new_offset(name) // GROUP_W
    return pl.pallas_call(
        _mixer_b_body,
        grid=(bsz, seq // L),
        in_specs=[pl.BlockSpec((1, L, GROUP_W), lambda b, c: (b, c, blk('b_q'))),
                  pl.BlockSpec((1, L, GROUP_W), lambda b, c: (b, c, blk('b_k'))),
                  pl.BlockSpec((1, L, GROUP_W), lambda b, c: (b, c, blk('b_v'))),
                  pl.BlockSpec((1, L, GROUP_W), lambda b, c: (b, c, blk('b_o'))),
                  pl.BlockSpec((1, L, GROUP_W), lambda b, c: (b, c, blk('b_z'))),
                  pl.BlockSpec((1, L, 128), lambda b, c: (b, c, _new_offset('b_if') // 128)),
                  pl.BlockSpec((B_CONV, 2 * GROUP_W), lambda b, c: (0, 0)),
                  pl.BlockSpec((1, 2 * GROUP_W), lambda b, c: (0, 0))],
        out_specs=pl.BlockSpec((1, L, GROUP_W), lambda b, c: (b, c, 0)),
        out_shape=jax.ShapeDtypeStruct((bsz, seq, GROUP_W), F32),
        scratch_shapes=[pltpu.VMEM((8, 2 * GROUP_W), F32),
                        pltpu.VMEM((N_HEADS, B_AUG, 128), F32),
                        pltpu.VMEM((8, 128), F32)],
        compiler_params=_cparams(("arbitrary", "arbitrary")),
        name="mixer_b",
    )(proj, proj, proj, proj, proj, proj, conv_w, conv_b.reshape(1, -1))


def _mixer_c_body(layer, q_ref, f_ref, i_ref, z_ref, lb_ref, gn_ref, tri_ref, o_ref, st_ref):
    c = pl.program_id(1)
    L = C_CHUNK
    nsub = L // C_SUB

    @pl.when(c == 0)
    def _():
        st_ref[...] = jnp.zeros_like(st_ref)

    raw = lb_ref[...]
    ex = jnp.exp(raw - jnp.max(raw, axis=0, keepdims=True))
    if layer == 0:
        lb = jnp.zeros((1, GROUP_W), F32)
    else:
        lb = jnp.sum(ex[1:layer + 1], axis=0, keepdims=True) / jnp.sum(ex, axis=0, keepdims=True)
    tri = (_iota((L, L), 1) <= _iota((L, L), 0)).astype(F32)
    head_of_lane = _iota((1, GROUP_W), 1) // HEAD_DIM
    same_head = ((_iota((GROUP_W, GROUP_W), 0) // HEAD_DIM)
                 == (_iota((GROUP_W, GROUP_W), 1) // HEAD_DIM))
    ones_bd = same_head.astype(BF16)

    def chunk(rows, st):
        f = lb + (1.0 - lb) * _sigmoid(f_ref[0, rows, :])
        logf = jnp.log(f)
        k = 1.0 - f
        q = _silu(q_ref[0, rows, :])
        v = i_ref[0, rows, :]
        a = _mm_sel(tri, logf) * LOG2E
        o = _mm_nt(q * jnp.exp2(a), st)

        pieces = [o[0:C_SUB]]
        for i in range(1, nsub):
            r0 = i * C_SUB
            piv = a[r0 - 1:r0, :]
            qt = q[r0:r0 + C_SUB] * jnp.exp2(a[r0:r0 + C_SUB] - piv)
            kt = k[0:r0] * jnp.exp2(piv - a[0:r0])
            qst = jnp.concatenate([jnp.where(head_of_lane == h, qt, 0.0) for h in range(N_HEADS)], axis=0)
            r = _mm(_mm_nt(qst, kt), v[0:r0])
            acc = o[r0:r0 + C_SUB]
            for h in range(N_HEADS):
                acc = acc + jnp.where(head_of_lane == h, r[h * C_SUB:(h + 1) * C_SUB], 0.0)
            pieces.append(acc)
        o = jnp.concatenate(pieces, axis=0)

        a3 = a.reshape(nsub, C_SUB, GROUP_W)
        q3 = q.reshape(nsub, C_SUB, GROUP_W)
        k3 = k.reshape(nsub, C_SUB, GROUP_W)
        v3 = v.reshape(nsub, C_SUB, GROUP_W)
        ps = []
        for s in range(C_SUB):
            e = jnp.exp2(a3 - a3[:, s:s + 1, :] + tri_ref[s])
            ps.append((q3 * k3[:, s:s + 1, :] * e).reshape(L, GROUP_W).astype(BF16))
        w = jnp.dot(jnp.concatenate(ps, axis=0), ones_bd, preferred_element_type=F32)
        for s in range(C_SUB):
            vs = jnp.broadcast_to(v3[:, s:s + 1, :], (nsub, C_SUB, GROUP_W)).reshape(L, GROUP_W)
            o = o + w[s * L:(s + 1) * L] * vs

        a_last = a[L - 1:L, :]
        kdec = k * jnp.exp2(a_last - a)
        st_new = st * jnp.exp2(a_last) + jnp.where(same_head, _mm_tn(v, kdec), 0.0)

        ms = jnp.dot((o * o).astype(BF16), ones_bd, preferred_element_type=F32) * (1.0 / HEAD_DIM)
        o = o * lax.rsqrt(ms + C_EPS) * gn_ref[...]
        o_ref[0, rows, :] = o * _silu(z_ref[0, rows, :])
        return st_new

    st = st_ref[...]
    for u in range(q_ref.shape[1] // L):
        st = chunk(slice(u * L, (u + 1) * L), st)
    st_ref[...] = st


def _mixer_c(proj, c_lb, norm_g, layer):
    bsz, seq, _ = proj.shape
    L = C_STEP
    blk = lambda name: _new_offset(name) // GROUP_W
    spec = lambda name: pl.BlockSpec((1, L, GROUP_W), lambda b, c: (b, c, blk(name)))
    vec = pl.BlockSpec((1, GROUP_W), lambda b, c: (0, 0))
    t_ge_s = np.arange(C_SUB)[None, :, None] >= np.arange(C_SUB)[:, None, None]
    causal_bias = np.broadcast_to(np.where(t_ge_s, 0.0, NEG), (C_SUB, C_SUB, GROUP_W)).astype(np.float32)
    return pl.pallas_call(
        functools.partial(_mixer_c_body, layer),
        grid=(bsz, seq // L),
        in_specs=[spec('c_q'), spec('c_f'), spec('c_i'), spec('c_z'),
                  pl.BlockSpec((DEPTH, GROUP_W), lambda b, c: (0, 0)), vec,
                  pl.BlockSpec((C_SUB, C_SUB, GROUP_W), lambda b, c: (0, 0, 0))],
        out_specs=pl.BlockSpec((1, L, GROUP_W), lambda b, c: (b, c, 0)),
        out_shape=jax.ShapeDtypeStruct((bsz, seq, GROUP_W), F32),
        scratch_shapes=[pltpu.VMEM((GROUP_W, GROUP_W), F32)],
        compiler_params=_cparams(("arbitrary", "arbitrary")),
        name="mixer_c",
    )(proj, proj, proj, proj, c_lb, norm_g.reshape(1, -1), jnp.asarray(causal_bias))


N_CMP_PAD = 128
KT = 512


def _mixer_d_body(q_ref, z_ref, ksv_ref, kwv_ref, cin_ref, g_ref, cos_ref, up_ref, dn_ref, kcos_ref, kup_ref,
                  kdn_ref, ccos_ref, cup_ref, cdn_ref, w1_ref, pe_ref, w2_ref, ov_ref, o_ref,
                  k_ref, vts_ref, vtw_ref, kc_ref, vct_ref, sel_ref):
    i = pl.program_id(1)
    seq = ksv_ref.shape[1]
    half_cmp = CMP_LEN // 2

    @pl.when(i == 0)
    def _():
        def rope_rows(c, carry):
            for u in range(PRO_ROWS // KT):
                t0 = c * (PRO_ROWS // KT) + u
                rows = pl.ds(pl.multiple_of(t0 * KT, KT), KT)
                kvs = _rope(ksv_ref[0, rows, :], kcos_ref[rows, :], kup_ref[rows, :], kdn_ref[rows, :])
                kvw = _rope(kwv_ref[0, rows, :], kcos_ref[rows, :], kup_ref[rows, :], kdn_ref[rows, :])
                k_ref[0, rows, :] = kvs[:, 0:HEAD_DIM].astype(BF16)
                k_ref[1, rows, :] = kvw[:, 0:HEAD_DIM].astype(BF16)
                vts_ref[t0] = kvs.T[HEAD_DIM:2 * HEAD_DIM, :].astype(BF16)
                vwt = kvw.T[HEAD_DIM:2 * HEAD_DIM, :].astype(BF16)
                for w in range(KT // QBLK):
                    vtw_ref[t0 * (KT // QBLK) + w] = vwt[:, w * QBLK:(w + 1) * QBLK]
            return carry
        lax.fori_loop(0, seq // PRO_ROWS, rope_rows, 0)
        u0 = jnp.zeros((N_CMP_PAD, 128), F32)
        u1 = jnp.zeros((N_CMP_PAD, 128), F32)
        for r in range(half_cmp):
            xr = cin_ref[0, pl.ds(r, N_CMP_PAD, stride=CMP_STRIDE), :]
            u0 = u0 + _mm_hi(xr + pe_ref[r:r + 1, :], w1_ref[r])
            u1 = u1 + _mm_hi(xr + pe_ref[half_cmp + r:half_cmp + r + 1, :], w1_ref[half_cmp + r])
        pre = u0 + pltpu.roll(u1, N_CMP_PAD - 1, axis=0)
        cv = _rope(_mm_hi(_silu(pre), w2_ref[...]), ccos_ref[...], cup_ref[...], cdn_ref[...])
        kc_ref[...] = cv[:, 0:HEAD_DIM]
        vct_ref[...] = cv.T[HEAD_DIM:2 * HEAD_DIM, :].astype(BF16)

    q0 = pl.multiple_of(i * QBLK, QBLK)
    qrows = pl.ds(q0, QBLK)
    qr = _rope(q_ref[0], cos_ref[qrows, :], up_ref[qrows, :], dn_ref[qrows, :]) * (SCALE * LOG2E)
    qt = qr.T
    qst = jnp.concatenate([qt[h * HEAD_DIM:(h + 1) * HEAD_DIM] for h in range(N_HEADS)], axis=1)
    qst_bf = qst.astype(BF16)
    cols4 = N_HEADS * QBLK
    qpos = q0 + _iota((1, QBLK), 1)

    def lanes4(x):
        return jnp.concatenate([x] * N_HEADS, axis=1)

    n_win = NSA_WINDOW // QBLK + 1
    span = n_win * QBLK
    kt0 = jnp.maximum(i - NSA_WINDOW // QBLK, 0)
    k0 = pl.multiple_of(kt0 * QBLK, QBLK)
    rel = qpos - (k0 + _iota((span, 1), 0))
    in_band = lax.bitcast_convert_type(rel, jnp.uint32) < NSA_WINDOW
    s = jnp.dot(k_ref[1, pl.ds(k0, span), :], qst_bf, preferred_element_type=F32)
    s = s + lanes4(jnp.where(in_band, 0.0, NEG))
    p = jnp.exp2(s - jnp.max(s, axis=0, keepdims=True))
    l_w = jnp.sum(p, axis=0, keepdims=True)
    p = p.astype(BF16)
    o_win = jnp.zeros((HEAD_DIM, cols4), F32)
    for c in range(n_win):
        o_win = o_win + jnp.dot(vtw_ref[kt0 + c], p[c * QBLK:(c + 1) * QBLK], preferred_element_type=F32)
    o_win = o_win / l_w

    sc = _mm_hi(kc_ref[...], qst)
    ends = _iota((N_CMP_PAD, 1), 0) * CMP_STRIDE + (CMP_LEN - 1)
    scm = sc + lanes4(jnp.where(ends <= qpos, 0.0, NEG))
    e = jnp.exp2(scm - jnp.max(scm, axis=0, keepdims=True))
    l = jnp.sum(e, axis=0, keepdims=True)
    any_valid = lanes4(jnp.where(qpos >= CMP_LEN - 1, 1.0, 0.0))
    pc = e * (any_valid / l)
    o_cmp = jnp.dot(vct_ref[...], pc.astype(BF16), preferred_element_type=F32)
    psum = pc[:, 0:QBLK] + pc[:, QBLK:2 * QBLK] + pc[:, 2 * QBLK:3 * QBLK] + pc[:, 3 * QBLK:4 * QBLK]

    n_slc = seq // SLC_LEN
    imp = _mm_sel(ov_ref[...], psum)
    jl = _iota((n_slc, 1), 0)
    cur = jnp.right_shift(qpos, SLC_LEN.bit_length() - 1)
    imp = jnp.where(jl == 0, FORCE_SCORE, imp)
    imp = jnp.where(jl == cur, FORCE_SCORE, imp)
    imp = jnp.where(jl == cur - 1, FORCE_SCORE, imp)
    imp = jnp.where(jl <= cur, imp, NEG)
    rank = jnp.zeros((n_slc, QBLK), F32)
    for jp in range(n_slc):
        cj = imp[jp:jp + 1, :]
        tie = jnp.where(jl > jp, 1.0, 0.0)
        rank = rank + jnp.where(cj > imp, 1.0, jnp.where(cj == imp, tie, 0.0))
    sel_bias = jnp.where(rank < N_SEL, 0.0, NEG)
    blocks_per_tile = KT // SLC_LEN
    for t in range(n_slc // blocks_per_tile):
        sel_ref[t] = sel_bias[t * blocks_per_tile:(t + 1) * blocks_per_tile, :]

    n_pair = 1
    pw = cols4 // n_pair

    def sel_tile(kt, carry):
        kbase = pl.multiple_of(kt * KT, KT)
        picks = sel_ref[kt]
        chosen = jnp.concatenate([jnp.broadcast_to(picks[j:j + 1, :], (SLC_LEN, QBLK))
                                  for j in range(blocks_per_tile)], axis=0)
        bias = jnp.where((kbase + _iota((KT, 1), 0)) <= qpos, chosen, NEG)
        bias = jnp.concatenate([bias] * (pw // QBLK), axis=1)
        kb = k_ref[0, pl.ds(kbase, KT), :]
        vb = vts_ref[kt]
        out = []
        for c in range(n_pair):
            m_old, l_old, acc = carry[c]
            s = jnp.dot(kb, qst_bf[:, c * pw:(c + 1) * pw], preferred_element_type=F32) + bias
            m_new = jnp.maximum(m_old, jnp.max(s, axis=0, keepdims=True))
            p = jnp.exp2(s - m_new)
            alpha = jnp.exp2(m_old - m_new)
            l_new = alpha * l_old + jnp.sum(p, axis=0, keepdims=True)
            acc_new = alpha * acc + jnp.dot(vb, p.astype(BF16), preferred_element_type=F32)
            out.append((m_new, l_new, acc_new))
        return tuple(out)

    init = tuple((jnp.full((1, pw), NEG, F32), jnp.zeros((1, pw), F32), jnp.zeros((HEAD_DIM, pw), F32))
                 for _ in range(n_pair))
    n_tiles = (i + KT // QBLK) // (KT // QBLK)
    fin = lax.fori_loop(0, n_tiles, sel_tile, init)
    o_slc = jnp.concatenate([acc / l_c for _, l_c, acc in fin], axis=1)

    gate = _sigmoid(g_ref[0]).T
    outs = []
    for h in range(N_HEADS):
        cs = slice(h * QBLK, (h + 1) * QBLK)
        outs.append(gate[3 * h:3 * h + 1, :] * o_cmp[:, cs] + gate[3 * h + 1:3 * h + 2, :] * o_slc[:, cs]
                    + gate[3 * h + 2:3 * h + 3, :] * o_win[:, cs])
    o_ref[0] = jnp.concatenate(outs, axis=0).T * _silu(z_ref[0])


def _mixer_d(proj, qtabs, ktabs, ctabs, w1bd, pe2, w2bd, overlap):
    bsz, seq, _ = proj.shape
    blk = lambda name: _new_offset(name) // GROUP_W
    full = lambda shape: pl.BlockSpec(shape, lambda b, i: (0,) * len(shape))
    return pl.pallas_call(
        _mixer_d_body,
        grid=(bsz, seq // QBLK),
        in_specs=[pl.BlockSpec((1, QBLK, GROUP_W), lambda b, i: (b, i, blk('d_q'))),
                  pl.BlockSpec((1, QBLK, GROUP_W), lambda b, i: (b, i, blk('d_z'))),
                  pl.BlockSpec((1, seq, 128), lambda b, i: (b, 0, _new_offset('d_ksv') // 128)),
                  pl.BlockSpec((1, seq, 128), lambda b, i: (b, 0, _new_offset('d_kwv') // 128)),
                  pl.BlockSpec((1, seq, 128), lambda b, i: (b, 0, _new_offset('d_kvc') // 128)),
                  pl.BlockSpec((1, QBLK, 128), lambda b, i: (b, i, _new_offset('d_g') // 128)),
                  full((seq, GROUP_W)), full((seq, GROUP_W)), full((seq, GROUP_W)),
                  full((seq, 128)), full((seq, 128)), full((seq, 128)),
                  full((N_CMP_PAD, 128)), full((N_CMP_PAD, 128)), full((N_CMP_PAD, 128)),
                  full((CMP_LEN, 128, 128)), full((CMP_LEN, 128)), full((128, 128)),
                  full((seq // SLC_LEN, N_CMP_PAD))],
        out_specs=pl.BlockSpec((1, QBLK, GROUP_W), lambda b, i: (b, i, 0)),
        out_shape=jax.ShapeDtypeStruct((bsz, seq, GROUP_W), F32),
        scratch_shapes=[pltpu.VMEM((2, seq, HEAD_DIM), BF16),
                        pltpu.VMEM((seq // KT, HEAD_DIM, KT), BF16),
                        pltpu.VMEM((seq // QBLK, HEAD_DIM, QBLK), BF16),
                        pltpu.VMEM((N_CMP_PAD, HEAD_DIM), F32),
                        pltpu.VMEM((HEAD_DIM, N_CMP_PAD), BF16),
                        pltpu.VMEM((seq // KT, KT // SLC_LEN, QBLK), F32)],
        compiler_params=_cparams(("arbitrary", "arbitrary")),
        name="mixer_d",
    )(proj, proj, proj, proj, proj, proj, *qtabs, *ktabs, *ctabs, w1bd, pe2, w2bd, overlap)


def _cmp_params(pe, w1, w2):
    w1 = w1.reshape(2, CMP_LEN, HEAD_DIM, HEAD_DIM)
    z = jnp.zeros((CMP_LEN, HEAD_DIM, HEAD_DIM), F32)
    w1bd = jnp.concatenate([jnp.concatenate([w1[0], z], axis=2), jnp.concatenate([z, w1[1]], axis=2)], axis=1)
    z2 = jnp.zeros((HEAD_DIM, HEAD_DIM), F32)
    w2bd = jnp.concatenate([jnp.concatenate([w2[0], z2], axis=1), jnp.concatenate([z2, w2[1]], axis=1)], axis=0)
    pe2 = jnp.concatenate([pe[0], pe[1]], axis=1)
    return w1bd, pe2, w2bd


def _overlap_matrix(seq):
    n_cmp = (seq - CMP_LEN) // CMP_STRIDE + 1
    starts = np.arange(N_CMP_PAD) * CMP_STRIDE
    blk = np.arange(seq // SLC_LEN)
    ov = ((starts[None, :] < (blk[:, None] + 1) * SLC_LEN) & (starts[None, :] + CMP_LEN > blk[:, None] * SLC_LEN))
    ov = ov & (np.arange(N_CMP_PAD)[None, :] < n_cmp)
    return jnp.asarray(ov.astype(np.float32))


def kernel(x, ln0_g, ln0_b, w_in, b_in, a_sinks, b_conv_w, b_conv_b, c_lb, c_norm_g, d_cmp_pe, d_cmp_w1, d_cmp_w2,
           w_out, ln_g, ln_b):
    bsz, seq, d = x.shape
    pos = jnp.arange(seq)
    qtabs = _rope_tables(pos, GROUP_W, 128)
    ktabs = _rope_tables(pos, 128, HEAD_DIM)
    ctabs = _rope_tables(jnp.arange(N_CMP_PAD) * CMP_STRIDE + (CMP_LEN - 1), 128, HEAD_DIM)
    overlap = _overlap_matrix(seq)

    h = _layer_norm(x.reshape(bsz * seq, d), ln0_g, ln0_b)
    for l in range(DEPTH):
        w_p, b_p = _prep_in_weights(w_in, b_in, l)
        proj = _in_proj(h, w_p, b_p).reshape(bsz, seq, N_PROJ)
        y_a = _mixer_a(proj, a_sinks[l], qtabs)
        y_b = _mixer_b(proj, b_conv_w[l], b_conv_b[l])
        y_c = _mixer_c(proj, c_lb, c_norm_g[l], l)
        w1bd, pe2, w2bd = _cmp_params(d_cmp_pe[l], d_cmp_w1[l], d_cmp_w2[l])
        y_d = _mixer_d(proj, qtabs, ktabs, ctabs, w1bd, pe2, w2bd, overlap)
        h = _out_proj((y_a, y_b, y_c, y_d), h, w_out[l].astype(BF16), ln_g[l], ln_b[l])
    return h.reshape(bsz, seq, d)
```

```python
import functools

import numpy as np
import jax
import jax.numpy as jnp
from jax import lax
from jax.experimental import pallas as pl
from jax.experimental.pallas import tpu as pltpu

F32 = jnp.float32
BF16 = jnp.bfloat16

D_MODEL = 1024
DEPTH = 2
HEAD_DIM = 64
HALF = HEAD_DIM // 2
N_HEADS = 4
GROUP_W = N_HEADS * HEAD_DIM
ROPE_THETA = 10000.0
NEG = -1e30
LN_EPS = 1e-5
A_WINDOW = 128
A_GROUP = 2
A_SUB = 4
LOG2E = 1.4426950408889634
PRO_ROWS = 512
B_CONV = 4
C_CHUNK = 64
C_SUB = 16
C_STEP = 128
C_EPS = 1e-6
CMP_LEN = 32
CMP_STRIDE = 16
SLC_LEN = 64
N_SEL = 8
NSA_WINDOW = 512
FORCE_SCORE = 1e6
DN_ALPHA = (2.0 * DEPTH) ** 0.25
QBLK = 128
SCALE = HEAD_DIM ** -0.5

VMEM_LIMIT = 56 * 1024 * 1024

LANES = 128

ORIG_SPLITS = (
    ('a_q', 256), ('a_k', 128), ('a_v', 128), ('a_z', 256),
    ('b_q', 256), ('b_k', 256), ('b_v', 256), ('b_if', 8), ('b_o', 256), ('b_z', 256),
    ('c_q', 256), ('c_f', 256), ('c_i', 256), ('c_z', 256),
    ('d_q', 256), ('d_kvc', 128), ('d_ksv', 128), ('d_kwv', 128), ('d_g', 12), ('d_z', 256),
)
N_COLS = sum(w for _, w in ORIG_SPLITS)
NEW_LAYOUT = (
    ('a_q', 256), ('a_k', 128), ('a_v', 128), ('a_z', 256), ('b_q', 256), ('b_k', 256), ('b_v', 256),
    ('b_o', 256), ('b_z', 256), ('c_q', 256), ('c_f', 256), ('c_i', 256), ('c_z', 256),
    ('d_q', 256), ('d_kvc', 128), ('d_ksv', 128), ('d_kwv', 128), ('b_if', 128), ('d_z', 256), ('d_g', 128),
)
N_PROJ = sum(w for _, w in NEW_LAYOUT)


def _offset(layout, name):
    off = 0
    for n, w in layout:
        if n == name:
            return off
        off += w
    raise KeyError(name)


def _new_offset(name):
    return _offset(NEW_LAYOUT, name)


def _permuted_cols(load):
    sizes = dict(ORIG_SPLITS)
    for name, width in NEW_LAYOUT:
        old, size, new = _offset(ORIG_SPLITS, name), sizes[name], _new_offset(name)
        lo = old // LANES * LANES
        if size == width:
            hi = min(-(-(old + size) // LANES) * LANES, N_COLS)
            yield new, load(lo, hi)[..., old - lo:old - lo + size]
        else:
            tile = load(lo, lo + LANES)
            if old != lo:
                tile = pltpu.roll(tile, LANES - (old - lo), axis=tile.ndim - 1)
            yield new, jnp.where(_iota(tile.shape, tile.ndim - 1) < size, tile, 0.0)


def _prep_body(w_ref, b_ref, wo_ref, bo_ref):
    for new, cols in _permuted_cols(lambda lo, hi: w_ref[0, :, lo:hi]):
        wo_ref[:, new:new + cols.shape[-1]] = cols.astype(BF16)
    for new, cols in _permuted_cols(lambda lo, hi: b_ref[0, :, lo:hi]):
        bo_ref[:, new:new + cols.shape[-1]] = cols


def _prep_in_weights(w_in, b_in, layer, rows=128):
    depth, d, n = w_in.shape
    return pl.pallas_call(
        _prep_body,
        grid=(d // rows,),
        in_specs=[pl.BlockSpec((1, rows, n), lambda r: (layer, r, 0)),
                  pl.BlockSpec((1, 1, n), lambda r: (layer, 0, 0))],
        out_specs=[pl.BlockSpec((rows, N_PROJ), lambda r: (r, 0)),
                   pl.BlockSpec((1, N_PROJ), lambda r: (0, 0))],
        out_shape=[jax.ShapeDtypeStruct((d, N_PROJ), BF16), jax.ShapeDtypeStruct((1, N_PROJ), F32)],
        compiler_params=_cparams(("arbitrary",)),
        name="prep_in_weights",
    )(w_in, b_in.reshape(depth, 1, n))


def _mm(a, b):
    return jnp.dot(a.astype(BF16), b.astype(BF16), preferred_element_type=F32)


def _mm_nt(a, b):
    return lax.dot_general(a.astype(BF16), b.astype(BF16), (((1,), (1,)), ((), ())),
                           preferred_element_type=F32)


def _mm_tn(a, b):
    return lax.dot_general(a.astype(BF16), b.astype(BF16), (((0,), (0,)), ((), ())),
                           preferred_element_type=F32)


def _dot_bf16(a, b):
    return jnp.dot(a, b, preferred_element_type=F32)


def _split_bf16(x, terms):
    out = []
    for _ in range(terms - 1):
        t = x.astype(BF16)
        out.append(t)
        x = x - t.astype(F32)
    out.append(x.astype(BF16))
    return out


def _mm_hi(a, b):
    ah, al = _split_bf16(a, 2)
    bh, bl = _split_bf16(b, 2)
    return _dot_bf16(ah, bh) + (_dot_bf16(ah, bl) + _dot_bf16(al, bh))


def _mm_sel(sel, x):
    sel = sel.astype(BF16)
    x1, x2, x3 = _split_bf16(x, 3)
    return _dot_bf16(sel, x1) + (_dot_bf16(sel, x2) + _dot_bf16(sel, x3))


def _sigmoid(x):
    return 1.0 / (1.0 + jnp.exp(-x))


def _silu(x):
    return x * _sigmoid(x)


def _log_sigmoid(x):
    return jnp.minimum(x, 0.0) - jnp.log1p(jnp.exp(-jnp.abs(x)))


def _rope(x, cos, sin_up, sin_dn):
    w = x.shape[-1]
    up = pltpu.roll(x, w - HALF, axis=1)
    dn = pltpu.roll(x, HALF, axis=1)
    return x * cos + up * sin_up + dn * sin_dn


def _rope_tables(pos, width, rot_lanes):
    inv = ROPE_THETA ** (-jnp.arange(HALF, dtype=F32) / HALF)
    ang = pos.astype(F32)[:, None] * inv[None, :]
    cos, sin = jnp.cos(ang), jnp.sin(ang)
    zero, one = jnp.zeros_like(sin), jnp.ones_like(cos)
    cos_h = jnp.concatenate([cos, cos], axis=1)
    up_h = jnp.concatenate([-sin, zero], axis=1)
    dn_h = jnp.concatenate([zero, sin], axis=1)
    id_c = jnp.concatenate([one, one], axis=1)
    id_s = jnp.concatenate([zero, zero], axis=1)
    cs, us, ds = [], [], []
    for l0 in range(0, width, HEAD_DIM):
        rot = (l0 % 128) < rot_lanes
        cs.append(cos_h if rot else id_c)
        us.append(up_h if rot else id_s)
        ds.append(dn_h if rot else id_s)
    return jnp.concatenate(cs, axis=1), jnp.concatenate(us, axis=1), jnp.concatenate(ds, axis=1)


def _iota(shape, dim):
    return lax.broadcasted_iota(jnp.int32, shape, dim)


def _cparams(sem):
    return pltpu.CompilerParams(dimension_semantics=sem, vmem_limit_bytes=VMEM_LIMIT)


def _ln_body(x_ref, g_ref, b_ref, o_ref):
    x = x_ref[...]
    mu = jnp.mean(x, axis=-1, keepdims=True)
    xc = x - mu
    var = jnp.mean(xc * xc, axis=-1, keepdims=True)
    o_ref[...] = xc * lax.rsqrt(var + LN_EPS) * g_ref[...] + b_ref[...]


def _layer_norm(x2, g, b, tm=512):
    m, d = x2.shape
    return pl.pallas_call(
        _ln_body,
        grid=(m // tm,),
        in_specs=[pl.BlockSpec((tm, d), lambda i: (i, 0)),
                  pl.BlockSpec((1, d), lambda i: (0, 0)),
                  pl.BlockSpec((1, d), lambda i: (0, 0))],
        out_specs=pl.BlockSpec((tm, d), lambda i: (i, 0)),
        out_shape=jax.ShapeDtypeStruct((m, d), F32),
        compiler_params=_cparams(("parallel",)),
        name="layer_norm",
    )(x2, g.reshape(1, d), b.reshape(1, d))


def _proj_body(x_ref, w_ref, b_ref, o_ref):
    o_ref[...] = jnp.dot(x_ref[...].astype(BF16), w_ref[...], preferred_element_type=F32) + b_ref[...]


def _in_proj(h2, w_bf, b, tm=256):
    m, d = h2.shape
    n = w_bf.shape[1]
    return pl.pallas_call(
        _proj_body,
        grid=(m // tm,),
        in_specs=[pl.BlockSpec((tm, d), lambda i: (i, 0)),
                  pl.BlockSpec((d, n), lambda i: (0, 0)),
                  pl.BlockSpec((1, n), lambda i: (0, 0))],
        out_specs=pl.BlockSpec((tm, n), lambda i: (i, 0)),
        out_shape=jax.ShapeDtypeStruct((m, n), F32),
        compiler_params=_cparams(("parallel",)),
        name="in_proj",
    )(h2, w_bf, b.reshape(1, n))


def _out_body(ya_ref, yb_ref, yc_ref, yd_ref, h_ref, w_ref, g_ref, b_ref, o_ref):
    y = _mm(ya_ref[...], w_ref[0:GROUP_W, :])
    y += _mm(yb_ref[...], w_ref[GROUP_W:2 * GROUP_W, :])
    y += _mm(yc_ref[...], w_ref[2 * GROUP_W:3 * GROUP_W, :])
    y += _mm(yd_ref[...], w_ref[3 * GROUP_W:4 * GROUP_W, :])
    r = DN_ALPHA * h_ref[...] + y
    mu = jnp.mean(r, axis=-1, keepdims=True)
    rc = r - mu
    var = jnp.mean(rc * rc, axis=-1, keepdims=True)
    o_ref[...] = rc * lax.rsqrt(var + LN_EPS) * g_ref[...] + b_ref[...]


def _out_proj(ys, h2, w_bf, g, b, tm=512):
    m, d = h2.shape
    yspec = pl.BlockSpec((tm, GROUP_W), lambda i: (i, 0))
    return pl.pallas_call(
        _out_body,
        grid=(m // tm,),
        in_specs=[yspec, yspec, yspec, yspec,
                  pl.BlockSpec((tm, d), lambda i: (i, 0)),
                  pl.BlockSpec((4 * GROUP_W, d), lambda i: (0, 0)),
                  pl.BlockSpec((1, d), lambda i: (0, 0)),
                  pl.BlockSpec((1, d), lambda i: (0, 0))],
        out_specs=pl.BlockSpec((tm, d), lambda i: (i, 0)),
        out_shape=jax.ShapeDtypeStruct((m, d), F32),
        compiler_params=_cparams(("parallel",)),
        name="out_proj",
    )(*[y.reshape(m, GROUP_W) for y in ys], h2, w_bf, g.reshape(1, d), b.reshape(1, d))


def _mixer_a_body(sink_ref, q_ref, kv_ref, z_ref, cos_ref, up_ref, dn_ref, o_ref, k_ref, vt_ref):
    i = pl.program_id(1)
    seq = kv_ref.shape[1]
    n_kv = N_HEADS // A_GROUP

    @pl.when(i == 0)
    def _():
        def rope_rows(c, carry):
            for u in range(PRO_ROWS // QBLK):
                t0 = c * (PRO_ROWS // QBLK) + u
                rows = pl.ds(pl.multiple_of(t0 * QBLK, QBLK), QBLK)
                kr = _rope(kv_ref[0, rows, 0:128], cos_ref[rows, 0:128], up_ref[rows, 0:128], dn_ref[rows, 0:128])
                v_t = kv_ref[0, rows, 128:256].T
                for g in range(n_kv):
                    k_ref[g, rows, :] = kr[:, g * HEAD_DIM:(g + 1) * HEAD_DIM].astype(BF16)
                    vt_ref[g, t0] = v_t[g * HEAD_DIM:(g + 1) * HEAD_DIM, :].astype(BF16)
            return carry
        lax.fori_loop(0, seq // PRO_ROWS, rope_rows, 0)

    n_blk = A_WINDOW // QBLK + 1
    span = n_blk * QBLK
    first_head = _iota((1, A_GROUP * QBLK), 1) < QBLK

    def query_block(u):
        blk_i = i * A_SUB + u
        sub = slice(u * QBLK, (u + 1) * QBLK)
        q0 = pl.multiple_of(blk_i * QBLK, QBLK)
        qrows = pl.ds(q0, QBLK)
        qr = _rope(q_ref[0, sub, :], cos_ref[qrows, :], up_ref[qrows, :], dn_ref[qrows, :]) * (SCALE * LOG2E)
        qt = qr.T
        kt0 = jnp.maximum(blk_i - A_WINDOW // QBLK, 0)
        k0 = pl.multiple_of(kt0 * QBLK, QBLK)
        rel = (q0 + _iota((1, QBLK), 1)) - (k0 + _iota((span, 1), 0))
        in_band = lax.bitcast_convert_type(rel, jnp.uint32) < A_WINDOW
        bias = jnp.concatenate([jnp.where(in_band, 0.0, NEG)] * A_GROUP, axis=1)
        outs = []
        for g in range(n_kv):
            heads = range(g * A_GROUP, (g + 1) * A_GROUP)
            qg = jnp.concatenate([qt[h * HEAD_DIM:(h + 1) * HEAD_DIM] for h in heads], axis=1).astype(BF16)
            sink = jnp.where(first_head, sink_ref[g * A_GROUP], sink_ref[g * A_GROUP + 1]) * LOG2E
            s = jnp.dot(k_ref[g, pl.ds(k0, span), :], qg, preferred_element_type=F32) + bias
            m = jnp.maximum(jnp.max(s, axis=0, keepdims=True), sink)
            p = jnp.exp2(s - m)
            l = jnp.sum(p, axis=0, keepdims=True) + jnp.exp2(sink - m)
            p = p.astype(BF16)
            o = jnp.zeros((HEAD_DIM, A_GROUP * QBLK), F32)
            for c in range(n_blk):
                o = o + jnp.dot(vt_ref[g, kt0 + c], p[c * QBLK:(c + 1) * QBLK], preferred_element_type=F32)
            o = o / l
            outs.extend(o[:, r * QBLK:(r + 1) * QBLK] for r in range(A_GROUP))
        o_ref[0, sub, :] = jnp.concatenate(outs, axis=0).T * _silu(z_ref[0, sub, :])

    for u in range(A_SUB):
        query_block(u)


def _mixer_a(proj, sinks, tabs):
    bsz, seq, _ = proj.shape
    cos, up, dn = tabs
    blk = lambda name: _new_offset(name) // GROUP_W
    tspec = pl.BlockSpec((seq, GROUP_W), lambda b, i: (0, 0))
    return pl.pallas_call(
        _mixer_a_body,
        grid=(bsz, seq // (A_SUB * QBLK)),
        in_specs=[pl.BlockSpec(memory_space=pltpu.SMEM),
                  pl.BlockSpec((1, A_SUB * QBLK, GROUP_W), lambda b, i: (b, i, blk('a_q'))),
                  pl.BlockSpec((1, seq, GROUP_W), lambda b, i: (b, 0, blk('a_k'))),
                  pl.BlockSpec((1, A_SUB * QBLK, GROUP_W), lambda b, i: (b, i, blk('a_z'))),
                  tspec, tspec, tspec],
        out_specs=pl.BlockSpec((1, A_SUB * QBLK, GROUP_W), lambda b, i: (b, i, 0)),
        out_shape=jax.ShapeDtypeStruct((bsz, seq, GROUP_W), F32),
        scratch_shapes=[pltpu.VMEM((N_HEADS // A_GROUP, seq, HEAD_DIM), BF16),
                        pltpu.VMEM((N_HEADS // A_GROUP, seq // QBLK, HEAD_DIM, QBLK), BF16)],
        compiler_params=_cparams(("arbitrary", "arbitrary")),
        name="mixer_a",
    )(sinks, proj, proj, proj, cos, up, dn)


B_CHUNK = 128
B_AUG = HEAD_DIM + 16


def _conv_silu(x, prev, w, b):
    row8 = _iota((8, x.shape[1]), 0)
    acc = b + w[B_CONV - 1:B_CONV, :] * x
    for s in range(1, B_CONV):
        xs = pltpu.roll(x, s, axis=0)
        ps = pltpu.roll(prev, s, axis=0)
        shifted = jnp.concatenate([jnp.where(row8 < s, ps, xs[0:8]), xs[8:]], axis=0)
        acc = acc + w[B_CONV - 1 - s:B_CONV - s, :] * shifted
    return _silu(acc)


def _mixer_b_body(xq_ref, xk_ref, v_ref, og_ref, z_ref, if_ref, cw_ref, cb_ref, o_ref, xprev_ref, c_ref, m_ref):
    c = pl.program_id(1)
    L = B_CHUNK

    @pl.when(c == 0)
    def _():
        xprev_ref[...] = jnp.zeros_like(xprev_ref)
        c_ref[...] = jnp.zeros_like(c_ref)
        m_ref[...] = jnp.zeros_like(m_ref)

    xq = xq_ref[0]
    xk = xk_ref[0]
    qc = _conv_silu(xq, xprev_ref[:, 0:GROUP_W], cw_ref[:, 0:GROUP_W], cb_ref[:, 0:GROUP_W]) * SCALE
    kc = _conv_silu(xk, xprev_ref[:, GROUP_W:], cw_ref[:, GROUP_W:], cb_ref[:, GROUP_W:])
    xprev_ref[:, 0:GROUP_W] = xq[L - 8:L, :]
    xprev_ref[:, GROUP_W:] = xk[L - 8:L, :]
    q_t = qc.T
    v_t = v_ref[0].T

    gates = if_ref[0]
    logf = _log_sigmoid(gates)
    tri = (_iota((L, L), 1) <= _iota((L, L), 0)).astype(F32)
    bcum = _mm_sel(tri, logf)
    gates_t = gates.T
    bcum_t = bcum.T
    key_first = _iota((L, L), 0) <= _iota((L, L), 1)
    half_of_lane = _iota((1, 128), 1) // HEAD_DIM
    ones_rows = jnp.ones((B_AUG - HEAD_DIM, L), F32)
    outs = []
    for h in range(N_HEADS):
        pair = slice((h // 2) * 128, (h // 2 + 1) * 128)
        own = half_of_lane == (h % 2)
        k_pair = kc[:, pair]
        q_pair_t = q_t[pair, :]
        i_row = gates_t[h:h + 1, :]
        b_row = bcum_t[N_HEADS + h:N_HEADS + h + 1, :]
        c_col = gates[:, h:h + 1] - bcum[:, N_HEADS + h:N_HEADS + h + 1]
        m_prev = m_ref[h:h + 1, 0:1]
        dmat = jnp.where(key_first, b_row + c_col, NEG)
        inter = b_row + m_prev
        m_t = jnp.maximum(jnp.max(dmat, axis=0, keepdims=True), inter)
        smat = _mm(jnp.where(own, k_pair, 0.0), q_pair_t) * jnp.exp(dmat - m_t)
        vaug_t = jnp.concatenate([v_t[h * HEAD_DIM:(h + 1) * HEAD_DIM, :], ones_rows], axis=0)
        cst = c_ref[h]
        tot = _mm(vaug_t, smat) + jnp.exp(inter - m_t) * _mm(cst, q_pair_t)
        num = tot[0:HEAD_DIM, :]
        den = tot[HEAD_DIM:HEAD_DIM + 1, :]
        outs.append(num / jnp.maximum(jnp.abs(den), jnp.exp(-m_t)))
        b_last = b_row[:, L - 1:L]
        d_row = b_last - b_row + i_row
        m_new = jnp.maximum(b_last + m_prev, jnp.max(d_row, axis=1, keepdims=True))
        w_row = jnp.exp(d_row - m_new)
        decay = jnp.exp(b_last + m_prev - m_new)
        c_ref[h] = decay * cst + jnp.where(own, _mm(vaug_t * w_row, k_pair), 0.0)
        m_ref[h:h + 1, :] = jnp.broadcast_to(m_new, (1, 128))
    o_ref[0] = _sigmoid(og_ref[0]) * jnp.concatenate(outs, axis=0).T * _silu(z_ref[0])


def _mixer_b(proj, conv_w, conv_b):
    bsz, seq, _ = proj.shape
    L = B_CHUNK
    blk = lambda name: _new_offset(name) // GROUP_W
    return pl.pallas_call(
        _mixer_b_body,
        grid=(bsz, seq // L),
        in_specs=[pl.BlockSpec((1, L, GROUP_W), lambda b, c: (b, c, blk('b_q'))),
                  pl.BlockSpec((1, L, GROUP_W), lambda b, c: (b, c, blk('b_k'))),
                  pl.BlockSpec((1, L, GROUP_W), lambda b, c: (b, c, blk('b_v'))),
                  pl.BlockSpec((1, L, GROUP_W), lambda b, c: (b, c, blk('b_o'))),
                  pl.BlockSpec((1, L, GROUP_W), lambda b, c: (b, c, blk('b_z'))),
                  pl.BlockSpec((1, L, 128), lambda b, c: (b, c, _new_offset('b_if') // 128)),
                  pl.BlockSpec((B_CONV, 2 * GROUP_W), lambda b, c: (0, 0)),
                  pl.BlockSpec((1, 2 * GROUP_W), lambda b, c: (0, 0))],
        out_specs=pl.BlockSpec((1, L, GROUP_W), lambda b, c: (b, c, 0)),
        out_shape=jax.ShapeDtypeStruct((bsz, seq, GROUP_W), F32),
        scratch_shapes=[pltpu.VMEM((8, 2 * GROUP_W), F32),
                        pltpu.VMEM((N_HEADS, B_AUG, 128), F32),
                        pltpu.VMEM((8, 128), F32)],
        compiler_params=_cparams(("arbitrary", "arbitrary")),
        name="mixer_b",
    )(proj, proj, proj, proj, proj, proj, conv_w, conv_b.reshape(1, -1))


def _mixer_c_body(layer, q_ref, f_ref, i_ref, z_ref, lb_ref, gn_ref, tri_ref, o_ref, st_ref):
    c = pl.program_id(1)
    L = C_CHUNK
    nsub = L // C_SUB

    @pl.when(c == 0)
    def _():
        st_ref[...] = jnp.zeros_like(st_ref)

    raw = lb_ref[...]
    ex = jnp.exp(raw - jnp.max(raw, axis=0, keepdims=True))
    if layer == 0:
        lb = jnp.zeros((1, GROUP_W), F32)
    else:
        lb = jnp.sum(ex[1:layer + 1], axis=0, keepdims=True) / jnp.sum(ex, axis=0, keepdims=True)
    tri = (_iota((L, L), 1) <= _iota((L, L), 0)).astype(F32)
    head_of_lane = _iota((1, GROUP_W), 1) // HEAD_DIM
    same_head = ((_iota((GROUP_W, GROUP_W), 0) // HEAD_DIM)
                 == (_iota((GROUP_W, GROUP_W), 1) // HEAD_DIM))
    ones_bd = same_head.astype(BF16)

    def chunk(rows, st):
        f = lb + (1.0 - lb) * _sigmoid(f_ref[0, rows, :])
        logf = jnp.log(f)
        k = 1.0 - f
        q = _silu(q_ref[0, rows, :])
        v = i_ref[0, rows, :]
        a = _mm_sel(tri, logf) * LOG2E
        o = _mm_nt(q * jnp.exp2(a), st)

        pieces = [o[0:C_SUB]]
        for i in range(1, nsub):
            r0 = i * C_SUB
            piv = a[r0 - 1:r0, :]
            qt = q[r0:r0 + C_SUB] * jnp.exp2(a[r0:r0 + C_SUB] - piv)
            kt = k[0:r0] * jnp.exp2(piv - a[0:r0])
            qst = jnp.concatenate([jnp.where(head_of_lane == h, qt, 0.0) for h in range(N_HEADS)], axis=0)
            r = _mm(_mm_nt(qst, kt), v[0:r0])
            acc = o[r0:r0 + C_SUB]
            for h in range(N_HEADS):
                acc = acc + jnp.where(head_of_lane == h, r[h * C_SUB:(h + 1) * C_SUB], 0.0)
            pieces.append(acc)
        o = jnp.concatenate(pieces, axis=0)

        a3 = a.reshape(nsub, C_SUB, GROUP_W)
        q3 = q.reshape(nsub, C_SUB, GROUP_W)
        k3 = k.reshape(nsub, C_SUB, GROUP_W)
        v3 = v.reshape(nsub, C_SUB, GROUP_W)
        ps = []
        for s in range(C_SUB):
            e = jnp.exp2(a3 - a3[:, s:s + 1, :] + tri_ref[s])
            ps.append((q3 * k3[:, s:s + 1, :] * e).reshape(L, GROUP_W).astype(BF16))
        w = jnp.dot(jnp.concatenate(ps, axis=0), ones_bd, preferred_element_type=F32)
        for s in range(C_SUB):
            vs = jnp.broadcast_to(v3[:, s:s + 1, :], (nsub, C_SUB, GROUP_W)).reshape(L, GROUP_W)
            o = o + w[s * L:(s + 1) * L] * vs

        a_last = a[L - 1:L, :]
        kdec = k * jnp.exp2(a_last - a)
        st_new = st * jnp.exp2(a_last) + jnp.where(same_head, _mm_tn(v, kdec), 0.0)

        ms = jnp.dot((o * o).astype(BF16), ones_bd, preferred_element_type=F32) * (1.0 / HEAD_DIM)
        o = o * lax.rsqrt(ms + C_EPS) * gn_ref[...]
        o_ref[0, rows, :] = o * _silu(z_ref[0, rows, :])
        return st_new

    st = st_ref[...]
    for u in range(q_ref.shape[1] // L):
        st = chunk(slice(u * L, (u + 1) * L), st)
    st_ref[...] = st


def _mixer_c(proj, c_lb, norm_g, layer):
    bsz, seq, _ = proj.shape
    L = C_STEP
    blk = lambda name: _new_offset(name) // GROUP_W
    spec = lambda name: pl.BlockSpec((1, L, GROUP_W), lambda b, c: (b, c, blk(name)))
    vec = pl.BlockSpec((1, GROUP_W), lambda b, c: (0, 0))
    t_ge_s = np.arange(C_SUB)[None, :, None] >= np.arange(C_SUB)[:, None, None]
    causal_bias = np.broadcast_to(np.where(t_ge_s, 0.0, NEG), (C_SUB, C_SUB, GROUP_W)).astype(np.float32)
    return pl.pallas_call(
        functools.partial(_mixer_c_body, layer),
        grid=(bsz, seq // L),
        in_specs=[spec('c_q'), spec('c_f'), spec('c_i'), spec('c_z'),
                  pl.BlockSpec((DEPTH, GROUP_W), lambda b, c: (0, 0)), vec,
                  pl.BlockSpec((C_SUB, C_SUB, GROUP_W), lambda b, c: (0, 0, 0))],
        out_specs=pl.BlockSpec((1, L, GROUP_W), lambda b, c: (b, c, 0)),
        out_shape=jax.ShapeDtypeStruct((bsz, seq, GROUP_W), F32),
        scratch_shapes=[pltpu.VMEM((GROUP_W, GROUP_W), F32)],
        compiler_params=_cparams(("arbitrary", "arbitrary")),
        name="mixer_c",
    )(proj, proj, proj, proj, c_lb, norm_g.reshape(1, -1), jnp.asarray(causal_bias))


N_CMP_PAD = 128
KT = 512
D_QBLK = 256


def _mixer_d_body(q_ref, z_ref, ksv_ref, kwv_ref, cin_ref, g_ref, cos_ref, up_ref, dn_ref, kcos_ref, kup_ref,
                  kdn_ref, ccos_ref, cup_ref, cdn_ref, w1_ref, pe_ref, w2_ref, ov_ref, o_ref,
                  k_ref, vts_ref, vtw_ref, kc_ref, vct_ref, sel_ref):
    QBLK = D_QBLK
    i = pl.program_id(1)
    seq = ksv_ref.shape[1]
    half_cmp = CMP_LEN // 2

    @pl.when(i == 0)
    def _():
        def rope_rows(c, carry):
            for u in range(PRO_ROWS // KT):
                t0 = c * (PRO_ROWS // KT) + u
                rows = pl.ds(pl.multiple_of(t0 * KT, KT), KT)
                kvs = _rope(ksv_ref[0, rows, :], kcos_ref[rows, :], kup_ref[rows, :], kdn_ref[rows, :])
                kvw = _rope(kwv_ref[0, rows, :], kcos_ref[rows, :], kup_ref[rows, :], kdn_ref[rows, :])
                k_ref[0, rows, :] = kvs[:, 0:HEAD_DIM].astype(BF16)
                k_ref[1, rows, :] = kvw[:, 0:HEAD_DIM].astype(BF16)
                vts_ref[t0] = kvs.T[HEAD_DIM:2 * HEAD_DIM, :].astype(BF16)
                vwt = kvw.T[HEAD_DIM:2 * HEAD_DIM, :].astype(BF16)
                for w in range(KT // QBLK):
                    vtw_ref[t0 * (KT // QBLK) + w] = vwt[:, w * QBLK:(w + 1) * QBLK]
            return carry
        lax.fori_loop(0, seq // PRO_ROWS, rope_rows, 0)
        u0 = jnp.zeros((N_CMP_PAD, 128), F32)
        u1 = jnp.zeros((N_CMP_PAD, 128), F32)
        for r in range(half_cmp):
            xr = cin_ref[0, pl.ds(r, N_CMP_PAD, stride=CMP_STRIDE), :]
            u0 = u0 + _mm_hi(xr + pe_ref[r:r + 1, :], w1_ref[r])
            u1 = u1 + _mm_hi(xr + pe_ref[half_cmp + r:half_cmp + r + 1, :], w1_ref[half_cmp + r])
        pre = u0 + pltpu.roll(u1, N_CMP_PAD - 1, axis=0)
        cv = _rope(_mm_hi(_silu(pre), w2_ref[...]), ccos_ref[...], cup_ref[...], cdn_ref[...])
        kc_ref[...] = cv[:, 0:HEAD_DIM]
        vct_ref[...] = cv.T[HEAD_DIM:2 * HEAD_DIM, :].astype(BF16)

    q0 = pl.multiple_of(i * QBLK, QBLK)
    qrows = pl.ds(q0, QBLK)
    qr = _rope(q_ref[0], cos_ref[qrows, :], up_ref[qrows, :], dn_ref[qrows, :]) * (SCALE * LOG2E)
    qt = qr.T
    qst = jnp.concatenate([qt[h * HEAD_DIM:(h + 1) * HEAD_DIM] for h in range(N_HEADS)], axis=1)
    qst_bf = qst.astype(BF16)
    cols4 = N_HEADS * QBLK
    qpos = q0 + _iota((1, QBLK), 1)

    def lanes4(x):
        return jnp.concatenate([x] * N_HEADS, axis=1)

    n_win = NSA_WINDOW // QBLK + 1
    span = n_win * QBLK
    kt0 = jnp.maximum(i - NSA_WINDOW // QBLK, 0)
    k0 = pl.multiple_of(kt0 * QBLK, QBLK)
    rel = qpos - (k0 + _iota((span, 1), 0))
    in_band = lax.bitcast_convert_type(rel, jnp.uint32) < NSA_WINDOW
    s = jnp.dot(k_ref[1, pl.ds(k0, span), :], qst_bf, preferred_element_type=F32)
    s = s + lanes4(jnp.where(in_band, 0.0, NEG))
    p = jnp.exp2(s - jnp.max(s, axis=0, keepdims=True))
    l_w = jnp.sum(p, axis=0, keepdims=True)
    p = p.astype(BF16)
    o_win = jnp.zeros((HEAD_DIM, cols4), F32)
    for c in range(n_win):
        o_win = o_win + jnp.dot(vtw_ref[kt0 + c], p[c * QBLK:(c + 1) * QBLK], preferred_element_type=F32)
    o_win = o_win / l_w

    sc = _mm_hi(kc_ref[...], qst)
    ends = _iota((N_CMP_PAD, 1), 0) * CMP_STRIDE + (CMP_LEN - 1)
    scm = sc + lanes4(jnp.where(ends <= qpos, 0.0, NEG))
    e = jnp.exp2(scm - jnp.max(scm, axis=0, keepdims=True))
    l = jnp.sum(e, axis=0, keepdims=True)
    any_valid = lanes4(jnp.where(qpos >= CMP_LEN - 1, 1.0, 0.0))
    pc = e * (any_valid / l)
    o_cmp = jnp.dot(vct_ref[...], pc.astype(BF16), preferred_element_type=F32)
    psum = pc[:, 0:QBLK] + pc[:, QBLK:2 * QBLK] + pc[:, 2 * QBLK:3 * QBLK] + pc[:, 3 * QBLK:4 * QBLK]

    n_slc = seq // SLC_LEN
    imp = _mm_sel(ov_ref[...], psum)
    jl = _iota((n_slc, 1), 0)
    cur = jnp.right_shift(qpos, SLC_LEN.bit_length() - 1)
    imp = jnp.where(jl == 0, FORCE_SCORE, imp)
    imp = jnp.where(jl == cur, FORCE_SCORE, imp)
    imp = jnp.where(jl == cur - 1, FORCE_SCORE, imp)
    imp = jnp.where(jl <= cur, imp, NEG)
    rank = jnp.zeros((n_slc, QBLK), F32)
    for jp in range(n_slc):
        cj = imp[jp:jp + 1, :]
        tie = jnp.where(jl > jp, 1.0, 0.0)
        rank = rank + jnp.where(cj > imp, 1.0, jnp.where(cj == imp, tie, 0.0))
    sel_bias = jnp.where(rank < N_SEL, 0.0, NEG)
    blocks_per_tile = KT // SLC_LEN
    for t in range(n_slc // blocks_per_tile):
        sel_ref[t] = sel_bias[t * blocks_per_tile:(t + 1) * blocks_per_tile, :]

    n_pair = 1
    pw = cols4 // n_pair

    def sel_tile(kt, carry):
        kbase = pl.multiple_of(kt * KT, KT)
        picks = sel_ref[kt]
        chosen = jnp.concatenate([jnp.broadcast_to(picks[j:j + 1, :], (SLC_LEN, QBLK))
                                  for j in range(blocks_per_tile)], axis=0)
        bias = jnp.where((kbase + _iota((KT, 1), 0)) <= qpos, chosen, NEG)
        bias = jnp.concatenate([bias] * (pw // QBLK), axis=1)
        kb = k_ref[0, pl.ds(kbase, KT), :]
        vb = vts_ref[kt]
        out = []
        for c in range(n_pair):
            m_old, l_old, acc = carry[c]
            s = jnp.dot(kb, qst_bf[:, c * pw:(c + 1) * pw], preferred_element_type=F32) + bias
            m_new = jnp.maximum(m_old, jnp.max(s, axis=0, keepdims=True))
            p = jnp.exp2(s - m_new)
            alpha = jnp.exp2(m_old - m_new)
            l_new = alpha * l_old + jnp.sum(p, axis=0, keepdims=True)
            acc_new = alpha * acc + jnp.dot(vb, p.astype(BF16), preferred_element_type=F32)
            out.append((m_new, l_new, acc_new))
        return tuple(out)

    init = tuple((jnp.full((1, pw), NEG, F32), jnp.zeros((1, pw), F32), jnp.zeros((HEAD_DIM, pw), F32))
                 for _ in range(n_pair))
    n_tiles = (i + KT // QBLK) // (KT // QBLK)
    fin = lax.fori_loop(0, n_tiles, sel_tile, init)
    o_slc = jnp.concatenate([acc / l_c for _, l_c, acc in fin], axis=1)

    gate = _sigmoid(g_ref[0]).T
    outs = []
    for h in range(N_HEADS):
        cs = slice(h * QBLK, (h + 1) * QBLK)
        outs.append(gate[3 * h:3 * h + 1, :] * o_cmp[:, cs] + gate[3 * h + 1:3 * h + 2, :] * o_slc[:, cs]
                    + gate[3 * h + 2:3 * h + 3, :] * o_win[:, cs])
    o_ref[0] = jnp.concatenate(outs, axis=0).T * _silu(z_ref[0])


def _mixer_d(proj, qtabs, ktabs, ctabs, w1bd, pe2, w2bd, overlap):
    QBLK = D_QBLK
    bsz, seq, _ = proj.shape
    blk = lambda name: _new_offset(name) // GROUP_W
    full = lambda shape: pl.BlockSpec(shape, lambda b, i: (0,) * len(shape))
    return pl.pallas_call(
        _mixer_d_body,
        grid=(bsz, seq // QBLK),
        in_specs=[pl.BlockSpec((1, QBLK, GROUP_W), lambda b, i: (b, i, blk('d_q'))),
                  pl.BlockSpec((1, QBLK, GROUP_W), lambda b, i: (b, i, blk('d_z'))),
                  pl.BlockSpec((1, seq, 128), lambda b, i: (b, 0, _new_offset('d_ksv') // 128)),
                  pl.BlockSpec((1, seq, 128), lambda b, i: (b, 0, _new_offset('d_kwv') // 128)),
                  pl.BlockSpec((1, seq, 128), lambda b, i: (b, 0, _new_offset('d_kvc') // 128)),
                  pl.BlockSpec((1, QBLK, 128), lambda b, i: (b, i, _new_offset('d_g') // 128)),
                  full((seq, GROUP_W)), full((seq, GROUP_W)), full((seq, GROUP_W)),
                  full((seq, 128)), full((seq, 128)), full((seq, 128)),
                  full((N_CMP_PAD, 128)), full((N_CMP_PAD, 128)), full((N_CMP_PAD, 128)),
                  full((CMP_LEN, 128, 128)), full((CMP_LEN, 128)), full((128, 128)),
                  full((seq // SLC_LEN, N_CMP_PAD))],
        out_specs=pl.BlockSpec((1, QBLK, GROUP_W), lambda b, i: (b, i, 0)),
        out_shape=jax.ShapeDtypeStruct((bsz, seq, GROUP_W), F32),
        scratch_shapes=[pltpu.VMEM((2, seq, HEAD_DIM), BF16),
                        pltpu.VMEM((seq // KT, HEAD_DIM, KT), BF16),
                        pltpu.VMEM((seq // QBLK, HEAD_DIM, QBLK), BF16),
                        pltpu.VMEM((N_CMP_PAD, HEAD_DIM), F32),
                        pltpu.VMEM((HEAD_DIM, N_CMP_PAD), BF16),
                        pltpu.VMEM((seq // KT, KT // SLC_LEN, QBLK), F32)],
        compiler_params=_cparams(("arbitrary", "arbitrary")),
        name="mixer_d",
    )(proj, proj, proj, proj, proj, proj, *qtabs, *ktabs, *ctabs, w1bd, pe2, w2bd, overlap)


def _cmp_params(pe, w1, w2):
    w1 = w1.reshape(2, CMP_LEN, HEAD_DIM, HEAD_DIM)
    z = jnp.zeros((CMP_LEN, HEAD_DIM, HEAD_DIM), F32)
    w1bd = jnp.concatenate([jnp.concatenate([w1[0], z], axis=2), jnp.concatenate([z, w1[1]], axis=2)], axis=1)
    z2 = jnp.zeros((HEAD_DIM, HEAD_DIM), F32)
    w2bd = jnp.concatenate([jnp.concatenate([w2[0], z2], axis=1), jnp.concatenate([z2, w2[1]], axis=1)], axis=0)
    pe2 = jnp.concatenate([pe[0], pe[1]], axis=1)
    return w1bd, pe2, w2bd


def _overlap_matrix(seq):
    n_cmp = (seq - CMP_LEN) // CMP_STRIDE + 1
    starts = np.arange(N_CMP_PAD) * CMP_STRIDE
    blk = np.arange(seq // SLC_LEN)
    ov = ((starts[None, :] < (blk[:, None] + 1) * SLC_LEN) & (starts[None, :] + CMP_LEN > blk[:, None] * SLC_LEN))
    ov = ov & (np.arange(N_CMP_PAD)[None, :] < n_cmp)
    return jnp.asarray(ov.astype(np.float32))


def kernel(x, ln0_g, ln0_b, w_in, b_in, a_sinks, b_conv_w, b_conv_b, c_lb, c_norm_g, d_cmp_pe, d_cmp_w1, d_cmp_w2,
           w_out, ln_g, ln_b):
    bsz, seq, d = x.shape
    pos = jnp.arange(seq)
    qtabs = _rope_tables(pos, GROUP_W, 128)
    ktabs = _rope_tables(pos, 128, HEAD_DIM)
    ctabs = _rope_tables(jnp.arange(N_CMP_PAD) * CMP_STRIDE + (CMP_LEN - 1), 128, HEAD_DIM)
    overlap = _overlap_matrix(seq)

    h = _layer_norm(x.reshape(bsz * seq, d), ln0_g, ln0_b)
    for l in range(DEPTH):
        w_p, b_p = _prep_in_weights(w_in, b_in, l)
        proj = _in_proj(h, w_p, b_p).reshape(bsz, seq, N_PROJ)
        y_a = _mixer_a(proj, a_sinks[l], qtabs)
        y_b = _mixer_b(proj, b_conv_w[l], b_conv_b[l])
        y_c = _mixer_c(proj, c_lb, c_norm_g[l], l)
        w1bd, pe2, w2bd = _cmp_params(d_cmp_pe[l], d_cmp_w1[l], d_cmp_w2[l])
        y_d = _mixer_d(proj, qtabs, ktabs, ctabs, w1bd, pe2, w2bd, overlap)
        h = _out_proj((y_a, y_b, y_c, y_d), h, w_out[l].astype(BF16), ln_g[l], ln_b[l])
    return h.reshape(bsz, seq, d)
```

```python
import functools

import numpy as np
import jax
import jax.numpy as jnp
from jax import lax
from jax.experimental import pallas as pl
from jax.experimental.pallas import tpu as pltpu

F32 = jnp.float32
BF16 = jnp.bfloat16

D_MODEL = 1024
DEPTH = 2
HEAD_DIM = 64
HALF = HEAD_DIM // 2
N_HEADS = 4
GROUP_W = N_HEADS * HEAD_DIM
ROPE_THETA = 10000.0
NEG = -1e30
LN_EPS = 1e-5
A_WINDOW = 128
A_GROUP = 2
A_SUB = 4
LOG2E = 1.4426950408889634
PRO_ROWS = 512
B_CONV = 4
C_CHUNK = 64
C_SUB = 8
C_STEP = 128
C_EPS = 1e-6
CMP_LEN = 32
CMP_STRIDE = 16
SLC_LEN = 64
N_SEL = 8
NSA_WINDOW = 512
FORCE_SCORE = 1e6
DN_ALPHA = (2.0 * DEPTH) ** 0.25
QBLK = 128
SCALE = HEAD_DIM ** -0.5

VMEM_LIMIT = 56 * 1024 * 1024

LANES = 128

ORIG_SPLITS = (
    ('a_q', 256), ('a_k', 128), ('a_v', 128), ('a_z', 256),
    ('b_q', 256), ('b_k', 256), ('b_v', 256), ('b_if', 8), ('b_o', 256), ('b_z', 256),
    ('c_q', 256), ('c_f', 256), ('c_i', 256), ('c_z', 256),
    ('d_q', 256), ('d_kvc', 128), ('d_ksv', 128), ('d_kwv', 128), ('d_g', 12), ('d_z', 256),
)
N_COLS = sum(w for _, w in ORIG_SPLITS)
NEW_LAYOUT = (
    ('a_q', 256), ('a_k', 128), ('a_v', 128), ('a_z', 256), ('b_q', 256), ('b_k', 256), ('b_v', 256),
    ('b_o', 256), ('b_z', 256), ('c_q', 256), ('c_f', 256), ('c_i', 256), ('c_z', 256),
    ('d_q', 256), ('d_kvc', 128), ('d_ksv', 128), ('d_kwv', 128), ('b_if', 128), ('d_z', 256), ('d_g', 128),
)
N_PROJ = sum(w for _, w in NEW_LAYOUT)


def _offset(layout, name):
    off = 0
    for n, w in layout:
        if n == name:
            return off
        off += w
    raise KeyError(name)


def _new_offset(name):
    return _offset(NEW_LAYOUT, name)


def _permuted_cols(load):
    sizes = dict(ORIG_SPLITS)
    for name, width in NEW_LAYOUT:
        old, size, new = _offset(ORIG_SPLITS, name), sizes[name], _new_offset(name)
        lo = old // LANES * LANES
        if size == width:
            hi = min(-(-(old + size) // LANES) * LANES, N_COLS)
            yield new, load(lo, hi)[..., old - lo:old - lo + size]
        else:
            tile = load(lo, lo + LANES)
            if old != lo:
                tile = pltpu.roll(tile, LANES - (old - lo), axis=tile.ndim - 1)
            yield new, jnp.where(_iota(tile.shape, tile.ndim - 1) < size, tile, 0.0)


def _prep_body(w_ref, b_ref, wo_ref, bo_ref):
    for new, cols in _permuted_cols(lambda lo, hi: w_ref[0, :, lo:hi]):
        wo_ref[:, new:new + cols.shape[-1]] = cols.astype(BF16)
    for new, cols in _permuted_cols(lambda lo, hi: b_ref[0, :, lo:hi]):
        bo_ref[:, new:new + cols.shape[-1]] = cols


def _prep_in_weights(w_in, b_in, layer, rows=128):
    depth, d, n = w_in.shape
    return pl.pallas_call(
        _prep_body,
        grid=(d // rows,),
        in_specs=[pl.BlockSpec((1, rows, n), lambda r: (layer, r, 0)),
                  pl.BlockSpec((1, 1, n), lambda r: (layer, 0, 0))],
        out_specs=[pl.BlockSpec((rows, N_PROJ), lambda r: (r, 0)),
                   pl.BlockSpec((1, N_PROJ), lambda r: (0, 0))],
        out_shape=[jax.ShapeDtypeStruct((d, N_PROJ), BF16), jax.ShapeDtypeStruct((1, N_PROJ), F32)],
        compiler_params=_cparams(("arbitrary",)),
        name="prep_in_weights",
    )(w_in, b_in.reshape(depth, 1, n))


def _mm(a, b):
    return jnp.dot(a.astype(BF16), b.astype(BF16), preferred_element_type=F32)


def _mm_nt(a, b):
    return lax.dot_general(a.astype(BF16), b.astype(BF16), (((1,), (1,)), ((), ())),
                           preferred_element_type=F32)


def _mm_tn(a, b):
    return lax.dot_general(a.astype(BF16), b.astype(BF16), (((0,), (0,)), ((), ())),
                           preferred_element_type=F32)


def _dot_bf16(a, b):
    return jnp.dot(a, b, preferred_element_type=F32)


def _split_bf16(x, terms):
    out = []
    for _ in range(terms - 1):
        t = x.astype(BF16)
        out.append(t)
        x = x - t.astype(F32)
    out.append(x.astype(BF16))
    return out


def _mm_hi(a, b):
    ah, al = _split_bf16(a, 2)
    bh, bl = _split_bf16(b, 2)
    return _dot_bf16(ah, bh) + (_dot_bf16(ah, bl) + _dot_bf16(al, bh))


def _mm_sel(sel, x):
    sel = sel.astype(BF16)
    x1, x2, x3 = _split_bf16(x, 3)
    return _dot_bf16(sel, x1) + (_dot_bf16(sel, x2) + _dot_bf16(sel, x3))


def _sigmoid(x):
    return 1.0 / (1.0 + jnp.exp2(x * (-LOG2E)))


def _silu(x):
    return x * _sigmoid(x)


def _log_sigmoid(x):
    return jnp.minimum(x, 0.0) - jnp.log1p(jnp.exp(-jnp.abs(x)))


def _rope(x, cos, sin_up, sin_dn):
    w = x.shape[-1]
    up = pltpu.roll(x, w - HALF, axis=1)
    dn = pltpu.roll(x, HALF, axis=1)
    return x * cos + up * sin_up + dn * sin_dn


def _rope_tables(pos, width, rot_lanes):
    inv = ROPE_THETA ** (-jnp.arange(HALF, dtype=F32) / HALF)
    ang = pos.astype(F32)[:, None] * inv[None, :]
    cos, sin = jnp.cos(ang), jnp.sin(ang)
    zero, one = jnp.zeros_like(sin), jnp.ones_like(cos)
    cos_h = jnp.concatenate([cos, cos], axis=1)
    up_h = jnp.concatenate([-sin, zero], axis=1)
    dn_h = jnp.concatenate([zero, sin], axis=1)
    id_c = jnp.concatenate([one, one], axis=1)
    id_s = jnp.concatenate([zero, zero], axis=1)
    cs, us, ds = [], [], []
    for l0 in range(0, width, HEAD_DIM):
        rot = (l0 % 128) < rot_lanes
        cs.append(cos_h if rot else id_c)
        us.append(up_h if rot else id_s)
        ds.append(dn_h if rot else id_s)
    return jnp.concatenate(cs, axis=1), jnp.concatenate(us, axis=1), jnp.concatenate(ds, axis=1)


def _iota(shape, dim):
    return lax.broadcasted_iota(jnp.int32, shape, dim)


def _cparams(sem):
    return pltpu.CompilerParams(dimension_semantics=sem, vmem_limit_bytes=VMEM_LIMIT)


def _ln_body(x_ref, g_ref, b_ref, o_ref):
    x = x_ref[...]
    mu = jnp.mean(x, axis=-1, keepdims=True)
    xc = x - mu
    var = jnp.mean(xc * xc, axis=-1, keepdims=True)
    o_ref[...] = xc * lax.rsqrt(var + LN_EPS) * g_ref[...] + b_ref[...]


def _layer_norm(x2, g, b, tm=512):
    m, d = x2.shape
    return pl.pallas_call(
        _ln_body,
        grid=(m // tm,),
        in_specs=[pl.BlockSpec((tm, d), lambda i: (i, 0)),
                  pl.BlockSpec((1, d), lambda i: (0, 0)),
                  pl.BlockSpec((1, d), lambda i: (0, 0))],
        out_specs=pl.BlockSpec((tm, d), lambda i: (i, 0)),
        out_shape=jax.ShapeDtypeStruct((m, d), F32),
        compiler_params=_cparams(("parallel",)),
        name="layer_norm",
    )(x2, g.reshape(1, d), b.reshape(1, d))


def _proj_body(x_ref, w_ref, b_ref, o_ref):
    o_ref[...] = jnp.dot(x_ref[...].astype(BF16), w_ref[...], preferred_element_type=F32) + b_ref[...]


def _in_proj(h2, w_bf, b, tm=256):
    m, d = h2.shape
    n = w_bf.shape[1]
    return pl.pallas_call(
        _proj_body,
        grid=(m // tm,),
        in_specs=[pl.BlockSpec((tm, d), lambda i: (i, 0)),
                  pl.BlockSpec((d, n), lambda i: (0, 0)),
                  pl.BlockSpec((1, n), lambda i: (0, 0))],
        out_specs=pl.BlockSpec((tm, n), lambda i: (i, 0)),
        out_shape=jax.ShapeDtypeStruct((m, n), F32),
        compiler_params=_cparams(("parallel",)),
        name="in_proj",
    )(h2, w_bf, b.reshape(1, n))


def _out_body(ya_ref, yb_ref, yc_ref, yd_ref, h_ref, w_ref, g_ref, b_ref, o_ref):
    y = _mm(ya_ref[...], w_ref[0:GROUP_W, :])
    y += _mm(yb_ref[...], w_ref[GROUP_W:2 * GROUP_W, :])
    y += _mm(yc_ref[...], w_ref[2 * GROUP_W:3 * GROUP_W, :])
    y += _mm(yd_ref[...], w_ref[3 * GROUP_W:4 * GROUP_W, :])
    r = DN_ALPHA * h_ref[...] + y
    mu = jnp.mean(r, axis=-1, keepdims=True)
    rc = r - mu
    var = jnp.mean(rc * rc, axis=-1, keepdims=True)
    o_ref[...] = rc * lax.rsqrt(var + LN_EPS) * g_ref[...] + b_ref[...]


def _out_proj(ys, h2, w_bf, g, b, tm=512):
    m, d = h2.shape
    yspec = pl.BlockSpec((tm, GROUP_W), lambda i: (i, 0))
    return pl.pallas_call(
        _out_body,
        grid=(m // tm,),
        in_specs=[yspec, yspec, yspec, yspec,
                  pl.BlockSpec((tm, d), lambda i: (i, 0)),
                  pl.BlockSpec((4 * GROUP_W, d), lambda i: (0, 0)),
                  pl.BlockSpec((1, d), lambda i: (0, 0)),
                  pl.BlockSpec((1, d), lambda i: (0, 0))],
        out_specs=pl.BlockSpec((tm, d), lambda i: (i, 0)),
        out_shape=jax.ShapeDtypeStruct((m, d), F32),
        compiler_params=_cparams(("parallel",)),
        name="out_proj",
    )(*[y.reshape(m, GROUP_W) for y in ys], h2, w_bf, g.reshape(1, d), b.reshape(1, d))


def _mixer_a_body(sink_ref, q_ref, kv_ref, z_ref, cos_ref, up_ref, dn_ref, o_ref, k_ref, vt_ref):
    i = pl.program_id(1)
    seq = kv_ref.shape[1]
    n_kv = N_HEADS // A_GROUP

    @pl.when(i == 0)
    def _():
        def rope_rows(c, carry):
            for u in range(PRO_ROWS // QBLK):
                t0 = c * (PRO_ROWS // QBLK) + u
                rows = pl.ds(pl.multiple_of(t0 * QBLK, QBLK), QBLK)
                kr = _rope(kv_ref[0, rows, 0:128], cos_ref[rows, 0:128], up_ref[rows, 0:128], dn_ref[rows, 0:128])
                v_t = kv_ref[0, rows, 128:256].T
                for g in range(n_kv):
                    k_ref[g, rows, :] = kr[:, g * HEAD_DIM:(g + 1) * HEAD_DIM].astype(BF16)
                    vt_ref[g, t0] = v_t[g * HEAD_DIM:(g + 1) * HEAD_DIM, :].astype(BF16)
            return carry
        lax.fori_loop(0, seq // PRO_ROWS, rope_rows, 0)

    n_blk = A_WINDOW // QBLK + 1
    span = n_blk * QBLK
    first_head = _iota((1, A_GROUP * QBLK), 1) < QBLK

    def query_block(u):
        blk_i = i * A_SUB + u
        sub = slice(u * QBLK, (u + 1) * QBLK)
        q0 = pl.multiple_of(blk_i * QBLK, QBLK)
        qrows = pl.ds(q0, QBLK)
        qr = _rope(q_ref[0, sub, :], cos_ref[qrows, :], up_ref[qrows, :], dn_ref[qrows, :]) * (SCALE * LOG2E)
        qt = qr.T
        kt0 = jnp.maximum(blk_i - A_WINDOW // QBLK, 0)
        k0 = pl.multiple_of(kt0 * QBLK, QBLK)
        rel = (q0 + _iota((1, QBLK), 1)) - (k0 + _iota((span, 1), 0))
        in_band = lax.bitcast_convert_type(rel, jnp.uint32) < A_WINDOW
        bias = jnp.concatenate([jnp.where(in_band, 0.0, NEG)] * A_GROUP, axis=1)
        outs = []
        for g in range(n_kv):
            heads = range(g * A_GROUP, (g + 1) * A_GROUP)
            qg = jnp.concatenate([qt[h * HEAD_DIM:(h + 1) * HEAD_DIM] for h in heads], axis=1).astype(BF16)
            sink = jnp.where(first_head, sink_ref[g * A_GROUP], sink_ref[g * A_GROUP + 1]) * LOG2E
            s = jnp.dot(k_ref[g, pl.ds(k0, span), :], qg, preferred_element_type=F32) + bias
            m = jnp.maximum(jnp.max(s, axis=0, keepdims=True), sink)
            p = jnp.exp2(s - m)
            l = jnp.sum(p, axis=0, keepdims=True) + jnp.exp2(sink - m)
            p = p.astype(BF16)
            o = jnp.zeros((HEAD_DIM, A_GROUP * QBLK), F32)
            for c in range(n_blk):
                o = o + jnp.dot(vt_ref[g, kt0 + c], p[c * QBLK:(c + 1) * QBLK], preferred_element_type=F32)
            o = o / l
            outs.extend(o[:, r * QBLK:(r + 1) * QBLK] for r in range(A_GROUP))
        o_ref[0, sub, :] = jnp.concatenate(outs, axis=0).T * _silu(z_ref[0, sub, :])

    for u in range(A_SUB):
        query_block(u)


def _mixer_a(proj, sinks, tabs):
    bsz, seq, _ = proj.shape
    cos, up, dn = tabs
    blk = lambda name: _new_offset(name) // GROUP_W
    tspec = pl.BlockSpec((seq, GROUP_W), lambda b, i: (0, 0))
    return pl.pallas_call(
        _mixer_a_body,
        grid=(bsz, seq // (A_SUB * QBLK)),
        in_specs=[pl.BlockSpec(memory_space=pltpu.SMEM),
                  pl.BlockSpec((1, A_SUB * QBLK, GROUP_W), lambda b, i: (b, i, blk('a_q'))),
                  pl.BlockSpec((1, seq, GROUP_W), lambda b, i: (b, 0, blk('a_k'))),
                  pl.BlockSpec((1, A_SUB * QBLK, GROUP_W), lambda b, i: (b, i, blk('a_z'))),
                  tspec, tspec, tspec],
        out_specs=pl.BlockSpec((1, A_SUB * QBLK, GROUP_W), lambda b, i: (b, i, 0)),
        out_shape=jax.ShapeDtypeStruct((bsz, seq, GROUP_W), F32),
        scratch_shapes=[pltpu.VMEM((N_HEADS // A_GROUP, seq, HEAD_DIM), BF16),
                        pltpu.VMEM((N_HEADS // A_GROUP, seq // QBLK, HEAD_DIM, QBLK), BF16)],
        compiler_params=_cparams(("arbitrary", "arbitrary")),
        name="mixer_a",
    )(sinks, proj, proj, proj, cos, up, dn)


B_CHUNK = 128
B_STEP = 256
B_AUG = HEAD_DIM + 16


def _conv_silu(x, prev, w, b):
    row8 = _iota((8, x.shape[1]), 0)
    acc = b + w[B_CONV - 1:B_CONV, :] * x
    for s in range(1, B_CONV):
        xs = pltpu.roll(x, s, axis=0)
        ps = pltpu.roll(prev, s, axis=0)
        shifted = jnp.concatenate([jnp.where(row8 < s, ps, xs[0:8]), xs[8:]], axis=0)
        acc = acc + w[B_CONV - 1 - s:B_CONV - s, :] * shifted
    return _silu(acc)


def _mixer_b_body(xq_ref, xk_ref, v_ref, og_ref, z_ref, if_ref, cw_ref, cb_ref, o_ref, xprev_ref, c_ref, m_ref):
    c = pl.program_id(1)
    L = B_CHUNK

    @pl.when(c == 0)
    def _():
        xprev_ref[...] = jnp.zeros_like(xprev_ref)
        c_ref[...] = jnp.zeros_like(c_ref)
        m_ref[...] = jnp.zeros_like(m_ref)

    tri = (_iota((L, L), 1) <= _iota((L, L), 0)).astype(F32)
    key_first = _iota((L, L), 0) <= _iota((L, L), 1)
    half_of_lane = _iota((1, 128), 1) // HEAD_DIM
    ones_rows = jnp.ones((B_AUG - HEAD_DIM, L), F32)

    def chunk(rows, prev_q, prev_k, state):
        xq = xq_ref[0, rows, :]
        xk = xk_ref[0, rows, :]
        qc = _conv_silu(xq, prev_q, cw_ref[:, 0:GROUP_W], cb_ref[:, 0:GROUP_W]) * SCALE
        kc = _conv_silu(xk, prev_k, cw_ref[:, GROUP_W:], cb_ref[:, GROUP_W:])
        q_t = qc.T
        v_t = v_ref[0, rows, :].T

        gates = if_ref[0, rows, :]
        bcum = _mm_sel(tri, _log_sigmoid(gates))
        gates_t = gates.T
        bcum_t = bcum.T
        outs, new_state = [], []
        for h in range(N_HEADS):
            pair = slice((h // 2) * 128, (h // 2 + 1) * 128)
            own = half_of_lane == (h % 2)
            k_pair = kc[:, pair]
            q_pair_t = q_t[pair, :]
            i_row = gates_t[h:h + 1, :]
            b_row = bcum_t[N_HEADS + h:N_HEADS + h + 1, :]
            c_col = gates[:, h:h + 1] - bcum[:, N_HEADS + h:N_HEADS + h + 1]
            cst, m_prev = state[h]
            dmat = jnp.where(key_first, b_row + c_col, NEG)
            inter = b_row + m_prev
            m_t = jnp.maximum(jnp.max(dmat, axis=0, keepdims=True), inter)
            smat = _mm(jnp.where(own, k_pair, 0.0), q_pair_t) * jnp.exp(dmat - m_t)
            vaug_t = jnp.concatenate([v_t[h * HEAD_DIM:(h + 1) * HEAD_DIM, :], ones_rows], axis=0)
            tot = _mm(vaug_t, smat) + jnp.exp(inter - m_t) * _mm(cst, q_pair_t)
            num = tot[0:HEAD_DIM, :]
            den = tot[HEAD_DIM:HEAD_DIM + 1, :]
            outs.append(num / jnp.maximum(jnp.abs(den), jnp.exp(-m_t)))
            b_last = b_row[:, L - 1:L]
            d_row = b_last - b_row + i_row
            m_new = jnp.maximum(b_last + m_prev, jnp.max(d_row, axis=1, keepdims=True))
            w_row = jnp.exp(d_row - m_new)
            decay = jnp.exp(b_last + m_prev - m_new)
            new_state.append((decay * cst + jnp.where(own, _mm(vaug_t * w_row, k_pair), 0.0), m_new))
        o_ref[0, rows, :] = (_sigmoid(og_ref[0, rows, :]) * jnp.concatenate(outs, axis=0).T
                             * _silu(z_ref[0, rows, :]))
        return xq[L - 8:L, :], xk[L - 8:L, :], new_state

    prev_q = xprev_ref[:, 0:GROUP_W]
    prev_k = xprev_ref[:, GROUP_W:]
    state = [(c_ref[h], m_ref[h:h + 1, 0:1]) for h in range(N_HEADS)]
    for u in range(xq_ref.shape[1] // L):
        prev_q, prev_k, state = chunk(slice(u * L, (u + 1) * L), prev_q, prev_k, state)
    xprev_ref[:, 0:GROUP_W] = prev_q
    xprev_ref[:, GROUP_W:] = prev_k
    for h in range(N_HEADS):
        c_ref[h] = state[h][0]
        m_ref[h:h + 1, :] = jnp.broadcast_to(state[h][1], (1, 128))


def _mixer_b(proj, conv_w, conv_b):
    bsz, seq, _ = proj.shape
    L = B_STEP
    blk = lambda name: _new_offset(name) // GROUP_W
    return pl.pallas_call(
        _mixer_b_body,
        grid=(bsz, seq // L),
        in_specs=[pl.BlockSpec((1, L, GROUP_W), lambda b, c: (b, c, blk('b_q'))),
                  pl.BlockSpec((1, L, GROUP_W), lambda b, c: (b, c, blk('b_k'))),
                  pl.BlockSpec((1, L, GROUP_W), lambda b, c: (b, c, blk('b_v'))),
                  pl.BlockSpec((1, L, GROUP_W), lambda b, c: (b, c, blk('b_o'))),
                  pl.BlockSpec((1, L, GROUP_W), lambda b, c: (b, c, blk('b_z'))),
                  pl.BlockSpec((1, L, 128), lambda b, c: (b, c, _new_offset('b_if') // 128)),
                  pl.BlockSpec((B_CONV, 2 * GROUP_W), lambda b, c: (0, 0)),
                  pl.BlockSpec((1, 2 * GROUP_W), lambda b, c: (0, 0))],
        out_specs=pl.BlockSpec((1, L, GROUP_W), lambda b, c: (b, c, 0)),
        out_shape=jax.ShapeDtypeStruct((bsz, seq, GROUP_W), F32),
        scratch_shapes=[pltpu.VMEM((8, 2 * GROUP_W), F32),
                        pltpu.VMEM((N_HEADS, B_AUG, 128), F32),
                        pltpu.VMEM((8, 128), F32)],
        compiler_params=_cparams(("arbitrary", "arbitrary")),
        name="mixer_b",
    )(proj, proj, proj, proj, proj, proj, conv_w, conv_b.reshape(1, -1))


def _mixer_c_body(layer, q_ref, f_ref, i_ref, z_ref, lb_ref, gn_ref, tri_ref, piv_ref, o_ref, st_ref):
    c = pl.program_id(1)
    L = C_CHUNK
    nsub = L // C_SUB

    @pl.when(c == 0)
    def _():
        st_ref[...] = jnp.zeros_like(st_ref)

    raw = lb_ref[...]
    ex = jnp.exp(raw - jnp.max(raw, axis=0, keepdims=True))
    if layer == 0:
        lb = jnp.zeros((1, GROUP_W), F32)
    else:
        lb = jnp.sum(ex[1:layer + 1], axis=0, keepdims=True) / jnp.sum(ex, axis=0, keepdims=True)
    tri = (_iota((L, L), 1) <= _iota((L, L), 0)).astype(F32)
    head_of_lane = _iota((1, GROUP_W), 1) // HEAD_DIM
    same_head = ((_iota((GROUP_W, GROUP_W), 0) // HEAD_DIM)
                 == (_iota((GROUP_W, GROUP_W), 1) // HEAD_DIM))
    ones_bd = same_head.astype(BF16)

    def chunk(rows, st):
        f = lb + (1.0 - lb) * _sigmoid(f_ref[0, rows, :])
        logf = jnp.log(f)
        k = 1.0 - f
        q = _silu(q_ref[0, rows, :])
        v = i_ref[0, rows, :]
        a = _mm_sel(tri, logf) * LOG2E
        o = _mm_nt(q * jnp.exp2(a), st)

        qsts, kts, vreps = [], [], []
        for i in range(1, nsub):
            r0 = i * C_SUB
            piv = a[r0 - 1:r0, :]
            qt = q[r0:r0 + C_SUB] * jnp.exp2(a[r0:r0 + C_SUB] - piv)
            qsts.extend(jnp.where(head_of_lane == h, qt, 0.0) for h in range(N_HEADS))
            kts.append((k[0:r0] * jnp.exp2(piv - a[0:r0])).astype(BF16))
            vreps.append(v[0:r0].astype(BF16))
        scores = _mm_nt(jnp.concatenate(qsts, axis=0), jnp.concatenate(kts, axis=0))
        r = _dot_bf16((scores * piv_ref[...]).astype(BF16), jnp.concatenate(vreps, axis=0))
        pieces = [o[0:C_SUB]]
        for i in range(1, nsub):
            acc = o[i * C_SUB:(i + 1) * C_SUB]
            for h in range(N_HEADS):
                g0 = ((i - 1) * N_HEADS + h) * C_SUB
                acc = acc + jnp.where(head_of_lane == h, r[g0:g0 + C_SUB], 0.0)
            pieces.append(acc)
        o = jnp.concatenate(pieces, axis=0)

        a3 = a.reshape(nsub, C_SUB, GROUP_W)
        q3 = q.reshape(nsub, C_SUB, GROUP_W)
        k3 = k.reshape(nsub, C_SUB, GROUP_W)
        v3 = v.reshape(nsub, C_SUB, GROUP_W)
        ps = []
        for s in range(C_SUB):
            e = jnp.exp2(a3 - a3[:, s:s + 1, :] + tri_ref[s])
            ps.append((q3 * k3[:, s:s + 1, :] * e).reshape(L, GROUP_W).astype(BF16))
        w = jnp.dot(jnp.concatenate(ps, axis=0), ones_bd, preferred_element_type=F32)
        for s in range(C_SUB):
            vs = jnp.broadcast_to(v3[:, s:s + 1, :], (nsub, C_SUB, GROUP_W)).reshape(L, GROUP_W)
            o = o + w[s * L:(s + 1) * L] * vs

        a_last = a[L - 1:L, :]
        kdec = k * jnp.exp2(a_last - a)
        st_new = st * jnp.exp2(a_last) + jnp.where(same_head, _mm_tn(v, kdec), 0.0)

        ms = jnp.dot((o * o).astype(BF16), ones_bd, preferred_element_type=F32) * (1.0 / HEAD_DIM)
        o = o * lax.rsqrt(ms + C_EPS) * gn_ref[...]
        o_ref[0, rows, :] = o * _silu(z_ref[0, rows, :])
        return st_new

    st = st_ref[...]
    for u in range(q_ref.shape[1] // L):
        st = chunk(slice(u * L, (u + 1) * L), st)
    st_ref[...] = st


def _mixer_c(proj, c_lb, norm_g, layer):
    bsz, seq, _ = proj.shape
    L = C_STEP
    blk = lambda name: _new_offset(name) // GROUP_W
    spec = lambda name: pl.BlockSpec((1, L, GROUP_W), lambda b, c: (b, c, blk(name)))
    vec = pl.BlockSpec((1, GROUP_W), lambda b, c: (0, 0))
    t_ge_s = np.arange(C_SUB)[None, :, None] >= np.arange(C_SUB)[:, None, None]
    causal_bias = np.broadcast_to(np.where(t_ge_s, 0.0, NEG), (C_SUB, C_SUB, GROUP_W)).astype(np.float32)
    pivots = np.arange(1, C_CHUNK // C_SUB)
    row_piv = np.repeat(pivots, N_HEADS * C_SUB)
    col_piv = np.repeat(pivots, pivots * C_SUB)
    same_pivot = (row_piv[:, None] == col_piv[None, :]).astype(np.float32)
    return pl.pallas_call(
        functools.partial(_mixer_c_body, layer),
        grid=(bsz, seq // L),
        in_specs=[spec('c_q'), spec('c_f'), spec('c_i'), spec('c_z'),
                  pl.BlockSpec((DEPTH, GROUP_W), lambda b, c: (0, 0)), vec,
                  pl.BlockSpec((C_SUB, C_SUB, GROUP_W), lambda b, c: (0, 0, 0)),
                  pl.BlockSpec(same_pivot.shape, lambda b, c: (0, 0))],
        out_specs=pl.BlockSpec((1, L, GROUP_W), lambda b, c: (b, c, 0)),
        out_shape=jax.ShapeDtypeStruct((bsz, seq, GROUP_W), F32),
        scratch_shapes=[pltpu.VMEM((GROUP_W, GROUP_W), F32)],
        compiler_params=_cparams(("arbitrary", "arbitrary")),
        name="mixer_c",
    )(proj, proj, proj, proj, c_lb, norm_g.reshape(1, -1), jnp.asarray(causal_bias), jnp.asarray(same_pivot))


N_CMP_PAD = 128
KT = 512
D_QBLK = 256


def _mixer_d_body(q_ref, z_ref, ksv_ref, kwv_ref, cin_ref, g_ref, cos_ref, up_ref, dn_ref, kcos_ref, kup_ref,
                  kdn_ref, ccos_ref, cup_ref, cdn_ref, w1_ref, pe_ref, w2_ref, ov_ref, o_ref,
                  k_ref, vts_ref, vtw_ref, kc_ref, vct_ref, sel_ref):
    QBLK = D_QBLK
    i = pl.program_id(1)
    seq = ksv_ref.shape[1]
    half_cmp = CMP_LEN // 2

    @pl.when(i == 0)
    def _():
        def rope_rows(c, carry):
            for u in range(PRO_ROWS // KT):
                t0 = c * (PRO_ROWS // KT) + u
                rows = pl.ds(pl.multiple_of(t0 * KT, KT), KT)
                kvs = _rope(ksv_ref[0, rows, :], kcos_ref[rows, :], kup_ref[rows, :], kdn_ref[rows, :])
                kvw = _rope(kwv_ref[0, rows, :], kcos_ref[rows, :], kup_ref[rows, :], kdn_ref[rows, :])
                k_ref[0, rows, :] = kvs[:, 0:HEAD_DIM].astype(BF16)
                k_ref[1, rows, :] = kvw[:, 0:HEAD_DIM].astype(BF16)
                vts_ref[t0] = kvs.T[HEAD_DIM:2 * HEAD_DIM, :].astype(BF16)
                vwt = kvw.T[HEAD_DIM:2 * HEAD_DIM, :].astype(BF16)
                for w in range(KT // QBLK):
                    vtw_ref[t0 * (KT // QBLK) + w] = vwt[:, w * QBLK:(w + 1) * QBLK]
            return carry
        lax.fori_loop(0, seq // PRO_ROWS, rope_rows, 0)
        u0 = jnp.zeros((N_CMP_PAD, 128), F32)
        u1 = jnp.zeros((N_CMP_PAD, 128), F32)
        for r in range(half_cmp):
            xr = cin_ref[0, pl.ds(r, N_CMP_PAD, stride=CMP_STRIDE), :]
            u0 = u0 + _mm_hi(xr + pe_ref[r:r + 1, :], w1_ref[r])
            u1 = u1 + _mm_hi(xr + pe_ref[half_cmp + r:half_cmp + r + 1, :], w1_ref[half_cmp + r])
        pre = u0 + pltpu.roll(u1, N_CMP_PAD - 1, axis=0)
        cv = _rope(_mm_hi(_silu(pre), w2_ref[...]), ccos_ref[...], cup_ref[...], cdn_ref[...])
        kc_ref[...] = cv[:, 0:HEAD_DIM]
        vct_ref[...] = cv.T[HEAD_DIM:2 * HEAD_DIM, :].astype(BF16)

    q0 = pl.multiple_of(i * QBLK, QBLK)
    qrows = pl.ds(q0, QBLK)
    qr = _rope(q_ref[0], cos_ref[qrows, :], up_ref[qrows, :], dn_ref[qrows, :]) * (SCALE * LOG2E)
    qt = qr.T
    qst = jnp.concatenate([qt[h * HEAD_DIM:(h + 1) * HEAD_DIM] for h in range(N_HEADS)], axis=1)
    qst_bf = qst.astype(BF16)
    cols4 = N_HEADS * QBLK
    qpos = q0 + _iota((1, QBLK), 1)

    def lanes4(x):
        return jnp.concatenate([x] * N_HEADS, axis=1)

    n_win = NSA_WINDOW // QBLK + 1
    span = n_win * QBLK
    kt0 = jnp.maximum(i - NSA_WINDOW // QBLK, 0)
    k0 = pl.multiple_of(kt0 * QBLK, QBLK)
    rel = qpos - (k0 + _iota((span, 1), 0))
    in_band = lax.bitcast_convert_type(rel, jnp.uint32) < NSA_WINDOW
    s = jnp.dot(k_ref[1, pl.ds(k0, span), :], qst_bf, preferred_element_type=F32)
    s = s + lanes4(jnp.where(in_band, 0.0, NEG))
    p = jnp.exp2(s - jnp.max(s, axis=0, keepdims=True))
    l_w = jnp.sum(p, axis=0, keepdims=True)
    p = p.astype(BF16)
    o_win = jnp.zeros((HEAD_DIM, cols4), F32)
    for c in range(n_win):
        o_win = o_win + jnp.dot(vtw_ref[kt0 + c], p[c * QBLK:(c + 1) * QBLK], preferred_element_type=F32)
    o_win = o_win / l_w

    sc = _mm_hi(kc_ref[...], qst)
    ends = _iota((N_CMP_PAD, 1), 0) * CMP_STRIDE + (CMP_LEN - 1)
    scm = sc + lanes4(jnp.where(ends <= qpos, 0.0, NEG))
    e = jnp.exp2(scm - jnp.max(scm, axis=0, keepdims=True))
    l = jnp.sum(e, axis=0, keepdims=True)
    any_valid = lanes4(jnp.where(qpos >= CMP_LEN - 1, 1.0, 0.0))
    pc = e * (any_valid / l)
    o_cmp = jnp.dot(vct_ref[...], pc.astype(BF16), preferred_element_type=F32)
    psum = pc[:, 0:QBLK] + pc[:, QBLK:2 * QBLK] + pc[:, 2 * QBLK:3 * QBLK] + pc[:, 3 * QBLK:4 * QBLK]

    n_slc = seq // SLC_LEN
    imp = _mm_sel(ov_ref[...], psum)
    jl = _iota((n_slc, 1), 0)
    cur = jnp.right_shift(qpos, SLC_LEN.bit_length() - 1)
    imp = jnp.where(jl == 0, FORCE_SCORE, imp)
    imp = jnp.where(jl == cur, FORCE_SCORE, imp)
    imp = jnp.where(jl == cur - 1, FORCE_SCORE, imp)
    imp = jnp.where(jl <= cur, imp, NEG)
    rank = jnp.zeros((n_slc, QBLK), F32)
    for jp in range(n_slc):
        cj = imp[jp:jp + 1, :]
        tie = jnp.where(jl > jp, 1.0, 0.0)
        rank = rank + jnp.where(cj > imp, 1.0, jnp.where(cj == imp, tie, 0.0))
    sel_bias = jnp.where(rank < N_SEL, 0.0, NEG)
    blocks_per_tile = KT // SLC_LEN
    for t in range(n_slc // blocks_per_tile):
        sel_ref[t] = sel_bias[t * blocks_per_tile:(t + 1) * blocks_per_tile, :]

    n_pair = 1
    pw = cols4 // n_pair

    def sel_tile(kt, carry):
        kbase = pl.multiple_of(kt * KT, KT)
        picks = sel_ref[kt]
        chosen = jnp.concatenate([jnp.broadcast_to(picks[j:j + 1, :], (SLC_LEN, QBLK))
                                  for j in range(blocks_per_tile)], axis=0)
        bias = jnp.where((kbase + _iota((KT, 1), 0)) <= qpos, chosen, NEG)
        bias = jnp.concatenate([bias] * (pw // QBLK), axis=1)
        kb = k_ref[0, pl.ds(kbase, KT), :]
        vb = vts_ref[kt]
        out = []
        for c in range(n_pair):
            m_old, l_old, acc = carry[c]
            s = jnp.dot(kb, qst_bf[:, c * pw:(c + 1) * pw], preferred_element_type=F32) + bias
            m_new = jnp.maximum(m_old, jnp.max(s, axis=0, keepdims=True))
            p = jnp.exp2(s - m_new)
            alpha = jnp.exp2(m_old - m_new)
            l_new = alpha * l_old + jnp.sum(p, axis=0, keepdims=True)
            acc_new = alpha * acc + jnp.dot(vb, p.astype(BF16), preferred_element_type=F32)
            out.append((m_new, l_new, acc_new))
        return tuple(out)

    init = tuple((jnp.full((1, pw), NEG, F32), jnp.zeros((1, pw), F32), jnp.zeros((HEAD_DIM, pw), F32))
                 for _ in range(n_pair))
    n_tiles = (i + KT // QBLK) // (KT // QBLK)
    fin = lax.fori_loop(0, n_tiles, sel_tile, init)
    o_slc = jnp.concatenate([acc / l_c for _, l_c, acc in fin], axis=1)

    gate = _sigmoid(g_ref[0]).T
    outs = []
    for h in range(N_HEADS):
        cs = slice(h * QBLK, (h + 1) * QBLK)
        outs.append(gate[3 * h:3 * h + 1, :] * o_cmp[:, cs] + gate[3 * h + 1:3 * h + 2, :] * o_slc[:, cs]
                    + gate[3 * h + 2:3 * h + 3, :] * o_win[:, cs])
    o_ref[0] = jnp.concatenate(outs, axis=0).T * _silu(z_ref[0])


def _mixer_d(proj, qtabs, ktabs, ctabs, w1bd, pe2, w2bd, overlap):
    QBLK = D_QBLK
    bsz, seq, _ = proj.shape
    blk = lambda name: _new_offset(name) // GROUP_W
    full = lambda shape: pl.BlockSpec(shape, lambda b, i: (0,) * len(shape))
    return pl.pallas_call(
        _mixer_d_body,
        grid=(bsz, seq // QBLK),
        in_specs=[pl.BlockSpec((1, QBLK, GROUP_W), lambda b, i: (b, i, blk('d_q'))),
                  pl.BlockSpec((1, QBLK, GROUP_W), lambda b, i: (b, i, blk('d_z'))),
                  pl.BlockSpec((1, seq, 128), lambda b, i: (b, 0, _new_offset('d_ksv') // 128)),
                  pl.BlockSpec((1, seq, 128), lambda b, i: (b, 0, _new_offset('d_kwv') // 128)),
                  pl.BlockSpec((1, seq, 128), lambda b, i: (b, 0, _new_offset('d_kvc') // 128)),
                  pl.BlockSpec((1, QBLK, 128), lambda b, i: (b, i, _new_offset('d_g') // 128)),
                  full((seq, GROUP_W)), full((seq, GROUP_W)), full((seq, GROUP_W)),
                  full((seq, 128)), full((seq, 128)), full((seq, 128)),
                  full((N_CMP_PAD, 128)), full((N_CMP_PAD, 128)), full((N_CMP_PAD, 128)),
                  full((CMP_LEN, 128, 128)), full((CMP_LEN, 128)), full((128, 128)),
                  full((seq // SLC_LEN, N_CMP_PAD))],
        out_specs=pl.BlockSpec((1, QBLK, GROUP_W), lambda b, i: (b, i, 0)),
        out_shape=jax.ShapeDtypeStruct((bsz, seq, GROUP_W), F32),
        scratch_shapes=[pltpu.VMEM((2, seq, HEAD_DIM), BF16),
                        pltpu.VMEM((seq // KT, HEAD_DIM, KT), BF16),
                        pltpu.VMEM((seq // QBLK, HEAD_DIM, QBLK), BF16),
                        pltpu.VMEM((N_CMP_PAD, HEAD_DIM), F32),
                        pltpu.VMEM((HEAD_DIM, N_CMP_PAD), BF16),
                        pltpu.VMEM((seq // KT, KT // SLC_LEN, QBLK), F32)],
        compiler_params=_cparams(("arbitrary", "arbitrary")),
        name="mixer_d",
    )(proj, proj, proj, proj, proj, proj, *qtabs, *ktabs, *ctabs, w1bd, pe2, w2bd, overlap)


def _cmp_params(pe, w1, w2):
    w1 = w1.reshape(2, CMP_LEN, HEAD_DIM, HEAD_DIM)
    z = jnp.zeros((CMP_LEN, HEAD_DIM, HEAD_DIM), F32)
    w1bd = jnp.concatenate([jnp.concatenate([w1[0], z], axis=2), jnp.concatenate([z, w1[1]], axis=2)], axis=1)
    z2 = jnp.zeros((HEAD_DIM, HEAD_DIM), F32)
    w2bd = jnp.concatenate([jnp.concatenate([w2[0], z2], axis=1), jnp.concatenate([z2, w2[1]], axis=1)], axis=0)
    pe2 = jnp.concatenate([pe[0], pe[1]], axis=1)
    return w1bd, pe2, w2bd


def _overlap_matrix(seq):
    n_cmp = (seq - CMP_LEN) // CMP_STRIDE + 1
    starts = np.arange(N_CMP_PAD) * CMP_STRIDE
    blk = np.arange(seq // SLC_LEN)
    ov = ((starts[None, :] < (blk[:, None] + 1) * SLC_LEN) & (starts[None, :] + CMP_LEN > blk[:, None] * SLC_LEN))
    ov = ov & (np.arange(N_CMP_PAD)[None, :] < n_cmp)
    return jnp.asarray(ov.astype(np.float32))


def kernel(x, ln0_g, ln0_b, w_in, b_in, a_sinks, b_conv_w, b_conv_b, c_lb, c_norm_g, d_cmp_pe, d_cmp_w1, d_cmp_w2,
           w_out, ln_g, ln_b):
    bsz, seq, d = x.shape
    pos = jnp.arange(seq)
    qtabs = _rope_tables(pos, GROUP_W, 128)
    ktabs = _rope_tables(pos, 128, HEAD_DIM)
    ctabs = _rope_tables(jnp.arange(N_CMP_PAD) * CMP_STRIDE + (CMP_LEN - 1), 128, HEAD_DIM)
    overlap = _overlap_matrix(seq)

    h = _layer_norm(x.reshape(bsz * seq, d), ln0_g, ln0_b)
    for l in range(DEPTH):
        w_p, b_p = _prep_in_weights(w_in, b_in, l)
        proj = _in_proj(h, w_p, b_p).reshape(bsz, seq, N_PROJ)
        y_a = _mixer_a(proj, a_sinks[l], qtabs)
        y_b = _mixer_b(proj, b_conv_w[l], b_conv_b[l])
        y_c = _mixer_c(proj, c_lb, c_norm_g[l], l)
        w1bd, pe2, w2bd = _cmp_params(d_cmp_pe[l], d_cmp_w1[l], d_cmp_w2[l])
        y_d = _mixer_d(proj, qtabs, ktabs, ctabs, w1bd, pe2, w2bd, overlap)
        h = _out_proj((y_a, y_b, y_c, y_d), h, w_out[l].astype(BF16), ln_g[l], ln_b[l])
    return h.reshape(bsz, seq, d)
```

```python
import functools

import numpy as np
import jax
import jax.numpy as jnp
from jax import lax
from jax.experimental import pallas as pl
from jax.experimental.pallas import tpu as pltpu

F32 = jnp.float32
BF16 = jnp.bfloat16
Y_DTYPE = BF16

D_MODEL = 1024
DEPTH = 2
HEAD_DIM = 64
HALF = HEAD_DIM // 2
N_HEADS = 4
GROUP_W = N_HEADS * HEAD_DIM
ROPE_THETA = 10000.0
NEG = -1e30
LN_EPS = 1e-5
A_WINDOW = 128
A_GROUP = 2
A_SUB = 4
V_AUG = HEAD_DIM + 16
LOG2E = 1.4426950408889634
PRO_ROWS = 512
B_CONV = 4
C_CHUNK = 64
C_SUB = 8
C_STEP = 256
C_EPS = 1e-6
CMP_LEN = 32
CMP_STRIDE = 16
SLC_LEN = 64
N_SEL = 8
NSA_WINDOW = 512
FORCE_SCORE = 1e6
DN_ALPHA = (2.0 * DEPTH) ** 0.25
QBLK = 128
SCALE = HEAD_DIM ** -0.5

VMEM_LIMIT = 56 * 1024 * 1024

LANES = 128

ORIG_SPLITS = (
    ('a_q', 256), ('a_k', 128), ('a_v', 128), ('a_z', 256),
    ('b_q', 256), ('b_k', 256), ('b_v', 256), ('b_if', 8), ('b_o', 256), ('b_z', 256),
    ('c_q', 256), ('c_f', 256), ('c_i', 256), ('c_z', 256),
    ('d_q', 256), ('d_kvc', 128), ('d_ksv', 128), ('d_kwv', 128), ('d_g', 12), ('d_z', 256),
)
N_COLS = sum(w for _, w in ORIG_SPLITS)
NEW_LAYOUT = (
    ('a_q', 256), ('a_k', 128), ('a_v', 128), ('a_z', 256), ('b_q', 256), ('b_k', 256), ('b_v', 256),
    ('b_o', 256), ('b_z', 256), ('c_q', 256), ('c_f', 256), ('c_i', 256), ('c_z', 256),
    ('d_q', 256), ('d_kvc', 128), ('d_ksv', 128), ('d_kwv', 128), ('b_if', 128), ('d_z', 256), ('d_g', 128),
)
N_PROJ = sum(w for _, w in NEW_LAYOUT)


def _offset(layout, name):
    off = 0
    for n, w in layout:
        if n == name:
            return off
        off += w
    raise KeyError(name)


def _new_offset(name):
    return _offset(NEW_LAYOUT, name)


def _permuted_cols(load):
    sizes = dict(ORIG_SPLITS)
    for name, width in NEW_LAYOUT:
        old, size, new = _offset(ORIG_SPLITS, name), sizes[name], _new_offset(name)
        lo = old // LANES * LANES
        if size == width:
            hi = min(-(-(old + size) // LANES) * LANES, N_COLS)
            yield new, load(lo, hi)[..., old - lo:old - lo + size]
        else:
            tile = load(lo, lo + LANES)
            if old != lo:
                tile = pltpu.roll(tile, LANES - (old - lo), axis=tile.ndim - 1)
            yield new, jnp.where(_iota(tile.shape, tile.ndim - 1) < size, tile, 0.0)


def _prep_body(w_ref, b_ref, wo_ref, bo_ref):
    for new, cols in _permuted_cols(lambda lo, hi: w_ref[0, :, lo:hi]):
        wo_ref[:, new:new + cols.shape[-1]] = cols.astype(BF16)
    for new, cols in _permuted_cols(lambda lo, hi: b_ref[0, :, lo:hi]):
        bo_ref[:, new:new + cols.shape[-1]] = cols


def _prep_in_weights(w_in, b_in, layer, rows=128):
    depth, d, n = w_in.shape
    return pl.pallas_call(
        _prep_body,
        grid=(d // rows,),
        in_specs=[pl.BlockSpec((1, rows, n), lambda r: (layer, r, 0)),
                  pl.BlockSpec((1, 1, n), lambda r: (layer, 0, 0))],
        out_specs=[pl.BlockSpec((rows, N_PROJ), lambda r: (r, 0)),
                   pl.BlockSpec((1, N_PROJ), lambda r: (0, 0))],
        out_shape=[jax.ShapeDtypeStruct((d, N_PROJ), BF16), jax.ShapeDtypeStruct((1, N_PROJ), F32)],
        compiler_params=_cparams(("arbitrary",)),
        name="prep_in_weights",
    )(w_in, b_in.reshape(depth, 1, n))


def _mm(a, b):
    return jnp.dot(a.astype(BF16), b.astype(BF16), preferred_element_type=F32)


def _mm_nt(a, b):
    return lax.dot_general(a.astype(BF16), b.astype(BF16), (((1,), (1,)), ((), ())),
                           preferred_element_type=F32)


def _mm_tn(a, b):
    return lax.dot_general(a.astype(BF16), b.astype(BF16), (((0,), (0,)), ((), ())),
                           preferred_element_type=F32)


def _dot_bf16(a, b):
    return jnp.dot(a, b, preferred_element_type=F32)


def _split_bf16(x, terms):
    out = []
    for _ in range(terms - 1):
        t = x.astype(BF16)
        out.append(t)
        x = x - t.astype(F32)
    out.append(x.astype(BF16))
    return out


def _mm_hi(a, b):
    ah, al = _split_bf16(a, 2)
    bh, bl = _split_bf16(b, 2)
    return _dot_bf16(ah, bh) + (_dot_bf16(ah, bl) + _dot_bf16(al, bh))


def _mm_sel(sel, x):
    sel = sel.astype(BF16)
    x1, x2, x3 = _split_bf16(x, 3)
    return _dot_bf16(sel, x1) + (_dot_bf16(sel, x2) + _dot_bf16(sel, x3))


def _sigmoid(x):
    return 1.0 / (1.0 + jnp.exp2(x * (-LOG2E)))


def _silu(x):
    return x * _sigmoid(x)


def _log_sigmoid(x):
    return jnp.minimum(x, 0.0) - jnp.log1p(jnp.exp(-jnp.abs(x)))


def _rope(x, cos, sin_up, sin_dn):
    w = x.shape[-1]
    up = pltpu.roll(x, w - HALF, axis=1)
    dn = pltpu.roll(x, HALF, axis=1)
    return x * cos + up * sin_up + dn * sin_dn


def _rope_tables(pos, width, rot_lanes):
    inv = ROPE_THETA ** (-jnp.arange(HALF, dtype=F32) / HALF)
    ang = pos.astype(F32)[:, None] * inv[None, :]
    cos, sin = jnp.cos(ang), jnp.sin(ang)
    zero, one = jnp.zeros_like(sin), jnp.ones_like(cos)
    cos_h = jnp.concatenate([cos, cos], axis=1)
    up_h = jnp.concatenate([-sin, zero], axis=1)
    dn_h = jnp.concatenate([zero, sin], axis=1)
    id_c = jnp.concatenate([one, one], axis=1)
    id_s = jnp.concatenate([zero, zero], axis=1)
    cs, us, ds = [], [], []
    for l0 in range(0, width, HEAD_DIM):
        rot = (l0 % 128) < rot_lanes
        cs.append(cos_h if rot else id_c)
        us.append(up_h if rot else id_s)
        ds.append(dn_h if rot else id_s)
    return jnp.concatenate(cs, axis=1), jnp.concatenate(us, axis=1), jnp.concatenate(ds, axis=1)


def _iota(shape, dim):
    return lax.broadcasted_iota(jnp.int32, shape, dim)


def _cparams(sem):
    return pltpu.CompilerParams(dimension_semantics=sem, vmem_limit_bytes=VMEM_LIMIT)


def _ln_body(x_ref, g_ref, b_ref, o_ref):
    x = x_ref[...]
    mu = jnp.mean(x, axis=-1, keepdims=True)
    xc = x - mu
    var = jnp.mean(xc * xc, axis=-1, keepdims=True)
    o_ref[...] = xc * lax.rsqrt(var + LN_EPS) * g_ref[...] + b_ref[...]


def _layer_norm(x2, g, b, tm=512):
    m, d = x2.shape
    return pl.pallas_call(
        _ln_body,
        grid=(m // tm,),
        in_specs=[pl.BlockSpec((tm, d), lambda i: (i, 0)),
                  pl.BlockSpec((1, d), lambda i: (0, 0)),
                  pl.BlockSpec((1, d), lambda i: (0, 0))],
        out_specs=pl.BlockSpec((tm, d), lambda i: (i, 0)),
        out_shape=jax.ShapeDtypeStruct((m, d), F32),
        compiler_params=_cparams(("parallel",)),
        name="layer_norm",
    )(x2, g.reshape(1, d), b.reshape(1, d))


def _proj_body(x_ref, w_ref, b_ref, o_ref):
    o_ref[...] = jnp.dot(x_ref[...].astype(BF16), w_ref[...], preferred_element_type=F32) + b_ref[...]


def _in_proj(h2, w_bf, b, tm=256):
    m, d = h2.shape
    n = w_bf.shape[1]
    return pl.pallas_call(
        _proj_body,
        grid=(m // tm,),
        in_specs=[pl.BlockSpec((tm, d), lambda i: (i, 0)),
                  pl.BlockSpec((d, n), lambda i: (0, 0)),
                  pl.BlockSpec((1, n), lambda i: (0, 0))],
        out_specs=pl.BlockSpec((tm, n), lambda i: (i, 0)),
        out_shape=jax.ShapeDtypeStruct((m, n), F32),
        compiler_params=_cparams(("parallel",)),
        name="in_proj",
    )(h2, w_bf, b.reshape(1, n))


def _out_body(ya_ref, yb_ref, yc_ref, yd_ref, h_ref, w_ref, g_ref, b_ref, o_ref):
    y = _mm(ya_ref[...], w_ref[0:GROUP_W, :])
    y += _mm(yb_ref[...], w_ref[GROUP_W:2 * GROUP_W, :])
    y += _mm(yc_ref[...], w_ref[2 * GROUP_W:3 * GROUP_W, :])
    y += _mm(yd_ref[...], w_ref[3 * GROUP_W:4 * GROUP_W, :])
    r = DN_ALPHA * h_ref[...] + y
    mu = jnp.mean(r, axis=-1, keepdims=True)
    rc = r - mu
    var = jnp.mean(rc * rc, axis=-1, keepdims=True)
    o_ref[...] = rc * lax.rsqrt(var + LN_EPS) * g_ref[...] + b_ref[...]


def _out_proj(ys, h2, w_bf, g, b, tm=512):
    m, d = h2.shape
    yspec = pl.BlockSpec((tm, GROUP_W), lambda i: (i, 0))
    return pl.pallas_call(
        _out_body,
        grid=(m // tm,),
        in_specs=[yspec, yspec, yspec, yspec,
                  pl.BlockSpec((tm, d), lambda i: (i, 0)),
                  pl.BlockSpec((4 * GROUP_W, d), lambda i: (0, 0)),
                  pl.BlockSpec((1, d), lambda i: (0, 0)),
                  pl.BlockSpec((1, d), lambda i: (0, 0))],
        out_specs=pl.BlockSpec((tm, d), lambda i: (i, 0)),
        out_shape=jax.ShapeDtypeStruct((m, d), F32),
        compiler_params=_cparams(("parallel",)),
        name="out_proj",
    )(*[y.reshape(m, GROUP_W) for y in ys], h2, w_bf, g.reshape(1, d), b.reshape(1, d))


def _mixer_a_body(sink_ref, q_ref, kv_ref, z_ref, cos_ref, up_ref, dn_ref, o_ref, k_ref, vt_ref):
    i = pl.program_id(1)
    seq = kv_ref.shape[1]
    n_kv = N_HEADS // A_GROUP

    @pl.when(i == 0)
    def _():
        def rope_rows(c, carry):
            for u in range(PRO_ROWS // QBLK):
                t0 = c * (PRO_ROWS // QBLK) + u
                rows = pl.ds(pl.multiple_of(t0 * QBLK, QBLK), QBLK)
                kr = _rope(kv_ref[0, rows, 0:128], cos_ref[rows, 0:128], up_ref[rows, 0:128], dn_ref[rows, 0:128])
                v_t = kv_ref[0, rows, 128:256].T.astype(BF16)
                ones_rows = jnp.ones((V_AUG - HEAD_DIM, QBLK), BF16)
                for g in range(n_kv):
                    k_ref[g, rows, :] = kr[:, g * HEAD_DIM:(g + 1) * HEAD_DIM].astype(BF16)
                    vt_ref[g, t0] = jnp.concatenate([v_t[g * HEAD_DIM:(g + 1) * HEAD_DIM, :], ones_rows], axis=0)
            return carry
        lax.fori_loop(0, seq // PRO_ROWS, rope_rows, 0)

    n_blk = A_WINDOW // QBLK + 1
    span = n_blk * QBLK
    first_head = _iota((1, A_GROUP * QBLK), 1) < QBLK

    def query_block(u):
        blk_i = i * A_SUB + u
        sub = slice(u * QBLK, (u + 1) * QBLK)
        q0 = pl.multiple_of(blk_i * QBLK, QBLK)
        qrows = pl.ds(q0, QBLK)
        qr = _rope(q_ref[0, sub, :], cos_ref[qrows, :], up_ref[qrows, :], dn_ref[qrows, :]) * (SCALE * LOG2E)
        qt = qr.T
        kt0 = jnp.maximum(blk_i - A_WINDOW // QBLK, 0)
        k0 = pl.multiple_of(kt0 * QBLK, QBLK)
        rel = (q0 + _iota((1, QBLK), 1)) - (k0 + _iota((span, 1), 0))
        in_band = lax.bitcast_convert_type(rel, jnp.uint32) < A_WINDOW
        bias = jnp.concatenate([jnp.where(in_band, 0.0, NEG)] * A_GROUP, axis=1)
        outs = []
        for g in range(n_kv):
            heads = range(g * A_GROUP, (g + 1) * A_GROUP)
            qg = jnp.concatenate([qt[h * HEAD_DIM:(h + 1) * HEAD_DIM] for h in heads], axis=1).astype(BF16)
            sink = jnp.where(first_head, sink_ref[g * A_GROUP], sink_ref[g * A_GROUP + 1]) * LOG2E
            s = jnp.dot(k_ref[g, pl.ds(k0, span), :], qg, preferred_element_type=F32) + bias
            m = jnp.maximum(jnp.max(s, axis=0, keepdims=True), sink)
            p = jnp.exp2(s - m).astype(BF16)
            o = jnp.zeros((V_AUG, A_GROUP * QBLK), F32)
            for c in range(n_blk):
                o = o + jnp.dot(vt_ref[g, kt0 + c], p[c * QBLK:(c + 1) * QBLK], preferred_element_type=F32)
            o = o[0:HEAD_DIM] / (o[HEAD_DIM:HEAD_DIM + 1] + jnp.exp2(sink - m))
            outs.extend(o[:, r * QBLK:(r + 1) * QBLK] for r in range(A_GROUP))
        o_ref[0, sub, :] = (jnp.concatenate(outs, axis=0).T * _silu(z_ref[0, sub, :])).astype(Y_DTYPE)

    for u in range(A_SUB):
        query_block(u)


def _mixer_a(proj, sinks, tabs):
    bsz, seq, _ = proj.shape
    cos, up, dn = tabs
    blk = lambda name: _new_offset(name) // GROUP_W
    tspec = pl.BlockSpec((seq, GROUP_W), lambda b, i: (0, 0))
    return pl.pallas_call(
        _mixer_a_body,
        grid=(bsz, seq // (A_SUB * QBLK)),
        in_specs=[pl.BlockSpec(memory_space=pltpu.SMEM),
                  pl.BlockSpec((1, A_SUB * QBLK, GROUP_W), lambda b, i: (b, i, blk('a_q'))),
                  pl.BlockSpec((1, seq, GROUP_W), lambda b, i: (b, 0, blk('a_k'))),
                  pl.BlockSpec((1, A_SUB * QBLK, GROUP_W), lambda b, i: (b, i, blk('a_z'))),
                  tspec, tspec, tspec],
        out_specs=pl.BlockSpec((1, A_SUB * QBLK, GROUP_W), lambda b, i: (b, i, 0)),
        out_shape=jax.ShapeDtypeStruct((bsz, seq, GROUP_W), Y_DTYPE),
        scratch_shapes=[pltpu.VMEM((N_HEADS // A_GROUP, seq, HEAD_DIM), BF16),
                        pltpu.VMEM((N_HEADS // A_GROUP, seq // QBLK, V_AUG, QBLK), BF16)],
        compiler_params=_cparams(("arbitrary", "arbitrary")),
        name="mixer_a",
    )(sinks, proj, proj, proj, cos, up, dn)


B_CHUNK = 128
B_STEP = 512
B_AUG = HEAD_DIM + 16


def _conv_silu(x, prev, w, b):
    row8 = _iota((8, x.shape[1]), 0)
    acc = b + w[B_CONV - 1:B_CONV, :] * x
    for s in range(1, B_CONV):
        xs = pltpu.roll(x, s, axis=0)
        ps = pltpu.roll(prev, s, axis=0)
        shifted = jnp.concatenate([jnp.where(row8 < s, ps, xs[0:8]), xs[8:]], axis=0)
        acc = acc + w[B_CONV - 1 - s:B_CONV - s, :] * shifted
    return _silu(acc)


def _mixer_b_body(xq_ref, xk_ref, v_ref, og_ref, z_ref, if_ref, cw_ref, cb_ref, o_ref, xprev_ref, c_ref, m_ref):
    c = pl.program_id(1)
    L = B_CHUNK

    @pl.when(c == 0)
    def _():
        xprev_ref[...] = jnp.zeros_like(xprev_ref)
        c_ref[...] = jnp.zeros_like(c_ref)
        m_ref[...] = jnp.zeros_like(m_ref)

    tri = (_iota((L, L), 1) <= _iota((L, L), 0)).astype(F32)
    key_first = _iota((L, L), 0) <= _iota((L, L), 1)
    half_of_lane = _iota((1, 128), 1) // HEAD_DIM
    ones_rows = jnp.ones((B_AUG - HEAD_DIM, L), F32)

    def chunk(rows, prev_q, prev_k, state):
        xq = xq_ref[0, rows, :]
        xk = xk_ref[0, rows, :]
        qc = _conv_silu(xq, prev_q, cw_ref[:, 0:GROUP_W], cb_ref[:, 0:GROUP_W]) * SCALE
        kc = _conv_silu(xk, prev_k, cw_ref[:, GROUP_W:], cb_ref[:, GROUP_W:])
        q_t = qc.T
        v_t = v_ref[0, rows, :].T

        gates = if_ref[0, rows, :]
        bcum = _mm_sel(tri, _log_sigmoid(gates))
        gates_t = gates.T
        bcum_t = bcum.T
        outs, new_state = [], []
        for h in range(N_HEADS):
            pair = slice((h // 2) * 128, (h // 2 + 1) * 128)
            own = half_of_lane == (h % 2)
            k_pair = kc[:, pair]
            q_pair_t = q_t[pair, :]
            i_row = gates_t[h:h + 1, :]
            b_row = bcum_t[N_HEADS + h:N_HEADS + h + 1, :]
            c_col = gates[:, h:h + 1] - bcum[:, N_HEADS + h:N_HEADS + h + 1]
            cst, m_prev = state[h]
            dmat = jnp.where(key_first, b_row + c_col, NEG)
            inter = b_row + m_prev
            m_t = jnp.maximum(jnp.max(dmat, axis=0, keepdims=True), inter)
            smat = _mm(jnp.where(own, k_pair, 0.0), q_pair_t) * jnp.exp(dmat - m_t)
            vaug_t = jnp.concatenate([v_t[h * HEAD_DIM:(h + 1) * HEAD_DIM, :], ones_rows], axis=0)
            tot = _mm(vaug_t, smat) + jnp.exp(inter - m_t) * _mm(cst, q_pair_t)
            num = tot[0:HEAD_DIM, :]
            den = tot[HEAD_DIM:HEAD_DIM + 1, :]
            outs.append(num / jnp.maximum(jnp.abs(den), jnp.exp(-m_t)))
            b_last = b_row[:, L - 1:L]
            d_row = b_last - b_row + i_row
            m_new = jnp.maximum(b_last + m_prev, jnp.max(d_row, axis=1, keepdims=True))
            w_row = jnp.exp(d_row - m_new)
            decay = jnp.exp(b_last + m_prev - m_new)
            new_state.append((decay * cst + jnp.where(own, _mm(vaug_t * w_row, k_pair), 0.0), m_new))
        o_ref[0, rows, :] = (_sigmoid(og_ref[0, rows, :]) * jnp.concatenate(outs, axis=0).T
                             * _silu(z_ref[0, rows, :])).astype(Y_DTYPE)
        return xq[L - 8:L, :], xk[L - 8:L, :], new_state

    prev_q = xprev_ref[:, 0:GROUP_W]
    prev_k = xprev_ref[:, GROUP_W:]
    state = [(c_ref[h], m_ref[h:h + 1, 0:1]) for h in range(N_HEADS)]
    for u in range(xq_ref.shape[1] // L):
        prev_q, prev_k, state = chunk(slice(u * L, (u + 1) * L), prev_q, prev_k, state)
    xprev_ref[:, 0:GROUP_W] = prev_q
    xprev_ref[:, GROUP_W:] = prev_k
    for h in range(N_HEADS):
        c_ref[h] = state[h][0]
        m_ref[h:h + 1, :] = jnp.broadcast_to(state[h][1], (1, 128))


def _mixer_b(proj, conv_w, conv_b):
    bsz, seq, _ = proj.shape
    L = B_STEP
    blk = lambda name: _new_offset(name) // GROUP_W
    return pl.pallas_call(
        _mixer_b_body,
        grid=(bsz, seq // L),
        in_specs=[pl.BlockSpec((1, L, GROUP_W), lambda b, c: (b, c, blk('b_q'))),
                  pl.BlockSpec((1, L, GROUP_W), lambda b, c: (b, c, blk('b_k'))),
                  pl.BlockSpec((1, L, GROUP_W), lambda b, c: (b, c, blk('b_v'))),
                  pl.BlockSpec((1, L, GROUP_W), lambda b, c: (b, c, blk('b_o'))),
                  pl.BlockSpec((1, L, GROUP_W), lambda b, c: (b, c, blk('b_z'))),
                  pl.BlockSpec((1, L, 128), lambda b, c: (b, c, _new_offset('b_if') // 128)),
                  pl.BlockSpec((B_CONV, 2 * GROUP_W), lambda b, c: (0, 0)),
                  pl.BlockSpec((1, 2 * GROUP_W), lambda b, c: (0, 0))],
        out_specs=pl.BlockSpec((1, L, GROUP_W), lambda b, c: (b, c, 0)),
        out_shape=jax.ShapeDtypeStruct((bsz, seq, GROUP_W), Y_DTYPE),
        scratch_shapes=[pltpu.VMEM((8, 2 * GROUP_W), F32),
                        pltpu.VMEM((N_HEADS, B_AUG, 128), F32),
                        pltpu.VMEM((8, 128), F32)],
        compiler_params=_cparams(("arbitrary", "arbitrary")),
        name="mixer_b",
    )(proj, proj, proj, proj, proj, proj, conv_w, conv_b.reshape(1, -1))


def _mixer_c_body(layer, q_ref, f_ref, i_ref, z_ref, lb_ref, gn_ref, tri_ref, piv_ref, o_ref, st_ref):
    c = pl.program_id(1)
    L = C_CHUNK
    nsub = L // C_SUB

    @pl.when(c == 0)
    def _():
        st_ref[...] = jnp.zeros_like(st_ref)

    raw = lb_ref[...]
    ex = jnp.exp(raw - jnp.max(raw, axis=0, keepdims=True))
    if layer == 0:
        lb = jnp.zeros((1, GROUP_W), F32)
    else:
        lb = jnp.sum(ex[1:layer + 1], axis=0, keepdims=True) / jnp.sum(ex, axis=0, keepdims=True)
    tri = (_iota((L, L), 1) <= _iota((L, L), 0)).astype(F32)
    head_of_lane = _iota((1, GROUP_W), 1) // HEAD_DIM
    same_head = ((_iota((GROUP_W, GROUP_W), 0) // HEAD_DIM)
                 == (_iota((GROUP_W, GROUP_W), 1) // HEAD_DIM))
    ones_bd = same_head.astype(BF16)

    def chunk(rows, st):
        f = lb + (1.0 - lb) * _sigmoid(f_ref[0, rows, :])
        logf = jnp.log(f)
        k = 1.0 - f
        q = _silu(q_ref[0, rows, :])
        v = i_ref[0, rows, :]
        a = _mm_sel(tri, logf) * LOG2E
        o = _mm_nt(q * jnp.exp2(a), st)

        qsts, kts, vreps = [], [], []
        for i in range(1, nsub):
            r0 = i * C_SUB
            piv = a[r0 - 1:r0, :]
            qt = q[r0:r0 + C_SUB] * jnp.exp2(a[r0:r0 + C_SUB] - piv)
            qsts.extend(jnp.where(head_of_lane == h, qt, 0.0) for h in range(N_HEADS))
            kts.append((k[0:r0] * jnp.exp2(piv - a[0:r0])).astype(BF16))
            vreps.append(v[0:r0].astype(BF16))
        scores = _mm_nt(jnp.concatenate(qsts, axis=0), jnp.concatenate(kts, axis=0))
        r = _dot_bf16((scores * piv_ref[...]).astype(BF16), jnp.concatenate(vreps, axis=0))
        pieces = [o[0:C_SUB]]
        for i in range(1, nsub):
            acc = o[i * C_SUB:(i + 1) * C_SUB]
            for h in range(N_HEADS):
                g0 = ((i - 1) * N_HEADS + h) * C_SUB
                acc = acc + jnp.where(head_of_lane == h, r[g0:g0 + C_SUB], 0.0)
            pieces.append(acc)
        o = jnp.concatenate(pieces, axis=0)

        a3 = a.reshape(nsub, C_SUB, GROUP_W)
        q3 = q.reshape(nsub, C_SUB, GROUP_W)
        k3 = k.reshape(nsub, C_SUB, GROUP_W)
        v3 = v.reshape(nsub, C_SUB, GROUP_W)
        ps = []
        for s in range(C_SUB):
            e = jnp.exp2(a3 - a3[:, s:s + 1, :] + tri_ref[s])
            ps.append((q3 * k3[:, s:s + 1, :] * e).reshape(L, GROUP_W).astype(BF16))
        w = jnp.dot(jnp.concatenate(ps, axis=0), ones_bd, preferred_element_type=F32)
        for s in range(C_SUB):
            vs = jnp.broadcast_to(v3[:, s:s + 1, :], (nsub, C_SUB, GROUP_W)).reshape(L, GROUP_W)
            o = o + w[s * L:(s + 1) * L] * vs

        a_last = a[L - 1:L, :]
        kdec = k * jnp.exp2(a_last - a)
        st_new = st * jnp.exp2(a_last) + jnp.where(same_head, _mm_tn(v, kdec), 0.0)

        ms = jnp.dot((o * o).astype(BF16), ones_bd, preferred_element_type=F32) * (1.0 / HEAD_DIM)
        o = o * lax.rsqrt(ms + C_EPS) * gn_ref[...]
        o_ref[0, rows, :] = (o * _silu(z_ref[0, rows, :])).astype(Y_DTYPE)
        return st_new

    st = st_ref[...]
    for u in range(q_ref.shape[1] // L):
        st = chunk(slice(u * L, (u + 1) * L), st)
    st_ref[...] = st


def _mixer_c(proj, c_lb, norm_g, layer):
    bsz, seq, _ = proj.shape
    L = C_STEP
    blk = lambda name: _new_offset(name) // GROUP_W
    spec = lambda name: pl.BlockSpec((1, L, GROUP_W), lambda b, c: (b, c, blk(name)))
    vec = pl.BlockSpec((1, GROUP_W), lambda b, c: (0, 0))
    t_ge_s = np.arange(C_SUB)[None, :, None] >= np.arange(C_SUB)[:, None, None]
    causal_bias = np.broadcast_to(np.where(t_ge_s, 0.0, NEG), (C_SUB, C_SUB, GROUP_W)).astype(np.float32)
    pivots = np.arange(1, C_CHUNK // C_SUB)
    row_piv = np.repeat(pivots, N_HEADS * C_SUB)
    col_piv = np.repeat(pivots, pivots * C_SUB)
    same_pivot = (row_piv[:, None] == col_piv[None, :]).astype(np.float32)
    return pl.pallas_call(
        functools.partial(_mixer_c_body, layer),
        grid=(bsz, seq // L),
        in_specs=[spec('c_q'), spec('c_f'), spec('c_i'), spec('c_z'),
                  pl.BlockSpec((DEPTH, GROUP_W), lambda b, c: (0, 0)), vec,
                  pl.BlockSpec((C_SUB, C_SUB, GROUP_W), lambda b, c: (0, 0, 0)),
                  pl.BlockSpec(same_pivot.shape, lambda b, c: (0, 0))],
        out_specs=pl.BlockSpec((1, L, GROUP_W), lambda b, c: (b, c, 0)),
        out_shape=jax.ShapeDtypeStruct((bsz, seq, GROUP_W), Y_DTYPE),
        scratch_shapes=[pltpu.VMEM((GROUP_W, GROUP_W), F32)],
        compiler_params=_cparams(("arbitrary", "arbitrary")),
        name="mixer_c",
    )(proj, proj, proj, proj, c_lb, norm_g.reshape(1, -1), jnp.asarray(causal_bias), jnp.asarray(same_pivot))


N_CMP_PAD = 128
KT = 512
D_QBLK = 256
WIN_HEADS = N_HEADS


def _mixer_d_body(q_ref, z_ref, ksv_ref, kwv_ref, cin_ref, g_ref, cos_ref, up_ref, dn_ref, kcos_ref, kup_ref,
                  kdn_ref, ccos_ref, cup_ref, cdn_ref, w1_ref, pe_ref, w2_ref, ov_ref, o_ref,
                  k_ref, vts_ref, vtw_ref, kc_ref, vct_ref, sel_ref):
    QBLK = D_QBLK
    i = pl.program_id(1)
    seq = ksv_ref.shape[1]
    half_cmp = CMP_LEN // 2

    @pl.when(i == 0)
    def _():
        def rope_rows(c, carry):
            for u in range(PRO_ROWS // KT):
                t0 = c * (PRO_ROWS // KT) + u
                rows = pl.ds(pl.multiple_of(t0 * KT, KT), KT)
                kvs = _rope(ksv_ref[0, rows, :], kcos_ref[rows, :], kup_ref[rows, :], kdn_ref[rows, :])
                kvw = _rope(kwv_ref[0, rows, :], kcos_ref[rows, :], kup_ref[rows, :], kdn_ref[rows, :])
                k_ref[0, rows, :] = kvs[:, 0:HEAD_DIM].astype(BF16)
                k_ref[1, rows, :] = kvw[:, 0:HEAD_DIM].astype(BF16)
                ones_rows = jnp.ones((V_AUG - HEAD_DIM, KT), BF16)
                vts_ref[t0] = jnp.concatenate([kvs.T[HEAD_DIM:2 * HEAD_DIM, :].astype(BF16), ones_rows], axis=0)
                vwt = jnp.concatenate([kvw.T[HEAD_DIM:2 * HEAD_DIM, :].astype(BF16), ones_rows], axis=0)
                for w in range(KT // QBLK):
                    vtw_ref[t0 * (KT // QBLK) + w] = vwt[:, w * QBLK:(w + 1) * QBLK]
            return carry
        lax.fori_loop(0, seq // PRO_ROWS, rope_rows, 0)
        u0 = jnp.zeros((N_CMP_PAD, 128), F32)
        u1 = jnp.zeros((N_CMP_PAD, 128), F32)
        for r in range(half_cmp):
            xr = cin_ref[0, pl.ds(r, N_CMP_PAD, stride=CMP_STRIDE), :]
            u0 = u0 + _mm_hi(xr + pe_ref[r:r + 1, :], w1_ref[r])
            u1 = u1 + _mm_hi(xr + pe_ref[half_cmp + r:half_cmp + r + 1, :], w1_ref[half_cmp + r])
        pre = u0 + pltpu.roll(u1, N_CMP_PAD - 1, axis=0)
        cv = _rope(_mm_hi(_silu(pre), w2_ref[...]), ccos_ref[...], cup_ref[...], cdn_ref[...])
        kc_ref[...] = cv[:, 0:HEAD_DIM]
        vct_ref[...] = cv.T[HEAD_DIM:2 * HEAD_DIM, :].astype(BF16)

    q0 = pl.multiple_of(i * QBLK, QBLK)
    qrows = pl.ds(q0, QBLK)
    qr = _rope(q_ref[0], cos_ref[qrows, :], up_ref[qrows, :], dn_ref[qrows, :]) * (SCALE * LOG2E)
    qt = qr.T
    qst = jnp.concatenate([qt[h * HEAD_DIM:(h + 1) * HEAD_DIM] for h in range(N_HEADS)], axis=1)
    qst_bf = qst.astype(BF16)
    cols4 = N_HEADS * QBLK
    qpos = q0 + _iota((1, QBLK), 1)

    def lanes4(x):
        return jnp.concatenate([x] * N_HEADS, axis=1)

    n_win = NSA_WINDOW // QBLK + 1
    span = n_win * QBLK
    kt0 = jnp.maximum(i - NSA_WINDOW // QBLK, 0)
    k0 = pl.multiple_of(kt0 * QBLK, QBLK)
    rel = qpos - (k0 + _iota((span, 1), 0))
    in_band = lax.bitcast_convert_type(rel, jnp.uint32) < NSA_WINDOW
    win_bias = jnp.where(in_band, 0.0, NEG)
    kw = k_ref[1, pl.ds(k0, span), :]
    o_wins = []
    for h in range(0, N_HEADS, WIN_HEADS):
        cs = slice(h * QBLK, (h + WIN_HEADS) * QBLK)
        s = jnp.dot(kw, qst_bf[:, cs], preferred_element_type=F32) + jnp.concatenate([win_bias] * WIN_HEADS, axis=1)
        p = jnp.exp2(s - jnp.max(s, axis=0, keepdims=True)).astype(BF16)
        acc = jnp.zeros((V_AUG, WIN_HEADS * QBLK), F32)
        for c in range(n_win):
            acc = acc + jnp.dot(vtw_ref[kt0 + c], p[c * QBLK:(c + 1) * QBLK], preferred_element_type=F32)
        o_wins.append(acc[0:HEAD_DIM] / acc[HEAD_DIM:HEAD_DIM + 1])
    o_win = jnp.concatenate(o_wins, axis=1)

    sc = _mm_hi(kc_ref[...], qst)
    ends = _iota((N_CMP_PAD, 1), 0) * CMP_STRIDE + (CMP_LEN - 1)
    scm = sc + lanes4(jnp.where(ends <= qpos, 0.0, NEG))
    e = jnp.exp2(scm - jnp.max(scm, axis=0, keepdims=True))
    l = jnp.sum(e, axis=0, keepdims=True)
    any_valid = lanes4(jnp.where(qpos >= CMP_LEN - 1, 1.0, 0.0))
    pc = e * (any_valid / l)
    o_cmp = jnp.dot(vct_ref[...], pc.astype(BF16), preferred_element_type=F32)
    psum = pc[:, 0:QBLK] + pc[:, QBLK:2 * QBLK] + pc[:, 2 * QBLK:3 * QBLK] + pc[:, 3 * QBLK:4 * QBLK]

    n_slc = seq // SLC_LEN
    imp = _mm_sel(ov_ref[...], psum)
    jl = _iota((n_slc, 1), 0)
    cur = jnp.right_shift(qpos, SLC_LEN.bit_length() - 1)
    imp = jnp.where(jl == 0, FORCE_SCORE, imp)
    imp = jnp.where(jl == cur, FORCE_SCORE, imp)
    imp = jnp.where(jl == cur - 1, FORCE_SCORE, imp)
    imp = jnp.where(jl <= cur, imp, NEG)
    rank = jnp.zeros((n_slc, QBLK), F32)
    for jp in range(n_slc):
        cj = imp[jp:jp + 1, :]
        tie = jnp.where(jl > jp, 1.0, 0.0)
        rank = rank + jnp.where(cj > imp, 1.0, jnp.where(cj == imp, tie, 0.0))
    sel_bias = jnp.where(rank < N_SEL, 0.0, NEG)
    blocks_per_tile = KT // SLC_LEN
    for t in range(n_slc // blocks_per_tile):
        sel_ref[t] = sel_bias[t * blocks_per_tile:(t + 1) * blocks_per_tile, :]

    def sel_tile(kt, carry):
        m_old, acc = carry
        kbase = pl.multiple_of(kt * KT, KT)
        picks = sel_ref[kt]
        chosen = jnp.concatenate([jnp.broadcast_to(picks[j:j + 1, :], (SLC_LEN, QBLK))
                                  for j in range(blocks_per_tile)], axis=0)
        bias = jnp.where((kbase + _iota((KT, 1), 0)) <= qpos, chosen, NEG)
        s = jnp.dot(k_ref[0, pl.ds(kbase, KT), :], qst_bf, preferred_element_type=F32) + lanes4(bias)
        m_new = jnp.maximum(m_old, jnp.max(s, axis=0, keepdims=True))
        p = jnp.exp2(s - m_new)
        acc_new = jnp.exp2(m_old - m_new) * acc + jnp.dot(vts_ref[kt], p.astype(BF16), preferred_element_type=F32)
        return m_new, acc_new

    init = (jnp.full((1, cols4), NEG, F32), jnp.zeros((V_AUG, cols4), F32))
    n_tiles = (i + KT // QBLK) // (KT // QBLK)
    _, acc_s = lax.fori_loop(0, n_tiles, sel_tile, init)
    o_slc = acc_s[0:HEAD_DIM] / acc_s[HEAD_DIM:HEAD_DIM + 1]

    gate = _sigmoid(g_ref[0]).T
    outs = []
    for h in range(N_HEADS):
        cs = slice(h * QBLK, (h + 1) * QBLK)
        outs.append(gate[3 * h:3 * h + 1, :] * o_cmp[:, cs] + gate[3 * h + 1:3 * h + 2, :] * o_slc[:, cs]
                    + gate[3 * h + 2:3 * h + 3, :] * o_win[:, cs])
    o_ref[0] = (jnp.concatenate(outs, axis=0).T * _silu(z_ref[0])).astype(Y_DTYPE)


def _mixer_d(proj, qtabs, ktabs, ctabs, w1bd, pe2, w2bd, overlap):
    QBLK = D_QBLK
    bsz, seq, _ = proj.shape
    blk = lambda name: _new_offset(name) // GROUP_W
    full = lambda shape: pl.BlockSpec(shape, lambda b, i: (0,) * len(shape))
    return pl.pallas_call(
        _mixer_d_body,
        grid=(bsz, seq // QBLK),
        in_specs=[pl.BlockSpec((1, QBLK, GROUP_W), lambda b, i: (b, i, blk('d_q'))),
                  pl.BlockSpec((1, QBLK, GROUP_W), lambda b, i: (b, i, blk('d_z'))),
                  pl.BlockSpec((1, seq, 128), lambda b, i: (b, 0, _new_offset('d_ksv') // 128)),
                  pl.BlockSpec((1, seq, 128), lambda b, i: (b, 0, _new_offset('d_kwv') // 128)),
                  pl.BlockSpec((1, seq, 128), lambda b, i: (b, 0, _new_offset('d_kvc') // 128)),
                  pl.BlockSpec((1, QBLK, 128), lambda b, i: (b, i, _new_offset('d_g') // 128)),
                  full((seq, GROUP_W)), full((seq, GROUP_W)), full((seq, GROUP_W)),
                  full((seq, 128)), full((seq, 128)), full((seq, 128)),
                  full((N_CMP_PAD, 128)), full((N_CMP_PAD, 128)), full((N_CMP_PAD, 128)),
                  full((CMP_LEN, 128, 128)), full((CMP_LEN, 128)), full((128, 128)),
                  full((seq // SLC_LEN, N_CMP_PAD))],
        out_specs=pl.BlockSpec((1, QBLK, GROUP_W), lambda b, i: (b, i, 0)),
        out_shape=jax.ShapeDtypeStruct((bsz, seq, GROUP_W), Y_DTYPE),
        scratch_shapes=[pltpu.VMEM((2, seq, HEAD_DIM), BF16),
                        pltpu.VMEM((seq // KT, V_AUG, KT), BF16),
                        pltpu.VMEM((seq // QBLK, V_AUG, QBLK), BF16),
                        pltpu.VMEM((N_CMP_PAD, HEAD_DIM), F32),
                        pltpu.VMEM((HEAD_DIM, N_CMP_PAD), BF16),
                        pltpu.VMEM((seq // KT, KT // SLC_LEN, QBLK), F32)],
        compiler_params=_cparams(("arbitrary", "arbitrary")),
        name="mixer_d",
    )(proj, proj, proj, proj, proj, proj, *qtabs, *ktabs, *ctabs, w1bd, pe2, w2bd, overlap)


def _cmp_params(pe, w1, w2):
    w1 = w1.reshape(2, CMP_LEN, HEAD_DIM, HEAD_DIM)
    z = jnp.zeros((CMP_LEN, HEAD_DIM, HEAD_DIM), F32)
    w1bd = jnp.concatenate([jnp.concatenate([w1[0], z], axis=2), jnp.concatenate([z, w1[1]], axis=2)], axis=1)
    z2 = jnp.zeros((HEAD_DIM, HEAD_DIM), F32)
    w2bd = jnp.concatenate([jnp.concatenate([w2[0], z2], axis=1), jnp.concatenate([z2, w2[1]], axis=1)], axis=0)
    pe2 = jnp.concatenate([pe[0], pe[1]], axis=1)
    return w1bd, pe2, w2bd


def _overlap_matrix(seq):
    n_cmp = (seq - CMP_LEN) // CMP_STRIDE + 1
    starts = np.arange(N_CMP_PAD) * CMP_STRIDE
    blk = np.arange(seq // SLC_LEN)
    ov = ((starts[None, :] < (blk[:, None] + 1) * SLC_LEN) & (starts[None, :] + CMP_LEN > blk[:, None] * SLC_LEN))
    ov = ov & (np.arange(N_CMP_PAD)[None, :] < n_cmp)
    return jnp.asarray(ov.astype(np.float32))


def kernel(x, ln0_g, ln0_b, w_in, b_in, a_sinks, b_conv_w, b_conv_b, c_lb, c_norm_g, d_cmp_pe, d_cmp_w1, d_cmp_w2,
           w_out, ln_g, ln_b):
    bsz, seq, d = x.shape
    pos = jnp.arange(seq)
    qtabs = _rope_tables(pos, GROUP_W, 128)
    ktabs = _rope_tables(pos, 128, HEAD_DIM)
    ctabs = _rope_tables(jnp.arange(N_CMP_PAD) * CMP_STRIDE + (CMP_LEN - 1), 128, HEAD_DIM)
    overlap = _overlap_matrix(seq)

    h = _layer_norm(x.reshape(bsz * seq, d), ln0_g, ln0_b)
    for l in range(DEPTH):
        w_p, b_p = _prep_in_weights(w_in, b_in, l)
        proj = _in_proj(h, w_p, b_p).reshape(bsz, seq, N_PROJ)
        y_a = _mixer_a(proj, a_sinks[l], qtabs)
        y_b = _mixer_b(proj, b_conv_w[l], b_conv_b[l])
        y_c = _mixer_c(proj, c_lb, c_norm_g[l], l)
        w1bd, pe2, w2bd = _cmp_params(d_cmp_pe[l], d_cmp_w1[l], d_cmp_w2[l])
        y_d = _mixer_d(proj, qtabs, ktabs, ctabs, w1bd, pe2, w2bd, overlap)
        h = _out_proj((y_a, y_b, y_c, y_d), h, w_out[l].astype(BF16), ln_g[l], ln_b[l])
    return h.reshape(bsz, seq, d)
```

```python
import functools

import numpy as np
import jax
import jax.numpy as jnp
from jax import lax
from jax.experimental import pallas as pl
from jax.experimental.pallas import tpu as pltpu

F32 = jnp.float32
BF16 = jnp.bfloat16
Y_DTYPE = BF16

D_MODEL = 1024
DEPTH = 2
HEAD_DIM = 64
HALF = HEAD_DIM // 2
N_HEADS = 4
GROUP_W = N_HEADS * HEAD_DIM
N_GROUPS = 4
ROPE_THETA = 10000.0
NEG = -1e30
LN_EPS = 1e-5
A_WINDOW = 128
A_GROUP = 2
A_SUB = 4
V_AUG = HEAD_DIM + 16
LOG2E = 1.4426950408889634
PRO_ROWS = 512
B_CONV = 4
C_CHUNK = 64
C_SUB = 8
C_STEP = 256
C_EPS = 1e-6
CMP_LEN = 32
CMP_STRIDE = 16
SLC_LEN = 64
N_SEL = 8
NSA_WINDOW = 512
FORCE_SCORE = 1e6
DN_ALPHA = (2.0 * DEPTH) ** 0.25
QBLK = 128
SCALE = HEAD_DIM ** -0.5

VMEM_LIMIT = 56 * 1024 * 1024

LANES = 128

ORIG_SPLITS = (
    ('a_q', 256), ('a_k', 128), ('a_v', 128), ('a_z', 256),
    ('b_q', 256), ('b_k', 256), ('b_v', 256), ('b_if', 8), ('b_o', 256), ('b_z', 256),
    ('c_q', 256), ('c_f', 256), ('c_i', 256), ('c_z', 256),
    ('d_q', 256), ('d_kvc', 128), ('d_ksv', 128), ('d_kwv', 128), ('d_g', 12), ('d_z', 256),
)
N_COLS = sum(w for _, w in ORIG_SPLITS)
GATE_SLOTS = ('b_if', 'd_g')
NEW_LAYOUT = (
    ('a_q', 256), ('a_k', 128), ('a_v', 128), ('a_z', 256), ('b_q', 256), ('b_k', 256), ('b_v', 256),
    ('b_o', 256), ('b_z', 256), ('c_q', 256), ('c_f', 256), ('c_i', 256), ('c_z', 256),
    ('d_q', 256), ('d_kvc', 128), ('d_ksv', 128), ('d_kwv', 128), ('gates', 128), ('d_z', 256),
)
N_PROJ = sum(w for _, w in NEW_LAYOUT)
D_GATE_LANE = dict(ORIG_SPLITS)['b_if']


def _offset(layout, name):
    off = 0
    for n, w in layout:
        if n == name:
            return off
        off += w
    raise KeyError(name)


def _new_offset(name):
    return _offset(NEW_LAYOUT, name)


def _permuted_cols(load):
    sizes = dict(ORIG_SPLITS)
    for name, width in NEW_LAYOUT:
        new = _new_offset(name)
        if name != 'gates':
            old = _offset(ORIG_SPLITS, name)
            lo = old // LANES * LANES
            hi = min(-(-(old + width) // LANES) * LANES, N_COLS)
            yield new, load(lo, hi)[..., old - lo:old - lo + width]
            continue
        gates, lane0 = None, 0
        for slot in GATE_SLOTS:
            old = _offset(ORIG_SPLITS, slot)
            lo = old // LANES * LANES
            tile = load(lo, lo + LANES)
            if old - lo != lane0:
                tile = pltpu.roll(tile, (lane0 - (old - lo)) % LANES, axis=tile.ndim - 1)
            lane = _iota(tile.shape, tile.ndim - 1)
            part = jnp.where((lane >= lane0) & (lane < lane0 + sizes[slot]), tile, 0.0)
            gates = part if gates is None else gates + part
            lane0 += sizes[slot]
        yield new, gates


def _prep_body(w_ref, b_ref, wo_ref, bo_ref):
    for new, cols in _permuted_cols(lambda lo, hi: w_ref[0, :, lo:hi]):
        wo_ref[:, new:new + cols.shape[-1]] = cols.astype(BF16)
    for new, cols in _permuted_cols(lambda lo, hi: b_ref[0, :, lo:hi]):
        bo_ref[:, new:new + cols.shape[-1]] = cols


def _prep_in_weights(w_in, b_in, layer, rows=128):
    depth, d, n = w_in.shape
    return pl.pallas_call(
        _prep_body,
        grid=(d // rows,),
        in_specs=[pl.BlockSpec((1, rows, n), lambda r: (layer, r, 0)),
                  pl.BlockSpec((1, 1, n), lambda r: (layer, 0, 0))],
        out_specs=[pl.BlockSpec((rows, N_PROJ), lambda r: (r, 0)),
                   pl.BlockSpec((1, N_PROJ), lambda r: (0, 0))],
        out_shape=[jax.ShapeDtypeStruct((d, N_PROJ), BF16), jax.ShapeDtypeStruct((1, N_PROJ), F32)],
        compiler_params=_cparams(("arbitrary",)),
        name="prep_in_weights",
    )(w_in, b_in.reshape(depth, 1, n))


def _mm(a, b):
    return jnp.dot(a.astype(BF16), b.astype(BF16), preferred_element_type=F32)


def _mm_nt(a, b):
    return lax.dot_general(a.astype(BF16), b.astype(BF16), (((1,), (1,)), ((), ())),
                           preferred_element_type=F32)


def _mm_tn(a, b):
    return lax.dot_general(a.astype(BF16), b.astype(BF16), (((0,), (0,)), ((), ())),
                           preferred_element_type=F32)


def _dot_bf16(a, b):
    return jnp.dot(a, b, preferred_element_type=F32)


def _split_bf16(x, terms):
    out = []
    for _ in range(terms - 1):
        t = x.astype(BF16)
        out.append(t)
        x = x - t.astype(F32)
    out.append(x.astype(BF16))
    return out


def _mm_hi(a, b):
    ah, al = _split_bf16(a, 2)
    bh, bl = _split_bf16(b, 2)
    return _dot_bf16(ah, bh) + (_dot_bf16(ah, bl) + _dot_bf16(al, bh))


def _mm_sel(sel, x):
    sel = sel.astype(BF16)
    x1, x2, x3 = _split_bf16(x, 3)
    return _dot_bf16(sel, x1) + (_dot_bf16(sel, x2) + _dot_bf16(sel, x3))


def _sigmoid(x):
    return 1.0 / (1.0 + jnp.exp2(x * (-LOG2E)))


def _silu(x):
    return x * _sigmoid(x)


def _log_sigmoid(x):
    return jnp.minimum(x, 0.0) - jnp.log1p(jnp.exp(-jnp.abs(x)))


def _rope(x, cos, sin_up, sin_dn):
    w = x.shape[-1]
    up = pltpu.roll(x, w - HALF, axis=1)
    dn = pltpu.roll(x, HALF, axis=1)
    return x * cos + up * sin_up + dn * sin_dn


def _rope_tables(pos, width, rot_lanes):
    inv = ROPE_THETA ** (-jnp.arange(HALF, dtype=F32) / HALF)
    ang = pos.astype(F32)[:, None] * inv[None, :]
    cos, sin = jnp.cos(ang), jnp.sin(ang)
    zero, one = jnp.zeros_like(sin), jnp.ones_like(cos)
    cos_h = jnp.concatenate([cos, cos], axis=1)
    up_h = jnp.concatenate([-sin, zero], axis=1)
    dn_h = jnp.concatenate([zero, sin], axis=1)
    id_c = jnp.concatenate([one, one], axis=1)
    id_s = jnp.concatenate([zero, zero], axis=1)
    cs, us, ds = [], [], []
    for l0 in range(0, width, HEAD_DIM):
        rot = (l0 % 128) < rot_lanes
        cs.append(cos_h if rot else id_c)
        us.append(up_h if rot else id_s)
        ds.append(dn_h if rot else id_s)
    return jnp.concatenate(cs, axis=1), jnp.concatenate(us, axis=1), jnp.concatenate(ds, axis=1)


def _iota(shape, dim):
    return lax.broadcasted_iota(jnp.int32, shape, dim)


def _cparams(sem):
    return pltpu.CompilerParams(dimension_semantics=sem, vmem_limit_bytes=VMEM_LIMIT)


def _ln_body(x_ref, g_ref, b_ref, o_ref):
    x = x_ref[...]
    mu = jnp.mean(x, axis=-1, keepdims=True)
    xc = x - mu
    var = jnp.mean(xc * xc, axis=-1, keepdims=True)
    o_ref[...] = xc * lax.rsqrt(var + LN_EPS) * g_ref[...] + b_ref[...]


def _layer_norm(x2, g, b, tm=512):
    m, d = x2.shape
    return pl.pallas_call(
        _ln_body,
        grid=(m // tm,),
        in_specs=[pl.BlockSpec((tm, d), lambda i: (i, 0)),
                  pl.BlockSpec((1, d), lambda i: (0, 0)),
                  pl.BlockSpec((1, d), lambda i: (0, 0))],
        out_specs=pl.BlockSpec((tm, d), lambda i: (i, 0)),
        out_shape=jax.ShapeDtypeStruct((m, d), F32),
        compiler_params=_cparams(("parallel",)),
        name="layer_norm",
    )(x2, g.reshape(1, d), b.reshape(1, d))


def _proj_body(x_ref, w_ref, b_ref, o_ref):
    o_ref[...] = jnp.dot(x_ref[...].astype(BF16), w_ref[...], preferred_element_type=F32) + b_ref[...]


def _in_proj(h2, w_bf, b, tm=256):
    m, d = h2.shape
    n = w_bf.shape[1]
    return pl.pallas_call(
        _proj_body,
        grid=(m // tm,),
        in_specs=[pl.BlockSpec((tm, d), lambda i: (i, 0)),
                  pl.BlockSpec((d, n), lambda i: (0, 0)),
                  pl.BlockSpec((1, n), lambda i: (0, 0))],
        out_specs=pl.BlockSpec((tm, n), lambda i: (i, 0)),
        out_shape=jax.ShapeDtypeStruct((m, n), F32),
        compiler_params=_cparams(("parallel",)),
        name="in_proj",
    )(h2, w_bf, b.reshape(1, n))


def _ln_rows(x, g, b):
    mu = jnp.mean(x, axis=-1, keepdims=True)
    xc = x - mu
    var = jnp.mean(xc * xc, axis=-1, keepdims=True)
    return xc * lax.rsqrt(var + LN_EPS) * g + b


def _mix_out(ya_ref, yb_ref, yc_ref, yd_ref, h_ref, w_ref, g_ref, b_ref):
    y = _mm(ya_ref[...], w_ref[0:GROUP_W, :])
    y += _mm(yb_ref[...], w_ref[GROUP_W:2 * GROUP_W, :])
    y += _mm(yc_ref[...], w_ref[2 * GROUP_W:3 * GROUP_W, :])
    y += _mm(yd_ref[...], w_ref[3 * GROUP_W:4 * GROUP_W, :])
    return _ln_rows(DN_ALPHA * h_ref[...] + y, g_ref[...], b_ref[...])


def _ln_proj_body(x_ref, g_ref, b_ref, w_ref, bias_ref, h_ref, o_ref):
    h = _ln_rows(x_ref[...], g_ref[...], b_ref[...])
    h_ref[...] = h
    o_ref[...] = jnp.dot(h.astype(BF16), w_ref[...], preferred_element_type=F32) + bias_ref[...]


def _out_proj_body(ya_ref, yb_ref, yc_ref, yd_ref, hp_ref, wo_ref, g_ref, b_ref, w_ref, bias_ref, h_ref, o_ref):
    h = _mix_out(ya_ref, yb_ref, yc_ref, yd_ref, hp_ref, wo_ref, g_ref, b_ref)
    h_ref[...] = h
    o_ref[...] = jnp.dot(h.astype(BF16), w_ref[...], preferred_element_type=F32) + bias_ref[...]


def _ln_in_proj(x2, g, b, w_bf, bias, tm=256):
    m, d = x2.shape
    n = w_bf.shape[1]
    row = lambda width: pl.BlockSpec((tm, width), lambda i: (i, 0))
    const = lambda shape: pl.BlockSpec(shape, lambda i: (0, 0))
    return pl.pallas_call(
        _ln_proj_body,
        grid=(m // tm,),
        in_specs=[row(d), const((1, d)), const((1, d)), const((d, n)), const((1, n))],
        out_specs=[row(d), row(n)],
        out_shape=[jax.ShapeDtypeStruct((m, d), F32), jax.ShapeDtypeStruct((m, n), F32)],
        compiler_params=_cparams(("parallel",)),
        name="ln_in_proj",
    )(x2, g.reshape(1, d), b.reshape(1, d), w_bf, bias.reshape(1, n))


def _out_in_proj(ys, h2, wo_bf, g, b, w_bf, bias, tm=256):
    m, d = h2.shape
    n = w_bf.shape[1]
    row = lambda width: pl.BlockSpec((tm, width), lambda i: (i, 0))
    const = lambda shape: pl.BlockSpec(shape, lambda i: (0, 0))
    return pl.pallas_call(
        _out_proj_body,
        grid=(m // tm,),
        in_specs=[row(GROUP_W)] * N_GROUPS + [row(d), const((N_GROUPS * GROUP_W, d)), const((1, d)), const((1, d)),
                                              const((d, n)), const((1, n))],
        out_specs=[row(d), row(n)],
        out_shape=[jax.ShapeDtypeStruct((m, d), F32), jax.ShapeDtypeStruct((m, n), F32)],
        compiler_params=_cparams(("parallel",)),
        name="out_in_proj",
    )(*[y.reshape(m, GROUP_W) for y in ys], h2, wo_bf, g.reshape(1, d), b.reshape(1, d), w_bf, bias.reshape(1, n))


def _out_body(ya_ref, yb_ref, yc_ref, yd_ref, h_ref, w_ref, g_ref, b_ref, o_ref):
    o_ref[...] = _mix_out(ya_ref, yb_ref, yc_ref, yd_ref, h_ref, w_ref, g_ref, b_ref)


def _out_proj(ys, h2, w_bf, g, b, tm=512):
    m, d = h2.shape
    yspec = pl.BlockSpec((tm, GROUP_W), lambda i: (i, 0))
    return pl.pallas_call(
        _out_body,
        grid=(m // tm,),
        in_specs=[yspec, yspec, yspec, yspec,
                  pl.BlockSpec((tm, d), lambda i: (i, 0)),
                  pl.BlockSpec((4 * GROUP_W, d), lambda i: (0, 0)),
                  pl.BlockSpec((1, d), lambda i: (0, 0)),
                  pl.BlockSpec((1, d), lambda i: (0, 0))],
        out_specs=pl.BlockSpec((tm, d), lambda i: (i, 0)),
        out_shape=jax.ShapeDtypeStruct((m, d), F32),
        compiler_params=_cparams(("parallel",)),
        name="out_proj",
    )(*[y.reshape(m, GROUP_W) for y in ys], h2, w_bf, g.reshape(1, d), b.reshape(1, d))


def _mixer_a_body(sink_ref, q_ref, kv_ref, z_ref, cos_ref, up_ref, dn_ref, o_ref, k_ref, vt_ref):
    i = pl.program_id(1)
    seq = kv_ref.shape[1]
    n_kv = N_HEADS // A_GROUP
    low_half = _iota((1, LANES), 1) < HEAD_DIM

    @pl.when(i == 0)
    def _():
        def rope_rows(c, carry):
            for u in range(PRO_ROWS // QBLK):
                t0 = c * (PRO_ROWS // QBLK) + u
                rows = pl.ds(pl.multiple_of(t0 * QBLK, QBLK), QBLK)
                kr = _rope(kv_ref[0, rows, 0:128], cos_ref[rows, 0:128], up_ref[rows, 0:128], dn_ref[rows, 0:128])
                v_t = kv_ref[0, rows, 128:256].T.astype(BF16)
                ones_rows = jnp.ones((V_AUG - HEAD_DIM, QBLK), BF16)
                other = pltpu.roll(kr, HEAD_DIM, axis=1)
                for g in range(n_kv):
                    dup = jnp.where(low_half, kr, other) if g == 0 else jnp.where(low_half, other, kr)
                    k_ref[g, rows, :] = dup.astype(BF16)
                    vt_ref[g, t0] = jnp.concatenate([v_t[g * HEAD_DIM:(g + 1) * HEAD_DIM, :], ones_rows], axis=0)
            return carry
        lax.fori_loop(0, seq // PRO_ROWS, rope_rows, 0)

    n_blk = A_WINDOW // QBLK + 1
    span = n_blk * QBLK
    first_head = _iota((1, A_GROUP * QBLK), 1) < QBLK
    eye = (_iota((QBLK, QBLK), 0) == _iota((QBLK, QBLK), 1)).astype(BF16)

    def query_block(u):
        blk_i = i * A_SUB + u
        sub = slice(u * QBLK, (u + 1) * QBLK)
        q0 = pl.multiple_of(blk_i * QBLK, QBLK)
        qrows = pl.ds(q0, QBLK)
        qr = _rope(q_ref[0, sub, :], cos_ref[qrows, :], up_ref[qrows, :], dn_ref[qrows, :]) * (SCALE * LOG2E)
        kt0 = jnp.maximum(blk_i - A_WINDOW // QBLK, 0)
        k0 = pl.multiple_of(kt0 * QBLK, QBLK)
        rel = (q0 + _iota((1, QBLK), 1)) - (k0 + _iota((span, 1), 0))
        in_band = lax.bitcast_convert_type(rel, jnp.uint32) < A_WINDOW
        bias = jnp.concatenate([jnp.where(in_band, 0.0, NEG)] * A_GROUP, axis=1)
        outs = []
        for g in range(n_kv):
            q_pair = qr[:, g * LANES:(g + 1) * LANES]
            qg = jnp.concatenate([jnp.where(low_half, q_pair, 0.0), jnp.where(low_half, 0.0, q_pair)], axis=0)
            sink = jnp.where(first_head, sink_ref[g * A_GROUP], sink_ref[g * A_GROUP + 1]) * LOG2E
            s = _mm_nt(k_ref[g, pl.ds(k0, span), :], qg) + bias
            m = jnp.maximum(jnp.max(s, axis=0, keepdims=True), sink)
            p = jnp.exp2(s - m).astype(BF16)
            o = jnp.zeros((V_AUG, A_GROUP * QBLK), F32)
            for c in range(n_blk):
                o = o + jnp.dot(vt_ref[g, kt0 + c], p[c * QBLK:(c + 1) * QBLK], preferred_element_type=F32)
            o = o[0:HEAD_DIM] / (o[HEAD_DIM:HEAD_DIM + 1] + jnp.exp2(sink - m))
            outs.extend(o[:, r * QBLK:(r + 1) * QBLK] for r in range(A_GROUP))
        y = _mm_nt(eye, jnp.concatenate(outs, axis=0))
        o_ref[0, sub, :] = (y * _silu(z_ref[0, sub, :])).astype(Y_DTYPE)

    for u in range(A_SUB):
        query_block(u)


def _mixer_a(proj, sinks, tabs):
    bsz, seq, _ = proj.shape
    cos, up, dn = tabs
    blk = lambda name: _new_offset(name) // GROUP_W
    tspec = pl.BlockSpec((seq, GROUP_W), lambda b, i: (0, 0))
    return pl.pallas_call(
        _mixer_a_body,
        grid=(bsz, seq // (A_SUB * QBLK)),
        in_specs=[pl.BlockSpec(memory_space=pltpu.SMEM),
                  pl.BlockSpec((1, A_SUB * QBLK, GROUP_W), lambda b, i: (b, i, blk('a_q'))),
                  pl.BlockSpec((1, seq, GROUP_W), lambda b, i: (b, 0, blk('a_k'))),
                  pl.BlockSpec((1, A_SUB * QBLK, GROUP_W), lambda b, i: (b, i, blk('a_z'))),
                  tspec, tspec, tspec],
        out_specs=pl.BlockSpec((1, A_SUB * QBLK, GROUP_W), lambda b, i: (b, i, 0)),
        out_shape=jax.ShapeDtypeStruct((bsz, seq, GROUP_W), Y_DTYPE),
        scratch_shapes=[pltpu.VMEM((N_HEADS // A_GROUP, seq, LANES), BF16),
                        pltpu.VMEM((N_HEADS // A_GROUP, seq // QBLK, V_AUG, QBLK), BF16)],
        compiler_params=_cparams(("arbitrary", "arbitrary")),
        name="mixer_a",
    )(sinks, proj, proj, proj, cos, up, dn)


B_CHUNK = 128
B_STEP = 512
B_AUG = HEAD_DIM + 16


def _conv_silu(x, prev, w, b):
    row8 = _iota((8, x.shape[1]), 0)
    acc = b + w[B_CONV - 1:B_CONV, :] * x
    for s in range(1, B_CONV):
        xs = pltpu.roll(x, s, axis=0)
        ps = pltpu.roll(prev, s, axis=0)
        shifted = jnp.concatenate([jnp.where(row8 < s, ps, xs[0:8]), xs[8:]], axis=0)
        acc = acc + w[B_CONV - 1 - s:B_CONV - s, :] * shifted
    return _silu(acc)


def _mixer_b_body(xq_ref, xk_ref, v_ref, og_ref, z_ref, if_ref, cw_ref, cb_ref, o_ref, xprev_ref, c_ref, m_ref):
    c = pl.program_id(1)
    L = B_CHUNK

    @pl.when(c == 0)
    def _():
        xprev_ref[...] = jnp.zeros_like(xprev_ref)
        c_ref[...] = jnp.zeros_like(c_ref)
        m_ref[...] = jnp.zeros_like(m_ref)

    tri = (_iota((L, L), 1) <= _iota((L, L), 0)).astype(F32)
    key_first = _iota((L, L), 0) <= _iota((L, L), 1)
    half_of_lane = _iota((1, 128), 1) // HEAD_DIM
    ones_rows = jnp.ones((B_AUG - HEAD_DIM, L), F32)

    def chunk(rows, prev_q, prev_k, state):
        xq = xq_ref[0, rows, :]
        xk = xk_ref[0, rows, :]
        qc = _conv_silu(xq, prev_q, cw_ref[:, 0:GROUP_W], cb_ref[:, 0:GROUP_W]) * SCALE
        kc = _conv_silu(xk, prev_k, cw_ref[:, GROUP_W:], cb_ref[:, GROUP_W:])
        q_t = qc.T
        v_t = v_ref[0, rows, :].T

        gates = if_ref[0, rows, :]
        bcum = _mm_sel(tri, _log_sigmoid(gates))
        gates_t = gates.T
        bcum_t = bcum.T
        outs, new_state = [], []
        for h in range(N_HEADS):
            pair = slice((h // 2) * 128, (h // 2 + 1) * 128)
            own = half_of_lane == (h % 2)
            k_pair = kc[:, pair]
            q_pair_t = q_t[pair, :]
            i_row = gates_t[h:h + 1, :]
            b_row = bcum_t[N_HEADS + h:N_HEADS + h + 1, :]
            c_col = gates[:, h:h + 1] - bcum[:, N_HEADS + h:N_HEADS + h + 1]
            cst, m_prev = state[h]
            dmat = jnp.where(key_first, b_row + c_col, NEG)
            inter = b_row + m_prev
            m_t = jnp.maximum(jnp.max(dmat, axis=0, keepdims=True), inter)
            smat = _mm(jnp.where(own, k_pair, 0.0), q_pair_t) * jnp.exp(dmat - m_t)
            vaug_t = jnp.concatenate([v_t[h * HEAD_DIM:(h + 1) * HEAD_DIM, :], ones_rows], axis=0)
            tot = _mm(vaug_t, smat) + jnp.exp(inter - m_t) * _mm(cst, q_pair_t)
            num = tot[0:HEAD_DIM, :]
            den = tot[HEAD_DIM:HEAD_DIM + 1, :]
            outs.append(num / jnp.maximum(jnp.abs(den), jnp.exp(-m_t)))
            b_last = b_row[:, L - 1:L]
            d_row = b_last - b_row + i_row
            m_new = jnp.maximum(b_last + m_prev, jnp.max(d_row, axis=1, keepdims=True))
            w_row = jnp.exp(d_row - m_new)
            decay = jnp.exp(b_last + m_prev - m_new)
            new_state.append((decay * cst + jnp.where(own, _mm(vaug_t * w_row, k_pair), 0.0), m_new))
        o_ref[0, rows, :] = (_sigmoid(og_ref[0, rows, :]) * jnp.concatenate(outs, axis=0).T
                             * _silu(z_ref[0, rows, :])).astype(Y_DTYPE)
        return xq[L - 8:L, :], xk[L - 8:L, :], new_state

    prev_q = xprev_ref[:, 0:GROUP_W]
    prev_k = xprev_ref[:, GROUP_W:]
    state = [(c_ref[h], m_ref[h:h + 1, 0:1]) for h in range(N_HEADS)]
    for u in range(xq_ref.shape[1] // L):
        prev_q, prev_k, state = chunk(slice(u * L, (u + 1) * L), prev_q, prev_k, state)
    xprev_ref[:, 0:GROUP_W] = prev_q
    xprev_ref[:, GROUP_W:] = prev_k
    for h in range(N_HEADS):
        c_ref[h] = state[h][0]
        m_ref[h:h + 1, :] = jnp.broadcast_to(state[h][1], (1, 128))


def _mixer_b(proj, conv_w, conv_b):
    bsz, seq, _ = proj.shape
    L = B_STEP
    blk = lambda name: _new_offset(name) // GROUP_W
    return pl.pallas_call(
        _mixer_b_body,
        grid=(bsz, seq // L),
        in_specs=[pl.BlockSpec((1, L, GROUP_W), lambda b, c: (b, c, blk('b_q'))),
                  pl.BlockSpec((1, L, GROUP_W), lambda b, c: (b, c, blk('b_k'))),
                  pl.BlockSpec((1, L, GROUP_W), lambda b, c: (b, c, blk('b_v'))),
                  pl.BlockSpec((1, L, GROUP_W), lambda b, c: (b, c, blk('b_o'))),
                  pl.BlockSpec((1, L, GROUP_W), lambda b, c: (b, c, blk('b_z'))),
                  pl.BlockSpec((1, L, LANES), lambda b, c: (b, c, _new_offset('gates') // LANES)),
                  pl.BlockSpec((B_CONV, 2 * GROUP_W), lambda b, c: (0, 0)),
                  pl.BlockSpec((1, 2 * GROUP_W), lambda b, c: (0, 0))],
        out_specs=pl.BlockSpec((1, L, GROUP_W), lambda b, c: (b, c, 0)),
        out_shape=jax.ShapeDtypeStruct((bsz, seq, GROUP_W), Y_DTYPE),
        scratch_shapes=[pltpu.VMEM((8, 2 * GROUP_W), F32),
                        pltpu.VMEM((N_HEADS, B_AUG, 128), F32),
                        pltpu.VMEM((8, 128), F32)],
        compiler_params=_cparams(("arbitrary", "arbitrary")),
        name="mixer_b",
    )(proj, proj, proj, proj, proj, proj, conv_w, conv_b.reshape(1, -1))


def _mixer_c_body(layer, q_ref, f_ref, i_ref, z_ref, lb_ref, gn_ref, tri_ref, piv_ref, o_ref, st_ref):
    c = pl.program_id(1)
    L = C_CHUNK
    nsub = L // C_SUB

    @pl.when(c == 0)
    def _():
        st_ref[...] = jnp.zeros_like(st_ref)

    raw = lb_ref[...]
    ex = jnp.exp(raw - jnp.max(raw, axis=0, keepdims=True))
    if layer == 0:
        lb = jnp.zeros((1, GROUP_W), F32)
    else:
        lb = jnp.sum(ex[1:layer + 1], axis=0, keepdims=True) / jnp.sum(ex, axis=0, keepdims=True)
    tri = (_iota((L, L), 1) <= _iota((L, L), 0)).astype(F32)
    head_of_lane = _iota((1, GROUP_W), 1) // HEAD_DIM
    same_head = ((_iota((GROUP_W, GROUP_W), 0) // HEAD_DIM)
                 == (_iota((GROUP_W, GROUP_W), 1) // HEAD_DIM))
    ones_bd = same_head.astype(BF16)

    def chunk(rows, st):
        f = lb + (1.0 - lb) * _sigmoid(f_ref[0, rows, :])
        logf = jnp.log(f)
        k = 1.0 - f
        q = _silu(q_ref[0, rows, :])
        v = i_ref[0, rows, :]
        a = _mm_sel(tri, logf) * LOG2E
        o = _mm_nt(q * jnp.exp2(a), st)

        qsts, kts, vreps = [], [], []
        for i in range(1, nsub):
            r0 = i * C_SUB
            piv = a[r0 - 1:r0, :]
            qt = q[r0:r0 + C_SUB] * jnp.exp2(a[r0:r0 + C_SUB] - piv)
            qsts.extend(jnp.where(head_of_lane == h, qt, 0.0) for h in range(N_HEADS))
            kts.append((k[0:r0] * jnp.exp2(piv - a[0:r0])).astype(BF16))
            vreps.append(v[0:r0].astype(BF16))
        scores = _mm_nt(jnp.concatenate(qsts, axis=0), jnp.concatenate(kts, axis=0))
        r = _dot_bf16((scores * piv_ref[...]).astype(BF16), jnp.concatenate(vreps, axis=0))
        pieces = [o[0:C_SUB]]
        for i in range(1, nsub):
            acc = o[i * C_SUB:(i + 1) * C_SUB]
            for h in range(N_HEADS):
                g0 = ((i - 1) * N_HEADS + h) * C_SUB
                acc = acc + jnp.where(head_of_lane == h, r[g0:g0 + C_SUB], 0.0)
            pieces.append(acc)
        o = jnp.concatenate(pieces, axis=0)

        a3 = a.reshape(nsub, C_SUB, GROUP_W)
        q3 = q.reshape(nsub, C_SUB, GROUP_W)
        k3 = k.reshape(nsub, C_SUB, GROUP_W)
        v3 = v.reshape(nsub, C_SUB, GROUP_W)
        ps = []
        for s in range(C_SUB):
            e = jnp.exp2(a3 - a3[:, s:s + 1, :] + tri_ref[s])
            ps.append((q3 * k3[:, s:s + 1, :] * e).reshape(L, GROUP_W).astype(BF16))
        w = jnp.dot(jnp.concatenate(ps, axis=0), ones_bd, preferred_element_type=F32)
        for s in range(C_SUB):
            vs = jnp.broadcast_to(v3[:, s:s + 1, :], (nsub, C_SUB, GROUP_W)).reshape(L, GROUP_W)
            o = o + w[s * L:(s + 1) * L] * vs

        a_last = a[L - 1:L, :]
        kdec = k * jnp.exp2(a_last - a)
        st_new = st * jnp.exp2(a_last) + jnp.where(same_head, _mm_tn(v, kdec), 0.0)

        ms = jnp.dot((o * o).astype(BF16), ones_bd, preferred_element_type=F32) * (1.0 / HEAD_DIM)
        o = o * lax.rsqrt(ms + C_EPS) * gn_ref[...]
        o_ref[0, rows, :] = (o * _silu(z_ref[0, rows, :])).astype(Y_DTYPE)
        return st_new

    st = st_ref[...]
    for u in range(q_ref.shape[1] // L):
        st = chunk(slice(u * L, (u + 1) * L), st)
    st_ref[...] = st


def _mixer_c(proj, c_lb, norm_g, layer):
    bsz, seq, _ = proj.shape
    L = C_STEP
    blk = lambda name: _new_offset(name) // GROUP_W
    spec = lambda name: pl.BlockSpec((1, L, GROUP_W), lambda b, c: (b, c, blk(name)))
    vec = pl.BlockSpec((1, GROUP_W), lambda b, c: (0, 0))
    t_ge_s = np.arange(C_SUB)[None, :, None] >= np.arange(C_SUB)[:, None, None]
    causal_bias = np.broadcast_to(np.where(t_ge_s, 0.0, NEG), (C_SUB, C_SUB, GROUP_W)).astype(np.float32)
    pivots = np.arange(1, C_CHUNK // C_SUB)
    row_piv = np.repeat(pivots, N_HEADS * C_SUB)
    col_piv = np.repeat(pivots, pivots * C_SUB)
    same_pivot = (row_piv[:, None] == col_piv[None, :]).astype(np.float32)
    return pl.pallas_call(
        functools.partial(_mixer_c_body, layer),
        grid=(bsz, seq // L),
        in_specs=[spec('c_q'), spec('c_f'), spec('c_i'), spec('c_z'),
                  pl.BlockSpec((DEPTH, GROUP_W), lambda b, c: (0, 0)), vec,
                  pl.BlockSpec((C_SUB, C_SUB, GROUP_W), lambda b, c: (0, 0, 0)),
                  pl.BlockSpec(same_pivot.shape, lambda b, c: (0, 0))],
        out_specs=pl.BlockSpec((1, L, GROUP_W), lambda b, c: (b, c, 0)),
        out_shape=jax.ShapeDtypeStruct((bsz, seq, GROUP_W), Y_DTYPE),
        scratch_shapes=[pltpu.VMEM((GROUP_W, GROUP_W), F32)],
        compiler_params=_cparams(("arbitrary", "arbitrary")),
        name="mixer_c",
    )(proj, proj, proj, proj, c_lb, norm_g.reshape(1, -1), jnp.asarray(causal_bias), jnp.asarray(same_pivot))


N_CMP_PAD = 128
KT = 512
D_QBLK = 256
WIN_HEADS = N_HEADS


def _mixer_d_body(q_ref, z_ref, ksv_ref, kwv_ref, cin_ref, g_ref, cos_ref, up_ref, dn_ref, kcos_ref, kup_ref,
                  kdn_ref, ccos_ref, cup_ref, cdn_ref, w1_ref, pe_ref, w2_ref, ov_ref, o_ref,
                  k_ref, vts_ref, vtw_ref, kc_ref, vct_ref, sel_ref):
    QBLK = D_QBLK
    i = pl.program_id(1)
    seq = ksv_ref.shape[1]
    half_cmp = CMP_LEN // 2

    @pl.when(i == 0)
    def _():
        def rope_rows(c, carry):
            for u in range(PRO_ROWS // KT):
                t0 = c * (PRO_ROWS // KT) + u
                rows = pl.ds(pl.multiple_of(t0 * KT, KT), KT)
                kvs = _rope(ksv_ref[0, rows, :], kcos_ref[rows, :], kup_ref[rows, :], kdn_ref[rows, :])
                kvw = _rope(kwv_ref[0, rows, :], kcos_ref[rows, :], kup_ref[rows, :], kdn_ref[rows, :])
                k_ref[0, rows, :] = kvs[:, 0:HEAD_DIM].astype(BF16)
                k_ref[1, rows, :] = kvw[:, 0:HEAD_DIM].astype(BF16)
                ones_rows = jnp.ones((V_AUG - HEAD_DIM, KT), BF16)
                vts_ref[t0] = jnp.concatenate([kvs.T[HEAD_DIM:2 * HEAD_DIM, :].astype(BF16), ones_rows], axis=0)
                vwt = jnp.concatenate([kvw.T[HEAD_DIM:2 * HEAD_DIM, :].astype(BF16), ones_rows], axis=0)
                for w in range(KT // QBLK):
                    vtw_ref[t0 * (KT // QBLK) + w] = vwt[:, w * QBLK:(w + 1) * QBLK]
            return carry
        lax.fori_loop(0, seq // PRO_ROWS, rope_rows, 0)
        u0 = jnp.zeros((N_CMP_PAD, 128), F32)
        u1 = jnp.zeros((N_CMP_PAD, 128), F32)
        for r in range(half_cmp):
            xr = cin_ref[0, pl.ds(r, N_CMP_PAD, stride=CMP_STRIDE), :]
            u0 = u0 + _mm_hi(xr + pe_ref[r:r + 1, :], w1_ref[r])
            u1 = u1 + _mm_hi(xr + pe_ref[half_cmp + r:half_cmp + r + 1, :], w1_ref[half_cmp + r])
        pre = u0 + pltpu.roll(u1, N_CMP_PAD - 1, axis=0)
        cv = _rope(_mm_hi(_silu(pre), w2_ref[...]), ccos_ref[...], cup_ref[...], cdn_ref[...])
        kc_ref[...] = cv[:, 0:HEAD_DIM]
        vct_ref[...] = cv.T[HEAD_DIM:2 * HEAD_DIM, :].astype(BF16)

    q0 = pl.multiple_of(i * QBLK, QBLK)
    qrows = pl.ds(q0, QBLK)
    qr = _rope(q_ref[0], cos_ref[qrows, :], up_ref[qrows, :], dn_ref[qrows, :]) * (SCALE * LOG2E)
    qt = qr.T
    qst = jnp.concatenate([qt[h * HEAD_DIM:(h + 1) * HEAD_DIM] for h in range(N_HEADS)], axis=1)
    qst_bf = qst.astype(BF16)
    cols4 = N_HEADS * QBLK
    qpos = q0 + _iota((1, QBLK), 1)

    def lanes4(x):
        return jnp.concatenate([x] * N_HEADS, axis=1)

    n_win = NSA_WINDOW // QBLK + 1
    span = n_win * QBLK
    kt0 = jnp.maximum(i - NSA_WINDOW // QBLK, 0)
    k0 = pl.multiple_of(kt0 * QBLK, QBLK)
    rel = qpos - (k0 + _iota((span, 1), 0))
    in_band = lax.bitcast_convert_type(rel, jnp.uint32) < NSA_WINDOW
    win_bias = jnp.where(in_band, 0.0, NEG)
    kw = k_ref[1, pl.ds(k0, span), :]
    o_wins = []
    for h in range(0, N_HEADS, WIN_HEADS):
        cs = slice(h * QBLK, (h + WIN_HEADS) * QBLK)
        s = jnp.dot(kw, qst_bf[:, cs], preferred_element_type=F32) + jnp.concatenate([win_bias] * WIN_HEADS, axis=1)
        p = jnp.exp2(s - jnp.max(s, axis=0, keepdims=True)).astype(BF16)
        acc = jnp.zeros((V_AUG, WIN_HEADS * QBLK), F32)
        for c in range(n_win):
            acc = acc + jnp.dot(vtw_ref[kt0 + c], p[c * QBLK:(c + 1) * QBLK], preferred_element_type=F32)
        o_wins.append(acc[0:HEAD_DIM] / acc[HEAD_DIM:HEAD_DIM + 1])
    o_win = jnp.concatenate(o_wins, axis=1)

    sc = _mm_hi(kc_ref[...], qst)
    ends = _iota((N_CMP_PAD, 1), 0) * CMP_STRIDE + (CMP_LEN - 1)
    scm = sc + lanes4(jnp.where(ends <= qpos, 0.0, NEG))
    e = jnp.exp2(scm - jnp.max(scm, axis=0, keepdims=True))
    l = jnp.sum(e, axis=0, keepdims=True)
    any_valid = lanes4(jnp.where(qpos >= CMP_LEN - 1, 1.0, 0.0))
    pc = e * (any_valid / l)
    o_cmp = jnp.dot(vct_ref[...], pc.astype(BF16), preferred_element_type=F32)
    psum = pc[:, 0:QBLK] + pc[:, QBLK:2 * QBLK] + pc[:, 2 * QBLK:3 * QBLK] + pc[:, 3 * QBLK:4 * QBLK]

    n_slc = seq // SLC_LEN
    imp = _mm_sel(ov_ref[...], psum)
    jl = _iota((n_slc, 1), 0)
    cur = jnp.right_shift(qpos, SLC_LEN.bit_length() - 1)
    imp = jnp.where(jl == 0, FORCE_SCORE, imp)
    imp = jnp.where(jl == cur, FORCE_SCORE, imp)
    imp = jnp.where(jl == cur - 1, FORCE_SCORE, imp)
    imp = jnp.where(jl <= cur, imp, NEG)
    rank = jnp.zeros((n_slc, QBLK), F32)
    for jp in range(n_slc):
        cj = imp[jp:jp + 1, :]
        tie = jnp.where(jl > jp, 1.0, 0.0)
        rank = rank + jnp.where(cj > imp, 1.0, jnp.where(cj == imp, tie, 0.0))
    sel_bias = jnp.where(rank < N_SEL, 0.0, NEG)
    blocks_per_tile = KT // SLC_LEN
    for t in range(n_slc // blocks_per_tile):
        sel_ref[t] = sel_bias[t * blocks_per_tile:(t + 1) * blocks_per_tile, :]

    def sel_tile(kt, carry):
        m_old, acc = carry
        kbase = pl.multiple_of(kt * KT, KT)
        picks = sel_ref[kt]
        chosen = jnp.concatenate([jnp.broadcast_to(picks[j:j + 1, :], (SLC_LEN, QBLK))
                                  for j in range(blocks_per_tile)], axis=0)
        bias = jnp.where((kbase + _iota((KT, 1), 0)) <= qpos, chosen, NEG)
        s = jnp.dot(k_ref[0, pl.ds(kbase, KT), :], qst_bf, preferred_element_type=F32) + lanes4(bias)
        m_new = jnp.maximum(m_old, jnp.max(s, axis=0, keepdims=True))
        p = jnp.exp2(s - m_new)
        acc_new = jnp.exp2(m_old - m_new) * acc + jnp.dot(vts_ref[kt], p.astype(BF16), preferred_element_type=F32)
        return m_new, acc_new

    init = (jnp.full((1, cols4), NEG, F32), jnp.zeros((V_AUG, cols4), F32))
    n_tiles = (i + KT // QBLK) // (KT // QBLK)
    _, acc_s = lax.fori_loop(0, n_tiles, sel_tile, init)
    o_slc = acc_s[0:HEAD_DIM] / acc_s[HEAD_DIM:HEAD_DIM + 1]

    gate = _sigmoid(g_ref[0]).T
    outs = []
    for h in range(N_HEADS):
        cs = slice(h * QBLK, (h + 1) * QBLK)
        g0 = D_GATE_LANE + 3 * h
        outs.append(gate[g0:g0 + 1, :] * o_cmp[:, cs] + gate[g0 + 1:g0 + 2, :] * o_slc[:, cs]
                    + gate[g0 + 2:g0 + 3, :] * o_win[:, cs])
    o_ref[0] = (jnp.concatenate(outs, axis=0).T * _silu(z_ref[0])).astype(Y_DTYPE)


def _mixer_d(proj, qtabs, ktabs, ctabs, w1bd, pe2, w2bd, overlap):
    QBLK = D_QBLK
    bsz, seq, _ = proj.shape
    blk = lambda name: _new_offset(name) // GROUP_W
    full = lambda shape: pl.BlockSpec(shape, lambda b, i: (0,) * len(shape))
    return pl.pallas_call(
        _mixer_d_body,
        grid=(bsz, seq // QBLK),
        in_specs=[pl.BlockSpec((1, QBLK, GROUP_W), lambda b, i: (b, i, blk('d_q'))),
                  pl.BlockSpec((1, QBLK, GROUP_W), lambda b, i: (b, i, blk('d_z'))),
                  pl.BlockSpec((1, seq, 128), lambda b, i: (b, 0, _new_offset('d_ksv') // 128)),
                  pl.BlockSpec((1, seq, 128), lambda b, i: (b, 0, _new_offset('d_kwv') // 128)),
                  pl.BlockSpec((1, seq, 128), lambda b, i: (b, 0, _new_offset('d_kvc') // 128)),
                  pl.BlockSpec((1, QBLK, LANES), lambda b, i: (b, i, _new_offset('gates') // LANES)),
                  full((seq, GROUP_W)), full((seq, GROUP_W)), full((seq, GROUP_W)),
                  full((seq, 128)), full((seq, 128)), full((seq, 128)),
                  full((N_CMP_PAD, 128)), full((N_CMP_PAD, 128)), full((N_CMP_PAD, 128)),
                  full((CMP_LEN, 128, 128)), full((CMP_LEN, 128)), full((128, 128)),
                  full((seq // SLC_LEN, N_CMP_PAD))],
        out_specs=pl.BlockSpec((1, QBLK, GROUP_W), lambda b, i: (b, i, 0)),
        out_shape=jax.ShapeDtypeStruct((bsz, seq, GROUP_W), Y_DTYPE),
        scratch_shapes=[pltpu.VMEM((2, seq, HEAD_DIM), BF16),
                        pltpu.VMEM((seq // KT, V_AUG, KT), BF16),
                        pltpu.VMEM((seq // QBLK, V_AUG, QBLK), BF16),
                        pltpu.VMEM((N_CMP_PAD, HEAD_DIM), F32),
                        pltpu.VMEM((HEAD_DIM, N_CMP_PAD), BF16),
                        pltpu.VMEM((seq // KT, KT // SLC_LEN, QBLK), F32)],
        compiler_params=_cparams(("arbitrary", "arbitrary")),
        name="mixer_d",
    )(proj, proj, proj, proj, proj, proj, *qtabs, *ktabs, *ctabs, w1bd, pe2, w2bd, overlap)


def _cmp_params(pe, w1, w2):
    w1 = w1.reshape(2, CMP_LEN, HEAD_DIM, HEAD_DIM)
    z = jnp.zeros((CMP_LEN, HEAD_DIM, HEAD_DIM), F32)
    w1bd = jnp.concatenate([jnp.concatenate([w1[0], z], axis=2), jnp.concatenate([z, w1[1]], axis=2)], axis=1)
    z2 = jnp.zeros((HEAD_DIM, HEAD_DIM), F32)
    w2bd = jnp.concatenate([jnp.concatenate([w2[0], z2], axis=1), jnp.concatenate([z2, w2[1]], axis=1)], axis=0)
    pe2 = jnp.concatenate([pe[0], pe[1]], axis=1)
    return w1bd, pe2, w2bd


def _overlap_matrix(seq):
    n_cmp = (seq - CMP_LEN) // CMP_STRIDE + 1
    starts = np.arange(N_CMP_PAD) * CMP_STRIDE
    blk = np.arange(seq // SLC_LEN)
    ov = ((starts[None, :] < (blk[:, None] + 1) * SLC_LEN) & (starts[None, :] + CMP_LEN > blk[:, None] * SLC_LEN))
    ov = ov & (np.arange(N_CMP_PAD)[None, :] < n_cmp)
    return jnp.asarray(ov.astype(np.float32))


def kernel(x, ln0_g, ln0_b, w_in, b_in, a_sinks, b_conv_w, b_conv_b, c_lb, c_norm_g, d_cmp_pe, d_cmp_w1, d_cmp_w2,
           w_out, ln_g, ln_b):
    bsz, seq, d = x.shape
    pos = jnp.arange(seq)
    qtabs = _rope_tables(pos, GROUP_W, 128)
    ktabs = _rope_tables(pos, 128, HEAD_DIM)
    ctabs = _rope_tables(jnp.arange(N_CMP_PAD) * CMP_STRIDE + (CMP_LEN - 1), 128, HEAD_DIM)
    overlap = _overlap_matrix(seq)

    w_p, b_p = _prep_in_weights(w_in, b_in, 0)
    h, proj = _ln_in_proj(x.reshape(bsz * seq, d), ln0_g, ln0_b, w_p, b_p)
    for l in range(DEPTH):
        proj = proj.reshape(bsz, seq, N_PROJ)
        y_a = _mixer_a(proj, a_sinks[l], qtabs)
        y_b = _mixer_b(proj, b_conv_w[l], b_conv_b[l])
        y_c = _mixer_c(proj, c_lb, c_norm_g[l], l)
        w1bd, pe2, w2bd = _cmp_params(d_cmp_pe[l], d_cmp_w1[l], d_cmp_w2[l])
        y_d = _mixer_d(proj, qtabs, ktabs, ctabs, w1bd, pe2, w2bd, overlap)
        ys = (y_a, y_b, y_c, y_d)
        if l + 1 < DEPTH:
            w_p, b_p = _prep_in_weights(w_in, b_in, l + 1)
            h, proj = _out_in_proj(ys, h, w_out[l].astype(BF16), ln_g[l], ln_b[l], w_p, b_p)
        else:
            h = _out_proj(ys, h, w_out[l].astype(BF16), ln_g[l], ln_b[l])
    return h.reshape(bsz, seq, d)
```

```python
import functools

import numpy as np
import jax
import jax.numpy as jnp
from jax import lax
from jax.experimental import pallas as pl
from jax.experimental.pallas import tpu as pltpu

F32 = jnp.float32
BF16 = jnp.bfloat16
Y_DTYPE = BF16

D_MODEL = 1024
DEPTH = 2
HEAD_DIM = 64
HALF = HEAD_DIM // 2
N_HEADS = 4
GROUP_W = N_HEADS * HEAD_DIM
N_GROUPS = 4
ROPE_THETA = 10000.0
NEG = -1e30
LN_EPS = 1e-5
A_WINDOW = 128
A_GROUP = 2
A_SUB = 4
V_AUG = HEAD_DIM + 16
LOG2E = 1.4426950408889634
PRO_ROWS = 512
B_CONV = 4
C_CHUNK = 64
C_SUB = 8
C_STEP = 512
C_EPS = 1e-6
CMP_LEN = 32
CMP_STRIDE = 16
SLC_LEN = 64
N_SEL = 8
NSA_WINDOW = 512
FORCE_SCORE = 1e6
DN_ALPHA = (2.0 * DEPTH) ** 0.25
QBLK = 128
SCALE = HEAD_DIM ** -0.5

VMEM_LIMIT = 56 * 1024 * 1024
PROJ_TM = 512
PROJ_SUB = 256

LANES = 128

ORIG_SPLITS = (
    ('a_q', 256), ('a_k', 128), ('a_v', 128), ('a_z', 256),
    ('b_q', 256), ('b_k', 256), ('b_v', 256), ('b_if', 8), ('b_o', 256), ('b_z', 256),
    ('c_q', 256), ('c_f', 256), ('c_i', 256), ('c_z', 256),
    ('d_q', 256), ('d_kvc', 128), ('d_ksv', 128), ('d_kwv', 128), ('d_g', 12), ('d_z', 256),
)
N_COLS = sum(w for _, w in ORIG_SPLITS)
GATE_SLOTS = ('b_if', 'd_g')
NEW_LAYOUT = (
    ('a_q', 256), ('a_k', 128), ('a_v', 128), ('a_z', 256), ('b_q', 256), ('b_k', 256), ('b_v', 256),
    ('b_o', 256), ('b_z', 256), ('c_q', 256), ('c_f', 256), ('c_i', 256), ('c_z', 256),
    ('d_q', 256), ('d_kvc', 128), ('d_ksv', 128), ('d_kwv', 128), ('gates', 128), ('d_z', 256),
)
N_PROJ = sum(w for _, w in NEW_LAYOUT)
D_GATE_LANE = dict(ORIG_SPLITS)['b_if']


def _offset(layout, name):
    off = 0
    for n, w in layout:
        if n == name:
            return off
        off += w
    raise KeyError(name)


def _new_offset(name):
    return _offset(NEW_LAYOUT, name)


def _permuted_cols(load):
    sizes = dict(ORIG_SPLITS)
    for name, width in NEW_LAYOUT:
        new = _new_offset(name)
        if name != 'gates':
            old = _offset(ORIG_SPLITS, name)
            lo = old // LANES * LANES
            hi = min(-(-(old + width) // LANES) * LANES, N_COLS)
            yield new, load(lo, hi)[..., old - lo:old - lo + width]
            continue
        gates, lane0 = None, 0
        for slot in GATE_SLOTS:
            old = _offset(ORIG_SPLITS, slot)
            lo = old // LANES * LANES
            tile = load(lo, lo + LANES)
            if old - lo != lane0:
                tile = pltpu.roll(tile, (lane0 - (old - lo)) % LANES, axis=tile.ndim - 1)
            lane = _iota(tile.shape, tile.ndim - 1)
            part = jnp.where((lane >= lane0) & (lane < lane0 + sizes[slot]), tile, 0.0)
            gates = part if gates is None else gates + part
            lane0 += sizes[slot]
        yield new, gates


def _prep_body(w_ref, b_ref, wo_ref, bo_ref):
    for new, cols in _permuted_cols(lambda lo, hi: w_ref[0, :, lo:hi]):
        wo_ref[:, new:new + cols.shape[-1]] = cols.astype(BF16)
    for new, cols in _permuted_cols(lambda lo, hi: b_ref[0, :, lo:hi]):
        bo_ref[:, new:new + cols.shape[-1]] = cols


def _prep_in_weights(w_in, b_in, layer, rows=128):
    depth, d, n = w_in.shape
    return pl.pallas_call(
        _prep_body,
        grid=(d // rows,),
        in_specs=[pl.BlockSpec((1, rows, n), lambda r: (layer, r, 0)),
                  pl.BlockSpec((1, 1, n), lambda r: (layer, 0, 0))],
        out_specs=[pl.BlockSpec((rows, N_PROJ), lambda r: (r, 0)),
                   pl.BlockSpec((1, N_PROJ), lambda r: (0, 0))],
        out_shape=[jax.ShapeDtypeStruct((d, N_PROJ), BF16), jax.ShapeDtypeStruct((1, N_PROJ), F32)],
        compiler_params=_cparams(("arbitrary",)),
        name="prep_in_weights",
    )(w_in, b_in.reshape(depth, 1, n))


def _mm(a, b):
    return jnp.dot(a.astype(BF16), b.astype(BF16), preferred_element_type=F32)


def _mm_nt(a, b):
    return lax.dot_general(a.astype(BF16), b.astype(BF16), (((1,), (1,)), ((), ())),
                           preferred_element_type=F32)


def _mm_tn(a, b):
    return lax.dot_general(a.astype(BF16), b.astype(BF16), (((0,), (0,)), ((), ())),
                           preferred_element_type=F32)


def _dot_bf16(a, b):
    return jnp.dot(a, b, preferred_element_type=F32)


def _split_bf16(x, terms):
    out = []
    for _ in range(terms - 1):
        t = x.astype(BF16)
        out.append(t)
        x = x - t.astype(F32)
    out.append(x.astype(BF16))
    return out


def _mm_hi(a, b):
    ah, al = _split_bf16(a, 2)
    bh, bl = _split_bf16(b, 2)
    return _dot_bf16(ah, bh) + (_dot_bf16(ah, bl) + _dot_bf16(al, bh))


def _mm_sel(sel, x):
    sel = sel.astype(BF16)
    x1, x2, x3 = _split_bf16(x, 3)
    return _dot_bf16(sel, x1) + (_dot_bf16(sel, x2) + _dot_bf16(sel, x3))


def _sigmoid(x):
    return 1.0 / (1.0 + jnp.exp2(x * (-LOG2E)))


def _silu(x):
    return x * _sigmoid(x)


def _log_sigmoid(x):
    return jnp.minimum(x, 0.0) - jnp.log1p(jnp.exp(-jnp.abs(x)))


def _rope(x, cos, sin_up, sin_dn):
    w = x.shape[-1]
    up = pltpu.roll(x, w - HALF, axis=1)
    dn = pltpu.roll(x, HALF, axis=1)
    return x * cos + up * sin_up + dn * sin_dn


def _rope_tables(pos, width, rot_lanes):
    inv = ROPE_THETA ** (-jnp.arange(HALF, dtype=F32) / HALF)
    ang = pos.astype(F32)[:, None] * inv[None, :]
    cos, sin = jnp.cos(ang), jnp.sin(ang)
    zero, one = jnp.zeros_like(sin), jnp.ones_like(cos)
    cos_h = jnp.concatenate([cos, cos], axis=1)
    up_h = jnp.concatenate([-sin, zero], axis=1)
    dn_h = jnp.concatenate([zero, sin], axis=1)
    id_c = jnp.concatenate([one, one], axis=1)
    id_s = jnp.concatenate([zero, zero], axis=1)
    cs, us, ds = [], [], []
    for l0 in range(0, width, HEAD_DIM):
        rot = (l0 % 128) < rot_lanes
        cs.append(cos_h if rot else id_c)
        us.append(up_h if rot else id_s)
        ds.append(dn_h if rot else id_s)
    return jnp.concatenate(cs, axis=1), jnp.concatenate(us, axis=1), jnp.concatenate(ds, axis=1)


def _iota(shape, dim):
    return lax.broadcasted_iota(jnp.int32, shape, dim)


def _cparams(sem):
    return pltpu.CompilerParams(dimension_semantics=sem, vmem_limit_bytes=VMEM_LIMIT)


def _ln_body(x_ref, g_ref, b_ref, o_ref):
    x = x_ref[...]
    mu = jnp.mean(x, axis=-1, keepdims=True)
    xc = x - mu
    var = jnp.mean(xc * xc, axis=-1, keepdims=True)
    o_ref[...] = xc * lax.rsqrt(var + LN_EPS) * g_ref[...] + b_ref[...]


def _layer_norm(x2, g, b, tm=512):
    m, d = x2.shape
    return pl.pallas_call(
        _ln_body,
        grid=(m // tm,),
        in_specs=[pl.BlockSpec((tm, d), lambda i: (i, 0)),
                  pl.BlockSpec((1, d), lambda i: (0, 0)),
                  pl.BlockSpec((1, d), lambda i: (0, 0))],
        out_specs=pl.BlockSpec((tm, d), lambda i: (i, 0)),
        out_shape=jax.ShapeDtypeStruct((m, d), F32),
        compiler_params=_cparams(("parallel",)),
        name="layer_norm",
    )(x2, g.reshape(1, d), b.reshape(1, d))


def _proj_body(x_ref, w_ref, b_ref, o_ref):
    o_ref[...] = jnp.dot(x_ref[...].astype(BF16), w_ref[...], preferred_element_type=F32) + b_ref[...]


def _in_proj(h2, w_bf, b, tm=256):
    m, d = h2.shape
    n = w_bf.shape[1]
    return pl.pallas_call(
        _proj_body,
        grid=(m // tm,),
        in_specs=[pl.BlockSpec((tm, d), lambda i: (i, 0)),
                  pl.BlockSpec((d, n), lambda i: (0, 0)),
                  pl.BlockSpec((1, n), lambda i: (0, 0))],
        out_specs=pl.BlockSpec((tm, n), lambda i: (i, 0)),
        out_shape=jax.ShapeDtypeStruct((m, n), F32),
        compiler_params=_cparams(("parallel",)),
        name="in_proj",
    )(h2, w_bf, b.reshape(1, n))


def _ln_rows(x, g, b):
    mu = jnp.mean(x, axis=-1, keepdims=True)
    xc = x - mu
    var = jnp.mean(xc * xc, axis=-1, keepdims=True)
    return xc * lax.rsqrt(var + LN_EPS) * g + b


def _mix_out(rows, ya_ref, yb_ref, yc_ref, yd_ref, h_ref, w_ref, g_ref, b_ref):
    y = _mm(ya_ref[rows, :], w_ref[0:GROUP_W, :])
    y += _mm(yb_ref[rows, :], w_ref[GROUP_W:2 * GROUP_W, :])
    y += _mm(yc_ref[rows, :], w_ref[2 * GROUP_W:3 * GROUP_W, :])
    y += _mm(yd_ref[rows, :], w_ref[3 * GROUP_W:4 * GROUP_W, :])
    return _ln_rows(DN_ALPHA * h_ref[rows, :] + y, g_ref[...], b_ref[...])


def _sub_tiles(ref):
    return [slice(r, r + PROJ_SUB) for r in range(0, ref.shape[0], PROJ_SUB)]


def _ln_proj_body(x_ref, g_ref, b_ref, w_ref, bias_ref, h_ref, o_ref):
    for rows in _sub_tiles(x_ref):
        h = _ln_rows(x_ref[rows, :], g_ref[...], b_ref[...])
        h_ref[rows, :] = h
        o_ref[rows, :] = jnp.dot(h.astype(BF16), w_ref[...], preferred_element_type=F32) + bias_ref[...]


def _out_proj_body(ya_ref, yb_ref, yc_ref, yd_ref, hp_ref, wo_ref, g_ref, b_ref, w_ref, bias_ref, h_ref, o_ref):
    for rows in _sub_tiles(hp_ref):
        h = _mix_out(rows, ya_ref, yb_ref, yc_ref, yd_ref, hp_ref, wo_ref, g_ref, b_ref)
        h_ref[rows, :] = h
        o_ref[rows, :] = jnp.dot(h.astype(BF16), w_ref[...], preferred_element_type=F32) + bias_ref[...]


def _ln_in_proj(x2, g, b, w_bf, bias, tm=PROJ_TM):
    m, d = x2.shape
    n = w_bf.shape[1]
    row = lambda width: pl.BlockSpec((tm, width), lambda i: (i, 0))
    const = lambda shape: pl.BlockSpec(shape, lambda i: (0, 0))
    return pl.pallas_call(
        _ln_proj_body,
        grid=(m // tm,),
        in_specs=[row(d), const((1, d)), const((1, d)), const((d, n)), const((1, n))],
        out_specs=[row(d), row(n)],
        out_shape=[jax.ShapeDtypeStruct((m, d), F32), jax.ShapeDtypeStruct((m, n), F32)],
        compiler_params=_cparams(("parallel",)),
        name="ln_in_proj",
    )(x2, g.reshape(1, d), b.reshape(1, d), w_bf, bias.reshape(1, n))


def _out_in_proj(ys, h2, wo_bf, g, b, w_bf, bias, tm=PROJ_TM):
    m, d = h2.shape
    n = w_bf.shape[1]
    row = lambda width: pl.BlockSpec((tm, width), lambda i: (i, 0))
    const = lambda shape: pl.BlockSpec(shape, lambda i: (0, 0))
    return pl.pallas_call(
        _out_proj_body,
        grid=(m // tm,),
        in_specs=[row(GROUP_W)] * N_GROUPS + [row(d), const((N_GROUPS * GROUP_W, d)), const((1, d)), const((1, d)),
                                              const((d, n)), const((1, n))],
        out_specs=[row(d), row(n)],
        out_shape=[jax.ShapeDtypeStruct((m, d), F32), jax.ShapeDtypeStruct((m, n), F32)],
        compiler_params=_cparams(("parallel",)),
        name="out_in_proj",
    )(*[y.reshape(m, GROUP_W) for y in ys], h2, wo_bf, g.reshape(1, d), b.reshape(1, d), w_bf, bias.reshape(1, n))


def _out_body(ya_ref, yb_ref, yc_ref, yd_ref, h_ref, w_ref, g_ref, b_ref, o_ref):
    o_ref[...] = _mix_out(slice(None), ya_ref, yb_ref, yc_ref, yd_ref, h_ref, w_ref, g_ref, b_ref)


def _out_proj(ys, h2, w_bf, g, b, tm=512):
    m, d = h2.shape
    yspec = pl.BlockSpec((tm, GROUP_W), lambda i: (i, 0))
    return pl.pallas_call(
        _out_body,
        grid=(m // tm,),
        in_specs=[yspec, yspec, yspec, yspec,
                  pl.BlockSpec((tm, d), lambda i: (i, 0)),
                  pl.BlockSpec((4 * GROUP_W, d), lambda i: (0, 0)),
                  pl.BlockSpec((1, d), lambda i: (0, 0)),
                  pl.BlockSpec((1, d), lambda i: (0, 0))],
        out_specs=pl.BlockSpec((tm, d), lambda i: (i, 0)),
        out_shape=jax.ShapeDtypeStruct((m, d), F32),
        compiler_params=_cparams(("parallel",)),
        name="out_proj",
    )(*[y.reshape(m, GROUP_W) for y in ys], h2, w_bf, g.reshape(1, d), b.reshape(1, d))


def _mixer_a_body(sink_ref, q_ref, kv_ref, z_ref, cos_ref, up_ref, dn_ref, o_ref, k_ref, vt_ref):
    i = pl.program_id(1)
    seq = kv_ref.shape[1]
    n_kv = N_HEADS // A_GROUP
    low_half = _iota((1, LANES), 1) < HEAD_DIM

    @pl.when(i == 0)
    def _():
        def rope_rows(c, carry):
            for u in range(PRO_ROWS // QBLK):
                t0 = c * (PRO_ROWS // QBLK) + u
                rows = pl.ds(pl.multiple_of(t0 * QBLK, QBLK), QBLK)
                kr = _rope(kv_ref[0, rows, 0:128], cos_ref[rows, 0:128], up_ref[rows, 0:128], dn_ref[rows, 0:128])
                v_t = kv_ref[0, rows, 128:256].T.astype(BF16)
                ones_rows = jnp.ones((V_AUG - HEAD_DIM, QBLK), BF16)
                other = pltpu.roll(kr, HEAD_DIM, axis=1)
                for g in range(n_kv):
                    dup = jnp.where(low_half, kr, other) if g == 0 else jnp.where(low_half, other, kr)
                    k_ref[g, rows, :] = dup.astype(BF16)
                    vt_ref[g, t0] = jnp.concatenate([v_t[g * HEAD_DIM:(g + 1) * HEAD_DIM, :], ones_rows], axis=0)
            return carry
        lax.fori_loop(0, seq // PRO_ROWS, rope_rows, 0)

    n_blk = A_WINDOW // QBLK + 1
    span = n_blk * QBLK
    first_head = _iota((1, A_GROUP * QBLK), 1) < QBLK
    eye = (_iota((QBLK, QBLK), 0) == _iota((QBLK, QBLK), 1)).astype(BF16)

    def query_block(u):
        blk_i = i * A_SUB + u
        sub = slice(u * QBLK, (u + 1) * QBLK)
        q0 = pl.multiple_of(blk_i * QBLK, QBLK)
        qrows = pl.ds(q0, QBLK)
        qr = _rope(q_ref[0, sub, :], cos_ref[qrows, :], up_ref[qrows, :], dn_ref[qrows, :]) * (SCALE * LOG2E)
        kt0 = jnp.maximum(blk_i - A_WINDOW // QBLK, 0)
        k0 = pl.multiple_of(kt0 * QBLK, QBLK)
        rel = (q0 + _iota((1, QBLK), 1)) - (k0 + _iota((span, 1), 0))
        in_band = lax.bitcast_convert_type(rel, jnp.uint32) < A_WINDOW
        bias = jnp.concatenate([jnp.where(in_band, 0.0, NEG)] * A_GROUP, axis=1)
        outs = []
        for g in range(n_kv):
            q_pair = qr[:, g * LANES:(g + 1) * LANES]
            qg = jnp.concatenate([jnp.where(low_half, q_pair, 0.0), jnp.where(low_half, 0.0, q_pair)], axis=0)
            sink = jnp.where(first_head, sink_ref[g * A_GROUP], sink_ref[g * A_GROUP + 1]) * LOG2E
            s = _mm_nt(k_ref[g, pl.ds(k0, span), :], qg) + bias
            m = jnp.maximum(jnp.max(s, axis=0, keepdims=True), sink)
            p = jnp.exp2(s - m).astype(BF16)
            o = jnp.zeros((V_AUG, A_GROUP * QBLK), F32)
            for c in range(n_blk):
                o = o + jnp.dot(vt_ref[g, kt0 + c], p[c * QBLK:(c + 1) * QBLK], preferred_element_type=F32)
            o = o[0:HEAD_DIM] / (o[HEAD_DIM:HEAD_DIM + 1] + jnp.exp2(sink - m))
            outs.extend(o[:, r * QBLK:(r + 1) * QBLK] for r in range(A_GROUP))
        y = _mm_nt(eye, jnp.concatenate(outs, axis=0))
        o_ref[0, sub, :] = (y * _silu(z_ref[0, sub, :])).astype(Y_DTYPE)

    for u in range(A_SUB):
        query_block(u)


def _mixer_a(proj, sinks, tabs):
    bsz, seq, _ = proj.shape
    cos, up, dn = tabs
    blk = lambda name: _new_offset(name) // GROUP_W
    tspec = pl.BlockSpec((seq, GROUP_W), lambda b, i: (0, 0))
    return pl.pallas_call(
        _mixer_a_body,
        grid=(bsz, seq // (A_SUB * QBLK)),
        in_specs=[pl.BlockSpec(memory_space=pltpu.SMEM),
                  pl.BlockSpec((1, A_SUB * QBLK, GROUP_W), lambda b, i: (b, i, blk('a_q'))),
                  pl.BlockSpec((1, seq, GROUP_W), lambda b, i: (b, 0, blk('a_k'))),
                  pl.BlockSpec((1, A_SUB * QBLK, GROUP_W), lambda b, i: (b, i, blk('a_z'))),
                  tspec, tspec, tspec],
        out_specs=pl.BlockSpec((1, A_SUB * QBLK, GROUP_W), lambda b, i: (b, i, 0)),
        out_shape=jax.ShapeDtypeStruct((bsz, seq, GROUP_W), Y_DTYPE),
        scratch_shapes=[pltpu.VMEM((N_HEADS // A_GROUP, seq, LANES), BF16),
                        pltpu.VMEM((N_HEADS // A_GROUP, seq // QBLK, V_AUG, QBLK), BF16)],
        compiler_params=_cparams(("arbitrary", "arbitrary")),
        name="mixer_a",
    )(sinks, proj, proj, proj, cos, up, dn)


B_CHUNK = 128
B_STEP = 1024
B_AUG = HEAD_DIM + 16


def _conv_silu(x_ref, r0, prev, w, b):
    L = B_CHUNK
    x = x_ref[0, r0:r0 + L, :]
    acc = b + w[B_CONV - 1:B_CONV, :] * x
    for s in range(1, B_CONV):
        if r0 >= s:
            shifted = x_ref[0, r0 - s:r0 - s + L, :]
        else:
            row8 = _iota((8, x.shape[1]), 0)
            xs = pltpu.roll(x, s, axis=0)
            ps = pltpu.roll(prev, s, axis=0)
            shifted = jnp.concatenate([jnp.where(row8 < s, ps, xs[0:8]), xs[8:]], axis=0)
        acc = acc + w[B_CONV - 1 - s:B_CONV - s, :] * shifted
    return _silu(acc)


def _mixer_b_body(xq_ref, xk_ref, v_ref, og_ref, z_ref, if_ref, cw_ref, cb_ref, o_ref, xprev_ref, c_ref, m_ref):
    c = pl.program_id(1)
    L = B_CHUNK

    @pl.when(c == 0)
    def _():
        xprev_ref[...] = jnp.zeros_like(xprev_ref)
        c_ref[...] = jnp.zeros_like(c_ref)
        m_ref[...] = jnp.zeros_like(m_ref)

    tri = (_iota((L, L), 1) <= _iota((L, L), 0)).astype(F32)
    key_first = _iota((L, L), 0) <= _iota((L, L), 1)
    half_of_lane = _iota((1, 128), 1) // HEAD_DIM
    ones_rows = jnp.ones((B_AUG - HEAD_DIM, L), F32)

    def chunk(r0, state):
        rows = slice(r0, r0 + L)
        qc = _conv_silu(xq_ref, r0, xprev_ref[:, 0:GROUP_W], cw_ref[:, 0:GROUP_W], cb_ref[:, 0:GROUP_W]) * SCALE
        kc = _conv_silu(xk_ref, r0, xprev_ref[:, GROUP_W:], cw_ref[:, GROUP_W:], cb_ref[:, GROUP_W:])
        q_t = qc.T
        v_t = v_ref[0, rows, :].T

        gates = if_ref[0, rows, :]
        bcum = _mm_sel(tri, _log_sigmoid(gates))
        gates_t = gates.T
        bcum_t = bcum.T
        outs, new_state = [], []
        for h in range(N_HEADS):
            pair = slice((h // 2) * 128, (h // 2 + 1) * 128)
            own = half_of_lane == (h % 2)
            k_pair = kc[:, pair]
            q_pair_t = q_t[pair, :]
            i_row = gates_t[h:h + 1, :]
            b_row = bcum_t[N_HEADS + h:N_HEADS + h + 1, :]
            c_col = gates[:, h:h + 1] - bcum[:, N_HEADS + h:N_HEADS + h + 1]
            cst, m_prev = state[h]
            dmat = jnp.where(key_first, b_row + c_col, NEG)
            inter = b_row + m_prev
            m_t = jnp.maximum(jnp.max(dmat, axis=0, keepdims=True), inter)
            smat = _mm(jnp.where(own, k_pair, 0.0), q_pair_t) * jnp.exp(dmat - m_t)
            vaug_t = jnp.concatenate([v_t[h * HEAD_DIM:(h + 1) * HEAD_DIM, :], ones_rows], axis=0)
            tot = _mm(vaug_t, smat) + jnp.exp(inter - m_t) * _mm(cst, q_pair_t)
            num = tot[0:HEAD_DIM, :]
            den = tot[HEAD_DIM:HEAD_DIM + 1, :]
            outs.append(num / jnp.maximum(jnp.abs(den), jnp.exp(-m_t)))
            b_last = b_row[:, L - 1:L]
            d_row = b_last - b_row + i_row
            m_new = jnp.maximum(b_last + m_prev, jnp.max(d_row, axis=1, keepdims=True))
            w_row = jnp.exp(d_row - m_new)
            decay = jnp.exp(b_last + m_prev - m_new)
            new_state.append((decay * cst + jnp.where(own, _mm(vaug_t * w_row, k_pair), 0.0), m_new))
        o_ref[0, rows, :] = (_sigmoid(og_ref[0, rows, :]) * jnp.concatenate(outs, axis=0).T
                             * _silu(z_ref[0, rows, :])).astype(Y_DTYPE)
        return new_state

    rows_step = xq_ref.shape[1]
    state = [(c_ref[h], m_ref[h:h + 1, 0:1]) for h in range(N_HEADS)]
    for u in range(rows_step // L):
        state = chunk(u * L, state)
    xprev_ref[:, 0:GROUP_W] = xq_ref[0, rows_step - 8:rows_step, :]
    xprev_ref[:, GROUP_W:] = xk_ref[0, rows_step - 8:rows_step, :]
    for h in range(N_HEADS):
        c_ref[h] = state[h][0]
        m_ref[h:h + 1, :] = jnp.broadcast_to(state[h][1], (1, 128))


def _mixer_b(proj, conv_w, conv_b):
    bsz, seq, _ = proj.shape
    L = B_STEP
    blk = lambda name: _new_offset(name) // GROUP_W
    return pl.pallas_call(
        _mixer_b_body,
        grid=(bsz, seq // L),
        in_specs=[pl.BlockSpec((1, L, GROUP_W), lambda b, c: (b, c, blk('b_q'))),
                  pl.BlockSpec((1, L, GROUP_W), lambda b, c: (b, c, blk('b_k'))),
                  pl.BlockSpec((1, L, GROUP_W), lambda b, c: (b, c, blk('b_v'))),
                  pl.BlockSpec((1, L, GROUP_W), lambda b, c: (b, c, blk('b_o'))),
                  pl.BlockSpec((1, L, GROUP_W), lambda b, c: (b, c, blk('b_z'))),
                  pl.BlockSpec((1, L, LANES), lambda b, c: (b, c, _new_offset('gates') // LANES)),
                  pl.BlockSpec((B_CONV, 2 * GROUP_W), lambda b, c: (0, 0)),
                  pl.BlockSpec((1, 2 * GROUP_W), lambda b, c: (0, 0))],
        out_specs=pl.BlockSpec((1, L, GROUP_W), lambda b, c: (b, c, 0)),
        out_shape=jax.ShapeDtypeStruct((bsz, seq, GROUP_W), Y_DTYPE),
        scratch_shapes=[pltpu.VMEM((8, 2 * GROUP_W), F32),
                        pltpu.VMEM((N_HEADS, B_AUG, 128), F32),
                        pltpu.VMEM((8, 128), F32)],
        compiler_params=_cparams(("arbitrary", "arbitrary")),
        name="mixer_b",
    )(proj, proj, proj, proj, proj, proj, conv_w, conv_b.reshape(1, -1))


def _mixer_c_body(layer, q_ref, f_ref, i_ref, z_ref, lb_ref, gn_ref, tri_ref, piv_ref, o_ref, st_ref):
    c = pl.program_id(1)
    L = C_CHUNK
    nsub = L // C_SUB

    @pl.when(c == 0)
    def _():
        st_ref[...] = jnp.zeros_like(st_ref)

    raw = lb_ref[...]
    ex = jnp.exp(raw - jnp.max(raw, axis=0, keepdims=True))
    if layer == 0:
        lb = jnp.zeros((1, GROUP_W), F32)
    else:
        lb = jnp.sum(ex[1:layer + 1], axis=0, keepdims=True) / jnp.sum(ex, axis=0, keepdims=True)
    tri = (_iota((L, L), 1) <= _iota((L, L), 0)).astype(F32)
    head_of_lane = _iota((1, GROUP_W), 1) // HEAD_DIM
    same_head = ((_iota((GROUP_W, GROUP_W), 0) // HEAD_DIM)
                 == (_iota((GROUP_W, GROUP_W), 1) // HEAD_DIM))
    ones_bd = same_head.astype(BF16)

    def chunk(rows, st):
        f = lb + (1.0 - lb) * _sigmoid(f_ref[0, rows, :])
        logf = jnp.log(f)
        k = 1.0 - f
        q = _silu(q_ref[0, rows, :])
        v = i_ref[0, rows, :]
        a = _mm_sel(tri, logf) * LOG2E
        o = _mm_nt(q * jnp.exp2(a), st)

        qsts, kts, vreps = [], [], []
        for i in range(1, nsub):
            r0 = i * C_SUB
            piv = a[r0 - 1:r0, :]
            qt = q[r0:r0 + C_SUB] * jnp.exp2(a[r0:r0 + C_SUB] - piv)
            qsts.extend(jnp.where(head_of_lane == h, qt, 0.0) for h in range(N_HEADS))
            kts.append((k[0:r0] * jnp.exp2(piv - a[0:r0])).astype(BF16))
            vreps.append(v[0:r0].astype(BF16))
        scores = _mm_nt(jnp.concatenate(qsts, axis=0), jnp.concatenate(kts, axis=0))
        r = _dot_bf16((scores * piv_ref[...]).astype(BF16), jnp.concatenate(vreps, axis=0))
        pieces = [o[0:C_SUB]]
        for i in range(1, nsub):
            acc = o[i * C_SUB:(i + 1) * C_SUB]
            for h in range(N_HEADS):
                g0 = ((i - 1) * N_HEADS + h) * C_SUB
                acc = acc + jnp.where(head_of_lane == h, r[g0:g0 + C_SUB], 0.0)
            pieces.append(acc)
        o = jnp.concatenate(pieces, axis=0)

        a3 = a.reshape(nsub, C_SUB, GROUP_W)
        q3 = q.reshape(nsub, C_SUB, GROUP_W)
        k3 = k.reshape(nsub, C_SUB, GROUP_W)
        v3 = v.reshape(nsub, C_SUB, GROUP_W)
        ps = []
        for s in range(C_SUB):
            e = jnp.exp2(a3 - a3[:, s:s + 1, :] + tri_ref[s])
            ps.append((q3 * k3[:, s:s + 1, :] * e).reshape(L, GROUP_W).astype(BF16))
        w = jnp.dot(jnp.concatenate(ps, axis=0), ones_bd, preferred_element_type=F32)
        for s in range(C_SUB):
            vs = jnp.broadcast_to(v3[:, s:s + 1, :], (nsub, C_SUB, GROUP_W)).reshape(L, GROUP_W)
            o = o + w[s * L:(s + 1) * L] * vs

        a_last = a[L - 1:L, :]
        kdec = k * jnp.exp2(a_last - a)
        st_new = st * jnp.exp2(a_last) + jnp.where(same_head, _mm_tn(v, kdec), 0.0)

        ms = jnp.dot((o * o).astype(BF16), ones_bd, preferred_element_type=F32) * (1.0 / HEAD_DIM)
        o = o * lax.rsqrt(ms + C_EPS) * gn_ref[...]
        o_ref[0, rows, :] = (o * _silu(z_ref[0, rows, :])).astype(Y_DTYPE)
        return st_new

    st = st_ref[...]
    for u in range(q_ref.shape[1] // L):
        st = chunk(slice(u * L, (u + 1) * L), st)
    st_ref[...] = st


def _mixer_c(proj, c_lb, norm_g, layer):
    bsz, seq, _ = proj.shape
    L = C_STEP
    blk = lambda name: _new_offset(name) // GROUP_W
    spec = lambda name: pl.BlockSpec((1, L, GROUP_W), lambda b, c: (b, c, blk(name)))
    vec = pl.BlockSpec((1, GROUP_W), lambda b, c: (0, 0))
    t_ge_s = np.arange(C_SUB)[None, :, None] >= np.arange(C_SUB)[:, None, None]
    causal_bias = np.broadcast_to(np.where(t_ge_s, 0.0, NEG), (C_SUB, C_SUB, GROUP_W)).astype(np.float32)
    pivots = np.arange(1, C_CHUNK // C_SUB)
    row_piv = np.repeat(pivots, N_HEADS * C_SUB)
    col_piv = np.repeat(pivots, pivots * C_SUB)
    same_pivot = (row_piv[:, None] == col_piv[None, :]).astype(np.float32)
    return pl.pallas_call(
        functools.partial(_mixer_c_body, layer),
        grid=(bsz, seq // L),
        in_specs=[spec('c_q'), spec('c_f'), spec('c_i'), spec('c_z'),
                  pl.BlockSpec((DEPTH, GROUP_W), lambda b, c: (0, 0)), vec,
                  pl.BlockSpec((C_SUB, C_SUB, GROUP_W), lambda b, c: (0, 0, 0)),
                  pl.BlockSpec(same_pivot.shape, lambda b, c: (0, 0))],
        out_specs=pl.BlockSpec((1, L, GROUP_W), lambda b, c: (b, c, 0)),
        out_shape=jax.ShapeDtypeStruct((bsz, seq, GROUP_W), Y_DTYPE),
        scratch_shapes=[pltpu.VMEM((GROUP_W, GROUP_W), F32)],
        compiler_params=_cparams(("arbitrary", "arbitrary")),
        name="mixer_c",
    )(proj, proj, proj, proj, c_lb, norm_g.reshape(1, -1), jnp.asarray(causal_bias), jnp.asarray(same_pivot))


N_CMP_PAD = 128
KT = 512
D_QBLK = 256


def _mixer_d_body(q_ref, z_ref, ksv_ref, kwv_ref, cin_ref, g_ref, cos_ref, up_ref, dn_ref, kcos_ref, kup_ref,
                  kdn_ref, ccos_ref, cup_ref, cdn_ref, w1_ref, pe_ref, w2_ref, ov_ref, o_ref,
                  k_ref, vts_ref, vtw_ref, kc_ref, vct_ref, sel_ref):
    QBLK = D_QBLK
    i = pl.program_id(1)
    seq = ksv_ref.shape[1]
    half_cmp = CMP_LEN // 2

    @pl.when(i == 0)
    def _():
        def rope_rows(c, carry):
            for u in range(PRO_ROWS // KT):
                t0 = c * (PRO_ROWS // KT) + u
                rows = pl.ds(pl.multiple_of(t0 * KT, KT), KT)
                kvs = _rope(ksv_ref[0, rows, :], kcos_ref[rows, :], kup_ref[rows, :], kdn_ref[rows, :])
                kvw = _rope(kwv_ref[0, rows, :], kcos_ref[rows, :], kup_ref[rows, :], kdn_ref[rows, :])
                k_ref[0, rows, :] = kvs[:, 0:HEAD_DIM].astype(BF16)
                k_ref[1, rows, :] = kvw[:, 0:HEAD_DIM].astype(BF16)
                ones_rows = jnp.ones((V_AUG - HEAD_DIM, KT), BF16)
                vts_ref[t0] = jnp.concatenate([kvs.T[HEAD_DIM:2 * HEAD_DIM, :].astype(BF16), ones_rows], axis=0)
                vwt = jnp.concatenate([kvw.T[HEAD_DIM:2 * HEAD_DIM, :].astype(BF16), ones_rows], axis=0)
                for w in range(KT // QBLK):
                    vtw_ref[t0 * (KT // QBLK) + w] = vwt[:, w * QBLK:(w + 1) * QBLK]
            return carry
        lax.fori_loop(0, seq // PRO_ROWS, rope_rows, 0)
        u0 = jnp.zeros((N_CMP_PAD, 128), F32)
        u1 = jnp.zeros((N_CMP_PAD, 128), F32)
        for r in range(half_cmp):
            xr = cin_ref[0, pl.ds(r, N_CMP_PAD, stride=CMP_STRIDE), :]
            u0 = u0 + _mm_hi(xr + pe_ref[r:r + 1, :], w1_ref[r])
            u1 = u1 + _mm_hi(xr + pe_ref[half_cmp + r:half_cmp + r + 1, :], w1_ref[half_cmp + r])
        pre = u0 + pltpu.roll(u1, N_CMP_PAD - 1, axis=0)
        cv = _rope(_mm_hi(_silu(pre), w2_ref[...]), ccos_ref[...], cup_ref[...], cdn_ref[...])
        kc_ref[...] = cv[:, 0:HEAD_DIM]
        vct_ref[...] = cv.T[HEAD_DIM:2 * HEAD_DIM, :].astype(BF16)

    q0 = pl.multiple_of(i * QBLK, QBLK)
    qrows = pl.ds(q0, QBLK)
    qr = _rope(q_ref[0], cos_ref[qrows, :], up_ref[qrows, :], dn_ref[qrows, :]) * (SCALE * LOG2E)
    qt = qr.T
    qst = jnp.concatenate([qt[h * HEAD_DIM:(h + 1) * HEAD_DIM] for h in range(N_HEADS)], axis=1)
    qst_bf = qst.astype(BF16)
    cols4 = N_HEADS * QBLK
    qpos = q0 + _iota((1, QBLK), 1)

    def lanes4(x):
        return jnp.concatenate([x] * N_HEADS, axis=1)

    sc = _mm_hi(kc_ref[...], qst)
    ends = _iota((N_CMP_PAD, 1), 0) * CMP_STRIDE + (CMP_LEN - 1)
    scm = sc + lanes4(jnp.where(ends <= qpos, 0.0, NEG))
    e = jnp.exp2(scm - jnp.max(scm, axis=0, keepdims=True))
    l = jnp.sum(e, axis=0, keepdims=True)
    any_valid = lanes4(jnp.where(qpos >= CMP_LEN - 1, 1.0, 0.0))
    pc = e * (any_valid / l)
    o_cmp = jnp.dot(vct_ref[...], pc.astype(BF16), preferred_element_type=F32)
    psum = pc[:, 0:QBLK] + pc[:, QBLK:2 * QBLK] + pc[:, 2 * QBLK:3 * QBLK] + pc[:, 3 * QBLK:4 * QBLK]

    n_slc = seq // SLC_LEN
    imp = _mm_sel(ov_ref[...], psum)
    jl = _iota((n_slc, 1), 0)
    cur = jnp.right_shift(qpos, SLC_LEN.bit_length() - 1)
    imp = jnp.where(jl == 0, FORCE_SCORE, imp)
    imp = jnp.where(jl == cur, FORCE_SCORE, imp)
    imp = jnp.where(jl == cur - 1, FORCE_SCORE, imp)
    imp = jnp.where(jl <= cur, imp, NEG)
    rank = jnp.zeros((n_slc, QBLK), F32)
    for jp in range(n_slc):
        cj = imp[jp:jp + 1, :]
        tie = jnp.where(jl > jp, 1.0, 0.0)
        rank = rank + jnp.where(cj > imp, 1.0, jnp.where(cj == imp, tie, 0.0))
    sel_bias = jnp.where(rank < N_SEL, 0.0, NEG)
    blocks_per_tile = KT // SLC_LEN
    for t in range(n_slc // blocks_per_tile):
        sel_ref[t] = sel_bias[t * blocks_per_tile:(t + 1) * blocks_per_tile, :]

    n_win = NSA_WINDOW // QBLK + 1
    span = n_win * QBLK
    kt0 = jnp.maximum(i - NSA_WINDOW // QBLK, 0)
    k0 = pl.multiple_of(kt0 * QBLK, QBLK)
    rel = qpos - (k0 + _iota((span, 1), 0))
    in_band = lax.bitcast_convert_type(rel, jnp.uint32) < NSA_WINDOW
    s = jnp.dot(k_ref[1, pl.ds(k0, span), :], qst_bf, preferred_element_type=F32)
    s = s + lanes4(jnp.where(in_band, 0.0, NEG))
    p = jnp.exp2(s - jnp.max(s, axis=0, keepdims=True)).astype(BF16)
    o_win = jnp.zeros((V_AUG, cols4), F32)
    for c in range(n_win):
        o_win = o_win + jnp.dot(vtw_ref[kt0 + c], p[c * QBLK:(c + 1) * QBLK], preferred_element_type=F32)
    o_win = o_win[0:HEAD_DIM] / o_win[HEAD_DIM:HEAD_DIM + 1]

    def sel_tile(kt, carry):
        m_old, acc = carry
        kbase = pl.multiple_of(kt * KT, KT)
        picks = sel_ref[kt]
        chosen = jnp.concatenate([jnp.broadcast_to(picks[j:j + 1, :], (SLC_LEN, QBLK))
                                  for j in range(blocks_per_tile)], axis=0)
        bias = jnp.where((kbase + _iota((KT, 1), 0)) <= qpos, chosen, NEG)
        s = jnp.dot(k_ref[0, pl.ds(kbase, KT), :], qst_bf, preferred_element_type=F32) + lanes4(bias)
        m_new = jnp.maximum(m_old, jnp.max(s, axis=0, keepdims=True))
        p = jnp.exp2(s - m_new)
        acc_new = jnp.exp2(m_old - m_new) * acc + jnp.dot(vts_ref[kt], p.astype(BF16), preferred_element_type=F32)
        return m_new, acc_new

    init = (jnp.full((1, cols4), NEG, F32), jnp.zeros((V_AUG, cols4), F32))
    n_tiles = (i + KT // QBLK) // (KT // QBLK)
    _, acc_s = lax.fori_loop(0, n_tiles, sel_tile, init)
    o_slc = acc_s[0:HEAD_DIM] / acc_s[HEAD_DIM:HEAD_DIM + 1]

    gate = _sigmoid(g_ref[0]).T
    outs = []
    for h in range(N_HEADS):
        cs = slice(h * QBLK, (h + 1) * QBLK)
        g0 = D_GATE_LANE + 3 * h
        outs.append(gate[g0:g0 + 1, :] * o_cmp[:, cs] + gate[g0 + 1:g0 + 2, :] * o_slc[:, cs]
                    + gate[g0 + 2:g0 + 3, :] * o_win[:, cs])
    o_ref[0] = (jnp.concatenate(outs, axis=0).T * _silu(z_ref[0])).astype(Y_DTYPE)


def _mixer_d(proj, qtabs, ktabs, ctabs, w1bd, pe2, w2bd, overlap):
    QBLK = D_QBLK
    bsz, seq, _ = proj.shape
    blk = lambda name: _new_offset(name) // GROUP_W
    full = lambda shape: pl.BlockSpec(shape, lambda b, i: (0,) * len(shape))
    return pl.pallas_call(
        _mixer_d_body,
        grid=(bsz, seq // QBLK),
        in_specs=[pl.BlockSpec((1, QBLK, GROUP_W), lambda b, i: (b, i, blk('d_q'))),
                  pl.BlockSpec((1, QBLK, GROUP_W), lambda b, i: (b, i, blk('d_z'))),
                  pl.BlockSpec((1, seq, 128), lambda b, i: (b, 0, _new_offset('d_ksv') // 128)),
                  pl.BlockSpec((1, seq, 128), lambda b, i: (b, 0, _new_offset('d_kwv') // 128)),
                  pl.BlockSpec((1, seq, 128), lambda b, i: (b, 0, _new_offset('d_kvc') // 128)),
                  pl.BlockSpec((1, QBLK, LANES), lambda b, i: (b, i, _new_offset('gates') // LANES)),
                  full((seq, GROUP_W)), full((seq, GROUP_W)), full((seq, GROUP_W)),
                  full((seq, 128)), full((seq, 128)), full((seq, 128)),
                  full((N_CMP_PAD, 128)), full((N_CMP_PAD, 128)), full((N_CMP_PAD, 128)),
                  full((CMP_LEN, 128, 128)), full((CMP_LEN, 128)), full((128, 128)),
                  full((seq // SLC_LEN, N_CMP_PAD))],
        out_specs=pl.BlockSpec((1, QBLK, GROUP_W), lambda b, i: (b, i, 0)),
        out_shape=jax.ShapeDtypeStruct((bsz, seq, GROUP_W), Y_DTYPE),
        scratch_shapes=[pltpu.VMEM((2, seq, HEAD_DIM), BF16),
                        pltpu.VMEM((seq // KT, V_AUG, KT), BF16),
                        pltpu.VMEM((seq // QBLK, V_AUG, QBLK), BF16),
                        pltpu.VMEM((N_CMP_PAD, HEAD_DIM), F32),
                        pltpu.VMEM((HEAD_DIM, N_CMP_PAD), BF16),
                        pltpu.VMEM((seq // KT, KT // SLC_LEN, QBLK), F32)],
        compiler_params=_cparams(("arbitrary", "arbitrary")),
        name="mixer_d",
    )(proj, proj, proj, proj, proj, proj, *qtabs, *ktabs, *ctabs, w1bd, pe2, w2bd, overlap)


def _cmp_params(pe, w1, w2):
    w1 = w1.reshape(2, CMP_LEN, HEAD_DIM, HEAD_DIM)
    z = jnp.zeros((CMP_LEN, HEAD_DIM, HEAD_DIM), F32)
    w1bd = jnp.concatenate([jnp.concatenate([w1[0], z], axis=2), jnp.concatenate([z, w1[1]], axis=2)], axis=1)
    z2 = jnp.zeros((HEAD_DIM, HEAD_DIM), F32)
    w2bd = jnp.concatenate([jnp.concatenate([w2[0], z2], axis=1), jnp.concatenate([z2, w2[1]], axis=1)], axis=0)
    pe2 = jnp.concatenate([pe[0], pe[1]], axis=1)
    return w1bd, pe2, w2bd


def _overlap_matrix(seq):
    n_cmp = (seq - CMP_LEN) // CMP_STRIDE + 1
    starts = np.arange(N_CMP_PAD) * CMP_STRIDE
    blk = np.arange(seq // SLC_LEN)
    ov = ((starts[None, :] < (blk[:, None] + 1) * SLC_LEN) & (starts[None, :] + CMP_LEN > blk[:, None] * SLC_LEN))
    ov = ov & (np.arange(N_CMP_PAD)[None, :] < n_cmp)
    return jnp.asarray(ov.astype(np.float32))


def kernel(x, ln0_g, ln0_b, w_in, b_in, a_sinks, b_conv_w, b_conv_b, c_lb, c_norm_g, d_cmp_pe, d_cmp_w1, d_cmp_w2,
           w_out, ln_g, ln_b):
    bsz, seq, d = x.shape
    pos = jnp.arange(seq)
    qtabs = _rope_tables(pos, GROUP_W, 128)
    ktabs = _rope_tables(pos, 128, HEAD_DIM)
    ctabs = _rope_tables(jnp.arange(N_CMP_PAD) * CMP_STRIDE + (CMP_LEN - 1), 128, HEAD_DIM)
    overlap = _overlap_matrix(seq)

    w_p, b_p = _prep_in_weights(w_in, b_in, 0)
    h, proj = _ln_in_proj(x.reshape(bsz * seq, d), ln0_g, ln0_b, w_p, b_p)
    for l in range(DEPTH):
        proj = proj.reshape(bsz, seq, N_PROJ)
        y_a = _mixer_a(proj, a_sinks[l], qtabs)
        y_b = _mixer_b(proj, b_conv_w[l], b_conv_b[l])
        y_c = _mixer_c(proj, c_lb, c_norm_g[l], l)
        w1bd, pe2, w2bd = _cmp_params(d_cmp_pe[l], d_cmp_w1[l], d_cmp_w2[l])
        y_d = _mixer_d(proj, qtabs, ktabs, ctabs, w1bd, pe2, w2bd, overlap)
        ys = (y_a, y_b, y_c, y_d)
        if l + 1 < DEPTH:
            w_p, b_p = _prep_in_weights(w_in, b_in, l + 1)
            h, proj = _out_in_proj(ys, h, w_out[l].astype(BF16), ln_g[l], ln_b[l], w_p, b_p)
        else:
            h = _out_proj(ys, h, w_out[l].astype(BF16), ln_g[l], ln_b[l])
    return h.reshape(bsz, seq, d)
```

```python
import functools

import numpy as np
import jax
import jax.numpy as jnp
from jax import lax
from jax.experimental import pallas as pl
from jax.experimental.pallas import tpu as pltpu

F32 = jnp.float32
BF16 = jnp.bfloat16
Y_DTYPE = BF16

D_MODEL = 1024
DEPTH = 2
HEAD_DIM = 64
HALF = HEAD_DIM // 2
N_HEADS = 4
GROUP_W = N_HEADS * HEAD_DIM
N_GROUPS = 4
ROPE_THETA = 10000.0
NEG = -1e30
LN_EPS = 1e-5
A_WINDOW = 128
A_GROUP = 2
A_SUB = 4
V_AUG = HEAD_DIM + 16
LOG2E = 1.4426950408889634
PRO_ROWS = 512
B_CONV = 4
C_CHUNK = 64
C_SUB = 8
C_STEP = 512
C_EPS = 1e-6
CMP_LEN = 32
CMP_STRIDE = 16
SLC_LEN = 64
N_SEL = 8
NSA_WINDOW = 512
FORCE_SCORE = 1e6
DN_ALPHA = (2.0 * DEPTH) ** 0.25
QBLK = 128
SCALE = HEAD_DIM ** -0.5

VMEM_LIMIT = 56 * 1024 * 1024
PROJ_TM = 512
PROJ_SUB = 256

LANES = 128

ORIG_SPLITS = (
    ('a_q', 256), ('a_k', 128), ('a_v', 128), ('a_z', 256),
    ('b_q', 256), ('b_k', 256), ('b_v', 256), ('b_if', 8), ('b_o', 256), ('b_z', 256),
    ('c_q', 256), ('c_f', 256), ('c_i', 256), ('c_z', 256),
    ('d_q', 256), ('d_kvc', 128), ('d_ksv', 128), ('d_kwv', 128), ('d_g', 12), ('d_z', 256),
)
N_COLS = sum(w for _, w in ORIG_SPLITS)
GATE_SLOTS = ('b_if', 'd_g')
NEW_LAYOUT = (
    ('a_q', 256), ('a_k', 128), ('a_v', 128), ('a_z', 256), ('b_q', 256), ('b_k', 256), ('b_v', 256),
    ('b_o', 256), ('b_z', 256), ('c_q', 256), ('c_f', 256), ('c_i', 256), ('c_z', 256),
    ('d_q', 256), ('d_kvc', 128), ('d_ksv', 128), ('d_kwv', 128), ('gates', 128), ('d_z', 256),
)
N_PROJ = sum(w for _, w in NEW_LAYOUT)
D_GATE_LANE = dict(ORIG_SPLITS)['b_if']


def _offset(layout, name):
    off = 0
    for n, w in layout:
        if n == name:
            return off
        off += w
    raise KeyError(name)


def _new_offset(name):
    return _offset(NEW_LAYOUT, name)


def _permuted_cols(load):
    sizes = dict(ORIG_SPLITS)
    for name, width in NEW_LAYOUT:
        new = _new_offset(name)
        if name != 'gates':
            old = _offset(ORIG_SPLITS, name)
            lo = old // LANES * LANES
            hi = min(-(-(old + width) // LANES) * LANES, N_COLS)
            yield new, load(lo, hi)[..., old - lo:old - lo + width]
            continue
        gates, lane0 = None, 0
        for slot in GATE_SLOTS:
            old = _offset(ORIG_SPLITS, slot)
            lo = old // LANES * LANES
            tile = load(lo, lo + LANES)
            if old - lo != lane0:
                tile = pltpu.roll(tile, (lane0 - (old - lo)) % LANES, axis=tile.ndim - 1)
            lane = _iota(tile.shape, tile.ndim - 1)
            part = jnp.where((lane >= lane0) & (lane < lane0 + sizes[slot]), tile, 0.0)
            gates = part if gates is None else gates + part
            lane0 += sizes[slot]
        yield new, gates


def _prep_body(w_ref, b_ref, wo_ref, bo_ref):
    for new, cols in _permuted_cols(lambda lo, hi: w_ref[0, :, lo:hi]):
        wo_ref[:, new:new + cols.shape[-1]] = cols.astype(BF16)
    for new, cols in _permuted_cols(lambda lo, hi: b_ref[0, :, lo:hi]):
        bo_ref[:, new:new + cols.shape[-1]] = cols


def _prep_in_weights(w_in, b_in, layer, rows=128):
    depth, d, n = w_in.shape
    return pl.pallas_call(
        _prep_body,
        grid=(d // rows,),
        in_specs=[pl.BlockSpec((1, rows, n), lambda r: (layer, r, 0)),
                  pl.BlockSpec((1, 1, n), lambda r: (layer, 0, 0))],
        out_specs=[pl.BlockSpec((rows, N_PROJ), lambda r: (r, 0)),
                   pl.BlockSpec((1, N_PROJ), lambda r: (0, 0))],
        out_shape=[jax.ShapeDtypeStruct((d, N_PROJ), BF16), jax.ShapeDtypeStruct((1, N_PROJ), F32)],
        compiler_params=_cparams(("arbitrary",)),
        name="prep_in_weights",
    )(w_in, b_in.reshape(depth, 1, n))


def _mm(a, b):
    return jnp.dot(a.astype(BF16), b.astype(BF16), preferred_element_type=F32)


def _mm_nt(a, b):
    return lax.dot_general(a.astype(BF16), b.astype(BF16), (((1,), (1,)), ((), ())),
                           preferred_element_type=F32)


def _mm_tn(a, b):
    return lax.dot_general(a.astype(BF16), b.astype(BF16), (((0,), (0,)), ((), ())),
                           preferred_element_type=F32)


def _dot_bf16(a, b):
    return jnp.dot(a, b, preferred_element_type=F32)


def _split_bf16(x, terms):
    out = []
    for _ in range(terms - 1):
        t = x.astype(BF16)
        out.append(t)
        x = x - t.astype(F32)
    out.append(x.astype(BF16))
    return out


def _mm_hi(a, b):
    ah, al = _split_bf16(a, 2)
    bh, bl = _split_bf16(b, 2)
    return _dot_bf16(ah, bh) + (_dot_bf16(ah, bl) + _dot_bf16(al, bh))


def _mm_sel(sel, x):
    sel = sel.astype(BF16)
    x1, x2, x3 = _split_bf16(x, 3)
    return _dot_bf16(sel, x1) + (_dot_bf16(sel, x2) + _dot_bf16(sel, x3))


def _sigmoid(x):
    return 1.0 / (1.0 + jnp.exp2(x * (-LOG2E)))


def _silu(x):
    return x * _sigmoid(x)


def _log_sigmoid(x):
    return jnp.minimum(x, 0.0) - jnp.log1p(jnp.exp(-jnp.abs(x)))


def _rope(x, cos, sin_up, sin_dn):
    w = x.shape[-1]
    up = pltpu.roll(x, w - HALF, axis=1)
    dn = pltpu.roll(x, HALF, axis=1)
    return x * cos + up * sin_up + dn * sin_dn


def _rope_tables(pos, width, rot_lanes):
    inv = ROPE_THETA ** (-jnp.arange(HALF, dtype=F32) / HALF)
    ang = pos.astype(F32)[:, None] * inv[None, :]
    cos, sin = jnp.cos(ang), jnp.sin(ang)
    zero, one = jnp.zeros_like(sin), jnp.ones_like(cos)
    cos_h = jnp.concatenate([cos, cos], axis=1)
    up_h = jnp.concatenate([-sin, zero], axis=1)
    dn_h = jnp.concatenate([zero, sin], axis=1)
    id_c = jnp.concatenate([one, one], axis=1)
    id_s = jnp.concatenate([zero, zero], axis=1)
    cs, us, ds = [], [], []
    for l0 in range(0, width, HEAD_DIM):
        rot = (l0 % 128) < rot_lanes
        cs.append(cos_h if rot else id_c)
        us.append(up_h if rot else id_s)
        ds.append(dn_h if rot else id_s)
    return jnp.concatenate(cs, axis=1), jnp.concatenate(us, axis=1), jnp.concatenate(ds, axis=1)


def _iota(shape, dim):
    return lax.broadcasted_iota(jnp.int32, shape, dim)


def _cparams(sem):
    return pltpu.CompilerParams(dimension_semantics=sem, vmem_limit_bytes=VMEM_LIMIT)


def _ln_body(x_ref, g_ref, b_ref, o_ref):
    x = x_ref[...]
    mu = jnp.mean(x, axis=-1, keepdims=True)
    xc = x - mu
    var = jnp.mean(xc * xc, axis=-1, keepdims=True)
    o_ref[...] = xc * lax.rsqrt(var + LN_EPS) * g_ref[...] + b_ref[...]


def _layer_norm(x2, g, b, tm=512):
    m, d = x2.shape
    return pl.pallas_call(
        _ln_body,
        grid=(m // tm,),
        in_specs=[pl.BlockSpec((tm, d), lambda i: (i, 0)),
                  pl.BlockSpec((1, d), lambda i: (0, 0)),
                  pl.BlockSpec((1, d), lambda i: (0, 0))],
        out_specs=pl.BlockSpec((tm, d), lambda i: (i, 0)),
        out_shape=jax.ShapeDtypeStruct((m, d), F32),
        compiler_params=_cparams(("parallel",)),
        name="layer_norm",
    )(x2, g.reshape(1, d), b.reshape(1, d))


def _proj_body(x_ref, w_ref, b_ref, o_ref):
    o_ref[...] = jnp.dot(x_ref[...].astype(BF16), w_ref[...], preferred_element_type=F32) + b_ref[...]


def _in_proj(h2, w_bf, b, tm=256):
    m, d = h2.shape
    n = w_bf.shape[1]
    return pl.pallas_call(
        _proj_body,
        grid=(m // tm,),
        in_specs=[pl.BlockSpec((tm, d), lambda i: (i, 0)),
                  pl.BlockSpec((d, n), lambda i: (0, 0)),
                  pl.BlockSpec((1, n), lambda i: (0, 0))],
        out_specs=pl.BlockSpec((tm, n), lambda i: (i, 0)),
        out_shape=jax.ShapeDtypeStruct((m, n), F32),
        compiler_params=_cparams(("parallel",)),
        name="in_proj",
    )(h2, w_bf, b.reshape(1, n))


def _ln_rows(x, g, b):
    mu = jnp.mean(x, axis=-1, keepdims=True)
    xc = x - mu
    var = jnp.mean(xc * xc, axis=-1, keepdims=True)
    return xc * lax.rsqrt(var + LN_EPS) * g + b


def _mix_out(rows, ya_ref, yb_ref, yc_ref, yd_ref, h_ref, w_ref, g_ref, b_ref):
    y = _mm(ya_ref[rows, :], w_ref[0:GROUP_W, :])
    y += _mm(yb_ref[rows, :], w_ref[GROUP_W:2 * GROUP_W, :])
    y += _mm(yc_ref[rows, :], w_ref[2 * GROUP_W:3 * GROUP_W, :])
    y += _mm(yd_ref[rows, :], w_ref[3 * GROUP_W:4 * GROUP_W, :])
    return _ln_rows(DN_ALPHA * h_ref[rows, :] + y, g_ref[...], b_ref[...])


def _sub_tiles(ref):
    return [slice(r, r + PROJ_SUB) for r in range(0, ref.shape[0], PROJ_SUB)]


def _ln_proj_body(x_ref, g_ref, b_ref, w_ref, bias_ref, h_ref, o_ref):
    for rows in _sub_tiles(x_ref):
        h = _ln_rows(x_ref[rows, :], g_ref[...], b_ref[...])
        h_ref[rows, :] = h
        o_ref[rows, :] = jnp.dot(h.astype(BF16), w_ref[...], preferred_element_type=F32) + bias_ref[...]


def _out_proj_body(ya_ref, yb_ref, yc_ref, yd_ref, hp_ref, wo_ref, g_ref, b_ref, w_ref, bias_ref, h_ref, o_ref):
    for rows in _sub_tiles(hp_ref):
        h = _mix_out(rows, ya_ref, yb_ref, yc_ref, yd_ref, hp_ref, wo_ref, g_ref, b_ref)
        h_ref[rows, :] = h
        o_ref[rows, :] = jnp.dot(h.astype(BF16), w_ref[...], preferred_element_type=F32) + bias_ref[...]


def _ln_in_proj(x2, g, b, w_bf, bias, tm=PROJ_TM):
    m, d = x2.shape
    n = w_bf.shape[1]
    row = lambda width: pl.BlockSpec((tm, width), lambda i: (i, 0))
    const = lambda shape: pl.BlockSpec(shape, lambda i: (0, 0))
    return pl.pallas_call(
        _ln_proj_body,
        grid=(m // tm,),
        in_specs=[row(d), const((1, d)), const((1, d)), const((d, n)), const((1, n))],
        out_specs=[row(d), row(n)],
        out_shape=[jax.ShapeDtypeStruct((m, d), F32), jax.ShapeDtypeStruct((m, n), F32)],
        compiler_params=_cparams(("parallel",)),
        name="ln_in_proj",
    )(x2, g.reshape(1, d), b.reshape(1, d), w_bf, bias.reshape(1, n))


def _out_in_proj(ys, h2, wo_bf, g, b, w_bf, bias, tm=PROJ_TM):
    m, d = h2.shape
    n = w_bf.shape[1]
    row = lambda width: pl.BlockSpec((tm, width), lambda i: (i, 0))
    const = lambda shape: pl.BlockSpec(shape, lambda i: (0, 0))
    return pl.pallas_call(
        _out_proj_body,
        grid=(m // tm,),
        in_specs=[row(GROUP_W)] * N_GROUPS + [row(d), const((N_GROUPS * GROUP_W, d)), const((1, d)), const((1, d)),
                                              const((d, n)), const((1, n))],
        out_specs=[row(d), row(n)],
        out_shape=[jax.ShapeDtypeStruct((m, d), F32), jax.ShapeDtypeStruct((m, n), F32)],
        compiler_params=_cparams(("parallel",)),
        name="out_in_proj",
    )(*[y.reshape(m, GROUP_W) for y in ys], h2, wo_bf, g.reshape(1, d), b.reshape(1, d), w_bf, bias.reshape(1, n))


def _out_body(ya_ref, yb_ref, yc_ref, yd_ref, h_ref, w_ref, g_ref, b_ref, o_ref):
    o_ref[...] = _mix_out(slice(None), ya_ref, yb_ref, yc_ref, yd_ref, h_ref, w_ref, g_ref, b_ref)


def _out_proj(ys, h2, w_bf, g, b, tm=512):
    m, d = h2.shape
    yspec = pl.BlockSpec((tm, GROUP_W), lambda i: (i, 0))
    return pl.pallas_call(
        _out_body,
        grid=(m // tm,),
        in_specs=[yspec, yspec, yspec, yspec,
                  pl.BlockSpec((tm, d), lambda i: (i, 0)),
                  pl.BlockSpec((4 * GROUP_W, d), lambda i: (0, 0)),
                  pl.BlockSpec((1, d), lambda i: (0, 0)),
                  pl.BlockSpec((1, d), lambda i: (0, 0))],
        out_specs=pl.BlockSpec((tm, d), lambda i: (i, 0)),
        out_shape=jax.ShapeDtypeStruct((m, d), F32),
        compiler_params=_cparams(("parallel",)),
        name="out_proj",
    )(*[y.reshape(m, GROUP_W) for y in ys], h2, w_bf, g.reshape(1, d), b.reshape(1, d))


def _mixer_a_body(sink_ref, q_ref, kv_ref, z_ref, cos_ref, up_ref, dn_ref, o_ref, k_ref, vt_ref):
    i = pl.program_id(1)
    seq = kv_ref.shape[1]
    n_kv = N_HEADS // A_GROUP
    low_half = _iota((1, LANES), 1) < HEAD_DIM

    @pl.when(i == 0)
    def _():
        def rope_rows(c, carry):
            for u in range(PRO_ROWS // QBLK):
                t0 = c * (PRO_ROWS // QBLK) + u
                rows = pl.ds(pl.multiple_of(t0 * QBLK, QBLK), QBLK)
                kr = _rope(kv_ref[0, rows, 0:128], cos_ref[rows, 0:128], up_ref[rows, 0:128], dn_ref[rows, 0:128])
                v_t = kv_ref[0, rows, 128:256].T.astype(BF16)
                ones_rows = jnp.ones((V_AUG - HEAD_DIM, QBLK), BF16)
                other = pltpu.roll(kr, HEAD_DIM, axis=1)
                for g in range(n_kv):
                    dup = jnp.where(low_half, kr, other) if g == 0 else jnp.where(low_half, other, kr)
                    k_ref[g, rows, :] = dup.astype(BF16)
                    vt_ref[g, t0] = jnp.concatenate([v_t[g * HEAD_DIM:(g + 1) * HEAD_DIM, :], ones_rows], axis=0)
            return carry
        lax.fori_loop(0, seq // PRO_ROWS, rope_rows, 0)

    n_blk = A_WINDOW // QBLK + 1
    span = n_blk * QBLK
    first_head = _iota((1, A_GROUP * QBLK), 1) < QBLK
    eye = (_iota((QBLK, QBLK), 0) == _iota((QBLK, QBLK), 1)).astype(BF16)

    def query_block(u):
        blk_i = i * A_SUB + u
        sub = slice(u * QBLK, (u + 1) * QBLK)
        q0 = pl.multiple_of(blk_i * QBLK, QBLK)
        qrows = pl.ds(q0, QBLK)
        qr = _rope(q_ref[0, sub, :], cos_ref[qrows, :], up_ref[qrows, :], dn_ref[qrows, :]) * (SCALE * LOG2E)
        kt0 = jnp.maximum(blk_i - A_WINDOW // QBLK, 0)
        k0 = pl.multiple_of(kt0 * QBLK, QBLK)
        rel = (q0 + _iota((1, QBLK), 1)) - (k0 + _iota((span, 1), 0))
        in_band = lax.bitcast_convert_type(rel, jnp.uint32) < A_WINDOW
        bias = jnp.concatenate([jnp.where(in_band, 0.0, NEG)] * A_GROUP, axis=1)
        outs = []
        for g in range(n_kv):
            q_pair = qr[:, g * LANES:(g + 1) * LANES]
            qg = jnp.concatenate([jnp.where(low_half, q_pair, 0.0), jnp.where(low_half, 0.0, q_pair)], axis=0)
            sink = jnp.where(first_head, sink_ref[g * A_GROUP], sink_ref[g * A_GROUP + 1]) * LOG2E
            s = _mm_nt(k_ref[g, pl.ds(k0, span), :], qg) + bias
            m = jnp.maximum(jnp.max(s, axis=0, keepdims=True), sink)
            p = jnp.exp2(s - m).astype(BF16)
            o = jnp.zeros((V_AUG, A_GROUP * QBLK), F32)
            for c in range(n_blk):
                o = o + jnp.dot(vt_ref[g, kt0 + c], p[c * QBLK:(c + 1) * QBLK], preferred_element_type=F32)
            o = o[0:HEAD_DIM] / (o[HEAD_DIM:HEAD_DIM + 1] + jnp.exp2(sink - m))
            outs.extend(o[:, r * QBLK:(r + 1) * QBLK] for r in range(A_GROUP))
        y = _mm_nt(eye, jnp.concatenate(outs, axis=0))
        o_ref[0, sub, :] = (y * _silu(z_ref[0, sub, :])).astype(Y_DTYPE)

    for u in range(A_SUB):
        query_block(u)


def _mixer_a(proj, sinks, tabs):
    bsz, seq, _ = proj.shape
    cos, up, dn = tabs
    blk = lambda name: _new_offset(name) // GROUP_W
    tspec = pl.BlockSpec((seq, GROUP_W), lambda b, i: (0, 0))
    return pl.pallas_call(
        _mixer_a_body,
        grid=(bsz, seq // (A_SUB * QBLK)),
        in_specs=[pl.BlockSpec(memory_space=pltpu.SMEM),
                  pl.BlockSpec((1, A_SUB * QBLK, GROUP_W), lambda b, i: (b, i, blk('a_q'))),
                  pl.BlockSpec((1, seq, GROUP_W), lambda b, i: (b, 0, blk('a_k'))),
                  pl.BlockSpec((1, A_SUB * QBLK, GROUP_W), lambda b, i: (b, i, blk('a_z'))),
                  tspec, tspec, tspec],
        out_specs=pl.BlockSpec((1, A_SUB * QBLK, GROUP_W), lambda b, i: (b, i, 0)),
        out_shape=jax.ShapeDtypeStruct((bsz, seq, GROUP_W), Y_DTYPE),
        scratch_shapes=[pltpu.VMEM((N_HEADS // A_GROUP, seq, LANES), BF16),
                        pltpu.VMEM((N_HEADS // A_GROUP, seq // QBLK, V_AUG, QBLK), BF16)],
        compiler_params=_cparams(("arbitrary", "arbitrary")),
        name="mixer_a",
    )(sinks, proj, proj, proj, cos, up, dn)


B_CHUNK = 128
B_STEP = 1024
B_AUG = HEAD_DIM + 16


def _conv_silu(x_ref, r0, prev, w, b):
    L = B_CHUNK
    x = x_ref[0, r0:r0 + L, :]
    acc = b + w[B_CONV - 1:B_CONV, :] * x
    for s in range(1, B_CONV):
        if r0 >= s:
            shifted = x_ref[0, r0 - s:r0 - s + L, :]
        else:
            row8 = _iota((8, x.shape[1]), 0)
            xs = pltpu.roll(x, s, axis=0)
            ps = pltpu.roll(prev, s, axis=0)
            shifted = jnp.concatenate([jnp.where(row8 < s, ps, xs[0:8]), xs[8:]], axis=0)
        acc = acc + w[B_CONV - 1 - s:B_CONV - s, :] * shifted
    return _silu(acc)


def _mixer_b_body(xq_ref, xk_ref, v_ref, og_ref, z_ref, if_ref, cw_ref, cb_ref, o_ref, xprev_ref, c_ref, m_ref):
    c = pl.program_id(1)
    L = B_CHUNK

    @pl.when(c == 0)
    def _():
        xprev_ref[...] = jnp.zeros_like(xprev_ref)
        c_ref[...] = jnp.zeros_like(c_ref)
        m_ref[...] = jnp.zeros_like(m_ref)

    tri = (_iota((L, L), 1) <= _iota((L, L), 0)).astype(F32)
    key_first = _iota((L, L), 0) <= _iota((L, L), 1)
    half_of_lane = _iota((1, 128), 1) // HEAD_DIM
    ones_rows = jnp.ones((B_AUG - HEAD_DIM, L), F32)

    def chunk(r0, state):
        rows = slice(r0, r0 + L)
        qc = _conv_silu(xq_ref, r0, xprev_ref[:, 0:GROUP_W], cw_ref[:, 0:GROUP_W], cb_ref[:, 0:GROUP_W]) * SCALE
        kc = _conv_silu(xk_ref, r0, xprev_ref[:, GROUP_W:], cw_ref[:, GROUP_W:], cb_ref[:, GROUP_W:])
        q_t = qc.T
        v_t = v_ref[0, rows, :].T

        gates = if_ref[0, rows, :]
        bcum = _mm_sel(tri, _log_sigmoid(gates))
        gates_t = gates.T
        bcum_t = bcum.T
        outs, new_state = [], []
        for h in range(N_HEADS):
            pair = slice((h // 2) * 128, (h // 2 + 1) * 128)
            own = half_of_lane == (h % 2)
            k_pair = kc[:, pair]
            q_pair_t = q_t[pair, :]
            i_row = gates_t[h:h + 1, :]
            b_row = bcum_t[N_HEADS + h:N_HEADS + h + 1, :]
            c_col = gates[:, h:h + 1] - bcum[:, N_HEADS + h:N_HEADS + h + 1]
            cst, m_prev = state[h]
            dmat = jnp.where(key_first, b_row + c_col, NEG)
            inter = b_row + m_prev
            m_t = jnp.maximum(jnp.max(dmat, axis=0, keepdims=True), inter)
            smat = _mm(jnp.where(own, k_pair, 0.0), q_pair_t) * jnp.exp(dmat - m_t)
            vaug_t = jnp.concatenate([v_t[h * HEAD_DIM:(h + 1) * HEAD_DIM, :], ones_rows], axis=0)
            tot = _mm(vaug_t, smat) + jnp.exp(inter - m_t) * _mm(cst, q_pair_t)
            num = tot[0:HEAD_DIM, :]
            den = tot[HEAD_DIM:HEAD_DIM + 1, :]
            outs.append(num / jnp.maximum(jnp.abs(den), jnp.exp(-m_t)))
            b_last = b_row[:, L - 1:L]
            d_row = b_last - b_row + i_row
            m_new = jnp.maximum(b_last + m_prev, jnp.max(d_row, axis=1, keepdims=True))
            w_row = jnp.exp(d_row - m_new)
            decay = jnp.exp(b_last + m_prev - m_new)
            new_state.append((decay * cst + jnp.where(own, _mm(vaug_t * w_row, k_pair), 0.0), m_new))
        o_ref[0, rows, :] = (_sigmoid(og_ref[0, rows, :]) * jnp.concatenate(outs, axis=0).T
                             * _silu(z_ref[0, rows, :])).astype(Y_DTYPE)
        return new_state

    rows_step = xq_ref.shape[1]
    state = [(c_ref[h], m_ref[h:h + 1, 0:1]) for h in range(N_HEADS)]
    for u in range(rows_step // L):
        state = chunk(u * L, state)
    xprev_ref[:, 0:GROUP_W] = xq_ref[0, rows_step - 8:rows_step, :]
    xprev_ref[:, GROUP_W:] = xk_ref[0, rows_step - 8:rows_step, :]
    for h in range(N_HEADS):
        c_ref[h] = state[h][0]
        m_ref[h:h + 1, :] = jnp.broadcast_to(state[h][1], (1, 128))


def _mixer_b(proj, conv_w, conv_b):
    bsz, seq, _ = proj.shape
    L = B_STEP
    blk = lambda name: _new_offset(name) // GROUP_W
    return pl.pallas_call(
        _mixer_b_body,
        grid=(bsz, seq // L),
        in_specs=[pl.BlockSpec((1, L, GROUP_W), lambda b, c: (b, c, blk('b_q'))),
                  pl.BlockSpec((1, L, GROUP_W), lambda b, c: (b, c, blk('b_k'))),
                  pl.BlockSpec((1, L, GROUP_W), lambda b, c: (b, c, blk('b_v'))),
                  pl.BlockSpec((1, L, GROUP_W), lambda b, c: (b, c, blk('b_o'))),
                  pl.BlockSpec((1, L, GROUP_W), lambda b, c: (b, c, blk('b_z'))),
                  pl.BlockSpec((1, L, LANES), lambda b, c: (b, c, _new_offset('gates') // LANES)),
                  pl.BlockSpec((B_CONV, 2 * GROUP_W), lambda b, c: (0, 0)),
                  pl.BlockSpec((1, 2 * GROUP_W), lambda b, c: (0, 0))],
        out_specs=pl.BlockSpec((1, L, GROUP_W), lambda b, c: (b, c, 0)),
        out_shape=jax.ShapeDtypeStruct((bsz, seq, GROUP_W), Y_DTYPE),
        scratch_shapes=[pltpu.VMEM((8, 2 * GROUP_W), F32),
                        pltpu.VMEM((N_HEADS, B_AUG, 128), F32),
                        pltpu.VMEM((8, 128), F32)],
        compiler_params=_cparams(("arbitrary", "arbitrary")),
        name="mixer_b",
    )(proj, proj, proj, proj, proj, proj, conv_w, conv_b.reshape(1, -1))


def _mixer_c_body(layer, q_ref, f_ref, i_ref, z_ref, lb_ref, gn_ref, tri_ref, piv_ref, o_ref, st_ref):
    c = pl.program_id(1)
    L = C_CHUNK
    nsub = L // C_SUB

    @pl.when(c == 0)
    def _():
        st_ref[...] = jnp.zeros_like(st_ref)

    raw = lb_ref[...]
    ex = jnp.exp(raw - jnp.max(raw, axis=0, keepdims=True))
    if layer == 0:
        lb = jnp.zeros((1, GROUP_W), F32)
    else:
        lb = jnp.sum(ex[1:layer + 1], axis=0, keepdims=True) / jnp.sum(ex, axis=0, keepdims=True)
    tri = (_iota((L, L), 1) <= _iota((L, L), 0)).astype(F32)
    head_of_lane = _iota((1, GROUP_W), 1) // HEAD_DIM
    same_head = ((_iota((GROUP_W, GROUP_W), 0) // HEAD_DIM)
                 == (_iota((GROUP_W, GROUP_W), 1) // HEAD_DIM))
    ones_bd = same_head.astype(BF16)

    def chunk(rows, st):
        f = lb + (1.0 - lb) * _sigmoid(f_ref[0, rows, :])
        logf = jnp.log(f)
        k = 1.0 - f
        q = _silu(q_ref[0, rows, :])
        v = i_ref[0, rows, :]
        a = _mm_sel(tri, logf) * LOG2E
        o = _mm_nt(q * jnp.exp2(a), st)

        qsts, kts, vreps = [], [], []
        for i in range(1, nsub):
            r0 = i * C_SUB
            piv = a[r0 - 1:r0, :]
            qt = q[r0:r0 + C_SUB] * jnp.exp2(a[r0:r0 + C_SUB] - piv)
            qsts.extend(jnp.where(head_of_lane == h, qt, 0.0) for h in range(N_HEADS))
            kts.append((k[0:r0] * jnp.exp2(piv - a[0:r0])).astype(BF16))
            vreps.append(v[0:r0].astype(BF16))
        scores = _mm_nt(jnp.concatenate(qsts, axis=0), jnp.concatenate(kts, axis=0))
        r = _dot_bf16((scores * piv_ref[...]).astype(BF16), jnp.concatenate(vreps, axis=0))
        pieces = [o[0:C_SUB]]
        for i in range(1, nsub):
            acc = o[i * C_SUB:(i + 1) * C_SUB]
            for h in range(N_HEADS):
                g0 = ((i - 1) * N_HEADS + h) * C_SUB
                acc = acc + jnp.where(head_of_lane == h, r[g0:g0 + C_SUB], 0.0)
            pieces.append(acc)
        o = jnp.concatenate(pieces, axis=0)

        a3 = a.reshape(nsub, C_SUB, GROUP_W)
        q3 = q.reshape(nsub, C_SUB, GROUP_W)
        k3 = k.reshape(nsub, C_SUB, GROUP_W)
        v3 = v.reshape(nsub, C_SUB, GROUP_W)
        ps = []
        for s in range(C_SUB):
            e = jnp.exp2(a3 - a3[:, s:s + 1, :] + tri_ref[s])
            ps.append((q3 * k3[:, s:s + 1, :] * e).reshape(L, GROUP_W).astype(BF16))
        w = jnp.dot(jnp.concatenate(ps, axis=0), ones_bd, preferred_element_type=F32)
        for s in range(C_SUB):
            vs = jnp.broadcast_to(v3[:, s:s + 1, :], (nsub, C_SUB, GROUP_W)).reshape(L, GROUP_W)
            o = o + w[s * L:(s + 1) * L] * vs

        a_last = a[L - 1:L, :]
        kdec = k * jnp.exp2(a_last - a)
        st_new = st * jnp.exp2(a_last) + jnp.where(same_head, _mm_tn(v, kdec), 0.0)

        ms = jnp.dot((o * o).astype(BF16), ones_bd, preferred_element_type=F32) * (1.0 / HEAD_DIM)
        o = o * lax.rsqrt(ms + C_EPS) * gn_ref[...]
        o_ref[0, rows, :] = (o * _silu(z_ref[0, rows, :])).astype(Y_DTYPE)
        return st_new

    st = st_ref[...]
    for u in range(q_ref.shape[1] // L):
        st = chunk(slice(u * L, (u + 1) * L), st)
    st_ref[...] = st


def _mixer_c(proj, c_lb, norm_g, layer):
    bsz, seq, _ = proj.shape
    L = C_STEP
    blk = lambda name: _new_offset(name) // GROUP_W
    spec = lambda name: pl.BlockSpec((1, L, GROUP_W), lambda b, c: (b, c, blk(name)))
    vec = pl.BlockSpec((1, GROUP_W), lambda b, c: (0, 0))
    t_ge_s = np.arange(C_SUB)[None, :, None] >= np.arange(C_SUB)[:, None, None]
    causal_bias = np.broadcast_to(np.where(t_ge_s, 0.0, NEG), (C_SUB, C_SUB, GROUP_W)).astype(np.float32)
    pivots = np.arange(1, C_CHUNK // C_SUB)
    row_piv = np.repeat(pivots, N_HEADS * C_SUB)
    col_piv = np.repeat(pivots, pivots * C_SUB)
    same_pivot = (row_piv[:, None] == col_piv[None, :]).astype(np.float32)
    return pl.pallas_call(
        functools.partial(_mixer_c_body, layer),
        grid=(bsz, seq // L),
        in_specs=[spec('c_q'), spec('c_f'), spec('c_i'), spec('c_z'),
                  pl.BlockSpec((DEPTH, GROUP_W), lambda b, c: (0, 0)), vec,
                  pl.BlockSpec((C_SUB, C_SUB, GROUP_W), lambda b, c: (0, 0, 0)),
                  pl.BlockSpec(same_pivot.shape, lambda b, c: (0, 0))],
        out_specs=pl.BlockSpec((1, L, GROUP_W), lambda b, c: (b, c, 0)),
        out_shape=jax.ShapeDtypeStruct((bsz, seq, GROUP_W), Y_DTYPE),
        scratch_shapes=[pltpu.VMEM((GROUP_W, GROUP_W), F32)],
        compiler_params=_cparams(("arbitrary", "arbitrary")),
        name="mixer_c",
    )(proj, proj, proj, proj, c_lb, norm_g.reshape(1, -1), jnp.asarray(causal_bias), jnp.asarray(same_pivot))


N_CMP_PAD = 128
KT = 512
D_QBLK = 256
SEL_ROWS = 16


def _mixer_d_body(q_ref, z_ref, ksv_ref, kwv_ref, cin_ref, g_ref, cos_ref, up_ref, dn_ref, kcos_ref, kup_ref,
                  kdn_ref, ccos_ref, cup_ref, cdn_ref, w1_ref, pe_ref, w2_ref, ov_ref, o_ref,
                  ks_ref, kw_ref, vts_ref, vtw_ref, kc_ref, vct_ref, sel_ref, qaug_ref):
    QBLK = D_QBLK
    i = pl.program_id(1)
    seq = ksv_ref.shape[1]
    half_cmp = CMP_LEN // 2

    @pl.when(i == 0)
    def _():
        def rope_rows(c, carry):
            for u in range(PRO_ROWS // KT):
                t0 = c * (PRO_ROWS // KT) + u
                rows = pl.ds(pl.multiple_of(t0 * KT, KT), KT)
                kvs = _rope(ksv_ref[0, rows, :], kcos_ref[rows, :], kup_ref[rows, :], kdn_ref[rows, :])
                kvw = _rope(kwv_ref[0, rows, :], kcos_ref[rows, :], kup_ref[rows, :], kdn_ref[rows, :])
                blk_in_tile = _iota((KT, HEAD_DIM), 0) // SLC_LEN
                onehot = jnp.where(blk_in_tile == _iota((KT, HEAD_DIM), 1), 1.0, 0.0)
                ks_ref[rows, :] = jnp.concatenate([kvs[:, 0:HEAD_DIM], onehot], axis=1).astype(BF16)
                kw_ref[rows, :] = kvw[:, 0:HEAD_DIM].astype(BF16)
                ones_rows = jnp.ones((V_AUG - HEAD_DIM, KT), BF16)
                vts_ref[t0] = jnp.concatenate([kvs.T[HEAD_DIM:2 * HEAD_DIM, :].astype(BF16), ones_rows], axis=0)
                vwt = jnp.concatenate([kvw.T[HEAD_DIM:2 * HEAD_DIM, :].astype(BF16), ones_rows], axis=0)
                for w in range(KT // QBLK):
                    vtw_ref[t0 * (KT // QBLK) + w] = vwt[:, w * QBLK:(w + 1) * QBLK]
            return carry
        lax.fori_loop(0, seq // PRO_ROWS, rope_rows, 0)
        u0 = jnp.zeros((N_CMP_PAD, 128), F32)
        u1 = jnp.zeros((N_CMP_PAD, 128), F32)
        for r in range(half_cmp):
            xr = cin_ref[0, pl.ds(r, N_CMP_PAD, stride=CMP_STRIDE), :]
            u0 = u0 + _mm_hi(xr + pe_ref[r:r + 1, :], w1_ref[r])
            u1 = u1 + _mm_hi(xr + pe_ref[half_cmp + r:half_cmp + r + 1, :], w1_ref[half_cmp + r])
        pre = u0 + pltpu.roll(u1, N_CMP_PAD - 1, axis=0)
        cv = _rope(_mm_hi(_silu(pre), w2_ref[...]), ccos_ref[...], cup_ref[...], cdn_ref[...])
        kc_ref[...] = cv[:, 0:HEAD_DIM]
        vct_ref[...] = cv.T[HEAD_DIM:2 * HEAD_DIM, :].astype(BF16)

    q0 = pl.multiple_of(i * QBLK, QBLK)
    qrows = pl.ds(q0, QBLK)
    qr = _rope(q_ref[0], cos_ref[qrows, :], up_ref[qrows, :], dn_ref[qrows, :]) * (SCALE * LOG2E)
    qt = qr.T
    qst = jnp.concatenate([qt[h * HEAD_DIM:(h + 1) * HEAD_DIM] for h in range(N_HEADS)], axis=1)
    qst_bf = qst.astype(BF16)
    cols4 = N_HEADS * QBLK
    qpos = q0 + _iota((1, QBLK), 1)

    def lanes4(x):
        return jnp.concatenate([x] * N_HEADS, axis=1)

    sc = _mm_hi(kc_ref[...], qst)
    ends = _iota((N_CMP_PAD, 1), 0) * CMP_STRIDE + (CMP_LEN - 1)
    scm = sc + lanes4(jnp.where(ends <= qpos, 0.0, NEG))
    e = jnp.exp2(scm - jnp.max(scm, axis=0, keepdims=True))
    l = jnp.sum(e, axis=0, keepdims=True)
    any_valid = lanes4(jnp.where(qpos >= CMP_LEN - 1, 1.0, 0.0))
    pc = e * (any_valid / l)
    o_cmp = jnp.dot(vct_ref[...], pc.astype(BF16), preferred_element_type=F32)
    psum = pc[:, 0:QBLK] + pc[:, QBLK:2 * QBLK] + pc[:, 2 * QBLK:3 * QBLK] + pc[:, 3 * QBLK:4 * QBLK]

    n_slc = seq // SLC_LEN
    imp = _mm_sel(ov_ref[...], psum)
    jl = _iota((n_slc, 1), 0)
    cur = jnp.right_shift(qpos, SLC_LEN.bit_length() - 1)
    imp = jnp.where(jl == 0, FORCE_SCORE, imp)
    imp = jnp.where(jl == cur, FORCE_SCORE, imp)
    imp = jnp.where(jl == cur - 1, FORCE_SCORE, imp)
    imp = jnp.where(jl <= cur, imp, NEG)
    rank = jnp.zeros((n_slc, QBLK), F32)
    for jp in range(n_slc):
        cj = imp[jp:jp + 1, :]
        tie = jnp.where(jl > jp, 1.0, 0.0)
        rank = rank + jnp.where(cj > imp, 1.0, jnp.where(cj == imp, tie, 0.0))
    sel_bias = jnp.where(rank < N_SEL, 0.0, NEG)
    blocks_per_tile = KT // SLC_LEN
    for t in range(n_slc // blocks_per_tile):
        sel_ref[t] = sel_bias[t * blocks_per_tile:(t + 1) * blocks_per_tile, :]

    n_win = NSA_WINDOW // QBLK + 1
    span = n_win * QBLK
    kt0 = jnp.maximum(i - NSA_WINDOW // QBLK, 0)
    k0 = pl.multiple_of(kt0 * QBLK, QBLK)
    rel = qpos - (k0 + _iota((span, 1), 0))
    in_band = lax.bitcast_convert_type(rel, jnp.uint32) < NSA_WINDOW
    s = jnp.dot(kw_ref[pl.ds(k0, span), :], qst_bf, preferred_element_type=F32)
    s = s + lanes4(jnp.where(in_band, 0.0, NEG))
    p = jnp.exp2(s - jnp.max(s, axis=0, keepdims=True)).astype(BF16)
    o_win = jnp.zeros((V_AUG, cols4), F32)
    for c in range(n_win):
        o_win = o_win + jnp.dot(vtw_ref[kt0 + c], p[c * QBLK:(c + 1) * QBLK], preferred_element_type=F32)
    o_win = o_win[0:HEAD_DIM] / o_win[HEAD_DIM:HEAD_DIM + 1]

    qaug_ref[0:HEAD_DIM, :] = qst_bf
    qaug_ref[HEAD_DIM:2 * HEAD_DIM, :] = jnp.zeros((HEAD_DIM, cols4), BF16)

    def sel_scores(kt):
        picks = jnp.concatenate([sel_ref[kt], jnp.zeros((SEL_ROWS - blocks_per_tile, QBLK), F32)], axis=0)
        qaug_ref[HEAD_DIM:HEAD_DIM + SEL_ROWS, :] = lanes4(picks).astype(BF16)
        kbase = pl.multiple_of(kt * KT, KT)
        return jnp.dot(ks_ref[pl.ds(kbase, KT), :], qaug_ref[...], preferred_element_type=F32)

    def sel_update(kt, s, m_old, acc):
        m_new = jnp.maximum(m_old, jnp.max(s, axis=0, keepdims=True))
        p = jnp.exp2(s - m_new)
        acc_new = jnp.exp2(m_old - m_new) * acc + jnp.dot(vts_ref[kt], p.astype(BF16), preferred_element_type=F32)
        return m_new, acc_new

    init = (jnp.full((1, cols4), NEG, F32), jnp.zeros((V_AUG, cols4), F32))
    last = (i + KT // QBLK) // (KT // QBLK) - 1
    m_s, acc_s = lax.fori_loop(0, last, lambda kt, c: sel_update(kt, sel_scores(kt), *c), init)
    causal = jnp.where((last * KT + _iota((KT, 1), 0)) <= qpos, 0.0, NEG)
    _, acc_s = sel_update(last, sel_scores(last) + lanes4(causal), m_s, acc_s)
    o_slc = acc_s[0:HEAD_DIM] / acc_s[HEAD_DIM:HEAD_DIM + 1]

    gate = _sigmoid(g_ref[0]).T
    outs = []
    for h in range(N_HEADS):
        cs = slice(h * QBLK, (h + 1) * QBLK)
        g0 = D_GATE_LANE + 3 * h
        outs.append(gate[g0:g0 + 1, :] * o_cmp[:, cs] + gate[g0 + 1:g0 + 2, :] * o_slc[:, cs]
                    + gate[g0 + 2:g0 + 3, :] * o_win[:, cs])
    o_ref[0] = (jnp.concatenate(outs, axis=0).T * _silu(z_ref[0])).astype(Y_DTYPE)


def _mixer_d(proj, qtabs, ktabs, ctabs, w1bd, pe2, w2bd, overlap):
    QBLK = D_QBLK
    bsz, seq, _ = proj.shape
    blk = lambda name: _new_offset(name) // GROUP_W
    full = lambda shape: pl.BlockSpec(shape, lambda b, i: (0,) * len(shape))
    return pl.pallas_call(
        _mixer_d_body,
        grid=(bsz, seq // QBLK),
        in_specs=[pl.BlockSpec((1, QBLK, GROUP_W), lambda b, i: (b, i, blk('d_q'))),
                  pl.BlockSpec((1, QBLK, GROUP_W), lambda b, i: (b, i, blk('d_z'))),
                  pl.BlockSpec((1, seq, 128), lambda b, i: (b, 0, _new_offset('d_ksv') // 128)),
                  pl.BlockSpec((1, seq, 128), lambda b, i: (b, 0, _new_offset('d_kwv') // 128)),
                  pl.BlockSpec((1, seq, 128), lambda b, i: (b, 0, _new_offset('d_kvc') // 128)),
                  pl.BlockSpec((1, QBLK, LANES), lambda b, i: (b, i, _new_offset('gates') // LANES)),
                  full((seq, GROUP_W)), full((seq, GROUP_W)), full((seq, GROUP_W)),
                  full((seq, 128)), full((seq, 128)), full((seq, 128)),
                  full((N_CMP_PAD, 128)), full((N_CMP_PAD, 128)), full((N_CMP_PAD, 128)),
                  full((CMP_LEN, 128, 128)), full((CMP_LEN, 128)), full((128, 128)),
                  full((seq // SLC_LEN, N_CMP_PAD))],
        out_specs=pl.BlockSpec((1, QBLK, GROUP_W), lambda b, i: (b, i, 0)),
        out_shape=jax.ShapeDtypeStruct((bsz, seq, GROUP_W), Y_DTYPE),
        scratch_shapes=[pltpu.VMEM((seq, 2 * HEAD_DIM), BF16),
                        pltpu.VMEM((seq, HEAD_DIM), BF16),
                        pltpu.VMEM((seq // KT, V_AUG, KT), BF16),
                        pltpu.VMEM((seq // QBLK, V_AUG, QBLK), BF16),
                        pltpu.VMEM((N_CMP_PAD, HEAD_DIM), F32),
                        pltpu.VMEM((HEAD_DIM, N_CMP_PAD), BF16),
                        pltpu.VMEM((seq // KT, KT // SLC_LEN, QBLK), F32),
                        pltpu.VMEM((2 * HEAD_DIM, N_HEADS * QBLK), BF16)],
        compiler_params=_cparams(("arbitrary", "arbitrary")),
        name="mixer_d",
    )(proj, proj, proj, proj, proj, proj, *qtabs, *ktabs, *ctabs, w1bd, pe2, w2bd, overlap)


def _cmp_params(pe, w1, w2):
    w1 = w1.reshape(2, CMP_LEN, HEAD_DIM, HEAD_DIM)
    z = jnp.zeros((CMP_LEN, HEAD_DIM, HEAD_DIM), F32)
    w1bd = jnp.concatenate([jnp.concatenate([w1[0], z], axis=2), jnp.concatenate([z, w1[1]], axis=2)], axis=1)
    z2 = jnp.zeros((HEAD_DIM, HEAD_DIM), F32)
    w2bd = jnp.concatenate([jnp.concatenate([w2[0], z2], axis=1), jnp.concatenate([z2, w2[1]], axis=1)], axis=0)
    pe2 = jnp.concatenate([pe[0], pe[1]], axis=1)
    return w1bd, pe2, w2bd


def _overlap_matrix(seq):
    n_cmp = (seq - CMP_LEN) // CMP_STRIDE + 1
    starts = np.arange(N_CMP_PAD) * CMP_STRIDE
    blk = np.arange(seq // SLC_LEN)
    ov = ((starts[None, :] < (blk[:, None] + 1) * SLC_LEN) & (starts[None, :] + CMP_LEN > blk[:, None] * SLC_LEN))
    ov = ov & (np.arange(N_CMP_PAD)[None, :] < n_cmp)
    return jnp.asarray(ov.astype(np.float32))


def kernel(x, ln0_g, ln0_b, w_in, b_in, a_sinks, b_conv_w, b_conv_b, c_lb, c_norm_g, d_cmp_pe, d_cmp_w1, d_cmp_w2,
           w_out, ln_g, ln_b):
    bsz, seq, d = x.shape
    pos = jnp.arange(seq)
    qtabs = _rope_tables(pos, GROUP_W, 128)
    ktabs = _rope_tables(pos, 128, HEAD_DIM)
    ctabs = _rope_tables(jnp.arange(N_CMP_PAD) * CMP_STRIDE + (CMP_LEN - 1), 128, HEAD_DIM)
    overlap = _overlap_matrix(seq)

    w_p, b_p = _prep_in_weights(w_in, b_in, 0)
    h, proj = _ln_in_proj(x.reshape(bsz * seq, d), ln0_g, ln0_b, w_p, b_p)
    for l in range(DEPTH):
        proj = proj.reshape(bsz, seq, N_PROJ)
        y_a = _mixer_a(proj, a_sinks[l], qtabs)
        y_b = _mixer_b(proj, b_conv_w[l], b_conv_b[l])
        y_c = _mixer_c(proj, c_lb, c_norm_g[l], l)
        w1bd, pe2, w2bd = _cmp_params(d_cmp_pe[l], d_cmp_w1[l], d_cmp_w2[l])
        y_d = _mixer_d(proj, qtabs, ktabs, ctabs, w1bd, pe2, w2bd, overlap)
        ys = (y_a, y_b, y_c, y_d)
        if l + 1 < DEPTH:
            w_p, b_p = _prep_in_weights(w_in, b_in, l + 1)
            h, proj = _out_in_proj(ys, h, w_out[l].astype(BF16), ln_g[l], ln_b[l], w_p, b_p)
        else:
            h = _out_proj(ys, h, w_out[l].astype(BF16), ln_g[l], ln_b[l])
    return h.reshape(bsz, seq, d)
```

```python
import functools

import numpy as np
import jax
import jax.numpy as jnp
from jax import lax
from jax.experimental import pallas as pl
from jax.experimental.pallas import tpu as pltpu

F32 = jnp.float32
BF16 = jnp.bfloat16
Y_DTYPE = BF16

D_MODEL = 1024
DEPTH = 2
HEAD_DIM = 64
HALF = HEAD_DIM // 2
N_HEADS = 4
GROUP_W = N_HEADS * HEAD_DIM
N_GROUPS = 4
ROPE_THETA = 10000.0
NEG = -1e30
LN_EPS = 1e-5
A_WINDOW = 128
A_GROUP = 2
A_SUB = 16
V_AUG = HEAD_DIM + 16
LOG2E = 1.4426950408889634
PRO_ROWS = 512
B_CONV = 4
C_CHUNK = 64
C_SUB = 8
C_STEP = 1024
C_EPS = 1e-6
CMP_LEN = 32
CMP_STRIDE = 16
SLC_LEN = 64
N_SEL = 8
NSA_WINDOW = 512
FORCE_SCORE = 1e6
DN_ALPHA = (2.0 * DEPTH) ** 0.25
QBLK = 128
SCALE = HEAD_DIM ** -0.5

VMEM_LIMIT = 56 * 1024 * 1024
PROJ_TM = 512
PROJ_SUB = 256

LANES = 128

ORIG_SPLITS = (
    ('a_q', 256), ('a_k', 128), ('a_v', 128), ('a_z', 256),
    ('b_q', 256), ('b_k', 256), ('b_v', 256), ('b_if', 8), ('b_o', 256), ('b_z', 256),
    ('c_q', 256), ('c_f', 256), ('c_i', 256), ('c_z', 256),
    ('d_q', 256), ('d_kvc', 128), ('d_ksv', 128), ('d_kwv', 128), ('d_g', 12), ('d_z', 256),
)
N_COLS = sum(w for _, w in ORIG_SPLITS)
GATE_SLOTS = ('b_if', 'd_g')
NEW_LAYOUT = (
    ('a_q', 256), ('a_k', 128), ('a_v', 128), ('a_z', 256), ('b_q', 256), ('b_k', 256), ('b_v', 256),
    ('b_o', 256), ('b_z', 256), ('c_q', 256), ('c_f', 256), ('c_i', 256), ('c_z', 256),
    ('d_q', 256), ('d_kvc', 128), ('d_ksv', 128), ('d_kwv', 128), ('gates', 128), ('d_z', 256),
)
N_PROJ = sum(w for _, w in NEW_LAYOUT)
D_GATE_LANE = dict(ORIG_SPLITS)['b_if']


def _offset(layout, name):
    off = 0
    for n, w in layout:
        if n == name:
            return off
        off += w
    raise KeyError(name)


def _new_offset(name):
    return _offset(NEW_LAYOUT, name)


def _permuted_cols(load):
    sizes = dict(ORIG_SPLITS)
    for name, width in NEW_LAYOUT:
        new = _new_offset(name)
        if name != 'gates':
            old = _offset(ORIG_SPLITS, name)
            lo = old // LANES * LANES
            hi = min(-(-(old + width) // LANES) * LANES, N_COLS)
            yield new, load(lo, hi)[..., old - lo:old - lo + width]
            continue
        gates, lane0 = None, 0
        for slot in GATE_SLOTS:
            old = _offset(ORIG_SPLITS, slot)
            lo = old // LANES * LANES
            tile = load(lo, lo + LANES)
            if old - lo != lane0:
                tile = pltpu.roll(tile, (lane0 - (old - lo)) % LANES, axis=tile.ndim - 1)
            lane = _iota(tile.shape, tile.ndim - 1)
            part = jnp.where((lane >= lane0) & (lane < lane0 + sizes[slot]), tile, 0.0)
            gates = part if gates is None else gates + part
            lane0 += sizes[slot]
        yield new, gates


def _prep_body(w_ref, b_ref, wo_ref, bo_ref):
    for new, cols in _permuted_cols(lambda lo, hi: w_ref[0, :, lo:hi]):
        wo_ref[:, new:new + cols.shape[-1]] = cols.astype(BF16)
    for new, cols in _permuted_cols(lambda lo, hi: b_ref[0, :, lo:hi]):
        bo_ref[:, new:new + cols.shape[-1]] = cols


def _prep_in_weights(w_in, b_in, layer, rows=128):
    depth, d, n = w_in.shape
    return pl.pallas_call(
        _prep_body,
        grid=(d // rows,),
        in_specs=[pl.BlockSpec((1, rows, n), lambda r: (layer, r, 0)),
                  pl.BlockSpec((1, 1, n), lambda r: (layer, 0, 0))],
        out_specs=[pl.BlockSpec((rows, N_PROJ), lambda r: (r, 0)),
                   pl.BlockSpec((1, N_PROJ), lambda r: (0, 0))],
        out_shape=[jax.ShapeDtypeStruct((d, N_PROJ), BF16), jax.ShapeDtypeStruct((1, N_PROJ), F32)],
        compiler_params=_cparams(("arbitrary",)),
        name="prep_in_weights",
    )(w_in, b_in.reshape(depth, 1, n))


def _mm(a, b):
    return jnp.dot(a.astype(BF16), b.astype(BF16), preferred_element_type=F32)


def _mm_nt(a, b):
    return lax.dot_general(a.astype(BF16), b.astype(BF16), (((1,), (1,)), ((), ())),
                           preferred_element_type=F32)


def _mm_tn(a, b):
    return lax.dot_general(a.astype(BF16), b.astype(BF16), (((0,), (0,)), ((), ())),
                           preferred_element_type=F32)


def _dot_bf16(a, b):
    return jnp.dot(a, b, preferred_element_type=F32)


def _split_bf16(x, terms):
    out = []
    for _ in range(terms - 1):
        t = x.astype(BF16)
        out.append(t)
        x = x - t.astype(F32)
    out.append(x.astype(BF16))
    return out


def _mm_hi(a, b):
    ah, al = _split_bf16(a, 2)
    bh, bl = _split_bf16(b, 2)
    return _dot_bf16(ah, bh) + (_dot_bf16(ah, bl) + _dot_bf16(al, bh))


def _mm_sel(sel, x):
    sel = sel.astype(BF16)
    x1, x2, x3 = _split_bf16(x, 3)
    return _dot_bf16(sel, x1) + (_dot_bf16(sel, x2) + _dot_bf16(sel, x3))


def _sigmoid(x):
    return 1.0 / (1.0 + jnp.exp2(x * (-LOG2E)))


def _silu(x):
    return x * _sigmoid(x)


def _log_sigmoid(x):
    return jnp.minimum(x, 0.0) - jnp.log1p(jnp.exp(-jnp.abs(x)))


def _rope(x, cos, sin_up, sin_dn):
    w = x.shape[-1]
    up = pltpu.roll(x, w - HALF, axis=1)
    dn = pltpu.roll(x, HALF, axis=1)
    return x * cos + up * sin_up + dn * sin_dn


def _rope_bf16(x, cos, sin_up, sin_dn, perm):
    swapped = jnp.dot(x.astype(BF16), perm, preferred_element_type=F32)
    return x * cos + swapped * (sin_up + sin_dn)


def _rope_perm(width, rot_lanes):
    lane = np.arange(width)
    src = np.where(lane % HEAD_DIM < HALF, lane + HALF, lane - HALF)
    rotated = (lane % LANES) < rot_lanes
    perm = (np.arange(width)[:, None] == src[None, :]) & rotated[None, :]
    return jnp.asarray(perm.astype(np.float32), dtype=BF16)


def _rope_tables(pos, width, rot_lanes):
    inv = ROPE_THETA ** (-jnp.arange(HALF, dtype=F32) / HALF)
    ang = pos.astype(F32)[:, None] * inv[None, :]
    cos, sin = jnp.cos(ang), jnp.sin(ang)
    zero, one = jnp.zeros_like(sin), jnp.ones_like(cos)
    cos_h = jnp.concatenate([cos, cos], axis=1)
    up_h = jnp.concatenate([-sin, zero], axis=1)
    dn_h = jnp.concatenate([zero, sin], axis=1)
    id_c = jnp.concatenate([one, one], axis=1)
    id_s = jnp.concatenate([zero, zero], axis=1)
    cs, us, ds = [], [], []
    for l0 in range(0, width, HEAD_DIM):
        rot = (l0 % 128) < rot_lanes
        cs.append(cos_h if rot else id_c)
        us.append(up_h if rot else id_s)
        ds.append(dn_h if rot else id_s)
    return jnp.concatenate(cs, axis=1), jnp.concatenate(us, axis=1), jnp.concatenate(ds, axis=1)


def _iota(shape, dim):
    return lax.broadcasted_iota(jnp.int32, shape, dim)


def _cparams(sem):
    return pltpu.CompilerParams(dimension_semantics=sem, vmem_limit_bytes=VMEM_LIMIT)


def _ln_body(x_ref, g_ref, b_ref, o_ref):
    x = x_ref[...]
    mu = jnp.mean(x, axis=-1, keepdims=True)
    xc = x - mu
    var = jnp.mean(xc * xc, axis=-1, keepdims=True)
    o_ref[...] = xc * lax.rsqrt(var + LN_EPS) * g_ref[...] + b_ref[...]


def _layer_norm(x2, g, b, tm=512):
    m, d = x2.shape
    return pl.pallas_call(
        _ln_body,
        grid=(m // tm,),
        in_specs=[pl.BlockSpec((tm, d), lambda i: (i, 0)),
                  pl.BlockSpec((1, d), lambda i: (0, 0)),
                  pl.BlockSpec((1, d), lambda i: (0, 0))],
        out_specs=pl.BlockSpec((tm, d), lambda i: (i, 0)),
        out_shape=jax.ShapeDtypeStruct((m, d), F32),
        compiler_params=_cparams(("parallel",)),
        name="layer_norm",
    )(x2, g.reshape(1, d), b.reshape(1, d))


def _proj_body(x_ref, w_ref, b_ref, o_ref):
    o_ref[...] = jnp.dot(x_ref[...].astype(BF16), w_ref[...], preferred_element_type=F32) + b_ref[...]


def _in_proj(h2, w_bf, b, tm=256):
    m, d = h2.shape
    n = w_bf.shape[1]
    return pl.pallas_call(
        _proj_body,
        grid=(m // tm,),
        in_specs=[pl.BlockSpec((tm, d), lambda i: (i, 0)),
                  pl.BlockSpec((d, n), lambda i: (0, 0)),
                  pl.BlockSpec((1, n), lambda i: (0, 0))],
        out_specs=pl.BlockSpec((tm, n), lambda i: (i, 0)),
        out_shape=jax.ShapeDtypeStruct((m, n), F32),
        compiler_params=_cparams(("parallel",)),
        name="in_proj",
    )(h2, w_bf, b.reshape(1, n))


def _ln_rows(x, g, b):
    mu = jnp.mean(x, axis=-1, keepdims=True)
    xc = x - mu
    var = jnp.mean(xc * xc, axis=-1, keepdims=True)
    return xc * lax.rsqrt(var + LN_EPS) * g + b


def _mix_out(rows, ya_ref, yb_ref, yc_ref, yd_ref, h_ref, w_ref, g_ref, b_ref):
    y = _mm(ya_ref[rows, :], w_ref[0:GROUP_W, :])
    y += _mm(yb_ref[rows, :], w_ref[GROUP_W:2 * GROUP_W, :])
    y += _mm(yc_ref[rows, :], w_ref[2 * GROUP_W:3 * GROUP_W, :])
    y += _mm(yd_ref[rows, :], w_ref[3 * GROUP_W:4 * GROUP_W, :])
    return _ln_rows(DN_ALPHA * h_ref[rows, :] + y, g_ref[...], b_ref[...])


def _sub_tiles(ref):
    return [slice(r, r + PROJ_SUB) for r in range(0, ref.shape[0], PROJ_SUB)]


def _ln_proj_body(x_ref, g_ref, b_ref, w_ref, bias_ref, h_ref, o_ref):
    for rows in _sub_tiles(x_ref):
        h = _ln_rows(x_ref[rows, :], g_ref[...], b_ref[...])
        h_ref[rows, :] = h
        o_ref[rows, :] = jnp.dot(h.astype(BF16), w_ref[...], preferred_element_type=F32) + bias_ref[...]


def _out_proj_body(ya_ref, yb_ref, yc_ref, yd_ref, hp_ref, wo_ref, g_ref, b_ref, w_ref, bias_ref, h_ref, o_ref):
    for rows in _sub_tiles(hp_ref):
        h = _mix_out(rows, ya_ref, yb_ref, yc_ref, yd_ref, hp_ref, wo_ref, g_ref, b_ref)
        h_ref[rows, :] = h
        o_ref[rows, :] = jnp.dot(h.astype(BF16), w_ref[...], preferred_element_type=F32) + bias_ref[...]


def _ln_in_proj(x2, g, b, w_bf, bias, tm=PROJ_TM):
    m, d = x2.shape
    n = w_bf.shape[1]
    row = lambda width: pl.BlockSpec((tm, width), lambda i: (i, 0))
    const = lambda shape: pl.BlockSpec(shape, lambda i: (0, 0))
    return pl.pallas_call(
        _ln_proj_body,
        grid=(m // tm,),
        in_specs=[row(d), const((1, d)), const((1, d)), const((d, n)), const((1, n))],
        out_specs=[row(d), row(n)],
        out_shape=[jax.ShapeDtypeStruct((m, d), F32), jax.ShapeDtypeStruct((m, n), F32)],
        compiler_params=_cparams(("parallel",)),
        name="ln_in_proj",
    )(x2, g.reshape(1, d), b.reshape(1, d), w_bf, bias.reshape(1, n))


def _out_in_proj(ys, h2, wo_bf, g, b, w_bf, bias, tm=PROJ_TM):
    m, d = h2.shape
    n = w_bf.shape[1]
    row = lambda width: pl.BlockSpec((tm, width), lambda i: (i, 0))
    const = lambda shape: pl.BlockSpec(shape, lambda i: (0, 0))
    return pl.pallas_call(
        _out_proj_body,
        grid=(m // tm,),
        in_specs=[row(GROUP_W)] * N_GROUPS + [row(d), const((N_GROUPS * GROUP_W, d)), const((1, d)), const((1, d)),
                                              const((d, n)), const((1, n))],
        out_specs=[row(d), row(n)],
        out_shape=[jax.ShapeDtypeStruct((m, d), F32), jax.ShapeDtypeStruct((m, n), F32)],
        compiler_params=_cparams(("parallel",)),
        name="out_in_proj",
    )(*[y.reshape(m, GROUP_W) for y in ys], h2, wo_bf, g.reshape(1, d), b.reshape(1, d), w_bf, bias.reshape(1, n))


def _out_body(ya_ref, yb_ref, yc_ref, yd_ref, h_ref, w_ref, g_ref, b_ref, o_ref):
    o_ref[...] = _mix_out(slice(None), ya_ref, yb_ref, yc_ref, yd_ref, h_ref, w_ref, g_ref, b_ref)


def _out_proj(ys, h2, w_bf, g, b, tm=512):
    m, d = h2.shape
    yspec = pl.BlockSpec((tm, GROUP_W), lambda i: (i, 0))
    return pl.pallas_call(
        _out_body,
        grid=(m // tm,),
        in_specs=[yspec, yspec, yspec, yspec,
                  pl.BlockSpec((tm, d), lambda i: (i, 0)),
                  pl.BlockSpec((4 * GROUP_W, d), lambda i: (0, 0)),
                  pl.BlockSpec((1, d), lambda i: (0, 0)),
                  pl.BlockSpec((1, d), lambda i: (0, 0))],
        out_specs=pl.BlockSpec((tm, d), lambda i: (i, 0)),
        out_shape=jax.ShapeDtypeStruct((m, d), F32),
        compiler_params=_cparams(("parallel",)),
        name="out_proj",
    )(*[y.reshape(m, GROUP_W) for y in ys], h2, w_bf, g.reshape(1, d), b.reshape(1, d))


def _mixer_a_body(sink_ref, q_ref, kv_ref, z_ref, cos_ref, up_ref, dn_ref, perm_ref, o_ref, k_ref, vt_ref):
    i = pl.program_id(1)
    seq = kv_ref.shape[1]
    n_kv = N_HEADS // A_GROUP
    low_half = _iota((1, LANES), 1) < HEAD_DIM

    @pl.when(i == 0)
    def _():
        def rope_rows(c, carry):
            for u in range(PRO_ROWS // QBLK):
                t0 = c * (PRO_ROWS // QBLK) + u
                rows = pl.ds(pl.multiple_of(t0 * QBLK, QBLK), QBLK)
                kr = _rope_bf16(kv_ref[0, rows, 0:LANES], cos_ref[rows, 0:LANES], up_ref[rows, 0:LANES],
                                dn_ref[rows, 0:LANES], perm_ref[0:LANES, 0:LANES])
                v_t = _mm_nt(_iota((LANES, LANES), 0) == _iota((LANES, LANES), 1), kv_ref[0, rows, 128:256])
                v_t = v_t.astype(BF16)
                ones_rows = jnp.ones((V_AUG - HEAD_DIM, QBLK), BF16)
                other = pltpu.roll(kr, HEAD_DIM, axis=1)
                for g in range(n_kv):
                    dup = jnp.where(low_half, kr, other) if g == 0 else jnp.where(low_half, other, kr)
                    k_ref[g, rows, :] = dup.astype(BF16)
                    vt_ref[g, t0] = jnp.concatenate([v_t[g * HEAD_DIM:(g + 1) * HEAD_DIM, :], ones_rows], axis=0)
            return carry
        lax.fori_loop(0, seq // PRO_ROWS, rope_rows, 0)

    n_blk = A_WINDOW // QBLK + 1
    span = n_blk * QBLK
    first_head = _iota((1, A_GROUP * QBLK), 1) < QBLK
    eye = (_iota((QBLK, QBLK), 0) == _iota((QBLK, QBLK), 1)).astype(BF16)

    def query_block(u):
        blk_i = i * A_SUB + u
        sub = slice(u * QBLK, (u + 1) * QBLK)
        q0 = pl.multiple_of(blk_i * QBLK, QBLK)
        qrows = pl.ds(q0, QBLK)
        qr = _rope(q_ref[0, sub, :], cos_ref[qrows, :], up_ref[qrows, :], dn_ref[qrows, :]) * (SCALE * LOG2E)
        kt0 = jnp.maximum(blk_i - A_WINDOW // QBLK, 0)
        k0 = pl.multiple_of(kt0 * QBLK, QBLK)
        rel = (q0 + _iota((1, QBLK), 1)) - (k0 + _iota((span, 1), 0))
        in_band = lax.bitcast_convert_type(rel, jnp.uint32) < A_WINDOW
        bias = jnp.concatenate([jnp.where(in_band, 0.0, NEG)] * A_GROUP, axis=1)
        outs = []
        for g in range(n_kv):
            q_pair = qr[:, g * LANES:(g + 1) * LANES]
            qg = jnp.concatenate([jnp.where(low_half, q_pair, 0.0), jnp.where(low_half, 0.0, q_pair)], axis=0)
            sink = jnp.where(first_head, sink_ref[g * A_GROUP], sink_ref[g * A_GROUP + 1]) * LOG2E
            s = _mm_nt(k_ref[g, pl.ds(k0, span), :], qg) + bias
            m = jnp.maximum(jnp.max(s, axis=0, keepdims=True), sink)
            p = jnp.exp2(s - m).astype(BF16)
            o = jnp.zeros((V_AUG, A_GROUP * QBLK), F32)
            for c in range(n_blk):
                o = o + jnp.dot(vt_ref[g, kt0 + c], p[c * QBLK:(c + 1) * QBLK], preferred_element_type=F32)
            o = o[0:HEAD_DIM] / (o[HEAD_DIM:HEAD_DIM + 1] + jnp.exp2(sink - m))
            outs.extend(o[:, r * QBLK:(r + 1) * QBLK] for r in range(A_GROUP))
        y = _mm_nt(eye, jnp.concatenate(outs, axis=0))
        o_ref[0, sub, :] = (y * _silu(z_ref[0, sub, :])).astype(Y_DTYPE)

    for u in range(A_SUB):
        query_block(u)


def _mixer_a(proj, sinks, tabs):
    bsz, seq, _ = proj.shape
    cos, up, dn = tabs
    blk = lambda name: _new_offset(name) // GROUP_W
    tspec = pl.BlockSpec((seq, GROUP_W), lambda b, i: (0, 0))
    return pl.pallas_call(
        _mixer_a_body,
        grid=(bsz, seq // (A_SUB * QBLK)),
        in_specs=[pl.BlockSpec(memory_space=pltpu.SMEM),
                  pl.BlockSpec((1, A_SUB * QBLK, GROUP_W), lambda b, i: (b, i, blk('a_q'))),
                  pl.BlockSpec((1, seq, GROUP_W), lambda b, i: (b, 0, blk('a_k'))),
                  pl.BlockSpec((1, A_SUB * QBLK, GROUP_W), lambda b, i: (b, i, blk('a_z'))),
                  tspec, tspec, tspec, pl.BlockSpec((GROUP_W, GROUP_W), lambda b, i: (0, 0))],
        out_specs=pl.BlockSpec((1, A_SUB * QBLK, GROUP_W), lambda b, i: (b, i, 0)),
        out_shape=jax.ShapeDtypeStruct((bsz, seq, GROUP_W), Y_DTYPE),
        scratch_shapes=[pltpu.VMEM((N_HEADS // A_GROUP, seq, LANES), BF16),
                        pltpu.VMEM((N_HEADS // A_GROUP, seq // QBLK, V_AUG, QBLK), BF16)],
        compiler_params=_cparams(("arbitrary", "arbitrary")),
        name="mixer_a",
    )(sinks, proj, proj, proj, cos, up, dn, _rope_perm(GROUP_W, LANES))


B_CHUNK = 128
B_STEP = 2048
B_AUG = HEAD_DIM + 16


def _conv_silu(x_ref, r0, prev, w, b):
    L = B_CHUNK
    x = x_ref[0, r0:r0 + L, :]
    acc = b + w[B_CONV - 1:B_CONV, :] * x
    for s in range(1, B_CONV):
        if r0 >= s:
            shifted = x_ref[0, r0 - s:r0 - s + L, :]
        else:
            row8 = _iota((8, x.shape[1]), 0)
            xs = pltpu.roll(x, s, axis=0)
            ps = pltpu.roll(prev, s, axis=0)
            shifted = jnp.concatenate([jnp.where(row8 < s, ps, xs[0:8]), xs[8:]], axis=0)
        acc = acc + w[B_CONV - 1 - s:B_CONV - s, :] * shifted
    return _silu(acc)


def _mixer_b_body(xq_ref, xk_ref, v_ref, og_ref, z_ref, if_ref, cw_ref, cb_ref, o_ref, xprev_ref, c_ref, m_ref):
    c = pl.program_id(1)
    L = B_CHUNK

    @pl.when(c == 0)
    def _():
        xprev_ref[...] = jnp.zeros_like(xprev_ref)
        c_ref[...] = jnp.zeros_like(c_ref)
        m_ref[...] = jnp.zeros_like(m_ref)

    tri = (_iota((L, L), 1) <= _iota((L, L), 0)).astype(F32)
    key_first = _iota((L, L), 0) <= _iota((L, L), 1)
    half_of_lane = _iota((1, 128), 1) // HEAD_DIM
    ones_rows = jnp.ones((B_AUG - HEAD_DIM, L), F32)

    def chunk(r0, state):
        rows = slice(r0, r0 + L)
        qc = _conv_silu(xq_ref, r0, xprev_ref[:, 0:GROUP_W], cw_ref[:, 0:GROUP_W], cb_ref[:, 0:GROUP_W]) * SCALE
        kc = _conv_silu(xk_ref, r0, xprev_ref[:, GROUP_W:], cw_ref[:, GROUP_W:], cb_ref[:, GROUP_W:])
        q_t = qc.T
        v_t = v_ref[0, rows, :].T

        gates = if_ref[0, rows, :]
        bcum = _mm_sel(tri, _log_sigmoid(gates))
        gates_t = gates.T
        bcum_t = bcum.T
        outs, new_state = [], []
        for h in range(N_HEADS):
            pair = slice((h // 2) * 128, (h // 2 + 1) * 128)
            own = half_of_lane == (h % 2)
            k_pair = kc[:, pair]
            q_pair_t = q_t[pair, :]
            i_row = gates_t[h:h + 1, :]
            b_row = bcum_t[N_HEADS + h:N_HEADS + h + 1, :]
            c_col = gates[:, h:h + 1] - bcum[:, N_HEADS + h:N_HEADS + h + 1]
            cst, m_prev = state[h]
            dmat = jnp.where(key_first, b_row + c_col, NEG)
            inter = b_row + m_prev
            m_t = jnp.maximum(jnp.max(dmat, axis=0, keepdims=True), inter)
            smat = _mm(jnp.where(own, k_pair, 0.0), q_pair_t) * jnp.exp(dmat - m_t)
            vaug_t = jnp.concatenate([v_t[h * HEAD_DIM:(h + 1) * HEAD_DIM, :], ones_rows], axis=0)
            tot = _mm(vaug_t, smat) + jnp.exp(inter - m_t) * _mm(cst, q_pair_t)
            num = tot[0:HEAD_DIM, :]
            den = tot[HEAD_DIM:HEAD_DIM + 1, :]
            outs.append(num / jnp.maximum(jnp.abs(den), jnp.exp(-m_t)))
            b_last = b_row[:, L - 1:L]
            d_row = b_last - b_row + i_row
            m_new = jnp.maximum(b_last + m_prev, jnp.max(d_row, axis=1, keepdims=True))
            w_row = jnp.exp(d_row - m_new)
            decay = jnp.exp(b_last + m_prev - m_new)
            new_state.append((decay * cst + jnp.where(own, _mm(vaug_t * w_row, k_pair), 0.0), m_new))
        o_ref[0, rows, :] = (_sigmoid(og_ref[0, rows, :]) * jnp.concatenate(outs, axis=0).T
                             * _silu(z_ref[0, rows, :])).astype(Y_DTYPE)
        return new_state

    rows_step = xq_ref.shape[1]
    state = [(c_ref[h], m_ref[h:h + 1, 0:1]) for h in range(N_HEADS)]
    for u in range(rows_step // L):
        state = chunk(u * L, state)
    xprev_ref[:, 0:GROUP_W] = xq_ref[0, rows_step - 8:rows_step, :]
    xprev_ref[:, GROUP_W:] = xk_ref[0, rows_step - 8:rows_step, :]
    for h in range(N_HEADS):
        c_ref[h] = state[h][0]
        m_ref[h:h + 1, :] = jnp.broadcast_to(state[h][1], (1, 128))


def _mixer_b(proj, conv_w, conv_b):
    bsz, seq, _ = proj.shape
    L = B_STEP
    blk = lambda name: _new_offset(name) // GROUP_W
    return pl.pallas_call(
        _mixer_b_body,
        grid=(bsz, seq // L),
        in_specs=[pl.BlockSpec((1, L, GROUP_W), lambda b, c: (b, c, blk('b_q'))),
                  pl.BlockSpec((1, L, GROUP_W), lambda b, c: (b, c, blk('b_k'))),
                  pl.BlockSpec((1, L, GROUP_W), lambda b, c: (b, c, blk('b_v'))),
                  pl.BlockSpec((1, L, GROUP_W), lambda b, c: (b, c, blk('b_o'))),
                  pl.BlockSpec((1, L, GROUP_W), lambda b, c: (b, c, blk('b_z'))),
                  pl.BlockSpec((1, L, LANES), lambda b, c: (b, c, _new_offset('gates') // LANES)),
                  pl.BlockSpec((B_CONV, 2 * GROUP_W), lambda b, c: (0, 0)),
                  pl.BlockSpec((1, 2 * GROUP_W), lambda b, c: (0, 0))],
        out_specs=pl.BlockSpec((1, L, GROUP_W), lambda b, c: (b, c, 0)),
        out_shape=jax.ShapeDtypeStruct((bsz, seq, GROUP_W), Y_DTYPE),
        scratch_shapes=[pltpu.VMEM((8, 2 * GROUP_W), F32),
                        pltpu.VMEM((N_HEADS, B_AUG, 128), F32),
                        pltpu.VMEM((8, 128), F32)],
        compiler_params=_cparams(("arbitrary", "arbitrary")),
        name="mixer_b",
    )(proj, proj, proj, proj, proj, proj, conv_w, conv_b.reshape(1, -1))


def _mixer_c_body(layer, q_ref, f_ref, i_ref, z_ref, lb_ref, gn_ref, tri_ref, piv_ref, o_ref, st_ref):
    c = pl.program_id(1)
    L = C_CHUNK
    nsub = L // C_SUB

    @pl.when(c == 0)
    def _():
        st_ref[...] = jnp.zeros_like(st_ref)

    raw = lb_ref[...]
    ex = jnp.exp(raw - jnp.max(raw, axis=0, keepdims=True))
    if layer == 0:
        lb = jnp.zeros((1, GROUP_W), F32)
    else:
        lb = jnp.sum(ex[1:layer + 1], axis=0, keepdims=True) / jnp.sum(ex, axis=0, keepdims=True)
    tri = (_iota((L, L), 1) <= _iota((L, L), 0)).astype(F32)
    head_of_lane = _iota((1, GROUP_W), 1) // HEAD_DIM
    same_head = ((_iota((GROUP_W, GROUP_W), 0) // HEAD_DIM)
                 == (_iota((GROUP_W, GROUP_W), 1) // HEAD_DIM))
    ones_bd = same_head.astype(BF16)

    def chunk(rows, st):
        f = lb + (1.0 - lb) * _sigmoid(f_ref[0, rows, :])
        logf = jnp.log(f)
        k = 1.0 - f
        q = _silu(q_ref[0, rows, :])
        v = i_ref[0, rows, :]
        a = _mm_sel(tri, logf) * LOG2E
        o = _mm_nt(q * jnp.exp2(a), st)

        qsts, kts, vreps = [], [], []
        for i in range(1, nsub):
            r0 = i * C_SUB
            piv = a[r0 - 1:r0, :]
            qt = q[r0:r0 + C_SUB] * jnp.exp2(a[r0:r0 + C_SUB] - piv)
            qsts.extend(jnp.where(head_of_lane == h, qt, 0.0) for h in range(N_HEADS))
            kts.append((k[0:r0] * jnp.exp2(piv - a[0:r0])).astype(BF16))
            vreps.append(v[0:r0].astype(BF16))
        scores = _mm_nt(jnp.concatenate(qsts, axis=0), jnp.concatenate(kts, axis=0))
        r = _dot_bf16((scores * piv_ref[...]).astype(BF16), jnp.concatenate(vreps, axis=0))
        pieces = [o[0:C_SUB]]
        for i in range(1, nsub):
            acc = o[i * C_SUB:(i + 1) * C_SUB]
            for h in range(N_HEADS):
                g0 = ((i - 1) * N_HEADS + h) * C_SUB
                acc = acc + jnp.where(head_of_lane == h, r[g0:g0 + C_SUB], 0.0)
            pieces.append(acc)
        o = jnp.concatenate(pieces, axis=0)

        a3 = a.reshape(nsub, C_SUB, GROUP_W)
        q3 = q.reshape(nsub, C_SUB, GROUP_W)
        k3 = k.reshape(nsub, C_SUB, GROUP_W)
        v3 = v.reshape(nsub, C_SUB, GROUP_W)
        ps = []
        for s in range(C_SUB):
            e = jnp.exp2(a3 - a3[:, s:s + 1, :] + tri_ref[s])
            ps.append((q3 * k3[:, s:s + 1, :] * e).reshape(L, GROUP_W).astype(BF16))
        w = jnp.dot(jnp.concatenate(ps, axis=0), ones_bd, preferred_element_type=F32)
        for s in range(C_SUB):
            vs = jnp.broadcast_to(v3[:, s:s + 1, :], (nsub, C_SUB, GROUP_W)).reshape(L, GROUP_W)
            o = o + w[s * L:(s + 1) * L] * vs

        a_last = a[L - 1:L, :]
        kdec = k * jnp.exp2(a_last - a)
        st_new = st * jnp.exp2(a_last) + jnp.where(same_head, _mm_tn(v, kdec), 0.0)

        ms = jnp.dot((o * o).astype(BF16), ones_bd, preferred_element_type=F32) * (1.0 / HEAD_DIM)
        o = o * lax.rsqrt(ms + C_EPS) * gn_ref[...]
        o_ref[0, rows, :] = (o * _silu(z_ref[0, rows, :])).astype(Y_DTYPE)
        return st_new

    st = st_ref[...]
    for u in range(q_ref.shape[1] // L):
        st = chunk(slice(u * L, (u + 1) * L), st)
    st_ref[...] = st


def _mixer_c(proj, c_lb, norm_g, layer):
    bsz, seq, _ = proj.shape
    L = C_STEP
    blk = lambda name: _new_offset(name) // GROUP_W
    spec = lambda name: pl.BlockSpec((1, L, GROUP_W), lambda b, c: (b, c, blk(name)))
    vec = pl.BlockSpec((1, GROUP_W), lambda b, c: (0, 0))
    t_ge_s = np.arange(C_SUB)[None, :, None] >= np.arange(C_SUB)[:, None, None]
    causal_bias = np.broadcast_to(np.where(t_ge_s, 0.0, NEG), (C_SUB, C_SUB, GROUP_W)).astype(np.float32)
    pivots = np.arange(1, C_CHUNK // C_SUB)
    row_piv = np.repeat(pivots, N_HEADS * C_SUB)
    col_piv = np.repeat(pivots, pivots * C_SUB)
    same_pivot = (row_piv[:, None] == col_piv[None, :]).astype(np.float32)
    return pl.pallas_call(
        functools.partial(_mixer_c_body, layer),
        grid=(bsz, seq // L),
        in_specs=[spec('c_q'), spec('c_f'), spec('c_i'), spec('c_z'),
                  pl.BlockSpec((DEPTH, GROUP_W), lambda b, c: (0, 0)), vec,
                  pl.BlockSpec((C_SUB, C_SUB, GROUP_W), lambda b, c: (0, 0, 0)),
                  pl.BlockSpec(same_pivot.shape, lambda b, c: (0, 0))],
        out_specs=pl.BlockSpec((1, L, GROUP_W), lambda b, c: (b, c, 0)),
        out_shape=jax.ShapeDtypeStruct((bsz, seq, GROUP_W), Y_DTYPE),
        scratch_shapes=[pltpu.VMEM((GROUP_W, GROUP_W), F32)],
        compiler_params=_cparams(("arbitrary", "arbitrary")),
        name="mixer_c",
    )(proj, proj, proj, proj, c_lb, norm_g.reshape(1, -1), jnp.asarray(causal_bias), jnp.asarray(same_pivot))


N_CMP_PAD = 128
KT = 512
D_QBLK = 256
PE_ROWS = 16
SEL_ROWS = 16


def _mixer_d_body(q_ref, z_ref, ksv_ref, kwv_ref, cin_ref, g_ref, cos_ref, up_ref, dn_ref, kcos_ref, kup_ref,
                  kdn_ref, kperm_ref, ccos_ref, cup_ref, cdn_ref, w1_ref, pe_ref, w2_ref, ov_ref, o_ref,
                  ks_ref, kw_ref, vts_ref, vtw_ref, kc_ref, vct_ref, sel_ref, qaug_ref):
    QBLK = D_QBLK
    i = pl.program_id(1)
    seq = ksv_ref.shape[1]
    half_cmp = CMP_LEN // 2

    @pl.when(i == 0)
    def _():
        def rope_rows(c, carry):
            for u in range(PRO_ROWS // KT):
                t0 = c * (PRO_ROWS // KT) + u
                rows = pl.ds(pl.multiple_of(t0 * KT, KT), KT)
                kvs_in = ksv_ref[0, rows, :]
                kvw_in = kwv_ref[0, rows, :]
                kvs = _rope_bf16(kvs_in, kcos_ref[rows, :], kup_ref[rows, :], kdn_ref[rows, :], kperm_ref[...])
                kvw = _rope_bf16(kvw_in, kcos_ref[rows, :], kup_ref[rows, :], kdn_ref[rows, :], kperm_ref[...])
                blk_in_tile = _iota((KT, HEAD_DIM), 0) // SLC_LEN
                onehot = jnp.where(blk_in_tile == _iota((KT, HEAD_DIM), 1), 1.0, 0.0)
                ks_ref[rows, :] = jnp.concatenate([kvs[:, 0:HEAD_DIM], onehot], axis=1).astype(BF16)
                kw_ref[rows, :] = kvw[:, 0:HEAD_DIM].astype(BF16)
                ones_rows = jnp.ones((V_AUG - HEAD_DIM, KT), BF16)
                pick_v = _iota((HEAD_DIM, LANES), 1) == _iota((HEAD_DIM, LANES), 0) + HEAD_DIM
                vts_ref[t0] = jnp.concatenate([_mm_nt(pick_v, kvs_in).astype(BF16), ones_rows], axis=0)
                vwt = jnp.concatenate([_mm_nt(pick_v, kvw_in).astype(BF16), ones_rows], axis=0)
                for w in range(KT // QBLK):
                    vtw_ref[t0 * (KT // QBLK) + w] = vwt[:, w * QBLK:(w + 1) * QBLK]
            return carry
        lax.fori_loop(0, seq // PRO_ROWS, rope_rows, 0)
        acc = jnp.zeros((N_CMP_PAD + PE_ROWS, 2 * LANES), F32)
        for rr in range(half_cmp // 2):
            xa = cin_ref[0, pl.ds(2 * rr, N_CMP_PAD, stride=CMP_STRIDE), :]
            xb = cin_ref[0, pl.ds(2 * rr + 1, N_CMP_PAD, stride=CMP_STRIDE), :]
            lhs = jnp.concatenate([jnp.concatenate([xa, xb], axis=1), pe_ref[rr]], axis=0)
            acc = acc + _mm_hi(lhs, w1_ref[rr])
        u0 = acc[0:N_CMP_PAD, 0:LANES] + acc[N_CMP_PAD:N_CMP_PAD + 1, 0:LANES]
        u1 = acc[0:N_CMP_PAD, LANES:] + acc[N_CMP_PAD + 1:N_CMP_PAD + 2, LANES:]
        pre = u0 + pltpu.roll(u1, N_CMP_PAD - 1, axis=0)
        cv = _rope(_mm_hi(_silu(pre), w2_ref[...]), ccos_ref[...], cup_ref[...], cdn_ref[...])
        kc_ref[...] = cv[:, 0:HEAD_DIM]
        vct_ref[...] = cv.T[HEAD_DIM:2 * HEAD_DIM, :].astype(BF16)

    q0 = pl.multiple_of(i * QBLK, QBLK)
    qrows = pl.ds(q0, QBLK)
    qr = _rope(q_ref[0], cos_ref[qrows, :], up_ref[qrows, :], dn_ref[qrows, :]) * (SCALE * LOG2E)
    qt = qr.T
    qst = jnp.concatenate([qt[h * HEAD_DIM:(h + 1) * HEAD_DIM] for h in range(N_HEADS)], axis=1)
    qst_bf = qst.astype(BF16)
    cols4 = N_HEADS * QBLK
    qpos = q0 + _iota((1, QBLK), 1)

    def lanes4(x):
        return jnp.concatenate([x] * N_HEADS, axis=1)

    sc = _mm_hi(kc_ref[...], qst)
    ends = _iota((N_CMP_PAD, 1), 0) * CMP_STRIDE + (CMP_LEN - 1)
    scm = sc + lanes4(jnp.where(ends <= qpos, 0.0, NEG))
    e = jnp.exp2(scm - jnp.max(scm, axis=0, keepdims=True))
    l = jnp.sum(e, axis=0, keepdims=True)
    any_valid = lanes4(jnp.where(qpos >= CMP_LEN - 1, 1.0, 0.0))
    pc = e * (any_valid / l)
    o_cmp = jnp.dot(vct_ref[...], pc.astype(BF16), preferred_element_type=F32)
    psum = pc[:, 0:QBLK] + pc[:, QBLK:2 * QBLK] + pc[:, 2 * QBLK:3 * QBLK] + pc[:, 3 * QBLK:4 * QBLK]

    n_slc = seq // SLC_LEN
    imp = _mm_sel(ov_ref[...], psum)
    jl = _iota((n_slc, 1), 0)
    cur = jnp.right_shift(qpos, SLC_LEN.bit_length() - 1)
    imp = jnp.where(jl == 0, FORCE_SCORE, imp)
    imp = jnp.where(jl == cur, FORCE_SCORE, imp)
    imp = jnp.where(jl == cur - 1, FORCE_SCORE, imp)
    imp = jnp.where(jl <= cur, imp, NEG)
    rank = jnp.zeros((n_slc, QBLK), F32)
    for jp in range(n_slc):
        cj = imp[jp:jp + 1, :]
        tie = jnp.where(jl > jp, 1.0, 0.0)
        rank = rank + jnp.where(cj > imp, 1.0, jnp.where(cj == imp, tie, 0.0))
    sel_bias = jnp.where(rank < N_SEL, 0.0, NEG)
    blocks_per_tile = KT // SLC_LEN
    for t in range(n_slc // blocks_per_tile):
        sel_ref[t] = sel_bias[t * blocks_per_tile:(t + 1) * blocks_per_tile, :]

    n_win = NSA_WINDOW // QBLK + 1
    span = n_win * QBLK
    kt0 = jnp.maximum(i - NSA_WINDOW // QBLK, 0)
    k0 = pl.multiple_of(kt0 * QBLK, QBLK)
    rel = qpos - (k0 + _iota((span, 1), 0))
    in_band = lax.bitcast_convert_type(rel, jnp.uint32) < NSA_WINDOW
    s = jnp.dot(kw_ref[pl.ds(k0, span), :], qst_bf, preferred_element_type=F32)
    s = s + lanes4(jnp.where(in_band, 0.0, NEG))
    p = jnp.exp2(s - jnp.max(s, axis=0, keepdims=True)).astype(BF16)
    o_win = jnp.zeros((V_AUG, cols4), F32)
    for c in range(n_win):
        o_win = o_win + jnp.dot(vtw_ref[kt0 + c], p[c * QBLK:(c + 1) * QBLK], preferred_element_type=F32)
    o_win = o_win[0:HEAD_DIM] / o_win[HEAD_DIM:HEAD_DIM + 1]

    qaug_ref[0:HEAD_DIM, :] = qst_bf
    qaug_ref[HEAD_DIM:2 * HEAD_DIM, :] = jnp.zeros((HEAD_DIM, cols4), BF16)

    def sel_scores(kt):
        picks = jnp.concatenate([sel_ref[kt], jnp.zeros((SEL_ROWS - blocks_per_tile, QBLK), F32)], axis=0)
        qaug_ref[HEAD_DIM:HEAD_DIM + SEL_ROWS, :] = lanes4(picks).astype(BF16)
        kbase = pl.multiple_of(kt * KT, KT)
        return jnp.dot(ks_ref[pl.ds(kbase, KT), :], qaug_ref[...], preferred_element_type=F32)

    def sel_update(kt, s, m_old, acc):
        m_new = jnp.maximum(m_old, jnp.max(s, axis=0, keepdims=True))
        p = jnp.exp2(s - m_new)
        acc_new = jnp.exp2(m_old - m_new) * acc + jnp.dot(vts_ref[kt], p.astype(BF16), preferred_element_type=F32)
        return m_new, acc_new

    init = (jnp.full((1, cols4), NEG, F32), jnp.zeros((V_AUG, cols4), F32))
    last = (i + KT // QBLK) // (KT // QBLK) - 1
    m_s, acc_s = lax.fori_loop(0, last, lambda kt, c: sel_update(kt, sel_scores(kt), *c), init)
    causal = jnp.where((last * KT + _iota((KT, 1), 0)) <= qpos, 0.0, NEG)
    _, acc_s = sel_update(last, sel_scores(last) + lanes4(causal), m_s, acc_s)
    o_slc = acc_s[0:HEAD_DIM] / acc_s[HEAD_DIM:HEAD_DIM + 1]

    gate = _sigmoid(g_ref[0]).T
    outs = []
    for h in range(N_HEADS):
        cs = slice(h * QBLK, (h + 1) * QBLK)
        g0 = D_GATE_LANE + 3 * h
        outs.append(gate[g0:g0 + 1, :] * o_cmp[:, cs] + gate[g0 + 1:g0 + 2, :] * o_slc[:, cs]
                    + gate[g0 + 2:g0 + 3, :] * o_win[:, cs])
    o_ref[0] = (jnp.concatenate(outs, axis=0).T * _silu(z_ref[0])).astype(Y_DTYPE)


def _mixer_d(proj, qtabs, ktabs, ctabs, w1bd, pe2, w2bd, overlap):
    QBLK = D_QBLK
    bsz, seq, _ = proj.shape
    blk = lambda name: _new_offset(name) // GROUP_W
    full = lambda shape: pl.BlockSpec(shape, lambda b, i: (0,) * len(shape))
    return pl.pallas_call(
        _mixer_d_body,
        grid=(bsz, seq // QBLK),
        in_specs=[pl.BlockSpec((1, QBLK, GROUP_W), lambda b, i: (b, i, blk('d_q'))),
                  pl.BlockSpec((1, QBLK, GROUP_W), lambda b, i: (b, i, blk('d_z'))),
                  pl.BlockSpec((1, seq, 128), lambda b, i: (b, 0, _new_offset('d_ksv') // 128)),
                  pl.BlockSpec((1, seq, 128), lambda b, i: (b, 0, _new_offset('d_kwv') // 128)),
                  pl.BlockSpec((1, seq, 128), lambda b, i: (b, 0, _new_offset('d_kvc') // 128)),
                  pl.BlockSpec((1, QBLK, LANES), lambda b, i: (b, i, _new_offset('gates') // LANES)),
                  full((seq, GROUP_W)), full((seq, GROUP_W)), full((seq, GROUP_W)),
                  full((seq, 128)), full((seq, 128)), full((seq, 128)), full((LANES, LANES)),
                  full((N_CMP_PAD, 128)), full((N_CMP_PAD, 128)), full((N_CMP_PAD, 128)),
                  full((CMP_LEN // 4, 2 * LANES, 2 * LANES)), full((CMP_LEN // 4, PE_ROWS, 2 * LANES)), full((128, 128)),
                  full((seq // SLC_LEN, N_CMP_PAD))],
        out_specs=pl.BlockSpec((1, QBLK, GROUP_W), lambda b, i: (b, i, 0)),
        out_shape=jax.ShapeDtypeStruct((bsz, seq, GROUP_W), Y_DTYPE),
        scratch_shapes=[pltpu.VMEM((seq, 2 * HEAD_DIM), BF16),
                        pltpu.VMEM((seq, HEAD_DIM), BF16),
                        pltpu.VMEM((seq // KT, V_AUG, KT), BF16),
                        pltpu.VMEM((seq // QBLK, V_AUG, QBLK), BF16),
                        pltpu.VMEM((N_CMP_PAD, HEAD_DIM), F32),
                        pltpu.VMEM((HEAD_DIM, N_CMP_PAD), BF16),
                        pltpu.VMEM((seq // KT, KT // SLC_LEN, QBLK), F32),
                        pltpu.VMEM((2 * HEAD_DIM, N_HEADS * QBLK), BF16)],
        compiler_params=_cparams(("arbitrary", "arbitrary")),
        name="mixer_d",
    )(proj, proj, proj, proj, proj, proj, *qtabs, *ktabs, _rope_perm(LANES, HEAD_DIM), *ctabs, w1bd, pe2, w2bd,
      overlap)


def _cmp_params(pe, w1, w2):
    w1 = w1.reshape(2, CMP_LEN, HEAD_DIM, HEAD_DIM)
    z = jnp.zeros((CMP_LEN, HEAD_DIM, HEAD_DIM), F32)
    w1bd = jnp.concatenate([jnp.concatenate([w1[0], z], axis=2), jnp.concatenate([z, w1[1]], axis=2)], axis=1)
    z2 = jnp.zeros((HEAD_DIM, HEAD_DIM), F32)
    w2bd = jnp.concatenate([jnp.concatenate([w2[0], z2], axis=1), jnp.concatenate([z2, w2[1]], axis=1)], axis=0)
    pe2 = jnp.concatenate([pe[0], pe[1]], axis=1)
    half = CMP_LEN // 2
    w1pair = jnp.stack([jnp.concatenate([jnp.concatenate([w1bd[r], w1bd[half + r]], axis=1),
                                         jnp.concatenate([w1bd[r + 1], w1bd[half + r + 1]], axis=1)], axis=0)
                        for r in range(0, half, 2)])
    pe_rows = jnp.stack([jnp.concatenate([jnp.concatenate([pe2[r], pe2[r + 1]])[None, :],
                                          jnp.concatenate([pe2[half + r], pe2[half + r + 1]])[None, :],
                                          jnp.zeros((PE_ROWS - 2, 2 * LANES), F32)], axis=0)
                         for r in range(0, half, 2)])
    return w1pair, pe_rows, w2bd


def _overlap_matrix(seq):
    n_cmp = (seq - CMP_LEN) // CMP_STRIDE + 1
    starts = np.arange(N_CMP_PAD) * CMP_STRIDE
    blk = np.arange(seq // SLC_LEN)
    ov = ((starts[None, :] < (blk[:, None] + 1) * SLC_LEN) & (starts[None, :] + CMP_LEN > blk[:, None] * SLC_LEN))
    ov = ov & (np.arange(N_CMP_PAD)[None, :] < n_cmp)
    return jnp.asarray(ov.astype(np.float32))


def kernel(x, ln0_g, ln0_b, w_in, b_in, a_sinks, b_conv_w, b_conv_b, c_lb, c_norm_g, d_cmp_pe, d_cmp_w1, d_cmp_w2,
           w_out, ln_g, ln_b):
    bsz, seq, d = x.shape
    pos = jnp.arange(seq)
    qtabs = _rope_tables(pos, GROUP_W, 128)
    ktabs = _rope_tables(pos, 128, HEAD_DIM)
    ctabs = _rope_tables(jnp.arange(N_CMP_PAD) * CMP_STRIDE + (CMP_LEN - 1), 128, HEAD_DIM)
    overlap = _overlap_matrix(seq)

    w_p, b_p = _prep_in_weights(w_in, b_in, 0)
    h, proj = _ln_in_proj(x.reshape(bsz * seq, d), ln0_g, ln0_b, w_p, b_p)
    for l in range(DEPTH):
        proj = proj.reshape(bsz, seq, N_PROJ)
        y_a = _mixer_a(proj, a_sinks[l], qtabs)
        y_b = _mixer_b(proj, b_conv_w[l], b_conv_b[l])
        y_c = _mixer_c(proj, c_lb, c_norm_g[l], l)
        w1bd, pe2, w2bd = _cmp_params(d_cmp_pe[l], d_cmp_w1[l], d_cmp_w2[l])
        y_d = _mixer_d(proj, qtabs, ktabs, ctabs, w1bd, pe2, w2bd, overlap)
        ys = (y_a, y_b, y_c, y_d)
        if l + 1 < DEPTH:
            w_p, b_p = _prep_in_weights(w_in, b_in, l + 1)
            h, proj = _out_in_proj(ys, h, w_out[l].astype(BF16), ln_g[l], ln_b[l], w_p, b_p)
        else:
            h = _out_proj(ys, h, w_out[l].astype(BF16), ln_g[l], ln_b[l])
    return h.reshape(bsz, seq, d)
```

```python
import functools

import numpy as np
import jax
import jax.numpy as jnp
from jax import lax
from jax.experimental import pallas as pl
from jax.experimental.pallas import tpu as pltpu

F32 = jnp.float32
BF16 = jnp.bfloat16
Y_DTYPE = BF16

D_MODEL = 1024
DEPTH = 2
HEAD_DIM = 64
HALF = HEAD_DIM // 2
N_HEADS = 4
GROUP_W = N_HEADS * HEAD_DIM
N_GROUPS = 4
ROPE_THETA = 10000.0
NEG = -1e30
LN_EPS = 1e-5
A_WINDOW = 128
A_GROUP = 2
A_SUB = 16
V_AUG = HEAD_DIM + 16
LOG2E = 1.4426950408889634
PRO_ROWS = 512
B_CONV = 4
C_CHUNK = 64
C_SUB = 8
BC_STEP = 1024
C_EPS = 1e-6
CMP_LEN = 32
CMP_STRIDE = 16
SLC_LEN = 64
N_SEL = 8
NSA_WINDOW = 512
FORCE_SCORE = 1e6
DN_ALPHA = (2.0 * DEPTH) ** 0.25
QBLK = 128
SCALE = HEAD_DIM ** -0.5

VMEM_LIMIT = 56 * 1024 * 1024
PROJ_TM = 512
PROJ_SUB = 256

LANES = 128

ORIG_SPLITS = (
    ('a_q', 256), ('a_k', 128), ('a_v', 128), ('a_z', 256),
    ('b_q', 256), ('b_k', 256), ('b_v', 256), ('b_if', 8), ('b_o', 256), ('b_z', 256),
    ('c_q', 256), ('c_f', 256), ('c_i', 256), ('c_z', 256),
    ('d_q', 256), ('d_kvc', 128), ('d_ksv', 128), ('d_kwv', 128), ('d_g', 12), ('d_z', 256),
)
N_COLS = sum(w for _, w in ORIG_SPLITS)
GATE_SLOTS = ('b_if', 'd_g')
NEW_LAYOUT = (
    ('a_q', 256), ('a_k', 128), ('a_v', 128), ('a_z', 256), ('b_q', 256), ('b_k', 256), ('b_v', 256),
    ('b_o', 256), ('b_z', 256), ('c_q', 256), ('c_f', 256), ('c_i', 256), ('c_z', 256),
    ('d_q', 256), ('d_kvc', 128), ('d_ksv', 128), ('d_kwv', 128), ('gates', 128), ('d_z', 256),
)
N_PROJ = sum(w for _, w in NEW_LAYOUT)
D_GATE_LANE = dict(ORIG_SPLITS)['b_if']


def _offset(layout, name):
    off = 0
    for n, w in layout:
        if n == name:
            return off
        off += w
    raise KeyError(name)


def _new_offset(name):
    return _offset(NEW_LAYOUT, name)


def _permuted_cols(load):
    sizes = dict(ORIG_SPLITS)
    for name, width in NEW_LAYOUT:
        new = _new_offset(name)
        if name != 'gates':
            old = _offset(ORIG_SPLITS, name)
            lo = old // LANES * LANES
            hi = min(-(-(old + width) // LANES) * LANES, N_COLS)
            yield new, load(lo, hi)[..., old - lo:old - lo + width]
            continue
        gates, lane0 = None, 0
        for slot in GATE_SLOTS:
            old = _offset(ORIG_SPLITS, slot)
            lo = old // LANES * LANES
            tile = load(lo, lo + LANES)
            if old - lo != lane0:
                tile = pltpu.roll(tile, (lane0 - (old - lo)) % LANES, axis=tile.ndim - 1)
            lane = _iota(tile.shape, tile.ndim - 1)
            part = jnp.where((lane >= lane0) & (lane < lane0 + sizes[slot]), tile, 0.0)
            gates = part if gates is None else gates + part
            lane0 += sizes[slot]
        yield new, gates


def _prep_body(w_ref, b_ref, wo_ref, bo_ref):
    for new, cols in _permuted_cols(lambda lo, hi: w_ref[0, :, lo:hi]):
        wo_ref[:, new:new + cols.shape[-1]] = cols.astype(BF16)
    for new, cols in _permuted_cols(lambda lo, hi: b_ref[0, :, lo:hi]):
        bo_ref[:, new:new + cols.shape[-1]] = cols


def _prep_in_weights(w_in, b_in, layer, rows=128):
    depth, d, n = w_in.shape
    return pl.pallas_call(
        _prep_body,
        grid=(d // rows,),
        in_specs=[pl.BlockSpec((1, rows, n), lambda r: (layer, r, 0)),
                  pl.BlockSpec((1, 1, n), lambda r: (layer, 0, 0))],
        out_specs=[pl.BlockSpec((rows, N_PROJ), lambda r: (r, 0)),
                   pl.BlockSpec((1, N_PROJ), lambda r: (0, 0))],
        out_shape=[jax.ShapeDtypeStruct((d, N_PROJ), BF16), jax.ShapeDtypeStruct((1, N_PROJ), F32)],
        compiler_params=_cparams(("arbitrary",)),
        name="prep_in_weights",
    )(w_in, b_in.reshape(depth, 1, n))


def _mm(a, b):
    return jnp.dot(a.astype(BF16), b.astype(BF16), preferred_element_type=F32)


def _mm_nt(a, b):
    return lax.dot_general(a.astype(BF16), b.astype(BF16), (((1,), (1,)), ((), ())),
                           preferred_element_type=F32)


def _mm_tn(a, b):
    return lax.dot_general(a.astype(BF16), b.astype(BF16), (((0,), (0,)), ((), ())),
                           preferred_element_type=F32)


def _dot_bf16(a, b):
    return jnp.dot(a, b, preferred_element_type=F32)


def _split_bf16(x, terms):
    out = []
    for _ in range(terms - 1):
        t = x.astype(BF16)
        out.append(t)
        x = x - t.astype(F32)
    out.append(x.astype(BF16))
    return out


def _mm_hi(a, b):
    ah, al = _split_bf16(a, 2)
    bh, bl = _split_bf16(b, 2)
    return _dot_bf16(ah, bh) + (_dot_bf16(ah, bl) + _dot_bf16(al, bh))


def _mm_sel(sel, x):
    sel = sel.astype(BF16)
    x1, x2, x3 = _split_bf16(x, 3)
    return _dot_bf16(sel, x1) + (_dot_bf16(sel, x2) + _dot_bf16(sel, x3))


def _sigmoid(x):
    return 1.0 / (1.0 + jnp.exp2(x * (-LOG2E)))


def _silu(x):
    return x * _sigmoid(x)


def _log_sigmoid(x):
    return jnp.minimum(x, 0.0) - jnp.log1p(jnp.exp(-jnp.abs(x)))


def _rope(x, cos, sin_up, sin_dn):
    w = x.shape[-1]
    up = pltpu.roll(x, w - HALF, axis=1)
    dn = pltpu.roll(x, HALF, axis=1)
    return x * cos + up * sin_up + dn * sin_dn


def _rope_bf16(x, cos, sin_up, sin_dn, perm):
    swapped = jnp.dot(x.astype(BF16), perm, preferred_element_type=F32)
    return x * cos + swapped * (sin_up + sin_dn)


def _rope_perm(width, rot_lanes):
    lane = np.arange(width)
    src = np.where(lane % HEAD_DIM < HALF, lane + HALF, lane - HALF)
    rotated = (lane % LANES) < rot_lanes
    perm = (np.arange(width)[:, None] == src[None, :]) & rotated[None, :]
    return jnp.asarray(perm.astype(np.float32), dtype=BF16)


def _rope_tables(pos, width, rot_lanes):
    inv = ROPE_THETA ** (-jnp.arange(HALF, dtype=F32) / HALF)
    ang = pos.astype(F32)[:, None] * inv[None, :]
    cos, sin = jnp.cos(ang), jnp.sin(ang)
    zero, one = jnp.zeros_like(sin), jnp.ones_like(cos)
    cos_h = jnp.concatenate([cos, cos], axis=1)
    up_h = jnp.concatenate([-sin, zero], axis=1)
    dn_h = jnp.concatenate([zero, sin], axis=1)
    id_c = jnp.concatenate([one, one], axis=1)
    id_s = jnp.concatenate([zero, zero], axis=1)
    cs, us, ds = [], [], []
    for l0 in range(0, width, HEAD_DIM):
        rot = (l0 % 128) < rot_lanes
        cs.append(cos_h if rot else id_c)
        us.append(up_h if rot else id_s)
        ds.append(dn_h if rot else id_s)
    return jnp.concatenate(cs, axis=1), jnp.concatenate(us, axis=1), jnp.concatenate(ds, axis=1)


def _iota(shape, dim):
    return lax.broadcasted_iota(jnp.int32, shape, dim)


def _cparams(sem):
    return pltpu.CompilerParams(dimension_semantics=sem, vmem_limit_bytes=VMEM_LIMIT)


def _ln_body(x_ref, g_ref, b_ref, o_ref):
    x = x_ref[...]
    mu = jnp.mean(x, axis=-1, keepdims=True)
    xc = x - mu
    var = jnp.mean(xc * xc, axis=-1, keepdims=True)
    o_ref[...] = xc * lax.rsqrt(var + LN_EPS) * g_ref[...] + b_ref[...]


def _layer_norm(x2, g, b, tm=512):
    m, d = x2.shape
    return pl.pallas_call(
        _ln_body,
        grid=(m // tm,),
        in_specs=[pl.BlockSpec((tm, d), lambda i: (i, 0)),
                  pl.BlockSpec((1, d), lambda i: (0, 0)),
                  pl.BlockSpec((1, d), lambda i: (0, 0))],
        out_specs=pl.BlockSpec((tm, d), lambda i: (i, 0)),
        out_shape=jax.ShapeDtypeStruct((m, d), F32),
        compiler_params=_cparams(("parallel",)),
        name="layer_norm",
    )(x2, g.reshape(1, d), b.reshape(1, d))


def _proj_body(x_ref, w_ref, b_ref, o_ref):
    o_ref[...] = jnp.dot(x_ref[...].astype(BF16), w_ref[...], preferred_element_type=F32) + b_ref[...]


def _in_proj(h2, w_bf, b, tm=256):
    m, d = h2.shape
    n = w_bf.shape[1]
    return pl.pallas_call(
        _proj_body,
        grid=(m // tm,),
        in_specs=[pl.BlockSpec((tm, d), lambda i: (i, 0)),
                  pl.BlockSpec((d, n), lambda i: (0, 0)),
                  pl.BlockSpec((1, n), lambda i: (0, 0))],
        out_specs=pl.BlockSpec((tm, n), lambda i: (i, 0)),
        out_shape=jax.ShapeDtypeStruct((m, n), F32),
        compiler_params=_cparams(("parallel",)),
        name="in_proj",
    )(h2, w_bf, b.reshape(1, n))


def _ln_rows(x, g, b):
    mu = jnp.mean(x, axis=-1, keepdims=True)
    xc = x - mu
    var = jnp.mean(xc * xc, axis=-1, keepdims=True)
    return xc * lax.rsqrt(var + LN_EPS) * g + b


def _mix_out(rows, ya_ref, yb_ref, yc_ref, yd_ref, h_ref, w_ref, g_ref, b_ref):
    y = _mm(ya_ref[rows, :], w_ref[0:GROUP_W, :])
    y += _mm(yb_ref[rows, :], w_ref[GROUP_W:2 * GROUP_W, :])
    y += _mm(yc_ref[rows, :], w_ref[2 * GROUP_W:3 * GROUP_W, :])
    y += _mm(yd_ref[rows, :], w_ref[3 * GROUP_W:4 * GROUP_W, :])
    return _ln_rows(DN_ALPHA * h_ref[rows, :] + y, g_ref[...], b_ref[...])


def _sub_tiles(ref):
    return [slice(r, r + PROJ_SUB) for r in range(0, ref.shape[0], PROJ_SUB)]


def _ln_proj_body(x_ref, g_ref, b_ref, w_ref, bias_ref, h_ref, o_ref):
    for rows in _sub_tiles(x_ref):
        h = _ln_rows(x_ref[rows, :], g_ref[...], b_ref[...])
        h_ref[rows, :] = h
        o_ref[rows, :] = jnp.dot(h.astype(BF16), w_ref[...], preferred_element_type=F32) + bias_ref[...]


def _out_proj_body(ya_ref, yb_ref, yc_ref, yd_ref, hp_ref, wo_ref, g_ref, b_ref, w_ref, bias_ref, h_ref, o_ref):
    for rows in _sub_tiles(hp_ref):
        h = _mix_out(rows, ya_ref, yb_ref, yc_ref, yd_ref, hp_ref, wo_ref, g_ref, b_ref)
        h_ref[rows, :] = h
        o_ref[rows, :] = jnp.dot(h.astype(BF16), w_ref[...], preferred_element_type=F32) + bias_ref[...]


def _ln_in_proj(x2, g, b, w_bf, bias, tm=PROJ_TM):
    m, d = x2.shape
    n = w_bf.shape[1]
    row = lambda width: pl.BlockSpec((tm, width), lambda i: (i, 0))
    const = lambda shape: pl.BlockSpec(shape, lambda i: (0, 0))
    return pl.pallas_call(
        _ln_proj_body,
        grid=(m // tm,),
        in_specs=[row(d), const((1, d)), const((1, d)), const((d, n)), const((1, n))],
        out_specs=[row(d), row(n)],
        out_shape=[jax.ShapeDtypeStruct((m, d), F32), jax.ShapeDtypeStruct((m, n), F32)],
        compiler_params=_cparams(("parallel",)),
        name="ln_in_proj",
    )(x2, g.reshape(1, d), b.reshape(1, d), w_bf, bias.reshape(1, n))


def _out_in_proj(ys, h2, wo_bf, g, b, w_bf, bias, tm=PROJ_TM):
    m, d = h2.shape
    n = w_bf.shape[1]
    row = lambda width: pl.BlockSpec((tm, width), lambda i: (i, 0))
    const = lambda shape: pl.BlockSpec(shape, lambda i: (0, 0))
    return pl.pallas_call(
        _out_proj_body,
        grid=(m // tm,),
        in_specs=[row(GROUP_W)] * N_GROUPS + [row(d), const((N_GROUPS * GROUP_W, d)), const((1, d)), const((1, d)),
                                              const((d, n)), const((1, n))],
        out_specs=[row(d), row(n)],
        out_shape=[jax.ShapeDtypeStruct((m, d), F32), jax.ShapeDtypeStruct((m, n), F32)],
        compiler_params=_cparams(("parallel",)),
        name="out_in_proj",
    )(*[y.reshape(m, GROUP_W) for y in ys], h2, wo_bf, g.reshape(1, d), b.reshape(1, d), w_bf, bias.reshape(1, n))


def _out_body(ya_ref, yb_ref, yc_ref, yd_ref, h_ref, w_ref, g_ref, b_ref, o_ref):
    o_ref[...] = _mix_out(slice(None), ya_ref, yb_ref, yc_ref, yd_ref, h_ref, w_ref, g_ref, b_ref)


def _out_proj(ys, h2, w_bf, g, b, tm=512):
    m, d = h2.shape
    yspec = pl.BlockSpec((tm, GROUP_W), lambda i: (i, 0))
    return pl.pallas_call(
        _out_body,
        grid=(m // tm,),
        in_specs=[yspec, yspec, yspec, yspec,
                  pl.BlockSpec((tm, d), lambda i: (i, 0)),
                  pl.BlockSpec((4 * GROUP_W, d), lambda i: (0, 0)),
                  pl.BlockSpec((1, d), lambda i: (0, 0)),
                  pl.BlockSpec((1, d), lambda i: (0, 0))],
        out_specs=pl.BlockSpec((tm, d), lambda i: (i, 0)),
        out_shape=jax.ShapeDtypeStruct((m, d), F32),
        compiler_params=_cparams(("parallel",)),
        name="out_proj",
    )(*[y.reshape(m, GROUP_W) for y in ys], h2, w_bf, g.reshape(1, d), b.reshape(1, d))


def _mixer_a_body(sink_ref, q_ref, kv_ref, z_ref, cos_ref, up_ref, dn_ref, perm_ref, o_ref, k_ref, vt_ref):
    i = pl.program_id(1)
    seq = kv_ref.shape[1]
    n_kv = N_HEADS // A_GROUP
    low_half = _iota((1, LANES), 1) < HEAD_DIM

    @pl.when(i == 0)
    def _():
        def rope_rows(c, carry):
            for u in range(PRO_ROWS // QBLK):
                t0 = c * (PRO_ROWS // QBLK) + u
                rows = pl.ds(pl.multiple_of(t0 * QBLK, QBLK), QBLK)
                kr = _rope_bf16(kv_ref[0, rows, 0:LANES], cos_ref[rows, 0:LANES], up_ref[rows, 0:LANES],
                                dn_ref[rows, 0:LANES], perm_ref[0:LANES, 0:LANES])
                v_t = _mm_nt(_iota((LANES, LANES), 0) == _iota((LANES, LANES), 1), kv_ref[0, rows, 128:256])
                v_t = v_t.astype(BF16)
                ones_rows = jnp.ones((V_AUG - HEAD_DIM, QBLK), BF16)
                other = pltpu.roll(kr, HEAD_DIM, axis=1)
                for g in range(n_kv):
                    dup = jnp.where(low_half, kr, other) if g == 0 else jnp.where(low_half, other, kr)
                    k_ref[g, rows, :] = dup.astype(BF16)
                    vt_ref[g, t0] = jnp.concatenate([v_t[g * HEAD_DIM:(g + 1) * HEAD_DIM, :], ones_rows], axis=0)
            return carry
        lax.fori_loop(0, seq // PRO_ROWS, rope_rows, 0)

    n_blk = A_WINDOW // QBLK + 1
    span = n_blk * QBLK
    first_head = _iota((1, A_GROUP * QBLK), 1) < QBLK
    eye = (_iota((QBLK, QBLK), 0) == _iota((QBLK, QBLK), 1)).astype(BF16)

    def query_block(u):
        blk_i = i * A_SUB + u
        sub = slice(u * QBLK, (u + 1) * QBLK)
        q0 = pl.multiple_of(blk_i * QBLK, QBLK)
        qrows = pl.ds(q0, QBLK)
        qr = _rope(q_ref[0, sub, :], cos_ref[qrows, :], up_ref[qrows, :], dn_ref[qrows, :]) * (SCALE * LOG2E)
        kt0 = jnp.maximum(blk_i - A_WINDOW // QBLK, 0)
        k0 = pl.multiple_of(kt0 * QBLK, QBLK)
        rel = (q0 + _iota((1, QBLK), 1)) - (k0 + _iota((span, 1), 0))
        in_band = lax.bitcast_convert_type(rel, jnp.uint32) < A_WINDOW
        bias = jnp.concatenate([jnp.where(in_band, 0.0, NEG)] * A_GROUP, axis=1)
        outs = []
        for g in range(n_kv):
            q_pair = qr[:, g * LANES:(g + 1) * LANES]
            qg = jnp.concatenate([jnp.where(low_half, q_pair, 0.0), jnp.where(low_half, 0.0, q_pair)], axis=0)
            sink = jnp.where(first_head, sink_ref[g * A_GROUP], sink_ref[g * A_GROUP + 1]) * LOG2E
            s = _mm_nt(k_ref[g, pl.ds(k0, span), :], qg) + bias
            m = jnp.maximum(jnp.max(s, axis=0, keepdims=True), sink)
            p = jnp.exp2(s - m).astype(BF16)
            o = jnp.zeros((V_AUG, A_GROUP * QBLK), F32)
            for c in range(n_blk):
                o = o + jnp.dot(vt_ref[g, kt0 + c], p[c * QBLK:(c + 1) * QBLK], preferred_element_type=F32)
            o = o[0:HEAD_DIM] / (o[HEAD_DIM:HEAD_DIM + 1] + jnp.exp2(sink - m))
            outs.extend(o[:, r * QBLK:(r + 1) * QBLK] for r in range(A_GROUP))
        y = _mm_nt(eye, jnp.concatenate(outs, axis=0))
        o_ref[0, sub, :] = (y * _silu(z_ref[0, sub, :])).astype(Y_DTYPE)

    for u in range(A_SUB):
        query_block(u)


def _mixer_a(proj, sinks, tabs):
    bsz, seq, _ = proj.shape
    cos, up, dn = tabs
    blk = lambda name: _new_offset(name) // GROUP_W
    tspec = pl.BlockSpec((seq, GROUP_W), lambda b, i: (0, 0))
    return pl.pallas_call(
        _mixer_a_body,
        grid=(bsz, seq // (A_SUB * QBLK)),
        in_specs=[pl.BlockSpec(memory_space=pltpu.SMEM),
                  pl.BlockSpec((1, A_SUB * QBLK, GROUP_W), lambda b, i: (b, i, blk('a_q'))),
                  pl.BlockSpec((1, seq, GROUP_W), lambda b, i: (b, 0, blk('a_k'))),
                  pl.BlockSpec((1, A_SUB * QBLK, GROUP_W), lambda b, i: (b, i, blk('a_z'))),
                  tspec, tspec, tspec, pl.BlockSpec((GROUP_W, GROUP_W), lambda b, i: (0, 0))],
        out_specs=pl.BlockSpec((1, A_SUB * QBLK, GROUP_W), lambda b, i: (b, i, 0)),
        out_shape=jax.ShapeDtypeStruct((bsz, seq, GROUP_W), Y_DTYPE),
        scratch_shapes=[pltpu.VMEM((N_HEADS // A_GROUP, seq, LANES), BF16),
                        pltpu.VMEM((N_HEADS // A_GROUP, seq // QBLK, V_AUG, QBLK), BF16)],
        compiler_params=_cparams(("arbitrary", "arbitrary")),
        name="mixer_a",
    )(sinks, proj, proj, proj, cos, up, dn, _rope_perm(GROUP_W, LANES))


B_CHUNK = 128
B_AUG = HEAD_DIM + 16


def _conv_silu(x_ref, r0, prev, w, b):
    L = B_CHUNK
    x = x_ref[0, r0:r0 + L, :]
    acc = b + w[B_CONV - 1:B_CONV, :] * x
    for s in range(1, B_CONV):
        if r0 >= s:
            shifted = x_ref[0, r0 - s:r0 - s + L, :]
        else:
            row8 = _iota((8, x.shape[1]), 0)
            xs = pltpu.roll(x, s, axis=0)
            ps = pltpu.roll(prev, s, axis=0)
            shifted = jnp.concatenate([jnp.where(row8 < s, ps, xs[0:8]), xs[8:]], axis=0)
        acc = acc + w[B_CONV - 1 - s:B_CONV - s, :] * shifted
    return _silu(acc)


def _mixer_b_init(xprev_ref, c_ref, m_ref):
    xprev_ref[...] = jnp.zeros_like(xprev_ref)
    c_ref[...] = jnp.zeros_like(c_ref)
    m_ref[...] = jnp.zeros_like(m_ref)


def _mixer_b_main(xq_ref, xk_ref, v_ref, og_ref, z_ref, if_ref, cw_ref, cb_ref, o_ref, xprev_ref, c_ref, m_ref):
    L = B_CHUNK
    tri = (_iota((L, L), 1) <= _iota((L, L), 0)).astype(F32)
    key_first = _iota((L, L), 0) <= _iota((L, L), 1)
    half_of_lane = _iota((1, 128), 1) // HEAD_DIM
    ones_rows = jnp.ones((B_AUG - HEAD_DIM, L), F32)

    def chunk(r0, state):
        rows = slice(r0, r0 + L)
        qc = _conv_silu(xq_ref, r0, xprev_ref[:, 0:GROUP_W], cw_ref[:, 0:GROUP_W], cb_ref[:, 0:GROUP_W]) * SCALE
        kc = _conv_silu(xk_ref, r0, xprev_ref[:, GROUP_W:], cw_ref[:, GROUP_W:], cb_ref[:, GROUP_W:])
        q_t = qc.T
        v_t = v_ref[0, rows, :].T

        gates = if_ref[0, rows, :]
        bcum = _mm_sel(tri, _log_sigmoid(gates))
        gates_t = gates.T
        bcum_t = bcum.T
        outs, new_state = [], []
        for h in range(N_HEADS):
            pair = slice((h // 2) * 128, (h // 2 + 1) * 128)
            own = half_of_lane == (h % 2)
            k_pair = kc[:, pair]
            q_pair_t = q_t[pair, :]
            i_row = gates_t[h:h + 1, :]
            b_row = bcum_t[N_HEADS + h:N_HEADS + h + 1, :]
            c_col = gates[:, h:h + 1] - bcum[:, N_HEADS + h:N_HEADS + h + 1]
            cst, m_prev = state[h]
            dmat = jnp.where(key_first, b_row + c_col, NEG)
            inter = b_row + m_prev
            m_t = jnp.maximum(jnp.max(dmat, axis=0, keepdims=True), inter)
            smat = _mm(jnp.where(own, k_pair, 0.0), q_pair_t) * jnp.exp(dmat - m_t)
            vaug_t = jnp.concatenate([v_t[h * HEAD_DIM:(h + 1) * HEAD_DIM, :], ones_rows], axis=0)
            tot = _mm(vaug_t, smat) + jnp.exp(inter - m_t) * _mm(cst, q_pair_t)
            num = tot[0:HEAD_DIM, :]
            den = tot[HEAD_DIM:HEAD_DIM + 1, :]
            outs.append(num / jnp.maximum(jnp.abs(den), jnp.exp(-m_t)))
            b_last = b_row[:, L - 1:L]
            d_row = b_last - b_row + i_row
            m_new = jnp.maximum(b_last + m_prev, jnp.max(d_row, axis=1, keepdims=True))
            w_row = jnp.exp(d_row - m_new)
            decay = jnp.exp(b_last + m_prev - m_new)
            new_state.append((decay * cst + jnp.where(own, _mm(vaug_t * w_row, k_pair), 0.0), m_new))
        o_ref[0, rows, :] = (_sigmoid(og_ref[0, rows, :]) * jnp.concatenate(outs, axis=0).T
                             * _silu(z_ref[0, rows, :])).astype(Y_DTYPE)
        return new_state

    rows_step = xq_ref.shape[1]
    state = [(c_ref[h], m_ref[h:h + 1, 0:1]) for h in range(N_HEADS)]
    for u in range(rows_step // L):
        state = chunk(u * L, state)
        yield
    xprev_ref[:, 0:GROUP_W] = xq_ref[0, rows_step - 8:rows_step, :]
    xprev_ref[:, GROUP_W:] = xk_ref[0, rows_step - 8:rows_step, :]
    for h in range(N_HEADS):
        c_ref[h] = state[h][0]
        m_ref[h:h + 1, :] = jnp.broadcast_to(state[h][1], (1, 128))


def _mixer_c_main(layer, q_ref, f_ref, i_ref, z_ref, lb_ref, gn_ref, tri_ref, piv_ref, o_ref, st_ref):
    L = C_CHUNK
    nsub = L // C_SUB

    raw = lb_ref[...]
    ex = jnp.exp(raw - jnp.max(raw, axis=0, keepdims=True))
    if layer == 0:
        lb = jnp.zeros((1, GROUP_W), F32)
    else:
        lb = jnp.sum(ex[1:layer + 1], axis=0, keepdims=True) / jnp.sum(ex, axis=0, keepdims=True)
    tri = (_iota((L, L), 1) <= _iota((L, L), 0)).astype(F32)
    head_of_lane = _iota((1, GROUP_W), 1) // HEAD_DIM
    same_head = ((_iota((GROUP_W, GROUP_W), 0) // HEAD_DIM)
                 == (_iota((GROUP_W, GROUP_W), 1) // HEAD_DIM))
    ones_bd = same_head.astype(BF16)

    def chunk(rows, st):
        f = lb + (1.0 - lb) * _sigmoid(f_ref[0, rows, :])
        logf = jnp.log(f)
        k = 1.0 - f
        q = _silu(q_ref[0, rows, :])
        v = i_ref[0, rows, :]
        a = _mm_sel(tri, logf) * LOG2E
        o = _mm_nt(q * jnp.exp2(a), st)

        qsts, kts, vreps = [], [], []
        for i in range(1, nsub):
            r0 = i * C_SUB
            piv = a[r0 - 1:r0, :]
            qt = q[r0:r0 + C_SUB] * jnp.exp2(a[r0:r0 + C_SUB] - piv)
            qsts.extend(jnp.where(head_of_lane == h, qt, 0.0) for h in range(N_HEADS))
            kts.append((k[0:r0] * jnp.exp2(piv - a[0:r0])).astype(BF16))
            vreps.append(v[0:r0].astype(BF16))
        scores = _mm_nt(jnp.concatenate(qsts, axis=0), jnp.concatenate(kts, axis=0))
        r = _dot_bf16((scores * piv_ref[...]).astype(BF16), jnp.concatenate(vreps, axis=0))
        pieces = [o[0:C_SUB]]
        for i in range(1, nsub):
            acc = o[i * C_SUB:(i + 1) * C_SUB]
            for h in range(N_HEADS):
                g0 = ((i - 1) * N_HEADS + h) * C_SUB
                acc = acc + jnp.where(head_of_lane == h, r[g0:g0 + C_SUB], 0.0)
            pieces.append(acc)
        o = jnp.concatenate(pieces, axis=0)

        a3 = a.reshape(nsub, C_SUB, GROUP_W)
        q3 = q.reshape(nsub, C_SUB, GROUP_W)
        k3 = k.reshape(nsub, C_SUB, GROUP_W)
        v3 = v.reshape(nsub, C_SUB, GROUP_W)
        ps = []
        for s in range(C_SUB):
            e = jnp.exp2(a3 - a3[:, s:s + 1, :] + tri_ref[s])
            ps.append((q3 * k3[:, s:s + 1, :] * e).reshape(L, GROUP_W).astype(BF16))
        w = jnp.dot(jnp.concatenate(ps, axis=0), ones_bd, preferred_element_type=F32)
        for s in range(C_SUB):
            vs = jnp.broadcast_to(v3[:, s:s + 1, :], (nsub, C_SUB, GROUP_W)).reshape(L, GROUP_W)
            o = o + w[s * L:(s + 1) * L] * vs

        a_last = a[L - 1:L, :]
        kdec = k * jnp.exp2(a_last - a)
        st_new = st * jnp.exp2(a_last) + jnp.where(same_head, _mm_tn(v, kdec), 0.0)

        ms = jnp.dot((o * o).astype(BF16), ones_bd, preferred_element_type=F32) * (1.0 / HEAD_DIM)
        o = o * lax.rsqrt(ms + C_EPS) * gn_ref[...]
        o_ref[0, rows, :] = (o * _silu(z_ref[0, rows, :])).astype(Y_DTYPE)
        return st_new

    st = st_ref[...]
    for u in range(q_ref.shape[1] // L):
        st = chunk(slice(u * L, (u + 1) * L), st)
        yield
    st_ref[...] = st


N_B_IN, N_C_IN, N_B_SCRATCH = 8, 8, 3
C_PER_B = B_CHUNK // C_CHUNK


def _mixer_bc_body(layer, *refs):
    b_in = refs[:N_B_IN]
    c_in = refs[N_B_IN:N_B_IN + N_C_IN]
    ob_ref, oc_ref = refs[N_B_IN + N_C_IN:N_B_IN + N_C_IN + 2]
    scratch = refs[N_B_IN + N_C_IN + 2:]
    b_scratch, c_scratch = scratch[:N_B_SCRATCH], scratch[N_B_SCRATCH:]

    @pl.when(pl.program_id(1) == 0)
    def _():
        _mixer_b_init(*b_scratch)
        c_scratch[0][...] = jnp.zeros_like(c_scratch[0])

    pending = [_mixer_b_main(*b_in, ob_ref, *b_scratch), _mixer_c_main(layer, *c_in, oc_ref, *c_scratch)]
    weights = [1, C_PER_B]
    while pending:
        for gen, reps in list(zip(pending, weights)):
            for _ in range(reps):
                if next(gen, StopIteration) is StopIteration:
                    idx = pending.index(gen)
                    del pending[idx], weights[idx]
                    break


def _mixer_bc(proj, conv_w, conv_b, c_lb, norm_g, layer):
    bsz, seq, _ = proj.shape
    L = BC_STEP
    blk = lambda name: _new_offset(name) // GROUP_W
    spec = lambda name: pl.BlockSpec((1, L, GROUP_W), lambda b, c: (b, c, blk(name)))
    vec = pl.BlockSpec((1, GROUP_W), lambda b, c: (0, 0))
    t_ge_s = np.arange(C_SUB)[None, :, None] >= np.arange(C_SUB)[:, None, None]
    causal_bias = np.broadcast_to(np.where(t_ge_s, 0.0, NEG), (C_SUB, C_SUB, GROUP_W)).astype(np.float32)
    pivots = np.arange(1, C_CHUNK // C_SUB)
    row_piv = np.repeat(pivots, N_HEADS * C_SUB)
    col_piv = np.repeat(pivots, pivots * C_SUB)
    same_pivot = (row_piv[:, None] == col_piv[None, :]).astype(np.float32)
    out_spec = pl.BlockSpec((1, L, GROUP_W), lambda b, c: (b, c, 0))
    out_shape = jax.ShapeDtypeStruct((bsz, seq, GROUP_W), Y_DTYPE)
    return pl.pallas_call(
        functools.partial(_mixer_bc_body, layer),
        grid=(bsz, seq // L),
        in_specs=[spec('b_q'), spec('b_k'), spec('b_v'), spec('b_o'), spec('b_z'),
                  pl.BlockSpec((1, L, LANES), lambda b, c: (b, c, _new_offset('gates') // LANES)),
                  pl.BlockSpec((B_CONV, 2 * GROUP_W), lambda b, c: (0, 0)),
                  pl.BlockSpec((1, 2 * GROUP_W), lambda b, c: (0, 0)),
                  spec('c_q'), spec('c_f'), spec('c_i'), spec('c_z'),
                  pl.BlockSpec((DEPTH, GROUP_W), lambda b, c: (0, 0)), vec,
                  pl.BlockSpec((C_SUB, C_SUB, GROUP_W), lambda b, c: (0, 0, 0)),
                  pl.BlockSpec(same_pivot.shape, lambda b, c: (0, 0))],
        out_specs=[out_spec, out_spec],
        out_shape=[out_shape, out_shape],
        scratch_shapes=[pltpu.VMEM((8, 2 * GROUP_W), F32),
                        pltpu.VMEM((N_HEADS, B_AUG, 128), F32),
                        pltpu.VMEM((8, 128), F32),
                        pltpu.VMEM((GROUP_W, GROUP_W), F32)],
        compiler_params=_cparams(("arbitrary", "arbitrary")),
        name="mixer_bc",
    )(proj, proj, proj, proj, proj, proj, conv_w, conv_b.reshape(1, -1),
      proj, proj, proj, proj, c_lb, norm_g.reshape(1, -1), jnp.asarray(causal_bias), jnp.asarray(same_pivot))


N_CMP_PAD = 128
KT = 512
D_QBLK = 256
PE_ROWS = 16
SEL_ROWS = 16


def _mixer_d_body(q_ref, z_ref, ksv_ref, kwv_ref, cin_ref, g_ref, cos_ref, up_ref, dn_ref, kcos_ref, kup_ref,
                  kdn_ref, kperm_ref, ccos_ref, cup_ref, cdn_ref, w1_ref, pe_ref, w2_ref, ov_ref, o_ref,
                  ks_ref, kw_ref, vts_ref, vtw_ref, kc_ref, vct_ref, sel_ref, qaug_ref):
    QBLK = D_QBLK
    i = pl.program_id(1)
    seq = ksv_ref.shape[1]
    half_cmp = CMP_LEN // 2

    @pl.when(i == 0)
    def _():
        def rope_rows(c, carry):
            for u in range(PRO_ROWS // KT):
                t0 = c * (PRO_ROWS // KT) + u
                rows = pl.ds(pl.multiple_of(t0 * KT, KT), KT)
                kvs_in = ksv_ref[0, rows, :]
                kvw_in = kwv_ref[0, rows, :]
                kvs = _rope_bf16(kvs_in, kcos_ref[rows, :], kup_ref[rows, :], kdn_ref[rows, :], kperm_ref[...])
                kvw = _rope_bf16(kvw_in, kcos_ref[rows, :], kup_ref[rows, :], kdn_ref[rows, :], kperm_ref[...])
                blk_in_tile = _iota((KT, HEAD_DIM), 0) // SLC_LEN
                onehot = jnp.where(blk_in_tile == _iota((KT, HEAD_DIM), 1), 1.0, 0.0)
                ks_ref[rows, :] = jnp.concatenate([kvs[:, 0:HEAD_DIM], onehot], axis=1).astype(BF16)
                kw_ref[rows, :] = kvw[:, 0:HEAD_DIM].astype(BF16)
                ones_rows = jnp.ones((V_AUG - HEAD_DIM, KT), BF16)
                pick_v = _iota((HEAD_DIM, LANES), 1) == _iota((HEAD_DIM, LANES), 0) + HEAD_DIM
                vts_ref[t0] = jnp.concatenate([_mm_nt(pick_v, kvs_in).astype(BF16), ones_rows], axis=0)
                vwt = jnp.concatenate([_mm_nt(pick_v, kvw_in).astype(BF16), ones_rows], axis=0)
                for w in range(KT // QBLK):
                    vtw_ref[t0 * (KT // QBLK) + w] = vwt[:, w * QBLK:(w + 1) * QBLK]
            return carry
        lax.fori_loop(0, seq // PRO_ROWS, rope_rows, 0)
        acc = jnp.zeros((N_CMP_PAD + PE_ROWS, 2 * LANES), F32)
        for rr in range(half_cmp // 2):
            xa = cin_ref[0, pl.ds(2 * rr, N_CMP_PAD, stride=CMP_STRIDE), :]
            xb = cin_ref[0, pl.ds(2 * rr + 1, N_CMP_PAD, stride=CMP_STRIDE), :]
            lhs = jnp.concatenate([jnp.concatenate([xa, xb], axis=1), pe_ref[rr]], axis=0)
            acc = acc + _mm_hi(lhs, w1_ref[rr])
        u0 = acc[0:N_CMP_PAD, 0:LANES] + acc[N_CMP_PAD:N_CMP_PAD + 1, 0:LANES]
        u1 = acc[0:N_CMP_PAD, LANES:] + acc[N_CMP_PAD + 1:N_CMP_PAD + 2, LANES:]
        pre = u0 + pltpu.roll(u1, N_CMP_PAD - 1, axis=0)
        cv = _rope(_mm_hi(_silu(pre), w2_ref[...]), ccos_ref[...], cup_ref[...], cdn_ref[...])
        kc_ref[...] = cv[:, 0:HEAD_DIM]
        vct_ref[...] = cv.T[HEAD_DIM:2 * HEAD_DIM, :].astype(BF16)

    q0 = pl.multiple_of(i * QBLK, QBLK)
    qrows = pl.ds(q0, QBLK)
    qr = _rope(q_ref[0], cos_ref[qrows, :], up_ref[qrows, :], dn_ref[qrows, :]) * (SCALE * LOG2E)
    qt = qr.T
    qst = jnp.concatenate([qt[h * HEAD_DIM:(h + 1) * HEAD_DIM] for h in range(N_HEADS)], axis=1)
    qst_bf = qst.astype(BF16)
    cols4 = N_HEADS * QBLK
    qpos = q0 + _iota((1, QBLK), 1)

    def lanes4(x):
        return jnp.concatenate([x] * N_HEADS, axis=1)

    sc = _mm_hi(kc_ref[...], qst)
    ends = _iota((N_CMP_PAD, 1), 0) * CMP_STRIDE + (CMP_LEN - 1)
    scm = sc + lanes4(jnp.where(ends <= qpos, 0.0, NEG))
    e = jnp.exp2(scm - jnp.max(scm, axis=0, keepdims=True))
    l = jnp.sum(e, axis=0, keepdims=True)
    any_valid = lanes4(jnp.where(qpos >= CMP_LEN - 1, 1.0, 0.0))
    pc = e * (any_valid / l)
    o_cmp = jnp.dot(vct_ref[...], pc.astype(BF16), preferred_element_type=F32)
    psum = pc[:, 0:QBLK] + pc[:, QBLK:2 * QBLK] + pc[:, 2 * QBLK:3 * QBLK] + pc[:, 3 * QBLK:4 * QBLK]

    n_slc = seq // SLC_LEN
    imp = _mm_sel(ov_ref[...], psum)
    jl = _iota((n_slc, 1), 0)
    cur = jnp.right_shift(qpos, SLC_LEN.bit_length() - 1)
    imp = jnp.where(jl == 0, FORCE_SCORE, imp)
    imp = jnp.where(jl == cur, FORCE_SCORE, imp)
    imp = jnp.where(jl == cur - 1, FORCE_SCORE, imp)
    imp = jnp.where(jl <= cur, imp, NEG)
    rank = jnp.zeros((n_slc, QBLK), F32)
    for jp in range(n_slc):
        cj = imp[jp:jp + 1, :]
        tie = jnp.where(jl > jp, 1.0, 0.0)
        rank = rank + jnp.where(cj > imp, 1.0, jnp.where(cj == imp, tie, 0.0))
    sel_bias = jnp.where(rank < N_SEL, 0.0, NEG)
    blocks_per_tile = KT // SLC_LEN
    for t in range(n_slc // blocks_per_tile):
        sel_ref[t] = sel_bias[t * blocks_per_tile:(t + 1) * blocks_per_tile, :]

    n_win = NSA_WINDOW // QBLK + 1
    span = n_win * QBLK
    kt0 = jnp.maximum(i - NSA_WINDOW // QBLK, 0)
    k0 = pl.multiple_of(kt0 * QBLK, QBLK)
    rel = qpos - (k0 + _iota((span, 1), 0))
    in_band = lax.bitcast_convert_type(rel, jnp.uint32) < NSA_WINDOW
    s = jnp.dot(kw_ref[pl.ds(k0, span), :], qst_bf, preferred_element_type=F32)
    s = s + lanes4(jnp.where(in_band, 0.0, NEG))
    p = jnp.exp2(s - jnp.max(s, axis=0, keepdims=True)).astype(BF16)
    o_win = jnp.zeros((V_AUG, cols4), F32)
    for c in range(n_win):
        o_win = o_win + jnp.dot(vtw_ref[kt0 + c], p[c * QBLK:(c + 1) * QBLK], preferred_element_type=F32)
    o_win = o_win[0:HEAD_DIM] / o_win[HEAD_DIM:HEAD_DIM + 1]

    qaug_ref[0:HEAD_DIM, :] = qst_bf
    qaug_ref[HEAD_DIM:2 * HEAD_DIM, :] = jnp.zeros((HEAD_DIM, cols4), BF16)

    def sel_scores(kt):
        picks = jnp.concatenate([sel_ref[kt], jnp.zeros((SEL_ROWS - blocks_per_tile, QBLK), F32)], axis=0)
        qaug_ref[HEAD_DIM:HEAD_DIM + SEL_ROWS, :] = lanes4(picks).astype(BF16)
        kbase = pl.multiple_of(kt * KT, KT)
        return jnp.dot(ks_ref[pl.ds(kbase, KT), :], qaug_ref[...], preferred_element_type=F32)

    def sel_update(kt, s, m_old, acc):
        m_new = jnp.maximum(m_old, jnp.max(s, axis=0, keepdims=True))
        p = jnp.exp2(s - m_new)
        acc_new = jnp.exp2(m_old - m_new) * acc + jnp.dot(vts_ref[kt], p.astype(BF16), preferred_element_type=F32)
        return m_new, acc_new

    init = (jnp.full((1, cols4), NEG, F32), jnp.zeros((V_AUG, cols4), F32))
    last = (i + KT // QBLK) // (KT // QBLK) - 1
    m_s, acc_s = lax.fori_loop(0, last, lambda kt, c: sel_update(kt, sel_scores(kt), *c), init)
    causal = jnp.where((last * KT + _iota((KT, 1), 0)) <= qpos, 0.0, NEG)
    _, acc_s = sel_update(last, sel_scores(last) + lanes4(causal), m_s, acc_s)
    o_slc = acc_s[0:HEAD_DIM] / acc_s[HEAD_DIM:HEAD_DIM + 1]

    gate = _sigmoid(g_ref[0]).T
    outs = []
    for h in range(N_HEADS):
        cs = slice(h * QBLK, (h + 1) * QBLK)
        g0 = D_GATE_LANE + 3 * h
        outs.append(gate[g0:g0 + 1, :] * o_cmp[:, cs] + gate[g0 + 1:g0 + 2, :] * o_slc[:, cs]
                    + gate[g0 + 2:g0 + 3, :] * o_win[:, cs])
    o_ref[0] = (jnp.concatenate(outs, axis=0).T * _silu(z_ref[0])).astype(Y_DTYPE)


def _mixer_d(proj, qtabs, ktabs, ctabs, w1bd, pe2, w2bd, overlap):
    QBLK = D_QBLK
    bsz, seq, _ = proj.shape
    blk = lambda name: _new_offset(name) // GROUP_W
    full = lambda shape: pl.BlockSpec(shape, lambda b, i: (0,) * len(shape))
    return pl.pallas_call(
        _mixer_d_body,
        grid=(bsz, seq // QBLK),
        in_specs=[pl.BlockSpec((1, QBLK, GROUP_W), lambda b, i: (b, i, blk('d_q'))),
                  pl.BlockSpec((1, QBLK, GROUP_W), lambda b, i: (b, i, blk('d_z'))),
                  pl.BlockSpec((1, seq, 128), lambda b, i: (b, 0, _new_offset('d_ksv') // 128)),
                  pl.BlockSpec((1, seq, 128), lambda b, i: (b, 0, _new_offset('d_kwv') // 128)),
                  pl.BlockSpec((1, seq, 128), lambda b, i: (b, 0, _new_offset('d_kvc') // 128)),
                  pl.BlockSpec((1, QBLK, LANES), lambda b, i: (b, i, _new_offset('gates') // LANES)),
                  full((seq, GROUP_W)), full((seq, GROUP_W)), full((seq, GROUP_W)),
                  full((seq, 128)), full((seq, 128)), full((seq, 128)), full((LANES, LANES)),
                  full((N_CMP_PAD, 128)), full((N_CMP_PAD, 128)), full((N_CMP_PAD, 128)),
                  full((CMP_LEN // 4, 2 * LANES, 2 * LANES)), full((CMP_LEN // 4, PE_ROWS, 2 * LANES)), full((128, 128)),
                  full((seq // SLC_LEN, N_CMP_PAD))],
        out_specs=pl.BlockSpec((1, QBLK, GROUP_W), lambda b, i: (b, i, 0)),
        out_shape=jax.ShapeDtypeStruct((bsz, seq, GROUP_W), Y_DTYPE),
        scratch_shapes=[pltpu.VMEM((seq, 2 * HEAD_DIM), BF16),
                        pltpu.VMEM((seq, HEAD_DIM), BF16),
                        pltpu.VMEM((seq // KT, V_AUG, KT), BF16),
                        pltpu.VMEM((seq // QBLK, V_AUG, QBLK), BF16),
                        pltpu.VMEM((N_CMP_PAD, HEAD_DIM), F32),
                        pltpu.VMEM((HEAD_DIM, N_CMP_PAD), BF16),
                        pltpu.VMEM((seq // KT, KT // SLC_LEN, QBLK), F32),
                        pltpu.VMEM((2 * HEAD_DIM, N_HEADS * QBLK), BF16)],
        compiler_params=_cparams(("arbitrary", "arbitrary")),
        name="mixer_d",
    )(proj, proj, proj, proj, proj, proj, *qtabs, *ktabs, _rope_perm(LANES, HEAD_DIM), *ctabs, w1bd, pe2, w2bd,
      overlap)


def _cmp_params(pe, w1, w2):
    w1 = w1.reshape(2, CMP_LEN, HEAD_DIM, HEAD_DIM)
    z = jnp.zeros((CMP_LEN, HEAD_DIM, HEAD_DIM), F32)
    w1bd = jnp.concatenate([jnp.concatenate([w1[0], z], axis=2), jnp.concatenate([z, w1[1]], axis=2)], axis=1)
    z2 = jnp.zeros((HEAD_DIM, HEAD_DIM), F32)
    w2bd = jnp.concatenate([jnp.concatenate([w2[0], z2], axis=1), jnp.concatenate([z2, w2[1]], axis=1)], axis=0)
    pe2 = jnp.concatenate([pe[0], pe[1]], axis=1)
    half = CMP_LEN // 2
    w1pair = jnp.stack([jnp.concatenate([jnp.concatenate([w1bd[r], w1bd[half + r]], axis=1),
                                         jnp.concatenate([w1bd[r + 1], w1bd[half + r + 1]], axis=1)], axis=0)
                        for r in range(0, half, 2)])
    pe_rows = jnp.stack([jnp.concatenate([jnp.concatenate([pe2[r], pe2[r + 1]])[None, :],
                                          jnp.concatenate([pe2[half + r], pe2[half + r + 1]])[None, :],
                                          jnp.zeros((PE_ROWS - 2, 2 * LANES), F32)], axis=0)
                         for r in range(0, half, 2)])
    return w1pair, pe_rows, w2bd


def _overlap_matrix(seq):
    n_cmp = (seq - CMP_LEN) // CMP_STRIDE + 1
    starts = np.arange(N_CMP_PAD) * CMP_STRIDE
    blk = np.arange(seq // SLC_LEN)
    ov = ((starts[None, :] < (blk[:, None] + 1) * SLC_LEN) & (starts[None, :] + CMP_LEN > blk[:, None] * SLC_LEN))
    ov = ov & (np.arange(N_CMP_PAD)[None, :] < n_cmp)
    return jnp.asarray(ov.astype(np.float32))


def kernel(x, ln0_g, ln0_b, w_in, b_in, a_sinks, b_conv_w, b_conv_b, c_lb, c_norm_g, d_cmp_pe, d_cmp_w1, d_cmp_w2,
           w_out, ln_g, ln_b):
    bsz, seq, d = x.shape
    pos = jnp.arange(seq)
    qtabs = _rope_tables(pos, GROUP_W, 128)
    ktabs = _rope_tables(pos, 128, HEAD_DIM)
    ctabs = _rope_tables(jnp.arange(N_CMP_PAD) * CMP_STRIDE + (CMP_LEN - 1), 128, HEAD_DIM)
    overlap = _overlap_matrix(seq)

    w_p, b_p = _prep_in_weights(w_in, b_in, 0)
    h, proj = _ln_in_proj(x.reshape(bsz * seq, d), ln0_g, ln0_b, w_p, b_p)
    for l in range(DEPTH):
        proj = proj.reshape(bsz, seq, N_PROJ)
        y_a = _mixer_a(proj, a_sinks[l], qtabs)
        y_b, y_c = _mixer_bc(proj, b_conv_w[l], b_conv_b[l], c_lb, c_norm_g[l], l)
        w1bd, pe2, w2bd = _cmp_params(d_cmp_pe[l], d_cmp_w1[l], d_cmp_w2[l])
        y_d = _mixer_d(proj, qtabs, ktabs, ctabs, w1bd, pe2, w2bd, overlap)
        ys = (y_a, y_b, y_c, y_d)
        if l + 1 < DEPTH:
            w_p, b_p = _prep_in_weights(w_in, b_in, l + 1)
            h, proj = _out_in_proj(ys, h, w_out[l].astype(BF16), ln_g[l], ln_b[l], w_p, b_p)
        else:
            h = _out_proj(ys, h, w_out[l].astype(BF16), ln_g[l], ln_b[l])
    return h.reshape(bsz, seq, d)
```

```python
import functools

import numpy as np
import jax
import jax.numpy as jnp
from jax import lax
from jax.experimental import pallas as pl
from jax.experimental.pallas import tpu as pltpu

F32 = jnp.float32
BF16 = jnp.bfloat16
Y_DTYPE = BF16

DEPTH = 2
HEAD_DIM = 64
HALF = HEAD_DIM // 2
N_HEADS = 4
GROUP_W = N_HEADS * HEAD_DIM
N_GROUPS = 4
ROPE_THETA = 10000.0
NEG = -1e30
LN_EPS = 1e-5
A_WINDOW = 128
A_GROUP = 2
A_SUB = 16
V_AUG = HEAD_DIM + 16
LOG2E = 1.4426950408889634
PRO_ROWS = 512
B_CONV = 4
C_CHUNK = 64
C_SUB = 8
BC_STEP = 1024
C_EPS = 1e-6
CMP_LEN = 32
CMP_STRIDE = 16
SLC_LEN = 64
N_SEL = 8
NSA_WINDOW = 512
FORCE_SCORE = 1e6
DN_ALPHA = (2.0 * DEPTH) ** 0.25
QBLK = 128
SCALE = HEAD_DIM ** -0.5

VMEM_LIMIT = 56 * 1024 * 1024
PROJ_TM = 512
PROJ_SUB = 256

LANES = 128

ORIG_SPLITS = (
    ('a_q', 256), ('a_k', 128), ('a_v', 128), ('a_z', 256),
    ('b_q', 256), ('b_k', 256), ('b_v', 256), ('b_if', 8), ('b_o', 256), ('b_z', 256),
    ('c_q', 256), ('c_f', 256), ('c_i', 256), ('c_z', 256),
    ('d_q', 256), ('d_kvc', 128), ('d_ksv', 128), ('d_kwv', 128), ('d_g', 12), ('d_z', 256),
)
N_COLS = sum(w for _, w in ORIG_SPLITS)
GATE_SLOTS = ('b_if', 'd_g')
NEW_LAYOUT = (
    ('a_q', 256), ('a_k', 128), ('a_v', 128), ('a_z', 256), ('b_q', 256), ('b_k', 256), ('b_v', 256),
    ('b_o', 256), ('b_z', 256), ('c_q', 256), ('c_f', 256), ('c_i', 256), ('c_z', 256),
    ('d_q', 256), ('d_kvc', 128), ('d_ksv', 128), ('d_kwv', 128), ('gates', 128), ('d_z', 256),
)
N_PROJ = sum(w for _, w in NEW_LAYOUT)
D_GATE_LANE = dict(ORIG_SPLITS)['b_if']


def _offset(layout, name):
    off = 0
    for n, w in layout:
        if n == name:
            return off
        off += w
    raise KeyError(name)


def _new_offset(name):
    return _offset(NEW_LAYOUT, name)


def _permuted_cols(load):
    sizes = dict(ORIG_SPLITS)
    for name, width in NEW_LAYOUT:
        new = _new_offset(name)
        if name != 'gates':
            old = _offset(ORIG_SPLITS, name)
            lo = old // LANES * LANES
            hi = min(-(-(old + width) // LANES) * LANES, N_COLS)
            yield new, load(lo, hi)[..., old - lo:old - lo + width]
            continue
        gates, lane0 = None, 0
        for slot in GATE_SLOTS:
            old = _offset(ORIG_SPLITS, slot)
            lo = old // LANES * LANES
            tile = load(lo, lo + LANES)
            if old - lo != lane0:
                tile = pltpu.roll(tile, (lane0 - (old - lo)) % LANES, axis=tile.ndim - 1)
            lane = _iota(tile.shape, tile.ndim - 1)
            part = jnp.where((lane >= lane0) & (lane < lane0 + sizes[slot]), tile, 0.0)
            gates = part if gates is None else gates + part
            lane0 += sizes[slot]
        yield new, gates


def _prep_body(w_ref, b_ref, wo_ref, bo_ref):
    for new, cols in _permuted_cols(lambda lo, hi: w_ref[0, :, lo:hi]):
        wo_ref[:, new:new + cols.shape[-1]] = cols.astype(BF16)
    for new, cols in _permuted_cols(lambda lo, hi: b_ref[0, :, lo:hi]):
        bo_ref[:, new:new + cols.shape[-1]] = cols


def _prep_in_weights(w_in, b_in, layer, rows=128):
    depth, d, n = w_in.shape
    return pl.pallas_call(
        _prep_body,
        grid=(d // rows,),
        in_specs=[pl.BlockSpec((1, rows, n), lambda r: (layer, r, 0)),
                  pl.BlockSpec((1, 1, n), lambda r: (layer, 0, 0))],
        out_specs=[pl.BlockSpec((rows, N_PROJ), lambda r: (r, 0)),
                   pl.BlockSpec((1, N_PROJ), lambda r: (0, 0))],
        out_shape=[jax.ShapeDtypeStruct((d, N_PROJ), BF16), jax.ShapeDtypeStruct((1, N_PROJ), F32)],
        compiler_params=_cparams(("arbitrary",)),
        name="prep_in_weights",
    )(w_in, b_in.reshape(depth, 1, n))


def _mm(a, b):
    return jnp.dot(a.astype(BF16), b.astype(BF16), preferred_element_type=F32)


def _mm_nt(a, b):
    return lax.dot_general(a.astype(BF16), b.astype(BF16), (((1,), (1,)), ((), ())),
                           preferred_element_type=F32)


def _mm_tn(a, b):
    return lax.dot_general(a.astype(BF16), b.astype(BF16), (((0,), (0,)), ((), ())),
                           preferred_element_type=F32)


def _dot_bf16(a, b):
    return jnp.dot(a, b, preferred_element_type=F32)


def _split_bf16(x, terms):
    out = []
    for _ in range(terms - 1):
        t = x.astype(BF16)
        out.append(t)
        x = x - t.astype(F32)
    out.append(x.astype(BF16))
    return out


def _mm_hi(a, b):
    ah, al = _split_bf16(a, 2)
    bh, bl = _split_bf16(b, 2)
    return _dot_bf16(ah, bh) + (_dot_bf16(ah, bl) + _dot_bf16(al, bh))


def _mm_sel(sel, x):
    sel = sel.astype(BF16)
    x1, x2, x3 = _split_bf16(x, 3)
    return _dot_bf16(sel, x1) + (_dot_bf16(sel, x2) + _dot_bf16(sel, x3))


def _sigmoid(x):
    return 1.0 / (1.0 + jnp.exp2(x * (-LOG2E)))


def _silu(x):
    return x * _sigmoid(x)


def _log_sigmoid(x):
    return jnp.minimum(x, 0.0) - jnp.log1p(jnp.exp(-jnp.abs(x)))


def _rope(x, cos, sin_up, sin_dn):
    w = x.shape[-1]
    up = pltpu.roll(x, w - HALF, axis=1)
    dn = pltpu.roll(x, HALF, axis=1)
    return x * cos + up * sin_up + dn * sin_dn


def _rope_bf16(x, cos, sin_up, sin_dn, perm):
    swapped = jnp.dot(x.astype(BF16), perm, preferred_element_type=F32)
    return x * cos + swapped * (sin_up + sin_dn)


def _rope_perm(width, rot_lanes):
    lane = np.arange(width)
    src = np.where(lane % HEAD_DIM < HALF, lane + HALF, lane - HALF)
    rotated = (lane % LANES) < rot_lanes
    perm = (np.arange(width)[:, None] == src[None, :]) & rotated[None, :]
    return jnp.asarray(perm.astype(np.float32), dtype=BF16)


def _rope_tables(pos, width, rot_lanes):
    inv = ROPE_THETA ** (-jnp.arange(HALF, dtype=F32) / HALF)
    ang = pos.astype(F32)[:, None] * inv[None, :]
    cos, sin = jnp.cos(ang), jnp.sin(ang)
    zero, one = jnp.zeros_like(sin), jnp.ones_like(cos)
    cos_h = jnp.concatenate([cos, cos], axis=1)
    up_h = jnp.concatenate([-sin, zero], axis=1)
    dn_h = jnp.concatenate([zero, sin], axis=1)
    id_c = jnp.concatenate([one, one], axis=1)
    id_s = jnp.concatenate([zero, zero], axis=1)
    cs, us, ds = [], [], []
    for l0 in range(0, width, HEAD_DIM):
        rot = (l0 % 128) < rot_lanes
        cs.append(cos_h if rot else id_c)
        us.append(up_h if rot else id_s)
        ds.append(dn_h if rot else id_s)
    return jnp.concatenate(cs, axis=1), jnp.concatenate(us, axis=1), jnp.concatenate(ds, axis=1)


def _iota(shape, dim):
    return lax.broadcasted_iota(jnp.int32, shape, dim)


def _cparams(sem):
    return pltpu.CompilerParams(dimension_semantics=sem, vmem_limit_bytes=VMEM_LIMIT)


def _ln_rows(x, g, b):
    mu = jnp.mean(x, axis=-1, keepdims=True)
    xc = x - mu
    var = jnp.mean(xc * xc, axis=-1, keepdims=True)
    return xc * lax.rsqrt(var + LN_EPS) * g + b


def _mix_out(rows, ya_ref, yb_ref, yc_ref, yd_ref, h_ref, w_ref, g_ref, b_ref):
    y = _mm(ya_ref[rows, :], w_ref[0:GROUP_W, :])
    y += _mm(yb_ref[rows, :], w_ref[GROUP_W:2 * GROUP_W, :])
    y += _mm(yc_ref[rows, :], w_ref[2 * GROUP_W:3 * GROUP_W, :])
    y += _mm(yd_ref[rows, :], w_ref[3 * GROUP_W:4 * GROUP_W, :])
    return _ln_rows(DN_ALPHA * h_ref[rows, :] + y, g_ref[...], b_ref[...])


def _sub_tiles(ref):
    return [slice(r, r + PROJ_SUB) for r in range(0, ref.shape[0], PROJ_SUB)]


def _ln_proj_body(x_ref, g_ref, b_ref, w_ref, bias_ref, h_ref, o_ref):
    for rows in _sub_tiles(x_ref):
        h = _ln_rows(x_ref[rows, :], g_ref[...], b_ref[...])
        h_ref[rows, :] = h
        o_ref[rows, :] = jnp.dot(h.astype(BF16), w_ref[...], preferred_element_type=F32) + bias_ref[...]


def _out_proj_body(ya_ref, yb_ref, yc_ref, yd_ref, hp_ref, wo_ref, g_ref, b_ref, w_ref, bias_ref, h_ref, o_ref):
    for rows in _sub_tiles(hp_ref):
        h = _mix_out(rows, ya_ref, yb_ref, yc_ref, yd_ref, hp_ref, wo_ref, g_ref, b_ref)
        h_ref[rows, :] = h
        o_ref[rows, :] = jnp.dot(h.astype(BF16), w_ref[...], preferred_element_type=F32) + bias_ref[...]


def _ln_in_proj(x2, g, b, w_bf, bias, tm=PROJ_TM):
    m, d = x2.shape
    n = w_bf.shape[1]
    row = lambda width: pl.BlockSpec((tm, width), lambda i: (i, 0))
    const = lambda shape: pl.BlockSpec(shape, lambda i: (0, 0))
    return pl.pallas_call(
        _ln_proj_body,
        grid=(m // tm,),
        in_specs=[row(d), const((1, d)), const((1, d)), const((d, n)), const((1, n))],
        out_specs=[row(d), row(n)],
        out_shape=[jax.ShapeDtypeStruct((m, d), F32), jax.ShapeDtypeStruct((m, n), F32)],
        compiler_params=_cparams(("parallel",)),
        name="ln_in_proj",
    )(x2, g.reshape(1, d), b.reshape(1, d), w_bf, bias.reshape(1, n))


def _out_in_proj(ys, h2, wo_bf, g, b, w_bf, bias, tm=PROJ_TM):
    m, d = h2.shape
    n = w_bf.shape[1]
    row = lambda width: pl.BlockSpec((tm, width), lambda i: (i, 0))
    const = lambda shape: pl.BlockSpec(shape, lambda i: (0, 0))
    return pl.pallas_call(
        _out_proj_body,
        grid=(m // tm,),
        in_specs=[row(GROUP_W)] * N_GROUPS + [row(d), const((N_GROUPS * GROUP_W, d)), const((1, d)), const((1, d)),
                                              const((d, n)), const((1, n))],
        out_specs=[row(d), row(n)],
        out_shape=[jax.ShapeDtypeStruct((m, d), F32), jax.ShapeDtypeStruct((m, n), F32)],
        compiler_params=_cparams(("parallel",)),
        name="out_in_proj",
    )(*[y.reshape(m, GROUP_W) for y in ys], h2, wo_bf, g.reshape(1, d), b.reshape(1, d), w_bf, bias.reshape(1, n))


def _out_body(ya_ref, yb_ref, yc_ref, yd_ref, h_ref, w_ref, g_ref, b_ref, o_ref):
    o_ref[...] = _mix_out(slice(None), ya_ref, yb_ref, yc_ref, yd_ref, h_ref, w_ref, g_ref, b_ref)


def _out_proj(ys, h2, w_bf, g, b, tm=512):
    m, d = h2.shape
    yspec = pl.BlockSpec((tm, GROUP_W), lambda i: (i, 0))
    return pl.pallas_call(
        _out_body,
        grid=(m // tm,),
        in_specs=[yspec, yspec, yspec, yspec,
                  pl.BlockSpec((tm, d), lambda i: (i, 0)),
                  pl.BlockSpec((4 * GROUP_W, d), lambda i: (0, 0)),
                  pl.BlockSpec((1, d), lambda i: (0, 0)),
                  pl.BlockSpec((1, d), lambda i: (0, 0))],
        out_specs=pl.BlockSpec((tm, d), lambda i: (i, 0)),
        out_shape=jax.ShapeDtypeStruct((m, d), F32),
        compiler_params=_cparams(("parallel",)),
        name="out_proj",
    )(*[y.reshape(m, GROUP_W) for y in ys], h2, w_bf, g.reshape(1, d), b.reshape(1, d))


def _mixer_a_body(sink_ref, q_ref, kv_ref, z_ref, cos_ref, up_ref, dn_ref, perm_ref, o_ref, k_ref, vt_ref):
    i = pl.program_id(1)
    seq = kv_ref.shape[1]
    n_kv = N_HEADS // A_GROUP
    low_half = _iota((1, LANES), 1) < HEAD_DIM

    @pl.when(i == 0)
    def _():
        def rope_rows(c, carry):
            for u in range(PRO_ROWS // QBLK):
                t0 = c * (PRO_ROWS // QBLK) + u
                rows = pl.ds(pl.multiple_of(t0 * QBLK, QBLK), QBLK)
                kr = _rope_bf16(kv_ref[0, rows, 0:LANES], cos_ref[rows, 0:LANES], up_ref[rows, 0:LANES],
                                dn_ref[rows, 0:LANES], perm_ref[0:LANES, 0:LANES])
                v_t = _mm_nt(_iota((LANES, LANES), 0) == _iota((LANES, LANES), 1), kv_ref[0, rows, LANES:2 * LANES])
                v_t = v_t.astype(BF16)
                ones_rows = jnp.ones((V_AUG - HEAD_DIM, QBLK), BF16)
                other = pltpu.roll(kr, HEAD_DIM, axis=1)
                for g in range(n_kv):
                    dup = jnp.where(low_half, kr, other) if g == 0 else jnp.where(low_half, other, kr)
                    k_ref[g, rows, :] = dup.astype(BF16)
                    vt_ref[g, t0] = jnp.concatenate([v_t[g * HEAD_DIM:(g + 1) * HEAD_DIM, :], ones_rows], axis=0)
            return carry
        lax.fori_loop(0, seq // PRO_ROWS, rope_rows, 0)

    n_blk = A_WINDOW // QBLK + 1
    span = n_blk * QBLK
    first_head = _iota((1, A_GROUP * QBLK), 1) < QBLK
    eye = (_iota((QBLK, QBLK), 0) == _iota((QBLK, QBLK), 1)).astype(BF16)

    def query_block(u):
        blk_i = i * A_SUB + u
        sub = slice(u * QBLK, (u + 1) * QBLK)
        q0 = pl.multiple_of(blk_i * QBLK, QBLK)
        qrows = pl.ds(q0, QBLK)
        qr = _rope(q_ref[0, sub, :], cos_ref[qrows, :], up_ref[qrows, :], dn_ref[qrows, :]) * (SCALE * LOG2E)
        kt0 = jnp.maximum(blk_i - A_WINDOW // QBLK, 0)
        k0 = pl.multiple_of(kt0 * QBLK, QBLK)
        rel = (q0 + _iota((1, QBLK), 1)) - (k0 + _iota((span, 1), 0))
        in_band = lax.bitcast_convert_type(rel, jnp.uint32) < A_WINDOW
        bias = jnp.concatenate([jnp.where(in_band, 0.0, NEG)] * A_GROUP, axis=1)
        outs = []
        for g in range(n_kv):
            q_pair = qr[:, g * LANES:(g + 1) * LANES]
            qg = jnp.concatenate([jnp.where(low_half, q_pair, 0.0), jnp.where(low_half, 0.0, q_pair)], axis=0)
            sink = jnp.where(first_head, sink_ref[g * A_GROUP], sink_ref[g * A_GROUP + 1]) * LOG2E
            s = _mm_nt(k_ref[g, pl.ds(k0, span), :], qg) + bias
            m = jnp.maximum(jnp.max(s, axis=0, keepdims=True), sink)
            p = jnp.exp2(s - m).astype(BF16)
            o = jnp.zeros((V_AUG, A_GROUP * QBLK), F32)
            for c in range(n_blk):
                o = o + jnp.dot(vt_ref[g, kt0 + c], p[c * QBLK:(c + 1) * QBLK], preferred_element_type=F32)
            o = o[0:HEAD_DIM] / (o[HEAD_DIM:HEAD_DIM + 1] + jnp.exp2(sink - m))
            outs.extend(o[:, r * QBLK:(r + 1) * QBLK] for r in range(A_GROUP))
        y = _mm_nt(eye, jnp.concatenate(outs, axis=0))
        o_ref[0, sub, :] = (y * _silu(z_ref[0, sub, :])).astype(Y_DTYPE)

    for u in range(A_SUB):
        query_block(u)


def _mixer_a(proj, sinks, tabs):
    assert A_GROUP == 2 and A_GROUP * HEAD_DIM == LANES, "a kv head's two query heads share one lane tile"
    bsz, seq, _ = proj.shape
    cos, up, dn = tabs
    blk = lambda name: _new_offset(name) // GROUP_W
    tspec = pl.BlockSpec((seq, GROUP_W), lambda b, i: (0, 0))
    return pl.pallas_call(
        _mixer_a_body,
        grid=(bsz, seq // (A_SUB * QBLK)),
        in_specs=[pl.BlockSpec(memory_space=pltpu.SMEM),
                  pl.BlockSpec((1, A_SUB * QBLK, GROUP_W), lambda b, i: (b, i, blk('a_q'))),
                  pl.BlockSpec((1, seq, GROUP_W), lambda b, i: (b, 0, blk('a_k'))),
                  pl.BlockSpec((1, A_SUB * QBLK, GROUP_W), lambda b, i: (b, i, blk('a_z'))),
                  tspec, tspec, tspec, pl.BlockSpec((GROUP_W, GROUP_W), lambda b, i: (0, 0))],
        out_specs=pl.BlockSpec((1, A_SUB * QBLK, GROUP_W), lambda b, i: (b, i, 0)),
        out_shape=jax.ShapeDtypeStruct((bsz, seq, GROUP_W), Y_DTYPE),
        scratch_shapes=[pltpu.VMEM((N_HEADS // A_GROUP, seq, LANES), BF16),
                        pltpu.VMEM((N_HEADS // A_GROUP, seq // QBLK, V_AUG, QBLK), BF16)],
        compiler_params=_cparams(("arbitrary", "arbitrary")),
        name="mixer_a",
    )(sinks, proj, proj, proj, cos, up, dn, _rope_perm(GROUP_W, LANES))


B_CHUNK = 128
B_AUG = V_AUG


def _conv_silu(x_ref, r0, prev, w, b):
    L = B_CHUNK
    x = x_ref[0, r0:r0 + L, :]
    acc = b + w[B_CONV - 1:B_CONV, :] * x
    for s in range(1, B_CONV):
        if r0 >= s:
            shifted = x_ref[0, r0 - s:r0 - s + L, :]
        else:
            row8 = _iota((8, x.shape[1]), 0)
            xs = pltpu.roll(x, s, axis=0)
            ps = pltpu.roll(prev, s, axis=0)
            shifted = jnp.concatenate([jnp.where(row8 < s, ps, xs[0:8]), xs[8:]], axis=0)
        acc = acc + w[B_CONV - 1 - s:B_CONV - s, :] * shifted
    return _silu(acc)


def _mixer_b_init(xprev_ref, c_ref, m_ref):
    xprev_ref[...] = jnp.zeros_like(xprev_ref)
    c_ref[...] = jnp.zeros_like(c_ref)
    m_ref[...] = jnp.zeros_like(m_ref)


def _mixer_b_main(xq_ref, xk_ref, v_ref, og_ref, z_ref, if_ref, cw_ref, cb_ref, o_ref, xprev_ref, c_ref, m_ref):
    L = B_CHUNK
    tri = (_iota((L, L), 1) <= _iota((L, L), 0)).astype(F32)
    key_first = _iota((L, L), 0) <= _iota((L, L), 1)
    half_of_lane = _iota((1, 128), 1) // HEAD_DIM
    ones_rows = jnp.ones((B_AUG - HEAD_DIM, L), F32)

    def chunk(r0, state):
        rows = slice(r0, r0 + L)
        qc = _conv_silu(xq_ref, r0, xprev_ref[:, 0:GROUP_W], cw_ref[:, 0:GROUP_W], cb_ref[:, 0:GROUP_W]) * SCALE
        kc = _conv_silu(xk_ref, r0, xprev_ref[:, GROUP_W:], cw_ref[:, GROUP_W:], cb_ref[:, GROUP_W:])
        q_t = qc.T
        v_t = v_ref[0, rows, :].T

        gates = if_ref[0, rows, :]
        bcum = _mm_sel(tri, _log_sigmoid(gates))
        gates_t = gates.T
        bcum_t = bcum.T
        outs, new_state = [], []
        for h in range(N_HEADS):
            pair = slice((h // 2) * 128, (h // 2 + 1) * 128)
            own = half_of_lane == (h % 2)
            k_pair = kc[:, pair]
            q_pair_t = q_t[pair, :]
            i_row = gates_t[h:h + 1, :]
            b_row = bcum_t[N_HEADS + h:N_HEADS + h + 1, :]
            c_col = gates[:, h:h + 1] - bcum[:, N_HEADS + h:N_HEADS + h + 1]
            cst, m_prev = state[h]
            dmat = jnp.where(key_first, b_row + c_col, NEG)
            inter = b_row + m_prev
            m_t = jnp.maximum(jnp.max(dmat, axis=0, keepdims=True), inter)
            smat = _mm(jnp.where(own, k_pair, 0.0), q_pair_t) * jnp.exp(dmat - m_t)
            vaug_t = jnp.concatenate([v_t[h * HEAD_DIM:(h + 1) * HEAD_DIM, :], ones_rows], axis=0)
            tot = _mm(vaug_t, smat) + jnp.exp(inter - m_t) * _mm(cst, q_pair_t)
            num = tot[0:HEAD_DIM, :]
            den = tot[HEAD_DIM:HEAD_DIM + 1, :]
            outs.append(num / jnp.maximum(jnp.abs(den), jnp.exp(-m_t)))
            b_last = b_row[:, L - 1:L]
            d_row = b_last - b_row + i_row
            m_new = jnp.maximum(b_last + m_prev, jnp.max(d_row, axis=1, keepdims=True))
            w_row = jnp.exp(d_row - m_new)
            decay = jnp.exp(b_last + m_prev - m_new)
            new_state.append((decay * cst + jnp.where(own, _mm(vaug_t * w_row, k_pair), 0.0), m_new))
        o_ref[0, rows, :] = (_sigmoid(og_ref[0, rows, :]) * jnp.concatenate(outs, axis=0).T
                             * _silu(z_ref[0, rows, :])).astype(Y_DTYPE)
        return new_state

    rows_step = xq_ref.shape[1]
    state = [(c_ref[h], m_ref[h:h + 1, 0:1]) for h in range(N_HEADS)]
    for u in range(rows_step // L):
        state = chunk(u * L, state)
        yield
    xprev_ref[:, 0:GROUP_W] = xq_ref[0, rows_step - 8:rows_step, :]
    xprev_ref[:, GROUP_W:] = xk_ref[0, rows_step - 8:rows_step, :]
    for h in range(N_HEADS):
        c_ref[h] = state[h][0]
        m_ref[h:h + 1, :] = jnp.broadcast_to(state[h][1], (1, 128))


def _mixer_c_main(layer, q_ref, f_ref, i_ref, z_ref, lb_ref, gn_ref, tri_ref, piv_ref, o_ref, st_ref):
    L = C_CHUNK
    nsub = L // C_SUB

    raw = lb_ref[...]
    ex = jnp.exp(raw - jnp.max(raw, axis=0, keepdims=True))
    if layer == 0:
        lb = jnp.zeros((1, GROUP_W), F32)
    else:
        lb = jnp.sum(ex[1:layer + 1], axis=0, keepdims=True) / jnp.sum(ex, axis=0, keepdims=True)
    tri = (_iota((L, L), 1) <= _iota((L, L), 0)).astype(F32)
    head_of_lane = _iota((1, GROUP_W), 1) // HEAD_DIM
    same_head = ((_iota((GROUP_W, GROUP_W), 0) // HEAD_DIM)
                 == (_iota((GROUP_W, GROUP_W), 1) // HEAD_DIM))
    ones_bd = same_head.astype(BF16)

    def chunk(rows, st):
        f = lb + (1.0 - lb) * _sigmoid(f_ref[0, rows, :])
        logf = jnp.log(f)
        k = 1.0 - f
        q = _silu(q_ref[0, rows, :])
        v = i_ref[0, rows, :]
        a = _mm_sel(tri, logf) * LOG2E
        o = _mm_nt(q * jnp.exp2(a), st)

        qsts, kts, vreps = [], [], []
        for i in range(1, nsub):
            r0 = i * C_SUB
            piv = a[r0 - 1:r0, :]
            qt = q[r0:r0 + C_SUB] * jnp.exp2(a[r0:r0 + C_SUB] - piv)
            qsts.extend(jnp.where(head_of_lane == h, qt, 0.0) for h in range(N_HEADS))
            kts.append((k[0:r0] * jnp.exp2(piv - a[0:r0])).astype(BF16))
            vreps.append(v[0:r0].astype(BF16))
        scores = _mm_nt(jnp.concatenate(qsts, axis=0), jnp.concatenate(kts, axis=0))
        r = _dot_bf16((scores * piv_ref[...]).astype(BF16), jnp.concatenate(vreps, axis=0))
        pieces = [o[0:C_SUB]]
        for i in range(1, nsub):
            acc = o[i * C_SUB:(i + 1) * C_SUB]
            for h in range(N_HEADS):
                g0 = ((i - 1) * N_HEADS + h) * C_SUB
                acc = acc + jnp.where(head_of_lane == h, r[g0:g0 + C_SUB], 0.0)
            pieces.append(acc)
        o = jnp.concatenate(pieces, axis=0)

        a3 = a.reshape(nsub, C_SUB, GROUP_W)
        q3 = q.reshape(nsub, C_SUB, GROUP_W)
        k3 = k.reshape(nsub, C_SUB, GROUP_W)
        v3 = v.reshape(nsub, C_SUB, GROUP_W)
        ps = []
        for s in range(C_SUB):
            e = jnp.exp2(a3 - a3[:, s:s + 1, :] + tri_ref[s])
            ps.append((q3 * k3[:, s:s + 1, :] * e).reshape(L, GROUP_W).astype(BF16))
        w = jnp.dot(jnp.concatenate(ps, axis=0), ones_bd, preferred_element_type=F32)
        for s in range(C_SUB):
            vs = jnp.broadcast_to(v3[:, s:s + 1, :], (nsub, C_SUB, GROUP_W)).reshape(L, GROUP_W)
            o = o + w[s * L:(s + 1) * L] * vs

        a_last = a[L - 1:L, :]
        kdec = k * jnp.exp2(a_last - a)
        st_new = st * jnp.exp2(a_last) + jnp.where(same_head, _mm_tn(v, kdec), 0.0)

        ms = jnp.dot((o * o).astype(BF16), ones_bd, preferred_element_type=F32) * (1.0 / HEAD_DIM)
        o = o * lax.rsqrt(ms + C_EPS) * gn_ref[...]
        o_ref[0, rows, :] = (o * _silu(z_ref[0, rows, :])).astype(Y_DTYPE)
        return st_new

    st = st_ref[...]
    for u in range(q_ref.shape[1] // L):
        st = chunk(slice(u * L, (u + 1) * L), st)
        yield
    st_ref[...] = st


N_B_IN, N_C_IN, N_B_SCRATCH = 8, 8, 3
C_PER_B = B_CHUNK // C_CHUNK


def _mixer_bc_body(layer, *refs):
    b_in = refs[:N_B_IN]
    c_in = refs[N_B_IN:N_B_IN + N_C_IN]
    ob_ref, oc_ref = refs[N_B_IN + N_C_IN:N_B_IN + N_C_IN + 2]
    scratch = refs[N_B_IN + N_C_IN + 2:]
    b_scratch, c_scratch = scratch[:N_B_SCRATCH], scratch[N_B_SCRATCH:]

    @pl.when(pl.program_id(1) == 0)
    def _():
        _mixer_b_init(*b_scratch)
        c_scratch[0][...] = jnp.zeros_like(c_scratch[0])

    pending = [_mixer_b_main(*b_in, ob_ref, *b_scratch), _mixer_c_main(layer, *c_in, oc_ref, *c_scratch)]
    weights = [1, C_PER_B]
    while pending:
        for gen, reps in list(zip(pending, weights)):
            for _ in range(reps):
                if next(gen, StopIteration) is StopIteration:
                    idx = pending.index(gen)
                    del pending[idx], weights[idx]
                    break


def _mixer_bc(proj, conv_w, conv_b, c_lb, norm_g, layer):
    bsz, seq, _ = proj.shape
    L = BC_STEP
    blk = lambda name: _new_offset(name) // GROUP_W
    spec = lambda name: pl.BlockSpec((1, L, GROUP_W), lambda b, c: (b, c, blk(name)))
    vec = pl.BlockSpec((1, GROUP_W), lambda b, c: (0, 0))
    t_ge_s = np.arange(C_SUB)[None, :, None] >= np.arange(C_SUB)[:, None, None]
    causal_bias = np.broadcast_to(np.where(t_ge_s, 0.0, NEG), (C_SUB, C_SUB, GROUP_W)).astype(np.float32)
    pivots = np.arange(1, C_CHUNK // C_SUB)
    row_piv = np.repeat(pivots, N_HEADS * C_SUB)
    col_piv = np.repeat(pivots, pivots * C_SUB)
    same_pivot = (row_piv[:, None] == col_piv[None, :]).astype(np.float32)
    out_spec = pl.BlockSpec((1, L, GROUP_W), lambda b, c: (b, c, 0))
    out_shape = jax.ShapeDtypeStruct((bsz, seq, GROUP_W), Y_DTYPE)
    return pl.pallas_call(
        functools.partial(_mixer_bc_body, layer),
        grid=(bsz, seq // L),
        in_specs=[spec('b_q'), spec('b_k'), spec('b_v'), spec('b_o'), spec('b_z'),
                  pl.BlockSpec((1, L, LANES), lambda b, c: (b, c, _new_offset('gates') // LANES)),
                  pl.BlockSpec((B_CONV, 2 * GROUP_W), lambda b, c: (0, 0)),
                  pl.BlockSpec((1, 2 * GROUP_W), lambda b, c: (0, 0)),
                  spec('c_q'), spec('c_f'), spec('c_i'), spec('c_z'),
                  pl.BlockSpec((DEPTH, GROUP_W), lambda b, c: (0, 0)), vec,
                  pl.BlockSpec((C_SUB, C_SUB, GROUP_W), lambda b, c: (0, 0, 0)),
                  pl.BlockSpec(same_pivot.shape, lambda b, c: (0, 0))],
        out_specs=[out_spec, out_spec],
        out_shape=[out_shape, out_shape],
        scratch_shapes=[pltpu.VMEM((8, 2 * GROUP_W), F32),
                        pltpu.VMEM((N_HEADS, B_AUG, 128), F32),
                        pltpu.VMEM((8, 128), F32),
                        pltpu.VMEM((GROUP_W, GROUP_W), F32)],
        compiler_params=_cparams(("arbitrary", "arbitrary")),
        name="mixer_bc",
    )(proj, proj, proj, proj, proj, proj, conv_w, conv_b.reshape(1, -1),
      proj, proj, proj, proj, c_lb, norm_g.reshape(1, -1), jnp.asarray(causal_bias), jnp.asarray(same_pivot))


N_CMP_PAD = 128
KT = 512
D_QBLK = 256
PE_ROWS = 16
SEL_ROWS = 16


def _mixer_d_body(q_ref, z_ref, ksv_ref, kwv_ref, cin_ref, g_ref, cos_ref, up_ref, dn_ref, kcos_ref, kup_ref,
                  kdn_ref, kperm_ref, ccos_ref, cup_ref, cdn_ref, w1_ref, pe_ref, w2_ref, ov_ref, o_ref,
                  ks_ref, kw_ref, vts_ref, vtw_ref, kc_ref, vct_ref, sel_ref, qaug_ref):
    QBLK = D_QBLK
    i = pl.program_id(1)
    seq = ksv_ref.shape[1]
    half_cmp = CMP_LEN // 2

    @pl.when(i == 0)
    def _():
        def rope_rows(c, carry):
            for u in range(PRO_ROWS // KT):
                t0 = c * (PRO_ROWS // KT) + u
                rows = pl.ds(pl.multiple_of(t0 * KT, KT), KT)
                kvs_in = ksv_ref[0, rows, :]
                kvw_in = kwv_ref[0, rows, :]
                kvs = _rope_bf16(kvs_in, kcos_ref[rows, :], kup_ref[rows, :], kdn_ref[rows, :], kperm_ref[...])
                kvw = _rope_bf16(kvw_in, kcos_ref[rows, :], kup_ref[rows, :], kdn_ref[rows, :], kperm_ref[...])
                blk_in_tile = _iota((KT, HEAD_DIM), 0) // SLC_LEN
                onehot = jnp.where(blk_in_tile == _iota((KT, HEAD_DIM), 1), 1.0, 0.0)
                ks_ref[rows, :] = jnp.concatenate([kvs[:, 0:HEAD_DIM], onehot], axis=1).astype(BF16)
                kw_ref[rows, :] = kvw[:, 0:HEAD_DIM].astype(BF16)
                ones_rows = jnp.ones((V_AUG - HEAD_DIM, KT), BF16)
                pick_v = _iota((HEAD_DIM, LANES), 1) == _iota((HEAD_DIM, LANES), 0) + HEAD_DIM
                vts_ref[t0] = jnp.concatenate([_mm_nt(pick_v, kvs_in).astype(BF16), ones_rows], axis=0)
                vwt = jnp.concatenate([_mm_nt(pick_v, kvw_in).astype(BF16), ones_rows], axis=0)
                for w in range(KT // QBLK):
                    vtw_ref[t0 * (KT // QBLK) + w] = vwt[:, w * QBLK:(w + 1) * QBLK]
            return carry
        lax.fori_loop(0, seq // PRO_ROWS, rope_rows, 0)
        acc = jnp.zeros((N_CMP_PAD + PE_ROWS, 2 * LANES), F32)
        for rr in range(half_cmp // 2):
            xa = cin_ref[0, pl.ds(2 * rr, N_CMP_PAD, stride=CMP_STRIDE), :]
            xb = cin_ref[0, pl.ds(2 * rr + 1, N_CMP_PAD, stride=CMP_STRIDE), :]
            lhs = jnp.concatenate([jnp.concatenate([xa, xb], axis=1), pe_ref[rr]], axis=0)
            acc = acc + _mm_hi(lhs, w1_ref[rr])
        u0 = acc[0:N_CMP_PAD, 0:LANES] + acc[N_CMP_PAD:N_CMP_PAD + 1, 0:LANES]
        u1 = acc[0:N_CMP_PAD, LANES:] + acc[N_CMP_PAD + 1:N_CMP_PAD + 2, LANES:]
        pre = u0 + pltpu.roll(u1, N_CMP_PAD - 1, axis=0)
        cv = _rope(_mm_hi(_silu(pre), w2_ref[...]), ccos_ref[...], cup_ref[...], cdn_ref[...])
        kc_ref[...] = cv[:, 0:HEAD_DIM]
        vct_ref[...] = cv.T[HEAD_DIM:2 * HEAD_DIM, :].astype(BF16)

    q0 = pl.multiple_of(i * QBLK, QBLK)
    qrows = pl.ds(q0, QBLK)
    qr = _rope(q_ref[0], cos_ref[qrows, :], up_ref[qrows, :], dn_ref[qrows, :]) * (SCALE * LOG2E)
    qt = qr.T
    qst = jnp.concatenate([qt[h * HEAD_DIM:(h + 1) * HEAD_DIM] for h in range(N_HEADS)], axis=1)
    qst_bf = qst.astype(BF16)
    cols4 = N_HEADS * QBLK
    qpos = q0 + _iota((1, QBLK), 1)

    def lanes4(x):
        return jnp.concatenate([x] * N_HEADS, axis=1)

    sc = _mm_hi(kc_ref[...], qst)
    ends = _iota((N_CMP_PAD, 1), 0) * CMP_STRIDE + (CMP_LEN - 1)
    scm = sc + lanes4(jnp.where(ends <= qpos, 0.0, NEG))
    e = jnp.exp2(scm - jnp.max(scm, axis=0, keepdims=True))
    l = jnp.sum(e, axis=0, keepdims=True)
    any_valid = lanes4(jnp.where(qpos >= CMP_LEN - 1, 1.0, 0.0))
    pc = e * (any_valid / l)
    o_cmp = jnp.dot(vct_ref[...], pc.astype(BF16), preferred_element_type=F32)
    psum = pc[:, 0:QBLK] + pc[:, QBLK:2 * QBLK] + pc[:, 2 * QBLK:3 * QBLK] + pc[:, 3 * QBLK:4 * QBLK]

    n_slc = seq // SLC_LEN
    imp = _mm_sel(ov_ref[...], psum)
    jl = _iota((n_slc, 1), 0)
    cur = jnp.right_shift(qpos, SLC_LEN.bit_length() - 1)
    imp = jnp.where(jl == 0, FORCE_SCORE, imp)
    imp = jnp.where(jl == cur, FORCE_SCORE, imp)
    imp = jnp.where(jl == cur - 1, FORCE_SCORE, imp)
    imp = jnp.where(jl <= cur, imp, NEG)
    rank = jnp.zeros((n_slc, QBLK), F32)
    for jp in range(n_slc):
        cj = imp[jp:jp + 1, :]
        tie = jnp.where(jl > jp, 1.0, 0.0)
        rank = rank + jnp.where(cj > imp, 1.0, jnp.where(cj == imp, tie, 0.0))
    sel_bias = jnp.where(rank < N_SEL, 0.0, NEG)
    blocks_per_tile = KT // SLC_LEN
    for t in range(n_slc // blocks_per_tile):
        sel_ref[t] = sel_bias[t * blocks_per_tile:(t + 1) * blocks_per_tile, :]

    n_win = NSA_WINDOW // QBLK + 1
    span = n_win * QBLK
    kt0 = jnp.maximum(i - NSA_WINDOW // QBLK, 0)
    k0 = pl.multiple_of(kt0 * QBLK, QBLK)
    rel = qpos - (k0 + _iota((span, 1), 0))
    in_band = lax.bitcast_convert_type(rel, jnp.uint32) < NSA_WINDOW
    s = jnp.dot(kw_ref[pl.ds(k0, span), :], qst_bf, preferred_element_type=F32)
    s = s + lanes4(jnp.where(in_band, 0.0, NEG))
    p = jnp.exp2(s - jnp.max(s, axis=0, keepdims=True)).astype(BF16)
    o_win = jnp.zeros((V_AUG, cols4), F32)
    for c in range(n_win):
        o_win = o_win + jnp.dot(vtw_ref[kt0 + c], p[c * QBLK:(c + 1) * QBLK], preferred_element_type=F32)
    o_win = o_win[0:HEAD_DIM] / o_win[HEAD_DIM:HEAD_DIM + 1]

    qaug_ref[0:HEAD_DIM, :] = qst_bf
    qaug_ref[HEAD_DIM:2 * HEAD_DIM, :] = jnp.zeros((HEAD_DIM, cols4), BF16)

    def sel_scores(kt):
        picks = jnp.concatenate([sel_ref[kt], jnp.zeros((SEL_ROWS - blocks_per_tile, QBLK), F32)], axis=0)
        qaug_ref[HEAD_DIM:HEAD_DIM + SEL_ROWS, :] = lanes4(picks).astype(BF16)
        kbase = pl.multiple_of(kt * KT, KT)
        return jnp.dot(ks_ref[pl.ds(kbase, KT), :], qaug_ref[...], preferred_element_type=F32)

    def sel_update(kt, s, m_old, acc):
        m_new = jnp.maximum(m_old, jnp.max(s, axis=0, keepdims=True))
        p = jnp.exp2(s - m_new)
        acc_new = jnp.exp2(m_old - m_new) * acc + jnp.dot(vts_ref[kt], p.astype(BF16), preferred_element_type=F32)
        return m_new, acc_new

    init = (jnp.full((1, cols4), NEG, F32), jnp.zeros((V_AUG, cols4), F32))
    last = (i + KT // QBLK) // (KT // QBLK) - 1
    m_s, acc_s = lax.fori_loop(0, last, lambda kt, c: sel_update(kt, sel_scores(kt), *c), init)
    causal = jnp.where((last * KT + _iota((KT, 1), 0)) <= qpos, 0.0, NEG)
    _, acc_s = sel_update(last, sel_scores(last) + lanes4(causal), m_s, acc_s)
    o_slc = acc_s[0:HEAD_DIM] / acc_s[HEAD_DIM:HEAD_DIM + 1]

    gate = _sigmoid(g_ref[0]).T
    outs = []
    for h in range(N_HEADS):
        cs = slice(h * QBLK, (h + 1) * QBLK)
        g0 = D_GATE_LANE + 3 * h
        outs.append(gate[g0:g0 + 1, :] * o_cmp[:, cs] + gate[g0 + 1:g0 + 2, :] * o_slc[:, cs]
                    + gate[g0 + 2:g0 + 3, :] * o_win[:, cs])
    o_ref[0] = (jnp.concatenate(outs, axis=0).T * _silu(z_ref[0])).astype(Y_DTYPE)


def _mixer_d(proj, qtabs, ktabs, ctabs, w1bd, pe2, w2bd, overlap):
    QBLK = D_QBLK
    bsz, seq, _ = proj.shape
    blk = lambda name: _new_offset(name) // GROUP_W
    full = lambda shape: pl.BlockSpec(shape, lambda b, i: (0,) * len(shape))
    return pl.pallas_call(
        _mixer_d_body,
        grid=(bsz, seq // QBLK),
        in_specs=[pl.BlockSpec((1, QBLK, GROUP_W), lambda b, i: (b, i, blk('d_q'))),
                  pl.BlockSpec((1, QBLK, GROUP_W), lambda b, i: (b, i, blk('d_z'))),
                  pl.BlockSpec((1, seq, 128), lambda b, i: (b, 0, _new_offset('d_ksv') // 128)),
                  pl.BlockSpec((1, seq, 128), lambda b, i: (b, 0, _new_offset('d_kwv') // 128)),
                  pl.BlockSpec((1, seq, 128), lambda b, i: (b, 0, _new_offset('d_kvc') // 128)),
                  pl.BlockSpec((1, QBLK, LANES), lambda b, i: (b, i, _new_offset('gates') // LANES)),
                  full((seq, GROUP_W)), full((seq, GROUP_W)), full((seq, GROUP_W)),
                  full((seq, 128)), full((seq, 128)), full((seq, 128)), full((LANES, LANES)),
                  full((N_CMP_PAD, 128)), full((N_CMP_PAD, 128)), full((N_CMP_PAD, 128)),
                  full((CMP_LEN // 4, 2 * LANES, 2 * LANES)), full((CMP_LEN // 4, PE_ROWS, 2 * LANES)), full((128, 128)),
                  full((seq // SLC_LEN, N_CMP_PAD))],
        out_specs=pl.BlockSpec((1, QBLK, GROUP_W), lambda b, i: (b, i, 0)),
        out_shape=jax.ShapeDtypeStruct((bsz, seq, GROUP_W), Y_DTYPE),
        scratch_shapes=[pltpu.VMEM((seq, 2 * HEAD_DIM), BF16),
                        pltpu.VMEM((seq, HEAD_DIM), BF16),
                        pltpu.VMEM((seq // KT, V_AUG, KT), BF16),
                        pltpu.VMEM((seq // QBLK, V_AUG, QBLK), BF16),
                        pltpu.VMEM((N_CMP_PAD, HEAD_DIM), F32),
                        pltpu.VMEM((HEAD_DIM, N_CMP_PAD), BF16),
                        pltpu.VMEM((seq // KT, KT // SLC_LEN, QBLK), F32),
                        pltpu.VMEM((2 * HEAD_DIM, N_HEADS * QBLK), BF16)],
        compiler_params=_cparams(("arbitrary", "arbitrary")),
        name="mixer_d",
    )(proj, proj, proj, proj, proj, proj, *qtabs, *ktabs, _rope_perm(LANES, HEAD_DIM), *ctabs, w1bd, pe2, w2bd,
      overlap)


def _cmp_params(pe, w1, w2):
    w1 = w1.reshape(2, CMP_LEN, HEAD_DIM, HEAD_DIM)
    z = jnp.zeros((CMP_LEN, HEAD_DIM, HEAD_DIM), F32)
    w1bd = jnp.concatenate([jnp.concatenate([w1[0], z], axis=2), jnp.concatenate([z, w1[1]], axis=2)], axis=1)
    z2 = jnp.zeros((HEAD_DIM, HEAD_DIM), F32)
    w2bd = jnp.concatenate([jnp.concatenate([w2[0], z2], axis=1), jnp.concatenate([z2, w2[1]], axis=1)], axis=0)
    pe2 = jnp.concatenate([pe[0], pe[1]], axis=1)
    half = CMP_LEN // 2
    w1pair = jnp.stack([jnp.concatenate([jnp.concatenate([w1bd[r], w1bd[half + r]], axis=1),
                                         jnp.concatenate([w1bd[r + 1], w1bd[half + r + 1]], axis=1)], axis=0)
                        for r in range(0, half, 2)])
    pe_rows = jnp.stack([jnp.concatenate([jnp.concatenate([pe2[r], pe2[r + 1]])[None, :],
                                          jnp.concatenate([pe2[half + r], pe2[half + r + 1]])[None, :],
                                          jnp.zeros((PE_ROWS - 2, 2 * LANES), F32)], axis=0)
                         for r in range(0, half, 2)])
    return w1pair, pe_rows, w2bd


def _overlap_matrix(seq):
    n_cmp = (seq - CMP_LEN) // CMP_STRIDE + 1
    starts = np.arange(N_CMP_PAD) * CMP_STRIDE
    blk = np.arange(seq // SLC_LEN)
    ov = ((starts[None, :] < (blk[:, None] + 1) * SLC_LEN) & (starts[None, :] + CMP_LEN > blk[:, None] * SLC_LEN))
    ov = ov & (np.arange(N_CMP_PAD)[None, :] < n_cmp)
    return jnp.asarray(ov.astype(np.float32))


def kernel(x, ln0_g, ln0_b, w_in, b_in, a_sinks, b_conv_w, b_conv_b, c_lb, c_norm_g, d_cmp_pe, d_cmp_w1, d_cmp_w2,
           w_out, ln_g, ln_b):
    bsz, seq, d = x.shape
    pos = jnp.arange(seq)
    qtabs = _rope_tables(pos, GROUP_W, 128)
    ktabs = _rope_tables(pos, 128, HEAD_DIM)
    ctabs = _rope_tables(jnp.arange(N_CMP_PAD) * CMP_STRIDE + (CMP_LEN - 1), 128, HEAD_DIM)
    overlap = _overlap_matrix(seq)

    w_p, b_p = _prep_in_weights(w_in, b_in, 0)
    h, proj = _ln_in_proj(x.reshape(bsz * seq, d), ln0_g, ln0_b, w_p, b_p)
    for l in range(DEPTH):
        proj = proj.reshape(bsz, seq, N_PROJ)
        y_a = _mixer_a(proj, a_sinks[l], qtabs)
        y_b, y_c = _mixer_bc(proj, b_conv_w[l], b_conv_b[l], c_lb, c_norm_g[l], l)
        w1bd, pe2, w2bd = _cmp_params(d_cmp_pe[l], d_cmp_w1[l], d_cmp_w2[l])
        y_d = _mixer_d(proj, qtabs, ktabs, ctabs, w1bd, pe2, w2bd, overlap)
        ys = (y_a, y_b, y_c, y_d)
        if l + 1 < DEPTH:
            w_p, b_p = _prep_in_weights(w_in, b_in, l + 1)
            h, proj = _out_in_proj(ys, h, w_out[l].astype(BF16), ln_g[l], ln_b[l], w_p, b_p)
        else:
            h = _out_proj(ys, h, w_out[l].astype(BF16), ln_g[l], ln_b[l])
    return h.reshape(bsz, seq, d)
```

```python
import functools

import numpy as np
import jax
import jax.numpy as jnp
from jax import lax
from jax.experimental import pallas as pl
from jax.experimental.pallas import tpu as pltpu

F32 = jnp.float32
BF16 = jnp.bfloat16
Y_DTYPE = BF16

DEPTH = 2
HEAD_DIM = 64
HALF = HEAD_DIM // 2
N_HEADS = 4
GROUP_W = N_HEADS * HEAD_DIM
N_GROUPS = 4
ROPE_THETA = 10000.0
NEG = -1e30
LN_EPS = 1e-5
A_WINDOW = 128
A_GROUP = 2
A_SUB = 16
V_AUG = HEAD_DIM + 16
LOG2E = 1.4426950408889634
PRO_ROWS = 512
B_CONV = 4
C_CHUNK = 64
C_SUB = 8
BC_STEP = 1024
C_EPS = 1e-6
CMP_LEN = 32
CMP_STRIDE = 16
SLC_LEN = 64
N_SEL = 8
NSA_WINDOW = 512
FORCE_SCORE = 1e6
DN_ALPHA = (2.0 * DEPTH) ** 0.25
QBLK = 128
SCALE = HEAD_DIM ** -0.5

VMEM_LIMIT = 56 * 1024 * 1024
PROJ_TM = 512
PROJ_SUB = 256

LANES = 128

ORIG_SPLITS = (
    ('a_q', 256), ('a_k', 128), ('a_v', 128), ('a_z', 256),
    ('b_q', 256), ('b_k', 256), ('b_v', 256), ('b_if', 8), ('b_o', 256), ('b_z', 256),
    ('c_q', 256), ('c_f', 256), ('c_i', 256), ('c_z', 256),
    ('d_q', 256), ('d_kvc', 128), ('d_ksv', 128), ('d_kwv', 128), ('d_g', 12), ('d_z', 256),
)
N_COLS = sum(w for _, w in ORIG_SPLITS)
GATE_SLOTS = ('b_if', 'd_g')
NEW_LAYOUT = (
    ('a_q', 256), ('a_k', 128), ('a_v', 128), ('a_z', 256), ('b_q', 256), ('b_k', 256), ('b_v', 256),
    ('b_o', 256), ('b_z', 256), ('c_q', 256), ('c_f', 256), ('c_i', 256), ('c_z', 256),
    ('d_q', 256), ('d_kvc', 128), ('d_ksv', 128), ('d_kwv', 128), ('gates', 128), ('d_z', 256),
)
N_PROJ = sum(w for _, w in NEW_LAYOUT)
D_GATE_LANE = dict(ORIG_SPLITS)['b_if']


def _offset(layout, name):
    off = 0
    for n, w in layout:
        if n == name:
            return off
        off += w
    raise KeyError(name)


def _new_offset(name):
    return _offset(NEW_LAYOUT, name)


def _permuted_cols(load):
    sizes = dict(ORIG_SPLITS)
    for name, width in NEW_LAYOUT:
        new = _new_offset(name)
        if name != 'gates':
            old = _offset(ORIG_SPLITS, name)
            lo = old // LANES * LANES
            hi = min(-(-(old + width) // LANES) * LANES, N_COLS)
            yield new, load(lo, hi)[..., old - lo:old - lo + width]
            continue
        gates, lane0 = None, 0
        for slot in GATE_SLOTS:
            old = _offset(ORIG_SPLITS, slot)
            lo = old // LANES * LANES
            tile = load(lo, lo + LANES)
            if old - lo != lane0:
                tile = pltpu.roll(tile, (lane0 - (old - lo)) % LANES, axis=tile.ndim - 1)
            lane = _iota(tile.shape, tile.ndim - 1)
            part = jnp.where((lane >= lane0) & (lane < lane0 + sizes[slot]), tile, 0.0)
            gates = part if gates is None else gates + part
            lane0 += sizes[slot]
        yield new, gates


def _prep_body(w_ref, b_ref, wo_ref, bo_ref):
    for new, cols in _permuted_cols(lambda lo, hi: w_ref[0, :, lo:hi]):
        wo_ref[:, new:new + cols.shape[-1]] = cols.astype(BF16)
    for new, cols in _permuted_cols(lambda lo, hi: b_ref[0, :, lo:hi]):
        bo_ref[:, new:new + cols.shape[-1]] = cols


def _prep_in_weights(w_in, b_in, layer, rows=128):
    depth, d, n = w_in.shape
    return pl.pallas_call(
        _prep_body,
        grid=(d // rows,),
        in_specs=[pl.BlockSpec((1, rows, n), lambda r: (layer, r, 0)),
                  pl.BlockSpec((1, 1, n), lambda r: (layer, 0, 0))],
        out_specs=[pl.BlockSpec((rows, N_PROJ), lambda r: (r, 0)),
                   pl.BlockSpec((1, N_PROJ), lambda r: (0, 0))],
        out_shape=[jax.ShapeDtypeStruct((d, N_PROJ), BF16), jax.ShapeDtypeStruct((1, N_PROJ), F32)],
        compiler_params=_cparams(("arbitrary",)),
        name="prep_in_weights",
    )(w_in, b_in.reshape(depth, 1, n))


def _mm(a, b):
    return jnp.dot(a.astype(BF16), b.astype(BF16), preferred_element_type=F32)


def _mm_nt(a, b):
    return lax.dot_general(a.astype(BF16), b.astype(BF16), (((1,), (1,)), ((), ())),
                           preferred_element_type=F32)


def _mm_tn(a, b):
    return lax.dot_general(a.astype(BF16), b.astype(BF16), (((0,), (0,)), ((), ())),
                           preferred_element_type=F32)


def _dot_bf16(a, b):
    return jnp.dot(a, b, preferred_element_type=F32)


def _split_bf16(x, terms):
    out = []
    for _ in range(terms - 1):
        t = x.astype(BF16)
        out.append(t)
        x = x - t.astype(F32)
    out.append(x.astype(BF16))
    return out


def _mm_hi(a, b):
    ah, al = _split_bf16(a, 2)
    bh, bl = _split_bf16(b, 2)
    return _dot_bf16(ah, bh) + (_dot_bf16(ah, bl) + _dot_bf16(al, bh))


def _mm_sel(sel, x):
    sel = sel.astype(BF16)
    x1, x2, x3 = _split_bf16(x, 3)
    return _dot_bf16(sel, x1) + (_dot_bf16(sel, x2) + _dot_bf16(sel, x3))


def _sigmoid(x):
    return 1.0 / (1.0 + jnp.exp2(x * (-LOG2E)))


def _silu(x):
    return x * _sigmoid(x)


def _log_sigmoid(x):
    return jnp.minimum(x, 0.0) - jnp.log1p(jnp.exp(-jnp.abs(x)))


def _rope(x, cos, sin_up, sin_dn):
    w = x.shape[-1]
    up = pltpu.roll(x, w - HALF, axis=1)
    dn = pltpu.roll(x, HALF, axis=1)
    return x * cos + up * sin_up + dn * sin_dn


def _rope_bf16(x, cos, sin_up, sin_dn, perm):
    swapped = jnp.dot(x.astype(BF16), perm, preferred_element_type=F32)
    return x * cos + swapped * (sin_up + sin_dn)


def _rope_perm(width, rot_lanes):
    lane = np.arange(width)
    src = np.where(lane % HEAD_DIM < HALF, lane + HALF, lane - HALF)
    rotated = (lane % LANES) < rot_lanes
    perm = (np.arange(width)[:, None] == src[None, :]) & rotated[None, :]
    return jnp.asarray(perm.astype(np.float32), dtype=BF16)


def _rope_tables(pos, width, rot_lanes):
    inv = ROPE_THETA ** (-jnp.arange(HALF, dtype=F32) / HALF)
    ang = pos.astype(F32)[:, None] * inv[None, :]
    cos, sin = jnp.cos(ang), jnp.sin(ang)
    zero, one = jnp.zeros_like(sin), jnp.ones_like(cos)
    cos_h = jnp.concatenate([cos, cos], axis=1)
    up_h = jnp.concatenate([-sin, zero], axis=1)
    dn_h = jnp.concatenate([zero, sin], axis=1)
    id_c = jnp.concatenate([one, one], axis=1)
    id_s = jnp.concatenate([zero, zero], axis=1)
    cs, us, ds = [], [], []
    for l0 in range(0, width, HEAD_DIM):
        rot = (l0 % 128) < rot_lanes
        cs.append(cos_h if rot else id_c)
        us.append(up_h if rot else id_s)
        ds.append(dn_h if rot else id_s)
    return jnp.concatenate(cs, axis=1), jnp.concatenate(us, axis=1), jnp.concatenate(ds, axis=1)


def _iota(shape, dim):
    return lax.broadcasted_iota(jnp.int32, shape, dim)


def _cparams(sem):
    return pltpu.CompilerParams(dimension_semantics=sem, vmem_limit_bytes=VMEM_LIMIT)


def _ln_rows(x, g, b):
    mu = jnp.mean(x, axis=-1, keepdims=True)
    xc = x - mu
    var = jnp.mean(xc * xc, axis=-1, keepdims=True)
    return xc * lax.rsqrt(var + LN_EPS) * g + b


def _mix_out(rows, ya_ref, yb_ref, yc_ref, yd_ref, h_ref, w_ref, g_ref, b_ref):
    y = _mm(ya_ref[rows, :], w_ref[0:GROUP_W, :])
    y += _mm(yb_ref[rows, :], w_ref[GROUP_W:2 * GROUP_W, :])
    y += _mm(yc_ref[rows, :], w_ref[2 * GROUP_W:3 * GROUP_W, :])
    y += _mm(yd_ref[rows, :], w_ref[3 * GROUP_W:4 * GROUP_W, :])
    return _ln_rows(DN_ALPHA * h_ref[rows, :] + y, g_ref[...], b_ref[...])


def _sub_tiles(ref):
    return [slice(r, r + PROJ_SUB) for r in range(0, ref.shape[0], PROJ_SUB)]


def _ln_proj_body(x_ref, g_ref, b_ref, w_ref, bias_ref, h_ref, o_ref):
    for rows in _sub_tiles(x_ref):
        h = _ln_rows(x_ref[rows, :], g_ref[...], b_ref[...])
        h_ref[rows, :] = h
        o_ref[rows, :] = jnp.dot(h.astype(BF16), w_ref[...], preferred_element_type=F32) + bias_ref[...]


def _out_proj_body(ya_ref, yb_ref, yc_ref, yd_ref, hp_ref, wo_ref, g_ref, b_ref, w_ref, bias_ref, h_ref, o_ref):
    for rows in _sub_tiles(hp_ref):
        h = _mix_out(rows, ya_ref, yb_ref, yc_ref, yd_ref, hp_ref, wo_ref, g_ref, b_ref)
        h_ref[rows, :] = h
        o_ref[rows, :] = jnp.dot(h.astype(BF16), w_ref[...], preferred_element_type=F32) + bias_ref[...]


def _ln_in_proj(x2, g, b, w_bf, bias, tm=PROJ_TM):
    m, d = x2.shape
    n = w_bf.shape[1]
    row = lambda width: pl.BlockSpec((tm, width), lambda i: (i, 0))
    const = lambda shape: pl.BlockSpec(shape, lambda i: (0, 0))
    return pl.pallas_call(
        _ln_proj_body,
        grid=(m // tm,),
        in_specs=[row(d), const((1, d)), const((1, d)), const((d, n)), const((1, n))],
        out_specs=[row(d), row(n)],
        out_shape=[jax.ShapeDtypeStruct((m, d), F32), jax.ShapeDtypeStruct((m, n), F32)],
        compiler_params=_cparams(("parallel",)),
        name="ln_in_proj",
    )(x2, g.reshape(1, d), b.reshape(1, d), w_bf, bias.reshape(1, n))


def _out_in_proj(ys, h2, wo_bf, g, b, w_bf, bias, tm=PROJ_TM):
    m, d = h2.shape
    n = w_bf.shape[1]
    row = lambda width: pl.BlockSpec((tm, width), lambda i: (i, 0))
    const = lambda shape: pl.BlockSpec(shape, lambda i: (0, 0))
    return pl.pallas_call(
        _out_proj_body,
        grid=(m // tm,),
        in_specs=[row(GROUP_W)] * N_GROUPS + [row(d), const((N_GROUPS * GROUP_W, d)), const((1, d)), const((1, d)),
                                              const((d, n)), const((1, n))],
        out_specs=[row(d), row(n)],
        out_shape=[jax.ShapeDtypeStruct((m, d), F32), jax.ShapeDtypeStruct((m, n), F32)],
        compiler_params=_cparams(("parallel",)),
        name="out_in_proj",
    )(*[y.reshape(m, GROUP_W) for y in ys], h2, wo_bf, g.reshape(1, d), b.reshape(1, d), w_bf, bias.reshape(1, n))


def _out_body(ya_ref, yb_ref, yc_ref, yd_ref, h_ref, w_ref, g_ref, b_ref, o_ref):
    o_ref[...] = _mix_out(slice(None), ya_ref, yb_ref, yc_ref, yd_ref, h_ref, w_ref, g_ref, b_ref)


def _out_proj(ys, h2, w_bf, g, b, tm=512):
    m, d = h2.shape
    yspec = pl.BlockSpec((tm, GROUP_W), lambda i: (i, 0))
    return pl.pallas_call(
        _out_body,
        grid=(m // tm,),
        in_specs=[yspec, yspec, yspec, yspec,
                  pl.BlockSpec((tm, d), lambda i: (i, 0)),
                  pl.BlockSpec((4 * GROUP_W, d), lambda i: (0, 0)),
                  pl.BlockSpec((1, d), lambda i: (0, 0)),
                  pl.BlockSpec((1, d), lambda i: (0, 0))],
        out_specs=pl.BlockSpec((tm, d), lambda i: (i, 0)),
        out_shape=jax.ShapeDtypeStruct((m, d), F32),
        compiler_params=_cparams(("parallel",)),
        name="out_proj",
    )(*[y.reshape(m, GROUP_W) for y in ys], h2, w_bf, g.reshape(1, d), b.reshape(1, d))


def _mixer_a_body(sink_ref, q_ref, kv_ref, z_ref, cos_ref, up_ref, dn_ref, perm_ref, o_ref, k_ref, vt_ref):
    i = pl.program_id(1)
    seq = kv_ref.shape[1]
    n_kv = N_HEADS // A_GROUP
    low_half = _iota((1, LANES), 1) < HEAD_DIM

    @pl.when(i == 0)
    def _():
        def rope_rows(c, carry):
            for u in range(PRO_ROWS // QBLK):
                t0 = c * (PRO_ROWS // QBLK) + u
                rows = pl.ds(pl.multiple_of(t0 * QBLK, QBLK), QBLK)
                kr = _rope_bf16(kv_ref[0, rows, 0:LANES], cos_ref[rows, 0:LANES], up_ref[rows, 0:LANES],
                                dn_ref[rows, 0:LANES], perm_ref[0:LANES, 0:LANES])
                v_t = _mm_nt(_iota((LANES, LANES), 0) == _iota((LANES, LANES), 1), kv_ref[0, rows, LANES:2 * LANES])
                v_t = v_t.astype(BF16)
                ones_rows = jnp.ones((V_AUG - HEAD_DIM, QBLK), BF16)
                other = pltpu.roll(kr, HEAD_DIM, axis=1)
                for g in range(n_kv):
                    dup = jnp.where(low_half, kr, other) if g == 0 else jnp.where(low_half, other, kr)
                    k_ref[g, rows, :] = dup.astype(BF16)
                    vt_ref[g, t0] = jnp.concatenate([v_t[g * HEAD_DIM:(g + 1) * HEAD_DIM, :], ones_rows], axis=0)
            return carry
        lax.fori_loop(0, seq // PRO_ROWS, rope_rows, 0)

    n_blk = A_WINDOW // QBLK + 1
    span = n_blk * QBLK
    first_head = _iota((1, A_GROUP * QBLK), 1) < QBLK
    eye = (_iota((QBLK, QBLK), 0) == _iota((QBLK, QBLK), 1)).astype(BF16)

    def query_block(u):
        blk_i = i * A_SUB + u
        sub = slice(u * QBLK, (u + 1) * QBLK)
        q0 = pl.multiple_of(blk_i * QBLK, QBLK)
        qrows = pl.ds(q0, QBLK)
        qr = _rope(q_ref[0, sub, :], cos_ref[qrows, :], up_ref[qrows, :], dn_ref[qrows, :]) * (SCALE * LOG2E)
        kt0 = jnp.maximum(blk_i - A_WINDOW // QBLK, 0)
        k0 = pl.multiple_of(kt0 * QBLK, QBLK)
        rel = (q0 + _iota((1, QBLK), 1)) - (k0 + _iota((span, 1), 0))
        in_band = lax.bitcast_convert_type(rel, jnp.uint32) < A_WINDOW
        bias = jnp.concatenate([jnp.where(in_band, 0.0, NEG)] * A_GROUP, axis=1)
        outs = []
        for g in range(n_kv):
            q_pair = qr[:, g * LANES:(g + 1) * LANES]
            qg = jnp.concatenate([jnp.where(low_half, q_pair, 0.0), jnp.where(low_half, 0.0, q_pair)], axis=0)
            sink = jnp.where(first_head, sink_ref[g * A_GROUP], sink_ref[g * A_GROUP + 1]) * LOG2E
            s = _mm_nt(k_ref[g, pl.ds(k0, span), :], qg) + bias
            m = jnp.maximum(jnp.max(s, axis=0, keepdims=True), sink)
            p = jnp.exp2(s - m).astype(BF16)
            o = jnp.zeros((V_AUG, A_GROUP * QBLK), F32)
            for c in range(n_blk):
                o = o + jnp.dot(vt_ref[g, kt0 + c], p[c * QBLK:(c + 1) * QBLK], preferred_element_type=F32)
            o = o[0:HEAD_DIM] / (o[HEAD_DIM:HEAD_DIM + 1] + jnp.exp2(sink - m))
            outs.extend(o[:, r * QBLK:(r + 1) * QBLK] for r in range(A_GROUP))
        y = _mm_nt(eye, jnp.concatenate(outs, axis=0))
        o_ref[0, sub, :] = (y * _silu(z_ref[0, sub, :])).astype(Y_DTYPE)

    for u in range(A_SUB):
        query_block(u)


def _mixer_a(proj, sinks, tabs):
    assert A_GROUP == 2 and A_GROUP * HEAD_DIM == LANES, "a kv head's two query heads share one lane tile"
    bsz, seq, _ = proj.shape
    cos, up, dn = tabs
    blk = lambda name: _new_offset(name) // GROUP_W
    tspec = pl.BlockSpec((seq, GROUP_W), lambda b, i: (0, 0))
    return pl.pallas_call(
        _mixer_a_body,
        grid=(bsz, seq // (A_SUB * QBLK)),
        in_specs=[pl.BlockSpec(memory_space=pltpu.SMEM),
                  pl.BlockSpec((1, A_SUB * QBLK, GROUP_W), lambda b, i: (b, i, blk('a_q'))),
                  pl.BlockSpec((1, seq, GROUP_W), lambda b, i: (b, 0, blk('a_k'))),
                  pl.BlockSpec((1, A_SUB * QBLK, GROUP_W), lambda b, i: (b, i, blk('a_z'))),
                  tspec, tspec, tspec, pl.BlockSpec((GROUP_W, GROUP_W), lambda b, i: (0, 0))],
        out_specs=pl.BlockSpec((1, A_SUB * QBLK, GROUP_W), lambda b, i: (b, i, 0)),
        out_shape=jax.ShapeDtypeStruct((bsz, seq, GROUP_W), Y_DTYPE),
        scratch_shapes=[pltpu.VMEM((N_HEADS // A_GROUP, seq, LANES), BF16),
                        pltpu.VMEM((N_HEADS // A_GROUP, seq // QBLK, V_AUG, QBLK), BF16)],
        compiler_params=_cparams(("arbitrary", "arbitrary")),
        name="mixer_a",
    )(sinks, proj, proj, proj, cos, up, dn, _rope_perm(GROUP_W, LANES))


B_CHUNK = 128
B_AUG = V_AUG


def _conv_silu(x_ref, r0, prev, w, b):
    L = B_CHUNK
    x = x_ref[0, r0:r0 + L, :]
    acc = b + w[B_CONV - 1:B_CONV, :] * x
    for s in range(1, B_CONV):
        if r0 >= s:
            shifted = x_ref[0, r0 - s:r0 - s + L, :]
        else:
            row8 = _iota((8, x.shape[1]), 0)
            xs = pltpu.roll(x, s, axis=0)
            ps = pltpu.roll(prev, s, axis=0)
            shifted = jnp.concatenate([jnp.where(row8 < s, ps, xs[0:8]), xs[8:]], axis=0)
        acc = acc + w[B_CONV - 1 - s:B_CONV - s, :] * shifted
    return _silu(acc)


def _mixer_b_init(xprev_ref, c_ref, m_ref):
    xprev_ref[...] = jnp.zeros_like(xprev_ref)
    c_ref[...] = jnp.zeros_like(c_ref)
    m_ref[...] = jnp.zeros_like(m_ref)


def _mixer_b_main(xq_ref, xk_ref, v_ref, og_ref, z_ref, if_ref, cw_ref, cb_ref, o_ref, xprev_ref, c_ref, m_ref):
    L = B_CHUNK
    tri = (_iota((L, L), 1) <= _iota((L, L), 0)).astype(F32)
    key_first = _iota((L, L), 0) <= _iota((L, L), 1)
    half_of_lane = _iota((1, 128), 1) // HEAD_DIM
    ones_rows = jnp.ones((B_AUG - HEAD_DIM, L), F32)

    def chunk(r0, state):
        rows = slice(r0, r0 + L)
        qc = _conv_silu(xq_ref, r0, xprev_ref[:, 0:GROUP_W], cw_ref[:, 0:GROUP_W], cb_ref[:, 0:GROUP_W]) * SCALE
        kc = _conv_silu(xk_ref, r0, xprev_ref[:, GROUP_W:], cw_ref[:, GROUP_W:], cb_ref[:, GROUP_W:])
        q_t = qc.T
        v_t = v_ref[0, rows, :].T

        gates = if_ref[0, rows, :]
        bcum = _mm_sel(tri, _log_sigmoid(gates))
        gates_t = gates.T
        bcum_t = bcum.T
        outs, new_state = [], []
        for h in range(N_HEADS):
            pair = slice((h // 2) * 128, (h // 2 + 1) * 128)
            own = half_of_lane == (h % 2)
            k_pair = kc[:, pair]
            q_pair_t = q_t[pair, :]
            i_row = gates_t[h:h + 1, :]
            b_row = bcum_t[N_HEADS + h:N_HEADS + h + 1, :]
            c_col = gates[:, h:h + 1] - bcum[:, N_HEADS + h:N_HEADS + h + 1]
            cst, m_prev = state[h]
            dmat = jnp.where(key_first, b_row + c_col, NEG)
            inter = b_row + m_prev
            m_t = jnp.maximum(jnp.max(dmat, axis=0, keepdims=True), inter)
            smat = _mm(jnp.where(own, k_pair, 0.0), q_pair_t) * jnp.exp(dmat - m_t)
            vaug_t = jnp.concatenate([v_t[h * HEAD_DIM:(h + 1) * HEAD_DIM, :], ones_rows], axis=0)
            tot = _mm(vaug_t, smat) + jnp.exp(inter - m_t) * _mm(cst, q_pair_t)
            num = tot[0:HEAD_DIM, :]
            den = tot[HEAD_DIM:HEAD_DIM + 1, :]
            outs.append(num / jnp.maximum(jnp.abs(den), jnp.exp(-m_t)))
            b_last = b_row[:, L - 1:L]
            d_row = b_last - b_row + i_row
            m_new = jnp.maximum(b_last + m_prev, jnp.max(d_row, axis=1, keepdims=True))
            w_row = jnp.exp(d_row - m_new)
            decay = jnp.exp(b_last + m_prev - m_new)
            new_state.append((decay * cst + jnp.where(own, _mm(vaug_t * w_row, k_pair), 0.0), m_new))
        o_ref[0, rows, :] = (_sigmoid(og_ref[0, rows, :]) * jnp.concatenate(outs, axis=0).T
                             * _silu(z_ref[0, rows, :])).astype(Y_DTYPE)
        return new_state

    rows_step = xq_ref.shape[1]
    state = [(c_ref[h], m_ref[h:h + 1, 0:1]) for h in range(N_HEADS)]
    for u in range(rows_step // L):
        state = chunk(u * L, state)
        yield
    xprev_ref[:, 0:GROUP_W] = xq_ref[0, rows_step - 8:rows_step, :]
    xprev_ref[:, GROUP_W:] = xk_ref[0, rows_step - 8:rows_step, :]
    for h in range(N_HEADS):
        c_ref[h] = state[h][0]
        m_ref[h:h + 1, :] = jnp.broadcast_to(state[h][1], (1, 128))


def _mixer_c_main(layer, q_ref, f_ref, i_ref, z_ref, lb_ref, gn_ref, tri_ref, piv_ref, o_ref, st_ref):
    L = C_CHUNK
    nsub = L // C_SUB

    raw = lb_ref[...]
    ex = jnp.exp(raw - jnp.max(raw, axis=0, keepdims=True))
    if layer == 0:
        lb = jnp.zeros((1, GROUP_W), F32)
    else:
        lb = jnp.sum(ex[1:layer + 1], axis=0, keepdims=True) / jnp.sum(ex, axis=0, keepdims=True)
    tri = (_iota((L, L), 1) <= _iota((L, L), 0)).astype(F32)
    head_of_lane = _iota((1, GROUP_W), 1) // HEAD_DIM
    same_head = ((_iota((GROUP_W, GROUP_W), 0) // HEAD_DIM)
                 == (_iota((GROUP_W, GROUP_W), 1) // HEAD_DIM))
    ones_bd = same_head.astype(BF16)

    def chunk(rows, st):
        f = lb + (1.0 - lb) * _sigmoid(f_ref[0, rows, :])
        logf = jnp.log(f)
        k = 1.0 - f
        q = _silu(q_ref[0, rows, :])
        v = i_ref[0, rows, :]
        a = _mm_sel(tri, logf) * LOG2E
        o = _mm_nt(q * jnp.exp2(a), st)

        qsts, kts, vreps = [], [], []
        for i in range(1, nsub):
            r0 = i * C_SUB
            piv = a[r0 - 1:r0, :]
            qt = q[r0:r0 + C_SUB] * jnp.exp2(a[r0:r0 + C_SUB] - piv)
            qsts.extend(jnp.where(head_of_lane == h, qt, 0.0) for h in range(N_HEADS))
            kts.append((k[0:r0] * jnp.exp2(piv - a[0:r0])).astype(BF16))
            vreps.append(v[0:r0].astype(BF16))
        scores = _mm_nt(jnp.concatenate(qsts, axis=0), jnp.concatenate(kts, axis=0))
        r = _dot_bf16((scores * piv_ref[...]).astype(BF16), jnp.concatenate(vreps, axis=0))
        pieces = [o[0:C_SUB]]
        for i in range(1, nsub):
            acc = o[i * C_SUB:(i + 1) * C_SUB]
            for h in range(N_HEADS):
                g0 = ((i - 1) * N_HEADS + h) * C_SUB
                acc = acc + jnp.where(head_of_lane == h, r[g0:g0 + C_SUB], 0.0)
            pieces.append(acc)
        o = jnp.concatenate(pieces, axis=0)

        a3 = a.reshape(nsub, C_SUB, GROUP_W)
        q3 = q.reshape(nsub, C_SUB, GROUP_W)
        k3 = k.reshape(nsub, C_SUB, GROUP_W)
        v3 = v.reshape(nsub, C_SUB, GROUP_W)
        ps = []
        for s in range(C_SUB):
            e = jnp.exp2(a3 - a3[:, s:s + 1, :] + tri_ref[s])
            ps.append((q3 * k3[:, s:s + 1, :] * e).reshape(L, GROUP_W).astype(BF16))
        w = jnp.dot(jnp.concatenate(ps, axis=0), ones_bd, preferred_element_type=F32)
        for s in range(C_SUB):
            vs = jnp.broadcast_to(v3[:, s:s + 1, :], (nsub, C_SUB, GROUP_W)).reshape(L, GROUP_W)
            o = o + w[s * L:(s + 1) * L] * vs

        a_last = a[L - 1:L, :]
        kdec = k * jnp.exp2(a_last - a)
        st_new = st * jnp.exp2(a_last) + jnp.where(same_head, _mm_tn(v, kdec), 0.0)

        ms = jnp.dot((o * o).astype(BF16), ones_bd, preferred_element_type=F32) * (1.0 / HEAD_DIM)
        o = o * lax.rsqrt(ms + C_EPS) * gn_ref[...]
        o_ref[0, rows, :] = (o * _silu(z_ref[0, rows, :])).astype(Y_DTYPE)
        return st_new

    st = st_ref[...]
    for u in range(q_ref.shape[1] // L):
        st = chunk(slice(u * L, (u + 1) * L), st)
        yield
    st_ref[...] = st


N_B_IN, N_C_IN, N_B_SCRATCH = 8, 8, 3
C_PER_B = B_CHUNK // C_CHUNK


def _mixer_bc_body(layer, *refs):
    b_in = refs[:N_B_IN]
    c_in = refs[N_B_IN:N_B_IN + N_C_IN]
    ob_ref, oc_ref = refs[N_B_IN + N_C_IN:N_B_IN + N_C_IN + 2]
    scratch = refs[N_B_IN + N_C_IN + 2:]
    b_scratch, c_scratch = scratch[:N_B_SCRATCH], scratch[N_B_SCRATCH:]

    @pl.when(pl.program_id(1) == 0)
    def _():
        _mixer_b_init(*b_scratch)
        c_scratch[0][...] = jnp.zeros_like(c_scratch[0])

    pending = [_mixer_b_main(*b_in, ob_ref, *b_scratch), _mixer_c_main(layer, *c_in, oc_ref, *c_scratch)]
    weights = [1, C_PER_B]
    while pending:
        for gen, reps in list(zip(pending, weights)):
            for _ in range(reps):
                if next(gen, StopIteration) is StopIteration:
                    idx = pending.index(gen)
                    del pending[idx], weights[idx]
                    break


def _mixer_bc(proj, conv_w, conv_b, c_lb, norm_g, layer):
    bsz, seq, _ = proj.shape
    L = BC_STEP
    blk = lambda name: _new_offset(name) // GROUP_W
    spec = lambda name: pl.BlockSpec((1, L, GROUP_W), lambda b, c: (b, c, blk(name)))
    vec = pl.BlockSpec((1, GROUP_W), lambda b, c: (0, 0))
    t_ge_s = np.arange(C_SUB)[None, :, None] >= np.arange(C_SUB)[:, None, None]
    causal_bias = np.broadcast_to(np.where(t_ge_s, 0.0, NEG), (C_SUB, C_SUB, GROUP_W)).astype(np.float32)
    pivots = np.arange(1, C_CHUNK // C_SUB)
    row_piv = np.repeat(pivots, N_HEADS * C_SUB)
    col_piv = np.repeat(pivots, pivots * C_SUB)
    same_pivot = (row_piv[:, None] == col_piv[None, :]).astype(np.float32)
    out_spec = pl.BlockSpec((1, L, GROUP_W), lambda b, c: (b, c, 0))
    out_shape = jax.ShapeDtypeStruct((bsz, seq, GROUP_W), Y_DTYPE)
    return pl.pallas_call(
        functools.partial(_mixer_bc_body, layer),
        grid=(bsz, seq // L),
        in_specs=[spec('b_q'), spec('b_k'), spec('b_v'), spec('b_o'), spec('b_z'),
                  pl.BlockSpec((1, L, LANES), lambda b, c: (b, c, _new_offset('gates') // LANES)),
                  pl.BlockSpec((B_CONV, 2 * GROUP_W), lambda b, c: (0, 0)),
                  pl.BlockSpec((1, 2 * GROUP_W), lambda b, c: (0, 0)),
                  spec('c_q'), spec('c_f'), spec('c_i'), spec('c_z'),
                  pl.BlockSpec((DEPTH, GROUP_W), lambda b, c: (0, 0)), vec,
                  pl.BlockSpec((C_SUB, C_SUB, GROUP_W), lambda b, c: (0, 0, 0)),
                  pl.BlockSpec(same_pivot.shape, lambda b, c: (0, 0))],
        out_specs=[out_spec, out_spec],
        out_shape=[out_shape, out_shape],
        scratch_shapes=[pltpu.VMEM((8, 2 * GROUP_W), F32),
                        pltpu.VMEM((N_HEADS, B_AUG, 128), F32),
                        pltpu.VMEM((8, 128), F32),
                        pltpu.VMEM((GROUP_W, GROUP_W), F32)],
        compiler_params=_cparams(("arbitrary", "arbitrary")),
        name="mixer_bc",
    )(proj, proj, proj, proj, proj, proj, conv_w, conv_b.reshape(1, -1),
      proj, proj, proj, proj, c_lb, norm_g.reshape(1, -1), jnp.asarray(causal_bias), jnp.asarray(same_pivot))


N_CMP_PAD = 128
KT = 512
D_QBLK = 512
PE_ROWS = 16
SEL_ROWS = 16


def _mixer_d_body(q_ref, z_ref, ksv_ref, kwv_ref, cin_ref, g_ref, cos_ref, up_ref, dn_ref, kcos_ref, kup_ref,
                  kdn_ref, kperm_ref, ccos_ref, cup_ref, cdn_ref, w1_ref, pe_ref, w2_ref, ov_ref, o_ref,
                  ks_ref, kw_ref, vts_ref, vtw_ref, kc_ref, vct_ref, sel_ref, qaug_ref):
    QBLK = D_QBLK
    i = pl.program_id(1)
    seq = ksv_ref.shape[1]
    half_cmp = CMP_LEN // 2

    @pl.when(i == 0)
    def _():
        def rope_rows(c, carry):
            for u in range(PRO_ROWS // KT):
                t0 = c * (PRO_ROWS // KT) + u
                rows = pl.ds(pl.multiple_of(t0 * KT, KT), KT)
                kvs_in = ksv_ref[0, rows, :]
                kvw_in = kwv_ref[0, rows, :]
                kvs = _rope_bf16(kvs_in, kcos_ref[rows, :], kup_ref[rows, :], kdn_ref[rows, :], kperm_ref[...])
                kvw = _rope_bf16(kvw_in, kcos_ref[rows, :], kup_ref[rows, :], kdn_ref[rows, :], kperm_ref[...])
                blk_in_tile = _iota((KT, HEAD_DIM), 0) // SLC_LEN
                onehot = jnp.where(blk_in_tile == _iota((KT, HEAD_DIM), 1), 1.0, 0.0)
                ks_ref[rows, :] = jnp.concatenate([kvs[:, 0:HEAD_DIM], onehot], axis=1).astype(BF16)
                kw_ref[rows, :] = kvw[:, 0:HEAD_DIM].astype(BF16)
                ones_rows = jnp.ones((V_AUG - HEAD_DIM, KT), BF16)
                pick_v = _iota((HEAD_DIM, LANES), 1) == _iota((HEAD_DIM, LANES), 0) + HEAD_DIM
                vts_ref[t0] = jnp.concatenate([_mm_nt(pick_v, kvs_in).astype(BF16), ones_rows], axis=0)
                vwt = jnp.concatenate([_mm_nt(pick_v, kvw_in).astype(BF16), ones_rows], axis=0)
                for w in range(KT // QBLK):
                    vtw_ref[t0 * (KT // QBLK) + w] = vwt[:, w * QBLK:(w + 1) * QBLK]
            return carry
        lax.fori_loop(0, seq // PRO_ROWS, rope_rows, 0)
        acc = jnp.zeros((N_CMP_PAD + PE_ROWS, 2 * LANES), F32)
        for rr in range(half_cmp // 2):
            xa = cin_ref[0, pl.ds(2 * rr, N_CMP_PAD, stride=CMP_STRIDE), :]
            xb = cin_ref[0, pl.ds(2 * rr + 1, N_CMP_PAD, stride=CMP_STRIDE), :]
            lhs = jnp.concatenate([jnp.concatenate([xa, xb], axis=1), pe_ref[rr]], axis=0)
            acc = acc + _mm_hi(lhs, w1_ref[rr])
        u0 = acc[0:N_CMP_PAD, 0:LANES] + acc[N_CMP_PAD:N_CMP_PAD + 1, 0:LANES]
        u1 = acc[0:N_CMP_PAD, LANES:] + acc[N_CMP_PAD + 1:N_CMP_PAD + 2, LANES:]
        pre = u0 + pltpu.roll(u1, N_CMP_PAD - 1, axis=0)
        cv = _rope(_mm_hi(_silu(pre), w2_ref[...]), ccos_ref[...], cup_ref[...], cdn_ref[...])
        kc_ref[...] = cv[:, 0:HEAD_DIM]
        vct_ref[...] = cv.T[HEAD_DIM:2 * HEAD_DIM, :].astype(BF16)

    q0 = pl.multiple_of(i * QBLK, QBLK)
    qrows = pl.ds(q0, QBLK)
    qr = _rope(q_ref[0], cos_ref[qrows, :], up_ref[qrows, :], dn_ref[qrows, :]) * (SCALE * LOG2E)
    qt = qr.T
    qst = jnp.concatenate([qt[h * HEAD_DIM:(h + 1) * HEAD_DIM] for h in range(N_HEADS)], axis=1)
    qst_bf = qst.astype(BF16)
    cols4 = N_HEADS * QBLK
    qpos = q0 + _iota((1, QBLK), 1)

    def lanes4(x):
        return jnp.concatenate([x] * N_HEADS, axis=1)

    sc = _mm_hi(kc_ref[...], qst)
    ends = _iota((N_CMP_PAD, 1), 0) * CMP_STRIDE + (CMP_LEN - 1)
    scm = sc + lanes4(jnp.where(ends <= qpos, 0.0, NEG))
    e = jnp.exp2(scm - jnp.max(scm, axis=0, keepdims=True))
    l = jnp.sum(e, axis=0, keepdims=True)
    any_valid = lanes4(jnp.where(qpos >= CMP_LEN - 1, 1.0, 0.0))
    pc = e * (any_valid / l)
    o_cmp = jnp.dot(vct_ref[...], pc.astype(BF16), preferred_element_type=F32)
    psum = pc[:, 0:QBLK] + pc[:, QBLK:2 * QBLK] + pc[:, 2 * QBLK:3 * QBLK] + pc[:, 3 * QBLK:4 * QBLK]

    n_slc = seq // SLC_LEN
    imp = _mm_sel(ov_ref[...], psum)
    jl = _iota((n_slc, 1), 0)
    cur = jnp.right_shift(qpos, SLC_LEN.bit_length() - 1)
    imp = jnp.where(jl == 0, FORCE_SCORE, imp)
    imp = jnp.where(jl == cur, FORCE_SCORE, imp)
    imp = jnp.where(jl == cur - 1, FORCE_SCORE, imp)
    imp = jnp.where(jl <= cur, imp, NEG)
    rank = jnp.zeros((n_slc, QBLK), F32)
    for jp in range(n_slc):
        cj = imp[jp:jp + 1, :]
        tie = jnp.where(jl > jp, 1.0, 0.0)
        rank = rank + jnp.where(cj > imp, 1.0, jnp.where(cj == imp, tie, 0.0))
    sel_bias = jnp.where(rank < N_SEL, 0.0, NEG)
    blocks_per_tile = KT // SLC_LEN
    for t in range(n_slc // blocks_per_tile):
        sel_ref[t] = sel_bias[t * blocks_per_tile:(t + 1) * blocks_per_tile, :]

    n_win = NSA_WINDOW // QBLK + 1
    span = n_win * QBLK
    kt0 = jnp.maximum(i - NSA_WINDOW // QBLK, 0)
    k0 = pl.multiple_of(kt0 * QBLK, QBLK)
    rel = qpos - (k0 + _iota((span, 1), 0))
    in_band = lax.bitcast_convert_type(rel, jnp.uint32) < NSA_WINDOW
    s = jnp.dot(kw_ref[pl.ds(k0, span), :], qst_bf, preferred_element_type=F32)
    s = s + lanes4(jnp.where(in_band, 0.0, NEG))
    p = jnp.exp2(s - jnp.max(s, axis=0, keepdims=True)).astype(BF16)
    o_win = jnp.zeros((V_AUG, cols4), F32)
    for c in range(n_win):
        o_win = o_win + jnp.dot(vtw_ref[kt0 + c], p[c * QBLK:(c + 1) * QBLK], preferred_element_type=F32)
    o_win = o_win[0:HEAD_DIM] / o_win[HEAD_DIM:HEAD_DIM + 1]

    qaug_ref[0:HEAD_DIM, :] = qst_bf
    qaug_ref[HEAD_DIM:2 * HEAD_DIM, :] = jnp.zeros((HEAD_DIM, cols4), BF16)

    def sel_scores(kt):
        picks = jnp.concatenate([sel_ref[kt], jnp.zeros((SEL_ROWS - blocks_per_tile, QBLK), F32)], axis=0)
        qaug_ref[HEAD_DIM:HEAD_DIM + SEL_ROWS, :] = lanes4(picks).astype(BF16)
        kbase = pl.multiple_of(kt * KT, KT)
        return jnp.dot(ks_ref[pl.ds(kbase, KT), :], qaug_ref[...], preferred_element_type=F32)

    def sel_update(kt, s, m_old, acc):
        m_new = jnp.maximum(m_old, jnp.max(s, axis=0, keepdims=True))
        p = jnp.exp2(s - m_new)
        acc_new = jnp.exp2(m_old - m_new) * acc + jnp.dot(vts_ref[kt], p.astype(BF16), preferred_element_type=F32)
        return m_new, acc_new

    init = (jnp.full((1, cols4), NEG, F32), jnp.zeros((V_AUG, cols4), F32))
    last = (i + KT // QBLK) // (KT // QBLK) - 1
    m_s, acc_s = lax.fori_loop(0, last, lambda kt, c: sel_update(kt, sel_scores(kt), *c), init)
    causal = jnp.where((last * KT + _iota((KT, 1), 0)) <= qpos, 0.0, NEG)
    _, acc_s = sel_update(last, sel_scores(last) + lanes4(causal), m_s, acc_s)
    o_slc = acc_s[0:HEAD_DIM] / acc_s[HEAD_DIM:HEAD_DIM + 1]

    gate = _sigmoid(g_ref[0]).T
    outs = []
    for h in range(N_HEADS):
        cs = slice(h * QBLK, (h + 1) * QBLK)
        g0 = D_GATE_LANE + 3 * h
        outs.append(gate[g0:g0 + 1, :] * o_cmp[:, cs] + gate[g0 + 1:g0 + 2, :] * o_slc[:, cs]
                    + gate[g0 + 2:g0 + 3, :] * o_win[:, cs])
    o_ref[0] = (jnp.concatenate(outs, axis=0).T * _silu(z_ref[0])).astype(Y_DTYPE)


def _mixer_d(proj, qtabs, ktabs, ctabs, w1bd, pe2, w2bd, overlap):
    QBLK = D_QBLK
    bsz, seq, _ = proj.shape
    blk = lambda name: _new_offset(name) // GROUP_W
    full = lambda shape: pl.BlockSpec(shape, lambda b, i: (0,) * len(shape))
    return pl.pallas_call(
        _mixer_d_body,
        grid=(bsz, seq // QBLK),
        in_specs=[pl.BlockSpec((1, QBLK, GROUP_W), lambda b, i: (b, i, blk('d_q'))),
                  pl.BlockSpec((1, QBLK, GROUP_W), lambda b, i: (b, i, blk('d_z'))),
                  pl.BlockSpec((1, seq, 128), lambda b, i: (b, 0, _new_offset('d_ksv') // 128)),
                  pl.BlockSpec((1, seq, 128), lambda b, i: (b, 0, _new_offset('d_kwv') // 128)),
                  pl.BlockSpec((1, seq, 128), lambda b, i: (b, 0, _new_offset('d_kvc') // 128)),
                  pl.BlockSpec((1, QBLK, LANES), lambda b, i: (b, i, _new_offset('gates') // LANES)),
                  full((seq, GROUP_W)), full((seq, GROUP_W)), full((seq, GROUP_W)),
                  full((seq, 128)), full((seq, 128)), full((seq, 128)), full((LANES, LANES)),
                  full((N_CMP_PAD, 128)), full((N_CMP_PAD, 128)), full((N_CMP_PAD, 128)),
                  full((CMP_LEN // 4, 2 * LANES, 2 * LANES)), full((CMP_LEN // 4, PE_ROWS, 2 * LANES)), full((128, 128)),
                  full((seq // SLC_LEN, N_CMP_PAD))],
        out_specs=pl.BlockSpec((1, QBLK, GROUP_W), lambda b, i: (b, i, 0)),
        out_shape=jax.ShapeDtypeStruct((bsz, seq, GROUP_W), Y_DTYPE),
        scratch_shapes=[pltpu.VMEM((seq, 2 * HEAD_DIM), BF16),
                        pltpu.VMEM((seq, HEAD_DIM), BF16),
                        pltpu.VMEM((seq // KT, V_AUG, KT), BF16),
                        pltpu.VMEM((seq // QBLK, V_AUG, QBLK), BF16),
                        pltpu.VMEM((N_CMP_PAD, HEAD_DIM), F32),
                        pltpu.VMEM((HEAD_DIM, N_CMP_PAD), BF16),
                        pltpu.VMEM((seq // KT, KT // SLC_LEN, QBLK), F32),
                        pltpu.VMEM((2 * HEAD_DIM, N_HEADS * QBLK), BF16)],
        compiler_params=_cparams(("arbitrary", "arbitrary")),
        name="mixer_d",
    )(proj, proj, proj, proj, proj, proj, *qtabs, *ktabs, _rope_perm(LANES, HEAD_DIM), *ctabs, w1bd, pe2, w2bd,
      overlap)


def _cmp_params(pe, w1, w2):
    w1 = w1.reshape(2, CMP_LEN, HEAD_DIM, HEAD_DIM)
    z = jnp.zeros((CMP_LEN, HEAD_DIM, HEAD_DIM), F32)
    w1bd = jnp.concatenate([jnp.concatenate([w1[0], z], axis=2), jnp.concatenate([z, w1[1]], axis=2)], axis=1)
    z2 = jnp.zeros((HEAD_DIM, HEAD_DIM), F32)
    w2bd = jnp.concatenate([jnp.concatenate([w2[0], z2], axis=1), jnp.concatenate([z2, w2[1]], axis=1)], axis=0)
    pe2 = jnp.concatenate([pe[0], pe[1]], axis=1)
    half = CMP_LEN // 2
    w1pair = jnp.stack([jnp.concatenate([jnp.concatenate([w1bd[r], w1bd[half + r]], axis=1),
                                         jnp.concatenate([w1bd[r + 1], w1bd[half + r + 1]], axis=1)], axis=0)
                        for r in range(0, half, 2)])
    pe_rows = jnp.stack([jnp.concatenate([jnp.concatenate([pe2[r], pe2[r + 1]])[None, :],
                                          jnp.concatenate([pe2[half + r], pe2[half + r + 1]])[None, :],
                                          jnp.zeros((PE_ROWS - 2, 2 * LANES), F32)], axis=0)
                         for r in range(0, half, 2)])
    return w1pair, pe_rows, w2bd


def _overlap_matrix(seq):
    n_cmp = (seq - CMP_LEN) // CMP_STRIDE + 1
    starts = np.arange(N_CMP_PAD) * CMP_STRIDE
    blk = np.arange(seq // SLC_LEN)
    ov = ((starts[None, :] < (blk[:, None] + 1) * SLC_LEN) & (starts[None, :] + CMP_LEN > blk[:, None] * SLC_LEN))
    ov = ov & (np.arange(N_CMP_PAD)[None, :] < n_cmp)
    return jnp.asarray(ov.astype(np.float32))


def kernel(x, ln0_g, ln0_b, w_in, b_in, a_sinks, b_conv_w, b_conv_b, c_lb, c_norm_g, d_cmp_pe, d_cmp_w1, d_cmp_w2,
           w_out, ln_g, ln_b):
    bsz, seq, d = x.shape
    pos = jnp.arange(seq)
    qtabs = _rope_tables(pos, GROUP_W, 128)
    ktabs = _rope_tables(pos, 128, HEAD_DIM)
    ctabs = _rope_tables(jnp.arange(N_CMP_PAD) * CMP_STRIDE + (CMP_LEN - 1), 128, HEAD_DIM)
    overlap = _overlap_matrix(seq)

    w_p, b_p = _prep_in_weights(w_in, b_in, 0)
    h, proj = _ln_in_proj(x.reshape(bsz * seq, d), ln0_g, ln0_b, w_p, b_p)
    for l in range(DEPTH):
        proj = proj.reshape(bsz, seq, N_PROJ)
        y_a = _mixer_a(proj, a_sinks[l], qtabs)
        y_b, y_c = _mixer_bc(proj, b_conv_w[l], b_conv_b[l], c_lb, c_norm_g[l], l)
        w1bd, pe2, w2bd = _cmp_params(d_cmp_pe[l], d_cmp_w1[l], d_cmp_w2[l])
        y_d = _mixer_d(proj, qtabs, ktabs, ctabs, w1bd, pe2, w2bd, overlap)
        ys = (y_a, y_b, y_c, y_d)
        if l + 1 < DEPTH:
            w_p, b_p = _prep_in_weights(w_in, b_in, l + 1)
            h, proj = _out_in_proj(ys, h, w_out[l].astype(BF16), ln_g[l], ln_b[l], w_p, b_p)
        else:
            h = _out_proj(ys, h, w_out[l].astype(BF16), ln_g[l], ln_b[l])
    return h.reshape(bsz, seq, d)
```

```python
import functools

import numpy as np
import jax
import jax.numpy as jnp
from jax import lax
from jax.experimental import pallas as pl
from jax.experimental.pallas import tpu as pltpu

F32 = jnp.float32
BF16 = jnp.bfloat16
Y_DTYPE = BF16

DEPTH = 2
HEAD_DIM = 64
HALF = HEAD_DIM // 2
N_HEADS = 4
GROUP_W = N_HEADS * HEAD_DIM
N_GROUPS = 4
ROPE_THETA = 10000.0
NEG = -1e30
LN_EPS = 1e-5
A_WINDOW = 128
A_GROUP = 2
A_SUB = 16
V_AUG = HEAD_DIM + 16
LOG2E = 1.4426950408889634
PRO_ROWS = 512
B_CONV = 4
C_CHUNK = 64
C_SUB = 8
BC_STEP = 1024
C_EPS = 1e-6
CMP_LEN = 32
CMP_STRIDE = 16
SLC_LEN = 64
N_SEL = 8
NSA_WINDOW = 512
FORCE_SCORE = 1e6
DN_ALPHA = (2.0 * DEPTH) ** 0.25
QBLK = 128
SCALE = HEAD_DIM ** -0.5

VMEM_LIMIT = 56 * 1024 * 1024
PROJ_TM = 512
PROJ_SUB = 256

LANES = 128

ORIG_SPLITS = (
    ('a_q', 256), ('a_k', 128), ('a_v', 128), ('a_z', 256),
    ('b_q', 256), ('b_k', 256), ('b_v', 256), ('b_if', 8), ('b_o', 256), ('b_z', 256),
    ('c_q', 256), ('c_f', 256), ('c_i', 256), ('c_z', 256),
    ('d_q', 256), ('d_kvc', 128), ('d_ksv', 128), ('d_kwv', 128), ('d_g', 12), ('d_z', 256),
)
N_COLS = sum(w for _, w in ORIG_SPLITS)
GATE_SLOTS = ('b_if', 'd_g')
NEW_LAYOUT = (
    ('a_q', 256), ('a_k', 128), ('a_v', 128), ('a_z', 256), ('b_q', 256), ('b_k', 256), ('b_v', 256),
    ('b_o', 256), ('b_z', 256), ('c_q', 256), ('c_f', 256), ('c_i', 256), ('c_z', 256),
    ('d_q', 256), ('d_kvc', 128), ('d_ksv', 128), ('d_kwv', 128), ('gates', 128), ('d_z', 256),
)
N_PROJ = sum(w for _, w in NEW_LAYOUT)
D_GATE_LANE = dict(ORIG_SPLITS)['b_if']


def _offset(layout, name):
    off = 0
    for n, w in layout:
        if n == name:
            return off
        off += w
    raise KeyError(name)


def _new_offset(name):
    return _offset(NEW_LAYOUT, name)


def _permuted_cols(load):
    sizes = dict(ORIG_SPLITS)
    for name, width in NEW_LAYOUT:
        new = _new_offset(name)
        if name != 'gates':
            old = _offset(ORIG_SPLITS, name)
            lo = old // LANES * LANES
            hi = min(-(-(old + width) // LANES) * LANES, N_COLS)
            yield new, load(lo, hi)[..., old - lo:old - lo + width]
            continue
        gates, lane0 = None, 0
        for slot in GATE_SLOTS:
            old = _offset(ORIG_SPLITS, slot)
            lo = old // LANES * LANES
            tile = load(lo, lo + LANES)
            if old - lo != lane0:
                tile = pltpu.roll(tile, (lane0 - (old - lo)) % LANES, axis=tile.ndim - 1)
            lane = _iota(tile.shape, tile.ndim - 1)
            part = jnp.where((lane >= lane0) & (lane < lane0 + sizes[slot]), tile, 0.0)
            gates = part if gates is None else gates + part
            lane0 += sizes[slot]
        yield new, gates


def _prep_body(w_ref, b_ref, wo_ref, bo_ref):
    for new, cols in _permuted_cols(lambda lo, hi: w_ref[0, :, lo:hi]):
        wo_ref[:, new:new + cols.shape[-1]] = cols.astype(BF16)
    for new, cols in _permuted_cols(lambda lo, hi: b_ref[0, :, lo:hi]):
        bo_ref[:, new:new + cols.shape[-1]] = cols


def _prep_in_weights(w_in, b_in, layer, rows=128):
    depth, d, n = w_in.shape
    return pl.pallas_call(
        _prep_body,
        grid=(d // rows,),
        in_specs=[pl.BlockSpec((1, rows, n), lambda r: (layer, r, 0)),
                  pl.BlockSpec((1, 1, n), lambda r: (layer, 0, 0))],
        out_specs=[pl.BlockSpec((rows, N_PROJ), lambda r: (r, 0)),
                   pl.BlockSpec((1, N_PROJ), lambda r: (0, 0))],
        out_shape=[jax.ShapeDtypeStruct((d, N_PROJ), BF16), jax.ShapeDtypeStruct((1, N_PROJ), F32)],
        compiler_params=_cparams(("arbitrary",)),
        name="prep_in_weights",
    )(w_in, b_in.reshape(depth, 1, n))


def _mm(a, b):
    return jnp.dot(a.astype(BF16), b.astype(BF16), preferred_element_type=F32)


def _mm_nt(a, b):
    return lax.dot_general(a.astype(BF16), b.astype(BF16), (((1,), (1,)), ((), ())),
                           preferred_element_type=F32)


def _mm_tn(a, b):
    return lax.dot_general(a.astype(BF16), b.astype(BF16), (((0,), (0,)), ((), ())),
                           preferred_element_type=F32)


def _dot_bf16(a, b):
    return jnp.dot(a, b, preferred_element_type=F32)


def _split_bf16(x, terms):
    out = []
    for _ in range(terms - 1):
        t = x.astype(BF16)
        out.append(t)
        x = x - t.astype(F32)
    out.append(x.astype(BF16))
    return out


def _mm_hi(a, b):
    ah, al = _split_bf16(a, 2)
    bh, bl = _split_bf16(b, 2)
    return _dot_bf16(ah, bh) + (_dot_bf16(ah, bl) + _dot_bf16(al, bh))


def _mm_sel(sel, x):
    sel = sel.astype(BF16)
    x1, x2, x3 = _split_bf16(x, 3)
    return _dot_bf16(sel, x1) + (_dot_bf16(sel, x2) + _dot_bf16(sel, x3))


def _sigmoid(x):
    return 1.0 / (1.0 + jnp.exp2(x * (-LOG2E)))


def _silu(x):
    return x * _sigmoid(x)


def _log_sigmoid(x):
    return jnp.minimum(x, 0.0) - jnp.log1p(jnp.exp(-jnp.abs(x)))


def _rope(x, cos, sin_up, sin_dn):
    w = x.shape[-1]
    up = pltpu.roll(x, w - HALF, axis=1)
    dn = pltpu.roll(x, HALF, axis=1)
    return x * cos + up * sin_up + dn * sin_dn


def _rope_bf16(x, cos, sin_up, sin_dn, perm):
    swapped = jnp.dot(x.astype(BF16), perm, preferred_element_type=F32)
    return x * cos + swapped * (sin_up + sin_dn)


def _rope_perm(width, rot_lanes):
    lane = np.arange(width)
    src = np.where(lane % HEAD_DIM < HALF, lane + HALF, lane - HALF)
    rotated = (lane % LANES) < rot_lanes
    perm = (np.arange(width)[:, None] == src[None, :]) & rotated[None, :]
    return jnp.asarray(perm.astype(np.float32), dtype=BF16)


def _rope_tables(pos, width, rot_lanes):
    inv = ROPE_THETA ** (-jnp.arange(HALF, dtype=F32) / HALF)
    ang = pos.astype(F32)[:, None] * inv[None, :]
    cos, sin = jnp.cos(ang), jnp.sin(ang)
    zero, one = jnp.zeros_like(sin), jnp.ones_like(cos)
    cos_h = jnp.concatenate([cos, cos], axis=1)
    up_h = jnp.concatenate([-sin, zero], axis=1)
    dn_h = jnp.concatenate([zero, sin], axis=1)
    id_c = jnp.concatenate([one, one], axis=1)
    id_s = jnp.concatenate([zero, zero], axis=1)
    cs, us, ds = [], [], []
    for l0 in range(0, width, HEAD_DIM):
        rot = (l0 % 128) < rot_lanes
        cs.append(cos_h if rot else id_c)
        us.append(up_h if rot else id_s)
        ds.append(dn_h if rot else id_s)
    return jnp.concatenate(cs, axis=1), jnp.concatenate(us, axis=1), jnp.concatenate(ds, axis=1)


def _iota(shape, dim):
    return lax.broadcasted_iota(jnp.int32, shape, dim)


def _cparams(sem):
    return pltpu.CompilerParams(dimension_semantics=sem, vmem_limit_bytes=VMEM_LIMIT)


def _ln_rows(x, g, b):
    mu = jnp.mean(x, axis=-1, keepdims=True)
    xc = x - mu
    var = jnp.mean(xc * xc, axis=-1, keepdims=True)
    return xc * lax.rsqrt(var + LN_EPS) * g + b


def _mix_out(rows, ya_ref, yb_ref, yc_ref, yd_ref, h_ref, w_ref, g_ref, b_ref):
    y = _mm(ya_ref[rows, :], w_ref[0:GROUP_W, :])
    y += _mm(yb_ref[rows, :], w_ref[GROUP_W:2 * GROUP_W, :])
    y += _mm(yc_ref[rows, :], w_ref[2 * GROUP_W:3 * GROUP_W, :])
    y += _mm(yd_ref[rows, :], w_ref[3 * GROUP_W:4 * GROUP_W, :])
    return _ln_rows(DN_ALPHA * h_ref[rows, :] + y, g_ref[...], b_ref[...])


def _sub_tiles(ref):
    return [slice(r, r + PROJ_SUB) for r in range(0, ref.shape[0], PROJ_SUB)]


def _ln_proj_body(x_ref, g_ref, b_ref, w_ref, bias_ref, h_ref, o_ref):
    for rows in _sub_tiles(x_ref):
        h = _ln_rows(x_ref[rows, :], g_ref[...], b_ref[...])
        h_ref[rows, :] = h
        o_ref[rows, :] = jnp.dot(h.astype(BF16), w_ref[...], preferred_element_type=F32) + bias_ref[...]


def _out_proj_body(ya_ref, yb_ref, yc_ref, yd_ref, hp_ref, wo_ref, g_ref, b_ref, w_ref, bias_ref, h_ref, o_ref):
    for rows in _sub_tiles(hp_ref):
        h = _mix_out(rows, ya_ref, yb_ref, yc_ref, yd_ref, hp_ref, wo_ref, g_ref, b_ref)
        h_ref[rows, :] = h
        o_ref[rows, :] = jnp.dot(h.astype(BF16), w_ref[...], preferred_element_type=F32) + bias_ref[...]


def _ln_in_proj(x2, g, b, w_bf, bias, tm=PROJ_TM):
    m, d = x2.shape
    n = w_bf.shape[1]
    row = lambda width: pl.BlockSpec((tm, width), lambda i: (i, 0))
    const = lambda shape: pl.BlockSpec(shape, lambda i: (0, 0))
    return pl.pallas_call(
        _ln_proj_body,
        grid=(m // tm,),
        in_specs=[row(d), const((1, d)), const((1, d)), const((d, n)), const((1, n))],
        out_specs=[row(d), row(n)],
        out_shape=[jax.ShapeDtypeStruct((m, d), F32), jax.ShapeDtypeStruct((m, n), F32)],
        compiler_params=_cparams(("parallel",)),
        name="ln_in_proj",
    )(x2, g.reshape(1, d), b.reshape(1, d), w_bf, bias.reshape(1, n))


def _out_in_proj(ys, h2, wo_bf, g, b, w_bf, bias, tm=PROJ_TM):
    m, d = h2.shape
    n = w_bf.shape[1]
    row = lambda width: pl.BlockSpec((tm, width), lambda i: (i, 0))
    const = lambda shape: pl.BlockSpec(shape, lambda i: (0, 0))
    return pl.pallas_call(
        _out_proj_body,
        grid=(m // tm,),
        in_specs=[row(GROUP_W)] * N_GROUPS + [row(d), const((N_GROUPS * GROUP_W, d)), const((1, d)), const((1, d)),
                                              const((d, n)), const((1, n))],
        out_specs=[row(d), row(n)],
        out_shape=[jax.ShapeDtypeStruct((m, d), F32), jax.ShapeDtypeStruct((m, n), F32)],
        compiler_params=_cparams(("parallel",)),
        name="out_in_proj",
    )(*[y.reshape(m, GROUP_W) for y in ys], h2, wo_bf, g.reshape(1, d), b.reshape(1, d), w_bf, bias.reshape(1, n))


def _out_body(ya_ref, yb_ref, yc_ref, yd_ref, h_ref, w_ref, g_ref, b_ref, o_ref):
    o_ref[...] = _mix_out(slice(None), ya_ref, yb_ref, yc_ref, yd_ref, h_ref, w_ref, g_ref, b_ref)


def _out_proj(ys, h2, w_bf, g, b, tm=512):
    m, d = h2.shape
    yspec = pl.BlockSpec((tm, GROUP_W), lambda i: (i, 0))
    return pl.pallas_call(
        _out_body,
        grid=(m // tm,),
        in_specs=[yspec, yspec, yspec, yspec,
                  pl.BlockSpec((tm, d), lambda i: (i, 0)),
                  pl.BlockSpec((4 * GROUP_W, d), lambda i: (0, 0)),
                  pl.BlockSpec((1, d), lambda i: (0, 0)),
                  pl.BlockSpec((1, d), lambda i: (0, 0))],
        out_specs=pl.BlockSpec((tm, d), lambda i: (i, 0)),
        out_shape=jax.ShapeDtypeStruct((m, d), F32),
        compiler_params=_cparams(("parallel",)),
        name="out_proj",
    )(*[y.reshape(m, GROUP_W) for y in ys], h2, w_bf, g.reshape(1, d), b.reshape(1, d))


def _mixer_a_body(sink_ref, q_ref, kv_ref, z_ref, cos_ref, up_ref, dn_ref, perm_ref, o_ref, k_ref, vt_ref):
    i = pl.program_id(1)
    seq = kv_ref.shape[1]
    n_kv = N_HEADS // A_GROUP
    low_half = _iota((1, LANES), 1) < HEAD_DIM

    @pl.when(i == 0)
    def _():
        def rope_rows(c, carry):
            for u in range(PRO_ROWS // QBLK):
                t0 = c * (PRO_ROWS // QBLK) + u
                rows = pl.ds(pl.multiple_of(t0 * QBLK, QBLK), QBLK)
                kr = _rope_bf16(kv_ref[0, rows, 0:LANES], cos_ref[rows, 0:LANES], up_ref[rows, 0:LANES],
                                dn_ref[rows, 0:LANES], perm_ref[0:LANES, 0:LANES])
                v_t = _mm_nt(_iota((LANES, LANES), 0) == _iota((LANES, LANES), 1), kv_ref[0, rows, LANES:2 * LANES])
                v_t = v_t.astype(BF16)
                ones_rows = jnp.ones((V_AUG - HEAD_DIM, QBLK), BF16)
                other = pltpu.roll(kr, HEAD_DIM, axis=1)
                for g in range(n_kv):
                    dup = jnp.where(low_half, kr, other) if g == 0 else jnp.where(low_half, other, kr)
                    k_ref[g, rows, :] = dup.astype(BF16)
                    vt_ref[g, t0] = jnp.concatenate([v_t[g * HEAD_DIM:(g + 1) * HEAD_DIM, :], ones_rows], axis=0)
            return carry
        lax.fori_loop(0, seq // PRO_ROWS, rope_rows, 0)

    n_blk = A_WINDOW // QBLK + 1
    span = n_blk * QBLK
    first_head = _iota((1, A_GROUP * QBLK), 1) < QBLK
    eye = (_iota((QBLK, QBLK), 0) == _iota((QBLK, QBLK), 1)).astype(BF16)

    def query_block(u):
        blk_i = i * A_SUB + u
        sub = slice(u * QBLK, (u + 1) * QBLK)
        q0 = pl.multiple_of(blk_i * QBLK, QBLK)
        qrows = pl.ds(q0, QBLK)
        qr = _rope(q_ref[0, sub, :], cos_ref[qrows, :], up_ref[qrows, :], dn_ref[qrows, :]) * (SCALE * LOG2E)
        kt0 = jnp.maximum(blk_i - A_WINDOW // QBLK, 0)
        k0 = pl.multiple_of(kt0 * QBLK, QBLK)
        rel = (q0 + _iota((1, QBLK), 1)) - (k0 + _iota((span, 1), 0))
        in_band = lax.bitcast_convert_type(rel, jnp.uint32) < A_WINDOW
        bias = jnp.concatenate([jnp.where(in_band, 0.0, NEG)] * A_GROUP, axis=1)
        outs = []
        for g in range(n_kv):
            q_pair = qr[:, g * LANES:(g + 1) * LANES]
            qg = jnp.concatenate([jnp.where(low_half, q_pair, 0.0), jnp.where(low_half, 0.0, q_pair)], axis=0)
            sink = jnp.where(first_head, sink_ref[g * A_GROUP], sink_ref[g * A_GROUP + 1]) * LOG2E
            s = _mm_nt(k_ref[g, pl.ds(k0, span), :], qg) + bias
            m = jnp.maximum(jnp.max(s, axis=0, keepdims=True), sink)
            p = jnp.exp2(s - m).astype(BF16)
            o = jnp.zeros((V_AUG, A_GROUP * QBLK), F32)
            for c in range(n_blk):
                o = o + jnp.dot(vt_ref[g, kt0 + c], p[c * QBLK:(c + 1) * QBLK], preferred_element_type=F32)
            o = o[0:HEAD_DIM] / (o[HEAD_DIM:HEAD_DIM + 1] + jnp.exp2(sink - m))
            outs.extend(o[:, r * QBLK:(r + 1) * QBLK] for r in range(A_GROUP))
        y = _mm_nt(eye, jnp.concatenate(outs, axis=0))
        o_ref[0, sub, :] = (y * _silu(z_ref[0, sub, :])).astype(Y_DTYPE)

    for u in range(A_SUB):
        query_block(u)


def _mixer_a(proj, sinks, tabs):
    assert A_GROUP == 2 and A_GROUP * HEAD_DIM == LANES, "a kv head's two query heads share one lane tile"
    bsz, seq, _ = proj.shape
    cos, up, dn = tabs
    blk = lambda name: _new_offset(name) // GROUP_W
    tspec = pl.BlockSpec((seq, GROUP_W), lambda b, i: (0, 0))
    return pl.pallas_call(
        _mixer_a_body,
        grid=(bsz, seq // (A_SUB * QBLK)),
        in_specs=[pl.BlockSpec(memory_space=pltpu.SMEM),
                  pl.BlockSpec((1, A_SUB * QBLK, GROUP_W), lambda b, i: (b, i, blk('a_q'))),
                  pl.BlockSpec((1, seq, GROUP_W), lambda b, i: (b, 0, blk('a_k'))),
                  pl.BlockSpec((1, A_SUB * QBLK, GROUP_W), lambda b, i: (b, i, blk('a_z'))),
                  tspec, tspec, tspec, pl.BlockSpec((GROUP_W, GROUP_W), lambda b, i: (0, 0))],
        out_specs=pl.BlockSpec((1, A_SUB * QBLK, GROUP_W), lambda b, i: (b, i, 0)),
        out_shape=jax.ShapeDtypeStruct((bsz, seq, GROUP_W), Y_DTYPE),
        scratch_shapes=[pltpu.VMEM((N_HEADS // A_GROUP, seq, LANES), BF16),
                        pltpu.VMEM((N_HEADS // A_GROUP, seq // QBLK, V_AUG, QBLK), BF16)],
        compiler_params=_cparams(("arbitrary", "arbitrary")),
        name="mixer_a",
    )(sinks, proj, proj, proj, cos, up, dn, _rope_perm(GROUP_W, LANES))


B_CHUNK = 128
B_AUG = V_AUG


def _conv_silu(x_ref, r0, prev, w, b):
    L = B_CHUNK
    x = x_ref[0, r0:r0 + L, :]
    acc = b + w[B_CONV - 1:B_CONV, :] * x
    for s in range(1, B_CONV):
        if r0 >= s:
            shifted = x_ref[0, r0 - s:r0 - s + L, :]
        else:
            row8 = _iota((8, x.shape[1]), 0)
            xs = pltpu.roll(x, s, axis=0)
            ps = pltpu.roll(prev, s, axis=0)
            shifted = jnp.concatenate([jnp.where(row8 < s, ps, xs[0:8]), xs[8:]], axis=0)
        acc = acc + w[B_CONV - 1 - s:B_CONV - s, :] * shifted
    return _silu(acc)


def _mixer_b_init(xprev_ref, c_ref, m_ref):
    xprev_ref[...] = jnp.zeros_like(xprev_ref)
    c_ref[...] = jnp.zeros_like(c_ref)
    m_ref[...] = jnp.zeros_like(m_ref)


def _mixer_b_main(xq_ref, xk_ref, v_ref, og_ref, z_ref, if_ref, cw_ref, cb_ref, o_ref, xprev_ref, c_ref, m_ref):
    L = B_CHUNK
    tri = (_iota((L, L), 1) <= _iota((L, L), 0)).astype(F32)
    key_first = _iota((L, L), 0) <= _iota((L, L), 1)
    half_of_lane = _iota((1, 128), 1) // HEAD_DIM
    ones_rows = jnp.ones((B_AUG - HEAD_DIM, L), F32)

    def chunk(r0, state):
        rows = slice(r0, r0 + L)
        qc = _conv_silu(xq_ref, r0, xprev_ref[:, 0:GROUP_W], cw_ref[:, 0:GROUP_W], cb_ref[:, 0:GROUP_W]) * SCALE
        kc = _conv_silu(xk_ref, r0, xprev_ref[:, GROUP_W:], cw_ref[:, GROUP_W:], cb_ref[:, GROUP_W:])
        q_t = qc.T
        v_t = v_ref[0, rows, :].T

        gates = if_ref[0, rows, :]
        bcum = _mm_sel(tri, _log_sigmoid(gates))
        gates_t = gates.T
        bcum_t = bcum.T
        outs, new_state = [], []
        for h in range(N_HEADS):
            pair = slice((h // 2) * 128, (h // 2 + 1) * 128)
            own = half_of_lane == (h % 2)
            k_pair = kc[:, pair]
            q_pair_t = q_t[pair, :]
            i_row = gates_t[h:h + 1, :]
            b_row = bcum_t[N_HEADS + h:N_HEADS + h + 1, :]
            c_col = gates[:, h:h + 1] - bcum[:, N_HEADS + h:N_HEADS + h + 1]
            cst, m_prev = state[h]
            dmat = jnp.where(key_first, b_row + c_col, NEG)
            inter = b_row + m_prev
            m_t = jnp.maximum(jnp.max(dmat, axis=0, keepdims=True), inter)
            smat = _mm(jnp.where(own, k_pair, 0.0), q_pair_t) * jnp.exp(dmat - m_t)
            vaug_t = jnp.concatenate([v_t[h * HEAD_DIM:(h + 1) * HEAD_DIM, :], ones_rows], axis=0)
            tot = _mm(vaug_t, smat) + jnp.exp(inter - m_t) * _mm(cst, q_pair_t)
            num = tot[0:HEAD_DIM, :]
            den = tot[HEAD_DIM:HEAD_DIM + 1, :]
            outs.append(num / jnp.maximum(jnp.abs(den), jnp.exp(-m_t)))
            b_last = b_row[:, L - 1:L]
            d_row = b_last - b_row + i_row
            m_new = jnp.maximum(b_last + m_prev, jnp.max(d_row, axis=1, keepdims=True))
            w_row = jnp.exp(d_row - m_new)
            decay = jnp.exp(b_last + m_prev - m_new)
            new_state.append((decay * cst + jnp.where(own, _mm(vaug_t * w_row, k_pair), 0.0), m_new))
        o_ref[0, rows, :] = (_sigmoid(og_ref[0, rows, :]) * jnp.concatenate(outs, axis=0).T
                             * _silu(z_ref[0, rows, :])).astype(Y_DTYPE)
        return new_state

    rows_step = xq_ref.shape[1]
    state = [(c_ref[h], m_ref[h:h + 1, 0:1]) for h in range(N_HEADS)]
    for u in range(rows_step // L):
        state = chunk(u * L, state)
        yield
    xprev_ref[:, 0:GROUP_W] = xq_ref[0, rows_step - 8:rows_step, :]
    xprev_ref[:, GROUP_W:] = xk_ref[0, rows_step - 8:rows_step, :]
    for h in range(N_HEADS):
        c_ref[h] = state[h][0]
        m_ref[h:h + 1, :] = jnp.broadcast_to(state[h][1], (1, 128))


def _mixer_c_main(layer, q_ref, f_ref, i_ref, z_ref, lb_ref, gn_ref, tri_ref, piv_ref, o_ref, st_ref):
    L = C_CHUNK
    nsub = L // C_SUB

    raw = lb_ref[...]
    ex = jnp.exp(raw - jnp.max(raw, axis=0, keepdims=True))
    if layer == 0:
        lb = jnp.zeros((1, GROUP_W), F32)
    else:
        lb = jnp.sum(ex[1:layer + 1], axis=0, keepdims=True) / jnp.sum(ex, axis=0, keepdims=True)
    tri = (_iota((L, L), 1) <= _iota((L, L), 0)).astype(F32)
    head_of_lane = _iota((1, GROUP_W), 1) // HEAD_DIM
    same_head = ((_iota((GROUP_W, GROUP_W), 0) // HEAD_DIM)
                 == (_iota((GROUP_W, GROUP_W), 1) // HEAD_DIM))
    ones_bd = same_head.astype(BF16)

    def chunk(rows, st):
        f = lb + (1.0 - lb) * _sigmoid(f_ref[0, rows, :])
        logf = jnp.log(f)
        k = 1.0 - f
        q = _silu(q_ref[0, rows, :])
        v = i_ref[0, rows, :]
        a = _mm_sel(tri, logf) * LOG2E
        o = _mm_nt(q * jnp.exp2(a), st)

        qsts, kts, vreps = [], [], []
        for i in range(1, nsub):
            r0 = i * C_SUB
            piv = a[r0 - 1:r0, :]
            qt = q[r0:r0 + C_SUB] * jnp.exp2(a[r0:r0 + C_SUB] - piv)
            qsts.extend(jnp.where(head_of_lane == h, qt, 0.0) for h in range(N_HEADS))
            kts.append((k[0:r0] * jnp.exp2(piv - a[0:r0])).astype(BF16))
            vreps.append(v[0:r0].astype(BF16))
        scores = _mm_nt(jnp.concatenate(qsts, axis=0), jnp.concatenate(kts, axis=0))
        r = _dot_bf16((scores * piv_ref[...]).astype(BF16), jnp.concatenate(vreps, axis=0))
        pieces = [o[0:C_SUB]]
        for i in range(1, nsub):
            acc = o[i * C_SUB:(i + 1) * C_SUB]
            for h in range(N_HEADS):
                g0 = ((i - 1) * N_HEADS + h) * C_SUB
                acc = acc + jnp.where(head_of_lane == h, r[g0:g0 + C_SUB], 0.0)
            pieces.append(acc)
        o = jnp.concatenate(pieces, axis=0)

        a3 = a.reshape(nsub, C_SUB, GROUP_W)
        q3 = q.reshape(nsub, C_SUB, GROUP_W)
        k3 = k.reshape(nsub, C_SUB, GROUP_W)
        v3 = v.reshape(nsub, C_SUB, GROUP_W)
        ps = []
        for s in range(C_SUB):
            e = jnp.exp2(a3 - a3[:, s:s + 1, :] + tri_ref[s])
            ps.append((q3 * k3[:, s:s + 1, :] * e).reshape(L, GROUP_W).astype(BF16))
        w = jnp.dot(jnp.concatenate(ps, axis=0), ones_bd, preferred_element_type=F32)
        for s in range(C_SUB):
            vs = jnp.broadcast_to(v3[:, s:s + 1, :], (nsub, C_SUB, GROUP_W)).reshape(L, GROUP_W)
            o = o + w[s * L:(s + 1) * L] * vs

        a_last = a[L - 1:L, :]
        kdec = k * jnp.exp2(a_last - a)
        st_new = st * jnp.exp2(a_last) + jnp.where(same_head, _mm_tn(v, kdec), 0.0)

        ms = jnp.dot((o * o).astype(BF16), ones_bd, preferred_element_type=F32) * (1.0 / HEAD_DIM)
        o = o * lax.rsqrt(ms + C_EPS) * gn_ref[...]
        o_ref[0, rows, :] = (o * _silu(z_ref[0, rows, :])).astype(Y_DTYPE)
        return st_new

    st = st_ref[...]
    for u in range(q_ref.shape[1] // L):
        st = chunk(slice(u * L, (u + 1) * L), st)
        yield
    st_ref[...] = st


N_B_IN, N_C_IN, N_B_SCRATCH = 8, 8, 3
C_PER_B = B_CHUNK // C_CHUNK


def _mixer_bc_body(layer, *refs):
    b_in = refs[:N_B_IN]
    c_in = refs[N_B_IN:N_B_IN + N_C_IN]
    ob_ref, oc_ref = refs[N_B_IN + N_C_IN:N_B_IN + N_C_IN + 2]
    scratch = refs[N_B_IN + N_C_IN + 2:]
    b_scratch, c_scratch = scratch[:N_B_SCRATCH], scratch[N_B_SCRATCH:]

    @pl.when(pl.program_id(1) == 0)
    def _():
        _mixer_b_init(*b_scratch)
        c_scratch[0][...] = jnp.zeros_like(c_scratch[0])

    pending = [_mixer_b_main(*b_in, ob_ref, *b_scratch), _mixer_c_main(layer, *c_in, oc_ref, *c_scratch)]
    weights = [1, C_PER_B]
    while pending:
        for gen, reps in list(zip(pending, weights)):
            for _ in range(reps):
                if next(gen, StopIteration) is StopIteration:
                    idx = pending.index(gen)
                    del pending[idx], weights[idx]
                    break


def _mixer_bc(proj, conv_w, conv_b, c_lb, norm_g, layer):
    bsz, seq, _ = proj.shape
    L = BC_STEP
    blk = lambda name: _new_offset(name) // GROUP_W
    spec = lambda name: pl.BlockSpec((1, L, GROUP_W), lambda b, c: (b, c, blk(name)))
    vec = pl.BlockSpec((1, GROUP_W), lambda b, c: (0, 0))
    t_ge_s = np.arange(C_SUB)[None, :, None] >= np.arange(C_SUB)[:, None, None]
    causal_bias = np.broadcast_to(np.where(t_ge_s, 0.0, NEG), (C_SUB, C_SUB, GROUP_W)).astype(np.float32)
    pivots = np.arange(1, C_CHUNK // C_SUB)
    row_piv = np.repeat(pivots, N_HEADS * C_SUB)
    col_piv = np.repeat(pivots, pivots * C_SUB)
    same_pivot = (row_piv[:, None] == col_piv[None, :]).astype(np.float32)
    out_spec = pl.BlockSpec((1, L, GROUP_W), lambda b, c: (b, c, 0))
    out_shape = jax.ShapeDtypeStruct((bsz, seq, GROUP_W), Y_DTYPE)
    return pl.pallas_call(
        functools.partial(_mixer_bc_body, layer),
        grid=(bsz, seq // L),
        in_specs=[spec('b_q'), spec('b_k'), spec('b_v'), spec('b_o'), spec('b_z'),
                  pl.BlockSpec((1, L, LANES), lambda b, c: (b, c, _new_offset('gates') // LANES)),
                  pl.BlockSpec((B_CONV, 2 * GROUP_W), lambda b, c: (0, 0)),
                  pl.BlockSpec((1, 2 * GROUP_W), lambda b, c: (0, 0)),
                  spec('c_q'), spec('c_f'), spec('c_i'), spec('c_z'),
                  pl.BlockSpec((DEPTH, GROUP_W), lambda b, c: (0, 0)), vec,
                  pl.BlockSpec((C_SUB, C_SUB, GROUP_W), lambda b, c: (0, 0, 0)),
                  pl.BlockSpec(same_pivot.shape, lambda b, c: (0, 0))],
        out_specs=[out_spec, out_spec],
        out_shape=[out_shape, out_shape],
        scratch_shapes=[pltpu.VMEM((8, 2 * GROUP_W), F32),
                        pltpu.VMEM((N_HEADS, B_AUG, 128), F32),
                        pltpu.VMEM((8, 128), F32),
                        pltpu.VMEM((GROUP_W, GROUP_W), F32)],
        compiler_params=_cparams(("arbitrary", "arbitrary")),
        name="mixer_bc",
    )(proj, proj, proj, proj, proj, proj, conv_w, conv_b.reshape(1, -1),
      proj, proj, proj, proj, c_lb, norm_g.reshape(1, -1), jnp.asarray(causal_bias), jnp.asarray(same_pivot))


N_CMP_PAD = 128
KT = 512
D_QBLK = 512
WBLK = 256
PE_ROWS = 16
SEL_ROWS = 16


def _mixer_d_body(q_ref, z_ref, ksv_ref, kwv_ref, cin_ref, g_ref, cos_ref, up_ref, dn_ref, kcos_ref, kup_ref,
                  kdn_ref, kperm_ref, ccos_ref, cup_ref, cdn_ref, w1_ref, pe_ref, w2_ref, ov_ref, o_ref,
                  ks_ref, kw_ref, vts_ref, vtw_ref, kc_ref, vct_ref, sel_ref, qaug_ref):
    QBLK = D_QBLK
    i = pl.program_id(1)
    seq = ksv_ref.shape[1]
    half_cmp = CMP_LEN // 2

    @pl.when(i == 0)
    def _():
        def rope_rows(c, carry):
            for u in range(PRO_ROWS // KT):
                t0 = c * (PRO_ROWS // KT) + u
                rows = pl.ds(pl.multiple_of(t0 * KT, KT), KT)
                kvs_in = ksv_ref[0, rows, :]
                kvw_in = kwv_ref[0, rows, :]
                kvs = _rope_bf16(kvs_in, kcos_ref[rows, :], kup_ref[rows, :], kdn_ref[rows, :], kperm_ref[...])
                kvw = _rope_bf16(kvw_in, kcos_ref[rows, :], kup_ref[rows, :], kdn_ref[rows, :], kperm_ref[...])
                blk_in_tile = _iota((KT, HEAD_DIM), 0) // SLC_LEN
                onehot = jnp.where(blk_in_tile == _iota((KT, HEAD_DIM), 1), 1.0, 0.0)
                ks_ref[rows, :] = jnp.concatenate([kvs[:, 0:HEAD_DIM], onehot], axis=1).astype(BF16)
                kw_ref[rows, :] = kvw[:, 0:HEAD_DIM].astype(BF16)
                ones_rows = jnp.ones((V_AUG - HEAD_DIM, KT), BF16)
                pick_v = _iota((HEAD_DIM, LANES), 1) == _iota((HEAD_DIM, LANES), 0) + HEAD_DIM
                vts_ref[t0] = jnp.concatenate([_mm_nt(pick_v, kvs_in).astype(BF16), ones_rows], axis=0)
                vwt = jnp.concatenate([_mm_nt(pick_v, kvw_in).astype(BF16), ones_rows], axis=0)
                for w in range(KT // WBLK):
                    vtw_ref[t0 * (KT // WBLK) + w] = vwt[:, w * WBLK:(w + 1) * WBLK]
            return carry
        lax.fori_loop(0, seq // PRO_ROWS, rope_rows, 0)
        acc = jnp.zeros((N_CMP_PAD + PE_ROWS, 2 * LANES), F32)
        for rr in range(half_cmp // 2):
            xa = cin_ref[0, pl.ds(2 * rr, N_CMP_PAD, stride=CMP_STRIDE), :]
            xb = cin_ref[0, pl.ds(2 * rr + 1, N_CMP_PAD, stride=CMP_STRIDE), :]
            lhs = jnp.concatenate([jnp.concatenate([xa, xb], axis=1), pe_ref[rr]], axis=0)
            acc = acc + _mm_hi(lhs, w1_ref[rr])
        u0 = acc[0:N_CMP_PAD, 0:LANES] + acc[N_CMP_PAD:N_CMP_PAD + 1, 0:LANES]
        u1 = acc[0:N_CMP_PAD, LANES:] + acc[N_CMP_PAD + 1:N_CMP_PAD + 2, LANES:]
        pre = u0 + pltpu.roll(u1, N_CMP_PAD - 1, axis=0)
        cv = _rope(_mm_hi(_silu(pre), w2_ref[...]), ccos_ref[...], cup_ref[...], cdn_ref[...])
        kc_ref[...] = cv[:, 0:HEAD_DIM]
        vct_ref[...] = cv.T[HEAD_DIM:2 * HEAD_DIM, :].astype(BF16)

    q0 = pl.multiple_of(i * QBLK, QBLK)
    qrows = pl.ds(q0, QBLK)
    qr = _rope(q_ref[0], cos_ref[qrows, :], up_ref[qrows, :], dn_ref[qrows, :]) * (SCALE * LOG2E)
    qt = qr.T
    qst = jnp.concatenate([qt[h * HEAD_DIM:(h + 1) * HEAD_DIM] for h in range(N_HEADS)], axis=1)
    qst_bf = qst.astype(BF16)
    cols4 = N_HEADS * QBLK
    qpos = q0 + _iota((1, QBLK), 1)

    def lanes4(x):
        return jnp.concatenate([x] * N_HEADS, axis=1)

    sc = _mm_hi(kc_ref[...], qst)
    ends = _iota((N_CMP_PAD, 1), 0) * CMP_STRIDE + (CMP_LEN - 1)
    scm = sc + lanes4(jnp.where(ends <= qpos, 0.0, NEG))
    e = jnp.exp2(scm - jnp.max(scm, axis=0, keepdims=True))
    l = jnp.sum(e, axis=0, keepdims=True)
    any_valid = lanes4(jnp.where(qpos >= CMP_LEN - 1, 1.0, 0.0))
    pc = e * (any_valid / l)
    o_cmp = jnp.dot(vct_ref[...], pc.astype(BF16), preferred_element_type=F32)
    psum = pc[:, 0:QBLK] + pc[:, QBLK:2 * QBLK] + pc[:, 2 * QBLK:3 * QBLK] + pc[:, 3 * QBLK:4 * QBLK]

    n_slc = seq // SLC_LEN
    imp = _mm_sel(ov_ref[...], psum)
    jl = _iota((n_slc, 1), 0)
    cur = jnp.right_shift(qpos, SLC_LEN.bit_length() - 1)
    imp = jnp.where(jl == 0, FORCE_SCORE, imp)
    imp = jnp.where(jl == cur, FORCE_SCORE, imp)
    imp = jnp.where(jl == cur - 1, FORCE_SCORE, imp)
    imp = jnp.where(jl <= cur, imp, NEG)
    rank = jnp.zeros((n_slc, QBLK), F32)
    for jp in range(n_slc):
        cj = imp[jp:jp + 1, :]
        tie = jnp.where(jl > jp, 1.0, 0.0)
        rank = rank + jnp.where(cj > imp, 1.0, jnp.where(cj == imp, tie, 0.0))
    sel_bias = jnp.where(rank < N_SEL, 0.0, NEG)
    blocks_per_tile = KT // SLC_LEN
    for t in range(n_slc // blocks_per_tile):
        sel_ref[t] = sel_bias[t * blocks_per_tile:(t + 1) * blocks_per_tile, :]

    n_win = NSA_WINDOW // WBLK + 1
    span = n_win * WBLK
    win_parts = []
    for u in range(QBLK // WBLK):
        qcols = [slice(h * QBLK + u * WBLK, h * QBLK + (u + 1) * WBLK) for h in range(N_HEADS)]
        q_sub = jnp.concatenate([qst_bf[:, c] for c in qcols], axis=1)
        kt0 = jnp.maximum(i * (QBLK // WBLK) + u - NSA_WINDOW // WBLK, 0)
        k0 = pl.multiple_of(kt0 * WBLK, WBLK)
        rel = (q0 + u * WBLK + _iota((1, WBLK), 1)) - (k0 + _iota((span, 1), 0))
        in_band = lax.bitcast_convert_type(rel, jnp.uint32) < NSA_WINDOW
        s = jnp.dot(kw_ref[pl.ds(k0, span), :], q_sub, preferred_element_type=F32)
        s = s + lanes4(jnp.where(in_band, 0.0, NEG))
        p = jnp.exp2(s - jnp.max(s, axis=0, keepdims=True)).astype(BF16)
        acc = jnp.zeros((V_AUG, N_HEADS * WBLK), F32)
        for c in range(n_win):
            acc = acc + jnp.dot(vtw_ref[kt0 + c], p[c * WBLK:(c + 1) * WBLK], preferred_element_type=F32)
        win_parts.append(acc[0:HEAD_DIM] / acc[HEAD_DIM:HEAD_DIM + 1])
    o_win = jnp.concatenate([part[:, h * WBLK:(h + 1) * WBLK] for h in range(N_HEADS) for part in win_parts], axis=1)

    qaug_ref[0:HEAD_DIM, :] = qst_bf
    qaug_ref[HEAD_DIM:2 * HEAD_DIM, :] = jnp.zeros((HEAD_DIM, cols4), BF16)

    def sel_scores(kt):
        picks = jnp.concatenate([sel_ref[kt], jnp.zeros((SEL_ROWS - blocks_per_tile, QBLK), F32)], axis=0)
        qaug_ref[HEAD_DIM:HEAD_DIM + SEL_ROWS, :] = lanes4(picks).astype(BF16)
        kbase = pl.multiple_of(kt * KT, KT)
        return jnp.dot(ks_ref[pl.ds(kbase, KT), :], qaug_ref[...], preferred_element_type=F32)

    def sel_update(kt, s, m_old, acc):
        m_new = jnp.maximum(m_old, jnp.max(s, axis=0, keepdims=True))
        p = jnp.exp2(s - m_new)
        acc_new = jnp.exp2(m_old - m_new) * acc + jnp.dot(vts_ref[kt], p.astype(BF16), preferred_element_type=F32)
        return m_new, acc_new

    init = (jnp.full((1, cols4), NEG, F32), jnp.zeros((V_AUG, cols4), F32))
    last = (i + KT // QBLK) // (KT // QBLK) - 1
    m_s, acc_s = lax.fori_loop(0, last, lambda kt, c: sel_update(kt, sel_scores(kt), *c), init)
    causal = jnp.where((last * KT + _iota((KT, 1), 0)) <= qpos, 0.0, NEG)
    _, acc_s = sel_update(last, sel_scores(last) + lanes4(causal), m_s, acc_s)
    o_slc = acc_s[0:HEAD_DIM] / acc_s[HEAD_DIM:HEAD_DIM + 1]

    gate = _sigmoid(g_ref[0]).T
    outs = []
    for h in range(N_HEADS):
        cs = slice(h * QBLK, (h + 1) * QBLK)
        g0 = D_GATE_LANE + 3 * h
        outs.append(gate[g0:g0 + 1, :] * o_cmp[:, cs] + gate[g0 + 1:g0 + 2, :] * o_slc[:, cs]
                    + gate[g0 + 2:g0 + 3, :] * o_win[:, cs])
    o_ref[0] = (jnp.concatenate(outs, axis=0).T * _silu(z_ref[0])).astype(Y_DTYPE)


def _mixer_d(proj, qtabs, ktabs, ctabs, w1bd, pe2, w2bd, overlap):
    QBLK = D_QBLK
    bsz, seq, _ = proj.shape
    blk = lambda name: _new_offset(name) // GROUP_W
    full = lambda shape: pl.BlockSpec(shape, lambda b, i: (0,) * len(shape))
    return pl.pallas_call(
        _mixer_d_body,
        grid=(bsz, seq // QBLK),
        in_specs=[pl.BlockSpec((1, QBLK, GROUP_W), lambda b, i: (b, i, blk('d_q'))),
                  pl.BlockSpec((1, QBLK, GROUP_W), lambda b, i: (b, i, blk('d_z'))),
                  pl.BlockSpec((1, seq, 128), lambda b, i: (b, 0, _new_offset('d_ksv') // 128)),
                  pl.BlockSpec((1, seq, 128), lambda b, i: (b, 0, _new_offset('d_kwv') // 128)),
                  pl.BlockSpec((1, seq, 128), lambda b, i: (b, 0, _new_offset('d_kvc') // 128)),
                  pl.BlockSpec((1, QBLK, LANES), lambda b, i: (b, i, _new_offset('gates') // LANES)),
                  full((seq, GROUP_W)), full((seq, GROUP_W)), full((seq, GROUP_W)),
                  full((seq, 128)), full((seq, 128)), full((seq, 128)), full((LANES, LANES)),
                  full((N_CMP_PAD, 128)), full((N_CMP_PAD, 128)), full((N_CMP_PAD, 128)),
                  full((CMP_LEN // 4, 2 * LANES, 2 * LANES)), full((CMP_LEN // 4, PE_ROWS, 2 * LANES)), full((128, 128)),
                  full((seq // SLC_LEN, N_CMP_PAD))],
        out_specs=pl.BlockSpec((1, QBLK, GROUP_W), lambda b, i: (b, i, 0)),
        out_shape=jax.ShapeDtypeStruct((bsz, seq, GROUP_W), Y_DTYPE),
        scratch_shapes=[pltpu.VMEM((seq, 2 * HEAD_DIM), BF16),
                        pltpu.VMEM((seq, HEAD_DIM), BF16),
                        pltpu.VMEM((seq // KT, V_AUG, KT), BF16),
                        pltpu.VMEM((seq // WBLK, V_AUG, WBLK), BF16),
                        pltpu.VMEM((N_CMP_PAD, HEAD_DIM), F32),
                        pltpu.VMEM((HEAD_DIM, N_CMP_PAD), BF16),
                        pltpu.VMEM((seq // KT, KT // SLC_LEN, QBLK), F32),
                        pltpu.VMEM((2 * HEAD_DIM, N_HEADS * QBLK), BF16)],
        compiler_params=_cparams(("arbitrary", "arbitrary")),
        name="mixer_d",
    )(proj, proj, proj, proj, proj, proj, *qtabs, *ktabs, _rope_perm(LANES, HEAD_DIM), *ctabs, w1bd, pe2, w2bd,
      overlap)


def _cmp_params(pe, w1, w2):
    w1 = w1.reshape(2, CMP_LEN, HEAD_DIM, HEAD_DIM)
    z = jnp.zeros((CMP_LEN, HEAD_DIM, HEAD_DIM), F32)
    w1bd = jnp.concatenate([jnp.concatenate([w1[0], z], axis=2), jnp.concatenate([z, w1[1]], axis=2)], axis=1)
    z2 = jnp.zeros((HEAD_DIM, HEAD_DIM), F32)
    w2bd = jnp.concatenate([jnp.concatenate([w2[0], z2], axis=1), jnp.concatenate([z2, w2[1]], axis=1)], axis=0)
    pe2 = jnp.concatenate([pe[0], pe[1]], axis=1)
    half = CMP_LEN // 2
    w1pair = jnp.stack([jnp.concatenate([jnp.concatenate([w1bd[r], w1bd[half + r]], axis=1),
                                         jnp.concatenate([w1bd[r + 1], w1bd[half + r + 1]], axis=1)], axis=0)
                        for r in range(0, half, 2)])
    pe_rows = jnp.stack([jnp.concatenate([jnp.concatenate([pe2[r], pe2[r + 1]])[None, :],
                                          jnp.concatenate([pe2[half + r], pe2[half + r + 1]])[None, :],
                                          jnp.zeros((PE_ROWS - 2, 2 * LANES), F32)], axis=0)
                         for r in range(0, half, 2)])
    return w1pair, pe_rows, w2bd


def _overlap_matrix(seq):
    n_cmp = (seq - CMP_LEN) // CMP_STRIDE + 1
    starts = np.arange(N_CMP_PAD) * CMP_STRIDE
    blk = np.arange(seq // SLC_LEN)
    ov = ((starts[None, :] < (blk[:, None] + 1) * SLC_LEN) & (starts[None, :] + CMP_LEN > blk[:, None] * SLC_LEN))
    ov = ov & (np.arange(N_CMP_PAD)[None, :] < n_cmp)
    return jnp.asarray(ov.astype(np.float32))


def kernel(x, ln0_g, ln0_b, w_in, b_in, a_sinks, b_conv_w, b_conv_b, c_lb, c_norm_g, d_cmp_pe, d_cmp_w1, d_cmp_w2,
           w_out, ln_g, ln_b):
    bsz, seq, d = x.shape
    pos = jnp.arange(seq)
    qtabs = _rope_tables(pos, GROUP_W, 128)
    ktabs = _rope_tables(pos, 128, HEAD_DIM)
    ctabs = _rope_tables(jnp.arange(N_CMP_PAD) * CMP_STRIDE + (CMP_LEN - 1), 128, HEAD_DIM)
    overlap = _overlap_matrix(seq)

    w_p, b_p = _prep_in_weights(w_in, b_in, 0)
    h, proj = _ln_in_proj(x.reshape(bsz * seq, d), ln0_g, ln0_b, w_p, b_p)
    for l in range(DEPTH):
        proj = proj.reshape(bsz, seq, N_PROJ)
        y_a = _mixer_a(proj, a_sinks[l], qtabs)
        y_b, y_c = _mixer_bc(proj, b_conv_w[l], b_conv_b[l], c_lb, c_norm_g[l], l)
        w1bd, pe2, w2bd = _cmp_params(d_cmp_pe[l], d_cmp_w1[l], d_cmp_w2[l])
        y_d = _mixer_d(proj, qtabs, ktabs, ctabs, w1bd, pe2, w2bd, overlap)
        ys = (y_a, y_b, y_c, y_d)
        if l + 1 < DEPTH:
            w_p, b_p = _prep_in_weights(w_in, b_in, l + 1)
            h, proj = _out_in_proj(ys, h, w_out[l].astype(BF16), ln_g[l], ln_b[l], w_p, b_p)
        else:
            h = _out_proj(ys, h, w_out[l].astype(BF16), ln_g[l], ln_b[l])
    return h.reshape(bsz, seq, d)
```

```python
import functools

import numpy as np
import jax
import jax.numpy as jnp
from jax import lax
from jax.experimental import pallas as pl
from jax.experimental.pallas import tpu as pltpu

F32 = jnp.float32
BF16 = jnp.bfloat16
Y_DTYPE = BF16

DEPTH = 2
HEAD_DIM = 64
HALF = HEAD_DIM // 2
N_HEADS = 4
GROUP_W = N_HEADS * HEAD_DIM
N_GROUPS = 4
ROPE_THETA = 10000.0
NEG = -1e30
LN_EPS = 1e-5
A_WINDOW = 128
A_GROUP = 2
A_SUB = 16
V_AUG = HEAD_DIM + 16
LOG2E = 1.4426950408889634
PRO_ROWS = 512
B_CONV = 4
C_CHUNK = 64
C_SUB = 8
BC_STEP = 1024
C_EPS = 1e-6
CMP_LEN = 32
CMP_STRIDE = 16
SLC_LEN = 64
N_SEL = 8
NSA_WINDOW = 512
FORCE_SCORE = 1e6
DN_ALPHA = (2.0 * DEPTH) ** 0.25
QBLK = 128
SCALE = HEAD_DIM ** -0.5

VMEM_LIMIT = 56 * 1024 * 1024
PROJ_TM = 512
PROJ_SUB = 256

LANES = 128

ORIG_SPLITS = (
    ('a_q', 256), ('a_k', 128), ('a_v', 128), ('a_z', 256),
    ('b_q', 256), ('b_k', 256), ('b_v', 256), ('b_if', 8), ('b_o', 256), ('b_z', 256),
    ('c_q', 256), ('c_f', 256), ('c_i', 256), ('c_z', 256),
    ('d_q', 256), ('d_kvc', 128), ('d_ksv', 128), ('d_kwv', 128), ('d_g', 12), ('d_z', 256),
)
N_COLS = sum(w for _, w in ORIG_SPLITS)
GATE_SLOTS = ('b_if', 'd_g')
NEW_LAYOUT = (
    ('a_q', 256), ('a_k', 128), ('a_v', 128), ('a_z', 256), ('b_q', 256), ('b_k', 256), ('b_v', 256),
    ('b_o', 256), ('b_z', 256), ('c_q', 256), ('c_f', 256), ('c_i', 256), ('c_z', 256),
    ('d_q', 256), ('d_kvc', 128), ('d_ksv', 128), ('d_kwv', 128), ('gates', 128), ('d_z', 256),
)
N_PROJ = sum(w for _, w in NEW_LAYOUT)
D_GATE_LANE = dict(ORIG_SPLITS)['b_if']


def _offset(layout, name):
    off = 0
    for n, w in layout:
        if n == name:
            return off
        off += w
    raise KeyError(name)


def _new_offset(name):
    return _offset(NEW_LAYOUT, name)


def _permuted_cols(load):
    sizes = dict(ORIG_SPLITS)
    for name, width in NEW_LAYOUT:
        new = _new_offset(name)
        if name != 'gates':
            old = _offset(ORIG_SPLITS, name)
            lo = old // LANES * LANES
            hi = min(-(-(old + width) // LANES) * LANES, N_COLS)
            yield new, load(lo, hi)[..., old - lo:old - lo + width]
            continue
        gates, lane0 = None, 0
        for slot in GATE_SLOTS:
            old = _offset(ORIG_SPLITS, slot)
            lo = old // LANES * LANES
            tile = load(lo, lo + LANES)
            if old - lo != lane0:
                tile = pltpu.roll(tile, (lane0 - (old - lo)) % LANES, axis=tile.ndim - 1)
            lane = _iota(tile.shape, tile.ndim - 1)
            part = jnp.where((lane >= lane0) & (lane < lane0 + sizes[slot]), tile, 0.0)
            gates = part if gates is None else gates + part
            lane0 += sizes[slot]
        yield new, gates


def _prep_body(w_ref, b_ref, wo_ref, bo_ref):
    for new, cols in _permuted_cols(lambda lo, hi: w_ref[0, :, lo:hi]):
        wo_ref[:, new:new + cols.shape[-1]] = cols.astype(BF16)
    for new, cols in _permuted_cols(lambda lo, hi: b_ref[0, :, lo:hi]):
        bo_ref[:, new:new + cols.shape[-1]] = cols


def _prep_in_weights(w_in, b_in, layer, rows=128):
    depth, d, n = w_in.shape
    return pl.pallas_call(
        _prep_body,
        grid=(d // rows,),
        in_specs=[pl.BlockSpec((1, rows, n), lambda r: (layer, r, 0)),
                  pl.BlockSpec((1, 1, n), lambda r: (layer, 0, 0))],
        out_specs=[pl.BlockSpec((rows, N_PROJ), lambda r: (r, 0)),
                   pl.BlockSpec((1, N_PROJ), lambda r: (0, 0))],
        out_shape=[jax.ShapeDtypeStruct((d, N_PROJ), BF16), jax.ShapeDtypeStruct((1, N_PROJ), F32)],
        compiler_params=_cparams(("arbitrary",)),
        name="prep_in_weights",
    )(w_in, b_in.reshape(depth, 1, n))


def _mm(a, b):
    return jnp.dot(a.astype(BF16), b.astype(BF16), preferred_element_type=F32)


def _mm_nt(a, b):
    return lax.dot_general(a.astype(BF16), b.astype(BF16), (((1,), (1,)), ((), ())),
                           preferred_element_type=F32)


def _mm_tn(a, b):
    return lax.dot_general(a.astype(BF16), b.astype(BF16), (((0,), (0,)), ((), ())),
                           preferred_element_type=F32)


def _dot_bf16(a, b):
    return jnp.dot(a, b, preferred_element_type=F32)


def _split_bf16(x, terms):
    out = []
    for _ in range(terms - 1):
        t = x.astype(BF16)
        out.append(t)
        x = x - t.astype(F32)
    out.append(x.astype(BF16))
    return out


def _mm_hi(a, b):
    ah, al = _split_bf16(a, 2)
    bh, bl = _split_bf16(b, 2)
    return _dot_bf16(ah, bh) + (_dot_bf16(ah, bl) + _dot_bf16(al, bh))


def _mm_sel(sel, x):
    sel = sel.astype(BF16)
    x1, x2, x3 = _split_bf16(x, 3)
    return _dot_bf16(sel, x1) + (_dot_bf16(sel, x2) + _dot_bf16(sel, x3))


def _sigmoid(x):
    return 1.0 / (1.0 + jnp.exp2(x * (-LOG2E)))


def _silu(x):
    return x * _sigmoid(x)


def _log_sigmoid(x):
    return jnp.minimum(x, 0.0) - jnp.log1p(jnp.exp(-jnp.abs(x)))


def _rope(x, cos, sin_up, sin_dn):
    w = x.shape[-1]
    up = pltpu.roll(x, w - HALF, axis=1)
    dn = pltpu.roll(x, HALF, axis=1)
    return x * cos + up * sin_up + dn * sin_dn


def _rope_bf16(x, cos, sin_up, sin_dn, perm):
    swapped = jnp.dot(x.astype(BF16), perm, preferred_element_type=F32)
    return x * cos + swapped * (sin_up + sin_dn)


def _rope_perm(width, rot_lanes):
    lane = np.arange(width)
    src = np.where(lane % HEAD_DIM < HALF, lane + HALF, lane - HALF)
    rotated = (lane % LANES) < rot_lanes
    perm = (np.arange(width)[:, None] == src[None, :]) & rotated[None, :]
    return jnp.asarray(perm.astype(np.float32), dtype=BF16)


def _rope_tables(pos, width, rot_lanes):
    inv = ROPE_THETA ** (-jnp.arange(HALF, dtype=F32) / HALF)
    ang = pos.astype(F32)[:, None] * inv[None, :]
    cos, sin = jnp.cos(ang), jnp.sin(ang)
    zero, one = jnp.zeros_like(sin), jnp.ones_like(cos)
    cos_h = jnp.concatenate([cos, cos], axis=1)
    up_h = jnp.concatenate([-sin, zero], axis=1)
    dn_h = jnp.concatenate([zero, sin], axis=1)
    id_c = jnp.concatenate([one, one], axis=1)
    id_s = jnp.concatenate([zero, zero], axis=1)
    cs, us, ds = [], [], []
    for l0 in range(0, width, HEAD_DIM):
        rot = (l0 % 128) < rot_lanes
        cs.append(cos_h if rot else id_c)
        us.append(up_h if rot else id_s)
        ds.append(dn_h if rot else id_s)
    return jnp.concatenate(cs, axis=1), jnp.concatenate(us, axis=1), jnp.concatenate(ds, axis=1)


def _iota(shape, dim):
    return lax.broadcasted_iota(jnp.int32, shape, dim)


def _cparams(sem):
    return pltpu.CompilerParams(dimension_semantics=sem, vmem_limit_bytes=VMEM_LIMIT)


def _ln_rows(x, g, b):
    mu = jnp.mean(x, axis=-1, keepdims=True)
    xc = x - mu
    var = jnp.mean(xc * xc, axis=-1, keepdims=True)
    return xc * lax.rsqrt(var + LN_EPS) * g + b


def _mix_out(rows, ya_ref, yb_ref, yc_ref, yd_ref, h_ref, w_ref, g_ref, b_ref):
    y = _mm(ya_ref[rows, :], w_ref[0:GROUP_W, :])
    y += _mm(yb_ref[rows, :], w_ref[GROUP_W:2 * GROUP_W, :])
    y += _mm(yc_ref[rows, :], w_ref[2 * GROUP_W:3 * GROUP_W, :])
    y += _mm(yd_ref[rows, :], w_ref[3 * GROUP_W:4 * GROUP_W, :])
    return _ln_rows(DN_ALPHA * h_ref[rows, :] + y, g_ref[...], b_ref[...])


def _sub_tiles(ref):
    return [slice(r, r + PROJ_SUB) for r in range(0, ref.shape[0], PROJ_SUB)]


def _ln_proj_body(x_ref, g_ref, b_ref, w_ref, bias_ref, h_ref, o_ref):
    for rows in _sub_tiles(x_ref):
        h = _ln_rows(x_ref[rows, :], g_ref[...], b_ref[...])
        h_ref[rows, :] = h
        o_ref[rows, :] = jnp.dot(h.astype(BF16), w_ref[...], preferred_element_type=F32) + bias_ref[...]


def _out_proj_body(ya_ref, yb_ref, yc_ref, yd_ref, hp_ref, wo_ref, g_ref, b_ref, w_ref, bias_ref, h_ref, o_ref):
    for rows in _sub_tiles(hp_ref):
        h = _mix_out(rows, ya_ref, yb_ref, yc_ref, yd_ref, hp_ref, wo_ref, g_ref, b_ref)
        h_ref[rows, :] = h
        o_ref[rows, :] = jnp.dot(h.astype(BF16), w_ref[...], preferred_element_type=F32) + bias_ref[...]


def _ln_in_proj(x2, g, b, w_bf, bias, tm=PROJ_TM):
    m, d = x2.shape
    n = w_bf.shape[1]
    row = lambda width: pl.BlockSpec((tm, width), lambda i: (i, 0))
    const = lambda shape: pl.BlockSpec(shape, lambda i: (0, 0))
    return pl.pallas_call(
        _ln_proj_body,
        grid=(m // tm,),
        in_specs=[row(d), const((1, d)), const((1, d)), const((d, n)), const((1, n))],
        out_specs=[row(d), row(n)],
        out_shape=[jax.ShapeDtypeStruct((m, d), F32), jax.ShapeDtypeStruct((m, n), F32)],
        compiler_params=_cparams(("parallel",)),
        name="ln_in_proj",
    )(x2, g.reshape(1, d), b.reshape(1, d), w_bf, bias.reshape(1, n))


def _out_in_proj(ys, h2, wo_bf, g, b, w_bf, bias, tm=PROJ_TM):
    m, d = h2.shape
    n = w_bf.shape[1]
    row = lambda width: pl.BlockSpec((tm, width), lambda i: (i, 0))
    const = lambda shape: pl.BlockSpec(shape, lambda i: (0, 0))
    return pl.pallas_call(
        _out_proj_body,
        grid=(m // tm,),
        in_specs=[row(GROUP_W)] * N_GROUPS + [row(d), const((N_GROUPS * GROUP_W, d)), const((1, d)), const((1, d)),
                                              const((d, n)), const((1, n))],
        out_specs=[row(d), row(n)],
        out_shape=[jax.ShapeDtypeStruct((m, d), F32), jax.ShapeDtypeStruct((m, n), F32)],
        compiler_params=_cparams(("parallel",)),
        name="out_in_proj",
    )(*[y.reshape(m, GROUP_W) for y in ys], h2, wo_bf, g.reshape(1, d), b.reshape(1, d), w_bf, bias.reshape(1, n))


def _out_body(ya_ref, yb_ref, yc_ref, yd_ref, h_ref, w_ref, g_ref, b_ref, o_ref):
    o_ref[...] = _mix_out(slice(None), ya_ref, yb_ref, yc_ref, yd_ref, h_ref, w_ref, g_ref, b_ref)


def _out_proj(ys, h2, w_bf, g, b, tm=512):
    m, d = h2.shape
    yspec = pl.BlockSpec((tm, GROUP_W), lambda i: (i, 0))
    return pl.pallas_call(
        _out_body,
        grid=(m // tm,),
        in_specs=[yspec, yspec, yspec, yspec,
                  pl.BlockSpec((tm, d), lambda i: (i, 0)),
                  pl.BlockSpec((4 * GROUP_W, d), lambda i: (0, 0)),
                  pl.BlockSpec((1, d), lambda i: (0, 0)),
                  pl.BlockSpec((1, d), lambda i: (0, 0))],
        out_specs=pl.BlockSpec((tm, d), lambda i: (i, 0)),
        out_shape=jax.ShapeDtypeStruct((m, d), F32),
        compiler_params=_cparams(("parallel",)),
        name="out_proj",
    )(*[y.reshape(m, GROUP_W) for y in ys], h2, w_bf, g.reshape(1, d), b.reshape(1, d))


def _mixer_a_body(sink_ref, q_ref, kv_ref, z_ref, cos_ref, up_ref, dn_ref, perm_ref, o_ref, k_ref, vt_ref):
    i = pl.program_id(1)
    seq = kv_ref.shape[1]
    n_kv = N_HEADS // A_GROUP
    low_half = _iota((1, LANES), 1) < HEAD_DIM

    @pl.when(i == 0)
    def _():
        def rope_rows(c, carry):
            for u in range(PRO_ROWS // QBLK):
                t0 = c * (PRO_ROWS // QBLK) + u
                rows = pl.ds(pl.multiple_of(t0 * QBLK, QBLK), QBLK)
                kr = _rope_bf16(kv_ref[0, rows, 0:LANES], cos_ref[rows, 0:LANES], up_ref[rows, 0:LANES],
                                dn_ref[rows, 0:LANES], perm_ref[0:LANES, 0:LANES])
                v_t = _mm_nt(_iota((LANES, LANES), 0) == _iota((LANES, LANES), 1), kv_ref[0, rows, LANES:2 * LANES])
                v_t = v_t.astype(BF16)
                ones_rows = jnp.ones((V_AUG - HEAD_DIM, QBLK), BF16)
                other = pltpu.roll(kr, HEAD_DIM, axis=1)
                for g in range(n_kv):
                    dup = jnp.where(low_half, kr, other) if g == 0 else jnp.where(low_half, other, kr)
                    k_ref[g, rows, :] = dup.astype(BF16)
                    vt_ref[g, t0] = jnp.concatenate([v_t[g * HEAD_DIM:(g + 1) * HEAD_DIM, :], ones_rows], axis=0)
            return carry
        lax.fori_loop(0, seq // PRO_ROWS, rope_rows, 0)

    n_blk = A_WINDOW // QBLK + 1
    span = n_blk * QBLK
    first_head = _iota((1, A_GROUP * QBLK), 1) < QBLK
    eye = (_iota((QBLK, QBLK), 0) == _iota((QBLK, QBLK), 1)).astype(BF16)

    def query_block(u):
        blk_i = i * A_SUB + u
        sub = slice(u * QBLK, (u + 1) * QBLK)
        q0 = pl.multiple_of(blk_i * QBLK, QBLK)
        qrows = pl.ds(q0, QBLK)
        qr = _rope(q_ref[0, sub, :], cos_ref[qrows, :], up_ref[qrows, :], dn_ref[qrows, :]) * (SCALE * LOG2E)
        kt0 = jnp.maximum(blk_i - A_WINDOW // QBLK, 0)
        k0 = pl.multiple_of(kt0 * QBLK, QBLK)
        rel = (q0 + _iota((1, QBLK), 1)) - (k0 + _iota((span, 1), 0))
        in_band = lax.bitcast_convert_type(rel, jnp.uint32) < A_WINDOW
        bias = jnp.concatenate([jnp.where(in_band, 0.0, NEG)] * A_GROUP, axis=1)
        outs = []
        for g in range(n_kv):
            q_pair = qr[:, g * LANES:(g + 1) * LANES]
            qg = jnp.concatenate([jnp.where(low_half, q_pair, 0.0), jnp.where(low_half, 0.0, q_pair)], axis=0)
            sink = jnp.where(first_head, sink_ref[g * A_GROUP], sink_ref[g * A_GROUP + 1]) * LOG2E
            s = _mm_nt(k_ref[g, pl.ds(k0, span), :], qg) + bias
            m = jnp.maximum(jnp.max(s, axis=0, keepdims=True), sink)
            p = jnp.exp2(s - m).astype(BF16)
            o = jnp.zeros((V_AUG, A_GROUP * QBLK), F32)
            for c in range(n_blk):
                o = o + jnp.dot(vt_ref[g, kt0 + c], p[c * QBLK:(c + 1) * QBLK], preferred_element_type=F32)
            o = o[0:HEAD_DIM] / (o[HEAD_DIM:HEAD_DIM + 1] + jnp.exp2(sink - m))
            outs.extend(o[:, r * QBLK:(r + 1) * QBLK] for r in range(A_GROUP))
        y = _mm_nt(eye, jnp.concatenate(outs, axis=0))
        o_ref[0, sub, :] = (y * _silu(z_ref[0, sub, :])).astype(Y_DTYPE)

    for u in range(A_SUB):
        query_block(u)


def _mixer_a(proj, sinks, tabs):
    assert A_GROUP == 2 and A_GROUP * HEAD_DIM == LANES, "a kv head's two query heads share one lane tile"
    bsz, seq, _ = proj.shape
    cos, up, dn = tabs
    blk = lambda name: _new_offset(name) // GROUP_W
    tspec = pl.BlockSpec((seq, GROUP_W), lambda b, i: (0, 0))
    return pl.pallas_call(
        _mixer_a_body,
        grid=(bsz, seq // (A_SUB * QBLK)),
        in_specs=[pl.BlockSpec(memory_space=pltpu.SMEM),
                  pl.BlockSpec((1, A_SUB * QBLK, GROUP_W), lambda b, i: (b, i, blk('a_q'))),
                  pl.BlockSpec((1, seq, GROUP_W), lambda b, i: (b, 0, blk('a_k'))),
                  pl.BlockSpec((1, A_SUB * QBLK, GROUP_W), lambda b, i: (b, i, blk('a_z'))),
                  tspec, tspec, tspec, pl.BlockSpec((GROUP_W, GROUP_W), lambda b, i: (0, 0))],
        out_specs=pl.BlockSpec((1, A_SUB * QBLK, GROUP_W), lambda b, i: (b, i, 0)),
        out_shape=jax.ShapeDtypeStruct((bsz, seq, GROUP_W), Y_DTYPE),
        scratch_shapes=[pltpu.VMEM((N_HEADS // A_GROUP, seq, LANES), BF16),
                        pltpu.VMEM((N_HEADS // A_GROUP, seq // QBLK, V_AUG, QBLK), BF16)],
        compiler_params=_cparams(("arbitrary", "arbitrary")),
        name="mixer_a",
    )(sinks, proj, proj, proj, cos, up, dn, _rope_perm(GROUP_W, LANES))


B_CHUNK = 128
B_AUG = V_AUG


def _conv_silu(x_ref, r0, prev, w, b):
    L = B_CHUNK
    x = x_ref[0, r0:r0 + L, :]
    acc = b + w[B_CONV - 1:B_CONV, :] * x
    for s in range(1, B_CONV):
        if r0 >= s:
            shifted = x_ref[0, r0 - s:r0 - s + L, :]
        else:
            row8 = _iota((8, x.shape[1]), 0)
            xs = pltpu.roll(x, s, axis=0)
            ps = pltpu.roll(prev, s, axis=0)
            shifted = jnp.concatenate([jnp.where(row8 < s, ps, xs[0:8]), xs[8:]], axis=0)
        acc = acc + w[B_CONV - 1 - s:B_CONV - s, :] * shifted
    return _silu(acc)


def _mixer_b_init(xprev_ref, c_ref, m_ref):
    xprev_ref[...] = jnp.zeros_like(xprev_ref)
    c_ref[...] = jnp.zeros_like(c_ref)
    m_ref[...] = jnp.zeros_like(m_ref)


def _mixer_b_main(xq_ref, xk_ref, v_ref, og_ref, z_ref, if_ref, cw_ref, cb_ref, o_ref, xprev_ref, c_ref, m_ref):
    L = B_CHUNK
    tri = (_iota((L, L), 1) <= _iota((L, L), 0)).astype(F32)
    key_first = _iota((L, L), 0) <= _iota((L, L), 1)
    half_of_lane = _iota((1, 128), 1) // HEAD_DIM
    ones_rows = jnp.ones((B_AUG - HEAD_DIM, L), F32)

    def chunk(r0, state):
        rows = slice(r0, r0 + L)
        qc = _conv_silu(xq_ref, r0, xprev_ref[:, 0:GROUP_W], cw_ref[:, 0:GROUP_W], cb_ref[:, 0:GROUP_W]) * SCALE
        kc = _conv_silu(xk_ref, r0, xprev_ref[:, GROUP_W:], cw_ref[:, GROUP_W:], cb_ref[:, GROUP_W:])
        q_t = qc.T
        v_t = v_ref[0, rows, :].T

        gates = if_ref[0, rows, :]
        bcum = _mm_sel(tri, _log_sigmoid(gates))
        gates_t = gates.T
        bcum_t = bcum.T
        outs, new_state = [], []
        for h in range(N_HEADS):
            pair = slice((h // 2) * 128, (h // 2 + 1) * 128)
            own = half_of_lane == (h % 2)
            k_pair = kc[:, pair]
            q_pair_t = q_t[pair, :]
            i_row = gates_t[h:h + 1, :]
            b_row = bcum_t[N_HEADS + h:N_HEADS + h + 1, :]
            c_col = gates[:, h:h + 1] - bcum[:, N_HEADS + h:N_HEADS + h + 1]
            cst, m_prev = state[h]
            dmat = jnp.where(key_first, b_row + c_col, NEG)
            inter = b_row + m_prev
            m_t = jnp.maximum(jnp.max(dmat, axis=0, keepdims=True), inter)
            smat = _mm(jnp.where(own, k_pair, 0.0), q_pair_t) * jnp.exp(dmat - m_t)
            vaug_t = jnp.concatenate([v_t[h * HEAD_DIM:(h + 1) * HEAD_DIM, :], ones_rows], axis=0)
            tot = _mm(vaug_t, smat) + jnp.exp(inter - m_t) * _mm(cst, q_pair_t)
            num = tot[0:HEAD_DIM, :]
            den = tot[HEAD_DIM:HEAD_DIM + 1, :]
            outs.append(num / jnp.maximum(jnp.abs(den), jnp.exp(-m_t)))
            b_last = b_row[:, L - 1:L]
            d_row = b_last - b_row + i_row
            m_new = jnp.maximum(b_last + m_prev, jnp.max(d_row, axis=1, keepdims=True))
            w_row = jnp.exp(d_row - m_new)
            decay = jnp.exp(b_last + m_prev - m_new)
            new_state.append((decay * cst + jnp.where(own, _mm(vaug_t * w_row, k_pair), 0.0), m_new))
        o_ref[0, rows, :] = (_sigmoid(og_ref[0, rows, :]) * jnp.concatenate(outs, axis=0).T
                             * _silu(z_ref[0, rows, :])).astype(Y_DTYPE)
        return new_state

    rows_step = xq_ref.shape[1]
    state = [(c_ref[h], m_ref[h:h + 1, 0:1]) for h in range(N_HEADS)]
    for u in range(rows_step // L):
        state = chunk(u * L, state)
        yield
    xprev_ref[:, 0:GROUP_W] = xq_ref[0, rows_step - 8:rows_step, :]
    xprev_ref[:, GROUP_W:] = xk_ref[0, rows_step - 8:rows_step, :]
    for h in range(N_HEADS):
        c_ref[h] = state[h][0]
        m_ref[h:h + 1, :] = jnp.broadcast_to(state[h][1], (1, 128))


def _mixer_c_main(layer, q_ref, f_ref, i_ref, z_ref, lb_ref, gn_ref, tri_ref, piv_ref, o_ref, st_ref):
    L = C_CHUNK
    nsub = L // C_SUB

    raw = lb_ref[...]
    ex = jnp.exp(raw - jnp.max(raw, axis=0, keepdims=True))
    if layer == 0:
        lb = jnp.zeros((1, GROUP_W), F32)
    else:
        lb = jnp.sum(ex[1:layer + 1], axis=0, keepdims=True) / jnp.sum(ex, axis=0, keepdims=True)
    tri = (_iota((L, L), 1) <= _iota((L, L), 0)).astype(F32)
    head_of_lane = _iota((1, GROUP_W), 1) // HEAD_DIM
    same_head = ((_iota((GROUP_W, GROUP_W), 0) // HEAD_DIM)
                 == (_iota((GROUP_W, GROUP_W), 1) // HEAD_DIM))
    ones_bd = same_head.astype(BF16)

    def chunk(rows, st):
        f = lb + (1.0 - lb) * _sigmoid(f_ref[0, rows, :])
        logf = jnp.log(f)
        k = 1.0 - f
        q = _silu(q_ref[0, rows, :])
        v = i_ref[0, rows, :]
        a = _mm_sel(tri, logf) * LOG2E
        o = _mm_nt(q * jnp.exp2(a), st)

        qsts, kts, vreps = [], [], []
        for i in range(1, nsub):
            r0 = i * C_SUB
            piv = a[r0 - 1:r0, :]
            qt = q[r0:r0 + C_SUB] * jnp.exp2(a[r0:r0 + C_SUB] - piv)
            qsts.extend(jnp.where(head_of_lane == h, qt, 0.0) for h in range(N_HEADS))
            kts.append((k[0:r0] * jnp.exp2(piv - a[0:r0])).astype(BF16))
            vreps.append(v[0:r0].astype(BF16))
        scores = _mm_nt(jnp.concatenate(qsts, axis=0), jnp.concatenate(kts, axis=0))
        r = _dot_bf16((scores * piv_ref[...]).astype(BF16), jnp.concatenate(vreps, axis=0))
        pieces = [o[0:C_SUB]]
        for i in range(1, nsub):
            acc = o[i * C_SUB:(i + 1) * C_SUB]
            for h in range(N_HEADS):
                g0 = ((i - 1) * N_HEADS + h) * C_SUB
                acc = acc + jnp.where(head_of_lane == h, r[g0:g0 + C_SUB], 0.0)
            pieces.append(acc)
        o = jnp.concatenate(pieces, axis=0)

        a3 = a.reshape(nsub, C_SUB, GROUP_W)
        q3 = q.reshape(nsub, C_SUB, GROUP_W)
        k3 = k.reshape(nsub, C_SUB, GROUP_W)
        v3 = v.reshape(nsub, C_SUB, GROUP_W)
        ps = []
        for s in range(C_SUB):
            e = jnp.exp2(a3 - a3[:, s:s + 1, :] + tri_ref[s])
            ps.append((q3 * k3[:, s:s + 1, :] * e).reshape(L, GROUP_W).astype(BF16))
        w = jnp.dot(jnp.concatenate(ps, axis=0), ones_bd, preferred_element_type=F32)
        for s in range(C_SUB):
            vs = jnp.broadcast_to(v3[:, s:s + 1, :], (nsub, C_SUB, GROUP_W)).reshape(L, GROUP_W)
            o = o + w[s * L:(s + 1) * L] * vs

        a_last = a[L - 1:L, :]
        kdec = k * jnp.exp2(a_last - a)
        st_new = st * jnp.exp2(a_last) + jnp.where(same_head, _mm_tn(v, kdec), 0.0)

        ms = jnp.dot((o * o).astype(BF16), ones_bd, preferred_element_type=F32) * (1.0 / HEAD_DIM)
        o = o * lax.rsqrt(ms + C_EPS) * gn_ref[...]
        o_ref[0, rows, :] = (o * _silu(z_ref[0, rows, :])).astype(Y_DTYPE)
        return st_new

    st = st_ref[...]
    for u in range(q_ref.shape[1] // L):
        st = chunk(slice(u * L, (u + 1) * L), st)
        yield
    st_ref[...] = st


N_B_IN, N_C_IN, N_B_SCRATCH = 8, 8, 3
C_PER_B = B_CHUNK // C_CHUNK


def _mixer_bc_body(layer, *refs):
    b_in = refs[:N_B_IN]
    c_in = refs[N_B_IN:N_B_IN + N_C_IN]
    ob_ref, oc_ref = refs[N_B_IN + N_C_IN:N_B_IN + N_C_IN + 2]
    scratch = refs[N_B_IN + N_C_IN + 2:]
    b_scratch, c_scratch = scratch[:N_B_SCRATCH], scratch[N_B_SCRATCH:]

    @pl.when(pl.program_id(1) == 0)
    def _():
        _mixer_b_init(*b_scratch)
        c_scratch[0][...] = jnp.zeros_like(c_scratch[0])

    pending = [_mixer_b_main(*b_in, ob_ref, *b_scratch), _mixer_c_main(layer, *c_in, oc_ref, *c_scratch)]
    weights = [1, C_PER_B]
    while pending:
        for gen, reps in list(zip(pending, weights)):
            for _ in range(reps):
                if next(gen, StopIteration) is StopIteration:
                    idx = pending.index(gen)
                    del pending[idx], weights[idx]
                    break


def _mixer_bc(proj, conv_w, conv_b, c_lb, norm_g, layer):
    bsz, seq, _ = proj.shape
    L = BC_STEP
    blk = lambda name: _new_offset(name) // GROUP_W
    spec = lambda name: pl.BlockSpec((1, L, GROUP_W), lambda b, c: (b, c, blk(name)))
    vec = pl.BlockSpec((1, GROUP_W), lambda b, c: (0, 0))
    t_ge_s = np.arange(C_SUB)[None, :, None] >= np.arange(C_SUB)[:, None, None]
    causal_bias = np.broadcast_to(np.where(t_ge_s, 0.0, NEG), (C_SUB, C_SUB, GROUP_W)).astype(np.float32)
    pivots = np.arange(1, C_CHUNK // C_SUB)
    row_piv = np.repeat(pivots, N_HEADS * C_SUB)
    col_piv = np.repeat(pivots, pivots * C_SUB)
    same_pivot = (row_piv[:, None] == col_piv[None, :]).astype(np.float32)
    out_spec = pl.BlockSpec((1, L, GROUP_W), lambda b, c: (b, c, 0))
    out_shape = jax.ShapeDtypeStruct((bsz, seq, GROUP_W), Y_DTYPE)
    return pl.pallas_call(
        functools.partial(_mixer_bc_body, layer),
        grid=(bsz, seq // L),
        in_specs=[spec('b_q'), spec('b_k'), spec('b_v'), spec('b_o'), spec('b_z'),
                  pl.BlockSpec((1, L, LANES), lambda b, c: (b, c, _new_offset('gates') // LANES)),
                  pl.BlockSpec((B_CONV, 2 * GROUP_W), lambda b, c: (0, 0)),
                  pl.BlockSpec((1, 2 * GROUP_W), lambda b, c: (0, 0)),
                  spec('c_q'), spec('c_f'), spec('c_i'), spec('c_z'),
                  pl.BlockSpec((DEPTH, GROUP_W), lambda b, c: (0, 0)), vec,
                  pl.BlockSpec((C_SUB, C_SUB, GROUP_W), lambda b, c: (0, 0, 0)),
                  pl.BlockSpec(same_pivot.shape, lambda b, c: (0, 0))],
        out_specs=[out_spec, out_spec],
        out_shape=[out_shape, out_shape],
        scratch_shapes=[pltpu.VMEM((8, 2 * GROUP_W), F32),
                        pltpu.VMEM((N_HEADS, B_AUG, 128), F32),
                        pltpu.VMEM((8, 128), F32),
                        pltpu.VMEM((GROUP_W, GROUP_W), F32)],
        compiler_params=_cparams(("arbitrary", "arbitrary")),
        name="mixer_bc",
    )(proj, proj, proj, proj, proj, proj, conv_w, conv_b.reshape(1, -1),
      proj, proj, proj, proj, c_lb, norm_g.reshape(1, -1), jnp.asarray(causal_bias), jnp.asarray(same_pivot))


N_CMP_PAD = 128
KT = 512
D_QBLK = 512
WBLK = 256
PE_ROWS = 16
SEL_ROWS = 16


def _mixer_d_body(q_ref, z_ref, ksv_ref, kwv_ref, cin_ref, g_ref, cos_ref, up_ref, dn_ref, kcos_ref, kup_ref,
                  kdn_ref, kperm_ref, ccos_ref, cup_ref, cdn_ref, w1_ref, pe_ref, w2_ref, ov_ref, o_ref,
                  ks_ref, kw_ref, vts_ref, vtw_ref, kc_ref, vct_ref, sel_ref, qaug_ref):
    QBLK = D_QBLK
    i = pl.program_id(1)
    seq = ksv_ref.shape[1]
    half_cmp = CMP_LEN // 2

    @pl.when(i == 0)
    def _():
        def rope_rows(c, carry):
            for u in range(PRO_ROWS // KT):
                t0 = c * (PRO_ROWS // KT) + u
                rows = pl.ds(pl.multiple_of(t0 * KT, KT), KT)
                kvs_in = ksv_ref[0, rows, :]
                kvw_in = kwv_ref[0, rows, :]
                kvs = _rope_bf16(kvs_in, kcos_ref[rows, :], kup_ref[rows, :], kdn_ref[rows, :], kperm_ref[...])
                kvw = _rope_bf16(kvw_in, kcos_ref[rows, :], kup_ref[rows, :], kdn_ref[rows, :], kperm_ref[...])
                blk_in_tile = _iota((KT, HEAD_DIM), 0) // SLC_LEN
                onehot = jnp.where(blk_in_tile == _iota((KT, HEAD_DIM), 1), 1.0, 0.0)
                ks_ref[rows, :] = jnp.concatenate([kvs[:, 0:HEAD_DIM], onehot], axis=1).astype(BF16)
                kw_ref[rows, :] = kvw[:, 0:HEAD_DIM].astype(BF16)
                ones_rows = jnp.ones((V_AUG - HEAD_DIM, KT), BF16)
                pick_v = _iota((HEAD_DIM, LANES), 1) == _iota((HEAD_DIM, LANES), 0) + HEAD_DIM
                vts_ref[t0] = jnp.concatenate([_mm_nt(pick_v, kvs_in).astype(BF16), ones_rows], axis=0)
                vwt = jnp.concatenate([_mm_nt(pick_v, kvw_in).astype(BF16), ones_rows], axis=0)
                for w in range(KT // WBLK):
                    vtw_ref[t0 * (KT // WBLK) + w] = vwt[:, w * WBLK:(w + 1) * WBLK]
            return carry
        lax.fori_loop(0, seq // PRO_ROWS, rope_rows, 0)
        acc = jnp.zeros((N_CMP_PAD + PE_ROWS, 2 * LANES), F32)
        for rr in range(half_cmp // 2):
            xa = cin_ref[0, pl.ds(2 * rr, N_CMP_PAD, stride=CMP_STRIDE), :]
            xb = cin_ref[0, pl.ds(2 * rr + 1, N_CMP_PAD, stride=CMP_STRIDE), :]
            lhs = jnp.concatenate([jnp.concatenate([xa, xb], axis=1), pe_ref[rr]], axis=0)
            acc = acc + _mm_hi(lhs, w1_ref[rr])
        u0 = acc[0:N_CMP_PAD, 0:LANES] + acc[N_CMP_PAD:N_CMP_PAD + 1, 0:LANES]
        u1 = acc[0:N_CMP_PAD, LANES:] + acc[N_CMP_PAD + 1:N_CMP_PAD + 2, LANES:]
        pre = u0 + pltpu.roll(u1, N_CMP_PAD - 1, axis=0)
        cv = _rope(_mm_hi(_silu(pre), w2_ref[...]), ccos_ref[...], cup_ref[...], cdn_ref[...])
        kc_ref[...] = cv[:, 0:HEAD_DIM]
        vct_ref[...] = cv.T[HEAD_DIM:2 * HEAD_DIM, :].astype(BF16)

    q0 = pl.multiple_of(i * QBLK, QBLK)
    qrows = pl.ds(q0, QBLK)
    qr = _rope(q_ref[0], cos_ref[qrows, :], up_ref[qrows, :], dn_ref[qrows, :]) * (SCALE * LOG2E)
    qt = qr.T
    qst = jnp.concatenate([qt[h * HEAD_DIM:(h + 1) * HEAD_DIM] for h in range(N_HEADS)], axis=1)
    qst_bf = qst.astype(BF16)
    cols4 = N_HEADS * QBLK
    qpos = q0 + _iota((1, QBLK), 1)

    def lanes4(x):
        return jnp.concatenate([x] * N_HEADS, axis=1)

    sc = _mm_hi(kc_ref[...], qst)
    ends = _iota((N_CMP_PAD, 1), 0) * CMP_STRIDE + (CMP_LEN - 1)
    scm = sc + lanes4(jnp.where(ends <= qpos, 0.0, NEG))
    e = jnp.exp2(scm - jnp.max(scm, axis=0, keepdims=True))
    l = jnp.sum(e, axis=0, keepdims=True)
    any_valid = lanes4(jnp.where(qpos >= CMP_LEN - 1, 1.0, 0.0))
    pc = e * (any_valid / l)
    o_cmp = jnp.dot(vct_ref[...], pc.astype(BF16), preferred_element_type=F32)
    psum = pc[:, 0:QBLK] + pc[:, QBLK:2 * QBLK] + pc[:, 2 * QBLK:3 * QBLK] + pc[:, 3 * QBLK:4 * QBLK]

    n_slc = seq // SLC_LEN
    imp = _mm_sel(ov_ref[...], psum)
    jl = _iota((n_slc, 1), 0)
    cur = jnp.right_shift(qpos, SLC_LEN.bit_length() - 1)
    imp = jnp.where(jl == 0, FORCE_SCORE, imp)
    imp = jnp.where(jl == cur, FORCE_SCORE, imp)
    imp = jnp.where(jl == cur - 1, FORCE_SCORE, imp)
    imp = jnp.where(jl <= cur, imp, NEG)
    rank = jnp.zeros((n_slc, QBLK), F32)
    for jp in range(n_slc):
        cj = imp[jp:jp + 1, :]
        tie = jnp.where(jl > jp, 1.0, 0.0)
        rank = rank + jnp.where(cj > imp, 1.0, jnp.where(cj == imp, tie, 0.0))
    sel_bias = jnp.where(rank < N_SEL, 0.0, NEG)
    blocks_per_tile = KT // SLC_LEN
    for t in range(n_slc // blocks_per_tile):
        sel_ref[t] = sel_bias[t * blocks_per_tile:(t + 1) * blocks_per_tile, :]

    n_win = NSA_WINDOW // WBLK + 1
    span = n_win * WBLK
    win_parts = []
    for u in range(QBLK // WBLK):
        qcols = [slice(h * QBLK + u * WBLK, h * QBLK + (u + 1) * WBLK) for h in range(N_HEADS)]
        q_sub = jnp.concatenate([qst_bf[:, c] for c in qcols], axis=1)
        kt0 = jnp.maximum(i * (QBLK // WBLK) + u - NSA_WINDOW // WBLK, 0)
        k0 = pl.multiple_of(kt0 * WBLK, WBLK)
        rel = (q0 + u * WBLK + _iota((1, WBLK), 1)) - (k0 + _iota((span, 1), 0))
        in_band = lax.bitcast_convert_type(rel, jnp.uint32) < NSA_WINDOW
        s = jnp.dot(kw_ref[pl.ds(k0, span), :], q_sub, preferred_element_type=F32)
        s = s + lanes4(jnp.where(in_band, 0.0, NEG))
        p = jnp.exp2(s - jnp.max(s, axis=0, keepdims=True)).astype(BF16)
        acc = jnp.zeros((V_AUG, N_HEADS * WBLK), F32)
        for c in range(n_win):
            acc = acc + jnp.dot(vtw_ref[kt0 + c], p[c * WBLK:(c + 1) * WBLK], preferred_element_type=F32)
        win_parts.append(acc[0:HEAD_DIM] / acc[HEAD_DIM:HEAD_DIM + 1])
    o_win = jnp.concatenate([part[:, h * WBLK:(h + 1) * WBLK] for h in range(N_HEADS) for part in win_parts], axis=1)

    qaug_ref[0:HEAD_DIM, :] = qst_bf
    qaug_ref[HEAD_DIM:2 * HEAD_DIM, :] = jnp.zeros((HEAD_DIM, cols4), BF16)

    def sel_scores(kt):
        picks = jnp.concatenate([sel_ref[kt], jnp.zeros((SEL_ROWS - blocks_per_tile, QBLK), F32)], axis=0)
        qaug_ref[HEAD_DIM:HEAD_DIM + SEL_ROWS, :] = lanes4(picks).astype(BF16)
        kbase = pl.multiple_of(kt * KT, KT)
        return jnp.dot(ks_ref[pl.ds(kbase, KT), :], qaug_ref[...], preferred_element_type=F32)

    def sel_update(kt, s, m_old, acc):
        m_new = jnp.maximum(m_old, jnp.max(s, axis=0, keepdims=True))
        p = jnp.exp2(s - m_new)
        acc_new = jnp.exp2(m_old - m_new) * acc + jnp.dot(vts_ref[kt], p.astype(BF16), preferred_element_type=F32)
        return m_new, acc_new

    init = (jnp.full((1, cols4), NEG, F32), jnp.zeros((V_AUG, cols4), F32))
    m_s, acc_s = lax.fori_loop(0, i, lambda kt, c: sel_update(kt, sel_scores(kt), *c), init)

    hk = KT // 2
    tri = jnp.where(_iota((hk, hk), 0) <= _iota((hk, hk), 1), 0.0, NEG)
    picks = jnp.concatenate([sel_ref[i], jnp.zeros((SEL_ROWS - blocks_per_tile, QBLK), F32)], axis=0)
    qaug_ref[HEAD_DIM:HEAD_DIM + SEL_ROWS, :] = lanes4(picks).astype(BF16)
    kbase = pl.multiple_of(i * KT, KT)
    late = [slice(h * QBLK + hk, (h + 1) * QBLK) for h in range(N_HEADS)]
    bias_a = lanes4(jnp.concatenate([tri, jnp.zeros((hk, hk), F32)], axis=1))
    s_a = jnp.dot(ks_ref[pl.ds(kbase, hk), :], qaug_ref[...], preferred_element_type=F32) + bias_a
    m_a = jnp.maximum(m_s, jnp.max(s_a, axis=0, keepdims=True))
    acc_a = jnp.exp2(m_s - m_a) * acc_s + jnp.dot(vts_ref[i][:, 0:hk], jnp.exp2(s_a - m_a).astype(BF16),
                                                 preferred_element_type=F32)
    q_late = jnp.concatenate([qaug_ref[:, c] for c in late], axis=1)
    m_l = jnp.concatenate([m_a[:, c] for c in late], axis=1)
    acc_l = jnp.concatenate([acc_a[:, c] for c in late], axis=1)
    s_b = jnp.dot(ks_ref[pl.ds(kbase + hk, hk), :], q_late, preferred_element_type=F32) + lanes4(tri)
    m_b = jnp.maximum(m_l, jnp.max(s_b, axis=0, keepdims=True))
    acc_b = jnp.exp2(m_l - m_b) * acc_l + jnp.dot(vts_ref[i][:, hk:KT], jnp.exp2(s_b - m_b).astype(BF16),
                                                 preferred_element_type=F32)
    acc_s = jnp.concatenate([part for h in range(N_HEADS)
                             for part in (acc_a[:, h * QBLK:h * QBLK + hk], acc_b[:, h * hk:(h + 1) * hk])], axis=1)
    o_slc = acc_s[0:HEAD_DIM] / acc_s[HEAD_DIM:HEAD_DIM + 1]

    gate = _sigmoid(g_ref[0]).T
    outs = []
    for h in range(N_HEADS):
        cs = slice(h * QBLK, (h + 1) * QBLK)
        g0 = D_GATE_LANE + 3 * h
        outs.append(gate[g0:g0 + 1, :] * o_cmp[:, cs] + gate[g0 + 1:g0 + 2, :] * o_slc[:, cs]
                    + gate[g0 + 2:g0 + 3, :] * o_win[:, cs])
    o_ref[0] = (jnp.concatenate(outs, axis=0).T * _silu(z_ref[0])).astype(Y_DTYPE)


def _mixer_d(proj, qtabs, ktabs, ctabs, w1bd, pe2, w2bd, overlap):
    QBLK = D_QBLK
    bsz, seq, _ = proj.shape
    blk = lambda name: _new_offset(name) // GROUP_W
    full = lambda shape: pl.BlockSpec(shape, lambda b, i: (0,) * len(shape))
    return pl.pallas_call(
        _mixer_d_body,
        grid=(bsz, seq // QBLK),
        in_specs=[pl.BlockSpec((1, QBLK, GROUP_W), lambda b, i: (b, i, blk('d_q'))),
                  pl.BlockSpec((1, QBLK, GROUP_W), lambda b, i: (b, i, blk('d_z'))),
                  pl.BlockSpec((1, seq, 128), lambda b, i: (b, 0, _new_offset('d_ksv') // 128)),
                  pl.BlockSpec((1, seq, 128), lambda b, i: (b, 0, _new_offset('d_kwv') // 128)),
                  pl.BlockSpec((1, seq, 128), lambda b, i: (b, 0, _new_offset('d_kvc') // 128)),
                  pl.BlockSpec((1, QBLK, LANES), lambda b, i: (b, i, _new_offset('gates') // LANES)),
                  full((seq, GROUP_W)), full((seq, GROUP_W)), full((seq, GROUP_W)),
                  full((seq, 128)), full((seq, 128)), full((seq, 128)), full((LANES, LANES)),
                  full((N_CMP_PAD, 128)), full((N_CMP_PAD, 128)), full((N_CMP_PAD, 128)),
                  full((CMP_LEN // 4, 2 * LANES, 2 * LANES)), full((CMP_LEN // 4, PE_ROWS, 2 * LANES)), full((128, 128)),
                  full((seq // SLC_LEN, N_CMP_PAD))],
        out_specs=pl.BlockSpec((1, QBLK, GROUP_W), lambda b, i: (b, i, 0)),
        out_shape=jax.ShapeDtypeStruct((bsz, seq, GROUP_W), Y_DTYPE),
        scratch_shapes=[pltpu.VMEM((seq, 2 * HEAD_DIM), BF16),
                        pltpu.VMEM((seq, HEAD_DIM), BF16),
                        pltpu.VMEM((seq // KT, V_AUG, KT), BF16),
                        pltpu.VMEM((seq // WBLK, V_AUG, WBLK), BF16),
                        pltpu.VMEM((N_CMP_PAD, HEAD_DIM), F32),
                        pltpu.VMEM((HEAD_DIM, N_CMP_PAD), BF16),
                        pltpu.VMEM((seq // KT, KT // SLC_LEN, QBLK), F32),
                        pltpu.VMEM((2 * HEAD_DIM, N_HEADS * QBLK), BF16)],
        compiler_params=_cparams(("arbitrary", "arbitrary")),
        name="mixer_d",
    )(proj, proj, proj, proj, proj, proj, *qtabs, *ktabs, _rope_perm(LANES, HEAD_DIM), *ctabs, w1bd, pe2, w2bd,
      overlap)


def _cmp_params(pe, w1, w2):
    w1 = w1.reshape(2, CMP_LEN, HEAD_DIM, HEAD_DIM)
    z = jnp.zeros((CMP_LEN, HEAD_DIM, HEAD_DIM), F32)
    w1bd = jnp.concatenate([jnp.concatenate([w1[0], z], axis=2), jnp.concatenate([z, w1[1]], axis=2)], axis=1)
    z2 = jnp.zeros((HEAD_DIM, HEAD_DIM), F32)
    w2bd = jnp.concatenate([jnp.concatenate([w2[0], z2], axis=1), jnp.concatenate([z2, w2[1]], axis=1)], axis=0)
    pe2 = jnp.concatenate([pe[0], pe[1]], axis=1)
    half = CMP_LEN // 2
    w1pair = jnp.stack([jnp.concatenate([jnp.concatenate([w1bd[r], w1bd[half + r]], axis=1),
                                         jnp.concatenate([w1bd[r + 1], w1bd[half + r + 1]], axis=1)], axis=0)
                        for r in range(0, half, 2)])
    pe_rows = jnp.stack([jnp.concatenate([jnp.concatenate([pe2[r], pe2[r + 1]])[None, :],
                                          jnp.concatenate([pe2[half + r], pe2[half + r + 1]])[None, :],
                                          jnp.zeros((PE_ROWS - 2, 2 * LANES), F32)], axis=0)
                         for r in range(0, half, 2)])
    return w1pair, pe_rows, w2bd


def _overlap_matrix(seq):
    n_cmp = (seq - CMP_LEN) // CMP_STRIDE + 1
    starts = np.arange(N_CMP_PAD) * CMP_STRIDE
    blk = np.arange(seq // SLC_LEN)
    ov = ((starts[None, :] < (blk[:, None] + 1) * SLC_LEN) & (starts[None, :] + CMP_LEN > blk[:, None] * SLC_LEN))
    ov = ov & (np.arange(N_CMP_PAD)[None, :] < n_cmp)
    return jnp.asarray(ov.astype(np.float32))


def kernel(x, ln0_g, ln0_b, w_in, b_in, a_sinks, b_conv_w, b_conv_b, c_lb, c_norm_g, d_cmp_pe, d_cmp_w1, d_cmp_w2,
           w_out, ln_g, ln_b):
    bsz, seq, d = x.shape
    pos = jnp.arange(seq)
    qtabs = _rope_tables(pos, GROUP_W, 128)
    ktabs = _rope_tables(pos, 128, HEAD_DIM)
    ctabs = _rope_tables(jnp.arange(N_CMP_PAD) * CMP_STRIDE + (CMP_LEN - 1), 128, HEAD_DIM)
    overlap = _overlap_matrix(seq)

    w_p, b_p = _prep_in_weights(w_in, b_in, 0)
    h, proj = _ln_in_proj(x.reshape(bsz * seq, d), ln0_g, ln0_b, w_p, b_p)
    for l in range(DEPTH):
        proj = proj.reshape(bsz, seq, N_PROJ)
        y_a = _mixer_a(proj, a_sinks[l], qtabs)
        y_b, y_c = _mixer_bc(proj, b_conv_w[l], b_conv_b[l], c_lb, c_norm_g[l], l)
        w1bd, pe2, w2bd = _cmp_params(d_cmp_pe[l], d_cmp_w1[l], d_cmp_w2[l])
        y_d = _mixer_d(proj, qtabs, ktabs, ctabs, w1bd, pe2, w2bd, overlap)
        ys = (y_a, y_b, y_c, y_d)
        if l + 1 < DEPTH:
            w_p, b_p = _prep_in_weights(w_in, b_in, l + 1)
            h, proj = _out_in_proj(ys, h, w_out[l].astype(BF16), ln_g[l], ln_b[l], w_p, b_p)
        else:
            h = _out_proj(ys, h, w_out[l].astype(BF16), ln_g[l], ln_b[l])
    return h.reshape(bsz, seq, d)
```

```python
import functools

import numpy as np
import jax
import jax.numpy as jnp
from jax import lax
from jax.experimental import pallas as pl
from jax.experimental.pallas import tpu as pltpu

F32 = jnp.float32
BF16 = jnp.bfloat16
Y_DTYPE = BF16

DEPTH = 2
HEAD_DIM = 64
HALF = HEAD_DIM // 2
N_HEADS = 4
GROUP_W = N_HEADS * HEAD_DIM
N_GROUPS = 4
ROPE_THETA = 10000.0
NEG = -1e30
LN_EPS = 1e-5
A_WINDOW = 128
A_GROUP = 2
A_SUB = 16
V_AUG = HEAD_DIM + 16
LOG2E = 1.4426950408889634
PRO_ROWS = 512
B_CONV = 4
C_CHUNK = 64
C_SUB = 8
BC_STEP = 1024
C_EPS = 1e-6
CMP_LEN = 32
CMP_STRIDE = 16
SLC_LEN = 64
N_SEL = 8
NSA_WINDOW = 512
FORCE_SCORE = 1e6
DN_ALPHA = (2.0 * DEPTH) ** 0.25
QBLK = 128
SCALE = HEAD_DIM ** -0.5

VMEM_LIMIT = 56 * 1024 * 1024
PROJ_TM = 512
PROJ_SUB = 256

LANES = 128

ORIG_SPLITS = (
    ('a_q', 256), ('a_k', 128), ('a_v', 128), ('a_z', 256),
    ('b_q', 256), ('b_k', 256), ('b_v', 256), ('b_if', 8), ('b_o', 256), ('b_z', 256),
    ('c_q', 256), ('c_f', 256), ('c_i', 256), ('c_z', 256),
    ('d_q', 256), ('d_kvc', 128), ('d_ksv', 128), ('d_kwv', 128), ('d_g', 12), ('d_z', 256),
)
N_COLS = sum(w for _, w in ORIG_SPLITS)
GATE_SLOTS = ('b_if', 'd_g')
NEW_LAYOUT = (
    ('a_q', 256), ('a_k', 128), ('a_v', 128), ('a_z', 256), ('b_q', 256), ('b_k', 256), ('b_v', 256),
    ('b_o', 256), ('b_z', 256), ('c_q', 256), ('c_f', 256), ('c_i', 256), ('c_z', 256),
    ('d_q', 256), ('d_kvc', 128), ('d_ksv', 128), ('d_kwv', 128), ('gates', 128), ('d_z', 256),
)
N_PROJ = sum(w for _, w in NEW_LAYOUT)
D_GATE_LANE = dict(ORIG_SPLITS)['b_if']


def _offset(layout, name):
    off = 0
    for n, w in layout:
        if n == name:
            return off
        off += w
    raise KeyError(name)


def _new_offset(name):
    return _offset(NEW_LAYOUT, name)


def _permuted_cols(load):
    sizes = dict(ORIG_SPLITS)
    for name, width in NEW_LAYOUT:
        new = _new_offset(name)
        if name != 'gates':
            old = _offset(ORIG_SPLITS, name)
            lo = old // LANES * LANES
            hi = min(-(-(old + width) // LANES) * LANES, N_COLS)
            yield new, load(lo, hi)[..., old - lo:old - lo + width]
            continue
        gates, lane0 = None, 0
        for slot in GATE_SLOTS:
            old = _offset(ORIG_SPLITS, slot)
            lo = old // LANES * LANES
            tile = load(lo, lo + LANES)
            if old - lo != lane0:
                tile = pltpu.roll(tile, (lane0 - (old - lo)) % LANES, axis=tile.ndim - 1)
            lane = _iota(tile.shape, tile.ndim - 1)
            part = jnp.where((lane >= lane0) & (lane < lane0 + sizes[slot]), tile, 0.0)
            gates = part if gates is None else gates + part
            lane0 += sizes[slot]
        yield new, gates


def _prep_body(w_ref, b_ref, wo_ref, bo_ref):
    for new, cols in _permuted_cols(lambda lo, hi: w_ref[0, :, lo:hi]):
        wo_ref[:, new:new + cols.shape[-1]] = cols.astype(BF16)
    for new, cols in _permuted_cols(lambda lo, hi: b_ref[0, :, lo:hi]):
        bo_ref[:, new:new + cols.shape[-1]] = cols


def _prep_in_weights(w_in, b_in, layer, rows=128):
    depth, d, n = w_in.shape
    return pl.pallas_call(
        _prep_body,
        grid=(d // rows,),
        in_specs=[pl.BlockSpec((1, rows, n), lambda r: (layer, r, 0)),
                  pl.BlockSpec((1, 1, n), lambda r: (layer, 0, 0))],
        out_specs=[pl.BlockSpec((rows, N_PROJ), lambda r: (r, 0)),
                   pl.BlockSpec((1, N_PROJ), lambda r: (0, 0))],
        out_shape=[jax.ShapeDtypeStruct((d, N_PROJ), BF16), jax.ShapeDtypeStruct((1, N_PROJ), F32)],
        compiler_params=_cparams(("arbitrary",)),
        name="prep_in_weights",
    )(w_in, b_in.reshape(depth, 1, n))


def _mm(a, b):
    return jnp.dot(a.astype(BF16), b.astype(BF16), preferred_element_type=F32)


def _mm_nt(a, b):
    return lax.dot_general(a.astype(BF16), b.astype(BF16), (((1,), (1,)), ((), ())),
                           preferred_element_type=F32)


def _mm_tn(a, b):
    return lax.dot_general(a.astype(BF16), b.astype(BF16), (((0,), (0,)), ((), ())),
                           preferred_element_type=F32)


def _dot_bf16(a, b):
    return jnp.dot(a, b, preferred_element_type=F32)


def _split_bf16(x, terms):
    out = []
    for _ in range(terms - 1):
        t = x.astype(BF16)
        out.append(t)
        x = x - t.astype(F32)
    out.append(x.astype(BF16))
    return out


def _mm_hi(a, b):
    ah, al = _split_bf16(a, 2)
    bh, bl = _split_bf16(b, 2)
    return _dot_bf16(ah, bh) + (_dot_bf16(ah, bl) + _dot_bf16(al, bh))


def _mm_sel(sel, x):
    sel = sel.astype(BF16)
    x1, x2, x3 = _split_bf16(x, 3)
    return _dot_bf16(sel, x1) + (_dot_bf16(sel, x2) + _dot_bf16(sel, x3))


def _sigmoid(x):
    return 1.0 / (1.0 + jnp.exp2(x * (-LOG2E)))


def _silu(x):
    return x * _sigmoid(x)


def _log_sigmoid(x):
    return jnp.minimum(x, 0.0) - jnp.log1p(jnp.exp(-jnp.abs(x)))


def _rope(x, cos, sin_up, sin_dn):
    w = x.shape[-1]
    up = pltpu.roll(x, w - HALF, axis=1)
    dn = pltpu.roll(x, HALF, axis=1)
    return x * cos + up * sin_up + dn * sin_dn


def _rope_bf16(x, cos, sin_up, sin_dn, perm):
    swapped = jnp.dot(x.astype(BF16), perm, preferred_element_type=F32)
    return x * cos + swapped * (sin_up + sin_dn)


def _rope_perm(width, rot_lanes):
    lane = np.arange(width)
    src = np.where(lane % HEAD_DIM < HALF, lane + HALF, lane - HALF)
    rotated = (lane % LANES) < rot_lanes
    perm = (np.arange(width)[:, None] == src[None, :]) & rotated[None, :]
    return jnp.asarray(perm.astype(np.float32), dtype=BF16)


def _rope_tables_t(seq, blk):
    inv = ROPE_THETA ** (-jnp.arange(HALF, dtype=F32) / HALF)
    ang = jnp.arange(seq).astype(F32)[:, None] * inv[None, :]
    to_blocks = lambda a: a.T.reshape(HALF, seq // blk, blk).transpose(1, 0, 2)
    return to_blocks(jnp.cos(ang)), to_blocks(jnp.sin(ang))


def _rope_tables(pos, width, rot_lanes):
    inv = ROPE_THETA ** (-jnp.arange(HALF, dtype=F32) / HALF)
    ang = pos.astype(F32)[:, None] * inv[None, :]
    cos, sin = jnp.cos(ang), jnp.sin(ang)
    zero, one = jnp.zeros_like(sin), jnp.ones_like(cos)
    cos_h = jnp.concatenate([cos, cos], axis=1)
    up_h = jnp.concatenate([-sin, zero], axis=1)
    dn_h = jnp.concatenate([zero, sin], axis=1)
    id_c = jnp.concatenate([one, one], axis=1)
    id_s = jnp.concatenate([zero, zero], axis=1)
    cs, us, ds = [], [], []
    for l0 in range(0, width, HEAD_DIM):
        rot = (l0 % 128) < rot_lanes
        cs.append(cos_h if rot else id_c)
        us.append(up_h if rot else id_s)
        ds.append(dn_h if rot else id_s)
    return jnp.concatenate(cs, axis=1), jnp.concatenate(us, axis=1), jnp.concatenate(ds, axis=1)


def _iota(shape, dim):
    return lax.broadcasted_iota(jnp.int32, shape, dim)


def _cparams(sem):
    return pltpu.CompilerParams(dimension_semantics=sem, vmem_limit_bytes=VMEM_LIMIT)


def _ln_rows(x, g, b):
    mu = jnp.mean(x, axis=-1, keepdims=True)
    xc = x - mu
    var = jnp.mean(xc * xc, axis=-1, keepdims=True)
    return xc * lax.rsqrt(var + LN_EPS) * g + b


def _mix_out(rows, ya_ref, yb_ref, yc_ref, yd_ref, h_ref, w_ref, g_ref, b_ref):
    y = _mm(ya_ref[rows, :], w_ref[0:GROUP_W, :])
    y += _mm(yb_ref[rows, :], w_ref[GROUP_W:2 * GROUP_W, :])
    y += _mm(yc_ref[rows, :], w_ref[2 * GROUP_W:3 * GROUP_W, :])
    y += _mm(yd_ref[rows, :], w_ref[3 * GROUP_W:4 * GROUP_W, :])
    return _ln_rows(DN_ALPHA * h_ref[rows, :] + y, g_ref[...], b_ref[...])


def _sub_tiles(ref):
    return [slice(r, r + PROJ_SUB) for r in range(0, ref.shape[0], PROJ_SUB)]


def _ln_proj_body(x_ref, g_ref, b_ref, w_ref, bias_ref, h_ref, o_ref):
    for rows in _sub_tiles(x_ref):
        h = _ln_rows(x_ref[rows, :], g_ref[...], b_ref[...])
        h_ref[rows, :] = h
        o_ref[rows, :] = jnp.dot(h.astype(BF16), w_ref[...], preferred_element_type=F32) + bias_ref[...]


def _out_proj_body(ya_ref, yb_ref, yc_ref, yd_ref, hp_ref, wo_ref, g_ref, b_ref, w_ref, bias_ref, h_ref, o_ref):
    for rows in _sub_tiles(hp_ref):
        h = _mix_out(rows, ya_ref, yb_ref, yc_ref, yd_ref, hp_ref, wo_ref, g_ref, b_ref)
        h_ref[rows, :] = h
        o_ref[rows, :] = jnp.dot(h.astype(BF16), w_ref[...], preferred_element_type=F32) + bias_ref[...]


def _ln_in_proj(x2, g, b, w_bf, bias, tm=PROJ_TM):
    m, d = x2.shape
    n = w_bf.shape[1]
    row = lambda width: pl.BlockSpec((tm, width), lambda i: (i, 0))
    const = lambda shape: pl.BlockSpec(shape, lambda i: (0, 0))
    return pl.pallas_call(
        _ln_proj_body,
        grid=(m // tm,),
        in_specs=[row(d), const((1, d)), const((1, d)), const((d, n)), const((1, n))],
        out_specs=[row(d), row(n)],
        out_shape=[jax.ShapeDtypeStruct((m, d), F32), jax.ShapeDtypeStruct((m, n), F32)],
        compiler_params=_cparams(("parallel",)),
        name="ln_in_proj",
    )(x2, g.reshape(1, d), b.reshape(1, d), w_bf, bias.reshape(1, n))


def _out_in_proj(ys, h2, wo_bf, g, b, w_bf, bias, tm=PROJ_TM):
    m, d = h2.shape
    n = w_bf.shape[1]
    row = lambda width: pl.BlockSpec((tm, width), lambda i: (i, 0))
    const = lambda shape: pl.BlockSpec(shape, lambda i: (0, 0))
    return pl.pallas_call(
        _out_proj_body,
        grid=(m // tm,),
        in_specs=[row(GROUP_W)] * N_GROUPS + [row(d), const((N_GROUPS * GROUP_W, d)), const((1, d)), const((1, d)),
                                              const((d, n)), const((1, n))],
        out_specs=[row(d), row(n)],
        out_shape=[jax.ShapeDtypeStruct((m, d), F32), jax.ShapeDtypeStruct((m, n), F32)],
        compiler_params=_cparams(("parallel",)),
        name="out_in_proj",
    )(*[y.reshape(m, GROUP_W) for y in ys], h2, wo_bf, g.reshape(1, d), b.reshape(1, d), w_bf, bias.reshape(1, n))


def _out_body(ya_ref, yb_ref, yc_ref, yd_ref, h_ref, w_ref, g_ref, b_ref, o_ref):
    o_ref[...] = _mix_out(slice(None), ya_ref, yb_ref, yc_ref, yd_ref, h_ref, w_ref, g_ref, b_ref)


def _out_proj(ys, h2, w_bf, g, b, tm=512):
    m, d = h2.shape
    yspec = pl.BlockSpec((tm, GROUP_W), lambda i: (i, 0))
    return pl.pallas_call(
        _out_body,
        grid=(m // tm,),
        in_specs=[yspec, yspec, yspec, yspec,
                  pl.BlockSpec((tm, d), lambda i: (i, 0)),
                  pl.BlockSpec((4 * GROUP_W, d), lambda i: (0, 0)),
                  pl.BlockSpec((1, d), lambda i: (0, 0)),
                  pl.BlockSpec((1, d), lambda i: (0, 0))],
        out_specs=pl.BlockSpec((tm, d), lambda i: (i, 0)),
        out_shape=jax.ShapeDtypeStruct((m, d), F32),
        compiler_params=_cparams(("parallel",)),
        name="out_proj",
    )(*[y.reshape(m, GROUP_W) for y in ys], h2, w_bf, g.reshape(1, d), b.reshape(1, d))


def _mixer_a_body(sink_ref, q_ref, kv_ref, z_ref, cos_ref, up_ref, dn_ref, perm_ref, o_ref, k_ref, vt_ref):
    i = pl.program_id(1)
    seq = kv_ref.shape[1]
    n_kv = N_HEADS // A_GROUP
    low_half = _iota((1, LANES), 1) < HEAD_DIM

    @pl.when(i == 0)
    def _():
        def rope_rows(c, carry):
            for u in range(PRO_ROWS // QBLK):
                t0 = c * (PRO_ROWS // QBLK) + u
                rows = pl.ds(pl.multiple_of(t0 * QBLK, QBLK), QBLK)
                kr = _rope_bf16(kv_ref[0, rows, 0:LANES], cos_ref[rows, 0:LANES], up_ref[rows, 0:LANES],
                                dn_ref[rows, 0:LANES], perm_ref[0:LANES, 0:LANES])
                v_t = _mm_nt(_iota((LANES, LANES), 0) == _iota((LANES, LANES), 1), kv_ref[0, rows, LANES:2 * LANES])
                v_t = v_t.astype(BF16)
                ones_rows = jnp.ones((V_AUG - HEAD_DIM, QBLK), BF16)
                other = pltpu.roll(kr, HEAD_DIM, axis=1)
                for g in range(n_kv):
                    dup = jnp.where(low_half, kr, other) if g == 0 else jnp.where(low_half, other, kr)
                    k_ref[g, rows, :] = dup.astype(BF16)
                    vt_ref[g, t0] = jnp.concatenate([v_t[g * HEAD_DIM:(g + 1) * HEAD_DIM, :], ones_rows], axis=0)
            return carry
        lax.fori_loop(0, seq // PRO_ROWS, rope_rows, 0)

    n_blk = A_WINDOW // QBLK + 1
    span = n_blk * QBLK
    first_head = _iota((1, A_GROUP * QBLK), 1) < QBLK
    eye = (_iota((QBLK, QBLK), 0) == _iota((QBLK, QBLK), 1)).astype(BF16)

    def query_block(u):
        blk_i = i * A_SUB + u
        sub = slice(u * QBLK, (u + 1) * QBLK)
        q0 = pl.multiple_of(blk_i * QBLK, QBLK)
        qrows = pl.ds(q0, QBLK)
        qr = _rope(q_ref[0, sub, :], cos_ref[qrows, :], up_ref[qrows, :], dn_ref[qrows, :]) * (SCALE * LOG2E)
        kt0 = jnp.maximum(blk_i - A_WINDOW // QBLK, 0)
        k0 = pl.multiple_of(kt0 * QBLK, QBLK)
        rel = (q0 + _iota((1, QBLK), 1)) - (k0 + _iota((span, 1), 0))
        in_band = lax.bitcast_convert_type(rel, jnp.uint32) < A_WINDOW
        bias = jnp.concatenate([jnp.where(in_band, 0.0, NEG)] * A_GROUP, axis=1)
        outs = []
        for g in range(n_kv):
            q_pair = qr[:, g * LANES:(g + 1) * LANES]
            qg = jnp.concatenate([jnp.where(low_half, q_pair, 0.0), jnp.where(low_half, 0.0, q_pair)], axis=0)
            sink = jnp.where(first_head, sink_ref[g * A_GROUP], sink_ref[g * A_GROUP + 1]) * LOG2E
            s = _mm_nt(k_ref[g, pl.ds(k0, span), :], qg) + bias
            m = jnp.maximum(jnp.max(s, axis=0, keepdims=True), sink)
            p = jnp.exp2(s - m).astype(BF16)
            o = jnp.zeros((V_AUG, A_GROUP * QBLK), F32)
            for c in range(n_blk):
                o = o + jnp.dot(vt_ref[g, kt0 + c], p[c * QBLK:(c + 1) * QBLK], preferred_element_type=F32)
            o = o[0:HEAD_DIM] / (o[HEAD_DIM:HEAD_DIM + 1] + jnp.exp2(sink - m))
            outs.extend(o[:, r * QBLK:(r + 1) * QBLK] for r in range(A_GROUP))
        y = _mm_nt(eye, jnp.concatenate(outs, axis=0))
        o_ref[0, sub, :] = (y * _silu(z_ref[0, sub, :])).astype(Y_DTYPE)

    for u in range(A_SUB):
        query_block(u)


def _mixer_a(proj, sinks, tabs):
    assert A_GROUP == 2 and A_GROUP * HEAD_DIM == LANES, "a kv head's two query heads share one lane tile"
    bsz, seq, _ = proj.shape
    cos, up, dn = tabs
    blk = lambda name: _new_offset(name) // GROUP_W
    tspec = pl.BlockSpec((seq, GROUP_W), lambda b, i: (0, 0))
    return pl.pallas_call(
        _mixer_a_body,
        grid=(bsz, seq // (A_SUB * QBLK)),
        in_specs=[pl.BlockSpec(memory_space=pltpu.SMEM),
                  pl.BlockSpec((1, A_SUB * QBLK, GROUP_W), lambda b, i: (b, i, blk('a_q'))),
                  pl.BlockSpec((1, seq, GROUP_W), lambda b, i: (b, 0, blk('a_k'))),
                  pl.BlockSpec((1, A_SUB * QBLK, GROUP_W), lambda b, i: (b, i, blk('a_z'))),
                  tspec, tspec, tspec, pl.BlockSpec((GROUP_W, GROUP_W), lambda b, i: (0, 0))],
        out_specs=pl.BlockSpec((1, A_SUB * QBLK, GROUP_W), lambda b, i: (b, i, 0)),
        out_shape=jax.ShapeDtypeStruct((bsz, seq, GROUP_W), Y_DTYPE),
        scratch_shapes=[pltpu.VMEM((N_HEADS // A_GROUP, seq, LANES), BF16),
                        pltpu.VMEM((N_HEADS // A_GROUP, seq // QBLK, V_AUG, QBLK), BF16)],
        compiler_params=_cparams(("arbitrary", "arbitrary")),
        name="mixer_a",
    )(sinks, proj, proj, proj, cos, up, dn, _rope_perm(GROUP_W, LANES))


B_CHUNK = 128
B_AUG = V_AUG


def _conv_silu(x_ref, r0, prev, w, b):
    L = B_CHUNK
    x = x_ref[0, r0:r0 + L, :]
    acc = b + w[B_CONV - 1:B_CONV, :] * x
    for s in range(1, B_CONV):
        if r0 >= s:
            shifted = x_ref[0, r0 - s:r0 - s + L, :]
        else:
            row8 = _iota((8, x.shape[1]), 0)
            xs = pltpu.roll(x, s, axis=0)
            ps = pltpu.roll(prev, s, axis=0)
            shifted = jnp.concatenate([jnp.where(row8 < s, ps, xs[0:8]), xs[8:]], axis=0)
        acc = acc + w[B_CONV - 1 - s:B_CONV - s, :] * shifted
    return _silu(acc)


def _mixer_b_init(xprev_ref, c_ref, m_ref):
    xprev_ref[...] = jnp.zeros_like(xprev_ref)
    c_ref[...] = jnp.zeros_like(c_ref)
    m_ref[...] = jnp.zeros_like(m_ref)


def _mixer_b_main(xq_ref, xk_ref, v_ref, og_ref, z_ref, if_ref, cw_ref, cb_ref, o_ref, xprev_ref, c_ref, m_ref):
    L = B_CHUNK
    tri = (_iota((L, L), 1) <= _iota((L, L), 0)).astype(F32)
    key_first = _iota((L, L), 0) <= _iota((L, L), 1)
    half_of_lane = _iota((1, 128), 1) // HEAD_DIM
    ones_rows = jnp.ones((B_AUG - HEAD_DIM, L), F32)

    def chunk(r0, state):
        rows = slice(r0, r0 + L)
        qc = _conv_silu(xq_ref, r0, xprev_ref[:, 0:GROUP_W], cw_ref[:, 0:GROUP_W], cb_ref[:, 0:GROUP_W]) * SCALE
        kc = _conv_silu(xk_ref, r0, xprev_ref[:, GROUP_W:], cw_ref[:, GROUP_W:], cb_ref[:, GROUP_W:])
        q_t = qc.T
        v_t = v_ref[0, rows, :].T

        gates = if_ref[0, rows, :]
        bcum = _mm_sel(tri, _log_sigmoid(gates))
        gates_t = gates.T
        bcum_t = bcum.T
        outs, new_state = [], []
        for h in range(N_HEADS):
            pair = slice((h // 2) * 128, (h // 2 + 1) * 128)
            own = half_of_lane == (h % 2)
            k_pair = kc[:, pair]
            q_pair_t = q_t[pair, :]
            i_row = gates_t[h:h + 1, :]
            b_row = bcum_t[N_HEADS + h:N_HEADS + h + 1, :]
            c_col = gates[:, h:h + 1] - bcum[:, N_HEADS + h:N_HEADS + h + 1]
            cst, m_prev = state[h]
            dmat = jnp.where(key_first, b_row + c_col, NEG)
            inter = b_row + m_prev
            m_t = jnp.maximum(jnp.max(dmat, axis=0, keepdims=True), inter)
            smat = _mm(jnp.where(own, k_pair, 0.0), q_pair_t) * jnp.exp(dmat - m_t)
            vaug_t = jnp.concatenate([v_t[h * HEAD_DIM:(h + 1) * HEAD_DIM, :], ones_rows], axis=0)
            tot = _mm(vaug_t, smat) + jnp.exp(inter - m_t) * _mm(cst, q_pair_t)
            num = tot[0:HEAD_DIM, :]
            den = tot[HEAD_DIM:HEAD_DIM + 1, :]
            outs.append(num / jnp.maximum(jnp.abs(den), jnp.exp(-m_t)))
            b_last = b_row[:, L - 1:L]
            d_row = b_last - b_row + i_row
            m_new = jnp.maximum(b_last + m_prev, jnp.max(d_row, axis=1, keepdims=True))
            w_row = jnp.exp(d_row - m_new)
            decay = jnp.exp(b_last + m_prev - m_new)
            new_state.append((decay * cst + jnp.where(own, _mm(vaug_t * w_row, k_pair), 0.0), m_new))
        o_ref[0, rows, :] = (_sigmoid(og_ref[0, rows, :]) * jnp.concatenate(outs, axis=0).T
                             * _silu(z_ref[0, rows, :])).astype(Y_DTYPE)
        return new_state

    rows_step = xq_ref.shape[1]
    state = [(c_ref[h], m_ref[h:h + 1, 0:1]) for h in range(N_HEADS)]
    for u in range(rows_step // L):
        state = chunk(u * L, state)
        yield
    xprev_ref[:, 0:GROUP_W] = xq_ref[0, rows_step - 8:rows_step, :]
    xprev_ref[:, GROUP_W:] = xk_ref[0, rows_step - 8:rows_step, :]
    for h in range(N_HEADS):
        c_ref[h] = state[h][0]
        m_ref[h:h + 1, :] = jnp.broadcast_to(state[h][1], (1, 128))


def _mixer_c_main(layer, q_ref, f_ref, i_ref, z_ref, lb_ref, gn_ref, tri_ref, piv_ref, o_ref, st_ref):
    L = C_CHUNK
    nsub = L // C_SUB

    raw = lb_ref[...]
    ex = jnp.exp(raw - jnp.max(raw, axis=0, keepdims=True))
    if layer == 0:
        lb = jnp.zeros((1, GROUP_W), F32)
    else:
        lb = jnp.sum(ex[1:layer + 1], axis=0, keepdims=True) / jnp.sum(ex, axis=0, keepdims=True)
    tri = (_iota((L, L), 1) <= _iota((L, L), 0)).astype(F32)
    head_of_lane = _iota((1, GROUP_W), 1) // HEAD_DIM
    same_head = ((_iota((GROUP_W, GROUP_W), 0) // HEAD_DIM)
                 == (_iota((GROUP_W, GROUP_W), 1) // HEAD_DIM))
    ones_bd = same_head.astype(BF16)

    def chunk(rows, st):
        f = lb + (1.0 - lb) * _sigmoid(f_ref[0, rows, :])
        logf = jnp.log(f)
        k = 1.0 - f
        q = _silu(q_ref[0, rows, :])
        v = i_ref[0, rows, :]
        a = _mm_sel(tri, logf) * LOG2E
        o = _mm_nt(q * jnp.exp2(a), st)

        qsts, kts, vreps = [], [], []
        for i in range(1, nsub):
            r0 = i * C_SUB
            piv = a[r0 - 1:r0, :]
            qt = q[r0:r0 + C_SUB] * jnp.exp2(a[r0:r0 + C_SUB] - piv)
            qsts.extend(jnp.where(head_of_lane == h, qt, 0.0) for h in range(N_HEADS))
            kts.append((k[0:r0] * jnp.exp2(piv - a[0:r0])).astype(BF16))
            vreps.append(v[0:r0].astype(BF16))
        scores = _mm_nt(jnp.concatenate(qsts, axis=0), jnp.concatenate(kts, axis=0))
        r = _dot_bf16((scores * piv_ref[...]).astype(BF16), jnp.concatenate(vreps, axis=0))
        pieces = [o[0:C_SUB]]
        for i in range(1, nsub):
            acc = o[i * C_SUB:(i + 1) * C_SUB]
            for h in range(N_HEADS):
                g0 = ((i - 1) * N_HEADS + h) * C_SUB
                acc = acc + jnp.where(head_of_lane == h, r[g0:g0 + C_SUB], 0.0)
            pieces.append(acc)
        o = jnp.concatenate(pieces, axis=0)

        a3 = a.reshape(nsub, C_SUB, GROUP_W)
        q3 = q.reshape(nsub, C_SUB, GROUP_W)
        k3 = k.reshape(nsub, C_SUB, GROUP_W)
        v3 = v.reshape(nsub, C_SUB, GROUP_W)
        ps = []
        for s in range(C_SUB):
            e = jnp.exp2(a3 - a3[:, s:s + 1, :] + tri_ref[s])
            ps.append((q3 * k3[:, s:s + 1, :] * e).reshape(L, GROUP_W).astype(BF16))
        w = jnp.dot(jnp.concatenate(ps, axis=0), ones_bd, preferred_element_type=F32)
        for s in range(C_SUB):
            vs = jnp.broadcast_to(v3[:, s:s + 1, :], (nsub, C_SUB, GROUP_W)).reshape(L, GROUP_W)
            o = o + w[s * L:(s + 1) * L] * vs

        a_last = a[L - 1:L, :]
        kdec = k * jnp.exp2(a_last - a)
        st_new = st * jnp.exp2(a_last) + jnp.where(same_head, _mm_tn(v, kdec), 0.0)

        ms = jnp.dot((o * o).astype(BF16), ones_bd, preferred_element_type=F32) * (1.0 / HEAD_DIM)
        o = o * lax.rsqrt(ms + C_EPS) * gn_ref[...]
        o_ref[0, rows, :] = (o * _silu(z_ref[0, rows, :])).astype(Y_DTYPE)
        return st_new

    st = st_ref[...]
    for u in range(q_ref.shape[1] // L):
        st = chunk(slice(u * L, (u + 1) * L), st)
        yield
    st_ref[...] = st


N_B_IN, N_C_IN, N_B_SCRATCH = 8, 8, 3
C_PER_B = B_CHUNK // C_CHUNK


def _mixer_bc_body(layer, *refs):
    b_in = refs[:N_B_IN]
    c_in = refs[N_B_IN:N_B_IN + N_C_IN]
    ob_ref, oc_ref = refs[N_B_IN + N_C_IN:N_B_IN + N_C_IN + 2]
    scratch = refs[N_B_IN + N_C_IN + 2:]
    b_scratch, c_scratch = scratch[:N_B_SCRATCH], scratch[N_B_SCRATCH:]

    @pl.when(pl.program_id(1) == 0)
    def _():
        _mixer_b_init(*b_scratch)
        c_scratch[0][...] = jnp.zeros_like(c_scratch[0])

    pending = [_mixer_b_main(*b_in, ob_ref, *b_scratch), _mixer_c_main(layer, *c_in, oc_ref, *c_scratch)]
    weights = [1, C_PER_B]
    while pending:
        for gen, reps in list(zip(pending, weights)):
            for _ in range(reps):
                if next(gen, StopIteration) is StopIteration:
                    idx = pending.index(gen)
                    del pending[idx], weights[idx]
                    break


def _mixer_bc(proj, conv_w, conv_b, c_lb, norm_g, layer):
    bsz, seq, _ = proj.shape
    L = BC_STEP
    blk = lambda name: _new_offset(name) // GROUP_W
    spec = lambda name: pl.BlockSpec((1, L, GROUP_W), lambda b, c: (b, c, blk(name)))
    vec = pl.BlockSpec((1, GROUP_W), lambda b, c: (0, 0))
    t_ge_s = np.arange(C_SUB)[None, :, None] >= np.arange(C_SUB)[:, None, None]
    causal_bias = np.broadcast_to(np.where(t_ge_s, 0.0, NEG), (C_SUB, C_SUB, GROUP_W)).astype(np.float32)
    pivots = np.arange(1, C_CHUNK // C_SUB)
    row_piv = np.repeat(pivots, N_HEADS * C_SUB)
    col_piv = np.repeat(pivots, pivots * C_SUB)
    same_pivot = (row_piv[:, None] == col_piv[None, :]).astype(np.float32)
    out_spec = pl.BlockSpec((1, L, GROUP_W), lambda b, c: (b, c, 0))
    out_shape = jax.ShapeDtypeStruct((bsz, seq, GROUP_W), Y_DTYPE)
    return pl.pallas_call(
        functools.partial(_mixer_bc_body, layer),
        grid=(bsz, seq // L),
        in_specs=[spec('b_q'), spec('b_k'), spec('b_v'), spec('b_o'), spec('b_z'),
                  pl.BlockSpec((1, L, LANES), lambda b, c: (b, c, _new_offset('gates') // LANES)),
                  pl.BlockSpec((B_CONV, 2 * GROUP_W), lambda b, c: (0, 0)),
                  pl.BlockSpec((1, 2 * GROUP_W), lambda b, c: (0, 0)),
                  spec('c_q'), spec('c_f'), spec('c_i'), spec('c_z'),
                  pl.BlockSpec((DEPTH, GROUP_W), lambda b, c: (0, 0)), vec,
                  pl.BlockSpec((C_SUB, C_SUB, GROUP_W), lambda b, c: (0, 0, 0)),
                  pl.BlockSpec(same_pivot.shape, lambda b, c: (0, 0))],
        out_specs=[out_spec, out_spec],
        out_shape=[out_shape, out_shape],
        scratch_shapes=[pltpu.VMEM((8, 2 * GROUP_W), F32),
                        pltpu.VMEM((N_HEADS, B_AUG, 128), F32),
                        pltpu.VMEM((8, 128), F32),
                        pltpu.VMEM((GROUP_W, GROUP_W), F32)],
        compiler_params=_cparams(("arbitrary", "arbitrary")),
        name="mixer_bc",
    )(proj, proj, proj, proj, proj, proj, conv_w, conv_b.reshape(1, -1),
      proj, proj, proj, proj, c_lb, norm_g.reshape(1, -1), jnp.asarray(causal_bias), jnp.asarray(same_pivot))


N_CMP_PAD = 128
KT = 512
D_QBLK = 512
WBLK = 256
PE_ROWS = 16
SEL_ROWS = 16


def _mixer_d_body(q_ref, z_ref, ksv_ref, kwv_ref, cin_ref, g_ref, qcos_ref, qsin_ref, kcos_ref, kup_ref,
                  kdn_ref, kperm_ref, ccos_ref, cup_ref, cdn_ref, w1_ref, pe_ref, w2_ref, ov_ref, o_ref,
                  ks_ref, kw_ref, vts_ref, vtw_ref, kc_ref, vct_ref, sel_ref, qaug_ref):
    QBLK = D_QBLK
    i = pl.program_id(1)
    seq = ksv_ref.shape[1]
    half_cmp = CMP_LEN // 2

    @pl.when(i == 0)
    def _():
        def rope_rows(c, carry):
            for u in range(PRO_ROWS // KT):
                t0 = c * (PRO_ROWS // KT) + u
                rows = pl.ds(pl.multiple_of(t0 * KT, KT), KT)
                kvs_in = ksv_ref[0, rows, :]
                kvw_in = kwv_ref[0, rows, :]
                kvs = _rope_bf16(kvs_in, kcos_ref[rows, :], kup_ref[rows, :], kdn_ref[rows, :], kperm_ref[...])
                kvw = _rope_bf16(kvw_in, kcos_ref[rows, :], kup_ref[rows, :], kdn_ref[rows, :], kperm_ref[...])
                blk_in_tile = _iota((KT, HEAD_DIM), 0) // SLC_LEN
                onehot = jnp.where(blk_in_tile == _iota((KT, HEAD_DIM), 1), 1.0, 0.0)
                ks_ref[rows, :] = jnp.concatenate([kvs[:, 0:HEAD_DIM], onehot], axis=1).astype(BF16)
                kw_ref[rows, :] = kvw[:, 0:HEAD_DIM].astype(BF16)
                ones_rows = jnp.ones((V_AUG - HEAD_DIM, KT), BF16)
                pick_v = _iota((HEAD_DIM, LANES), 1) == _iota((HEAD_DIM, LANES), 0) + HEAD_DIM
                vts_ref[t0] = jnp.concatenate([_mm_nt(pick_v, kvs_in).astype(BF16), ones_rows], axis=0)
                vwt = jnp.concatenate([_mm_nt(pick_v, kvw_in).astype(BF16), ones_rows], axis=0)
                for w in range(KT // WBLK):
                    vtw_ref[t0 * (KT // WBLK) + w] = vwt[:, w * WBLK:(w + 1) * WBLK]
            return carry
        lax.fori_loop(0, seq // PRO_ROWS, rope_rows, 0)
        acc = jnp.zeros((N_CMP_PAD + PE_ROWS, 2 * LANES), F32)
        for rr in range(half_cmp // 2):
            xa = cin_ref[0, pl.ds(2 * rr, N_CMP_PAD, stride=CMP_STRIDE), :]
            xb = cin_ref[0, pl.ds(2 * rr + 1, N_CMP_PAD, stride=CMP_STRIDE), :]
            lhs = jnp.concatenate([jnp.concatenate([xa, xb], axis=1), pe_ref[rr]], axis=0)
            acc = acc + _mm_hi(lhs, w1_ref[rr])
        u0 = acc[0:N_CMP_PAD, 0:LANES] + acc[N_CMP_PAD:N_CMP_PAD + 1, 0:LANES]
        u1 = acc[0:N_CMP_PAD, LANES:] + acc[N_CMP_PAD + 1:N_CMP_PAD + 2, LANES:]
        pre = u0 + pltpu.roll(u1, N_CMP_PAD - 1, axis=0)
        cv = _rope(_mm_hi(_silu(pre), w2_ref[...]), ccos_ref[...], cup_ref[...], cdn_ref[...])
        kc_ref[...] = cv[:, 0:HEAD_DIM]
        vct_ref[...] = cv.T[HEAD_DIM:2 * HEAD_DIM, :].astype(BF16)

    q0 = pl.multiple_of(i * QBLK, QBLK)
    qt = (q_ref[0] * (SCALE * LOG2E)).T
    cos_t = qcos_ref[i]
    sin_t = qsin_ref[i]
    heads_t = []
    for h in range(N_HEADS):
        x1 = qt[h * HEAD_DIM:h * HEAD_DIM + HALF]
        x2 = qt[h * HEAD_DIM + HALF:(h + 1) * HEAD_DIM]
        heads_t.append(jnp.concatenate([x1 * cos_t - x2 * sin_t, x1 * sin_t + x2 * cos_t], axis=0))
    qst = jnp.concatenate(heads_t, axis=1)
    qst_bf = qst.astype(BF16)
    cols4 = N_HEADS * QBLK
    qpos = q0 + _iota((1, QBLK), 1)

    def lanes4(x):
        return jnp.concatenate([x] * N_HEADS, axis=1)

    sc = _mm_hi(kc_ref[...], qst)
    ends = _iota((N_CMP_PAD, 1), 0) * CMP_STRIDE + (CMP_LEN - 1)
    scm = sc + lanes4(jnp.where(ends <= qpos, 0.0, NEG))
    e = jnp.exp2(scm - jnp.max(scm, axis=0, keepdims=True))
    l = jnp.sum(e, axis=0, keepdims=True)
    any_valid = lanes4(jnp.where(qpos >= CMP_LEN - 1, 1.0, 0.0))
    pc = e * (any_valid / l)
    o_cmp = jnp.dot(vct_ref[...], pc.astype(BF16), preferred_element_type=F32)
    psum = pc[:, 0:QBLK] + pc[:, QBLK:2 * QBLK] + pc[:, 2 * QBLK:3 * QBLK] + pc[:, 3 * QBLK:4 * QBLK]

    n_slc = seq // SLC_LEN
    imp = _mm_sel(ov_ref[...], psum)
    jl = _iota((n_slc, 1), 0)
    cur = jnp.right_shift(qpos, SLC_LEN.bit_length() - 1)
    imp = jnp.where(jl == 0, FORCE_SCORE, imp)
    imp = jnp.where(jl == cur, FORCE_SCORE, imp)
    imp = jnp.where(jl == cur - 1, FORCE_SCORE, imp)
    imp = jnp.where(jl <= cur, imp, NEG)
    rank = jnp.zeros((n_slc, QBLK), F32)
    for jp in range(n_slc):
        cj = imp[jp:jp + 1, :]
        tie = jnp.where(jl > jp, 1.0, 0.0)
        rank = rank + jnp.where(cj > imp, 1.0, jnp.where(cj == imp, tie, 0.0))
    sel_bias = jnp.where(rank < N_SEL, 0.0, NEG)
    blocks_per_tile = KT // SLC_LEN
    for t in range(n_slc // blocks_per_tile):
        sel_ref[t] = sel_bias[t * blocks_per_tile:(t + 1) * blocks_per_tile, :]

    n_win = NSA_WINDOW // WBLK + 1
    span = n_win * WBLK
    win_parts = []
    for u in range(QBLK // WBLK):
        qcols = [slice(h * QBLK + u * WBLK, h * QBLK + (u + 1) * WBLK) for h in range(N_HEADS)]
        q_sub = jnp.concatenate([qst_bf[:, c] for c in qcols], axis=1)
        kt0 = jnp.maximum(i * (QBLK // WBLK) + u - NSA_WINDOW // WBLK, 0)
        k0 = pl.multiple_of(kt0 * WBLK, WBLK)
        rel = (q0 + u * WBLK + _iota((1, WBLK), 1)) - (k0 + _iota((span, 1), 0))
        in_band = lax.bitcast_convert_type(rel, jnp.uint32) < NSA_WINDOW
        s = jnp.dot(kw_ref[pl.ds(k0, span), :], q_sub, preferred_element_type=F32)
        s = s + lanes4(jnp.where(in_band, 0.0, NEG))
        p = jnp.exp2(s - jnp.max(s, axis=0, keepdims=True)).astype(BF16)
        acc = jnp.zeros((V_AUG, N_HEADS * WBLK), F32)
        for c in range(n_win):
            acc = acc + jnp.dot(vtw_ref[kt0 + c], p[c * WBLK:(c + 1) * WBLK], preferred_element_type=F32)
        win_parts.append(acc[0:HEAD_DIM] / acc[HEAD_DIM:HEAD_DIM + 1])
    o_win = jnp.concatenate([part[:, h * WBLK:(h + 1) * WBLK] for h in range(N_HEADS) for part in win_parts], axis=1)

    qaug_ref[0:HEAD_DIM, :] = qst_bf
    qaug_ref[HEAD_DIM:2 * HEAD_DIM, :] = jnp.zeros((HEAD_DIM, cols4), BF16)

    def sel_scores(kt):
        picks = jnp.concatenate([sel_ref[kt], jnp.zeros((SEL_ROWS - blocks_per_tile, QBLK), F32)], axis=0)
        qaug_ref[HEAD_DIM:HEAD_DIM + SEL_ROWS, :] = lanes4(picks).astype(BF16)
        kbase = pl.multiple_of(kt * KT, KT)
        return jnp.dot(ks_ref[pl.ds(kbase, KT), :], qaug_ref[...], preferred_element_type=F32)

    def sel_update(kt, s, m_old, acc):
        m_new = jnp.maximum(m_old, jnp.max(s, axis=0, keepdims=True))
        p = jnp.exp2(s - m_new)
        acc_new = jnp.exp2(m_old - m_new) * acc + jnp.dot(vts_ref[kt], p.astype(BF16), preferred_element_type=F32)
        return m_new, acc_new

    init = (jnp.full((1, cols4), NEG, F32), jnp.zeros((V_AUG, cols4), F32))
    m_s, acc_s = lax.fori_loop(0, i, lambda kt, c: sel_update(kt, sel_scores(kt), *c), init)

    hk = KT // 2
    tri = jnp.where(_iota((hk, hk), 0) <= _iota((hk, hk), 1), 0.0, NEG)
    picks = jnp.concatenate([sel_ref[i], jnp.zeros((SEL_ROWS - blocks_per_tile, QBLK), F32)], axis=0)
    qaug_ref[HEAD_DIM:HEAD_DIM + SEL_ROWS, :] = lanes4(picks).astype(BF16)
    kbase = pl.multiple_of(i * KT, KT)
    late = [slice(h * QBLK + hk, (h + 1) * QBLK) for h in range(N_HEADS)]
    bias_a = lanes4(jnp.concatenate([tri, jnp.zeros((hk, hk), F32)], axis=1))
    s_a = jnp.dot(ks_ref[pl.ds(kbase, hk), :], qaug_ref[...], preferred_element_type=F32) + bias_a
    m_a = jnp.maximum(m_s, jnp.max(s_a, axis=0, keepdims=True))
    acc_a = jnp.exp2(m_s - m_a) * acc_s + jnp.dot(vts_ref[i][:, 0:hk], jnp.exp2(s_a - m_a).astype(BF16),
                                                 preferred_element_type=F32)
    q_late = jnp.concatenate([qaug_ref[:, c] for c in late], axis=1)
    m_l = jnp.concatenate([m_a[:, c] for c in late], axis=1)
    acc_l = jnp.concatenate([acc_a[:, c] for c in late], axis=1)
    s_b = jnp.dot(ks_ref[pl.ds(kbase + hk, hk), :], q_late, preferred_element_type=F32) + lanes4(tri)
    m_b = jnp.maximum(m_l, jnp.max(s_b, axis=0, keepdims=True))
    acc_b = jnp.exp2(m_l - m_b) * acc_l + jnp.dot(vts_ref[i][:, hk:KT], jnp.exp2(s_b - m_b).astype(BF16),
                                                 preferred_element_type=F32)
    acc_s = jnp.concatenate([part for h in range(N_HEADS)
                             for part in (acc_a[:, h * QBLK:h * QBLK + hk], acc_b[:, h * hk:(h + 1) * hk])], axis=1)
    o_slc = acc_s[0:HEAD_DIM] / acc_s[HEAD_DIM:HEAD_DIM + 1]

    gate = _sigmoid(g_ref[0]).T
    outs = []
    for h in range(N_HEADS):
        cs = slice(h * QBLK, (h + 1) * QBLK)
        g0 = D_GATE_LANE + 3 * h
        outs.append(gate[g0:g0 + 1, :] * o_cmp[:, cs] + gate[g0 + 1:g0 + 2, :] * o_slc[:, cs]
                    + gate[g0 + 2:g0 + 3, :] * o_win[:, cs])
    o_ref[0] = (jnp.concatenate(outs, axis=0).T * _silu(z_ref[0])).astype(Y_DTYPE)


def _mixer_d(proj, qtabs, ktabs, ctabs, w1bd, pe2, w2bd, overlap):
    QBLK = D_QBLK
    bsz, seq, _ = proj.shape
    blk = lambda name: _new_offset(name) // GROUP_W
    full = lambda shape: pl.BlockSpec(shape, lambda b, i: (0,) * len(shape))
    return pl.pallas_call(
        _mixer_d_body,
        grid=(bsz, seq // QBLK),
        in_specs=[pl.BlockSpec((1, QBLK, GROUP_W), lambda b, i: (b, i, blk('d_q'))),
                  pl.BlockSpec((1, QBLK, GROUP_W), lambda b, i: (b, i, blk('d_z'))),
                  pl.BlockSpec((1, seq, 128), lambda b, i: (b, 0, _new_offset('d_ksv') // 128)),
                  pl.BlockSpec((1, seq, 128), lambda b, i: (b, 0, _new_offset('d_kwv') // 128)),
                  pl.BlockSpec((1, seq, 128), lambda b, i: (b, 0, _new_offset('d_kvc') // 128)),
                  pl.BlockSpec((1, QBLK, LANES), lambda b, i: (b, i, _new_offset('gates') // LANES)),
                  full((seq // QBLK, HALF, QBLK)), full((seq // QBLK, HALF, QBLK)),
                  full((seq, 128)), full((seq, 128)), full((seq, 128)), full((LANES, LANES)),
                  full((N_CMP_PAD, 128)), full((N_CMP_PAD, 128)), full((N_CMP_PAD, 128)),
                  full((CMP_LEN // 4, 2 * LANES, 2 * LANES)), full((CMP_LEN // 4, PE_ROWS, 2 * LANES)), full((128, 128)),
                  full((seq // SLC_LEN, N_CMP_PAD))],
        out_specs=pl.BlockSpec((1, QBLK, GROUP_W), lambda b, i: (b, i, 0)),
        out_shape=jax.ShapeDtypeStruct((bsz, seq, GROUP_W), Y_DTYPE),
        scratch_shapes=[pltpu.VMEM((seq, 2 * HEAD_DIM), BF16),
                        pltpu.VMEM((seq, HEAD_DIM), BF16),
                        pltpu.VMEM((seq // KT, V_AUG, KT), BF16),
                        pltpu.VMEM((seq // WBLK, V_AUG, WBLK), BF16),
                        pltpu.VMEM((N_CMP_PAD, HEAD_DIM), F32),
                        pltpu.VMEM((HEAD_DIM, N_CMP_PAD), BF16),
                        pltpu.VMEM((seq // KT, KT // SLC_LEN, QBLK), F32),
                        pltpu.VMEM((2 * HEAD_DIM, N_HEADS * QBLK), BF16)],
        compiler_params=_cparams(("arbitrary", "arbitrary")),
        name="mixer_d",
    )(proj, proj, proj, proj, proj, proj, *qtabs, *ktabs, _rope_perm(LANES, HEAD_DIM), *ctabs, w1bd, pe2, w2bd,
      overlap)


def _cmp_params(pe, w1, w2):
    w1 = w1.reshape(2, CMP_LEN, HEAD_DIM, HEAD_DIM)
    z = jnp.zeros((CMP_LEN, HEAD_DIM, HEAD_DIM), F32)
    w1bd = jnp.concatenate([jnp.concatenate([w1[0], z], axis=2), jnp.concatenate([z, w1[1]], axis=2)], axis=1)
    z2 = jnp.zeros((HEAD_DIM, HEAD_DIM), F32)
    w2bd = jnp.concatenate([jnp.concatenate([w2[0], z2], axis=1), jnp.concatenate([z2, w2[1]], axis=1)], axis=0)
    pe2 = jnp.concatenate([pe[0], pe[1]], axis=1)
    half = CMP_LEN // 2
    w1pair = jnp.stack([jnp.concatenate([jnp.concatenate([w1bd[r], w1bd[half + r]], axis=1),
                                         jnp.concatenate([w1bd[r + 1], w1bd[half + r + 1]], axis=1)], axis=0)
                        for r in range(0, half, 2)])
    pe_rows = jnp.stack([jnp.concatenate([jnp.concatenate([pe2[r], pe2[r + 1]])[None, :],
                                          jnp.concatenate([pe2[half + r], pe2[half + r + 1]])[None, :],
                                          jnp.zeros((PE_ROWS - 2, 2 * LANES), F32)], axis=0)
                         for r in range(0, half, 2)])
    return w1pair, pe_rows, w2bd


def _overlap_matrix(seq):
    n_cmp = (seq - CMP_LEN) // CMP_STRIDE + 1
    starts = np.arange(N_CMP_PAD) * CMP_STRIDE
    blk = np.arange(seq // SLC_LEN)
    ov = ((starts[None, :] < (blk[:, None] + 1) * SLC_LEN) & (starts[None, :] + CMP_LEN > blk[:, None] * SLC_LEN))
    ov = ov & (np.arange(N_CMP_PAD)[None, :] < n_cmp)
    return jnp.asarray(ov.astype(np.float32))


def kernel(x, ln0_g, ln0_b, w_in, b_in, a_sinks, b_conv_w, b_conv_b, c_lb, c_norm_g, d_cmp_pe, d_cmp_w1, d_cmp_w2,
           w_out, ln_g, ln_b):
    bsz, seq, d = x.shape
    pos = jnp.arange(seq)
    qtabs = _rope_tables(pos, GROUP_W, 128)
    ktabs = _rope_tables(pos, 128, HEAD_DIM)
    ctabs = _rope_tables(jnp.arange(N_CMP_PAD) * CMP_STRIDE + (CMP_LEN - 1), 128, HEAD_DIM)
    overlap = _overlap_matrix(seq)

    w_p, b_p = _prep_in_weights(w_in, b_in, 0)
    h, proj = _ln_in_proj(x.reshape(bsz * seq, d), ln0_g, ln0_b, w_p, b_p)
    for l in range(DEPTH):
        proj = proj.reshape(bsz, seq, N_PROJ)
        y_a = _mixer_a(proj, a_sinks[l], qtabs)
        y_b, y_c = _mixer_bc(proj, b_conv_w[l], b_conv_b[l], c_lb, c_norm_g[l], l)
        w1bd, pe2, w2bd = _cmp_params(d_cmp_pe[l], d_cmp_w1[l], d_cmp_w2[l])
        y_d = _mixer_d(proj, _rope_tables_t(seq, D_QBLK), ktabs, ctabs, w1bd, pe2, w2bd, overlap)
        ys = (y_a, y_b, y_c, y_d)
        if l + 1 < DEPTH:
            w_p, b_p = _prep_in_weights(w_in, b_in, l + 1)
            h, proj = _out_in_proj(ys, h, w_out[l].astype(BF16), ln_g[l], ln_b[l], w_p, b_p)
        else:
            h = _out_proj(ys, h, w_out[l].astype(BF16), ln_g[l], ln_b[l])
    return h.reshape(bsz, seq, d)
```

```python
import functools

import numpy as np
import jax
import jax.numpy as jnp
from jax import lax
from jax.experimental import pallas as pl
from jax.experimental.pallas import tpu as pltpu

F32 = jnp.float32
BF16 = jnp.bfloat16
Y_DTYPE = BF16

DEPTH = 2
HEAD_DIM = 64
HALF = HEAD_DIM // 2
N_HEADS = 4
GROUP_W = N_HEADS * HEAD_DIM
N_GROUPS = 4
ROPE_THETA = 10000.0
NEG = -1e30
LN_EPS = 1e-5
A_WINDOW = 128
A_GROUP = 2
A_SUB = 16
V_AUG = HEAD_DIM + 16
LOG2E = 1.4426950408889634
PRO_ROWS = 2048
A_PRO_ROWS = 2048
B_CONV = 4
C_CHUNK = 64
C_SUB = 8
BC_STEP = 1024
C_EPS = 1e-6
CMP_LEN = 32
CMP_STRIDE = 16
SLC_LEN = 64
N_SEL = 8
NSA_WINDOW = 512
FORCE_SCORE = 1e6
DN_ALPHA = (2.0 * DEPTH) ** 0.25
QBLK = 128
SCALE = HEAD_DIM ** -0.5

VMEM_LIMIT = 56 * 1024 * 1024
PROJ_TM = 512
PROJ_SUB = 256

LANES = 128

ORIG_SPLITS = (
    ('a_q', 256), ('a_k', 128), ('a_v', 128), ('a_z', 256),
    ('b_q', 256), ('b_k', 256), ('b_v', 256), ('b_if', 8), ('b_o', 256), ('b_z', 256),
    ('c_q', 256), ('c_f', 256), ('c_i', 256), ('c_z', 256),
    ('d_q', 256), ('d_kvc', 128), ('d_ksv', 128), ('d_kwv', 128), ('d_g', 12), ('d_z', 256),
)
N_COLS = sum(w for _, w in ORIG_SPLITS)
GATE_SLOTS = ('b_if', 'd_g')
NEW_LAYOUT = (
    ('a_q', 256), ('a_k', 128), ('a_v', 128), ('a_z', 256), ('b_q', 256), ('b_k', 256), ('b_v', 256),
    ('b_o', 256), ('b_z', 256), ('c_q', 256), ('c_f', 256), ('c_i', 256), ('c_z', 256),
    ('d_q', 256), ('d_kvc', 128), ('d_ksv', 128), ('d_kwv', 128), ('gates', 128), ('d_z', 256),
)
N_PROJ = sum(w for _, w in NEW_LAYOUT)
D_GATE_LANE = dict(ORIG_SPLITS)['b_if']


def _offset(layout, name):
    off = 0
    for n, w in layout:
        if n == name:
            return off
        off += w
    raise KeyError(name)


def _new_offset(name):
    return _offset(NEW_LAYOUT, name)


def _permuted_cols(load):
    sizes = dict(ORIG_SPLITS)
    for name, width in NEW_LAYOUT:
        new = _new_offset(name)
        if name != 'gates':
            old = _offset(ORIG_SPLITS, name)
            lo = old // LANES * LANES
            hi = min(-(-(old + width) // LANES) * LANES, N_COLS)
            yield new, load(lo, hi)[..., old - lo:old - lo + width]
            continue
        gates, lane0 = None, 0
        for slot in GATE_SLOTS:
            old = _offset(ORIG_SPLITS, slot)
            lo = old // LANES * LANES
            tile = load(lo, lo + LANES)
            if old - lo != lane0:
                tile = pltpu.roll(tile, (lane0 - (old - lo)) % LANES, axis=tile.ndim - 1)
            lane = _iota(tile.shape, tile.ndim - 1)
            part = jnp.where((lane >= lane0) & (lane < lane0 + sizes[slot]), tile, 0.0)
            gates = part if gates is None else gates + part
            lane0 += sizes[slot]
        yield new, gates


def _prep_body(w_ref, b_ref, wo_ref, bo_ref):
    for new, cols in _permuted_cols(lambda lo, hi: w_ref[0, :, lo:hi]):
        wo_ref[:, new:new + cols.shape[-1]] = cols.astype(BF16)
    for new, cols in _permuted_cols(lambda lo, hi: b_ref[0, :, lo:hi]):
        bo_ref[:, new:new + cols.shape[-1]] = cols


def _prep_in_weights(w_in, b_in, layer, rows=128):
    depth, d, n = w_in.shape
    return pl.pallas_call(
        _prep_body,
        grid=(d // rows,),
        in_specs=[pl.BlockSpec((1, rows, n), lambda r: (layer, r, 0)),
                  pl.BlockSpec((1, 1, n), lambda r: (layer, 0, 0))],
        out_specs=[pl.BlockSpec((rows, N_PROJ), lambda r: (r, 0)),
                   pl.BlockSpec((1, N_PROJ), lambda r: (0, 0))],
        out_shape=[jax.ShapeDtypeStruct((d, N_PROJ), BF16), jax.ShapeDtypeStruct((1, N_PROJ), F32)],
        compiler_params=_cparams(("arbitrary",)),
        name="prep_in_weights",
    )(w_in, b_in.reshape(depth, 1, n))


def _mm(a, b):
    return jnp.dot(a.astype(BF16), b.astype(BF16), preferred_element_type=F32)


def _mm_nt(a, b):
    return lax.dot_general(a.astype(BF16), b.astype(BF16), (((1,), (1,)), ((), ())),
                           preferred_element_type=F32)


def _mm_tn(a, b):
    return lax.dot_general(a.astype(BF16), b.astype(BF16), (((0,), (0,)), ((), ())),
                           preferred_element_type=F32)


def _dot_bf16(a, b):
    return jnp.dot(a, b, preferred_element_type=F32)


def _split_bf16(x, terms):
    out = []
    for _ in range(terms - 1):
        t = x.astype(BF16)
        out.append(t)
        x = x - t.astype(F32)
    out.append(x.astype(BF16))
    return out


def _mm_hi(a, b):
    ah, al = _split_bf16(a, 2)
    bh, bl = _split_bf16(b, 2)
    return _dot_bf16(ah, bh) + (_dot_bf16(ah, bl) + _dot_bf16(al, bh))


def _mm_sel(sel, x):
    sel = sel.astype(BF16)
    x1, x2, x3 = _split_bf16(x, 3)
    return _dot_bf16(sel, x1) + (_dot_bf16(sel, x2) + _dot_bf16(sel, x3))


def _sigmoid(x):
    return 1.0 / (1.0 + jnp.exp2(x * (-LOG2E)))


def _silu(x):
    return x * _sigmoid(x)


def _log_sigmoid(x):
    return jnp.minimum(x, 0.0) - jnp.log1p(jnp.exp(-jnp.abs(x)))


def _rope(x, cos, sin_up, sin_dn):
    w = x.shape[-1]
    up = pltpu.roll(x, w - HALF, axis=1)
    dn = pltpu.roll(x, HALF, axis=1)
    return x * cos + up * sin_up + dn * sin_dn


def _rope_bf16(x, cos, sin_up, sin_dn, perm):
    swapped = jnp.dot(x.astype(BF16), perm, preferred_element_type=F32)
    return x * cos + swapped * (sin_up + sin_dn)


def _rope_perm(width, rot_lanes):
    lane = np.arange(width)
    src = np.where(lane % HEAD_DIM < HALF, lane + HALF, lane - HALF)
    rotated = (lane % LANES) < rot_lanes
    perm = (np.arange(width)[:, None] == src[None, :]) & rotated[None, :]
    return jnp.asarray(perm.astype(np.float32), dtype=BF16)


def _rope_tables_t(seq, blk):
    inv = ROPE_THETA ** (-jnp.arange(HALF, dtype=F32) / HALF)
    ang = jnp.arange(seq).astype(F32)[:, None] * inv[None, :]
    to_blocks = lambda a: a.T.reshape(HALF, seq // blk, blk).transpose(1, 0, 2)
    return to_blocks(jnp.cos(ang)), to_blocks(jnp.sin(ang))


def _rope_tables(pos, width, rot_lanes):
    inv = ROPE_THETA ** (-jnp.arange(HALF, dtype=F32) / HALF)
    ang = pos.astype(F32)[:, None] * inv[None, :]
    cos, sin = jnp.cos(ang), jnp.sin(ang)
    zero, one = jnp.zeros_like(sin), jnp.ones_like(cos)
    cos_h = jnp.concatenate([cos, cos], axis=1)
    up_h = jnp.concatenate([-sin, zero], axis=1)
    dn_h = jnp.concatenate([zero, sin], axis=1)
    id_c = jnp.concatenate([one, one], axis=1)
    id_s = jnp.concatenate([zero, zero], axis=1)
    cs, us, ds = [], [], []
    for l0 in range(0, width, HEAD_DIM):
        rot = (l0 % 128) < rot_lanes
        cs.append(cos_h if rot else id_c)
        us.append(up_h if rot else id_s)
        ds.append(dn_h if rot else id_s)
    return jnp.concatenate(cs, axis=1), jnp.concatenate(us, axis=1), jnp.concatenate(ds, axis=1)


def _iota(shape, dim):
    return lax.broadcasted_iota(jnp.int32, shape, dim)


def _cparams(sem):
    return pltpu.CompilerParams(dimension_semantics=sem, vmem_limit_bytes=VMEM_LIMIT)


def _ln_rows(x, g, b):
    mu = jnp.mean(x, axis=-1, keepdims=True)
    xc = x - mu
    var = jnp.mean(xc * xc, axis=-1, keepdims=True)
    return xc * lax.rsqrt(var + LN_EPS) * g + b


def _mix_out(rows, ya_ref, yb_ref, yc_ref, yd_ref, h_ref, w_ref, g_ref, b_ref):
    y = _mm(ya_ref[rows, :], w_ref[0:GROUP_W, :])
    y += _mm(yb_ref[rows, :], w_ref[GROUP_W:2 * GROUP_W, :])
    y += _mm(yc_ref[rows, :], w_ref[2 * GROUP_W:3 * GROUP_W, :])
    y += _mm(yd_ref[rows, :], w_ref[3 * GROUP_W:4 * GROUP_W, :])
    return _ln_rows(DN_ALPHA * h_ref[rows, :] + y, g_ref[...], b_ref[...])


def _sub_tiles(ref):
    return [slice(r, r + PROJ_SUB) for r in range(0, ref.shape[0], PROJ_SUB)]


def _ln_proj_body(x_ref, g_ref, b_ref, w_ref, bias_ref, h_ref, o_ref):
    for rows in _sub_tiles(x_ref):
        h = _ln_rows(x_ref[rows, :], g_ref[...], b_ref[...])
        h_ref[rows, :] = h
        o_ref[rows, :] = jnp.dot(h.astype(BF16), w_ref[...], preferred_element_type=F32) + bias_ref[...]


def _out_proj_body(ya_ref, yb_ref, yc_ref, yd_ref, hp_ref, wo_ref, g_ref, b_ref, w_ref, bias_ref, h_ref, o_ref):
    for rows in _sub_tiles(hp_ref):
        h = _mix_out(rows, ya_ref, yb_ref, yc_ref, yd_ref, hp_ref, wo_ref, g_ref, b_ref)
        h_ref[rows, :] = h
        o_ref[rows, :] = jnp.dot(h.astype(BF16), w_ref[...], preferred_element_type=F32) + bias_ref[...]


def _ln_in_proj(x2, g, b, w_bf, bias, tm=PROJ_TM):
    m, d = x2.shape
    n = w_bf.shape[1]
    row = lambda width: pl.BlockSpec((tm, width), lambda i: (i, 0))
    const = lambda shape: pl.BlockSpec(shape, lambda i: (0, 0))
    return pl.pallas_call(
        _ln_proj_body,
        grid=(m // tm,),
        in_specs=[row(d), const((1, d)), const((1, d)), const((d, n)), const((1, n))],
        out_specs=[row(d), row(n)],
        out_shape=[jax.ShapeDtypeStruct((m, d), F32), jax.ShapeDtypeStruct((m, n), F32)],
        compiler_params=_cparams(("parallel",)),
        name="ln_in_proj",
    )(x2, g.reshape(1, d), b.reshape(1, d), w_bf, bias.reshape(1, n))


def _out_in_proj(ys, h2, wo_bf, g, b, w_bf, bias, tm=PROJ_TM):
    m, d = h2.shape
    n = w_bf.shape[1]
    row = lambda width: pl.BlockSpec((tm, width), lambda i: (i, 0))
    const = lambda shape: pl.BlockSpec(shape, lambda i: (0, 0))
    return pl.pallas_call(
        _out_proj_body,
        grid=(m // tm,),
        in_specs=[row(GROUP_W)] * N_GROUPS + [row(d), const((N_GROUPS * GROUP_W, d)), const((1, d)), const((1, d)),
                                              const((d, n)), const((1, n))],
        out_specs=[row(d), row(n)],
        out_shape=[jax.ShapeDtypeStruct((m, d), F32), jax.ShapeDtypeStruct((m, n), F32)],
        compiler_params=_cparams(("parallel",)),
        name="out_in_proj",
    )(*[y.reshape(m, GROUP_W) for y in ys], h2, wo_bf, g.reshape(1, d), b.reshape(1, d), w_bf, bias.reshape(1, n))


def _out_body(ya_ref, yb_ref, yc_ref, yd_ref, h_ref, w_ref, g_ref, b_ref, o_ref):
    o_ref[...] = _mix_out(slice(None), ya_ref, yb_ref, yc_ref, yd_ref, h_ref, w_ref, g_ref, b_ref)


def _out_proj(ys, h2, w_bf, g, b, tm=512):
    m, d = h2.shape
    yspec = pl.BlockSpec((tm, GROUP_W), lambda i: (i, 0))
    return pl.pallas_call(
        _out_body,
        grid=(m // tm,),
        in_specs=[yspec, yspec, yspec, yspec,
                  pl.BlockSpec((tm, d), lambda i: (i, 0)),
                  pl.BlockSpec((4 * GROUP_W, d), lambda i: (0, 0)),
                  pl.BlockSpec((1, d), lambda i: (0, 0)),
                  pl.BlockSpec((1, d), lambda i: (0, 0))],
        out_specs=pl.BlockSpec((tm, d), lambda i: (i, 0)),
        out_shape=jax.ShapeDtypeStruct((m, d), F32),
        compiler_params=_cparams(("parallel",)),
        name="out_proj",
    )(*[y.reshape(m, GROUP_W) for y in ys], h2, w_bf, g.reshape(1, d), b.reshape(1, d))


def _mixer_a_body(sink_ref, q_ref, kv_ref, z_ref, cos_ref, up_ref, dn_ref, perm_ref, o_ref, k_ref, vt_ref):
    i = pl.program_id(1)
    seq = kv_ref.shape[1]
    n_kv = N_HEADS // A_GROUP
    low_half = _iota((1, LANES), 1) < HEAD_DIM

    @pl.when(i == 0)
    def _():
        def rope_rows(c, carry):
            for u in range(A_PRO_ROWS // QBLK):
                t0 = c * (A_PRO_ROWS // QBLK) + u
                rows = pl.ds(pl.multiple_of(t0 * QBLK, QBLK), QBLK)
                kr = _rope_bf16(kv_ref[0, rows, 0:LANES], cos_ref[rows, 0:LANES], up_ref[rows, 0:LANES],
                                dn_ref[rows, 0:LANES], perm_ref[0:LANES, 0:LANES])
                v_t = _mm_nt(_iota((LANES, LANES), 0) == _iota((LANES, LANES), 1), kv_ref[0, rows, LANES:2 * LANES])
                v_t = v_t.astype(BF16)
                ones_rows = jnp.ones((V_AUG - HEAD_DIM, QBLK), BF16)
                other = pltpu.roll(kr, HEAD_DIM, axis=1)
                for g in range(n_kv):
                    dup = jnp.where(low_half, kr, other) if g == 0 else jnp.where(low_half, other, kr)
                    k_ref[g, rows, :] = dup.astype(BF16)
                    vt_ref[g, t0] = jnp.concatenate([v_t[g * HEAD_DIM:(g + 1) * HEAD_DIM, :], ones_rows], axis=0)
            return carry
        lax.fori_loop(0, seq // A_PRO_ROWS, rope_rows, 0)

    n_blk = A_WINDOW // QBLK + 1
    span = n_blk * QBLK
    first_head = _iota((1, A_GROUP * QBLK), 1) < QBLK
    eye = (_iota((QBLK, QBLK), 0) == _iota((QBLK, QBLK), 1)).astype(BF16)

    def query_block(u):
        blk_i = i * A_SUB + u
        sub = slice(u * QBLK, (u + 1) * QBLK)
        q0 = pl.multiple_of(blk_i * QBLK, QBLK)
        qrows = pl.ds(q0, QBLK)
        qr = _rope(q_ref[0, sub, :], cos_ref[qrows, :], up_ref[qrows, :], dn_ref[qrows, :]) * (SCALE * LOG2E)
        kt0 = jnp.maximum(blk_i - A_WINDOW // QBLK, 0)
        k0 = pl.multiple_of(kt0 * QBLK, QBLK)
        rel = (q0 + _iota((1, QBLK), 1)) - (k0 + _iota((span, 1), 0))
        in_band = lax.bitcast_convert_type(rel, jnp.uint32) < A_WINDOW
        bias = jnp.concatenate([jnp.where(in_band, 0.0, NEG)] * A_GROUP, axis=1)
        outs = []
        for g in range(n_kv):
            q_pair = qr[:, g * LANES:(g + 1) * LANES]
            qg = jnp.concatenate([jnp.where(low_half, q_pair, 0.0), jnp.where(low_half, 0.0, q_pair)], axis=0)
            sink = jnp.where(first_head, sink_ref[g * A_GROUP], sink_ref[g * A_GROUP + 1]) * LOG2E
            s = _mm_nt(k_ref[g, pl.ds(k0, span), :], qg) + bias
            m = jnp.maximum(jnp.max(s, axis=0, keepdims=True), sink)
            p = jnp.exp2(s - m).astype(BF16)
            o = jnp.zeros((V_AUG, A_GROUP * QBLK), F32)
            for c in range(n_blk):
                o = o + jnp.dot(vt_ref[g, kt0 + c], p[c * QBLK:(c + 1) * QBLK], preferred_element_type=F32)
            o = o[0:HEAD_DIM] / (o[HEAD_DIM:HEAD_DIM + 1] + jnp.exp2(sink - m))
            outs.extend(o[:, r * QBLK:(r + 1) * QBLK] for r in range(A_GROUP))
        y = _mm_nt(eye, jnp.concatenate(outs, axis=0))
        o_ref[0, sub, :] = (y * _silu(z_ref[0, sub, :])).astype(Y_DTYPE)

    for u in range(A_SUB):
        query_block(u)


def _mixer_a(proj, sinks, tabs):
    assert A_GROUP == 2 and A_GROUP * HEAD_DIM == LANES, "a kv head's two query heads share one lane tile"
    bsz, seq, _ = proj.shape
    cos, up, dn = tabs
    blk = lambda name: _new_offset(name) // GROUP_W
    tspec = pl.BlockSpec((seq, GROUP_W), lambda b, i: (0, 0))
    return pl.pallas_call(
        _mixer_a_body,
        grid=(bsz, seq // (A_SUB * QBLK)),
        in_specs=[pl.BlockSpec(memory_space=pltpu.SMEM),
                  pl.BlockSpec((1, A_SUB * QBLK, GROUP_W), lambda b, i: (b, i, blk('a_q'))),
                  pl.BlockSpec((1, seq, GROUP_W), lambda b, i: (b, 0, blk('a_k'))),
                  pl.BlockSpec((1, A_SUB * QBLK, GROUP_W), lambda b, i: (b, i, blk('a_z'))),
                  tspec, tspec, tspec, pl.BlockSpec((GROUP_W, GROUP_W), lambda b, i: (0, 0))],
        out_specs=pl.BlockSpec((1, A_SUB * QBLK, GROUP_W), lambda b, i: (b, i, 0)),
        out_shape=jax.ShapeDtypeStruct((bsz, seq, GROUP_W), Y_DTYPE),
        scratch_shapes=[pltpu.VMEM((N_HEADS // A_GROUP, seq, LANES), BF16),
                        pltpu.VMEM((N_HEADS // A_GROUP, seq // QBLK, V_AUG, QBLK), BF16)],
        compiler_params=_cparams(("arbitrary", "arbitrary")),
        name="mixer_a",
    )(sinks, proj, proj, proj, cos, up, dn, _rope_perm(GROUP_W, LANES))


B_CHUNK = 128
B_AUG = V_AUG


def _conv_silu(x_ref, r0, prev, w, b):
    L = B_CHUNK
    x = x_ref[0, r0:r0 + L, :]
    acc = b + w[B_CONV - 1:B_CONV, :] * x
    for s in range(1, B_CONV):
        if r0 >= s:
            shifted = x_ref[0, r0 - s:r0 - s + L, :]
        else:
            row8 = _iota((8, x.shape[1]), 0)
            xs = pltpu.roll(x, s, axis=0)
            ps = pltpu.roll(prev, s, axis=0)
            shifted = jnp.concatenate([jnp.where(row8 < s, ps, xs[0:8]), xs[8:]], axis=0)
        acc = acc + w[B_CONV - 1 - s:B_CONV - s, :] * shifted
    return _silu(acc)


def _mixer_b_init(xprev_ref, c_ref, m_ref):
    xprev_ref[...] = jnp.zeros_like(xprev_ref)
    c_ref[...] = jnp.zeros_like(c_ref)
    m_ref[...] = jnp.zeros_like(m_ref)


def _mixer_b_main(xq_ref, xk_ref, v_ref, og_ref, z_ref, if_ref, cw_ref, cb_ref, o_ref, xprev_ref, c_ref, m_ref):
    L = B_CHUNK
    tri = (_iota((L, L), 1) <= _iota((L, L), 0)).astype(F32)
    key_first = _iota((L, L), 0) <= _iota((L, L), 1)
    half_of_lane = _iota((1, 128), 1) // HEAD_DIM
    ones_rows = jnp.ones((B_AUG - HEAD_DIM, L), F32)

    def chunk(r0, state):
        rows = slice(r0, r0 + L)
        qc = _conv_silu(xq_ref, r0, xprev_ref[:, 0:GROUP_W], cw_ref[:, 0:GROUP_W], cb_ref[:, 0:GROUP_W]) * SCALE
        kc = _conv_silu(xk_ref, r0, xprev_ref[:, GROUP_W:], cw_ref[:, GROUP_W:], cb_ref[:, GROUP_W:])
        q_t = qc.T
        v_t = v_ref[0, rows, :].T

        gates = if_ref[0, rows, :]
        bcum = _mm_sel(tri, _log_sigmoid(gates))
        gates_t = gates.T
        bcum_t = bcum.T
        outs, new_state = [], []
        for h in range(N_HEADS):
            pair = slice((h // 2) * 128, (h // 2 + 1) * 128)
            own = half_of_lane == (h % 2)
            k_pair = kc[:, pair]
            q_pair_t = q_t[pair, :]
            i_row = gates_t[h:h + 1, :]
            b_row = bcum_t[N_HEADS + h:N_HEADS + h + 1, :]
            c_col = gates[:, h:h + 1] - bcum[:, N_HEADS + h:N_HEADS + h + 1]
            cst, m_prev = state[h]
            dmat = jnp.where(key_first, b_row + c_col, NEG)
            inter = b_row + m_prev
            m_t = jnp.maximum(jnp.max(dmat, axis=0, keepdims=True), inter)
            smat = _mm(jnp.where(own, k_pair, 0.0), q_pair_t) * jnp.exp(dmat - m_t)
            vaug_t = jnp.concatenate([v_t[h * HEAD_DIM:(h + 1) * HEAD_DIM, :], ones_rows], axis=0)
            tot = _mm(vaug_t, smat) + jnp.exp(inter - m_t) * _mm(cst, q_pair_t)
            num = tot[0:HEAD_DIM, :]
            den = tot[HEAD_DIM:HEAD_DIM + 1, :]
            outs.append(num / jnp.maximum(jnp.abs(den), jnp.exp(-m_t)))
            b_last = b_row[:, L - 1:L]
            d_row = b_last - b_row + i_row
            m_new = jnp.maximum(b_last + m_prev, jnp.max(d_row, axis=1, keepdims=True))
            w_row = jnp.exp(d_row - m_new)
            decay = jnp.exp(b_last + m_prev - m_new)
            new_state.append((decay * cst + jnp.where(own, _mm(vaug_t * w_row, k_pair), 0.0), m_new))
        o_ref[0, rows, :] = (_sigmoid(og_ref[0, rows, :]) * jnp.concatenate(outs, axis=0).T
                             * _silu(z_ref[0, rows, :])).astype(Y_DTYPE)
        return new_state

    rows_step = xq_ref.shape[1]
    state = [(c_ref[h], m_ref[h:h + 1, 0:1]) for h in range(N_HEADS)]
    for u in range(rows_step // L):
        state = chunk(u * L, state)
        yield
    xprev_ref[:, 0:GROUP_W] = xq_ref[0, rows_step - 8:rows_step, :]
    xprev_ref[:, GROUP_W:] = xk_ref[0, rows_step - 8:rows_step, :]
    for h in range(N_HEADS):
        c_ref[h] = state[h][0]
        m_ref[h:h + 1, :] = jnp.broadcast_to(state[h][1], (1, 128))


def _mixer_c_main(layer, q_ref, f_ref, i_ref, z_ref, lb_ref, gn_ref, tri_ref, piv_ref, o_ref, st_ref):
    L = C_CHUNK
    nsub = L // C_SUB

    raw = lb_ref[...]
    ex = jnp.exp(raw - jnp.max(raw, axis=0, keepdims=True))
    if layer == 0:
        lb = jnp.zeros((1, GROUP_W), F32)
    else:
        lb = jnp.sum(ex[1:layer + 1], axis=0, keepdims=True) / jnp.sum(ex, axis=0, keepdims=True)
    tri = (_iota((L, L), 1) <= _iota((L, L), 0)).astype(F32)
    head_of_lane = _iota((1, GROUP_W), 1) // HEAD_DIM
    same_head = ((_iota((GROUP_W, GROUP_W), 0) // HEAD_DIM)
                 == (_iota((GROUP_W, GROUP_W), 1) // HEAD_DIM))
    ones_bd = same_head.astype(BF16)

    def chunk(rows, st):
        f = lb + (1.0 - lb) * _sigmoid(f_ref[0, rows, :])
        logf = jnp.log(f)
        k = 1.0 - f
        q = _silu(q_ref[0, rows, :])
        v = i_ref[0, rows, :]
        a = _mm_sel(tri, logf) * LOG2E
        o = _mm_nt(q * jnp.exp2(a), st)

        qsts, kts, vreps = [], [], []
        for i in range(1, nsub):
            r0 = i * C_SUB
            piv = a[r0 - 1:r0, :]
            qt = q[r0:r0 + C_SUB] * jnp.exp2(a[r0:r0 + C_SUB] - piv)
            qsts.extend(jnp.where(head_of_lane == h, qt, 0.0) for h in range(N_HEADS))
            kts.append((k[0:r0] * jnp.exp2(piv - a[0:r0])).astype(BF16))
            vreps.append(v[0:r0].astype(BF16))
        scores = _mm_nt(jnp.concatenate(qsts, axis=0), jnp.concatenate(kts, axis=0))
        r = _dot_bf16((scores * piv_ref[...]).astype(BF16), jnp.concatenate(vreps, axis=0))
        pieces = [o[0:C_SUB]]
        for i in range(1, nsub):
            acc = o[i * C_SUB:(i + 1) * C_SUB]
            for h in range(N_HEADS):
                g0 = ((i - 1) * N_HEADS + h) * C_SUB
                acc = acc + jnp.where(head_of_lane == h, r[g0:g0 + C_SUB], 0.0)
            pieces.append(acc)
        o = jnp.concatenate(pieces, axis=0)

        a3 = a.reshape(nsub, C_SUB, GROUP_W)
        q3 = q.reshape(nsub, C_SUB, GROUP_W)
        k3 = k.reshape(nsub, C_SUB, GROUP_W)
        v3 = v.reshape(nsub, C_SUB, GROUP_W)
        ps = []
        for s in range(C_SUB):
            e = jnp.exp2(a3 - a3[:, s:s + 1, :] + tri_ref[s])
            ps.append((q3 * k3[:, s:s + 1, :] * e).reshape(L, GROUP_W).astype(BF16))
        w = jnp.dot(jnp.concatenate(ps, axis=0), ones_bd, preferred_element_type=F32)
        for s in range(C_SUB):
            vs = jnp.broadcast_to(v3[:, s:s + 1, :], (nsub, C_SUB, GROUP_W)).reshape(L, GROUP_W)
            o = o + w[s * L:(s + 1) * L] * vs

        a_last = a[L - 1:L, :]
        kdec = k * jnp.exp2(a_last - a)
        st_new = st * jnp.exp2(a_last) + jnp.where(same_head, _mm_tn(v, kdec), 0.0)

        ms = jnp.dot((o * o).astype(BF16), ones_bd, preferred_element_type=F32) * (1.0 / HEAD_DIM)
        o = o * lax.rsqrt(ms + C_EPS) * gn_ref[...]
        o_ref[0, rows, :] = (o * _silu(z_ref[0, rows, :])).astype(Y_DTYPE)
        return st_new

    st = st_ref[...]
    for u in range(q_ref.shape[1] // L):
        st = chunk(slice(u * L, (u + 1) * L), st)
        yield
    st_ref[...] = st


N_B_IN, N_C_IN, N_B_SCRATCH = 8, 8, 3
C_PER_B = B_CHUNK // C_CHUNK


def _mixer_bc_body(layer, *refs):
    b_in = refs[:N_B_IN]
    c_in = refs[N_B_IN:N_B_IN + N_C_IN]
    ob_ref, oc_ref = refs[N_B_IN + N_C_IN:N_B_IN + N_C_IN + 2]
    scratch = refs[N_B_IN + N_C_IN + 2:]
    b_scratch, c_scratch = scratch[:N_B_SCRATCH], scratch[N_B_SCRATCH:]

    @pl.when(pl.program_id(1) == 0)
    def _():
        _mixer_b_init(*b_scratch)
        c_scratch[0][...] = jnp.zeros_like(c_scratch[0])

    pending = [_mixer_b_main(*b_in, ob_ref, *b_scratch), _mixer_c_main(layer, *c_in, oc_ref, *c_scratch)]
    weights = [1, C_PER_B]
    while pending:
        for gen, reps in list(zip(pending, weights)):
            for _ in range(reps):
                if next(gen, StopIteration) is StopIteration:
                    idx = pending.index(gen)
                    del pending[idx], weights[idx]
                    break


def _mixer_bc(proj, conv_w, conv_b, c_lb, norm_g, layer):
    bsz, seq, _ = proj.shape
    L = BC_STEP
    blk = lambda name: _new_offset(name) // GROUP_W
    spec = lambda name: pl.BlockSpec((1, L, GROUP_W), lambda b, c: (b, c, blk(name)))
    vec = pl.BlockSpec((1, GROUP_W), lambda b, c: (0, 0))
    t_ge_s = np.arange(C_SUB)[None, :, None] >= np.arange(C_SUB)[:, None, None]
    causal_bias = np.broadcast_to(np.where(t_ge_s, 0.0, NEG), (C_SUB, C_SUB, GROUP_W)).astype(np.float32)
    pivots = np.arange(1, C_CHUNK // C_SUB)
    row_piv = np.repeat(pivots, N_HEADS * C_SUB)
    col_piv = np.repeat(pivots, pivots * C_SUB)
    same_pivot = (row_piv[:, None] == col_piv[None, :]).astype(np.float32)
    out_spec = pl.BlockSpec((1, L, GROUP_W), lambda b, c: (b, c, 0))
    out_shape = jax.ShapeDtypeStruct((bsz, seq, GROUP_W), Y_DTYPE)
    return pl.pallas_call(
        functools.partial(_mixer_bc_body, layer),
        grid=(bsz, seq // L),
        in_specs=[spec('b_q'), spec('b_k'), spec('b_v'), spec('b_o'), spec('b_z'),
                  pl.BlockSpec((1, L, LANES), lambda b, c: (b, c, _new_offset('gates') // LANES)),
                  pl.BlockSpec((B_CONV, 2 * GROUP_W), lambda b, c: (0, 0)),
                  pl.BlockSpec((1, 2 * GROUP_W), lambda b, c: (0, 0)),
                  spec('c_q'), spec('c_f'), spec('c_i'), spec('c_z'),
                  pl.BlockSpec((DEPTH, GROUP_W), lambda b, c: (0, 0)), vec,
                  pl.BlockSpec((C_SUB, C_SUB, GROUP_W), lambda b, c: (0, 0, 0)),
                  pl.BlockSpec(same_pivot.shape, lambda b, c: (0, 0))],
        out_specs=[out_spec, out_spec],
        out_shape=[out_shape, out_shape],
        scratch_shapes=[pltpu.VMEM((8, 2 * GROUP_W), F32),
                        pltpu.VMEM((N_HEADS, B_AUG, 128), F32),
                        pltpu.VMEM((8, 128), F32),
                        pltpu.VMEM((GROUP_W, GROUP_W), F32)],
        compiler_params=_cparams(("arbitrary", "arbitrary")),
        name="mixer_bc",
    )(proj, proj, proj, proj, proj, proj, conv_w, conv_b.reshape(1, -1),
      proj, proj, proj, proj, c_lb, norm_g.reshape(1, -1), jnp.asarray(causal_bias), jnp.asarray(same_pivot))


N_CMP_PAD = 128
KT = 512
D_QBLK = 512
WBLK = 256
PE_ROWS = 16
SEL_ROWS = 16


def _mixer_d_body(q_ref, z_ref, ksv_ref, kwv_ref, cin_ref, g_ref, qcos_ref, qsin_ref, kcos_ref, kup_ref,
                  kdn_ref, kperm_ref, ccos_ref, cup_ref, cdn_ref, w1_ref, pe_ref, w2_ref, ov_ref, o_ref,
                  ks_ref, kw_ref, vts_ref, vtw_ref, kc_ref, vct_ref, sel_ref, qaug_ref):
    QBLK = D_QBLK
    i = pl.program_id(1)
    seq = ksv_ref.shape[1]
    half_cmp = CMP_LEN // 2

    @pl.when(i == 0)
    def _():
        def rope_rows(c, carry):
            for u in range(PRO_ROWS // KT):
                t0 = c * (PRO_ROWS // KT) + u
                rows = pl.ds(pl.multiple_of(t0 * KT, KT), KT)
                kvs_in = ksv_ref[0, rows, :]
                kvw_in = kwv_ref[0, rows, :]
                kvs = _rope_bf16(kvs_in, kcos_ref[rows, :], kup_ref[rows, :], kdn_ref[rows, :], kperm_ref[...])
                kvw = _rope_bf16(kvw_in, kcos_ref[rows, :], kup_ref[rows, :], kdn_ref[rows, :], kperm_ref[...])
                blk_in_tile = _iota((KT, HEAD_DIM), 0) // SLC_LEN
                onehot = jnp.where(blk_in_tile == _iota((KT, HEAD_DIM), 1), 1.0, 0.0)
                ks_ref[rows, :] = jnp.concatenate([kvs[:, 0:HEAD_DIM], onehot], axis=1).astype(BF16)
                kw_ref[rows, :] = kvw[:, 0:HEAD_DIM].astype(BF16)
                ones_rows = jnp.ones((V_AUG - HEAD_DIM, KT), BF16)
                pick_v = _iota((HEAD_DIM, LANES), 1) == _iota((HEAD_DIM, LANES), 0) + HEAD_DIM
                vts_ref[t0] = jnp.concatenate([_mm_nt(pick_v, kvs_in).astype(BF16), ones_rows], axis=0)
                vwt = jnp.concatenate([_mm_nt(pick_v, kvw_in).astype(BF16), ones_rows], axis=0)
                for w in range(KT // WBLK):
                    vtw_ref[t0 * (KT // WBLK) + w] = vwt[:, w * WBLK:(w + 1) * WBLK]
            return carry
        lax.fori_loop(0, seq // PRO_ROWS, rope_rows, 0)
        acc = jnp.zeros((N_CMP_PAD + PE_ROWS, 2 * LANES), F32)
        for rr in range(half_cmp // 2):
            xa = cin_ref[0, pl.ds(2 * rr, N_CMP_PAD, stride=CMP_STRIDE), :]
            xb = cin_ref[0, pl.ds(2 * rr + 1, N_CMP_PAD, stride=CMP_STRIDE), :]
            lhs = jnp.concatenate([jnp.concatenate([xa, xb], axis=1), pe_ref[rr]], axis=0)
            acc = acc + _mm_hi(lhs, w1_ref[rr])
        u0 = acc[0:N_CMP_PAD, 0:LANES] + acc[N_CMP_PAD:N_CMP_PAD + 1, 0:LANES]
        u1 = acc[0:N_CMP_PAD, LANES:] + acc[N_CMP_PAD + 1:N_CMP_PAD + 2, LANES:]
        pre = u0 + pltpu.roll(u1, N_CMP_PAD - 1, axis=0)
        cv = _rope(_mm_hi(_silu(pre), w2_ref[...]), ccos_ref[...], cup_ref[...], cdn_ref[...])
        kc_ref[...] = cv[:, 0:HEAD_DIM]
        vct_ref[...] = cv.T[HEAD_DIM:2 * HEAD_DIM, :].astype(BF16)

    q0 = pl.multiple_of(i * QBLK, QBLK)
    qt = (q_ref[0] * (SCALE * LOG2E)).T
    cos_t = qcos_ref[i]
    sin_t = qsin_ref[i]
    heads_t = []
    for h in range(N_HEADS):
        x1 = qt[h * HEAD_DIM:h * HEAD_DIM + HALF]
        x2 = qt[h * HEAD_DIM + HALF:(h + 1) * HEAD_DIM]
        heads_t.append(jnp.concatenate([x1 * cos_t - x2 * sin_t, x1 * sin_t + x2 * cos_t], axis=0))
    qst = jnp.concatenate(heads_t, axis=1)
    qst_bf = qst.astype(BF16)
    cols4 = N_HEADS * QBLK
    qpos = q0 + _iota((1, QBLK), 1)

    def lanes4(x):
        return jnp.concatenate([x] * N_HEADS, axis=1)

    sc = _mm_hi(kc_ref[...], qst)
    ends = _iota((N_CMP_PAD, 1), 0) * CMP_STRIDE + (CMP_LEN - 1)
    scm = sc + lanes4(jnp.where(ends <= qpos, 0.0, NEG))
    e = jnp.exp2(scm - jnp.max(scm, axis=0, keepdims=True))
    l = jnp.sum(e, axis=0, keepdims=True)
    any_valid = lanes4(jnp.where(qpos >= CMP_LEN - 1, 1.0, 0.0))
    pc = e * (any_valid / l)
    o_cmp = jnp.dot(vct_ref[...], pc.astype(BF16), preferred_element_type=F32)
    psum = pc[:, 0:QBLK] + pc[:, QBLK:2 * QBLK] + pc[:, 2 * QBLK:3 * QBLK] + pc[:, 3 * QBLK:4 * QBLK]

    n_slc = seq // SLC_LEN
    imp = _mm_sel(ov_ref[...], psum)
    jl = _iota((n_slc, 1), 0)
    cur = jnp.right_shift(qpos, SLC_LEN.bit_length() - 1)
    imp = jnp.where(jl == 0, FORCE_SCORE, imp)
    imp = jnp.where(jl == cur, FORCE_SCORE, imp)
    imp = jnp.where(jl == cur - 1, FORCE_SCORE, imp)
    imp = jnp.where(jl <= cur, imp, NEG)
    rank = jnp.zeros((n_slc, QBLK), F32)
    for jp in range(n_slc):
        cj = imp[jp:jp + 1, :]
        tie = jnp.where(jl > jp, 1.0, 0.0)
        rank = rank + jnp.where(cj > imp, 1.0, jnp.where(cj == imp, tie, 0.0))
    sel_bias = jnp.where(rank < N_SEL, 0.0, NEG)
    blocks_per_tile = KT // SLC_LEN
    for t in range(n_slc // blocks_per_tile):
        sel_ref[t] = sel_bias[t * blocks_per_tile:(t + 1) * blocks_per_tile, :]

    n_win = NSA_WINDOW // WBLK + 1
    span = n_win * WBLK
    win_parts = []
    for u in range(QBLK // WBLK):
        qcols = [slice(h * QBLK + u * WBLK, h * QBLK + (u + 1) * WBLK) for h in range(N_HEADS)]
        q_sub = jnp.concatenate([qst_bf[:, c] for c in qcols], axis=1)
        kt0 = jnp.maximum(i * (QBLK // WBLK) + u - NSA_WINDOW // WBLK, 0)
        k0 = pl.multiple_of(kt0 * WBLK, WBLK)
        rel = (q0 + u * WBLK + _iota((1, WBLK), 1)) - (k0 + _iota((span, 1), 0))
        in_band = lax.bitcast_convert_type(rel, jnp.uint32) < NSA_WINDOW
        s = jnp.dot(kw_ref[pl.ds(k0, span), :], q_sub, preferred_element_type=F32)
        s = s + lanes4(jnp.where(in_band, 0.0, NEG))
        p = jnp.exp2(s - jnp.max(s, axis=0, keepdims=True)).astype(BF16)
        acc = jnp.zeros((V_AUG, N_HEADS * WBLK), F32)
        for c in range(n_win):
            acc = acc + jnp.dot(vtw_ref[kt0 + c], p[c * WBLK:(c + 1) * WBLK], preferred_element_type=F32)
        win_parts.append(acc[0:HEAD_DIM] / acc[HEAD_DIM:HEAD_DIM + 1])
    o_win = jnp.concatenate([part[:, h * WBLK:(h + 1) * WBLK] for h in range(N_HEADS) for part in win_parts], axis=1)

    qaug_ref[0:HEAD_DIM, :] = qst_bf
    qaug_ref[HEAD_DIM:2 * HEAD_DIM, :] = jnp.zeros((HEAD_DIM, cols4), BF16)

    def sel_scores(kt):
        picks = jnp.concatenate([sel_ref[kt], jnp.zeros((SEL_ROWS - blocks_per_tile, QBLK), F32)], axis=0)
        qaug_ref[HEAD_DIM:HEAD_DIM + SEL_ROWS, :] = lanes4(picks).astype(BF16)
        kbase = pl.multiple_of(kt * KT, KT)
        return jnp.dot(ks_ref[pl.ds(kbase, KT), :], qaug_ref[...], preferred_element_type=F32)

    def sel_update(kt, s, m_old, acc):
        m_new = jnp.maximum(m_old, jnp.max(s, axis=0, keepdims=True))
        p = jnp.exp2(s - m_new)
        acc_new = jnp.exp2(m_old - m_new) * acc + jnp.dot(vts_ref[kt], p.astype(BF16), preferred_element_type=F32)
        return m_new, acc_new

    init = (jnp.full((1, cols4), NEG, F32), jnp.zeros((V_AUG, cols4), F32))
    m_s, acc_s = lax.fori_loop(0, i, lambda kt, c: sel_update(kt, sel_scores(kt), *c), init)

    hk = KT // 2
    tri = jnp.where(_iota((hk, hk), 0) <= _iota((hk, hk), 1), 0.0, NEG)
    picks = jnp.concatenate([sel_ref[i], jnp.zeros((SEL_ROWS - blocks_per_tile, QBLK), F32)], axis=0)
    qaug_ref[HEAD_DIM:HEAD_DIM + SEL_ROWS, :] = lanes4(picks).astype(BF16)
    kbase = pl.multiple_of(i * KT, KT)
    late = [slice(h * QBLK + hk, (h + 1) * QBLK) for h in range(N_HEADS)]
    bias_a = lanes4(jnp.concatenate([tri, jnp.zeros((hk, hk), F32)], axis=1))
    s_a = jnp.dot(ks_ref[pl.ds(kbase, hk), :], qaug_ref[...], preferred_element_type=F32) + bias_a
    m_a = jnp.maximum(m_s, jnp.max(s_a, axis=0, keepdims=True))
    acc_a = jnp.exp2(m_s - m_a) * acc_s + jnp.dot(vts_ref[i][:, 0:hk], jnp.exp2(s_a - m_a).astype(BF16),
                                                 preferred_element_type=F32)
    q_late = jnp.concatenate([qaug_ref[:, c] for c in late], axis=1)
    m_l = jnp.concatenate([m_a[:, c] for c in late], axis=1)
    acc_l = jnp.concatenate([acc_a[:, c] for c in late], axis=1)
    s_b = jnp.dot(ks_ref[pl.ds(kbase + hk, hk), :], q_late, preferred_element_type=F32) + lanes4(tri)
    m_b = jnp.maximum(m_l, jnp.max(s_b, axis=0, keepdims=True))
    acc_b = jnp.exp2(m_l - m_b) * acc_l + jnp.dot(vts_ref[i][:, hk:KT], jnp.exp2(s_b - m_b).astype(BF16),
                                                 preferred_element_type=F32)
    acc_s = jnp.concatenate([part for h in range(N_HEADS)
                             for part in (acc_a[:, h * QBLK:h * QBLK + hk], acc_b[:, h * hk:(h + 1) * hk])], axis=1)
    o_slc = acc_s[0:HEAD_DIM] / acc_s[HEAD_DIM:HEAD_DIM + 1]

    gate = _sigmoid(g_ref[0]).T
    outs = []
    for h in range(N_HEADS):
        cs = slice(h * QBLK, (h + 1) * QBLK)
        g0 = D_GATE_LANE + 3 * h
        outs.append(gate[g0:g0 + 1, :] * o_cmp[:, cs] + gate[g0 + 1:g0 + 2, :] * o_slc[:, cs]
                    + gate[g0 + 2:g0 + 3, :] * o_win[:, cs])
    o_ref[0] = (jnp.concatenate(outs, axis=0).T * _silu(z_ref[0])).astype(Y_DTYPE)


def _mixer_d(proj, qtabs, ktabs, ctabs, w1bd, pe2, w2bd, overlap):
    QBLK = D_QBLK
    bsz, seq, _ = proj.shape
    blk = lambda name: _new_offset(name) // GROUP_W
    full = lambda shape: pl.BlockSpec(shape, lambda b, i: (0,) * len(shape))
    return pl.pallas_call(
        _mixer_d_body,
        grid=(bsz, seq // QBLK),
        in_specs=[pl.BlockSpec((1, QBLK, GROUP_W), lambda b, i: (b, i, blk('d_q'))),
                  pl.BlockSpec((1, QBLK, GROUP_W), lambda b, i: (b, i, blk('d_z'))),
                  pl.BlockSpec((1, seq, 128), lambda b, i: (b, 0, _new_offset('d_ksv') // 128)),
                  pl.BlockSpec((1, seq, 128), lambda b, i: (b, 0, _new_offset('d_kwv') // 128)),
                  pl.BlockSpec((1, seq, 128), lambda b, i: (b, 0, _new_offset('d_kvc') // 128)),
                  pl.BlockSpec((1, QBLK, LANES), lambda b, i: (b, i, _new_offset('gates') // LANES)),
                  full((seq // QBLK, HALF, QBLK)), full((seq // QBLK, HALF, QBLK)),
                  full((seq, 128)), full((seq, 128)), full((seq, 128)), full((LANES, LANES)),
                  full((N_CMP_PAD, 128)), full((N_CMP_PAD, 128)), full((N_CMP_PAD, 128)),
                  full((CMP_LEN // 4, 2 * LANES, 2 * LANES)), full((CMP_LEN // 4, PE_ROWS, 2 * LANES)), full((128, 128)),
                  full((seq // SLC_LEN, N_CMP_PAD))],
        out_specs=pl.BlockSpec((1, QBLK, GROUP_W), lambda b, i: (b, i, 0)),
        out_shape=jax.ShapeDtypeStruct((bsz, seq, GROUP_W), Y_DTYPE),
        scratch_shapes=[pltpu.VMEM((seq, 2 * HEAD_DIM), BF16),
                        pltpu.VMEM((seq, HEAD_DIM), BF16),
                        pltpu.VMEM((seq // KT, V_AUG, KT), BF16),
                        pltpu.VMEM((seq // WBLK, V_AUG, WBLK), BF16),
                        pltpu.VMEM((N_CMP_PAD, HEAD_DIM), F32),
                        pltpu.VMEM((HEAD_DIM, N_CMP_PAD), BF16),
                        pltpu.VMEM((seq // KT, KT // SLC_LEN, QBLK), F32),
                        pltpu.VMEM((2 * HEAD_DIM, N_HEADS * QBLK), BF16)],
        compiler_params=_cparams(("arbitrary", "arbitrary")),
        name="mixer_d",
    )(proj, proj, proj, proj, proj, proj, *qtabs, *ktabs, _rope_perm(LANES, HEAD_DIM), *ctabs, w1bd, pe2, w2bd,
      overlap)


def _cmp_params(pe, w1, w2):
    w1 = w1.reshape(2, CMP_LEN, HEAD_DIM, HEAD_DIM)
    z = jnp.zeros((CMP_LEN, HEAD_DIM, HEAD_DIM), F32)
    w1bd = jnp.concatenate([jnp.concatenate([w1[0], z], axis=2), jnp.concatenate([z, w1[1]], axis=2)], axis=1)
    z2 = jnp.zeros((HEAD_DIM, HEAD_DIM), F32)
    w2bd = jnp.concatenate([jnp.concatenate([w2[0], z2], axis=1), jnp.concatenate([z2, w2[1]], axis=1)], axis=0)
    pe2 = jnp.concatenate([pe[0], pe[1]], axis=1)
    half = CMP_LEN // 2
    w1pair = jnp.stack([jnp.concatenate([jnp.concatenate([w1bd[r], w1bd[half + r]], axis=1),
                                         jnp.concatenate([w1bd[r + 1], w1bd[half + r + 1]], axis=1)], axis=0)
                        for r in range(0, half, 2)])
    pe_rows = jnp.stack([jnp.concatenate([jnp.concatenate([pe2[r], pe2[r + 1]])[None, :],
                                          jnp.concatenate([pe2[half + r], pe2[half + r + 1]])[None, :],
                                          jnp.zeros((PE_ROWS - 2, 2 * LANES), F32)], axis=0)
                         for r in range(0, half, 2)])
    return w1pair, pe_rows, w2bd


def _overlap_matrix(seq):
    n_cmp = (seq - CMP_LEN) // CMP_STRIDE + 1
    starts = np.arange(N_CMP_PAD) * CMP_STRIDE
    blk = np.arange(seq // SLC_LEN)
    ov = ((starts[None, :] < (blk[:, None] + 1) * SLC_LEN) & (starts[None, :] + CMP_LEN > blk[:, None] * SLC_LEN))
    ov = ov & (np.arange(N_CMP_PAD)[None, :] < n_cmp)
    return jnp.asarray(ov.astype(np.float32))


def kernel(x, ln0_g, ln0_b, w_in, b_in, a_sinks, b_conv_w, b_conv_b, c_lb, c_norm_g, d_cmp_pe, d_cmp_w1, d_cmp_w2,
           w_out, ln_g, ln_b):
    bsz, seq, d = x.shape
    pos = jnp.arange(seq)
    qtabs = _rope_tables(pos, GROUP_W, 128)
    ktabs = _rope_tables(pos, 128, HEAD_DIM)
    ctabs = _rope_tables(jnp.arange(N_CMP_PAD) * CMP_STRIDE + (CMP_LEN - 1), 128, HEAD_DIM)
    overlap = _overlap_matrix(seq)

    w_p, b_p = _prep_in_weights(w_in, b_in, 0)
    h, proj = _ln_in_proj(x.reshape(bsz * seq, d), ln0_g, ln0_b, w_p, b_p)
    for l in range(DEPTH):
        proj = proj.reshape(bsz, seq, N_PROJ)
        y_a = _mixer_a(proj, a_sinks[l], qtabs)
        y_b, y_c = _mixer_bc(proj, b_conv_w[l], b_conv_b[l], c_lb, c_norm_g[l], l)
        w1bd, pe2, w2bd = _cmp_params(d_cmp_pe[l], d_cmp_w1[l], d_cmp_w2[l])
        y_d = _mixer_d(proj, _rope_tables_t(seq, D_QBLK), ktabs, ctabs, w1bd, pe2, w2bd, overlap)
        ys = (y_a, y_b, y_c, y_d)
        if l + 1 < DEPTH:
            w_p, b_p = _prep_in_weights(w_in, b_in, l + 1)
            h, proj = _out_in_proj(ys, h, w_out[l].astype(BF16), ln_g[l], ln_b[l], w_p, b_p)
        else:
            h = _out_proj(ys, h, w_out[l].astype(BF16), ln_g[l], ln_b[l])
    return h.reshape(bsz, seq, d)
```

```python
import functools

import numpy as np
import jax
import jax.numpy as jnp
from jax import lax
from jax.experimental import pallas as pl
from jax.experimental.pallas import tpu as pltpu

F32 = jnp.float32
BF16 = jnp.bfloat16
Y_DTYPE = BF16

DEPTH = 2
HEAD_DIM = 64
HALF = HEAD_DIM // 2
N_HEADS = 4
GROUP_W = N_HEADS * HEAD_DIM
N_GROUPS = 4
ROPE_THETA = 10000.0
NEG = -1e30
LN_EPS = 1e-5
A_WINDOW = 128
A_GROUP = 2
A_SUB = 16
V_AUG = HEAD_DIM + 16
LOG2E = 1.4426950408889634
PRO_ROWS = 2048
A_PRO_ROWS = 2048
B_CONV = 4
C_CHUNK = 64
C_SUB = 8
BC_STEP = 1024
C_EPS = 1e-6
CMP_LEN = 32
CMP_STRIDE = 16
SLC_LEN = 64
N_SEL = 8
NSA_WINDOW = 512
FORCE_SCORE = 1e6
DN_ALPHA = (2.0 * DEPTH) ** 0.25
QBLK = 128
SCALE = HEAD_DIM ** -0.5

VMEM_LIMIT = 56 * 1024 * 1024
PROJ_TM = 512
PROJ_SUB = 256

LANES = 128

ORIG_SPLITS = (
    ('a_q', 256), ('a_k', 128), ('a_v', 128), ('a_z', 256),
    ('b_q', 256), ('b_k', 256), ('b_v', 256), ('b_if', 8), ('b_o', 256), ('b_z', 256),
    ('c_q', 256), ('c_f', 256), ('c_i', 256), ('c_z', 256),
    ('d_q', 256), ('d_kvc', 128), ('d_ksv', 128), ('d_kwv', 128), ('d_g', 12), ('d_z', 256),
)
N_COLS = sum(w for _, w in ORIG_SPLITS)
GATE_SLOTS = ('b_if', 'd_g')
NEW_LAYOUT = (
    ('a_q', 256), ('a_k', 128), ('a_v', 128), ('a_z', 256), ('b_q', 256), ('b_k', 256), ('b_v', 256),
    ('b_o', 256), ('b_z', 256), ('c_q', 256), ('c_f', 256), ('c_i', 256), ('c_z', 256),
    ('d_q', 256), ('d_kvc', 128), ('d_ksv', 128), ('d_kwv', 128), ('gates', 128), ('d_z', 256),
)
N_PROJ = sum(w for _, w in NEW_LAYOUT)
D_GATE_LANE = dict(ORIG_SPLITS)['b_if']


def _offset(layout, name):
    off = 0
    for n, w in layout:
        if n == name:
            return off
        off += w
    raise KeyError(name)


def _new_offset(name):
    return _offset(NEW_LAYOUT, name)


def _permuted_cols(load):
    sizes = dict(ORIG_SPLITS)
    for name, width in NEW_LAYOUT:
        new = _new_offset(name)
        if name != 'gates':
            old = _offset(ORIG_SPLITS, name)
            lo = old // LANES * LANES
            hi = min(-(-(old + width) // LANES) * LANES, N_COLS)
            yield new, load(lo, hi)[..., old - lo:old - lo + width]
            continue
        gates, lane0 = None, 0
        for slot in GATE_SLOTS:
            old = _offset(ORIG_SPLITS, slot)
            lo = old // LANES * LANES
            tile = load(lo, lo + LANES)
            if old - lo != lane0:
                tile = pltpu.roll(tile, (lane0 - (old - lo)) % LANES, axis=tile.ndim - 1)
            lane = _iota(tile.shape, tile.ndim - 1)
            part = jnp.where((lane >= lane0) & (lane < lane0 + sizes[slot]), tile, 0.0)
            gates = part if gates is None else gates + part
            lane0 += sizes[slot]
        yield new, gates


def _prep_body(w_ref, b_ref, wo_ref, bo_ref):
    for new, cols in _permuted_cols(lambda lo, hi: w_ref[0, :, lo:hi]):
        wo_ref[:, new:new + cols.shape[-1]] = cols.astype(BF16)
    for new, cols in _permuted_cols(lambda lo, hi: b_ref[0, :, lo:hi]):
        bo_ref[:, new:new + cols.shape[-1]] = cols


def _prep_in_weights(w_in, b_in, layer, rows=128):
    depth, d, n = w_in.shape
    return pl.pallas_call(
        _prep_body,
        grid=(d // rows,),
        in_specs=[pl.BlockSpec((1, rows, n), lambda r: (layer, r, 0)),
                  pl.BlockSpec((1, 1, n), lambda r: (layer, 0, 0))],
        out_specs=[pl.BlockSpec((rows, N_PROJ), lambda r: (r, 0)),
                   pl.BlockSpec((1, N_PROJ), lambda r: (0, 0))],
        out_shape=[jax.ShapeDtypeStruct((d, N_PROJ), BF16), jax.ShapeDtypeStruct((1, N_PROJ), F32)],
        compiler_params=_cparams(("arbitrary",)),
        name="prep_in_weights",
    )(w_in, b_in.reshape(depth, 1, n))


def _mm(a, b):
    return jnp.dot(a.astype(BF16), b.astype(BF16), preferred_element_type=F32)


def _mm_nt(a, b):
    return lax.dot_general(a.astype(BF16), b.astype(BF16), (((1,), (1,)), ((), ())),
                           preferred_element_type=F32)


def _mm_tn(a, b):
    return lax.dot_general(a.astype(BF16), b.astype(BF16), (((0,), (0,)), ((), ())),
                           preferred_element_type=F32)


def _dot_bf16(a, b):
    return jnp.dot(a, b, preferred_element_type=F32)


def _split_bf16(x, terms):
    out = []
    for _ in range(terms - 1):
        t = x.astype(BF16)
        out.append(t)
        x = x - t.astype(F32)
    out.append(x.astype(BF16))
    return out


def _mm_hi(a, b):
    ah, al = _split_bf16(a, 2)
    bh, bl = _split_bf16(b, 2)
    return _dot_bf16(ah, bh) + (_dot_bf16(ah, bl) + _dot_bf16(al, bh))


def _mm_sel(sel, x):
    sel = sel.astype(BF16)
    x1, x2, x3 = _split_bf16(x, 3)
    return _dot_bf16(sel, x1) + (_dot_bf16(sel, x2) + _dot_bf16(sel, x3))


def _sigmoid(x):
    return 1.0 / (1.0 + jnp.exp2(x * (-LOG2E)))


def _silu(x):
    return x * _sigmoid(x)


def _log_sigmoid(x):
    return jnp.minimum(x, 0.0) - jnp.log1p(jnp.exp(-jnp.abs(x)))


def _rope(x, cos, sin_up, sin_dn):
    w = x.shape[-1]
    up = pltpu.roll(x, w - HALF, axis=1)
    dn = pltpu.roll(x, HALF, axis=1)
    return x * cos + up * sin_up + dn * sin_dn


def _rope_bf16(x, cos, sin_up, sin_dn, perm):
    swapped = jnp.dot(x.astype(BF16), perm, preferred_element_type=F32)
    return x * cos + swapped * (sin_up + sin_dn)


def _rope_perm(width, rot_lanes):
    lane = np.arange(width)
    src = np.where(lane % HEAD_DIM < HALF, lane + HALF, lane - HALF)
    rotated = (lane % LANES) < rot_lanes
    perm = (np.arange(width)[:, None] == src[None, :]) & rotated[None, :]
    return jnp.asarray(perm.astype(np.float32), dtype=BF16)


def _rope_tables_t(seq, blk):
    inv = ROPE_THETA ** (-jnp.arange(HALF, dtype=F32) / HALF)
    ang = jnp.arange(seq).astype(F32)[:, None] * inv[None, :]
    to_blocks = lambda a: a.T.reshape(HALF, seq // blk, blk).transpose(1, 0, 2)
    return to_blocks(jnp.cos(ang)), to_blocks(jnp.sin(ang))


def _rope_tables(pos, width, rot_lanes):
    inv = ROPE_THETA ** (-jnp.arange(HALF, dtype=F32) / HALF)
    ang = pos.astype(F32)[:, None] * inv[None, :]
    cos, sin = jnp.cos(ang), jnp.sin(ang)
    zero, one = jnp.zeros_like(sin), jnp.ones_like(cos)
    cos_h = jnp.concatenate([cos, cos], axis=1)
    up_h = jnp.concatenate([-sin, zero], axis=1)
    dn_h = jnp.concatenate([zero, sin], axis=1)
    id_c = jnp.concatenate([one, one], axis=1)
    id_s = jnp.concatenate([zero, zero], axis=1)
    cs, us, ds = [], [], []
    for l0 in range(0, width, HEAD_DIM):
        rot = (l0 % 128) < rot_lanes
        cs.append(cos_h if rot else id_c)
        us.append(up_h if rot else id_s)
        ds.append(dn_h if rot else id_s)
    return jnp.concatenate(cs, axis=1), jnp.concatenate(us, axis=1), jnp.concatenate(ds, axis=1)


def _iota(shape, dim):
    return lax.broadcasted_iota(jnp.int32, shape, dim)


def _cparams(sem):
    return pltpu.CompilerParams(dimension_semantics=sem, vmem_limit_bytes=VMEM_LIMIT)


def _ln_rows(x, g, b):
    mu = jnp.mean(x, axis=-1, keepdims=True)
    xc = x - mu
    var = jnp.mean(xc * xc, axis=-1, keepdims=True)
    return xc * lax.rsqrt(var + LN_EPS) * g + b


def _mix_out(rows, ya_ref, yb_ref, yc_ref, yd_ref, h_ref, w_ref, g_ref, b_ref):
    y = _mm(ya_ref[rows, :], w_ref[0:GROUP_W, :])
    y += _mm(yb_ref[rows, :], w_ref[GROUP_W:2 * GROUP_W, :])
    y += _mm(yc_ref[rows, :], w_ref[2 * GROUP_W:3 * GROUP_W, :])
    y += _mm(yd_ref[rows, :], w_ref[3 * GROUP_W:4 * GROUP_W, :])
    return _ln_rows(DN_ALPHA * h_ref[rows, :] + y, g_ref[...], b_ref[...])


def _sub_tiles(ref):
    return [slice(r, r + PROJ_SUB) for r in range(0, ref.shape[0], PROJ_SUB)]


def _ln_proj_body(x_ref, g_ref, b_ref, w_ref, bias_ref, h_ref, o_ref):
    for rows in _sub_tiles(x_ref):
        h = _ln_rows(x_ref[rows, :], g_ref[...], b_ref[...])
        h_ref[rows, :] = h
        o_ref[rows, :] = jnp.dot(h.astype(BF16), w_ref[...], preferred_element_type=F32) + bias_ref[...]


def _out_proj_body(ya_ref, yb_ref, yc_ref, yd_ref, hp_ref, wo_ref, g_ref, b_ref, w_ref, bias_ref, h_ref, o_ref):
    for rows in _sub_tiles(hp_ref):
        h = _mix_out(rows, ya_ref, yb_ref, yc_ref, yd_ref, hp_ref, wo_ref, g_ref, b_ref)
        h_ref[rows, :] = h
        o_ref[rows, :] = jnp.dot(h.astype(BF16), w_ref[...], preferred_element_type=F32) + bias_ref[...]


def _ln_in_proj(x2, g, b, w_bf, bias, tm=PROJ_TM):
    m, d = x2.shape
    n = w_bf.shape[1]
    row = lambda width: pl.BlockSpec((tm, width), lambda i: (i, 0))
    const = lambda shape: pl.BlockSpec(shape, lambda i: (0, 0))
    return pl.pallas_call(
        _ln_proj_body,
        grid=(m // tm,),
        in_specs=[row(d), const((1, d)), const((1, d)), const((d, n)), const((1, n))],
        out_specs=[row(d), row(n)],
        out_shape=[jax.ShapeDtypeStruct((m, d), F32), jax.ShapeDtypeStruct((m, n), F32)],
        compiler_params=_cparams(("parallel",)),
        name="ln_in_proj",
    )(x2, g.reshape(1, d), b.reshape(1, d), w_bf, bias.reshape(1, n))


def _out_in_proj(ys, h2, wo_bf, g, b, w_bf, bias, tm=PROJ_TM):
    m, d = h2.shape
    n = w_bf.shape[1]
    row = lambda width: pl.BlockSpec((tm, width), lambda i: (i, 0))
    const = lambda shape: pl.BlockSpec(shape, lambda i: (0, 0))
    return pl.pallas_call(
        _out_proj_body,
        grid=(m // tm,),
        in_specs=[row(GROUP_W)] * N_GROUPS + [row(d), const((N_GROUPS * GROUP_W, d)), const((1, d)), const((1, d)),
                                              const((d, n)), const((1, n))],
        out_specs=[row(d), row(n)],
        out_shape=[jax.ShapeDtypeStruct((m, d), F32), jax.ShapeDtypeStruct((m, n), F32)],
        compiler_params=_cparams(("parallel",)),
        name="out_in_proj",
    )(*[y.reshape(m, GROUP_W) for y in ys], h2, wo_bf, g.reshape(1, d), b.reshape(1, d), w_bf, bias.reshape(1, n))


def _out_body(ya_ref, yb_ref, yc_ref, yd_ref, h_ref, w_ref, g_ref, b_ref, o_ref):
    o_ref[...] = _mix_out(slice(None), ya_ref, yb_ref, yc_ref, yd_ref, h_ref, w_ref, g_ref, b_ref)


def _out_proj(ys, h2, w_bf, g, b, tm=512):
    m, d = h2.shape
    yspec = pl.BlockSpec((tm, GROUP_W), lambda i: (i, 0))
    return pl.pallas_call(
        _out_body,
        grid=(m // tm,),
        in_specs=[yspec, yspec, yspec, yspec,
                  pl.BlockSpec((tm, d), lambda i: (i, 0)),
                  pl.BlockSpec((4 * GROUP_W, d), lambda i: (0, 0)),
                  pl.BlockSpec((1, d), lambda i: (0, 0)),
                  pl.BlockSpec((1, d), lambda i: (0, 0))],
        out_specs=pl.BlockSpec((tm, d), lambda i: (i, 0)),
        out_shape=jax.ShapeDtypeStruct((m, d), F32),
        compiler_params=_cparams(("parallel",)),
        name="out_proj",
    )(*[y.reshape(m, GROUP_W) for y in ys], h2, w_bf, g.reshape(1, d), b.reshape(1, d))


def _mixer_a_body(sink_ref, q_ref, kv_ref, z_ref, cos_ref, up_ref, dn_ref, perm_ref, o_ref, k_ref, vt_ref):
    i = pl.program_id(1)
    seq = kv_ref.shape[1]
    n_kv = N_HEADS // A_GROUP
    low_half = _iota((1, LANES), 1) < HEAD_DIM

    @pl.when(i == 0)
    def _():
        def rope_rows(c, carry):
            for u in range(A_PRO_ROWS // QBLK):
                t0 = c * (A_PRO_ROWS // QBLK) + u
                rows = pl.ds(pl.multiple_of(t0 * QBLK, QBLK), QBLK)
                kr = _rope_bf16(kv_ref[0, rows, 0:LANES], cos_ref[rows, 0:LANES], up_ref[rows, 0:LANES],
                                dn_ref[rows, 0:LANES], perm_ref[0:LANES, 0:LANES])
                v_t = _mm_nt(_iota((LANES, LANES), 0) == _iota((LANES, LANES), 1), kv_ref[0, rows, LANES:2 * LANES])
                v_t = v_t.astype(BF16)
                ones_rows = jnp.ones((V_AUG - HEAD_DIM, QBLK), BF16)
                other = pltpu.roll(kr, HEAD_DIM, axis=1)
                for g in range(n_kv):
                    dup = jnp.where(low_half, kr, other) if g == 0 else jnp.where(low_half, other, kr)
                    k_ref[g, rows, :] = dup.astype(BF16)
                    vt_ref[g, t0] = jnp.concatenate([v_t[g * HEAD_DIM:(g + 1) * HEAD_DIM, :], ones_rows], axis=0)
            return carry
        lax.fori_loop(0, seq // A_PRO_ROWS, rope_rows, 0)

    n_blk = A_WINDOW // QBLK + 1
    span = n_blk * QBLK
    first_head = _iota((1, A_GROUP * QBLK), 1) < QBLK
    eye = (_iota((QBLK, QBLK), 0) == _iota((QBLK, QBLK), 1)).astype(BF16)

    def query_block(u):
        blk_i = i * A_SUB + u
        sub = slice(u * QBLK, (u + 1) * QBLK)
        q0 = pl.multiple_of(blk_i * QBLK, QBLK)
        qrows = pl.ds(q0, QBLK)
        qr = _rope(q_ref[0, sub, :], cos_ref[qrows, :], up_ref[qrows, :], dn_ref[qrows, :]) * (SCALE * LOG2E)
        kt0 = jnp.maximum(blk_i - A_WINDOW // QBLK, 0)
        k0 = pl.multiple_of(kt0 * QBLK, QBLK)
        rel = (q0 + _iota((1, QBLK), 1)) - (k0 + _iota((span, 1), 0))
        in_band = lax.bitcast_convert_type(rel, jnp.uint32) < A_WINDOW
        bias = jnp.concatenate([jnp.where(in_band, 0.0, NEG)] * A_GROUP, axis=1)
        outs = []
        for g in range(n_kv):
            q_pair = qr[:, g * LANES:(g + 1) * LANES]
            qg = jnp.concatenate([jnp.where(low_half, q_pair, 0.0), jnp.where(low_half, 0.0, q_pair)], axis=0)
            sink = jnp.where(first_head, sink_ref[g * A_GROUP], sink_ref[g * A_GROUP + 1]) * LOG2E
            s = _mm_nt(k_ref[g, pl.ds(k0, span), :], qg) + bias
            m = jnp.maximum(jnp.max(s, axis=0, keepdims=True), sink)
            p = jnp.exp2(s - m).astype(BF16)
            o = jnp.zeros((V_AUG, A_GROUP * QBLK), F32)
            for c in range(n_blk):
                o = o + jnp.dot(vt_ref[g, kt0 + c], p[c * QBLK:(c + 1) * QBLK], preferred_element_type=F32)
            o = o[0:HEAD_DIM] / (o[HEAD_DIM:HEAD_DIM + 1] + jnp.exp2(sink - m))
            outs.extend(o[:, r * QBLK:(r + 1) * QBLK] for r in range(A_GROUP))
        y = _mm_nt(eye, jnp.concatenate(outs, axis=0))
        o_ref[0, sub, :] = (y * _silu(z_ref[0, sub, :])).astype(Y_DTYPE)

    for u in range(A_SUB):
        query_block(u)


def _mixer_a(proj, sinks, tabs):
    assert A_GROUP == 2 and A_GROUP * HEAD_DIM == LANES, "a kv head's two query heads share one lane tile"
    bsz, seq, _ = proj.shape
    cos, up, dn = tabs
    blk = lambda name: _new_offset(name) // GROUP_W
    tspec = pl.BlockSpec((seq, GROUP_W), lambda b, i: (0, 0))
    return pl.pallas_call(
        _mixer_a_body,
        grid=(bsz, seq // (A_SUB * QBLK)),
        in_specs=[pl.BlockSpec(memory_space=pltpu.SMEM),
                  pl.BlockSpec((1, A_SUB * QBLK, GROUP_W), lambda b, i: (b, i, blk('a_q'))),
                  pl.BlockSpec((1, seq, GROUP_W), lambda b, i: (b, 0, blk('a_k'))),
                  pl.BlockSpec((1, A_SUB * QBLK, GROUP_W), lambda b, i: (b, i, blk('a_z'))),
                  tspec, tspec, tspec, pl.BlockSpec((GROUP_W, GROUP_W), lambda b, i: (0, 0))],
        out_specs=pl.BlockSpec((1, A_SUB * QBLK, GROUP_W), lambda b, i: (b, i, 0)),
        out_shape=jax.ShapeDtypeStruct((bsz, seq, GROUP_W), Y_DTYPE),
        scratch_shapes=[pltpu.VMEM((N_HEADS // A_GROUP, seq, LANES), BF16),
                        pltpu.VMEM((N_HEADS // A_GROUP, seq // QBLK, V_AUG, QBLK), BF16)],
        compiler_params=_cparams(("arbitrary", "arbitrary")),
        name="mixer_a",
    )(sinks, proj, proj, proj, cos, up, dn, _rope_perm(GROUP_W, LANES))


B_CHUNK = 128
B_AUG = V_AUG


def _conv_silu(x_ref, r0, prev, w, b):
    L = B_CHUNK
    x = x_ref[0, r0:r0 + L, :]
    acc = b + w[B_CONV - 1:B_CONV, :] * x
    for s in range(1, B_CONV):
        if r0 >= s:
            shifted = x_ref[0, r0 - s:r0 - s + L, :]
        else:
            row8 = _iota((8, x.shape[1]), 0)
            xs = pltpu.roll(x, s, axis=0)
            ps = pltpu.roll(prev, s, axis=0)
            shifted = jnp.concatenate([jnp.where(row8 < s, ps, xs[0:8]), xs[8:]], axis=0)
        acc = acc + w[B_CONV - 1 - s:B_CONV - s, :] * shifted
    return _silu(acc)


def _mixer_b_init(xprev_ref, c_ref, m_ref):
    xprev_ref[...] = jnp.zeros_like(xprev_ref)
    c_ref[...] = jnp.zeros_like(c_ref)
    m_ref[...] = jnp.zeros_like(m_ref)


def _mixer_b_main(xq_ref, xk_ref, v_ref, og_ref, z_ref, if_ref, cw_ref, cb_ref, o_ref, xprev_ref, c_ref, m_ref):
    L = B_CHUNK
    tri = (_iota((L, L), 1) <= _iota((L, L), 0)).astype(F32)
    key_first = _iota((L, L), 0) <= _iota((L, L), 1)
    half_of_lane = _iota((1, 128), 1) // HEAD_DIM
    ones_rows = jnp.ones((B_AUG - HEAD_DIM, L), F32)

    def chunk(r0, state):
        rows = slice(r0, r0 + L)
        qc = _conv_silu(xq_ref, r0, xprev_ref[:, 0:GROUP_W], cw_ref[:, 0:GROUP_W], cb_ref[:, 0:GROUP_W]) * SCALE
        kc = _conv_silu(xk_ref, r0, xprev_ref[:, GROUP_W:], cw_ref[:, GROUP_W:], cb_ref[:, GROUP_W:])
        q_t = qc.T
        v_t = v_ref[0, rows, :].T

        gates = if_ref[0, rows, :]
        bcum = _mm_sel(tri, _log_sigmoid(gates))
        gates_t = gates.T
        bcum_t = bcum.T
        outs, new_state = [], []
        for h in range(N_HEADS):
            pair = slice((h // 2) * 128, (h // 2 + 1) * 128)
            own = half_of_lane == (h % 2)
            k_pair = kc[:, pair]
            q_pair_t = q_t[pair, :]
            i_row = gates_t[h:h + 1, :]
            b_row = bcum_t[N_HEADS + h:N_HEADS + h + 1, :]
            c_col = gates[:, h:h + 1] - bcum[:, N_HEADS + h:N_HEADS + h + 1]
            cst, m_prev = state[h]
            dmat = jnp.where(key_first, b_row + c_col, NEG)
            inter = b_row + m_prev
            m_t = jnp.maximum(jnp.max(dmat, axis=0, keepdims=True), inter)
            smat = _mm(jnp.where(own, k_pair, 0.0), q_pair_t) * jnp.exp(dmat - m_t)
            vaug_t = jnp.concatenate([v_t[h * HEAD_DIM:(h + 1) * HEAD_DIM, :], ones_rows], axis=0)
            tot = _mm(vaug_t, smat) + jnp.exp(inter - m_t) * _mm(cst, q_pair_t)
            num = tot[0:HEAD_DIM, :]
            den = tot[HEAD_DIM:HEAD_DIM + 1, :]
            outs.append(num / jnp.maximum(jnp.abs(den), jnp.exp(-m_t)))
            b_last = b_row[:, L - 1:L]
            d_row = b_last - b_row + i_row
            m_new = jnp.maximum(b_last + m_prev, jnp.max(d_row, axis=1, keepdims=True))
            w_row = jnp.exp(d_row - m_new)
            decay = jnp.exp(b_last + m_prev - m_new)
            new_state.append((decay * cst + jnp.where(own, _mm(vaug_t * w_row, k_pair), 0.0), m_new))
        o_ref[0, rows, :] = (_sigmoid(og_ref[0, rows, :]) * jnp.concatenate(outs, axis=0).T
                             * _silu(z_ref[0, rows, :])).astype(Y_DTYPE)
        return new_state

    rows_step = xq_ref.shape[1]
    state = [(c_ref[h], m_ref[h:h + 1, 0:1]) for h in range(N_HEADS)]
    for u in range(rows_step // L):
        state = chunk(u * L, state)
        yield
    xprev_ref[:, 0:GROUP_W] = xq_ref[0, rows_step - 8:rows_step, :]
    xprev_ref[:, GROUP_W:] = xk_ref[0, rows_step - 8:rows_step, :]
    for h in range(N_HEADS):
        c_ref[h] = state[h][0]
        m_ref[h:h + 1, :] = jnp.broadcast_to(state[h][1], (1, 128))


def _mixer_c_main(layer, q_ref, f_ref, i_ref, z_ref, lb_ref, gn_ref, tri_ref, piv_ref, o_ref, st_ref):
    L = C_CHUNK
    nsub = L // C_SUB

    raw = lb_ref[...]
    ex = jnp.exp(raw - jnp.max(raw, axis=0, keepdims=True))
    if layer == 0:
        lb = jnp.zeros((1, GROUP_W), F32)
    else:
        lb = jnp.sum(ex[1:layer + 1], axis=0, keepdims=True) / jnp.sum(ex, axis=0, keepdims=True)
    tri = (_iota((L, L), 1) <= _iota((L, L), 0)).astype(F32)
    head_of_lane = _iota((1, GROUP_W), 1) // HEAD_DIM
    same_head = ((_iota((GROUP_W, GROUP_W), 0) // HEAD_DIM)
                 == (_iota((GROUP_W, GROUP_W), 1) // HEAD_DIM))
    ones_bd = same_head.astype(BF16)

    def chunk(rows, st):
        f = lb + (1.0 - lb) * _sigmoid(f_ref[0, rows, :])
        logf = jnp.log(f)
        k = 1.0 - f
        q = _silu(q_ref[0, rows, :])
        v = i_ref[0, rows, :]
        a = _mm_sel(tri, logf) * LOG2E
        o = _mm_nt(q * jnp.exp2(a), st)

        qsts, kts, vreps = [], [], []
        for i in range(1, nsub):
            r0 = i * C_SUB
            piv = a[r0 - 1:r0, :]
            qt = q[r0:r0 + C_SUB] * jnp.exp2(a[r0:r0 + C_SUB] - piv)
            qsts.extend(jnp.where(head_of_lane == h, qt, 0.0) for h in range(N_HEADS))
            kts.append((k[0:r0] * jnp.exp2(piv - a[0:r0])).astype(BF16))
            vreps.append(v[0:r0].astype(BF16))
        scores = _mm_nt(jnp.concatenate(qsts, axis=0), jnp.concatenate(kts, axis=0))
        r = _dot_bf16((scores * piv_ref[...]).astype(BF16), jnp.concatenate(vreps, axis=0))
        pieces = [o[0:C_SUB]]
        for i in range(1, nsub):
            acc = o[i * C_SUB:(i + 1) * C_SUB]
            for h in range(N_HEADS):
                g0 = ((i - 1) * N_HEADS + h) * C_SUB
                acc = acc + jnp.where(head_of_lane == h, r[g0:g0 + C_SUB], 0.0)
            pieces.append(acc)
        o = jnp.concatenate(pieces, axis=0)

        a3 = a.reshape(nsub, C_SUB, GROUP_W)
        q3 = q.reshape(nsub, C_SUB, GROUP_W)
        k3 = k.reshape(nsub, C_SUB, GROUP_W)
        v3 = v.reshape(nsub, C_SUB, GROUP_W)
        ps = []
        for s in range(C_SUB):
            e = jnp.exp2(a3 - a3[:, s:s + 1, :] + tri_ref[s])
            ps.append((q3 * k3[:, s:s + 1, :] * e).reshape(L, GROUP_W).astype(BF16))
        w = jnp.dot(jnp.concatenate(ps, axis=0), ones_bd, preferred_element_type=F32)
        for s in range(C_SUB):
            vs = jnp.broadcast_to(v3[:, s:s + 1, :], (nsub, C_SUB, GROUP_W)).reshape(L, GROUP_W)
            o = o + w[s * L:(s + 1) * L] * vs

        a_last = a[L - 1:L, :]
        kdec = k * jnp.exp2(a_last - a)
        st_new = st * jnp.exp2(a_last) + jnp.where(same_head, _mm_tn(v, kdec), 0.0)

        ms = jnp.dot((o * o).astype(BF16), ones_bd, preferred_element_type=F32) * (1.0 / HEAD_DIM)
        o = o * lax.rsqrt(ms + C_EPS) * gn_ref[...]
        o_ref[0, rows, :] = (o * _silu(z_ref[0, rows, :])).astype(Y_DTYPE)
        return st_new

    st = st_ref[...]
    for u in range(q_ref.shape[1] // L):
        st = chunk(slice(u * L, (u + 1) * L), st)
        yield
    st_ref[...] = st


N_B_IN, N_C_IN, N_B_SCRATCH = 8, 8, 3
C_PER_B = B_CHUNK // C_CHUNK


def _mixer_bc_body(layer, *refs):
    b_in = refs[:N_B_IN]
    c_in = refs[N_B_IN:N_B_IN + N_C_IN]
    ob_ref, oc_ref = refs[N_B_IN + N_C_IN:N_B_IN + N_C_IN + 2]
    scratch = refs[N_B_IN + N_C_IN + 2:]
    b_scratch, c_scratch = scratch[:N_B_SCRATCH], scratch[N_B_SCRATCH:]

    @pl.when(pl.program_id(1) == 0)
    def _():
        _mixer_b_init(*b_scratch)
        c_scratch[0][...] = jnp.zeros_like(c_scratch[0])

    pending = [_mixer_b_main(*b_in, ob_ref, *b_scratch), _mixer_c_main(layer, *c_in, oc_ref, *c_scratch)]
    weights = [1, C_PER_B]
    while pending:
        for gen, reps in list(zip(pending, weights)):
            for _ in range(reps):
                if next(gen, StopIteration) is StopIteration:
                    idx = pending.index(gen)
                    del pending[idx], weights[idx]
                    break


def _mixer_bc(proj, conv_w, conv_b, c_lb, norm_g, layer):
    bsz, seq, _ = proj.shape
    L = BC_STEP
    blk = lambda name: _new_offset(name) // GROUP_W
    spec = lambda name: pl.BlockSpec((1, L, GROUP_W), lambda b, c: (b, c, blk(name)))
    vec = pl.BlockSpec((1, GROUP_W), lambda b, c: (0, 0))
    t_ge_s = np.arange(C_SUB)[None, :, None] >= np.arange(C_SUB)[:, None, None]
    causal_bias = np.broadcast_to(np.where(t_ge_s, 0.0, NEG), (C_SUB, C_SUB, GROUP_W)).astype(np.float32)
    pivots = np.arange(1, C_CHUNK // C_SUB)
    row_piv = np.repeat(pivots, N_HEADS * C_SUB)
    col_piv = np.repeat(pivots, pivots * C_SUB)
    same_pivot = (row_piv[:, None] == col_piv[None, :]).astype(np.float32)
    out_spec = pl.BlockSpec((1, L, GROUP_W), lambda b, c: (b, c, 0))
    out_shape = jax.ShapeDtypeStruct((bsz, seq, GROUP_W), Y_DTYPE)
    return pl.pallas_call(
        functools.partial(_mixer_bc_body, layer),
        grid=(bsz, seq // L),
        in_specs=[spec('b_q'), spec('b_k'), spec('b_v'), spec('b_o'), spec('b_z'),
                  pl.BlockSpec((1, L, LANES), lambda b, c: (b, c, _new_offset('gates') // LANES)),
                  pl.BlockSpec((B_CONV, 2 * GROUP_W), lambda b, c: (0, 0)),
                  pl.BlockSpec((1, 2 * GROUP_W), lambda b, c: (0, 0)),
                  spec('c_q'), spec('c_f'), spec('c_i'), spec('c_z'),
                  pl.BlockSpec((DEPTH, GROUP_W), lambda b, c: (0, 0)), vec,
                  pl.BlockSpec((C_SUB, C_SUB, GROUP_W), lambda b, c: (0, 0, 0)),
                  pl.BlockSpec(same_pivot.shape, lambda b, c: (0, 0))],
        out_specs=[out_spec, out_spec],
        out_shape=[out_shape, out_shape],
        scratch_shapes=[pltpu.VMEM((8, 2 * GROUP_W), F32),
                        pltpu.VMEM((N_HEADS, B_AUG, 128), F32),
                        pltpu.VMEM((8, 128), F32),
                        pltpu.VMEM((GROUP_W, GROUP_W), F32)],
        compiler_params=_cparams(("arbitrary", "arbitrary")),
        name="mixer_bc",
    )(proj, proj, proj, proj, proj, proj, conv_w, conv_b.reshape(1, -1),
      proj, proj, proj, proj, c_lb, norm_g.reshape(1, -1), jnp.asarray(causal_bias), jnp.asarray(same_pivot))


N_CMP_PAD = 128
KT = 512
D_QBLK = 512
WBLK = 256
PE_ROWS = 16
SEL_ROWS = 16


def _mixer_d_body(q_ref, z_ref, ksv_ref, kwv_ref, cin_ref, g_ref, qcos_ref, qsin_ref, kcos_ref, kup_ref,
                  kdn_ref, kperm_ref, ccos_ref, cup_ref, cdn_ref, w1_ref, pe_ref, w2_ref, ov_ref, o_ref,
                  ks_ref, kw_ref, vts_ref, vtw_ref, kc_ref, vct_ref):
    QBLK = D_QBLK
    i = pl.program_id(1)
    seq = ksv_ref.shape[1]
    half_cmp = CMP_LEN // 2

    @pl.when(i == 0)
    def _():
        def rope_rows(c, carry):
            for u in range(PRO_ROWS // KT):
                t0 = c * (PRO_ROWS // KT) + u
                rows = pl.ds(pl.multiple_of(t0 * KT, KT), KT)
                kvs_in = ksv_ref[0, rows, :]
                kvw_in = kwv_ref[0, rows, :]
                kvs = _rope_bf16(kvs_in, kcos_ref[rows, :], kup_ref[rows, :], kdn_ref[rows, :], kperm_ref[...])
                kvw = _rope_bf16(kvw_in, kcos_ref[rows, :], kup_ref[rows, :], kdn_ref[rows, :], kperm_ref[...])
                blk_in_tile = _iota((KT, HEAD_DIM), 0) // SLC_LEN
                onehot = jnp.where(blk_in_tile == _iota((KT, HEAD_DIM), 1), 1.0, 0.0)
                ks_ref[rows, :] = jnp.concatenate([kvs[:, 0:HEAD_DIM], onehot], axis=1).astype(BF16)
                kw_ref[rows, :] = kvw[:, 0:HEAD_DIM].astype(BF16)
                ones_rows = jnp.ones((V_AUG - HEAD_DIM, KT), BF16)
                pick_v = _iota((HEAD_DIM, LANES), 1) == _iota((HEAD_DIM, LANES), 0) + HEAD_DIM
                vts_ref[t0] = jnp.concatenate([_mm_nt(pick_v, kvs_in).astype(BF16), ones_rows], axis=0)
                vwt = jnp.concatenate([_mm_nt(pick_v, kvw_in).astype(BF16), ones_rows], axis=0)
                for w in range(KT // WBLK):
                    vtw_ref[t0 * (KT // WBLK) + w] = vwt[:, w * WBLK:(w + 1) * WBLK]
            return carry
        lax.fori_loop(0, seq // PRO_ROWS, rope_rows, 0)
        acc = jnp.zeros((N_CMP_PAD + PE_ROWS, 2 * LANES), F32)
        for rr in range(half_cmp // 2):
            xa = cin_ref[0, pl.ds(2 * rr, N_CMP_PAD, stride=CMP_STRIDE), :]
            xb = cin_ref[0, pl.ds(2 * rr + 1, N_CMP_PAD, stride=CMP_STRIDE), :]
            lhs = jnp.concatenate([jnp.concatenate([xa, xb], axis=1), pe_ref[rr]], axis=0)
            acc = acc + _mm_hi(lhs, w1_ref[rr])
        u0 = acc[0:N_CMP_PAD, 0:LANES] + acc[N_CMP_PAD:N_CMP_PAD + 1, 0:LANES]
        u1 = acc[0:N_CMP_PAD, LANES:] + acc[N_CMP_PAD + 1:N_CMP_PAD + 2, LANES:]
        pre = u0 + pltpu.roll(u1, N_CMP_PAD - 1, axis=0)
        cv = _rope(_mm_hi(_silu(pre), w2_ref[...]), ccos_ref[...], cup_ref[...], cdn_ref[...])
        kc_ref[...] = cv[:, 0:HEAD_DIM]
        vct_ref[...] = cv.T[HEAD_DIM:2 * HEAD_DIM, :].astype(BF16)

    lax.switch(i, [functools.partial(_mixer_d_step, n, q_ref, z_ref, g_ref, qcos_ref, qsin_ref, ov_ref, o_ref,
                                     ks_ref, kw_ref, vts_ref, vtw_ref, kc_ref, vct_ref)
                   for n in range(seq // QBLK)])


def _mixer_d_step(n, q_ref, z_ref, g_ref, qcos_ref, qsin_ref, ov_ref, o_ref,
                  ks_ref, kw_ref, vts_ref, vtw_ref, kc_ref, vct_ref):
    QBLK = D_QBLK
    seq = ks_ref.shape[0]
    q0 = n * QBLK
    qt = (q_ref[0] * (SCALE * LOG2E)).T
    cos_t = qcos_ref[n]
    sin_t = qsin_ref[n]
    heads_t = []
    for h in range(N_HEADS):
        x1 = qt[h * HEAD_DIM:h * HEAD_DIM + HALF]
        x2 = qt[h * HEAD_DIM + HALF:(h + 1) * HEAD_DIM]
        heads_t.append(jnp.concatenate([x1 * cos_t - x2 * sin_t, x1 * sin_t + x2 * cos_t], axis=0))
    qst = jnp.concatenate(heads_t, axis=1)
    qst_bf = qst.astype(BF16)
    cols4 = N_HEADS * QBLK
    qpos = q0 + _iota((1, QBLK), 1)

    def lanes4(x):
        return jnp.concatenate([x] * N_HEADS, axis=1)

    n_cmp = min((q0 + QBLK) // CMP_STRIDE, N_CMP_PAD)
    sc = _mm_hi(kc_ref[0:n_cmp, :], qst)
    ends = _iota((n_cmp, 1), 0) * CMP_STRIDE + (CMP_LEN - 1)
    scm = sc + lanes4(jnp.where(ends <= qpos, 0.0, NEG))
    e = jnp.exp2(scm - jnp.max(scm, axis=0, keepdims=True))
    l = jnp.sum(e, axis=0, keepdims=True)
    any_valid = lanes4(jnp.where(qpos >= CMP_LEN - 1, 1.0, 0.0))
    pc = e * (any_valid / l)
    if n_cmp < N_CMP_PAD:
        pc = jnp.concatenate([pc, jnp.zeros((N_CMP_PAD - n_cmp, cols4), F32)], axis=0)
    o_cmp = jnp.dot(vct_ref[...], pc.astype(BF16), preferred_element_type=F32)
    psum = pc[:, 0:QBLK] + pc[:, QBLK:2 * QBLK] + pc[:, 2 * QBLK:3 * QBLK] + pc[:, 3 * QBLK:4 * QBLK]

    n_slc = (q0 + QBLK) // SLC_LEN
    imp = _mm_sel(ov_ref[0:n_slc, :], psum)
    jl = _iota((n_slc, 1), 0)
    cur = jnp.right_shift(qpos, SLC_LEN.bit_length() - 1)
    imp = jnp.where(jl == 0, FORCE_SCORE, imp)
    imp = jnp.where(jl == cur, FORCE_SCORE, imp)
    imp = jnp.where(jl == cur - 1, FORCE_SCORE, imp)
    imp = jnp.where(jl <= cur, imp, NEG)
    rank = jnp.zeros((n_slc, QBLK), F32)
    for jp in range(n_slc):
        cj = imp[jp:jp + 1, :]
        tie = jnp.where(jl > jp, 1.0, 0.0)
        rank = rank + jnp.where(cj > imp, 1.0, jnp.where(cj == imp, tie, 0.0))
    sel_bias = jnp.where(rank < N_SEL, 0.0, NEG)
    blocks_per_tile = KT // SLC_LEN

    win_parts = []
    for u in range(QBLK // WBLK):
        qcols = [slice(h * QBLK + u * WBLK, h * QBLK + (u + 1) * WBLK) for h in range(N_HEADS)]
        q_sub = jnp.concatenate([qst_bf[:, c] for c in qcols], axis=1)
        w = n * (QBLK // WBLK) + u
        kt0 = max(w - NSA_WINDOW // WBLK, 0)
        n_win = w - kt0 + 1
        span = n_win * WBLK
        k0 = kt0 * WBLK
        rel = (q0 + u * WBLK + _iota((1, WBLK), 1)) - (k0 + _iota((span, 1), 0))
        in_band = lax.bitcast_convert_type(rel, jnp.uint32) < NSA_WINDOW
        s = jnp.dot(kw_ref[k0:k0 + span, :], q_sub, preferred_element_type=F32)
        s = s + lanes4(jnp.where(in_band, 0.0, NEG))
        p = jnp.exp2(s - jnp.max(s, axis=0, keepdims=True)).astype(BF16)
        acc = jnp.zeros((V_AUG, N_HEADS * WBLK), F32)
        for c in range(n_win):
            acc = acc + jnp.dot(vtw_ref[kt0 + c], p[c * WBLK:(c + 1) * WBLK], preferred_element_type=F32)
        win_parts.append(acc[0:HEAD_DIM] / acc[HEAD_DIM:HEAD_DIM + 1])
    o_win = jnp.concatenate([part[:, h * WBLK:(h + 1) * WBLK] for h in range(N_HEADS) for part in win_parts], axis=1)

    pad_rows = jnp.zeros((HEAD_DIM - SEL_ROWS, cols4), BF16)
    hk = KT // 2
    tri = jnp.where(_iota((hk, hk), 0) <= _iota((hk, hk), 1), 0.0, NEG)
    late = [slice(h * QBLK + hk, (h + 1) * QBLK) for h in range(N_HEADS)]

    def q_aug(kt):
        picks = jnp.concatenate([sel_bias[kt * blocks_per_tile:(kt + 1) * blocks_per_tile],
                                 jnp.zeros((SEL_ROWS - blocks_per_tile, QBLK), F32)], axis=0)
        return jnp.concatenate([qst_bf, lanes4(picks).astype(BF16), pad_rows], axis=0)

    def sel_update(vt, s, m_old, acc):
        m_new = jnp.maximum(m_old, jnp.max(s, axis=0, keepdims=True))
        p = jnp.exp2(s - m_new)
        return m_new, jnp.exp2(m_old - m_new) * acc + jnp.dot(vt, p.astype(BF16), preferred_element_type=F32)

    m_s = jnp.full((1, cols4), NEG, F32)
    acc_s = jnp.zeros((V_AUG, cols4), F32)
    for kt in range(n):
        s = jnp.dot(ks_ref[kt * KT:(kt + 1) * KT, :], q_aug(kt), preferred_element_type=F32)
        m_s, acc_s = sel_update(vts_ref[kt], s, m_s, acc_s)
    qa = q_aug(n)
    bias_a = lanes4(jnp.concatenate([tri, jnp.zeros((hk, hk), F32)], axis=1))
    s_a = jnp.dot(ks_ref[n * KT:n * KT + hk, :], qa, preferred_element_type=F32) + bias_a
    m_a, acc_a = sel_update(vts_ref[n][:, 0:hk], s_a, m_s, acc_s)
    q_late = jnp.concatenate([qa[:, c] for c in late], axis=1)
    m_l = jnp.concatenate([m_a[:, c] for c in late], axis=1)
    acc_l = jnp.concatenate([acc_a[:, c] for c in late], axis=1)
    s_b = jnp.dot(ks_ref[n * KT + hk:(n + 1) * KT, :], q_late, preferred_element_type=F32) + lanes4(tri)
    _, acc_b = sel_update(vts_ref[n][:, hk:KT], s_b, m_l, acc_l)
    acc_s = jnp.concatenate([part for h in range(N_HEADS)
                             for part in (acc_a[:, h * QBLK:h * QBLK + hk], acc_b[:, h * hk:(h + 1) * hk])], axis=1)
    o_slc = acc_s[0:HEAD_DIM] / acc_s[HEAD_DIM:HEAD_DIM + 1]

    gate = _sigmoid(g_ref[0]).T
    outs = []
    for h in range(N_HEADS):
        cs = slice(h * QBLK, (h + 1) * QBLK)
        g0 = D_GATE_LANE + 3 * h
        outs.append(gate[g0:g0 + 1, :] * o_cmp[:, cs] + gate[g0 + 1:g0 + 2, :] * o_slc[:, cs]
                    + gate[g0 + 2:g0 + 3, :] * o_win[:, cs])
    o_ref[0] = (jnp.concatenate(outs, axis=0).T * _silu(z_ref[0])).astype(Y_DTYPE)
    return ()


def _mixer_d(proj, qtabs, ktabs, ctabs, w1bd, pe2, w2bd, overlap):
    QBLK = D_QBLK
    bsz, seq, _ = proj.shape
    blk = lambda name: _new_offset(name) // GROUP_W
    full = lambda shape: pl.BlockSpec(shape, lambda b, i: (0,) * len(shape))
    return pl.pallas_call(
        _mixer_d_body,
        grid=(bsz, seq // QBLK),
        in_specs=[pl.BlockSpec((1, QBLK, GROUP_W), lambda b, i: (b, i, blk('d_q'))),
                  pl.BlockSpec((1, QBLK, GROUP_W), lambda b, i: (b, i, blk('d_z'))),
                  pl.BlockSpec((1, seq, 128), lambda b, i: (b, 0, _new_offset('d_ksv') // 128)),
                  pl.BlockSpec((1, seq, 128), lambda b, i: (b, 0, _new_offset('d_kwv') // 128)),
                  pl.BlockSpec((1, seq, 128), lambda b, i: (b, 0, _new_offset('d_kvc') // 128)),
                  pl.BlockSpec((1, QBLK, LANES), lambda b, i: (b, i, _new_offset('gates') // LANES)),
                  full((seq // QBLK, HALF, QBLK)), full((seq // QBLK, HALF, QBLK)),
                  full((seq, 128)), full((seq, 128)), full((seq, 128)), full((LANES, LANES)),
                  full((N_CMP_PAD, 128)), full((N_CMP_PAD, 128)), full((N_CMP_PAD, 128)),
                  full((CMP_LEN // 4, 2 * LANES, 2 * LANES)), full((CMP_LEN // 4, PE_ROWS, 2 * LANES)), full((128, 128)),
                  full((seq // SLC_LEN, N_CMP_PAD))],
        out_specs=pl.BlockSpec((1, QBLK, GROUP_W), lambda b, i: (b, i, 0)),
        out_shape=jax.ShapeDtypeStruct((bsz, seq, GROUP_W), Y_DTYPE),
        scratch_shapes=[pltpu.VMEM((seq, 2 * HEAD_DIM), BF16),
                        pltpu.VMEM((seq, HEAD_DIM), BF16),
                        pltpu.VMEM((seq // KT, V_AUG, KT), BF16),
                        pltpu.VMEM((seq // WBLK, V_AUG, WBLK), BF16),
                        pltpu.VMEM((N_CMP_PAD, HEAD_DIM), F32),
                        pltpu.VMEM((HEAD_DIM, N_CMP_PAD), BF16)],
        compiler_params=_cparams(("arbitrary", "arbitrary")),
        name="mixer_d",
    )(proj, proj, proj, proj, proj, proj, *qtabs, *ktabs, _rope_perm(LANES, HEAD_DIM), *ctabs, w1bd, pe2, w2bd,
      overlap)


def _cmp_params(pe, w1, w2):
    w1 = w1.reshape(2, CMP_LEN, HEAD_DIM, HEAD_DIM)
    z = jnp.zeros((CMP_LEN, HEAD_DIM, HEAD_DIM), F32)
    w1bd = jnp.concatenate([jnp.concatenate([w1[0], z], axis=2), jnp.concatenate([z, w1[1]], axis=2)], axis=1)
    z2 = jnp.zeros((HEAD_DIM, HEAD_DIM), F32)
    w2bd = jnp.concatenate([jnp.concatenate([w2[0], z2], axis=1), jnp.concatenate([z2, w2[1]], axis=1)], axis=0)
    pe2 = jnp.concatenate([pe[0], pe[1]], axis=1)
    half = CMP_LEN // 2
    w1pair = jnp.stack([jnp.concatenate([jnp.concatenate([w1bd[r], w1bd[half + r]], axis=1),
                                         jnp.concatenate([w1bd[r + 1], w1bd[half + r + 1]], axis=1)], axis=0)
                        for r in range(0, half, 2)])
    pe_rows = jnp.stack([jnp.concatenate([jnp.concatenate([pe2[r], pe2[r + 1]])[None, :],
                                          jnp.concatenate([pe2[half + r], pe2[half + r + 1]])[None, :],
                                          jnp.zeros((PE_ROWS - 2, 2 * LANES), F32)], axis=0)
                         for r in range(0, half, 2)])
    return w1pair, pe_rows, w2bd


def _overlap_matrix(seq):
    n_cmp = (seq - CMP_LEN) // CMP_STRIDE + 1
    starts = np.arange(N_CMP_PAD) * CMP_STRIDE
    blk = np.arange(seq // SLC_LEN)
    ov = ((starts[None, :] < (blk[:, None] + 1) * SLC_LEN) & (starts[None, :] + CMP_LEN > blk[:, None] * SLC_LEN))
    ov = ov & (np.arange(N_CMP_PAD)[None, :] < n_cmp)
    return jnp.asarray(ov.astype(np.float32))


def kernel(x, ln0_g, ln0_b, w_in, b_in, a_sinks, b_conv_w, b_conv_b, c_lb, c_norm_g, d_cmp_pe, d_cmp_w1, d_cmp_w2,
           w_out, ln_g, ln_b):
    bsz, seq, d = x.shape
    pos = jnp.arange(seq)
    qtabs = _rope_tables(pos, GROUP_W, 128)
    ktabs = _rope_tables(pos, 128, HEAD_DIM)
    ctabs = _rope_tables(jnp.arange(N_CMP_PAD) * CMP_STRIDE + (CMP_LEN - 1), 128, HEAD_DIM)
    overlap = _overlap_matrix(seq)

    w_p, b_p = _prep_in_weights(w_in, b_in, 0)
    h, proj = _ln_in_proj(x.reshape(bsz * seq, d), ln0_g, ln0_b, w_p, b_p)
    for l in range(DEPTH):
        proj = proj.reshape(bsz, seq, N_PROJ)
        y_a = _mixer_a(proj, a_sinks[l], qtabs)
        y_b, y_c = _mixer_bc(proj, b_conv_w[l], b_conv_b[l], c_lb, c_norm_g[l], l)
        w1bd, pe2, w2bd = _cmp_params(d_cmp_pe[l], d_cmp_w1[l], d_cmp_w2[l])
        y_d = _mixer_d(proj, _rope_tables_t(seq, D_QBLK), ktabs, ctabs, w1bd, pe2, w2bd, overlap)
        ys = (y_a, y_b, y_c, y_d)
        if l + 1 < DEPTH:
            w_p, b_p = _prep_in_weights(w_in, b_in, l + 1)
            h, proj = _out_in_proj(ys, h, w_out[l].astype(BF16), ln_g[l], ln_b[l], w_p, b_p)
        else:
            h = _out_proj(ys, h, w_out[l].astype(BF16), ln_g[l], ln_b[l])
    return h.reshape(bsz, seq, d)
```

```python
import functools

import numpy as np
import jax
import jax.numpy as jnp
from jax import lax
from jax.experimental import pallas as pl
from jax.experimental.pallas import tpu as pltpu

F32 = jnp.float32
BF16 = jnp.bfloat16
Y_DTYPE = BF16

DEPTH = 2
HEAD_DIM = 64
HALF = HEAD_DIM // 2
N_HEADS = 4
GROUP_W = N_HEADS * HEAD_DIM
N_GROUPS = 4
ROPE_THETA = 10000.0
NEG = -1e30
LN_EPS = 1e-5
A_WINDOW = 128
A_GROUP = 2
A_SUB = 16
V_AUG = HEAD_DIM + 16
LOG2E = 1.4426950408889634
PRO_ROWS = 2048
A_PRO_ROWS = 2048
B_CONV = 4
C_CHUNK = 64
C_SUB = 8
BC_STEP = 1024
C_EPS = 1e-6
CMP_LEN = 32
CMP_STRIDE = 16
SLC_LEN = 64
N_SEL = 8
NSA_WINDOW = 512
FORCE_SCORE = 1e6
DN_ALPHA = (2.0 * DEPTH) ** 0.25
QBLK = 128
SCALE = HEAD_DIM ** -0.5

VMEM_LIMIT = 56 * 1024 * 1024
PROJ_TM = 512
PROJ_SUB = 256

LANES = 128

ORIG_SPLITS = (
    ('a_q', 256), ('a_k', 128), ('a_v', 128), ('a_z', 256),
    ('b_q', 256), ('b_k', 256), ('b_v', 256), ('b_if', 8), ('b_o', 256), ('b_z', 256),
    ('c_q', 256), ('c_f', 256), ('c_i', 256), ('c_z', 256),
    ('d_q', 256), ('d_kvc', 128), ('d_ksv', 128), ('d_kwv', 128), ('d_g', 12), ('d_z', 256),
)
N_COLS = sum(w for _, w in ORIG_SPLITS)
GATE_SLOTS = ('b_if', 'd_g')
NEW_LAYOUT = (
    ('a_q', 256), ('a_k', 128), ('a_v', 128), ('a_z', 256), ('b_q', 256), ('b_k', 256), ('b_v', 256),
    ('b_o', 256), ('b_z', 256), ('c_q', 256), ('c_f', 256), ('c_i', 256), ('c_z', 256),
    ('d_q', 256), ('d_kvc', 128), ('d_ksv', 128), ('d_kwv', 128), ('gates', 128), ('d_z', 256),
)
N_PROJ = sum(w for _, w in NEW_LAYOUT)
D_GATE_LANE = dict(ORIG_SPLITS)['b_if']


def _offset(layout, name):
    off = 0
    for n, w in layout:
        if n == name:
            return off
        off += w
    raise KeyError(name)


def _new_offset(name):
    return _offset(NEW_LAYOUT, name)


def _permuted_cols(load):
    sizes = dict(ORIG_SPLITS)
    for name, width in NEW_LAYOUT:
        new = _new_offset(name)
        if name != 'gates':
            old = _offset(ORIG_SPLITS, name)
            lo = old // LANES * LANES
            hi = min(-(-(old + width) // LANES) * LANES, N_COLS)
            yield new, load(lo, hi)[..., old - lo:old - lo + width]
            continue
        gates, lane0 = None, 0
        for slot in GATE_SLOTS:
            old = _offset(ORIG_SPLITS, slot)
            lo = old // LANES * LANES
            tile = load(lo, lo + LANES)
            if old - lo != lane0:
                tile = pltpu.roll(tile, (lane0 - (old - lo)) % LANES, axis=tile.ndim - 1)
            lane = _iota(tile.shape, tile.ndim - 1)
            part = jnp.where((lane >= lane0) & (lane < lane0 + sizes[slot]), tile, 0.0)
            gates = part if gates is None else gates + part
            lane0 += sizes[slot]
        yield new, gates


def _prep_body(w_ref, b_ref, wo_ref, bo_ref):
    for new, cols in _permuted_cols(lambda lo, hi: w_ref[0, :, lo:hi]):
        wo_ref[:, new:new + cols.shape[-1]] = cols.astype(BF16)
    for new, cols in _permuted_cols(lambda lo, hi: b_ref[0, :, lo:hi]):
        bo_ref[:, new:new + cols.shape[-1]] = cols


def _prep_in_weights(w_in, b_in, layer, rows=128):
    depth, d, n = w_in.shape
    return pl.pallas_call(
        _prep_body,
        grid=(d // rows,),
        in_specs=[pl.BlockSpec((1, rows, n), lambda r: (layer, r, 0)),
                  pl.BlockSpec((1, 1, n), lambda r: (layer, 0, 0))],
        out_specs=[pl.BlockSpec((rows, N_PROJ), lambda r: (r, 0)),
                   pl.BlockSpec((1, N_PROJ), lambda r: (0, 0))],
        out_shape=[jax.ShapeDtypeStruct((d, N_PROJ), BF16), jax.ShapeDtypeStruct((1, N_PROJ), F32)],
        compiler_params=_cparams(("arbitrary",)),
        name="prep_in_weights",
    )(w_in, b_in.reshape(depth, 1, n))


def _mm(a, b):
    return jnp.dot(a.astype(BF16), b.astype(BF16), preferred_element_type=F32)


def _mm_nt(a, b):
    return lax.dot_general(a.astype(BF16), b.astype(BF16), (((1,), (1,)), ((), ())),
                           preferred_element_type=F32)


def _mm_tn(a, b):
    return lax.dot_general(a.astype(BF16), b.astype(BF16), (((0,), (0,)), ((), ())),
                           preferred_element_type=F32)


def _dot_bf16(a, b):
    return jnp.dot(a, b, preferred_element_type=F32)


def _split_bf16(x, terms):
    out = []
    for _ in range(terms - 1):
        t = x.astype(BF16)
        out.append(t)
        x = x - t.astype(F32)
    out.append(x.astype(BF16))
    return out


def _mm_hi(a, b):
    ah, al = _split_bf16(a, 2)
    bh, bl = _split_bf16(b, 2)
    return _dot_bf16(ah, bh) + (_dot_bf16(ah, bl) + _dot_bf16(al, bh))


def _mm_sel(sel, x):
    sel = sel.astype(BF16)
    x1, x2, x3 = _split_bf16(x, 3)
    return _dot_bf16(sel, x1) + (_dot_bf16(sel, x2) + _dot_bf16(sel, x3))


def _sigmoid(x):
    return 1.0 / (1.0 + jnp.exp2(x * (-LOG2E)))


def _silu(x):
    return x * _sigmoid(x)


def _log_sigmoid(x):
    return jnp.minimum(x, 0.0) - jnp.log1p(jnp.exp(-jnp.abs(x)))


def _rope(x, cos, sin_up, sin_dn):
    w = x.shape[-1]
    up = pltpu.roll(x, w - HALF, axis=1)
    dn = pltpu.roll(x, HALF, axis=1)
    return x * cos + up * sin_up + dn * sin_dn


def _rope_bf16(x, cos, sin_up, sin_dn, perm):
    swapped = jnp.dot(x.astype(BF16), perm, preferred_element_type=F32)
    return x * cos + swapped * (sin_up + sin_dn)


def _rope_perm(width, rot_lanes):
    lane = np.arange(width)
    src = np.where(lane % HEAD_DIM < HALF, lane + HALF, lane - HALF)
    rotated = (lane % LANES) < rot_lanes
    perm = (np.arange(width)[:, None] == src[None, :]) & rotated[None, :]
    return jnp.asarray(perm.astype(np.float32), dtype=BF16)


def _rope_tables_t(seq, blk):
    inv = ROPE_THETA ** (-jnp.arange(HALF, dtype=F32) / HALF)
    ang = jnp.arange(seq).astype(F32)[:, None] * inv[None, :]
    to_blocks = lambda a: a.T.reshape(HALF, seq // blk, blk).transpose(1, 0, 2)
    return to_blocks(jnp.cos(ang)), to_blocks(jnp.sin(ang))


def _rope_tables(pos, width, rot_lanes):
    inv = ROPE_THETA ** (-jnp.arange(HALF, dtype=F32) / HALF)
    ang = pos.astype(F32)[:, None] * inv[None, :]
    cos, sin = jnp.cos(ang), jnp.sin(ang)
    zero, one = jnp.zeros_like(sin), jnp.ones_like(cos)
    cos_h = jnp.concatenate([cos, cos], axis=1)
    up_h = jnp.concatenate([-sin, zero], axis=1)
    dn_h = jnp.concatenate([zero, sin], axis=1)
    id_c = jnp.concatenate([one, one], axis=1)
    id_s = jnp.concatenate([zero, zero], axis=1)
    cs, us, ds = [], [], []
    for l0 in range(0, width, HEAD_DIM):
        rot = (l0 % 128) < rot_lanes
        cs.append(cos_h if rot else id_c)
        us.append(up_h if rot else id_s)
        ds.append(dn_h if rot else id_s)
    return jnp.concatenate(cs, axis=1), jnp.concatenate(us, axis=1), jnp.concatenate(ds, axis=1)


def _iota(shape, dim):
    return lax.broadcasted_iota(jnp.int32, shape, dim)


def _cparams(sem):
    return pltpu.CompilerParams(dimension_semantics=sem, vmem_limit_bytes=VMEM_LIMIT)


def _ln_rows(x, g, b):
    mu = jnp.mean(x, axis=-1, keepdims=True)
    xc = x - mu
    var = jnp.mean(xc * xc, axis=-1, keepdims=True)
    return xc * lax.rsqrt(var + LN_EPS) * g + b


def _mix_out(rows, ya_ref, yb_ref, yc_ref, yd_ref, h_ref, w_ref, g_ref, b_ref):
    y = _mm(ya_ref[rows, :], w_ref[0:GROUP_W, :])
    y += _mm(yb_ref[rows, :], w_ref[GROUP_W:2 * GROUP_W, :])
    y += _mm(yc_ref[rows, :], w_ref[2 * GROUP_W:3 * GROUP_W, :])
    y += _mm(yd_ref[rows, :], w_ref[3 * GROUP_W:4 * GROUP_W, :])
    return _ln_rows(DN_ALPHA * h_ref[rows, :] + y, g_ref[...], b_ref[...])


def _sub_tiles(ref):
    return [slice(r, r + PROJ_SUB) for r in range(0, ref.shape[0], PROJ_SUB)]


def _ln_proj_body(x_ref, g_ref, b_ref, w_ref, bias_ref, h_ref, o_ref):
    for rows in _sub_tiles(x_ref):
        h = _ln_rows(x_ref[rows, :], g_ref[...], b_ref[...])
        h_ref[rows, :] = h
        o_ref[rows, :] = jnp.dot(h.astype(BF16), w_ref[...], preferred_element_type=F32) + bias_ref[...]


def _out_proj_body(ya_ref, yb_ref, yc_ref, yd_ref, hp_ref, wo_ref, g_ref, b_ref, w_ref, bias_ref, h_ref, o_ref):
    for rows in _sub_tiles(hp_ref):
        h = _mix_out(rows, ya_ref, yb_ref, yc_ref, yd_ref, hp_ref, wo_ref, g_ref, b_ref)
        h_ref[rows, :] = h
        o_ref[rows, :] = jnp.dot(h.astype(BF16), w_ref[...], preferred_element_type=F32) + bias_ref[...]


def _ln_in_proj(x2, g, b, w_bf, bias, tm=PROJ_TM):
    m, d = x2.shape
    n = w_bf.shape[1]
    row = lambda width: pl.BlockSpec((tm, width), lambda i: (i, 0))
    const = lambda shape: pl.BlockSpec(shape, lambda i: (0, 0))
    return pl.pallas_call(
        _ln_proj_body,
        grid=(m // tm,),
        in_specs=[row(d), const((1, d)), const((1, d)), const((d, n)), const((1, n))],
        out_specs=[row(d), row(n)],
        out_shape=[jax.ShapeDtypeStruct((m, d), F32), jax.ShapeDtypeStruct((m, n), F32)],
        compiler_params=_cparams(("parallel",)),
        name="ln_in_proj",
    )(x2, g.reshape(1, d), b.reshape(1, d), w_bf, bias.reshape(1, n))


def _out_in_proj(ys, h2, wo_bf, g, b, w_bf, bias, tm=PROJ_TM):
    m, d = h2.shape
    n = w_bf.shape[1]
    row = lambda width: pl.BlockSpec((tm, width), lambda i: (i, 0))
    const = lambda shape: pl.BlockSpec(shape, lambda i: (0, 0))
    return pl.pallas_call(
        _out_proj_body,
        grid=(m // tm,),
        in_specs=[row(GROUP_W)] * N_GROUPS + [row(d), const((N_GROUPS * GROUP_W, d)), const((1, d)), const((1, d)),
                                              const((d, n)), const((1, n))],
        out_specs=[row(d), row(n)],
        out_shape=[jax.ShapeDtypeStruct((m, d), F32), jax.ShapeDtypeStruct((m, n), F32)],
        compiler_params=_cparams(("parallel",)),
        name="out_in_proj",
    )(*[y.reshape(m, GROUP_W) for y in ys], h2, wo_bf, g.reshape(1, d), b.reshape(1, d), w_bf, bias.reshape(1, n))


def _out_body(ya_ref, yb_ref, yc_ref, yd_ref, h_ref, w_ref, g_ref, b_ref, o_ref):
    o_ref[...] = _mix_out(slice(None), ya_ref, yb_ref, yc_ref, yd_ref, h_ref, w_ref, g_ref, b_ref)


def _out_proj(ys, h2, w_bf, g, b, tm=512):
    m, d = h2.shape
    yspec = pl.BlockSpec((tm, GROUP_W), lambda i: (i, 0))
    return pl.pallas_call(
        _out_body,
        grid=(m // tm,),
        in_specs=[yspec, yspec, yspec, yspec,
                  pl.BlockSpec((tm, d), lambda i: (i, 0)),
                  pl.BlockSpec((4 * GROUP_W, d), lambda i: (0, 0)),
                  pl.BlockSpec((1, d), lambda i: (0, 0)),
                  pl.BlockSpec((1, d), lambda i: (0, 0))],
        out_specs=pl.BlockSpec((tm, d), lambda i: (i, 0)),
        out_shape=jax.ShapeDtypeStruct((m, d), F32),
        compiler_params=_cparams(("parallel",)),
        name="out_proj",
    )(*[y.reshape(m, GROUP_W) for y in ys], h2, w_bf, g.reshape(1, d), b.reshape(1, d))


def _mixer_a_body(sink_ref, q_ref, kv_ref, z_ref, cos_ref, up_ref, dn_ref, perm_ref, o_ref, k_ref, vt_ref):
    i = pl.program_id(1)
    seq = kv_ref.shape[1]
    n_kv = N_HEADS // A_GROUP
    low_half = _iota((1, LANES), 1) < HEAD_DIM

    @pl.when(i == 0)
    def _():
        def rope_rows(c, carry):
            for u in range(A_PRO_ROWS // QBLK):
                t0 = c * (A_PRO_ROWS // QBLK) + u
                rows = pl.ds(pl.multiple_of(t0 * QBLK, QBLK), QBLK)
                kr = _rope_bf16(kv_ref[0, rows, 0:LANES], cos_ref[rows, 0:LANES], up_ref[rows, 0:LANES],
                                dn_ref[rows, 0:LANES], perm_ref[0:LANES, 0:LANES])
                v_t = _mm_nt(_iota((LANES, LANES), 0) == _iota((LANES, LANES), 1), kv_ref[0, rows, LANES:2 * LANES])
                v_t = v_t.astype(BF16)
                ones_rows = jnp.ones((V_AUG - HEAD_DIM, QBLK), BF16)
                other = pltpu.roll(kr, HEAD_DIM, axis=1)
                for g in range(n_kv):
                    dup = jnp.where(low_half, kr, other) if g == 0 else jnp.where(low_half, other, kr)
                    k_ref[g, rows, :] = dup.astype(BF16)
                    vt_ref[g, t0] = jnp.concatenate([v_t[g * HEAD_DIM:(g + 1) * HEAD_DIM, :], ones_rows], axis=0)
            return carry
        lax.fori_loop(0, seq // A_PRO_ROWS, rope_rows, 0)

    n_blk = A_WINDOW // QBLK + 1
    span = n_blk * QBLK
    first_head = _iota((1, A_GROUP * QBLK), 1) < QBLK
    eye = (_iota((QBLK, QBLK), 0) == _iota((QBLK, QBLK), 1)).astype(BF16)

    def query_block(u):
        blk_i = i * A_SUB + u
        sub = slice(u * QBLK, (u + 1) * QBLK)
        q0 = pl.multiple_of(blk_i * QBLK, QBLK)
        qrows = pl.ds(q0, QBLK)
        qr = _rope(q_ref[0, sub, :], cos_ref[qrows, :], up_ref[qrows, :], dn_ref[qrows, :]) * (SCALE * LOG2E)
        kt0 = jnp.maximum(blk_i - A_WINDOW // QBLK, 0)
        k0 = pl.multiple_of(kt0 * QBLK, QBLK)
        rel = (q0 + _iota((1, QBLK), 1)) - (k0 + _iota((span, 1), 0))
        in_band = lax.bitcast_convert_type(rel, jnp.uint32) < A_WINDOW
        bias = jnp.concatenate([jnp.where(in_band, 0.0, NEG)] * A_GROUP, axis=1)
        outs = []
        for g in range(n_kv):
            q_pair = qr[:, g * LANES:(g + 1) * LANES]
            qg = jnp.concatenate([jnp.where(low_half, q_pair, 0.0), jnp.where(low_half, 0.0, q_pair)], axis=0)
            sink = jnp.where(first_head, sink_ref[g * A_GROUP], sink_ref[g * A_GROUP + 1]) * LOG2E
            s = _mm_nt(k_ref[g, pl.ds(k0, span), :], qg) + bias
            m = jnp.maximum(jnp.max(s, axis=0, keepdims=True), sink)
            p = jnp.exp2(s - m).astype(BF16)
            o = jnp.zeros((V_AUG, A_GROUP * QBLK), F32)
            for c in range(n_blk):
                o = o + jnp.dot(vt_ref[g, kt0 + c], p[c * QBLK:(c + 1) * QBLK], preferred_element_type=F32)
            o = o[0:HEAD_DIM] / (o[HEAD_DIM:HEAD_DIM + 1] + jnp.exp2(sink - m))
            outs.extend(o[:, r * QBLK:(r + 1) * QBLK] for r in range(A_GROUP))
        y = _mm_nt(eye, jnp.concatenate(outs, axis=0))
        o_ref[0, sub, :] = (y * _silu(z_ref[0, sub, :])).astype(Y_DTYPE)

    for u in range(A_SUB):
        query_block(u)


def _mixer_a(proj, sinks, tabs):
    assert A_GROUP == 2 and A_GROUP * HEAD_DIM == LANES, "a kv head's two query heads share one lane tile"
    bsz, seq, _ = proj.shape
    cos, up, dn = tabs
    blk = lambda name: _new_offset(name) // GROUP_W
    tspec = pl.BlockSpec((seq, GROUP_W), lambda b, i: (0, 0))
    return pl.pallas_call(
        _mixer_a_body,
        grid=(bsz, seq // (A_SUB * QBLK)),
        in_specs=[pl.BlockSpec(memory_space=pltpu.SMEM),
                  pl.BlockSpec((1, A_SUB * QBLK, GROUP_W), lambda b, i: (b, i, blk('a_q'))),
                  pl.BlockSpec((1, seq, GROUP_W), lambda b, i: (b, 0, blk('a_k'))),
                  pl.BlockSpec((1, A_SUB * QBLK, GROUP_W), lambda b, i: (b, i, blk('a_z'))),
                  tspec, tspec, tspec, pl.BlockSpec((GROUP_W, GROUP_W), lambda b, i: (0, 0))],
        out_specs=pl.BlockSpec((1, A_SUB * QBLK, GROUP_W), lambda b, i: (b, i, 0)),
        out_shape=jax.ShapeDtypeStruct((bsz, seq, GROUP_W), Y_DTYPE),
        scratch_shapes=[pltpu.VMEM((N_HEADS // A_GROUP, seq, LANES), BF16),
                        pltpu.VMEM((N_HEADS // A_GROUP, seq // QBLK, V_AUG, QBLK), BF16)],
        compiler_params=_cparams(("arbitrary", "arbitrary")),
        name="mixer_a",
    )(sinks, proj, proj, proj, cos, up, dn, _rope_perm(GROUP_W, LANES))


B_CHUNK = 128
B_AUG = V_AUG


def _conv_silu(x_ref, r0, prev, w, b):
    L = B_CHUNK
    x = x_ref[0, r0:r0 + L, :]
    acc = b + w[B_CONV - 1:B_CONV, :] * x
    for s in range(1, B_CONV):
        if r0 >= s:
            shifted = x_ref[0, r0 - s:r0 - s + L, :]
        else:
            row8 = _iota((8, x.shape[1]), 0)
            xs = pltpu.roll(x, s, axis=0)
            ps = pltpu.roll(prev, s, axis=0)
            shifted = jnp.concatenate([jnp.where(row8 < s, ps, xs[0:8]), xs[8:]], axis=0)
        acc = acc + w[B_CONV - 1 - s:B_CONV - s, :] * shifted
    return _silu(acc)


def _mixer_b_init(xprev_ref, c_ref, m_ref):
    xprev_ref[...] = jnp.zeros_like(xprev_ref)
    c_ref[...] = jnp.zeros_like(c_ref)
    m_ref[...] = jnp.zeros_like(m_ref)


def _mixer_b_main(xq_ref, xk_ref, v_ref, og_ref, z_ref, if_ref, cw_ref, cb_ref, o_ref, xprev_ref, c_ref, m_ref):
    L = B_CHUNK
    tri = (_iota((L, L), 1) <= _iota((L, L), 0)).astype(F32)
    key_first = _iota((L, L), 0) <= _iota((L, L), 1)
    half_of_lane = _iota((1, 128), 1) // HEAD_DIM
    ones_rows = jnp.ones((B_AUG - HEAD_DIM, L), F32)

    def chunk(r0, state):
        rows = slice(r0, r0 + L)
        qc = _conv_silu(xq_ref, r0, xprev_ref[:, 0:GROUP_W], cw_ref[:, 0:GROUP_W], cb_ref[:, 0:GROUP_W]) * SCALE
        kc = _conv_silu(xk_ref, r0, xprev_ref[:, GROUP_W:], cw_ref[:, GROUP_W:], cb_ref[:, GROUP_W:])
        q_t = qc.T
        v_t = v_ref[0, rows, :].T

        gates = if_ref[0, rows, :]
        bcum = _mm_sel(tri, _log_sigmoid(gates))
        gates_t = gates.T
        bcum_t = bcum.T
        outs, new_state = [], []
        for h in range(N_HEADS):
            pair = slice((h // 2) * 128, (h // 2 + 1) * 128)
            own = half_of_lane == (h % 2)
            k_pair = kc[:, pair]
            q_pair_t = q_t[pair, :]
            i_row = gates_t[h:h + 1, :]
            b_row = bcum_t[N_HEADS + h:N_HEADS + h + 1, :]
            c_col = gates[:, h:h + 1] - bcum[:, N_HEADS + h:N_HEADS + h + 1]
            cst, m_prev = state[h]
            dmat = jnp.where(key_first, b_row + c_col, NEG)
            inter = b_row + m_prev
            m_t = jnp.maximum(jnp.max(dmat, axis=0, keepdims=True), inter)
            smat = _mm(jnp.where(own, k_pair, 0.0), q_pair_t) * jnp.exp(dmat - m_t)
            vaug_t = jnp.concatenate([v_t[h * HEAD_DIM:(h + 1) * HEAD_DIM, :], ones_rows], axis=0)
            tot = _mm(vaug_t, smat) + jnp.exp(inter - m_t) * _mm(cst, q_pair_t)
            num = tot[0:HEAD_DIM, :]
            den = tot[HEAD_DIM:HEAD_DIM + 1, :]
            outs.append(num / jnp.maximum(jnp.abs(den), jnp.exp(-m_t)))
            b_last = b_row[:, L - 1:L]
            d_row = b_last - b_row + i_row
            m_new = jnp.maximum(b_last + m_prev, jnp.max(d_row, axis=1, keepdims=True))
            w_row = jnp.exp(d_row - m_new)
            decay = jnp.exp(b_last + m_prev - m_new)
            new_state.append((decay * cst + jnp.where(own, _mm(vaug_t * w_row, k_pair), 0.0), m_new))
        o_ref[0, rows, :] = (_sigmoid(og_ref[0, rows, :]) * jnp.concatenate(outs, axis=0).T
                             * _silu(z_ref[0, rows, :])).astype(Y_DTYPE)
        return new_state

    rows_step = xq_ref.shape[1]
    state = [(c_ref[h], m_ref[h:h + 1, 0:1]) for h in range(N_HEADS)]
    for u in range(rows_step // L):
        state = chunk(u * L, state)
        yield
    xprev_ref[:, 0:GROUP_W] = xq_ref[0, rows_step - 8:rows_step, :]
    xprev_ref[:, GROUP_W:] = xk_ref[0, rows_step - 8:rows_step, :]
    for h in range(N_HEADS):
        c_ref[h] = state[h][0]
        m_ref[h:h + 1, :] = jnp.broadcast_to(state[h][1], (1, 128))


def _mixer_c_main(layer, q_ref, f_ref, i_ref, z_ref, lb_ref, gn_ref, tri_ref, piv_ref, o_ref, st_ref):
    L = C_CHUNK
    nsub = L // C_SUB

    raw = lb_ref[...]
    ex = jnp.exp(raw - jnp.max(raw, axis=0, keepdims=True))
    if layer == 0:
        lb = jnp.zeros((1, GROUP_W), F32)
    else:
        lb = jnp.sum(ex[1:layer + 1], axis=0, keepdims=True) / jnp.sum(ex, axis=0, keepdims=True)
    tri = (_iota((L, L), 1) <= _iota((L, L), 0)).astype(F32)
    head_of_lane = _iota((1, GROUP_W), 1) // HEAD_DIM
    same_head = ((_iota((GROUP_W, GROUP_W), 0) // HEAD_DIM)
                 == (_iota((GROUP_W, GROUP_W), 1) // HEAD_DIM))
    ones_bd = same_head.astype(BF16)

    def chunk(rows, st):
        f = lb + (1.0 - lb) * _sigmoid(f_ref[0, rows, :])
        logf = jnp.log(f)
        k = 1.0 - f
        q = _silu(q_ref[0, rows, :])
        v = i_ref[0, rows, :]
        a = _mm_sel(tri, logf) * LOG2E
        o = _mm_nt(q * jnp.exp2(a), st)

        qsts, kts, vreps = [], [], []
        for i in range(1, nsub):
            r0 = i * C_SUB
            piv = a[r0 - 1:r0, :]
            qt = q[r0:r0 + C_SUB] * jnp.exp2(a[r0:r0 + C_SUB] - piv)
            qsts.extend(jnp.where(head_of_lane == h, qt, 0.0) for h in range(N_HEADS))
            kts.append((k[0:r0] * jnp.exp2(piv - a[0:r0])).astype(BF16))
            vreps.append(v[0:r0].astype(BF16))
        scores = _mm_nt(jnp.concatenate(qsts, axis=0), jnp.concatenate(kts, axis=0))
        r = _dot_bf16((scores * piv_ref[...]).astype(BF16), jnp.concatenate(vreps, axis=0))
        pieces = [o[0:C_SUB]]
        for i in range(1, nsub):
            acc = o[i * C_SUB:(i + 1) * C_SUB]
            for h in range(N_HEADS):
                g0 = ((i - 1) * N_HEADS + h) * C_SUB
                acc = acc + jnp.where(head_of_lane == h, r[g0:g0 + C_SUB], 0.0)
            pieces.append(acc)
        o = jnp.concatenate(pieces, axis=0)

        a3 = a.reshape(nsub, C_SUB, GROUP_W)
        q3 = q.reshape(nsub, C_SUB, GROUP_W)
        k3 = k.reshape(nsub, C_SUB, GROUP_W)
        v3 = v.reshape(nsub, C_SUB, GROUP_W)
        ps = []
        for s in range(C_SUB):
            e = jnp.exp2(a3 - a3[:, s:s + 1, :] + tri_ref[s])
            ps.append((q3 * k3[:, s:s + 1, :] * e).reshape(L, GROUP_W).astype(BF16))
        w = jnp.dot(jnp.concatenate(ps, axis=0), ones_bd, preferred_element_type=F32)
        for s in range(C_SUB):
            vs = jnp.broadcast_to(v3[:, s:s + 1, :], (nsub, C_SUB, GROUP_W)).reshape(L, GROUP_W)
            o = o + w[s * L:(s + 1) * L] * vs

        a_last = a[L - 1:L, :]
        kdec = k * jnp.exp2(a_last - a)
        st_new = st * jnp.exp2(a_last) + jnp.where(same_head, _mm_tn(v, kdec), 0.0)

        ms = jnp.dot((o * o).astype(BF16), ones_bd, preferred_element_type=F32) * (1.0 / HEAD_DIM)
        o = o * lax.rsqrt(ms + C_EPS) * gn_ref[...]
        o_ref[0, rows, :] = (o * _silu(z_ref[0, rows, :])).astype(Y_DTYPE)
        return st_new

    st = st_ref[...]
    for u in range(q_ref.shape[1] // L):
        st = chunk(slice(u * L, (u + 1) * L), st)
        yield
    st_ref[...] = st


N_B_IN, N_C_IN, N_B_SCRATCH = 8, 8, 3
C_PER_B = B_CHUNK // C_CHUNK


def _mixer_bc_body(layer, *refs):
    b_in = refs[:N_B_IN]
    c_in = refs[N_B_IN:N_B_IN + N_C_IN]
    ob_ref, oc_ref = refs[N_B_IN + N_C_IN:N_B_IN + N_C_IN + 2]
    scratch = refs[N_B_IN + N_C_IN + 2:]
    b_scratch, c_scratch = scratch[:N_B_SCRATCH], scratch[N_B_SCRATCH:]

    @pl.when(pl.program_id(1) == 0)
    def _():
        _mixer_b_init(*b_scratch)
        c_scratch[0][...] = jnp.zeros_like(c_scratch[0])

    pending = [_mixer_b_main(*b_in, ob_ref, *b_scratch), _mixer_c_main(layer, *c_in, oc_ref, *c_scratch)]
    weights = [1, C_PER_B]
    while pending:
        for gen, reps in list(zip(pending, weights)):
            for _ in range(reps):
                if next(gen, StopIteration) is StopIteration:
                    idx = pending.index(gen)
                    del pending[idx], weights[idx]
                    break


def _mixer_bc(proj, conv_w, conv_b, c_lb, norm_g, layer):
    bsz, seq, _ = proj.shape
    L = BC_STEP
    blk = lambda name: _new_offset(name) // GROUP_W
    spec = lambda name: pl.BlockSpec((1, L, GROUP_W), lambda b, c: (b, c, blk(name)))
    vec = pl.BlockSpec((1, GROUP_W), lambda b, c: (0, 0))
    t_ge_s = np.arange(C_SUB)[None, :, None] >= np.arange(C_SUB)[:, None, None]
    causal_bias = np.broadcast_to(np.where(t_ge_s, 0.0, NEG), (C_SUB, C_SUB, GROUP_W)).astype(np.float32)
    pivots = np.arange(1, C_CHUNK // C_SUB)
    row_piv = np.repeat(pivots, N_HEADS * C_SUB)
    col_piv = np.repeat(pivots, pivots * C_SUB)
    same_pivot = (row_piv[:, None] == col_piv[None, :]).astype(np.float32)
    out_spec = pl.BlockSpec((1, L, GROUP_W), lambda b, c: (b, c, 0))
    out_shape = jax.ShapeDtypeStruct((bsz, seq, GROUP_W), Y_DTYPE)
    return pl.pallas_call(
        functools.partial(_mixer_bc_body, layer),
        grid=(bsz, seq // L),
        in_specs=[spec('b_q'), spec('b_k'), spec('b_v'), spec('b_o'), spec('b_z'),
                  pl.BlockSpec((1, L, LANES), lambda b, c: (b, c, _new_offset('gates') // LANES)),
                  pl.BlockSpec((B_CONV, 2 * GROUP_W), lambda b, c: (0, 0)),
                  pl.BlockSpec((1, 2 * GROUP_W), lambda b, c: (0, 0)),
                  spec('c_q'), spec('c_f'), spec('c_i'), spec('c_z'),
                  pl.BlockSpec((DEPTH, GROUP_W), lambda b, c: (0, 0)), vec,
                  pl.BlockSpec((C_SUB, C_SUB, GROUP_W), lambda b, c: (0, 0, 0)),
                  pl.BlockSpec(same_pivot.shape, lambda b, c: (0, 0))],
        out_specs=[out_spec, out_spec],
        out_shape=[out_shape, out_shape],
        scratch_shapes=[pltpu.VMEM((8, 2 * GROUP_W), F32),
                        pltpu.VMEM((N_HEADS, B_AUG, 128), F32),
                        pltpu.VMEM((8, 128), F32),
                        pltpu.VMEM((GROUP_W, GROUP_W), F32)],
        compiler_params=_cparams(("arbitrary", "arbitrary")),
        name="mixer_bc",
    )(proj, proj, proj, proj, proj, proj, conv_w, conv_b.reshape(1, -1),
      proj, proj, proj, proj, c_lb, norm_g.reshape(1, -1), jnp.asarray(causal_bias), jnp.asarray(same_pivot))


N_CMP_PAD = 128
KT = 512
D_QBLK = 512
WBLK = 256
PE_ROWS = 16
SEL_ROWS = 16


def _mixer_d_body(q_ref, z_ref, ksv_ref, kwv_ref, cin_ref, g_ref, qcos_ref, qsin_ref, kcos_ref, kup_ref,
                  kdn_ref, kperm_ref, ccos_ref, cup_ref, cdn_ref, w1_ref, pe_ref, w2_ref, ov_ref, o_ref,
                  ks_ref, kw_ref, vts_ref, vtw_ref, kc_ref, vct_ref):
    QBLK = D_QBLK
    i = pl.program_id(1)
    seq = ksv_ref.shape[1]
    half_cmp = CMP_LEN // 2

    @pl.when(i == 0)
    def _():
        def rope_rows(c, carry):
            for u in range(PRO_ROWS // KT):
                t0 = c * (PRO_ROWS // KT) + u
                rows = pl.ds(pl.multiple_of(t0 * KT, KT), KT)
                kvs_in = ksv_ref[0, rows, :]
                kvw_in = kwv_ref[0, rows, :]
                kvs = _rope_bf16(kvs_in, kcos_ref[rows, :], kup_ref[rows, :], kdn_ref[rows, :], kperm_ref[...])
                kvw = _rope_bf16(kvw_in, kcos_ref[rows, :], kup_ref[rows, :], kdn_ref[rows, :], kperm_ref[...])
                blk_in_tile = _iota((KT, HEAD_DIM), 0) // SLC_LEN
                onehot = jnp.where(blk_in_tile == _iota((KT, HEAD_DIM), 1), 1.0, 0.0)
                ks_ref[rows, :] = jnp.concatenate([kvs[:, 0:HEAD_DIM], onehot], axis=1).astype(BF16)
                kw_ref[rows, :] = kvw[:, 0:HEAD_DIM].astype(BF16)
                ones_rows = jnp.ones((V_AUG - HEAD_DIM, KT), BF16)
                pick_v = _iota((HEAD_DIM, LANES), 1) == _iota((HEAD_DIM, LANES), 0) + HEAD_DIM
                vts_ref[t0] = jnp.concatenate([_mm_nt(pick_v, kvs_in).astype(BF16), ones_rows], axis=0)
                vwt = jnp.concatenate([_mm_nt(pick_v, kvw_in).astype(BF16), ones_rows], axis=0)
                for w in range(KT // WBLK):
                    vtw_ref[t0 * (KT // WBLK) + w] = vwt[:, w * WBLK:(w + 1) * WBLK]
            return carry
        lax.fori_loop(0, seq // PRO_ROWS, rope_rows, 0)
        acc = jnp.zeros((N_CMP_PAD + PE_ROWS, 2 * LANES), F32)
        for rr in range(half_cmp // 2):
            xa = cin_ref[0, pl.ds(2 * rr, N_CMP_PAD, stride=CMP_STRIDE), :]
            xb = cin_ref[0, pl.ds(2 * rr + 1, N_CMP_PAD, stride=CMP_STRIDE), :]
            lhs = jnp.concatenate([jnp.concatenate([xa, xb], axis=1), pe_ref[rr]], axis=0)
            acc = acc + _mm_hi(lhs, w1_ref[rr])
        u0 = acc[0:N_CMP_PAD, 0:LANES] + acc[N_CMP_PAD:N_CMP_PAD + 1, 0:LANES]
        u1 = acc[0:N_CMP_PAD, LANES:] + acc[N_CMP_PAD + 1:N_CMP_PAD + 2, LANES:]
        pre = u0 + pltpu.roll(u1, N_CMP_PAD - 1, axis=0)
        cv = _rope(_mm_hi(_silu(pre), w2_ref[...]), ccos_ref[...], cup_ref[...], cdn_ref[...])
        kc_ref[...] = cv[:, 0:HEAD_DIM]
        vct_ref[...] = cv.T[HEAD_DIM:2 * HEAD_DIM, :].astype(BF16)

    for n in range(seq // QBLK):
        _mixer_d_step(n, q_ref, z_ref, g_ref, qcos_ref, qsin_ref, ov_ref, o_ref,
                      ks_ref, kw_ref, vts_ref, vtw_ref, kc_ref, vct_ref)


def _mixer_d_step(n, q_ref, z_ref, g_ref, qcos_ref, qsin_ref, ov_ref, o_ref,
                  ks_ref, kw_ref, vts_ref, vtw_ref, kc_ref, vct_ref):
    QBLK = D_QBLK
    seq = ks_ref.shape[0]
    q0 = n * QBLK
    rows = slice(q0, q0 + QBLK)
    qt = (q_ref[0, rows, :] * (SCALE * LOG2E)).T
    cos_t = qcos_ref[n]
    sin_t = qsin_ref[n]
    heads_t = []
    for h in range(N_HEADS):
        x1 = qt[h * HEAD_DIM:h * HEAD_DIM + HALF]
        x2 = qt[h * HEAD_DIM + HALF:(h + 1) * HEAD_DIM]
        heads_t.append(jnp.concatenate([x1 * cos_t - x2 * sin_t, x1 * sin_t + x2 * cos_t], axis=0))
    qst = jnp.concatenate(heads_t, axis=1)
    qst_bf = qst.astype(BF16)
    cols4 = N_HEADS * QBLK
    qpos = q0 + _iota((1, QBLK), 1)

    def lanes4(x):
        return jnp.concatenate([x] * N_HEADS, axis=1)

    n_cmp = min((q0 + QBLK) // CMP_STRIDE, N_CMP_PAD)
    sc = _mm_hi(kc_ref[0:n_cmp, :], qst)
    ends = _iota((n_cmp, 1), 0) * CMP_STRIDE + (CMP_LEN - 1)
    scm = sc + lanes4(jnp.where(ends <= qpos, 0.0, NEG))
    e = jnp.exp2(scm - jnp.max(scm, axis=0, keepdims=True))
    l = jnp.sum(e, axis=0, keepdims=True)
    any_valid = lanes4(jnp.where(qpos >= CMP_LEN - 1, 1.0, 0.0))
    pc = e * (any_valid / l)
    if n_cmp < N_CMP_PAD:
        pc = jnp.concatenate([pc, jnp.zeros((N_CMP_PAD - n_cmp, cols4), F32)], axis=0)
    o_cmp = jnp.dot(vct_ref[...], pc.astype(BF16), preferred_element_type=F32)
    psum = pc[:, 0:QBLK] + pc[:, QBLK:2 * QBLK] + pc[:, 2 * QBLK:3 * QBLK] + pc[:, 3 * QBLK:4 * QBLK]

    n_slc = (q0 + QBLK) // SLC_LEN
    imp = _mm_sel(ov_ref[0:n_slc, :], psum)
    jl = _iota((n_slc, 1), 0)
    cur = jnp.right_shift(qpos, SLC_LEN.bit_length() - 1)
    imp = jnp.where(jl == 0, FORCE_SCORE, imp)
    imp = jnp.where(jl == cur, FORCE_SCORE, imp)
    imp = jnp.where(jl == cur - 1, FORCE_SCORE, imp)
    imp = jnp.where(jl <= cur, imp, NEG)
    rank = jnp.zeros((n_slc, QBLK), F32)
    for jp in range(n_slc):
        cj = imp[jp:jp + 1, :]
        tie = jnp.where(jl > jp, 1.0, 0.0)
        rank = rank + jnp.where(cj > imp, 1.0, jnp.where(cj == imp, tie, 0.0))
    sel_bias = jnp.where(rank < N_SEL, 0.0, NEG)
    blocks_per_tile = KT // SLC_LEN

    win_parts = []
    for u in range(QBLK // WBLK):
        qcols = [slice(h * QBLK + u * WBLK, h * QBLK + (u + 1) * WBLK) for h in range(N_HEADS)]
        q_sub = jnp.concatenate([qst_bf[:, c] for c in qcols], axis=1)
        w = n * (QBLK // WBLK) + u
        kt0 = max(w - NSA_WINDOW // WBLK, 0)
        n_win = w - kt0 + 1
        span = n_win * WBLK
        k0 = kt0 * WBLK
        rel = (q0 + u * WBLK + _iota((1, WBLK), 1)) - (k0 + _iota((span, 1), 0))
        in_band = lax.bitcast_convert_type(rel, jnp.uint32) < NSA_WINDOW
        s = jnp.dot(kw_ref[k0:k0 + span, :], q_sub, preferred_element_type=F32)
        s = s + lanes4(jnp.where(in_band, 0.0, NEG))
        p = jnp.exp2(s - jnp.max(s, axis=0, keepdims=True)).astype(BF16)
        acc = jnp.zeros((V_AUG, N_HEADS * WBLK), F32)
        for c in range(n_win):
            acc = acc + jnp.dot(vtw_ref[kt0 + c], p[c * WBLK:(c + 1) * WBLK], preferred_element_type=F32)
        win_parts.append(acc[0:HEAD_DIM] / acc[HEAD_DIM:HEAD_DIM + 1])
    o_win = jnp.concatenate([part[:, h * WBLK:(h + 1) * WBLK] for h in range(N_HEADS) for part in win_parts], axis=1)

    pad_rows = jnp.zeros((HEAD_DIM - SEL_ROWS, cols4), BF16)
    hk = KT // 2
    tri = jnp.where(_iota((hk, hk), 0) <= _iota((hk, hk), 1), 0.0, NEG)
    late = [slice(h * QBLK + hk, (h + 1) * QBLK) for h in range(N_HEADS)]

    def q_aug(kt):
        picks = jnp.concatenate([sel_bias[kt * blocks_per_tile:(kt + 1) * blocks_per_tile],
                                 jnp.zeros((SEL_ROWS - blocks_per_tile, QBLK), F32)], axis=0)
        return jnp.concatenate([qst_bf, lanes4(picks).astype(BF16), pad_rows], axis=0)

    def sel_update(vt, s, m_old, acc):
        m_new = jnp.maximum(m_old, jnp.max(s, axis=0, keepdims=True))
        p = jnp.exp2(s - m_new)
        return m_new, jnp.exp2(m_old - m_new) * acc + jnp.dot(vt, p.astype(BF16), preferred_element_type=F32)

    m_s = jnp.full((1, cols4), NEG, F32)
    acc_s = jnp.zeros((V_AUG, cols4), F32)
    for kt in range(n):
        s = jnp.dot(ks_ref[kt * KT:(kt + 1) * KT, :], q_aug(kt), preferred_element_type=F32)
        m_s, acc_s = sel_update(vts_ref[kt], s, m_s, acc_s)
    qa = q_aug(n)
    bias_a = lanes4(jnp.concatenate([tri, jnp.zeros((hk, hk), F32)], axis=1))
    s_a = jnp.dot(ks_ref[n * KT:n * KT + hk, :], qa, preferred_element_type=F32) + bias_a
    m_a, acc_a = sel_update(vts_ref[n][:, 0:hk], s_a, m_s, acc_s)
    q_late = jnp.concatenate([qa[:, c] for c in late], axis=1)
    m_l = jnp.concatenate([m_a[:, c] for c in late], axis=1)
    acc_l = jnp.concatenate([acc_a[:, c] for c in late], axis=1)
    s_b = jnp.dot(ks_ref[n * KT + hk:(n + 1) * KT, :], q_late, preferred_element_type=F32) + lanes4(tri)
    _, acc_b = sel_update(vts_ref[n][:, hk:KT], s_b, m_l, acc_l)
    acc_s = jnp.concatenate([part for h in range(N_HEADS)
                             for part in (acc_a[:, h * QBLK:h * QBLK + hk], acc_b[:, h * hk:(h + 1) * hk])], axis=1)
    o_slc = acc_s[0:HEAD_DIM] / acc_s[HEAD_DIM:HEAD_DIM + 1]

    gate = _sigmoid(g_ref[0, rows, :]).T
    outs = []
    for h in range(N_HEADS):
        cs = slice(h * QBLK, (h + 1) * QBLK)
        g0 = D_GATE_LANE + 3 * h
        outs.append(gate[g0:g0 + 1, :] * o_cmp[:, cs] + gate[g0 + 1:g0 + 2, :] * o_slc[:, cs]
                    + gate[g0 + 2:g0 + 3, :] * o_win[:, cs])
    o_ref[0, rows, :] = (jnp.concatenate(outs, axis=0).T * _silu(z_ref[0, rows, :])).astype(Y_DTYPE)


def _mixer_d(proj, qtabs, ktabs, ctabs, w1bd, pe2, w2bd, overlap):
    QBLK = D_QBLK
    bsz, seq, _ = proj.shape
    blk = lambda name: _new_offset(name) // GROUP_W
    full = lambda shape: pl.BlockSpec(shape, lambda b, i: (0,) * len(shape))
    return pl.pallas_call(
        _mixer_d_body,
        grid=(bsz, 1),
        in_specs=[pl.BlockSpec((1, seq, GROUP_W), lambda b, i: (b, 0, blk('d_q'))),
                  pl.BlockSpec((1, seq, GROUP_W), lambda b, i: (b, 0, blk('d_z'))),
                  pl.BlockSpec((1, seq, 128), lambda b, i: (b, 0, _new_offset('d_ksv') // 128)),
                  pl.BlockSpec((1, seq, 128), lambda b, i: (b, 0, _new_offset('d_kwv') // 128)),
                  pl.BlockSpec((1, seq, 128), lambda b, i: (b, 0, _new_offset('d_kvc') // 128)),
                  pl.BlockSpec((1, seq, LANES), lambda b, i: (b, 0, _new_offset('gates') // LANES)),
                  full((seq // QBLK, HALF, QBLK)), full((seq // QBLK, HALF, QBLK)),
                  full((seq, 128)), full((seq, 128)), full((seq, 128)), full((LANES, LANES)),
                  full((N_CMP_PAD, 128)), full((N_CMP_PAD, 128)), full((N_CMP_PAD, 128)),
                  full((CMP_LEN // 4, 2 * LANES, 2 * LANES)), full((CMP_LEN // 4, PE_ROWS, 2 * LANES)), full((128, 128)),
                  full((seq // SLC_LEN, N_CMP_PAD))],
        out_specs=pl.BlockSpec((1, seq, GROUP_W), lambda b, i: (b, 0, 0)),
        out_shape=jax.ShapeDtypeStruct((bsz, seq, GROUP_W), Y_DTYPE),
        scratch_shapes=[pltpu.VMEM((seq, 2 * HEAD_DIM), BF16),
                        pltpu.VMEM((seq, HEAD_DIM), BF16),
                        pltpu.VMEM((seq // KT, V_AUG, KT), BF16),
                        pltpu.VMEM((seq // WBLK, V_AUG, WBLK), BF16),
                        pltpu.VMEM((N_CMP_PAD, HEAD_DIM), F32),
                        pltpu.VMEM((HEAD_DIM, N_CMP_PAD), BF16)],
        compiler_params=_cparams(("arbitrary", "arbitrary")),
        name="mixer_d",
    )(proj, proj, proj, proj, proj, proj, *qtabs, *ktabs, _rope_perm(LANES, HEAD_DIM), *ctabs, w1bd, pe2, w2bd,
      overlap)


def _cmp_params(pe, w1, w2):
    w1 = w1.reshape(2, CMP_LEN, HEAD_DIM, HEAD_DIM)
    z = jnp.zeros((CMP_LEN, HEAD_DIM, HEAD_DIM), F32)
    w1bd = jnp.concatenate([jnp.concatenate([w1[0], z], axis=2), jnp.concatenate([z, w1[1]], axis=2)], axis=1)
    z2 = jnp.zeros((HEAD_DIM, HEAD_DIM), F32)
    w2bd = jnp.concatenate([jnp.concatenate([w2[0], z2], axis=1), jnp.concatenate([z2, w2[1]], axis=1)], axis=0)
    pe2 = jnp.concatenate([pe[0], pe[1]], axis=1)
    half = CMP_LEN // 2
    w1pair = jnp.stack([jnp.concatenate([jnp.concatenate([w1bd[r], w1bd[half + r]], axis=1),
                                         jnp.concatenate([w1bd[r + 1], w1bd[half + r + 1]], axis=1)], axis=0)
                        for r in range(0, half, 2)])
    pe_rows = jnp.stack([jnp.concatenate([jnp.concatenate([pe2[r], pe2[r + 1]])[None, :],
                                          jnp.concatenate([pe2[half + r], pe2[half + r + 1]])[None, :],
                                          jnp.zeros((PE_ROWS - 2, 2 * LANES), F32)], axis=0)
                         for r in range(0, half, 2)])
    return w1pair, pe_rows, w2bd


def _overlap_matrix(seq):
    n_cmp = (seq - CMP_LEN) // CMP_STRIDE + 1
    starts = np.arange(N_CMP_PAD) * CMP_STRIDE
    blk = np.arange(seq // SLC_LEN)
    ov = ((starts[None, :] < (blk[:, None] + 1) * SLC_LEN) & (starts[None, :] + CMP_LEN > blk[:, None] * SLC_LEN))
    ov = ov & (np.arange(N_CMP_PAD)[None, :] < n_cmp)
    return jnp.asarray(ov.astype(np.float32))


def kernel(x, ln0_g, ln0_b, w_in, b_in, a_sinks, b_conv_w, b_conv_b, c_lb, c_norm_g, d_cmp_pe, d_cmp_w1, d_cmp_w2,
           w_out, ln_g, ln_b):
    bsz, seq, d = x.shape
    pos = jnp.arange(seq)
    qtabs = _rope_tables(pos, GROUP_W, 128)
    ktabs = _rope_tables(pos, 128, HEAD_DIM)
    ctabs = _rope_tables(jnp.arange(N_CMP_PAD) * CMP_STRIDE + (CMP_LEN - 1), 128, HEAD_DIM)
    overlap = _overlap_matrix(seq)

    w_p, b_p = _prep_in_weights(w_in, b_in, 0)
    h, proj = _ln_in_proj(x.reshape(bsz * seq, d), ln0_g, ln0_b, w_p, b_p)
    for l in range(DEPTH):
        proj = proj.reshape(bsz, seq, N_PROJ)
        y_a = _mixer_a(proj, a_sinks[l], qtabs)
        y_b, y_c = _mixer_bc(proj, b_conv_w[l], b_conv_b[l], c_lb, c_norm_g[l], l)
        w1bd, pe2, w2bd = _cmp_params(d_cmp_pe[l], d_cmp_w1[l], d_cmp_w2[l])
        y_d = _mixer_d(proj, _rope_tables_t(seq, D_QBLK), ktabs, ctabs, w1bd, pe2, w2bd, overlap)
        ys = (y_a, y_b, y_c, y_d)
        if l + 1 < DEPTH:
            w_p, b_p = _prep_in_weights(w_in, b_in, l + 1)
            h, proj = _out_in_proj(ys, h, w_out[l].astype(BF16), ln_g[l], ln_b[l], w_p, b_p)
        else:
            h = _out_proj(ys, h, w_out[l].astype(BF16), ln_g[l], ln_b[l])
    return h.reshape(bsz, seq, d)
```

```python
import functools

import numpy as np
import jax
import jax.numpy as jnp
from jax import lax
from jax.experimental import pallas as pl
from jax.experimental.pallas import tpu as pltpu

F32 = jnp.float32
BF16 = jnp.bfloat16
Y_DTYPE = BF16

DEPTH = 2
HEAD_DIM = 64
HALF = HEAD_DIM // 2
N_HEADS = 4
GROUP_W = N_HEADS * HEAD_DIM
N_GROUPS = 4
ROPE_THETA = 10000.0
NEG = -1e30
LN_EPS = 1e-5
A_WINDOW = 128
A_GROUP = 2
A_SUB = 16
V_AUG = HEAD_DIM + 16
LOG2E = 1.4426950408889634
PRO_ROWS = 2048
A_PRO_ROWS = 2048
B_CONV = 4
C_CHUNK = 64
C_SUB = 8
BC_STEP = 1024
C_EPS = 1e-6
CMP_LEN = 32
CMP_STRIDE = 16
SLC_LEN = 64
N_SEL = 8
NSA_WINDOW = 512
FORCE_SCORE = 1e6
DN_ALPHA = (2.0 * DEPTH) ** 0.25
QBLK = 128
SCALE = HEAD_DIM ** -0.5

VMEM_LIMIT = 56 * 1024 * 1024
PROJ_TM = 512
PROJ_SUB = 256

LANES = 128

ORIG_SPLITS = (
    ('a_q', 256), ('a_k', 128), ('a_v', 128), ('a_z', 256),
    ('b_q', 256), ('b_k', 256), ('b_v', 256), ('b_if', 8), ('b_o', 256), ('b_z', 256),
    ('c_q', 256), ('c_f', 256), ('c_i', 256), ('c_z', 256),
    ('d_q', 256), ('d_kvc', 128), ('d_ksv', 128), ('d_kwv', 128), ('d_g', 12), ('d_z', 256),
)
N_COLS = sum(w for _, w in ORIG_SPLITS)
GATE_SLOTS = ('b_if', 'd_g')
NEW_LAYOUT = (
    ('a_q', 256), ('a_k', 128), ('a_v', 128), ('a_z', 256), ('b_q', 256), ('b_k', 256), ('b_v', 256),
    ('b_o', 256), ('b_z', 256), ('c_q', 256), ('c_f', 256), ('c_i', 256), ('c_z', 256),
    ('d_q', 256), ('d_kvc', 128), ('d_ksv', 128), ('d_kwv', 128), ('gates', 128), ('d_z', 256),
)
N_PROJ = sum(w for _, w in NEW_LAYOUT)
D_GATE_LANE = dict(ORIG_SPLITS)['b_if']


def _offset(layout, name):
    off = 0
    for n, w in layout:
        if n == name:
            return off
        off += w
    raise KeyError(name)


def _new_offset(name):
    return _offset(NEW_LAYOUT, name)


def _permuted_cols(load):
    sizes = dict(ORIG_SPLITS)
    for name, width in NEW_LAYOUT:
        new = _new_offset(name)
        if name != 'gates':
            old = _offset(ORIG_SPLITS, name)
            lo = old // LANES * LANES
            hi = min(-(-(old + width) // LANES) * LANES, N_COLS)
            yield new, load(lo, hi)[..., old - lo:old - lo + width]
            continue
        gates, lane0 = None, 0
        for slot in GATE_SLOTS:
            old = _offset(ORIG_SPLITS, slot)
            lo = old // LANES * LANES
            tile = load(lo, lo + LANES)
            if old - lo != lane0:
                tile = pltpu.roll(tile, (lane0 - (old - lo)) % LANES, axis=tile.ndim - 1)
            lane = _iota(tile.shape, tile.ndim - 1)
            part = jnp.where((lane >= lane0) & (lane < lane0 + sizes[slot]), tile, 0.0)
            gates = part if gates is None else gates + part
            lane0 += sizes[slot]
        yield new, gates


def _prep_body(w_ref, b_ref, wo_ref, bo_ref):
    for new, cols in _permuted_cols(lambda lo, hi: w_ref[0, :, lo:hi]):
        wo_ref[:, new:new + cols.shape[-1]] = cols.astype(BF16)
    for new, cols in _permuted_cols(lambda lo, hi: b_ref[0, :, lo:hi]):
        bo_ref[:, new:new + cols.shape[-1]] = cols


def _prep_in_weights(w_in, b_in, layer, rows=128):
    depth, d, n = w_in.shape
    return pl.pallas_call(
        _prep_body,
        grid=(d // rows,),
        in_specs=[pl.BlockSpec((1, rows, n), lambda r: (layer, r, 0)),
                  pl.BlockSpec((1, 1, n), lambda r: (layer, 0, 0))],
        out_specs=[pl.BlockSpec((rows, N_PROJ), lambda r: (r, 0)),
                   pl.BlockSpec((1, N_PROJ), lambda r: (0, 0))],
        out_shape=[jax.ShapeDtypeStruct((d, N_PROJ), BF16), jax.ShapeDtypeStruct((1, N_PROJ), F32)],
        compiler_params=_cparams(("arbitrary",)),
        name="prep_in_weights",
    )(w_in, b_in.reshape(depth, 1, n))


def _mm(a, b):
    return jnp.dot(a.astype(BF16), b.astype(BF16), preferred_element_type=F32)


def _mm_nt(a, b):
    return lax.dot_general(a.astype(BF16), b.astype(BF16), (((1,), (1,)), ((), ())),
                           preferred_element_type=F32)


def _mm_tn(a, b):
    return lax.dot_general(a.astype(BF16), b.astype(BF16), (((0,), (0,)), ((), ())),
                           preferred_element_type=F32)


def _dot_bf16(a, b):
    return jnp.dot(a, b, preferred_element_type=F32)


def _split_bf16(x, terms):
    out = []
    for _ in range(terms - 1):
        t = x.astype(BF16)
        out.append(t)
        x = x - t.astype(F32)
    out.append(x.astype(BF16))
    return out


def _mm_hi(a, b):
    ah, al = _split_bf16(a, 2)
    bh, bl = _split_bf16(b, 2)
    return _dot_bf16(ah, bh) + (_dot_bf16(ah, bl) + _dot_bf16(al, bh))


def _mm_sel(sel, x):
    sel = sel.astype(BF16)
    x1, x2, x3 = _split_bf16(x, 3)
    return _dot_bf16(sel, x1) + (_dot_bf16(sel, x2) + _dot_bf16(sel, x3))


def _sigmoid(x):
    return 1.0 / (1.0 + jnp.exp2(x * (-LOG2E)))


def _silu(x):
    return x * _sigmoid(x)


def _log_sigmoid(x):
    return jnp.minimum(x, 0.0) - jnp.log1p(jnp.exp(-jnp.abs(x)))


def _rope(x, cos, sin_up, sin_dn):
    w = x.shape[-1]
    up = pltpu.roll(x, w - HALF, axis=1)
    dn = pltpu.roll(x, HALF, axis=1)
    return x * cos + up * sin_up + dn * sin_dn


def _rope_bf16(x, cos, sin_up, sin_dn, perm):
    swapped = jnp.dot(x.astype(BF16), perm, preferred_element_type=F32)
    return x * cos + swapped * (sin_up + sin_dn)


def _rope_perm(width, rot_lanes):
    lane = np.arange(width)
    src = np.where(lane % HEAD_DIM < HALF, lane + HALF, lane - HALF)
    rotated = (lane % LANES) < rot_lanes
    perm = (np.arange(width)[:, None] == src[None, :]) & rotated[None, :]
    return jnp.asarray(perm.astype(np.float32), dtype=BF16)


def _rope_tables_t(seq, blk):
    inv = ROPE_THETA ** (-jnp.arange(HALF, dtype=F32) / HALF)
    ang = jnp.arange(seq).astype(F32)[:, None] * inv[None, :]
    to_blocks = lambda a: a.T.reshape(HALF, seq // blk, blk).transpose(1, 0, 2)
    return to_blocks(jnp.cos(ang)), to_blocks(jnp.sin(ang))


def _rope_tables(pos, width, rot_lanes):
    inv = ROPE_THETA ** (-jnp.arange(HALF, dtype=F32) / HALF)
    ang = pos.astype(F32)[:, None] * inv[None, :]
    cos, sin = jnp.cos(ang), jnp.sin(ang)
    zero, one = jnp.zeros_like(sin), jnp.ones_like(cos)
    cos_h = jnp.concatenate([cos, cos], axis=1)
    up_h = jnp.concatenate([-sin, zero], axis=1)
    dn_h = jnp.concatenate([zero, sin], axis=1)
    id_c = jnp.concatenate([one, one], axis=1)
    id_s = jnp.concatenate([zero, zero], axis=1)
    cs, us, ds = [], [], []
    for l0 in range(0, width, HEAD_DIM):
        rot = (l0 % 128) < rot_lanes
        cs.append(cos_h if rot else id_c)
        us.append(up_h if rot else id_s)
        ds.append(dn_h if rot else id_s)
    return jnp.concatenate(cs, axis=1), jnp.concatenate(us, axis=1), jnp.concatenate(ds, axis=1)


def _iota(shape, dim):
    return lax.broadcasted_iota(jnp.int32, shape, dim)


def _cparams(sem):
    return pltpu.CompilerParams(dimension_semantics=sem, vmem_limit_bytes=VMEM_LIMIT)


def _ln_rows(x, g, b):
    mu = jnp.mean(x, axis=-1, keepdims=True)
    xc = x - mu
    var = jnp.mean(xc * xc, axis=-1, keepdims=True)
    return xc * lax.rsqrt(var + LN_EPS) * g + b


def _mix_out(rows, ya_ref, yb_ref, yc_ref, yd_ref, h_ref, w_ref, g_ref, b_ref):
    y = _mm(ya_ref[rows, :], w_ref[0:GROUP_W, :])
    y += _mm(yb_ref[rows, :], w_ref[GROUP_W:2 * GROUP_W, :])
    y += _mm(yc_ref[rows, :], w_ref[2 * GROUP_W:3 * GROUP_W, :])
    y += _mm(yd_ref[rows, :], w_ref[3 * GROUP_W:4 * GROUP_W, :])
    return _ln_rows(DN_ALPHA * h_ref[rows, :] + y, g_ref[...], b_ref[...])


def _sub_tiles(ref):
    return [slice(r, r + PROJ_SUB) for r in range(0, ref.shape[0], PROJ_SUB)]


def _ln_proj_body(x_ref, g_ref, b_ref, w_ref, bias_ref, h_ref, o_ref):
    for rows in _sub_tiles(x_ref):
        h = _ln_rows(x_ref[rows, :], g_ref[...], b_ref[...])
        h_ref[rows, :] = h
        o_ref[rows, :] = jnp.dot(h.astype(BF16), w_ref[...], preferred_element_type=F32) + bias_ref[...]


def _out_proj_body(ya_ref, yb_ref, yc_ref, yd_ref, hp_ref, wo_ref, g_ref, b_ref, w_ref, bias_ref, h_ref, o_ref):
    for rows in _sub_tiles(hp_ref):
        h = _mix_out(rows, ya_ref, yb_ref, yc_ref, yd_ref, hp_ref, wo_ref, g_ref, b_ref)
        h_ref[rows, :] = h
        o_ref[rows, :] = jnp.dot(h.astype(BF16), w_ref[...], preferred_element_type=F32) + bias_ref[...]


def _ln_in_proj(x2, g, b, w_bf, bias, tm=PROJ_TM):
    m, d = x2.shape
    n = w_bf.shape[1]
    row = lambda width: pl.BlockSpec((tm, width), lambda i: (i, 0))
    const = lambda shape: pl.BlockSpec(shape, lambda i: (0, 0))
    return pl.pallas_call(
        _ln_proj_body,
        grid=(m // tm,),
        in_specs=[row(d), const((1, d)), const((1, d)), const((d, n)), const((1, n))],
        out_specs=[row(d), row(n)],
        out_shape=[jax.ShapeDtypeStruct((m, d), F32), jax.ShapeDtypeStruct((m, n), F32)],
        compiler_params=_cparams(("parallel",)),
        name="ln_in_proj",
    )(x2, g.reshape(1, d), b.reshape(1, d), w_bf, bias.reshape(1, n))


def _out_in_proj(ys, h2, wo_bf, g, b, w_bf, bias, tm=PROJ_TM):
    m, d = h2.shape
    n = w_bf.shape[1]
    row = lambda width: pl.BlockSpec((tm, width), lambda i: (i, 0))
    const = lambda shape: pl.BlockSpec(shape, lambda i: (0, 0))
    return pl.pallas_call(
        _out_proj_body,
        grid=(m // tm,),
        in_specs=[row(GROUP_W)] * N_GROUPS + [row(d), const((N_GROUPS * GROUP_W, d)), const((1, d)), const((1, d)),
                                              const((d, n)), const((1, n))],
        out_specs=[row(d), row(n)],
        out_shape=[jax.ShapeDtypeStruct((m, d), F32), jax.ShapeDtypeStruct((m, n), F32)],
        compiler_params=_cparams(("parallel",)),
        name="out_in_proj",
    )(*[y.reshape(m, GROUP_W) for y in ys], h2, wo_bf, g.reshape(1, d), b.reshape(1, d), w_bf, bias.reshape(1, n))


def _out_body(ya_ref, yb_ref, yc_ref, yd_ref, h_ref, w_ref, g_ref, b_ref, o_ref):
    o_ref[...] = _mix_out(slice(None), ya_ref, yb_ref, yc_ref, yd_ref, h_ref, w_ref, g_ref, b_ref)


def _out_proj(ys, h2, w_bf, g, b, tm=512):
    m, d = h2.shape
    yspec = pl.BlockSpec((tm, GROUP_W), lambda i: (i, 0))
    return pl.pallas_call(
        _out_body,
        grid=(m // tm,),
        in_specs=[yspec, yspec, yspec, yspec,
                  pl.BlockSpec((tm, d), lambda i: (i, 0)),
                  pl.BlockSpec((4 * GROUP_W, d), lambda i: (0, 0)),
                  pl.BlockSpec((1, d), lambda i: (0, 0)),
                  pl.BlockSpec((1, d), lambda i: (0, 0))],
        out_specs=pl.BlockSpec((tm, d), lambda i: (i, 0)),
        out_shape=jax.ShapeDtypeStruct((m, d), F32),
        compiler_params=_cparams(("parallel",)),
        name="out_proj",
    )(*[y.reshape(m, GROUP_W) for y in ys], h2, w_bf, g.reshape(1, d), b.reshape(1, d))


def _mixer_a_body(sink_ref, q_ref, kv_ref, z_ref, cos_ref, up_ref, dn_ref, perm_ref, o_ref, k_ref, vt_ref):
    i = pl.program_id(1)
    seq = kv_ref.shape[1]
    n_kv = N_HEADS // A_GROUP
    low_half = _iota((1, LANES), 1) < HEAD_DIM

    @pl.when(i == 0)
    def _():
        def rope_rows(c, carry):
            for u in range(A_PRO_ROWS // QBLK):
                t0 = c * (A_PRO_ROWS // QBLK) + u
                rows = pl.ds(pl.multiple_of(t0 * QBLK, QBLK), QBLK)
                kr = _rope_bf16(kv_ref[0, rows, 0:LANES], cos_ref[rows, 0:LANES], up_ref[rows, 0:LANES],
                                dn_ref[rows, 0:LANES], perm_ref[0:LANES, 0:LANES])
                v_t = _mm_nt(_iota((LANES, LANES), 0) == _iota((LANES, LANES), 1), kv_ref[0, rows, LANES:2 * LANES])
                v_t = v_t.astype(BF16)
                ones_rows = jnp.ones((V_AUG - HEAD_DIM, QBLK), BF16)
                other = pltpu.roll(kr, HEAD_DIM, axis=1)
                for g in range(n_kv):
                    dup = jnp.where(low_half, kr, other) if g == 0 else jnp.where(low_half, other, kr)
                    k_ref[g, rows, :] = dup.astype(BF16)
                    vt_ref[g, t0] = jnp.concatenate([v_t[g * HEAD_DIM:(g + 1) * HEAD_DIM, :], ones_rows], axis=0)
            return carry
        lax.fori_loop(0, seq // A_PRO_ROWS, rope_rows, 0)

    first_head = _iota((1, A_GROUP * QBLK), 1) < QBLK
    eye = (_iota((QBLK, QBLK), 0) == _iota((QBLK, QBLK), 1)).astype(BF16)

    def query_block(u):
        sub = slice(u * QBLK, (u + 1) * QBLK)
        q0 = u * QBLK
        qrows = sub
        qr = _rope(q_ref[0, sub, :], cos_ref[qrows, :], up_ref[qrows, :], dn_ref[qrows, :]) * (SCALE * LOG2E)
        kt0 = max(u - A_WINDOW // QBLK, 0)
        n_blk = u - kt0 + 1
        span = n_blk * QBLK
        k0 = kt0 * QBLK
        rel = (q0 + _iota((1, QBLK), 1)) - (k0 + _iota((span, 1), 0))
        in_band = lax.bitcast_convert_type(rel, jnp.uint32) < A_WINDOW
        bias = jnp.concatenate([jnp.where(in_band, 0.0, NEG)] * A_GROUP, axis=1)
        outs = []
        for g in range(n_kv):
            q_pair = qr[:, g * LANES:(g + 1) * LANES]
            qg = jnp.concatenate([jnp.where(low_half, q_pair, 0.0), jnp.where(low_half, 0.0, q_pair)], axis=0)
            sink = jnp.where(first_head, sink_ref[g * A_GROUP], sink_ref[g * A_GROUP + 1]) * LOG2E
            s = _mm_nt(k_ref[g, k0:k0 + span, :], qg) + bias
            m = jnp.maximum(jnp.max(s, axis=0, keepdims=True), sink)
            p = jnp.exp2(s - m).astype(BF16)
            o = jnp.zeros((V_AUG, A_GROUP * QBLK), F32)
            for c in range(n_blk):
                o = o + jnp.dot(vt_ref[g, kt0 + c], p[c * QBLK:(c + 1) * QBLK], preferred_element_type=F32)
            o = o[0:HEAD_DIM] / (o[HEAD_DIM:HEAD_DIM + 1] + jnp.exp2(sink - m))
            outs.extend(o[:, r * QBLK:(r + 1) * QBLK] for r in range(A_GROUP))
        y = _mm_nt(eye, jnp.concatenate(outs, axis=0))
        o_ref[0, sub, :] = (y * _silu(z_ref[0, sub, :])).astype(Y_DTYPE)

    for u in range(A_SUB):
        query_block(u)


def _mixer_a(proj, sinks, tabs):
    assert A_GROUP == 2 and A_GROUP * HEAD_DIM == LANES, "a kv head's two query heads share one lane tile"
    bsz, seq, _ = proj.shape
    assert seq == A_SUB * QBLK, "one grid step per sequence: query block positions are static"
    cos, up, dn = tabs
    blk = lambda name: _new_offset(name) // GROUP_W
    tspec = pl.BlockSpec((seq, GROUP_W), lambda b, i: (0, 0))
    return pl.pallas_call(
        _mixer_a_body,
        grid=(bsz, seq // (A_SUB * QBLK)),
        in_specs=[pl.BlockSpec(memory_space=pltpu.SMEM),
                  pl.BlockSpec((1, A_SUB * QBLK, GROUP_W), lambda b, i: (b, i, blk('a_q'))),
                  pl.BlockSpec((1, seq, GROUP_W), lambda b, i: (b, 0, blk('a_k'))),
                  pl.BlockSpec((1, A_SUB * QBLK, GROUP_W), lambda b, i: (b, i, blk('a_z'))),
                  tspec, tspec, tspec, pl.BlockSpec((GROUP_W, GROUP_W), lambda b, i: (0, 0))],
        out_specs=pl.BlockSpec((1, A_SUB * QBLK, GROUP_W), lambda b, i: (b, i, 0)),
        out_shape=jax.ShapeDtypeStruct((bsz, seq, GROUP_W), Y_DTYPE),
        scratch_shapes=[pltpu.VMEM((N_HEADS // A_GROUP, seq, LANES), BF16),
                        pltpu.VMEM((N_HEADS // A_GROUP, seq // QBLK, V_AUG, QBLK), BF16)],
        compiler_params=_cparams(("arbitrary", "arbitrary")),
        name="mixer_a",
    )(sinks, proj, proj, proj, cos, up, dn, _rope_perm(GROUP_W, LANES))


B_CHUNK = 128
B_AUG = V_AUG


def _conv_silu(x_ref, r0, prev, w, b):
    L = B_CHUNK
    x = x_ref[0, r0:r0 + L, :]
    acc = b + w[B_CONV - 1:B_CONV, :] * x
    for s in range(1, B_CONV):
        if r0 >= s:
            shifted = x_ref[0, r0 - s:r0 - s + L, :]
        else:
            row8 = _iota((8, x.shape[1]), 0)
            xs = pltpu.roll(x, s, axis=0)
            ps = pltpu.roll(prev, s, axis=0)
            shifted = jnp.concatenate([jnp.where(row8 < s, ps, xs[0:8]), xs[8:]], axis=0)
        acc = acc + w[B_CONV - 1 - s:B_CONV - s, :] * shifted
    return _silu(acc)


def _mixer_b_init(xprev_ref, c_ref, m_ref):
    xprev_ref[...] = jnp.zeros_like(xprev_ref)
    c_ref[...] = jnp.zeros_like(c_ref)
    m_ref[...] = jnp.zeros_like(m_ref)


def _mixer_b_main(xq_ref, xk_ref, v_ref, og_ref, z_ref, if_ref, cw_ref, cb_ref, o_ref, xprev_ref, c_ref, m_ref):
    L = B_CHUNK
    tri = (_iota((L, L), 1) <= _iota((L, L), 0)).astype(F32)
    key_first = _iota((L, L), 0) <= _iota((L, L), 1)
    half_of_lane = _iota((1, 128), 1) // HEAD_DIM
    ones_rows = jnp.ones((B_AUG - HEAD_DIM, L), F32)

    def chunk(r0, state):
        rows = slice(r0, r0 + L)
        qc = _conv_silu(xq_ref, r0, xprev_ref[:, 0:GROUP_W], cw_ref[:, 0:GROUP_W], cb_ref[:, 0:GROUP_W]) * SCALE
        kc = _conv_silu(xk_ref, r0, xprev_ref[:, GROUP_W:], cw_ref[:, GROUP_W:], cb_ref[:, GROUP_W:])
        q_t = qc.T
        v_t = v_ref[0, rows, :].T

        gates = if_ref[0, rows, :]
        bcum = _mm_sel(tri, _log_sigmoid(gates))
        gates_t = gates.T
        bcum_t = bcum.T
        outs, new_state = [], []
        for h in range(N_HEADS):
            pair = slice((h // 2) * 128, (h // 2 + 1) * 128)
            own = half_of_lane == (h % 2)
            k_pair = kc[:, pair]
            q_pair_t = q_t[pair, :]
            i_row = gates_t[h:h + 1, :]
            b_row = bcum_t[N_HEADS + h:N_HEADS + h + 1, :]
            c_col = gates[:, h:h + 1] - bcum[:, N_HEADS + h:N_HEADS + h + 1]
            cst, m_prev = state[h]
            dmat = jnp.where(key_first, b_row + c_col, NEG)
            inter = b_row + m_prev
            m_t = jnp.maximum(jnp.max(dmat, axis=0, keepdims=True), inter)
            smat = _mm(jnp.where(own, k_pair, 0.0), q_pair_t) * jnp.exp(dmat - m_t)
            vaug_t = jnp.concatenate([v_t[h * HEAD_DIM:(h + 1) * HEAD_DIM, :], ones_rows], axis=0)
            tot = _mm(vaug_t, smat) + jnp.exp(inter - m_t) * _mm(cst, q_pair_t)
            num = tot[0:HEAD_DIM, :]
            den = tot[HEAD_DIM:HEAD_DIM + 1, :]
            outs.append(num / jnp.maximum(jnp.abs(den), jnp.exp(-m_t)))
            b_last = b_row[:, L - 1:L]
            d_row = b_last - b_row + i_row
            m_new = jnp.maximum(b_last + m_prev, jnp.max(d_row, axis=1, keepdims=True))
            w_row = jnp.exp(d_row - m_new)
            decay = jnp.exp(b_last + m_prev - m_new)
            new_state.append((decay * cst + jnp.where(own, _mm(vaug_t * w_row, k_pair), 0.0), m_new))
        o_ref[0, rows, :] = (_sigmoid(og_ref[0, rows, :]) * jnp.concatenate(outs, axis=0).T
                             * _silu(z_ref[0, rows, :])).astype(Y_DTYPE)
        return new_state

    rows_step = xq_ref.shape[1]
    state = [(c_ref[h], m_ref[h:h + 1, 0:1]) for h in range(N_HEADS)]
    for u in range(rows_step // L):
        state = chunk(u * L, state)
        yield
    xprev_ref[:, 0:GROUP_W] = xq_ref[0, rows_step - 8:rows_step, :]
    xprev_ref[:, GROUP_W:] = xk_ref[0, rows_step - 8:rows_step, :]
    for h in range(N_HEADS):
        c_ref[h] = state[h][0]
        m_ref[h:h + 1, :] = jnp.broadcast_to(state[h][1], (1, 128))


def _mixer_c_main(layer, q_ref, f_ref, i_ref, z_ref, lb_ref, gn_ref, tri_ref, piv_ref, o_ref, st_ref):
    L = C_CHUNK
    nsub = L // C_SUB

    raw = lb_ref[...]
    ex = jnp.exp(raw - jnp.max(raw, axis=0, keepdims=True))
    if layer == 0:
        lb = jnp.zeros((1, GROUP_W), F32)
    else:
        lb = jnp.sum(ex[1:layer + 1], axis=0, keepdims=True) / jnp.sum(ex, axis=0, keepdims=True)
    tri = (_iota((L, L), 1) <= _iota((L, L), 0)).astype(F32)
    head_of_lane = _iota((1, GROUP_W), 1) // HEAD_DIM
    same_head = ((_iota((GROUP_W, GROUP_W), 0) // HEAD_DIM)
                 == (_iota((GROUP_W, GROUP_W), 1) // HEAD_DIM))
    ones_bd = same_head.astype(BF16)

    def chunk(rows, st):
        f = lb + (1.0 - lb) * _sigmoid(f_ref[0, rows, :])
        logf = jnp.log(f)
        k = 1.0 - f
        q = _silu(q_ref[0, rows, :])
        v = i_ref[0, rows, :]
        a = _mm_sel(tri, logf) * LOG2E
        o = _mm_nt(q * jnp.exp2(a), st)

        qsts, kts, vreps = [], [], []
        for i in range(1, nsub):
            r0 = i * C_SUB
            piv = a[r0 - 1:r0, :]
            qt = q[r0:r0 + C_SUB] * jnp.exp2(a[r0:r0 + C_SUB] - piv)
            qsts.extend(jnp.where(head_of_lane == h, qt, 0.0) for h in range(N_HEADS))
            kts.append((k[0:r0] * jnp.exp2(piv - a[0:r0])).astype(BF16))
            vreps.append(v[0:r0].astype(BF16))
        scores = _mm_nt(jnp.concatenate(qsts, axis=0), jnp.concatenate(kts, axis=0))
        r = _dot_bf16((scores * piv_ref[...]).astype(BF16), jnp.concatenate(vreps, axis=0))
        pieces = [o[0:C_SUB]]
        for i in range(1, nsub):
            acc = o[i * C_SUB:(i + 1) * C_SUB]
            for h in range(N_HEADS):
                g0 = ((i - 1) * N_HEADS + h) * C_SUB
                acc = acc + jnp.where(head_of_lane == h, r[g0:g0 + C_SUB], 0.0)
            pieces.append(acc)
        o = jnp.concatenate(pieces, axis=0)

        a3 = a.reshape(nsub, C_SUB, GROUP_W)
        q3 = q.reshape(nsub, C_SUB, GROUP_W)
        k3 = k.reshape(nsub, C_SUB, GROUP_W)
        v3 = v.reshape(nsub, C_SUB, GROUP_W)
        ps = []
        for s in range(C_SUB):
            e = jnp.exp2(a3 - a3[:, s:s + 1, :] + tri_ref[s])
            ps.append((q3 * k3[:, s:s + 1, :] * e).reshape(L, GROUP_W).astype(BF16))
        w = jnp.dot(jnp.concatenate(ps, axis=0), ones_bd, preferred_element_type=F32)
        for s in range(C_SUB):
            vs = jnp.broadcast_to(v3[:, s:s + 1, :], (nsub, C_SUB, GROUP_W)).reshape(L, GROUP_W)
            o = o + w[s * L:(s + 1) * L] * vs

        a_last = a[L - 1:L, :]
        kdec = k * jnp.exp2(a_last - a)
        st_new = st * jnp.exp2(a_last) + jnp.where(same_head, _mm_tn(v, kdec), 0.0)

        ms = jnp.dot((o * o).astype(BF16), ones_bd, preferred_element_type=F32) * (1.0 / HEAD_DIM)
        o = o * lax.rsqrt(ms + C_EPS) * gn_ref[...]
        o_ref[0, rows, :] = (o * _silu(z_ref[0, rows, :])).astype(Y_DTYPE)
        return st_new

    st = st_ref[...]
    for u in range(q_ref.shape[1] // L):
        st = chunk(slice(u * L, (u + 1) * L), st)
        yield
    st_ref[...] = st


N_B_IN, N_C_IN, N_B_SCRATCH = 8, 8, 3
C_PER_B = B_CHUNK // C_CHUNK


def _mixer_bc_body(layer, *refs):
    b_in = refs[:N_B_IN]
    c_in = refs[N_B_IN:N_B_IN + N_C_IN]
    ob_ref, oc_ref = refs[N_B_IN + N_C_IN:N_B_IN + N_C_IN + 2]
    scratch = refs[N_B_IN + N_C_IN + 2:]
    b_scratch, c_scratch = scratch[:N_B_SCRATCH], scratch[N_B_SCRATCH:]

    @pl.when(pl.program_id(1) == 0)
    def _():
        _mixer_b_init(*b_scratch)
        c_scratch[0][...] = jnp.zeros_like(c_scratch[0])

    pending = [_mixer_b_main(*b_in, ob_ref, *b_scratch), _mixer_c_main(layer, *c_in, oc_ref, *c_scratch)]
    weights = [1, C_PER_B]
    while pending:
        for gen, reps in list(zip(pending, weights)):
            for _ in range(reps):
                if next(gen, StopIteration) is StopIteration:
                    idx = pending.index(gen)
                    del pending[idx], weights[idx]
                    break


def _mixer_bc(proj, conv_w, conv_b, c_lb, norm_g, layer):
    bsz, seq, _ = proj.shape
    L = BC_STEP
    blk = lambda name: _new_offset(name) // GROUP_W
    spec = lambda name: pl.BlockSpec((1, L, GROUP_W), lambda b, c: (b, c, blk(name)))
    vec = pl.BlockSpec((1, GROUP_W), lambda b, c: (0, 0))
    t_ge_s = np.arange(C_SUB)[None, :, None] >= np.arange(C_SUB)[:, None, None]
    causal_bias = np.broadcast_to(np.where(t_ge_s, 0.0, NEG), (C_SUB, C_SUB, GROUP_W)).astype(np.float32)
    pivots = np.arange(1, C_CHUNK // C_SUB)
    row_piv = np.repeat(pivots, N_HEADS * C_SUB)
    col_piv = np.repeat(pivots, pivots * C_SUB)
    same_pivot = (row_piv[:, None] == col_piv[None, :]).astype(np.float32)
    out_spec = pl.BlockSpec((1, L, GROUP_W), lambda b, c: (b, c, 0))
    out_shape = jax.ShapeDtypeStruct((bsz, seq, GROUP_W), Y_DTYPE)
    return pl.pallas_call(
        functools.partial(_mixer_bc_body, layer),
        grid=(bsz, seq // L),
        in_specs=[spec('b_q'), spec('b_k'), spec('b_v'), spec('b_o'), spec('b_z'),
                  pl.BlockSpec((1, L, LANES), lambda b, c: (b, c, _new_offset('gates') // LANES)),
                  pl.BlockSpec((B_CONV, 2 * GROUP_W), lambda b, c: (0, 0)),
                  pl.BlockSpec((1, 2 * GROUP_W), lambda b, c: (0, 0)),
                  spec('c_q'), spec('c_f'), spec('c_i'), spec('c_z'),
                  pl.BlockSpec((DEPTH, GROUP_W), lambda b, c: (0, 0)), vec,
                  pl.BlockSpec((C_SUB, C_SUB, GROUP_W), lambda b, c: (0, 0, 0)),
                  pl.BlockSpec(same_pivot.shape, lambda b, c: (0, 0))],
        out_specs=[out_spec, out_spec],
        out_shape=[out_shape, out_shape],
        scratch_shapes=[pltpu.VMEM((8, 2 * GROUP_W), F32),
                        pltpu.VMEM((N_HEADS, B_AUG, 128), F32),
                        pltpu.VMEM((8, 128), F32),
                        pltpu.VMEM((GROUP_W, GROUP_W), F32)],
        compiler_params=_cparams(("arbitrary", "arbitrary")),
        name="mixer_bc",
    )(proj, proj, proj, proj, proj, proj, conv_w, conv_b.reshape(1, -1),
      proj, proj, proj, proj, c_lb, norm_g.reshape(1, -1), jnp.asarray(causal_bias), jnp.asarray(same_pivot))


N_CMP_PAD = 128
KT = 512
D_QBLK = 512
WBLK = 256
PE_ROWS = 16
SEL_ROWS = 16


def _mixer_d_body(q_ref, z_ref, ksv_ref, kwv_ref, cin_ref, g_ref, qcos_ref, qsin_ref, kcos_ref, kup_ref,
                  kdn_ref, kperm_ref, ccos_ref, cup_ref, cdn_ref, w1_ref, pe_ref, w2_ref, ov_ref, o_ref,
                  ks_ref, kw_ref, vts_ref, vtw_ref, kc_ref, vct_ref):
    QBLK = D_QBLK
    i = pl.program_id(1)
    seq = ksv_ref.shape[1]
    half_cmp = CMP_LEN // 2

    @pl.when(i == 0)
    def _():
        def rope_rows(c, carry):
            for u in range(PRO_ROWS // KT):
                t0 = c * (PRO_ROWS // KT) + u
                rows = pl.ds(pl.multiple_of(t0 * KT, KT), KT)
                kvs_in = ksv_ref[0, rows, :]
                kvw_in = kwv_ref[0, rows, :]
                kvs = _rope_bf16(kvs_in, kcos_ref[rows, :], kup_ref[rows, :], kdn_ref[rows, :], kperm_ref[...])
                kvw = _rope_bf16(kvw_in, kcos_ref[rows, :], kup_ref[rows, :], kdn_ref[rows, :], kperm_ref[...])
                blk_in_tile = _iota((KT, HEAD_DIM), 0) // SLC_LEN
                onehot = jnp.where(blk_in_tile == _iota((KT, HEAD_DIM), 1), 1.0, 0.0)
                ks_ref[rows, :] = jnp.concatenate([kvs[:, 0:HEAD_DIM], onehot], axis=1).astype(BF16)
                kw_ref[rows, :] = kvw[:, 0:HEAD_DIM].astype(BF16)
                ones_rows = jnp.ones((V_AUG - HEAD_DIM, KT), BF16)
                pick_v = _iota((HEAD_DIM, LANES), 1) == _iota((HEAD_DIM, LANES), 0) + HEAD_DIM
                vts_ref[t0] = jnp.concatenate([_mm_nt(pick_v, kvs_in).astype(BF16), ones_rows], axis=0)
                vwt = jnp.concatenate([_mm_nt(pick_v, kvw_in).astype(BF16), ones_rows], axis=0)
                for w in range(KT // WBLK):
                    vtw_ref[t0 * (KT // WBLK) + w] = vwt[:, w * WBLK:(w + 1) * WBLK]
            return carry
        lax.fori_loop(0, seq // PRO_ROWS, rope_rows, 0)
        acc = jnp.zeros((N_CMP_PAD + PE_ROWS, 2 * LANES), F32)
        for rr in range(half_cmp // 2):
            xa = cin_ref[0, pl.ds(2 * rr, N_CMP_PAD, stride=CMP_STRIDE), :]
            xb = cin_ref[0, pl.ds(2 * rr + 1, N_CMP_PAD, stride=CMP_STRIDE), :]
            lhs = jnp.concatenate([jnp.concatenate([xa, xb], axis=1), pe_ref[rr]], axis=0)
            acc = acc + _mm_hi(lhs, w1_ref[rr])
        u0 = acc[0:N_CMP_PAD, 0:LANES] + acc[N_CMP_PAD:N_CMP_PAD + 1, 0:LANES]
        u1 = acc[0:N_CMP_PAD, LANES:] + acc[N_CMP_PAD + 1:N_CMP_PAD + 2, LANES:]
        pre = u0 + pltpu.roll(u1, N_CMP_PAD - 1, axis=0)
        cv = _rope(_mm_hi(_silu(pre), w2_ref[...]), ccos_ref[...], cup_ref[...], cdn_ref[...])
        kc_ref[...] = cv[:, 0:HEAD_DIM]
        vct_ref[...] = cv.T[HEAD_DIM:2 * HEAD_DIM, :].astype(BF16)

    for n in range(seq // QBLK):
        _mixer_d_step(n, q_ref, z_ref, g_ref, qcos_ref, qsin_ref, ov_ref, o_ref,
                      ks_ref, kw_ref, vts_ref, vtw_ref, kc_ref, vct_ref)


def _mixer_d_step(n, q_ref, z_ref, g_ref, qcos_ref, qsin_ref, ov_ref, o_ref,
                  ks_ref, kw_ref, vts_ref, vtw_ref, kc_ref, vct_ref):
    QBLK = D_QBLK
    seq = ks_ref.shape[0]
    q0 = n * QBLK
    rows = slice(q0, q0 + QBLK)
    qt = (q_ref[0, rows, :] * (SCALE * LOG2E)).T
    cos_t = qcos_ref[n]
    sin_t = qsin_ref[n]
    heads_t = []
    for h in range(N_HEADS):
        x1 = qt[h * HEAD_DIM:h * HEAD_DIM + HALF]
        x2 = qt[h * HEAD_DIM + HALF:(h + 1) * HEAD_DIM]
        heads_t.append(jnp.concatenate([x1 * cos_t - x2 * sin_t, x1 * sin_t + x2 * cos_t], axis=0))
    qst = jnp.concatenate(heads_t, axis=1)
    qst_bf = qst.astype(BF16)
    cols4 = N_HEADS * QBLK
    qpos = q0 + _iota((1, QBLK), 1)

    def lanes4(x):
        return jnp.concatenate([x] * N_HEADS, axis=1)

    n_cmp = min((q0 + QBLK) // CMP_STRIDE, N_CMP_PAD)
    sc = _mm_hi(kc_ref[0:n_cmp, :], qst)
    ends = _iota((n_cmp, 1), 0) * CMP_STRIDE + (CMP_LEN - 1)
    scm = sc + lanes4(jnp.where(ends <= qpos, 0.0, NEG))
    e = jnp.exp2(scm - jnp.max(scm, axis=0, keepdims=True))
    l = jnp.sum(e, axis=0, keepdims=True)
    any_valid = lanes4(jnp.where(qpos >= CMP_LEN - 1, 1.0, 0.0))
    pc = e * (any_valid / l)
    if n_cmp < N_CMP_PAD:
        pc = jnp.concatenate([pc, jnp.zeros((N_CMP_PAD - n_cmp, cols4), F32)], axis=0)
    o_cmp = jnp.dot(vct_ref[...], pc.astype(BF16), preferred_element_type=F32)
    psum = pc[:, 0:QBLK] + pc[:, QBLK:2 * QBLK] + pc[:, 2 * QBLK:3 * QBLK] + pc[:, 3 * QBLK:4 * QBLK]

    n_slc = (q0 + QBLK) // SLC_LEN
    imp = _mm_sel(ov_ref[0:n_slc, :], psum)
    jl = _iota((n_slc, 1), 0)
    cur = jnp.right_shift(qpos, SLC_LEN.bit_length() - 1)
    imp = jnp.where(jl == 0, FORCE_SCORE, imp)
    imp = jnp.where(jl == cur, FORCE_SCORE, imp)
    imp = jnp.where(jl == cur - 1, FORCE_SCORE, imp)
    imp = jnp.where(jl <= cur, imp, NEG)
    rank = jnp.zeros((n_slc, QBLK), F32)
    for jp in range(n_slc):
        cj = imp[jp:jp + 1, :]
        tie = jnp.where(jl > jp, 1.0, 0.0)
        rank = rank + jnp.where(cj > imp, 1.0, jnp.where(cj == imp, tie, 0.0))
    sel_bias = jnp.where(rank < N_SEL, 0.0, NEG)
    blocks_per_tile = KT // SLC_LEN

    win_parts = []
    for u in range(QBLK // WBLK):
        qcols = [slice(h * QBLK + u * WBLK, h * QBLK + (u + 1) * WBLK) for h in range(N_HEADS)]
        q_sub = jnp.concatenate([qst_bf[:, c] for c in qcols], axis=1)
        w = n * (QBLK // WBLK) + u
        kt0 = max(w - NSA_WINDOW // WBLK, 0)
        n_win = w - kt0 + 1
        span = n_win * WBLK
        k0 = kt0 * WBLK
        rel = (q0 + u * WBLK + _iota((1, WBLK), 1)) - (k0 + _iota((span, 1), 0))
        in_band = lax.bitcast_convert_type(rel, jnp.uint32) < NSA_WINDOW
        s = jnp.dot(kw_ref[k0:k0 + span, :], q_sub, preferred_element_type=F32)
        s = s + lanes4(jnp.where(in_band, 0.0, NEG))
        p = jnp.exp2(s - jnp.max(s, axis=0, keepdims=True)).astype(BF16)
        acc = jnp.zeros((V_AUG, N_HEADS * WBLK), F32)
        for c in range(n_win):
            acc = acc + jnp.dot(vtw_ref[kt0 + c], p[c * WBLK:(c + 1) * WBLK], preferred_element_type=F32)
        win_parts.append(acc[0:HEAD_DIM] / acc[HEAD_DIM:HEAD_DIM + 1])
    o_win = jnp.concatenate([part[:, h * WBLK:(h + 1) * WBLK] for h in range(N_HEADS) for part in win_parts], axis=1)

    pad_rows = jnp.zeros((HEAD_DIM - SEL_ROWS, cols4), BF16)
    hk = KT // 2
    tri = jnp.where(_iota((hk, hk), 0) <= _iota((hk, hk), 1), 0.0, NEG)
    late = [slice(h * QBLK + hk, (h + 1) * QBLK) for h in range(N_HEADS)]

    def q_aug(kt):
        picks = jnp.concatenate([sel_bias[kt * blocks_per_tile:(kt + 1) * blocks_per_tile],
                                 jnp.zeros((SEL_ROWS - blocks_per_tile, QBLK), F32)], axis=0)
        return jnp.concatenate([qst_bf, lanes4(picks).astype(BF16), pad_rows], axis=0)

    def sel_update(vt, s, m_old, acc):
        m_new = jnp.maximum(m_old, jnp.max(s, axis=0, keepdims=True))
        p = jnp.exp2(s - m_new)
        return m_new, jnp.exp2(m_old - m_new) * acc + jnp.dot(vt, p.astype(BF16), preferred_element_type=F32)

    m_s = jnp.full((1, cols4), NEG, F32)
    acc_s = jnp.zeros((V_AUG, cols4), F32)
    for kt in range(n):
        s = jnp.dot(ks_ref[kt * KT:(kt + 1) * KT, :], q_aug(kt), preferred_element_type=F32)
        m_s, acc_s = sel_update(vts_ref[kt], s, m_s, acc_s)
    qa = q_aug(n)
    bias_a = lanes4(jnp.concatenate([tri, jnp.zeros((hk, hk), F32)], axis=1))
    s_a = jnp.dot(ks_ref[n * KT:n * KT + hk, :], qa, preferred_element_type=F32) + bias_a
    m_a, acc_a = sel_update(vts_ref[n][:, 0:hk], s_a, m_s, acc_s)
    q_late = jnp.concatenate([qa[:, c] for c in late], axis=1)
    m_l = jnp.concatenate([m_a[:, c] for c in late], axis=1)
    acc_l = jnp.concatenate([acc_a[:, c] for c in late], axis=1)
    s_b = jnp.dot(ks_ref[n * KT + hk:(n + 1) * KT, :], q_late, preferred_element_type=F32) + lanes4(tri)
    _, acc_b = sel_update(vts_ref[n][:, hk:KT], s_b, m_l, acc_l)
    acc_s = jnp.concatenate([part for h in range(N_HEADS)
                             for part in (acc_a[:, h * QBLK:h * QBLK + hk], acc_b[:, h * hk:(h + 1) * hk])], axis=1)
    o_slc = acc_s[0:HEAD_DIM] / acc_s[HEAD_DIM:HEAD_DIM + 1]

    gate = _sigmoid(g_ref[0, rows, :]).T
    outs = []
    for h in range(N_HEADS):
        cs = slice(h * QBLK, (h + 1) * QBLK)
        g0 = D_GATE_LANE + 3 * h
        outs.append(gate[g0:g0 + 1, :] * o_cmp[:, cs] + gate[g0 + 1:g0 + 2, :] * o_slc[:, cs]
                    + gate[g0 + 2:g0 + 3, :] * o_win[:, cs])
    o_ref[0, rows, :] = (jnp.concatenate(outs, axis=0).T * _silu(z_ref[0, rows, :])).astype(Y_DTYPE)


def _mixer_d(proj, qtabs, ktabs, ctabs, w1bd, pe2, w2bd, overlap):
    QBLK = D_QBLK
    bsz, seq, _ = proj.shape
    blk = lambda name: _new_offset(name) // GROUP_W
    full = lambda shape: pl.BlockSpec(shape, lambda b, i: (0,) * len(shape))
    return pl.pallas_call(
        _mixer_d_body,
        grid=(bsz, 1),
        in_specs=[pl.BlockSpec((1, seq, GROUP_W), lambda b, i: (b, 0, blk('d_q'))),
                  pl.BlockSpec((1, seq, GROUP_W), lambda b, i: (b, 0, blk('d_z'))),
                  pl.BlockSpec((1, seq, 128), lambda b, i: (b, 0, _new_offset('d_ksv') // 128)),
                  pl.BlockSpec((1, seq, 128), lambda b, i: (b, 0, _new_offset('d_kwv') // 128)),
                  pl.BlockSpec((1, seq, 128), lambda b, i: (b, 0, _new_offset('d_kvc') // 128)),
                  pl.BlockSpec((1, seq, LANES), lambda b, i: (b, 0, _new_offset('gates') // LANES)),
                  full((seq // QBLK, HALF, QBLK)), full((seq // QBLK, HALF, QBLK)),
                  full((seq, 128)), full((seq, 128)), full((seq, 128)), full((LANES, LANES)),
                  full((N_CMP_PAD, 128)), full((N_CMP_PAD, 128)), full((N_CMP_PAD, 128)),
                  full((CMP_LEN // 4, 2 * LANES, 2 * LANES)), full((CMP_LEN // 4, PE_ROWS, 2 * LANES)), full((128, 128)),
                  full((seq // SLC_LEN, N_CMP_PAD))],
        out_specs=pl.BlockSpec((1, seq, GROUP_W), lambda b, i: (b, 0, 0)),
        out_shape=jax.ShapeDtypeStruct((bsz, seq, GROUP_W), Y_DTYPE),
        scratch_shapes=[pltpu.VMEM((seq, 2 * HEAD_DIM), BF16),
                        pltpu.VMEM((seq, HEAD_DIM), BF16),
                        pltpu.VMEM((seq // KT, V_AUG, KT), BF16),
                        pltpu.VMEM((seq // WBLK, V_AUG, WBLK), BF16),
                        pltpu.VMEM((N_CMP_PAD, HEAD_DIM), F32),
                        pltpu.VMEM((HEAD_DIM, N_CMP_PAD), BF16)],
        compiler_params=_cparams(("arbitrary", "arbitrary")),
        name="mixer_d",
    )(proj, proj, proj, proj, proj, proj, *qtabs, *ktabs, _rope_perm(LANES, HEAD_DIM), *ctabs, w1bd, pe2, w2bd,
      overlap)


def _cmp_params(pe, w1, w2):
    w1 = w1.reshape(2, CMP_LEN, HEAD_DIM, HEAD_DIM)
    z = jnp.zeros((CMP_LEN, HEAD_DIM, HEAD_DIM), F32)
    w1bd = jnp.concatenate([jnp.concatenate([w1[0], z], axis=2), jnp.concatenate([z, w1[1]], axis=2)], axis=1)
    z2 = jnp.zeros((HEAD_DIM, HEAD_DIM), F32)
    w2bd = jnp.concatenate([jnp.concatenate([w2[0], z2], axis=1), jnp.concatenate([z2, w2[1]], axis=1)], axis=0)
    pe2 = jnp.concatenate([pe[0], pe[1]], axis=1)
    half = CMP_LEN // 2
    w1pair = jnp.stack([jnp.concatenate([jnp.concatenate([w1bd[r], w1bd[half + r]], axis=1),
                                         jnp.concatenate([w1bd[r + 1], w1bd[half + r + 1]], axis=1)], axis=0)
                        for r in range(0, half, 2)])
    pe_rows = jnp.stack([jnp.concatenate([jnp.concatenate([pe2[r], pe2[r + 1]])[None, :],
                                          jnp.concatenate([pe2[half + r], pe2[half + r + 1]])[None, :],
                                          jnp.zeros((PE_ROWS - 2, 2 * LANES), F32)], axis=0)
                         for r in range(0, half, 2)])
    return w1pair, pe_rows, w2bd


def _overlap_matrix(seq):
    n_cmp = (seq - CMP_LEN) // CMP_STRIDE + 1
    starts = np.arange(N_CMP_PAD) * CMP_STRIDE
    blk = np.arange(seq // SLC_LEN)
    ov = ((starts[None, :] < (blk[:, None] + 1) * SLC_LEN) & (starts[None, :] + CMP_LEN > blk[:, None] * SLC_LEN))
    ov = ov & (np.arange(N_CMP_PAD)[None, :] < n_cmp)
    return jnp.asarray(ov.astype(np.float32))


def kernel(x, ln0_g, ln0_b, w_in, b_in, a_sinks, b_conv_w, b_conv_b, c_lb, c_norm_g, d_cmp_pe, d_cmp_w1, d_cmp_w2,
           w_out, ln_g, ln_b):
    bsz, seq, d = x.shape
    pos = jnp.arange(seq)
    qtabs = _rope_tables(pos, GROUP_W, 128)
    ktabs = _rope_tables(pos, 128, HEAD_DIM)
    ctabs = _rope_tables(jnp.arange(N_CMP_PAD) * CMP_STRIDE + (CMP_LEN - 1), 128, HEAD_DIM)
    overlap = _overlap_matrix(seq)

    w_p, b_p = _prep_in_weights(w_in, b_in, 0)
    h, proj = _ln_in_proj(x.reshape(bsz * seq, d), ln0_g, ln0_b, w_p, b_p)
    for l in range(DEPTH):
        proj = proj.reshape(bsz, seq, N_PROJ)
        y_a = _mixer_a(proj, a_sinks[l], qtabs)
        y_b, y_c = _mixer_bc(proj, b_conv_w[l], b_conv_b[l], c_lb, c_norm_g[l], l)
        w1bd, pe2, w2bd = _cmp_params(d_cmp_pe[l], d_cmp_w1[l], d_cmp_w2[l])
        y_d = _mixer_d(proj, _rope_tables_t(seq, D_QBLK), ktabs, ctabs, w1bd, pe2, w2bd, overlap)
        ys = (y_a, y_b, y_c, y_d)
        if l + 1 < DEPTH:
            w_p, b_p = _prep_in_weights(w_in, b_in, l + 1)
            h, proj = _out_in_proj(ys, h, w_out[l].astype(BF16), ln_g[l], ln_b[l], w_p, b_p)
        else:
            h = _out_proj(ys, h, w_out[l].astype(BF16), ln_g[l], ln_b[l])
    return h.reshape(bsz, seq, d)
```
